```python
import jax, jax.numpy as jnp
from jax import lax
import numpy as np

D_MODEL = 2048
BATCH = 8
SEQ = 4096
DEPTH = 2

N_MIXERS = 2
HG_HEADS = 16
HG_KEY_DIM = 128
HG_VAL_DIM = D_MODEL // HG_HEADS
HG_WIDTH = HG_HEADS * HG_KEY_DIM
HG_V_WIDTH = HG_HEADS * HG_VAL_DIM
HG_CHUNK = 64
SG_WIDTH = D_MODEL
SG_GROUPS = 16
SG_GROUP_DIM = SG_WIDTH // SG_GROUPS
SG_CHUNK = 128
D_FF = 5632
CONV_WIDTH = 3
ALPHA = (2 * DEPTH) ** 0.25
BETA = (8 * DEPTH) ** -0.25
LN_EPS = 1e-5
RMS_EPS = 1e-6
N_HG_LAYERS = (DEPTH + 1) // 2
N_SG_LAYERS = DEPTH // 2

kernel_name = 'hgrn2_gmlp_convffn_deepnorm_hybrid'


def layer_norm(x, g, b):
    xf = x.astype(jnp.float32)
    mu = jnp.mean(xf, axis=-1, keepdims=True)
    xc = xf - mu
    var = jnp.mean(xc * xc, axis=-1, keepdims=True)
    y = xc * lax.rsqrt(var + LN_EPS) * g.astype(jnp.float32) + b.astype(jnp.float32)
    return y.astype(x.dtype)


def hgrn2_mixer(x, w_in, norm_g, w_out, lb):
    b_, s_, _ = x.shape
    n = s_ // HG_CHUNK
    proj = x @ w_in
    q, f, i, g = jnp.split(proj, [HG_WIDTH, 2 * HG_WIDTH, 2 * HG_WIDTH + HG_V_WIDTH], axis=-1)
    f = f.astype(jnp.float32)
    q = jax.nn.silu(q.astype(jnp.float32))
    v = i.astype(jnp.float32)
    log_forget = jnp.logaddexp(jnp.log(lb), jnp.log1p(-lb) + jax.nn.log_sigmoid(f))
    k = (1.0 - lb) * jax.nn.sigmoid(-f)

    def to_chunks(t, d):
        return t.reshape(b_, n, HG_CHUNK, HG_HEADS, d).transpose(1, 0, 3, 2, 4)

    qc = to_chunks(q, HG_KEY_DIM)
    kc = to_chunks(k, HG_KEY_DIM)
    lfc = to_chunks(log_forget, HG_KEY_DIM)
    vc = to_chunks(v, HG_VAL_DIM)
    mask = jnp.tril(jnp.ones((HG_CHUNK, HG_CHUNK), dtype=bool))

    def step(state, inp):
        q_c, k_c, v_c, lf_c = inp
        cum = jnp.cumsum(lf_c, axis=2)
        rel = cum[:, :, :, None, :] - cum[:, :, None, :, :]
        decay = jnp.exp(jnp.where(mask[:, :, None], rel, -jnp.inf))
        scores = jnp.einsum('bhtk,bhsk,bhtsk->bhts', q_c, k_c, decay)
        out = (jnp.einsum('bhts,bhsv->bhtv', scores, v_c)
               + jnp.einsum('bhtk,bhkv->bhtv', q_c * jnp.exp(cum), state))
        last = cum[:, :, -1:, :]
        new_state = (jnp.exp(last[:, :, 0, :, None]) * state
                     + jnp.einsum('bhsk,bhsv->bhkv', k_c * jnp.exp(last - cum), v_c))
        return new_state, out

    state0 = jnp.zeros((b_, HG_HEADS, HG_KEY_DIM, HG_VAL_DIM), jnp.float32)
    _, o = lax.scan(step, state0, (qc, kc, vc, lfc))
    o = o.transpose(1, 0, 3, 2, 4).reshape(b_, s_, HG_HEADS, HG_VAL_DIM)
    o = o * lax.rsqrt(jnp.mean(o * o, axis=-1, keepdims=True) + RMS_EPS)
    o = o * norm_g.astype(jnp.float32).reshape(HG_HEADS, HG_VAL_DIM)
    o = o.reshape(b_, s_, HG_V_WIDTH) * jax.nn.silu(g.astype(jnp.float32))
    return o.astype(x.dtype) @ w_out


def chunked_gmlp(x, w_in, ln_g, ln_b, w_s, b_s, w_out):
    b_, s_, _ = x.shape
    n = s_ // SG_CHUNK
    z = jax.nn.gelu(x @ w_in, approximate=False)
    u, v = jnp.split(z, 2, axis=-1)
    v = layer_norm(v, ln_g, ln_b).reshape(b_, n, SG_CHUNK, SG_GROUPS, SG_GROUP_DIM)
    w_causal = w_s * jnp.tril(jnp.ones((SG_CHUNK, SG_CHUNK), w_s.dtype))
    gate = jnp.einsum('gts,bnsgc->bntgc', w_causal, v) + b_s.T[:, :, None]
    y = u * gate.reshape(b_, s_, SG_WIDTH)
    return y @ w_out


def conv_ffn(x, w_up, conv_w, conv_b, w_down):
    s_ = x.shape[1]
    h = x @ w_up
    a, b = jnp.split(h, 2, axis=-1)
    a_pad = jnp.pad(a, ((0, 0), (CONV_WIDTH - 1, 0), (0, 0)))
    a = sum(conv_w[j] * a_pad[:, j:j + s_] for j in range(CONV_WIDTH)) + conv_b
    return (jax.nn.silu(a) * b) @ w_down


def _fwd_setup_inputs(seed: int = 0) -> dict:
    key = jax.random.key(seed)
    ks = jax.random.split(key, 24)
    nrm = jax.random.normal
    f32 = jnp.float32
    d = D_MODEL
    hg_in_cols = 2 * HG_WIDTH + 2 * HG_V_WIDTH
    return {
        'x': nrm(ks[0], (BATCH, SEQ, d), f32),
        'lb_logits': 0.5 * nrm(ks[1], (DEPTH + 1, HG_WIDTH), f32),
        'hg_w_in': nrm(ks[2], (N_HG_LAYERS, d, hg_in_cols), f32) * d ** -0.5,
        'hg_norm_g': 1.0 + 0.02 * nrm(ks[3], (N_HG_LAYERS, HG_V_WIDTH), f32),
        'hg_w_out': nrm(ks[4], (N_HG_LAYERS, HG_V_WIDTH, d), f32) * HG_V_WIDTH ** -0.5 * BETA,
        'sg_w_in': nrm(ks[5], (N_SG_LAYERS, d, 2 * SG_WIDTH), f32) * d ** -0.5,
        'sg_ln_g': 1.0 + 0.02 * nrm(ks[6], (N_SG_LAYERS, SG_WIDTH), f32),
        'sg_ln_b': 0.02 * nrm(ks[7], (N_SG_LAYERS, SG_WIDTH), f32),
        'sg_w_s': nrm(ks[8], (N_SG_LAYERS, SG_GROUPS, SG_CHUNK, SG_CHUNK), f32) * 0.5 * SG_CHUNK ** -0.5,
        'sg_b_s': 1.0 + 0.1 * nrm(ks[9], (N_SG_LAYERS, SG_GROUPS, SG_CHUNK), f32),
        'sg_w_out': nrm(ks[10], (N_SG_LAYERS, SG_WIDTH, d), f32) * SG_WIDTH ** -0.5 * BETA,
        'ffn_w_up': nrm(ks[11], (DEPTH, d, 2 * D_FF), f32) * d ** -0.5,
        'ffn_conv_w': nrm(ks[12], (DEPTH, CONV_WIDTH, D_FF), f32) * CONV_WIDTH ** -0.5,
        'ffn_conv_b': 0.02 * nrm(ks[13], (DEPTH, D_FF), f32),
        'ffn_w_down': nrm(ks[14], (DEPTH, D_FF, d), f32) * D_FF ** -0.5 * BETA,
        'ln1_g': 1.0 + 0.02 * nrm(ks[15], (DEPTH, d), f32),
        'ln1_b': 0.02 * nrm(ks[16], (DEPTH, d), f32),
        'ln2_g': 1.0 + 0.02 * nrm(ks[17], (DEPTH, d), f32),
        'ln2_b': 0.02 * nrm(ks[18], (DEPTH, d), f32),
    }


def _fwd_reference(x, lb_logits, hg_w_in, hg_norm_g, hg_w_out, sg_w_in, sg_ln_g, sg_ln_b,
              sg_w_s, sg_b_s, sg_w_out, ffn_w_up, ffn_conv_w, ffn_conv_b, ffn_w_down,
              ln1_g, ln1_b, ln2_g, ln2_b):
    lower_bounds = jnp.cumsum(jax.nn.softmax(lb_logits.astype(jnp.float32), axis=0), axis=0)
    h = x
    for layer in range(DEPTH):
        occ = layer // N_MIXERS
        if layer % N_MIXERS == 0:
            mixed = hgrn2_mixer(h, hg_w_in[occ], hg_norm_g[occ], hg_w_out[occ], lower_bounds[layer])
        else:
            mixed = chunked_gmlp(h, sg_w_in[occ], sg_ln_g[occ], sg_ln_b[occ],
                                 sg_w_s[occ], sg_b_s[occ], sg_w_out[occ])
        h = layer_norm(ALPHA * h + mixed, ln1_g[layer], ln1_b[layer])
        ffn = conv_ffn(h, ffn_w_up[layer], ffn_conv_w[layer], ffn_conv_b[layer], ffn_w_down[layer])
        h = layer_norm(ALPHA * h + ffn, ln2_g[layer], ln2_b[layer])
    return h


import jax as _jax
import jax.numpy as _jnp

TWIN_FORMAT = 'train_step'
FWD_PARAMS = ['x', 'lb_logits', 'hg_w_in', 'hg_norm_g', 'hg_w_out', 'sg_w_in', 'sg_ln_g', 'sg_ln_b', 'sg_w_s', 'sg_b_s', 'sg_w_out', 'ffn_w_up', 'ffn_conv_w', 'ffn_conv_b', 'ffn_w_down', 'ln1_g', 'ln1_b', 'ln2_g', 'ln2_b']
TWIN_WEIGHTS = ['lb_logits', 'hg_w_in', 'hg_norm_g', 'hg_w_out', 'sg_w_in', 'sg_ln_g', 'sg_ln_b', 'sg_w_s', 'sg_b_s', 'sg_w_out', 'ffn_w_up', 'ffn_conv_w', 'ffn_conv_b', 'ffn_w_down', 'ln1_g', 'ln1_b', 'ln2_g', 'ln2_b']
TWIN_DIFF_INPUT = 'x'
TWIN_INPUTS = ['x', 'lb_logits', 'hg_w_in', 'hg_norm_g', 'hg_w_out', 'sg_w_in', 'sg_ln_g', 'sg_ln_b', 'sg_w_s', 'sg_b_s', 'sg_w_out', 'ffn_w_up', 'ffn_conv_w', 'ffn_conv_b', 'ffn_w_down', 'ln1_g', 'ln1_b', 'ln2_g', 'ln2_b', 'loss_target', 'm_lb_logits', 'm_hg_w_in', 'm_hg_norm_g', 'm_hg_w_out', 'm_sg_w_in', 'm_sg_ln_g', 'm_sg_ln_b', 'm_sg_w_s', 'm_sg_b_s', 'm_sg_w_out', 'm_ffn_w_up', 'm_ffn_conv_w', 'm_ffn_conv_b', 'm_ffn_w_down', 'm_ln1_g', 'm_ln1_b', 'm_ln2_g', 'm_ln2_b', 'v_lb_logits', 'v_hg_w_in', 'v_hg_norm_g', 'v_hg_w_out', 'v_sg_w_in', 'v_sg_ln_g', 'v_sg_ln_b', 'v_sg_w_s', 'v_sg_b_s', 'v_sg_w_out', 'v_ffn_w_up', 'v_ffn_conv_w', 'v_ffn_conv_b', 'v_ffn_w_down', 'v_ln1_g', 'v_ln1_b', 'v_ln2_g', 'v_ln2_b']
TWIN_OUTPUTS = ['loss', 'grad_x', 'grad_lb_logits', 'grad_hg_w_in', 'grad_hg_norm_g', 'grad_hg_w_out', 'grad_sg_w_in', 'grad_sg_ln_g', 'grad_sg_ln_b', 'grad_sg_w_s', 'grad_sg_b_s', 'grad_sg_w_out', 'grad_ffn_w_up', 'grad_ffn_conv_w', 'grad_ffn_conv_b', 'grad_ffn_w_down', 'grad_ln1_g', 'grad_ln1_b', 'grad_ln2_g', 'grad_ln2_b', 'delta_lb_logits', 'delta_hg_w_in', 'delta_hg_norm_g', 'delta_hg_w_out', 'delta_sg_w_in', 'delta_sg_ln_g', 'delta_sg_ln_b', 'delta_sg_w_s', 'delta_sg_b_s', 'delta_sg_w_out', 'delta_ffn_w_up', 'delta_ffn_conv_w', 'delta_ffn_conv_b', 'delta_ffn_w_down', 'delta_ln1_g', 'delta_ln1_b', 'delta_ln2_g', 'delta_ln2_b', 'new_m_lb_logits', 'new_m_hg_w_in', 'new_m_hg_norm_g', 'new_m_hg_w_out', 'new_m_sg_w_in', 'new_m_sg_ln_g', 'new_m_sg_ln_b', 'new_m_sg_w_s', 'new_m_sg_b_s', 'new_m_sg_w_out', 'new_m_ffn_w_up', 'new_m_ffn_conv_w', 'new_m_ffn_conv_b', 'new_m_ffn_w_down', 'new_m_ln1_g', 'new_m_ln1_b', 'new_m_ln2_g', 'new_m_ln2_b', 'new_v_lb_logits', 'new_v_hg_w_in', 'new_v_hg_norm_g', 'new_v_hg_w_out', 'new_v_sg_w_in', 'new_v_sg_ln_g', 'new_v_sg_ln_b', 'new_v_sg_w_s', 'new_v_sg_b_s', 'new_v_sg_w_out', 'new_v_ffn_w_up', 'new_v_ffn_conv_w', 'new_v_ffn_conv_b', 'new_v_ffn_w_down', 'new_v_ln1_g', 'new_v_ln1_b', 'new_v_ln2_g', 'new_v_ln2_b']
TWIN_LEAF_KINDS = {'loss': 'loss', 'grad_x': 'grad_x', 'grad_lb_logits': 'grad_w', 'grad_hg_w_in': 'grad_w', 'grad_hg_norm_g': 'grad_w', 'grad_hg_w_out': 'grad_w', 'grad_sg_w_in': 'grad_w', 'grad_sg_ln_g': 'grad_w', 'grad_sg_ln_b': 'grad_w', 'grad_sg_w_s': 'grad_w', 'grad_sg_b_s': 'grad_w', 'grad_sg_w_out': 'grad_w', 'grad_ffn_w_up': 'grad_w', 'grad_ffn_conv_w': 'grad_w', 'grad_ffn_conv_b': 'grad_w', 'grad_ffn_w_down': 'grad_w', 'grad_ln1_g': 'grad_w', 'grad_ln1_b': 'grad_w', 'grad_ln2_g': 'grad_w', 'grad_ln2_b': 'grad_w', 'delta_lb_logits': 'delta_w', 'delta_hg_w_in': 'delta_w', 'delta_hg_norm_g': 'delta_w', 'delta_hg_w_out': 'delta_w', 'delta_sg_w_in': 'delta_w', 'delta_sg_ln_g': 'delta_w', 'delta_sg_ln_b': 'delta_w', 'delta_sg_w_s': 'delta_w', 'delta_sg_b_s': 'delta_w', 'delta_sg_w_out': 'delta_w', 'delta_ffn_w_up': 'delta_w', 'delta_ffn_conv_w': 'delta_w', 'delta_ffn_conv_b': 'delta_w', 'delta_ffn_w_down': 'delta_w', 'delta_ln1_g': 'delta_w', 'delta_ln1_b': 'delta_w', 'delta_ln2_g': 'delta_w', 'delta_ln2_b': 'delta_w', 'new_m_lb_logits': 'new_m', 'new_m_hg_w_in': 'new_m', 'new_m_hg_norm_g': 'new_m', 'new_m_hg_w_out': 'new_m', 'new_m_sg_w_in': 'new_m', 'new_m_sg_ln_g': 'new_m', 'new_m_sg_ln_b': 'new_m', 'new_m_sg_w_s': 'new_m', 'new_m_sg_b_s': 'new_m', 'new_m_sg_w_out': 'new_m', 'new_m_ffn_w_up': 'new_m', 'new_m_ffn_conv_w': 'new_m', 'new_m_ffn_conv_b': 'new_m', 'new_m_ffn_w_down': 'new_m', 'new_m_ln1_g': 'new_m', 'new_m_ln1_b': 'new_m', 'new_m_ln2_g': 'new_m', 'new_m_ln2_b': 'new_m', 'new_v_lb_logits': 'new_v', 'new_v_hg_w_in': 'new_v', 'new_v_hg_norm_g': 'new_v', 'new_v_hg_w_out': 'new_v', 'new_v_sg_w_in': 'new_v', 'new_v_sg_ln_g': 'new_v', 'new_v_sg_ln_b': 'new_v', 'new_v_sg_w_s': 'new_v', 'new_v_sg_b_s': 'new_v', 'new_v_sg_w_out': 'new_v', 'new_v_ffn_w_up': 'new_v', 'new_v_ffn_conv_w': 'new_v', 'new_v_ffn_conv_b': 'new_v', 'new_v_ffn_w_down': 'new_v', 'new_v_ln1_g': 'new_v', 'new_v_ln1_b': 'new_v', 'new_v_ln2_g': 'new_v', 'new_v_ln2_b': 'new_v'}


def _forward(args):
    return _fwd_reference(*[args[k] for k in FWD_PARAMS])


def _output_shape():
    out = _jax.eval_shape(lambda: _forward(_fwd_setup_inputs(0)))
    return out.shape, out.dtype

N_MICROBATCH = 1
ADAM_LR = 0.001
ADAM_B1 = 0.9
ADAM_B2 = 0.999
ADAM_EPS = 1e-08
ADAM_WD = 0.01
ADAM_STEP = 10
PER_EXAMPLE_BATCH_AXIS = {'x': 0, 'loss_target': 0}
SHARED_INPUTS = []
_WEIGHT_DTYPES = {'lb_logits': _jnp.float32, 'hg_w_in': _jnp.float32, 'hg_norm_g': _jnp.float32, 'hg_w_out': _jnp.float32, 'sg_w_in': _jnp.float32, 'sg_ln_g': _jnp.float32, 'sg_ln_b': _jnp.float32, 'sg_w_s': _jnp.float32, 'sg_b_s': _jnp.float32, 'sg_w_out': _jnp.float32, 'ffn_w_up': _jnp.float32, 'ffn_conv_w': _jnp.float32, 'ffn_conv_b': _jnp.float32, 'ffn_w_down': _jnp.float32, 'ln1_g': _jnp.float32, 'ln1_b': _jnp.float32, 'ln2_g': _jnp.float32, 'ln2_b': _jnp.float32}
MOMENT_SCALE = {'lb_logits': 8.754826e-04, 'hg_w_in': 1.395829e-02, 'hg_norm_g': 1.947698e-02, 'hg_w_out': 3.845188e-02, 'sg_w_in': 1.682965e-02, 'sg_ln_g': 7.590656e-03, 'sg_ln_b': 7.220343e-03, 'sg_w_s': 1.441343e-02, 'sg_b_s': 2.057014e-02, 'sg_w_out': 5.541815e-02, 'ffn_w_up': 1.171844e-02, 'ffn_conv_w': 1.199027e-02, 'ffn_conv_b': 1.160823e-02, 'ffn_w_down': 3.836240e-02, 'ln1_g': 4.718162e-01, 'ln1_b': 2.306980e-01, 'ln2_g': 1.133304e+01, 'ln2_b': 1.063506e+00}


def _to_microbatches(a, axis):
    t = _jnp.moveaxis(a, axis, 0)
    t = t.reshape((N_MICROBATCH, t.shape[0] // N_MICROBATCH) + t.shape[1:])
    return _jnp.moveaxis(t, 1, axis + 1)


def setup_inputs(seed: int = 0) -> dict:
    inp = _fwd_setup_inputs(seed)
    key = _jax.random.fold_in(_jax.random.key(seed), 7919)
    shape, _ = _output_shape()
    out = dict(inp)
    out["loss_target"] = _jax.random.normal(_jax.random.fold_in(key, 0), shape, _jnp.float32)
    for i, name in enumerate(TWIN_WEIGHTS):
        w = inp[name].astype(_jnp.float32)
        if MOMENT_SCALE is None:
            s = _jnp.sqrt(_jnp.mean(_jnp.square(w)) + 1e-30)
        else:
            s = MOMENT_SCALE[name]
        km, kv = _jax.random.split(_jax.random.fold_in(key, i + 1))
        out[name] = w
        out["m_" + name] = s * _jax.random.normal(km, w.shape, _jnp.float32)
        out["v_" + name] = (s * s) * _jax.random.uniform(kv, w.shape, _jnp.float32, 0.5, 1.5)
    if N_MICROBATCH > 1:
        for name, axis in PER_EXAMPLE_BATCH_AXIS.items():
            out[name] = _to_microbatches(out[name], axis)
    return {'x': out['x'], 'lb_logits': out['lb_logits'], 'hg_w_in': out['hg_w_in'], 'hg_norm_g': out['hg_norm_g'], 'hg_w_out': out['hg_w_out'], 'sg_w_in': out['sg_w_in'], 'sg_ln_g': out['sg_ln_g'], 'sg_ln_b': out['sg_ln_b'], 'sg_w_s': out['sg_w_s'], 'sg_b_s': out['sg_b_s'], 'sg_w_out': out['sg_w_out'], 'ffn_w_up': out['ffn_w_up'], 'ffn_conv_w': out['ffn_conv_w'], 'ffn_conv_b': out['ffn_conv_b'], 'ffn_w_down': out['ffn_w_down'], 'ln1_g': out['ln1_g'], 'ln1_b': out['ln1_b'], 'ln2_g': out['ln2_g'], 'ln2_b': out['ln2_b'], 'loss_target': out['loss_target'], 'm_lb_logits': out['m_lb_logits'], 'm_hg_w_in': out['m_hg_w_in'], 'm_hg_norm_g': out['m_hg_norm_g'], 'm_hg_w_out': out['m_hg_w_out'], 'm_sg_w_in': out['m_sg_w_in'], 'm_sg_ln_g': out['m_sg_ln_g'], 'm_sg_ln_b': out['m_sg_ln_b'], 'm_sg_w_s': out['m_sg_w_s'], 'm_sg_b_s': out['m_sg_b_s'], 'm_sg_w_out': out['m_sg_w_out'], 'm_ffn_w_up': out['m_ffn_w_up'], 'm_ffn_conv_w': out['m_ffn_conv_w'], 'm_ffn_conv_b': out['m_ffn_conv_b'], 'm_ffn_w_down': out['m_ffn_w_down'], 'm_ln1_g': out['m_ln1_g'], 'm_ln1_b': out['m_ln1_b'], 'm_ln2_g': out['m_ln2_g'], 'm_ln2_b': out['m_ln2_b'], 'v_lb_logits': out['v_lb_logits'], 'v_hg_w_in': out['v_hg_w_in'], 'v_hg_norm_g': out['v_hg_norm_g'], 'v_hg_w_out': out['v_hg_w_out'], 'v_sg_w_in': out['v_sg_w_in'], 'v_sg_ln_g': out['v_sg_ln_g'], 'v_sg_ln_b': out['v_sg_ln_b'], 'v_sg_w_s': out['v_sg_w_s'], 'v_sg_b_s': out['v_sg_b_s'], 'v_sg_w_out': out['v_sg_w_out'], 'v_ffn_w_up': out['v_ffn_w_up'], 'v_ffn_conv_w': out['v_ffn_conv_w'], 'v_ffn_conv_b': out['v_ffn_conv_b'], 'v_ffn_w_down': out['v_ffn_w_down'], 'v_ln1_g': out['v_ln1_g'], 'v_ln1_b': out['v_ln1_b'], 'v_ln2_g': out['v_ln2_g'], 'v_ln2_b': out['v_ln2_b']}


def _loss(weights, diff, rest, loss_target):
    with _jax.named_scope("forward"):
        args = {**rest, TWIN_DIFF_INPUT: diff, **{k: w.astype(_WEIGHT_DTYPES[k]) for k, w in weights.items()}}
        y = _forward(args)
    with _jax.named_scope("loss_head"):
        err = _jnp.square(y.astype(_jnp.float32) - loss_target)
        return 0.5 * _jnp.sum(_jnp.mean(err, axis=-1)) if err.ndim else 0.5 * err


def _adamw(w, g, m, v):
    m = ADAM_B1 * m + (1.0 - ADAM_B1) * g
    v = ADAM_B2 * v + (1.0 - ADAM_B2) * _jnp.square(g)
    m_hat = m / (1.0 - ADAM_B1 ** ADAM_STEP)
    v_hat = v / (1.0 - ADAM_B2 ** ADAM_STEP)
    delta = -ADAM_LR * (m_hat / (_jnp.sqrt(v_hat) + ADAM_EPS) + ADAM_WD * w)
    return delta, m, v


def reference(x, lb_logits, hg_w_in, hg_norm_g, hg_w_out, sg_w_in, sg_ln_g, sg_ln_b, sg_w_s, sg_b_s, sg_w_out, ffn_w_up, ffn_conv_w, ffn_conv_b, ffn_w_down, ln1_g, ln1_b, ln2_g, ln2_b, loss_target, m_lb_logits, m_hg_w_in, m_hg_norm_g, m_hg_w_out, m_sg_w_in, m_sg_ln_g, m_sg_ln_b, m_sg_w_s, m_sg_b_s, m_sg_w_out, m_ffn_w_up, m_ffn_conv_w, m_ffn_conv_b, m_ffn_w_down, m_ln1_g, m_ln1_b, m_ln2_g, m_ln2_b, v_lb_logits, v_hg_w_in, v_hg_norm_g, v_hg_w_out, v_sg_w_in, v_sg_ln_g, v_sg_ln_b, v_sg_w_s, v_sg_b_s, v_sg_w_out, v_ffn_w_up, v_ffn_conv_w, v_ffn_conv_b, v_ffn_w_down, v_ln1_g, v_ln1_b, v_ln2_g, v_ln2_b):
    given = dict(x=x, lb_logits=lb_logits, hg_w_in=hg_w_in, hg_norm_g=hg_norm_g, hg_w_out=hg_w_out, sg_w_in=sg_w_in, sg_ln_g=sg_ln_g, sg_ln_b=sg_ln_b, sg_w_s=sg_w_s, sg_b_s=sg_b_s, sg_w_out=sg_w_out, ffn_w_up=ffn_w_up, ffn_conv_w=ffn_conv_w, ffn_conv_b=ffn_conv_b, ffn_w_down=ffn_w_down, ln1_g=ln1_g, ln1_b=ln1_b, ln2_g=ln2_g, ln2_b=ln2_b, loss_target=loss_target, m_lb_logits=m_lb_logits, m_hg_w_in=m_hg_w_in, m_hg_norm_g=m_hg_norm_g, m_hg_w_out=m_hg_w_out, m_sg_w_in=m_sg_w_in, m_sg_ln_g=m_sg_ln_g, m_sg_ln_b=m_sg_ln_b, m_sg_w_s=m_sg_w_s, m_sg_b_s=m_sg_b_s, m_sg_w_out=m_sg_w_out, m_ffn_w_up=m_ffn_w_up, m_ffn_conv_w=m_ffn_conv_w, m_ffn_conv_b=m_ffn_conv_b, m_ffn_w_down=m_ffn_w_down, m_ln1_g=m_ln1_g, m_ln1_b=m_ln1_b, m_ln2_g=m_ln2_g, m_ln2_b=m_ln2_b, v_lb_logits=v_lb_logits, v_hg_w_in=v_hg_w_in, v_hg_norm_g=v_hg_norm_g, v_hg_w_out=v_hg_w_out, v_sg_w_in=v_sg_w_in, v_sg_ln_g=v_sg_ln_g, v_sg_ln_b=v_sg_ln_b, v_sg_w_s=v_sg_w_s, v_sg_b_s=v_sg_b_s, v_sg_w_out=v_sg_w_out, v_ffn_w_up=v_ffn_w_up, v_ffn_conv_w=v_ffn_conv_w, v_ffn_conv_b=v_ffn_conv_b, v_ffn_w_down=v_ffn_w_down, v_ln1_g=v_ln1_g, v_ln1_b=v_ln1_b, v_ln2_g=v_ln2_g, v_ln2_b=v_ln2_b)
    weights = {n: given[n] for n in TWIN_WEIGHTS}
    shared = {n: given[n] for n in SHARED_INPUTS}
    per_example = {n: given[n] for n in ['x']}
    grad_fn = _jax.value_and_grad(_loss, argnums=(0, 1))

    def one_microbatch(ex, loss_target):
        ex = dict(ex)
        diff = ex.pop(TWIN_DIFF_INPUT)
        return grad_fn(weights, diff, {**shared, **ex}, loss_target)

    if N_MICROBATCH == 1:
        loss, (grad_w, grad_x) = one_microbatch(per_example, given["loss_target"])
    else:
        def body(carry, xs):
            loss_sum, grad_sum = carry
            l_k, (gw_k, gx_k) = one_microbatch(xs[0], xs[1])
            with _jax.named_scope("update"):
                return (loss_sum + l_k, _jax.tree.map(_jnp.add, grad_sum, gw_k)), gx_k

        init = (_jnp.zeros((), _jnp.float32), _jax.tree.map(_jnp.zeros_like, weights))
        (loss, grad_w), grad_x = _jax.lax.scan(body, init, (per_example, given["loss_target"]))
    with _jax.named_scope("update"):
        delta_w, new_m, new_v = {}, {}, {}
        for n in TWIN_WEIGHTS:
            delta_w[n], new_m[n], new_v[n] = _adamw(weights[n], grad_w[n], given["m_" + n], given["v_" + n])
    return (loss, grad_x, *[grad_w[n] for n in TWIN_WEIGHTS], *[delta_w[n] for n in TWIN_WEIGHTS],
            *[new_m[n] for n in TWIN_WEIGHTS], *[new_v[n] for n in TWIN_WEIGHTS])
```

```python
import functools

import numpy as np
import jax
import jax.numpy as jnp
from jax import lax
from jax.experimental import pallas as pl
from jax.experimental.pallas import tpu as pltpu

F32 = jnp.float32
BF16 = jnp.bfloat16
HI = lax.Precision.HIGHEST

N_DEV = 8
D_MODEL = 2048
HG_HEADS = 16
HG_DIM = 128
HG_CHUNK = 64
SG_GROUPS = 16
SG_DIM = 128
SG_CHUNK = 128
D_FF = 5632
DEPTH = 2
ALPHA = (2 * DEPTH) ** 0.25
LN_EPS = 1e-5
RMS_EPS = 1e-6
ADAM_LR = 0.001
ADAM_B1 = 0.9
ADAM_B2 = 0.999
ADAM_EPS = 1e-08
ADAM_WD = 0.01
ADAM_STEP = 10

VMEM_LIMIT = 56 * 1024 * 1024
MESH = pl.DeviceIdType.MESH
ANY = pl.BlockSpec(memory_space=pl.ANY)


def _cp(*sem):
    return pltpu.CompilerParams(dimension_semantics=sem, vmem_limit_bytes=VMEM_LIMIT)


def _pick(n, cands):
    for c in cands:
        if n % c == 0:
            return c
    raise ValueError(f"no tile for {n} in {cands}")


_DIMS = {"nn": (((1,), (0,)), ((), ())), "nt": (((1,), (1,)), ((), ())), "tn": (((0,), (0,)), ((), ()))}


def _mm(a, b, *, mode, name, out_dtype=F32, a_cs=False, b_cs=False, o_cs=None, addend=None,
        bm=None, bn=None, bk=None):
    if mode == "tn":
        kk, m = a.shape
    elif a_cs:
        m, kk = a.shape[1], a.shape[0] * a.shape[2]
    else:
        m, kk = a.shape
    if mode == "nt":
        n = b.shape[1] if b_cs else b.shape[0]
    else:
        n = b.shape[0] * b.shape[2] if b_cs else b.shape[1]
    a_c = a.shape[2] if a_cs else None
    b_c = b.shape[2] if b_cs else None
    o_c = n // o_cs if o_cs else None
    bm = bm or _pick(m, (1024, 1408, 512, 256))
    n_div = [c for c in (b_c if (b_cs and mode != "nt") else None, o_c) if c]
    k_div = [c for c in (a_c, b_c if (b_cs and mode == "nt") else None) if c]

    def pick_with(total, divs, cands):
        for c in cands:
            if total % c == 0 and all(d % c == 0 for d in divs):
                return c
        raise ValueError((total, divs))

    bn = bn or pick_with(n, n_div, (1024, 1408, 512, 256, 128))
    bk = bk or pick_with(kk, k_div, (512, 1408, 1024, 256, 128))
    nk = kk // bk

    def cs_idx(blk, per):
        return (blk * per[0]) // per[1], (blk * per[0] % per[1]) // per[0]

    if mode == "tn":
        a_spec = pl.BlockSpec((bk, bm), lambda i, j, k: (k, i))
    elif a_cs:
        def a_map(i, j, k):
            s, r = cs_idx(k, (bk, a_c))
            return (s, i, r)
        a_spec = pl.BlockSpec((None, bm, bk), a_map)
    else:
        a_spec = pl.BlockSpec((bm, bk), lambda i, j, k: (i, k))
    if mode == "nt":
        if b_cs:
            def b_map(i, j, k):
                s, r = cs_idx(k, (bk, b_c))
                return (s, j, r)
            b_spec = pl.BlockSpec((None, bn, bk), b_map)
        else:
            b_spec = pl.BlockSpec((bn, bk), lambda i, j, k: (j, k))
    else:
        if b_cs:
            def b_map(i, j, k):
                s, r = cs_idx(j, (bn, b_c))
                return (s, k, r)
            b_spec = pl.BlockSpec((None, bk, bn), b_map)
        else:
            b_spec = pl.BlockSpec((bk, bn), lambda i, j, k: (k, j))
    if o_cs:
        def o_map(i, j, k):
            s, r = cs_idx(j, (bn, o_c))
            return (s, i, r)
        o_spec = pl.BlockSpec((None, bm, bn), o_map)
        out_shape = jax.ShapeDtypeStruct((o_cs, m, o_c), out_dtype)
    else:
        o_spec = pl.BlockSpec((bm, bn), lambda i, j, k: (i, j))
        out_shape = jax.ShapeDtypeStruct((m, n), out_dtype)
    in_specs = [a_spec, b_spec]
    args = [a, b]
    if addend is not None:
        in_specs.append(pl.BlockSpec((bm, bn), lambda i, j, k: (i, j)))
        args.append(addend)
    dims = _DIMS[mode]

    def body(*refs):
        if addend is not None:
            a_ref, b_ref, add_ref, o_ref, acc_ref = refs
        else:
            a_ref, b_ref, o_ref, acc_ref = refs
        k = pl.program_id(2)

        @pl.when(k == 0)
        def _():
            acc_ref[...] = jnp.zeros_like(acc_ref)

        acc_ref[...] += lax.dot_general(a_ref[...].astype(BF16), b_ref[...].astype(BF16), dims,
                                        preferred_element_type=F32)

        @pl.when(k == nk - 1)
        def _():
            r = acc_ref[...]
            if addend is not None:
                r = r + add_ref[...]
            o_ref[...] = r.astype(o_ref.dtype)

    return pl.pallas_call(
        body, name=name, out_shape=out_shape, grid=(m // bm, n // bn, nk),
        in_specs=in_specs, out_specs=o_spec, scratch_shapes=[pltpu.VMEM((bm, bn), F32)],
        compiler_params=_cp("parallel", "parallel", "arbitrary"),
    )(*args)


LN_ROWS = 256


def _ln_stats(s):
    mu = jnp.mean(s, axis=-1, keepdims=True)
    sc = s - mu
    var = jnp.mean(sc * sc, axis=-1, keepdims=True)
    rstd = lax.rsqrt(var + LN_EPS)
    return sc * rstd, rstd


def _ln_fwd(h, sub, g, b, name):
    t, d = h.shape
    row = pl.BlockSpec((LN_ROWS, d), lambda i: (i, 0))
    vec = pl.BlockSpec((1, d), lambda i: (0, 0))

    def body(h_ref, s_ref, g_ref, b_ref, y_ref, yb_ref):
        xhat, _ = _ln_stats(ALPHA * h_ref[...] + s_ref[...])
        y = xhat * g_ref[...] + b_ref[...]
        y_ref[...] = y
        yb_ref[...] = y.astype(BF16)

    return pl.pallas_call(
        body, name=name, grid=(t // LN_ROWS,),
        out_shape=(jax.ShapeDtypeStruct((t, d), F32), jax.ShapeDtypeStruct((t, d), BF16)),
        in_specs=[row, row, vec, vec], out_specs=(row, row), compiler_params=_cp("parallel"),
    )(h, sub, g, b)


def _ln_bwd_math(xhat, rstd, dy, g):
    dxhat = dy * g
    m1 = jnp.mean(dxhat, axis=-1, keepdims=True)
    m2 = jnp.mean(dxhat * xhat, axis=-1, keepdims=True)
    ds = rstd * (dxhat - m1 - xhat * m2)
    dg = jnp.sum(dy * xhat, axis=0, keepdims=True)
    db = jnp.sum(dy, axis=0, keepdims=True)
    return ds, dg, db


def _ln_bwd(h, sub, dy_a, dy_b, g, name):
    t, d = h.shape
    row = pl.BlockSpec((LN_ROWS, d), lambda i: (i, 0))
    vec = pl.BlockSpec((1, d), lambda i: (0, 0))

    def body(h_ref, s_ref, da_ref, db_ref, g_ref, ds_ref, dres_ref, dg_ref, dbeta_ref):
        xhat, rstd = _ln_stats(ALPHA * h_ref[...] + s_ref[...])
        ds, dg, db = _ln_bwd_math(xhat, rstd, da_ref[...] + db_ref[...], g_ref[...])
        ds_ref[...] = ds.astype(BF16)
        dres_ref[...] = ALPHA * ds

        @pl.when(pl.program_id(0) == 0)
        def _():
            dg_ref[...] = jnp.zeros_like(dg_ref)
            dbeta_ref[...] = jnp.zeros_like(dbeta_ref)

        dg_ref[...] += dg
        dbeta_ref[...] += db

    return pl.pallas_call(
        body, name=name, grid=(t // LN_ROWS,),
        out_shape=(jax.ShapeDtypeStruct((t, d), BF16), jax.ShapeDtypeStruct((t, d), F32),
                   jax.ShapeDtypeStruct((1, d), F32), jax.ShapeDtypeStruct((1, d), F32)),
        in_specs=[row, row, row, row, vec], out_specs=(row, row, vec, vec),
        compiler_params=_cp("arbitrary"),
    )(h, sub, dy_a, dy_b, g)


def _ln_loss_bwd(h, sub, target, g, b, name):
    t, d = h.shape
    row = pl.BlockSpec((LN_ROWS, d), lambda i: (i, 0))
    vec = pl.BlockSpec((1, d), lambda i: (0, 0))
    lvec = pl.BlockSpec((1, 128), lambda i: (0, 0))

    def body(h_ref, s_ref, t_ref, g_ref, b_ref, loss_ref, ds_ref, dres_ref, dg_ref, dbeta_ref):
        xhat, rstd = _ln_stats(ALPHA * h_ref[...] + s_ref[...])
        y = xhat * g_ref[...] + b_ref[...]
        err = y - t_ref[...]
        part = 0.5 * jnp.sum(jnp.mean(err * err, axis=-1, keepdims=True), axis=0, keepdims=True)
        ds, dg, db = _ln_bwd_math(xhat, rstd, err * (1.0 / d), g_ref[...])
        ds_ref[...] = ds.astype(BF16)
        dres_ref[...] = ALPHA * ds

        @pl.when(pl.program_id(0) == 0)
        def _():
            loss_ref[...] = jnp.zeros_like(loss_ref)
            dg_ref[...] = jnp.zeros_like(dg_ref)
            dbeta_ref[...] = jnp.zeros_like(dbeta_ref)

        loss_ref[...] += jnp.broadcast_to(part, loss_ref.shape)
        dg_ref[...] += dg
        dbeta_ref[...] += db

    return pl.pallas_call(
        body, name=name, grid=(t // LN_ROWS,),
        out_shape=(jax.ShapeDtypeStruct((1, 128), F32), jax.ShapeDtypeStruct((t, d), BF16),
                   jax.ShapeDtypeStruct((t, d), F32), jax.ShapeDtypeStruct((1, d), F32),
                   jax.ShapeDtypeStruct((1, d), F32)),
        in_specs=[row, row, row, vec, vec], out_specs=(lvec, row, row, vec, vec),
        compiler_params=_cp("arbitrary"),
    )(h, sub, target, g, b)


HG_ROWS = 512
HG_LEVELS = 6


def _hg_constants():
    c = HG_CHUNK
    t = np.arange(c)[:, None]
    j = np.arange(c)[None, :]
    a_blocks, b_blocks, masks = [], [], [np.eye(c)]
    for lev in range(HG_LEVELS):
        m = 1 << lev
        second = (t % (2 * m)) >= m
        mid = (t // (2 * m)) * (2 * m) + m - 1
        a_blocks.append((second & (j > mid) & (j <= t)).astype(np.float32))
        b_blocks.append(((~second) & (j > t) & (j <= mid)).astype(np.float32))
        same = (t // (2 * m)) == (j // (2 * m))
        masks.append((same & second & ((j % (2 * m)) < m)).astype(np.float32))
    incl = (j <= t).astype(np.float32)
    after = (j > t).astype(np.float32)
    total = np.ones((c, c), np.float32)
    cm = np.concatenate(a_blocks + b_blocks + [incl, after, total], axis=0)
    mk = np.concatenate(masks, axis=0).astype(np.float32)
    return jnp.asarray(cm), jnp.asarray(mk)


def _dot(a, b, dims):
    return lax.dot_general(a, b, dims, precision=HI, preferred_element_type=F32)


def _hg_chunk(pq, pf, pi, pg, l0, l1, l2, ng, s_t, cm, mk):
    c = HG_CHUNK
    mx = jnp.maximum(jnp.maximum(l0, l1), l2)
    e0, e1, e2 = jnp.exp(l0 - mx), jnp.exp(l1 - mx), jnp.exp(l2 - mx)
    lb = e0 / (e0 + e1 + e2)
    q = pq * jax.nn.sigmoid(pq)
    t1 = jnp.log(lb)
    t2 = jnp.log1p(-lb) + jax.nn.log_sigmoid(pf)
    lf = jnp.maximum(t1, t2) + jnp.log1p(jnp.exp(-jnp.abs(t1 - t2)))
    k = (1.0 - lb) * jax.nn.sigmoid(-pf)
    x = jnp.exp(_dot(cm, lf, _DIMS["nn"]))
    scores = mk[0:c] * _dot(q, k, _DIMS["nt"])
    for lev in range(HG_LEVELS):
        qs = q * x[lev * c:(lev + 1) * c]
        ks = k * x[(HG_LEVELS + lev) * c:(HG_LEVELS + lev + 1) * c]
        scores = scores + mk[(lev + 1) * c:(lev + 2) * c] * _dot(qs, ks, _DIMS["nt"])
    x_incl = x[2 * HG_LEVELS * c:(2 * HG_LEVELS + 1) * c]
    x_after = x[(2 * HG_LEVELS + 1) * c:(2 * HG_LEVELS + 2) * c]
    x_total = x[(2 * HG_LEVELS + 2) * c:(2 * HG_LEVELS + 3) * c]
    o = _dot(scores, pi, _DIMS["nn"]) + _dot(q * x_incl, s_t, _DIMS["nt"])
    s_new = s_t * jnp.concatenate([x_total, x_total], axis=0) + _dot(pi, k * x_after, _DIMS["tn"])
    rstd = lax.rsqrt(jnp.mean(o * o, axis=-1, keepdims=True) + RMS_EPS)
    y = o * rstd * ng * (pg * jax.nn.sigmoid(pg))
    return y, s_new


def _hg_specs(t, reverse):
    nrb = t // HG_ROWS
    rb = (lambda r: nrb - 1 - r) if reverse else (lambda r: r)
    proj = pl.BlockSpec((4, HG_ROWS, HG_DIM), lambda h, r: (0, rb(r), h))
    vec = pl.BlockSpec((1, HG_DIM), lambda h, r: (0, h))
    cm = pl.BlockSpec((15 * HG_CHUNK, HG_CHUNK), lambda h, r: (0, 0))
    mk = pl.BlockSpec((7 * HG_CHUNK, HG_CHUNK), lambda h, r: (0, 0))
    rows = pl.BlockSpec((HG_ROWS, HG_DIM), lambda h, r: (rb(r), h))
    states = pl.BlockSpec((None, HG_ROWS // HG_CHUNK, HG_DIM, HG_DIM), lambda h, r: (h, rb(r), 0, 0))
    return proj, vec, cm, mk, rows, states


def _hg_fwd(proj, l0, l1, l2, ng):
    t = proj.shape[1]
    n_in = HG_ROWS // HG_CHUNK
    cm, mk = _hg_constants()
    p_spec, vec, cm_spec, mk_spec, rows, st_spec = _hg_specs(t, False)

    def body(p_ref, l0_ref, l1_ref, l2_ref, ng_ref, cm_ref, mk_ref, y_ref, st_ref, s_ref):
        @pl.when(pl.program_id(1) == 0)
        def _():
            s_ref[...] = jnp.zeros_like(s_ref)

        def step(ci, carry):
            sl = pl.ds(pl.multiple_of(ci * HG_CHUNK, HG_CHUNK), HG_CHUNK)
            s_t = s_ref[...]
            st_ref[ci] = s_t
            y, s_new = _hg_chunk(p_ref[0, sl, :], p_ref[1, sl, :], p_ref[2, sl, :], p_ref[3, sl, :],
                                 l0_ref[...], l1_ref[...], l2_ref[...], ng_ref[...], s_t,
                                 cm_ref[...], mk_ref[...])
            y_ref[sl, :] = y.astype(BF16)
            s_ref[...] = s_new
            return carry

        lax.fori_loop(0, n_in, step, 0)

    return pl.pallas_call(
        body, name="hg_fwd", grid=(HG_HEADS, t // HG_ROWS),
        out_shape=(jax.ShapeDtypeStruct((t, D_MODEL), BF16),
                   jax.ShapeDtypeStruct((HG_HEADS, t // HG_CHUNK, HG_DIM, HG_DIM), F32)),
        in_specs=[p_spec, vec, vec, vec, vec, cm_spec, mk_spec], out_specs=(rows, st_spec),
        scratch_shapes=[pltpu.VMEM((HG_DIM, HG_DIM), F32)],
        compiler_params=_cp("parallel", "arbitrary"),
    )(proj, l0, l1, l2, ng, cm, mk)


def _hg_bwd(proj, states, dy, l0, l1, l2, ng):
    t = proj.shape[1]
    n_in = HG_ROWS // HG_CHUNK
    cm, mk = _hg_constants()
    p_spec, vec, cm_spec, mk_spec, rows, st_spec = _hg_specs(t, True)

    def body(p_ref, st_ref, dy_ref, l0_ref, l1_ref, l2_ref, ng_ref, cm_ref, mk_ref,
             dp_ref, dl0_ref, dl1_ref, dl2_ref, dng_ref, ds_ref):
        @pl.when(pl.program_id(1) == 0)
        def _():
            ds_ref[...] = jnp.zeros_like(ds_ref)
            for r in (dl0_ref, dl1_ref, dl2_ref, dng_ref):
                r[...] = jnp.zeros_like(r)

        def step(it, carry):
            ci = n_in - 1 - it
            sl = pl.ds(pl.multiple_of(ci * HG_CHUNK, HG_CHUNK), HG_CHUNK)
            fn = functools.partial(_hg_chunk, cm=cm_ref[...], mk=mk_ref[...])
            _, vjp = jax.vjp(fn, p_ref[0, sl, :], p_ref[1, sl, :], p_ref[2, sl, :], p_ref[3, sl, :],
                             l0_ref[...], l1_ref[...], l2_ref[...], ng_ref[...], st_ref[ci])
            dq, df, di, dg, d0, d1, d2, dn, ds = vjp((dy_ref[sl, :], ds_ref[...]))
            dp_ref[0, sl, :] = dq.astype(BF16)
            dp_ref[1, sl, :] = df.astype(BF16)
            dp_ref[2, sl, :] = di.astype(BF16)
            dp_ref[3, sl, :] = dg.astype(BF16)
            dl0_ref[...] += d0
            dl1_ref[...] += d1
            dl2_ref[...] += d2
            dng_ref[...] += dn
            ds_ref[...] = ds
            return carry

        lax.fori_loop(0, n_in, step, 0)

    v_shape = jax.ShapeDtypeStruct((1, D_MODEL), F32)
    return pl.pallas_call(
        body, name="hg_bwd", grid=(HG_HEADS, t // HG_ROWS),
        out_shape=(jax.ShapeDtypeStruct((4, t, D_MODEL), BF16), v_shape, v_shape, v_shape, v_shape),
        in_specs=[p_spec, st_spec, rows, vec, vec, vec, vec, cm_spec, mk_spec],
        out_specs=(p_spec, vec, vec, vec, vec),
        scratch_shapes=[pltpu.VMEM((HG_DIM, HG_DIM), F32)],
        compiler_params=_cp("parallel", "arbitrary"),
    )(proj, states, dy, l0, l1, l2, ng, cm, mk)


_SQRT_HALF = 0.7071067811865476
_INV_SQRT_2PI = 0.3989422804014327


def _gelu(x):
    return 0.5 * x * (1.0 + lax.erf(x * _SQRT_HALF))


def _gelu_grad(x):
    return 0.5 * (1.0 + lax.erf(x * _SQRT_HALF)) + x * (_INV_SQRT_2PI * jnp.exp(-0.5 * x * x))


def _tril(n):
    return (lax.broadcasted_iota(jnp.int32, (n, n), 0) >= lax.broadcasted_iota(jnp.int32, (n, n), 1)).astype(F32)


def _sg_specs(d):
    row = lambda w: pl.BlockSpec((SG_CHUNK, w), lambda i: (i, 0))
    vec = pl.BlockSpec((1, d), lambda i: (0, 0))
    cube = pl.BlockSpec((SG_GROUPS, SG_CHUNK, SG_CHUNK), lambda i: (0, 0, 0))
    return row, vec, cube


def _sg_fwd(pre, ln_g, ln_b, w_s, bias):
    t = pre.shape[0]
    d = D_MODEL
    row, vec, cube = _sg_specs(d)

    def body(pre_ref, g_ref, b_ref, w_ref, bias_ref, y_ref, vln_ref):
        u = _gelu(pre_ref[:, :d])
        vhat, _ = _ln_stats(_gelu(pre_ref[:, d:]))
        vln_ref[...] = vhat * g_ref[...] + b_ref[...]
        tril = _tril(SG_CHUNK)
        for g in range(SG_GROUPS):
            cs = slice(g * SG_DIM, (g + 1) * SG_DIM)
            gate = _dot(w_ref[g] * tril, vln_ref[:, cs], _DIMS["nn"]) + bias_ref[g]
            y_ref[:, cs] = (u[:, cs] * gate).astype(BF16)

    return pl.pallas_call(
        body, name="sg_fwd", grid=(t // SG_CHUNK,), out_shape=jax.ShapeDtypeStruct((t, d), BF16),
        in_specs=[row(2 * d), vec, vec, cube, cube], out_specs=row(d),
        scratch_shapes=[pltpu.VMEM((SG_CHUNK, d), F32)], compiler_params=_cp("parallel"),
    )(pre, ln_g, ln_b, w_s, bias)


def _sg_bwd(pre, dy, ln_g, ln_b, w_s, bias):
    t = pre.shape[0]
    d = D_MODEL
    row, vec, cube = _sg_specs(d)
    dbs_spec = pl.BlockSpec((SG_GROUPS, 8, SG_CHUNK), lambda i: (0, 0, 0))

    def body(pre_ref, dy_ref, g_ref, b_ref, w_ref, bias_ref, dpre_ref, dw_ref, dbs_ref, dlg_ref, dlb_ref,
             vln_ref, dvln_ref):
        @pl.when(pl.program_id(0) == 0)
        def _():
            for r in (dw_ref, dbs_ref, dlg_ref, dlb_ref):
                r[...] = jnp.zeros_like(r)

        pu = pre_ref[:, :d]
        pv = pre_ref[:, d:]
        u = _gelu(pu)
        vhat, rstd = _ln_stats(_gelu(pv))
        vln_ref[...] = vhat * g_ref[...] + b_ref[...]
        tril = _tril(SG_CHUNK)
        ones = jnp.ones((8, SG_DIM), F32)
        for g in range(SG_GROUPS):
            cs = slice(g * SG_DIM, (g + 1) * SG_DIM)
            wc = w_ref[g] * tril
            vg = vln_ref[:, cs]
            gate = _dot(wc, vg, _DIMS["nn"]) + bias_ref[g]
            dyg = dy_ref[:, cs]
            dgate = dyg * u[:, cs]
            dpre_ref[:, cs] = (dyg * gate * _gelu_grad(pu[:, cs])).astype(BF16)
            dw_ref[g] += tril * _dot(dgate, vg, _DIMS["nt"])
            dbs_ref[g] += _dot(ones, dgate, _DIMS["nt"])
            dvln_ref[:, cs] = _dot(wc, dgate, _DIMS["tn"])
        dvln = dvln_ref[...]
        dv, dg, db = _ln_bwd_math(vhat, rstd, dvln, g_ref[...])
        dlg_ref[...] += dg
        dlb_ref[...] += db
        dpre_ref[:, d:] = (dv * _gelu_grad(pv)).astype(BF16)

    return pl.pallas_call(
        body, name="sg_bwd", grid=(t // SG_CHUNK,),
        out_shape=(jax.ShapeDtypeStruct((t, 2 * d), BF16), jax.ShapeDtypeStruct(w_s.shape, F32),
                   jax.ShapeDtypeStruct((SG_GROUPS, 8, SG_CHUNK), F32),
                   jax.ShapeDtypeStruct((1, d), F32), jax.ShapeDtypeStruct((1, d), F32)),
        in_specs=[row(2 * d), row(d), vec, vec, cube, cube],
        out_specs=(row(2 * d), cube, dbs_spec, vec, vec),
        scratch_shapes=[pltpu.VMEM((SG_CHUNK, d), F32), pltpu.VMEM((SG_CHUNK, d), F32)],
        compiler_params=_cp("arbitrary"),
    )(pre, dy, ln_g, ln_b, w_s, bias)


FFN_ROWS = 256
FFN_COLS = 1408
HALO = 8


def _ffn_conv(a_ext, w_ref, cb_ref):
    a1 = pltpu.roll(a_ext, 1, 0)
    a2 = pltpu.roll(a_ext, 2, 0)
    return w_ref[0:1, :] * a2 + w_ref[1:2, :] * a1 + w_ref[2:3, :] * a_ext + cb_ref[...], a1, a2


def _ffn_fwd(hh, conv_w, conv_b):
    t = hh.shape[1]
    nb8 = FFN_ROWS // HALO
    main = pl.BlockSpec((2, FFN_ROWS, FFN_COLS), lambda c, r: (0, r, c))
    prev = pl.BlockSpec((None, HALO, FFN_COLS), lambda c, r: (0, jnp.maximum(r * nb8 - 1, 0), c))
    wspec = pl.BlockSpec((3, FFN_COLS), lambda c, r: (0, c))
    bspec = pl.BlockSpec((1, FFN_COLS), lambda c, r: (0, c))

    def body(m_ref, p_ref, w_ref, cb_ref, z_ref):
        prev = jnp.where(pl.program_id(1) == 0, 0.0, p_ref[...])
        a_ext = jnp.concatenate([prev, m_ref[0]], axis=0)
        a, _, _ = _ffn_conv(a_ext, w_ref, cb_ref)
        a = a[HALO:]
        z_ref[...] = (a * jax.nn.sigmoid(a) * m_ref[1]).astype(BF16)

    return pl.pallas_call(
        body, name="ffn_fwd", grid=(D_FF // FFN_COLS, t // FFN_ROWS),
        out_shape=jax.ShapeDtypeStruct((t, D_FF), BF16),
        in_specs=[main, prev, wspec, bspec],
        out_specs=pl.BlockSpec((FFN_ROWS, FFN_COLS), lambda c, r: (r, c)),
        compiler_params=_cp("parallel", "parallel"),
    )(hh, hh, conv_w, conv_b)


def _ffn_bwd(hh, dz, conv_w, conv_b):
    t = hh.shape[1]
    nb8 = FFN_ROWS // HALO
    last8 = t // HALO - 1
    nr = t // FFN_ROWS
    main = pl.BlockSpec((2, FFN_ROWS, FFN_COLS), lambda c, r: (0, r, c))
    prev = pl.BlockSpec((None, HALO, FFN_COLS), lambda c, r: (0, jnp.maximum(r * nb8 - 1, 0), c))
    nxt = pl.BlockSpec((2, HALO, FFN_COLS), lambda c, r: (0, jnp.minimum((r + 1) * nb8, last8), c))
    dmain = pl.BlockSpec((FFN_ROWS, FFN_COLS), lambda c, r: (r, c))
    dnxt = pl.BlockSpec((HALO, FFN_COLS), lambda c, r: (jnp.minimum((r + 1) * nb8, last8), c))
    wspec = pl.BlockSpec((3, FFN_COLS), lambda c, r: (0, c))
    bspec = pl.BlockSpec((1, FFN_COLS), lambda c, r: (0, c))

    def body(m_ref, p_ref, n_ref, dz_ref, dzn_ref, w_ref, cb_ref, dh_ref, dw_ref, dcb_ref):
        r = pl.program_id(1)

        @pl.when(r == 0)
        def _():
            dw_ref[...] = jnp.zeros_like(dw_ref)
            dcb_ref[...] = jnp.zeros_like(dcb_ref)

        prev = jnp.where(r == 0, 0.0, p_ref[...])
        a_ext = jnp.concatenate([prev, m_ref[0], n_ref[0]], axis=0)
        a, a1, a2 = _ffn_conv(a_ext, w_ref, cb_ref)
        a = a[HALO:]
        b_ext = jnp.concatenate([m_ref[1], n_ref[1]], axis=0)
        dz_ext = jnp.concatenate([dz_ref[...], jnp.where(r == nr - 1, 0.0, dzn_ref[...])], axis=0)
        sig = jax.nn.sigmoid(a)
        da = dz_ext * b_ext * (sig * (1.0 + a * (1.0 - sig)))
        n_ext = FFN_ROWS + HALO
        da_p1 = pltpu.roll(da, n_ext - 1, 0)
        da_p2 = pltpu.roll(da, n_ext - 2, 0)
        da_raw = w_ref[2:3, :] * da + w_ref[1:2, :] * da_p1 + w_ref[0:1, :] * da_p2
        dh_ref[0] = da_raw[:FFN_ROWS].astype(BF16)
        dh_ref[1] = (dz_ref[...] * (a * sig)[:FFN_ROWS]).astype(BF16)
        dam = da[:FFN_ROWS]
        rows = slice(HALO, HALO + FFN_ROWS)
        dw_ref[0:1, :] += jnp.sum(dam * a2[rows], axis=0, keepdims=True)
        dw_ref[1:2, :] += jnp.sum(dam * a1[rows], axis=0, keepdims=True)
        dw_ref[2:3, :] += jnp.sum(dam * a_ext[rows], axis=0, keepdims=True)
        dcb_ref[...] += jnp.sum(dam, axis=0, keepdims=True)

    return pl.pallas_call(
        body, name="ffn_bwd", grid=(D_FF // FFN_COLS, nr),
        out_shape=(jax.ShapeDtypeStruct((2, t, D_FF), BF16), jax.ShapeDtypeStruct((3, D_FF), F32),
                   jax.ShapeDtypeStruct((1, D_FF), F32)),
        in_specs=[main, prev, nxt, dmain, dnxt, wspec, bspec],
        out_specs=(main, wspec, bspec), compiler_params=_cp("parallel", "arbitrary"),
    )(hh, hh, hh, dz, dz, conv_w, conv_b)


def _adamw_math(w, g, m, v):
    m = ADAM_B1 * m + (1.0 - ADAM_B1) * g
    v = ADAM_B2 * v + (1.0 - ADAM_B2) * (g * g)
    m_hat = m / (1.0 - ADAM_B1 ** ADAM_STEP)
    v_hat = v / (1.0 - ADAM_B2 ** ADAM_STEP)
    delta = -ADAM_LR * (m_hat / (jnp.sqrt(v_hat) + ADAM_EPS) + ADAM_WD * w)
    return delta, m, v


def _as2d(shape):
    n = int(np.prod(shape))
    c = shape[-1] if shape[-1] % 128 == 0 else (128 if n % 128 == 0 else shape[-1])
    return n // c, c


def _adamw(w, g, m, v, name, parts=None):
    shape = w.shape
    r, c = _as2d(shape)
    br = r if r <= 512 else _pick(r, (256, 176, 128, 64, 8))
    blk = pl.BlockSpec((br, c), lambda i: (i, 0))
    w2, m2, v2 = (a.reshape(r, c) for a in (w, m, v))
    if parts is not None:
        g_in = parts.reshape(N_DEV, r, c)
        g_spec = pl.BlockSpec((N_DEV, br, c), lambda i: (0, i, 0))
    else:
        g_in = g.reshape(r, c)
        g_spec = blk

    def body(w_ref, g_ref, m_ref, v_ref, go_ref, d_ref, mo_ref, vo_ref):
        if parts is not None:
            gsum = g_ref[0].astype(F32)
            for i in range(1, N_DEV):
                gsum = gsum + g_ref[i].astype(F32)
        else:
            gsum = g_ref[...]
        delta, m_new, v_new = _adamw_math(w_ref[...], gsum, m_ref[...], v_ref[...])
        go_ref[...] = gsum
        d_ref[...] = delta
        mo_ref[...] = m_new
        vo_ref[...] = v_new

    o = jax.ShapeDtypeStruct((r, c), F32)
    outs = pl.pallas_call(
        body, name=name, grid=(r // br,), out_shape=(o, o, o, o),
        in_specs=[blk, g_spec, blk, blk], out_specs=(blk, blk, blk, blk), compiler_params=_cp("parallel"),
    )(w2, g_in, m2, v2)
    return tuple(a.reshape(shape) for a in outs)


def _me_and_peers():
    x, y, c = lax.axis_index("x"), lax.axis_index("y"), lax.axis_index("c")
    me = 4 * x + 2 * y + c
    peers = []
    for k in range(1, N_DEV):
        kx, ky, kc = (k >> 2) & 1, (k >> 1) & 1, k & 1
        px, py, pc = x ^ kx, y ^ ky, c ^ kc
        peers.append(((px, py, pc), 4 * px + 2 * py + pc))
    return me, peers


def _all_gather(shard, name):
    def body(x_ref, o_ref, send_sems, recv_sems, local_sem):
        me, peers = _me_and_peers()
        mine = pltpu.make_async_copy(x_ref, o_ref.at[me], local_sem)
        mine.start()
        sends = []
        for k, (dev, _) in enumerate(peers):
            cp = pltpu.make_async_remote_copy(src_ref=x_ref, dst_ref=o_ref.at[me], send_sem=send_sems.at[k],
                                              recv_sem=recv_sems.at[k], device_id=dev, device_id_type=MESH)
            cp.start()
            sends.append(cp)
        for k, (dev, idx) in enumerate(peers):
            pltpu.make_async_remote_copy(src_ref=x_ref, dst_ref=o_ref.at[idx], send_sem=send_sems.at[k],
                                         recv_sem=recv_sems.at[k], device_id=dev, device_id_type=MESH).wait_recv()
        for cp in sends:
            cp.wait_send()
        mine.wait()

    return pl.pallas_call(
        body, name=name, out_shape=jax.ShapeDtypeStruct((N_DEV,) + shard.shape, shard.dtype),
        in_specs=[ANY], out_specs=ANY,
        scratch_shapes=[pltpu.SemaphoreType.DMA((N_DEV - 1,)), pltpu.SemaphoreType.DMA((N_DEV - 1,)),
                        pltpu.SemaphoreType.DMA],
        compiler_params=pltpu.CompilerParams(has_side_effects=True),
    )(shard)


def _exchange_parts(parts, name):
    def body(x_ref, o_ref, send_sems, recv_sems, local_sem):
        me, peers = _me_and_peers()
        mine = pltpu.make_async_copy(x_ref.at[me], o_ref.at[me], local_sem)
        mine.start()
        sends = []
        for k, (dev, idx) in enumerate(peers):
            cp = pltpu.make_async_remote_copy(src_ref=x_ref.at[idx], dst_ref=o_ref.at[me], send_sem=send_sems.at[k],
                                              recv_sem=recv_sems.at[k], device_id=dev, device_id_type=MESH)
            cp.start()
            sends.append(cp)
        for k, (dev, idx) in enumerate(peers):
            pltpu.make_async_remote_copy(src_ref=x_ref.at[idx], dst_ref=o_ref.at[idx], send_sem=send_sems.at[k],
                                         recv_sem=recv_sems.at[k], device_id=dev, device_id_type=MESH).wait_recv()
        for cp in sends:
            cp.wait_send()
        mine.wait()

    return pl.pallas_call(
        body, name=name, out_shape=jax.ShapeDtypeStruct(parts.shape, parts.dtype),
        in_specs=[ANY], out_specs=ANY,
        scratch_shapes=[pltpu.SemaphoreType.DMA((N_DEV - 1,)), pltpu.SemaphoreType.DMA((N_DEV - 1,)),
                        pltpu.SemaphoreType.DMA],
        compiler_params=pltpu.CompilerParams(has_side_effects=True),
    )(parts)


def _all_reduce_small(v, name):
    r, c = v.shape

    def body(x_ref, o_ref, slots, send_sems, recv_sems):
        me, peers = _me_and_peers()
        sends = []
        for k, (dev, _) in enumerate(peers):
            cp = pltpu.make_async_remote_copy(src_ref=x_ref, dst_ref=slots.at[me], send_sem=send_sems.at[k],
                                              recv_sem=recv_sems.at[k], device_id=dev, device_id_type=MESH)
            cp.start()
            sends.append(cp)
        slots[me] = x_ref[...]
        for k, (dev, idx) in enumerate(peers):
            pltpu.make_async_remote_copy(src_ref=x_ref, dst_ref=slots.at[idx], send_sem=send_sems.at[k],
                                         recv_sem=recv_sems.at[k], device_id=dev, device_id_type=MESH).wait_recv()
        for cp in sends:
            cp.wait_send()
        acc = slots[0]
        for i in range(1, N_DEV):
            acc = acc + slots[i]
        o_ref[...] = acc

    return pl.pallas_call(
        body, name=name, out_shape=jax.ShapeDtypeStruct((r, c), F32),
        in_specs=[pl.BlockSpec(memory_space=pltpu.VMEM)], out_specs=pl.BlockSpec(memory_space=pltpu.VMEM),
        scratch_shapes=[pltpu.VMEM((N_DEV, r, c), F32), pltpu.SemaphoreType.DMA((N_DEV - 1,)),
                        pltpu.SemaphoreType.DMA((N_DEV - 1,))],
        compiler_params=pltpu.CompilerParams(has_side_effects=True, vmem_limit_bytes=VMEM_LIMIT),
    )(v)


def _local_step(x, target, lb_logits, p):
    g = {}
    l0, l1, l2 = lb_logits[0:1], lb_logits[1:2], lb_logits[2:3]
    x_bf = x.astype(BF16)

    proj0 = _mm(x_bf, p["hg_w_in"], mode="nn", b_cs=True, o_cs=4, name="mm_hg_in")
    y0, states = _hg_fwd(proj0, l0, l1, l2, p["hg_norm_g"])
    mixed0 = _mm(y0, p["hg_w_out"], mode="nn", name="mm_hg_out")
    h = [x]
    hb = [x_bf]
    subs = [mixed0]
    hh_l, z_l = [], []

    def ffn_forward(layer, h_in_bf):
        hh = _mm(h_in_bf, p["ffn_w_up"][layer], mode="nn", b_cs=True, o_cs=2, name=f"mm_up{layer}")
        z = _ffn_fwd(hh, p["ffn_conv_w"][layer], p["ffn_conv_b"][layer])
        ffn = _mm(z, p["ffn_w_down"][layer], mode="nn", name=f"mm_down{layer}")
        hh_l.append(hh)
        z_l.append(z)
        return ffn

    h1, h1b = _ln_fwd(x, mixed0, p["ln1_g"][0], p["ln1_b"][0], "ln1_0")
    ffn0 = ffn_forward(0, h1b)
    h2, h2b = _ln_fwd(h1, ffn0, p["ln2_g"][0], p["ln2_b"][0], "ln2_0")
    pre1 = _mm(h2b, p["sg_w_in"], mode="nn", b_cs=True, name="mm_sg_in")
    y1 = _sg_fwd(pre1, p["sg_ln_g"], p["sg_ln_b"], p["sg_w_s"], p["sg_bias"])
    mixed1 = _mm(y1, p["sg_w_out"], mode="nn", name="mm_sg_out")
    h3, h3b = _ln_fwd(h2, mixed1, p["ln1_g"][1], p["ln1_b"][1], "ln1_1")
    ffn1 = ffn_forward(1, h3b)

    loss, ds, dres, g["ln2_g1"], g["ln2_b1"] = _ln_loss_bwd(h3, ffn1, target, p["ln2_g"][1], p["ln2_b"][1], "ln2_1_loss")

    def ffn_backward(layer, ds_bf, h_in_bf):
        dz = _mm(ds_bf, p["ffn_w_down"][layer], mode="nt", name=f"mm_d_z{layer}")
        g[f"ffn_w_down{layer}"] = _mm(z_l[layer], ds_bf, mode="tn", out_dtype=BF16, name=f"mm_dw_down{layer}")
        dhh, g[f"conv_w{layer}"], g[f"conv_b{layer}"] = _ffn_bwd(hh_l[layer], dz, p["ffn_conv_w"][layer],
                                                                   p["ffn_conv_b"][layer])
        dh = _mm(dhh, p["ffn_w_up"][layer], mode="nt", a_cs=True, b_cs=True, name=f"mm_d_up{layer}")
        g[f"ffn_w_up{layer}"] = _mm(h_in_bf, dhh, mode="tn", b_cs=True, o_cs=N_DEV, out_dtype=BF16,
                                    name=f"mm_dw_up{layer}")
        return dh

    dh3 = ffn_backward(1, ds, h3b)
    ds, dres, g["ln1_g1"], g["ln1_b1"] = _ln_bwd(h2, mixed1, dres, dh3, p["ln1_g"][1], "ln1_1_bwd")
    dy1 = _mm(ds, p["sg_w_out"], mode="nt", name="mm_d_sg_out")
    g["sg_w_out"] = _mm(y1, ds, mode="tn", out_dtype=BF16, name="mm_dw_sg_out")
    dpre1, g["sg_w_s"], dbs, g["sg_ln_g"], g["sg_ln_b"] = _sg_bwd(pre1, dy1, p["sg_ln_g"], p["sg_ln_b"],
                                                                    p["sg_w_s"], p["sg_bias"])
    g["sg_b_s"] = dbs[:, 0, :]
    dh2 = _mm(dpre1, p["sg_w_in"], mode="nt", b_cs=True, name="mm_d_sg_in")
    g["sg_w_in"] = _mm(h2b, dpre1, mode="tn", o_cs=N_DEV, out_dtype=BF16, name="mm_dw_sg_in")
    ds, dres, g["ln2_g0"], g["ln2_b0"] = _ln_bwd(h1, ffn0, dres, dh2, p["ln2_g"][0], "ln2_0_bwd")
    dh1 = ffn_backward(0, ds, h1b)
    ds, dres, g["ln1_g0"], g["ln1_b0"] = _ln_bwd(x, mixed0, dres, dh1, p["ln1_g"][0], "ln1_0_bwd")
    dy0 = _mm(ds, p["hg_w_out"], mode="nt", name="mm_d_hg_out")
    g["hg_w_out"] = _mm(y0, ds, mode="tn", out_dtype=BF16, name="mm_dw_hg_out")
    dproj, d0, d1, d2, g["hg_norm_g"] = _hg_bwd(proj0, states, dy0, l0, l1, l2, p["hg_norm_g"])
    g["lb_logits"] = jnp.concatenate([d0, d1, d2], axis=0)
    grad_x = _mm(dproj, p["hg_w_in"], mode="nt", a_cs=True, b_cs=True, addend=dres, name="mm_d_hg_in")
    g["hg_w_in"] = _mm(x_bf, dproj, mode="tn", b_cs=True, o_cs=N_DEV, out_dtype=BF16, name="mm_dw_hg_in")
    return loss[0, 0], grad_x, g


_SMALL = ("lb_logits", "hg_norm_g", "sg_ln_g", "sg_ln_b", "sg_w_s", "sg_b_s", "ffn_conv_w", "ffn_conv_b",
          "ln1_g", "ln1_b", "ln2_g", "ln2_b")
_NAMES = ("lb_logits", "hg_w_in", "hg_norm_g", "hg_w_out", "sg_w_in", "sg_ln_g", "sg_ln_b", "sg_w_s", "sg_b_s",
          "sg_w_out", "ffn_w_up", "ffn_conv_w", "ffn_conv_b", "ffn_w_down", "ln1_g", "ln1_b", "ln2_g", "ln2_b")


def kernel(x, lb_logits, hg_w_in, hg_norm_g, hg_w_out, sg_w_in, sg_ln_g, sg_ln_b, sg_w_s, sg_b_s, sg_w_out, ffn_w_up, ffn_conv_w, ffn_conv_b, ffn_w_down, ln1_g, ln1_b, ln2_g, ln2_b, loss_target, m_lb_logits, m_hg_w_in, m_hg_norm_g, m_hg_w_out, m_sg_w_in, m_sg_ln_g, m_sg_ln_b, m_sg_w_s, m_sg_b_s, m_sg_w_out, m_ffn_w_up, m_ffn_conv_w, m_ffn_conv_b, m_ffn_w_down, m_ln1_g, m_ln1_b, m_ln2_g, m_ln2_b, v_lb_logits, v_hg_w_in, v_hg_norm_g, v_hg_w_out, v_sg_w_in, v_sg_ln_g, v_sg_ln_b, v_sg_w_s, v_sg_b_s, v_sg_w_out, v_ffn_w_up, v_ffn_conv_w, v_ffn_conv_b, v_ffn_w_down, v_ln1_g, v_ln1_b, v_ln2_g, v_ln2_b):
    w = dict(lb_logits=lb_logits, hg_w_in=hg_w_in, hg_norm_g=hg_norm_g, hg_w_out=hg_w_out, sg_w_in=sg_w_in,
             sg_ln_g=sg_ln_g, sg_ln_b=sg_ln_b, sg_w_s=sg_w_s, sg_b_s=sg_b_s, sg_w_out=sg_w_out, ffn_w_up=ffn_w_up,
             ffn_conv_w=ffn_conv_w, ffn_conv_b=ffn_conv_b, ffn_w_down=ffn_w_down, ln1_g=ln1_g, ln1_b=ln1_b,
             ln2_g=ln2_g, ln2_b=ln2_b)
    m = dict(lb_logits=m_lb_logits, hg_w_in=m_hg_w_in, hg_norm_g=m_hg_norm_g, hg_w_out=m_hg_w_out,
             sg_w_in=m_sg_w_in, sg_ln_g=m_sg_ln_g, sg_ln_b=m_sg_ln_b, sg_w_s=m_sg_w_s, sg_b_s=m_sg_b_s,
             sg_w_out=m_sg_w_out, ffn_w_up=m_ffn_w_up, ffn_conv_w=m_ffn_conv_w, ffn_conv_b=m_ffn_conv_b,
             ffn_w_down=m_ffn_w_down, ln1_g=m_ln1_g, ln1_b=m_ln1_b, ln2_g=m_ln2_g, ln2_b=m_ln2_b)
    v = dict(lb_logits=v_lb_logits, hg_w_in=v_hg_w_in, hg_norm_g=v_hg_norm_g, hg_w_out=v_hg_w_out,
             sg_w_in=v_sg_w_in, sg_ln_g=v_sg_ln_g, sg_ln_b=v_sg_ln_b, sg_w_s=v_sg_w_s, sg_b_s=v_sg_b_s,
             sg_w_out=v_sg_w_out, ffn_w_up=v_ffn_w_up, ffn_conv_w=v_ffn_conv_w, ffn_conv_b=v_ffn_conv_b,
             ffn_w_down=v_ffn_w_down, ln1_g=v_ln1_g, ln1_b=v_ln1_b, ln2_g=v_ln2_g, ln2_b=v_ln2_b)
    me = 4 * lax.axis_index("x") + 2 * lax.axis_index("y") + lax.axis_index("c")
    d = D_MODEL

    p = {}
    p["hg_w_in"] = _all_gather(hg_w_in[0].astype(BF16), "ag_hg_w_in")
    p["hg_w_out"] = _all_gather(hg_w_out[0].astype(BF16), "ag_hg_w_out").reshape(d, d)
    p["sg_w_in"] = _all_gather(sg_w_in[0].astype(BF16), "ag_sg_w_in")
    p["sg_w_out"] = _all_gather(sg_w_out[0].astype(BF16), "ag_sg_w_out").reshape(d, d)
    p["ffn_w_up"] = [_all_gather(ffn_w_up[l].astype(BF16), f"ag_ffn_w_up{l}") for l in range(DEPTH)]
    p["ffn_w_down"] = [_all_gather(ffn_w_down[l].astype(BF16), f"ag_ffn_w_down{l}").reshape(D_FF, d)
                       for l in range(DEPTH)]
    sv = jnp.zeros((8, 768), F32)
    sv = sv.at[0, :256].set(sg_ln_g[0]).at[1, :256].set(sg_ln_b[0]).at[2:8, :704].set(ffn_conv_w.reshape(6, 704))
    sv = _all_gather(sv, "ag_small")
    p["sg_ln_g"] = sv[:, 0, :256].reshape(1, d)
    p["sg_ln_b"] = sv[:, 1, :256].reshape(1, d)
    conv_w_full = jnp.transpose(sv[:, 2:8, :704].reshape(N_DEV, DEPTH, 3, 704), (1, 2, 0, 3)).reshape(DEPTH, 3, D_FF)
    p["ffn_conv_w"] = [conv_w_full[l] for l in range(DEPTH)]
    p["ffn_conv_b"] = [ffn_conv_b[l:l + 1] for l in range(DEPTH)]
    p["hg_norm_g"] = hg_norm_g
    p["sg_w_s"] = sg_w_s[0]
    p["sg_bias"] = jnp.broadcast_to(sg_b_s[0][:, :, None], (SG_GROUPS, SG_CHUNK, SG_DIM))
    for n in ("ln1_g", "ln1_b", "ln2_g", "ln2_b"):
        p[n] = [w[n][l:l + 1] for l in range(DEPTH)]

    loss_part, grad_x, g = _local_step(x[0], loss_target[0], lb_logits, p)
    loss = lax.psum(loss_part, ("x", "y", "c"))

    small = [g["lb_logits"], g["hg_norm_g"], g["sg_ln_g"], g["sg_ln_b"], g["sg_w_s"], g["sg_b_s"],
             g["conv_w0"], g["conv_w1"], g["conv_b0"], g["conv_b1"],
             g["ln1_g0"], g["ln1_g1"], g["ln1_b0"], g["ln1_b1"], g["ln2_g0"], g["ln2_g1"], g["ln2_b0"], g["ln2_b1"]]
    sizes = [int(np.prod(a.shape)) for a in small]
    flat = jnp.concatenate([a.reshape(-1) for a in small])
    total = flat.shape[0]
    rows = -(-total // 128)
    rows = -(-rows // 8) * 8
    flat = jnp.pad(flat, (0, rows * 128 - total)).reshape(rows, 128)
    red = _all_reduce_small(flat, "ar_small").reshape(-1)
    offs = np.cumsum([0] + sizes)
    r = [red[offs[i]:offs[i + 1]].reshape(small[i].shape) for i in range(len(small))]
    gs = {}
    gs["lb_logits"] = r[0]
    gs["hg_norm_g"] = r[1]
    gs["sg_ln_g"] = lax.dynamic_slice(r[2], (0, me * 256), (1, 256))
    gs["sg_ln_b"] = lax.dynamic_slice(r[3], (0, me * 256), (1, 256))
    gs["sg_w_s"] = r[4][None]
    gs["sg_b_s"] = r[5][None]
    gs["ffn_conv_w"] = lax.dynamic_slice(jnp.stack([r[6], r[7]]), (0, 0, me * 704), (DEPTH, 3, 704))
    gs["ffn_conv_b"] = jnp.concatenate([r[8], r[9]], axis=0)
    gs["ln1_g"] = jnp.concatenate([r[10], r[11]], axis=0)
    gs["ln1_b"] = jnp.concatenate([r[12], r[13]], axis=0)
    gs["ln2_g"] = jnp.concatenate([r[14], r[15]], axis=0)
    gs["ln2_b"] = jnp.concatenate([r[16], r[17]], axis=0)

    out_g, out_d, out_m, out_v = {}, {}, {}, {}
    for n in _SMALL:
        out_g[n], out_d[n], out_m[n], out_v[n] = _adamw(w[n], gs[n], m[n], v[n], f"adamw_{n}")

    def big(n, parts, layer=None):
        sel = (lambda a: a[layer]) if layer is not None else (lambda a: a[0])
        shard_shape = sel(w[n]).shape
        recv = _exchange_parts(parts.reshape((N_DEV,) + shard_shape), f"rs_{n}{'' if layer is None else layer}")
        return _adamw(sel(w[n]), None, sel(m[n]), sel(v[n]), f"adamw_{n}{'' if layer is None else layer}", parts=recv)

    for n in ("hg_w_in", "hg_w_out", "sg_w_in", "sg_w_out"):
        res = big(n, g[n])
        out_g[n], out_d[n], out_m[n], out_v[n] = (a[None] for a in res)
    for n in ("ffn_w_up", "ffn_w_down"):
        res = [big(n, g[f"{n}{l}"], layer=l) for l in range(DEPTH)]
        out_g[n], out_d[n], out_m[n], out_v[n] = (jnp.stack([res[0][i], res[1][i]]) for i in range(4))

    return (loss, grad_x[None], *[out_g[n] for n in _NAMES], *[out_d[n] for n in _NAMES],
            *[out_m[n] for n in _NAMES], *[out_v[n] for n in _NAMES])
```

```python
import functools

import numpy as np
import jax
import jax.numpy as jnp
from jax import lax
from jax.experimental import pallas as pl
from jax.experimental.pallas import tpu as pltpu

F32 = jnp.float32
BF16 = jnp.bfloat16
HI = lax.Precision.HIGHEST

N_DEV = 8
D_MODEL = 2048
HG_HEADS = 16
HG_DIM = 128
HG_CHUNK = 64
SG_GROUPS = 16
SG_DIM = 128
SG_CHUNK = 128
D_FF = 5632
DEPTH = 2
ALPHA = (2 * DEPTH) ** 0.25
LN_EPS = 1e-5
RMS_EPS = 1e-6
ADAM_LR = 0.001
ADAM_B1 = 0.9
ADAM_B2 = 0.999
ADAM_EPS = 1e-08
ADAM_WD = 0.01
ADAM_STEP = 10

VMEM_LIMIT = 56 * 1024 * 1024
MESH = pl.DeviceIdType.MESH
ANY = pl.BlockSpec(memory_space=pl.ANY)


def _cp(*sem):
    return pltpu.CompilerParams(dimension_semantics=sem, vmem_limit_bytes=VMEM_LIMIT)


def _pick(n, cands):
    for c in cands:
        if n % c == 0:
            return c
    raise ValueError(f"no tile for {n} in {cands}")


_DIMS = {"nn": (((1,), (0,)), ((), ())), "nt": (((1,), (1,)), ((), ())), "tn": (((0,), (0,)), ((), ()))}


def _mm(a, b, *, mode, name, out_dtype=F32, a_cs=False, b_cs=False, o_cs=None, addend=None,
        bm=None, bn=None, bk=None):
    if mode == "tn":
        kk, m = a.shape
    elif a_cs:
        m, kk = a.shape[1], a.shape[0] * a.shape[2]
    else:
        m, kk = a.shape
    if mode == "nt":
        n = b.shape[1] if b_cs else b.shape[0]
    else:
        n = b.shape[0] * b.shape[2] if b_cs else b.shape[1]
    a_c = a.shape[2] if a_cs else None
    b_c = b.shape[2] if b_cs else None
    o_c = n // o_cs if o_cs else None
    bm = bm or _pick(m, (1024, 1408, 512, 256))
    n_div = [c for c in (b_c if (b_cs and mode != "nt") else None, o_c) if c]
    k_div = [c for c in (a_c, b_c if (b_cs and mode == "nt") else None) if c]

    def pick_with(total, divs, cands):
        for c in cands:
            if total % c == 0 and all(d % c == 0 for d in divs):
                return c
        raise ValueError((total, divs))

    bn = bn or pick_with(n, n_div, (1024, 1408, 512, 256, 128))
    bk = bk or pick_with(kk, k_div, (512, 1408, 1024, 256, 128))
    nk = kk // bk

    def cs_idx(blk, per):
        return (blk * per[0]) // per[1], (blk * per[0] % per[1]) // per[0]

    if mode == "tn":
        a_spec = pl.BlockSpec((bk, bm), lambda i, j, k: (k, i))
    elif a_cs:
        def a_map(i, j, k):
            s, r = cs_idx(k, (bk, a_c))
            return (s, i, r)
        a_spec = pl.BlockSpec((None, bm, bk), a_map)
    else:
        a_spec = pl.BlockSpec((bm, bk), lambda i, j, k: (i, k))
    if mode == "nt":
        if b_cs:
            def b_map(i, j, k):
                s, r = cs_idx(k, (bk, b_c))
                return (s, j, r)
            b_spec = pl.BlockSpec((None, bn, bk), b_map)
        else:
            b_spec = pl.BlockSpec((bn, bk), lambda i, j, k: (j, k))
    else:
        if b_cs:
            def b_map(i, j, k):
                s, r = cs_idx(j, (bn, b_c))
                return (s, k, r)
            b_spec = pl.BlockSpec((None, bk, bn), b_map)
        else:
            b_spec = pl.BlockSpec((bk, bn), lambda i, j, k: (k, j))
    if o_cs:
        def o_map(i, j, k):
            s, r = cs_idx(j, (bn, o_c))
            return (s, i, r)
        o_spec = pl.BlockSpec((None, bm, bn), o_map)
        out_shape = jax.ShapeDtypeStruct((o_cs, m, o_c), out_dtype)
    else:
        o_spec = pl.BlockSpec((bm, bn), lambda i, j, k: (i, j))
        out_shape = jax.ShapeDtypeStruct((m, n), out_dtype)
    in_specs = [a_spec, b_spec]
    args = [a, b]
    if addend is not None:
        in_specs.append(pl.BlockSpec((bm, bn), lambda i, j, k: (i, j)))
        args.append(addend)
    dims = _DIMS[mode]

    def body(*refs):
        if addend is not None:
            a_ref, b_ref, add_ref, o_ref, acc_ref = refs
        else:
            a_ref, b_ref, o_ref, acc_ref = refs
        k = pl.program_id(2)

        @pl.when(k == 0)
        def _():
            acc_ref[...] = jnp.zeros_like(acc_ref)

        acc_ref[...] += lax.dot_general(a_ref[...].astype(BF16), b_ref[...].astype(BF16), dims,
                                        preferred_element_type=F32)

        @pl.when(k == nk - 1)
        def _():
            r = acc_ref[...]
            if addend is not None:
                r = r + add_ref[...]
            o_ref[...] = r.astype(o_ref.dtype)

    return pl.pallas_call(
        body, name=name, out_shape=out_shape, grid=(m // bm, n // bn, nk),
        in_specs=in_specs, out_specs=o_spec, scratch_shapes=[pltpu.VMEM((bm, bn), F32)],
        compiler_params=_cp("parallel", "parallel", "arbitrary"),
    )(*args)


LN_ROWS = 256


def _ln_stats(s):
    mu = jnp.mean(s, axis=-1, keepdims=True)
    sc = s - mu
    var = jnp.mean(sc * sc, axis=-1, keepdims=True)
    rstd = lax.rsqrt(var + LN_EPS)
    return sc * rstd, rstd


def _ln_fwd(h, sub, g, b, name):
    t, d = h.shape
    row = pl.BlockSpec((LN_ROWS, d), lambda i: (i, 0))
    vec = pl.BlockSpec((1, d), lambda i: (0, 0))

    def body(h_ref, s_ref, g_ref, b_ref, y_ref, yb_ref):
        xhat, _ = _ln_stats(ALPHA * h_ref[...] + s_ref[...])
        y = xhat * g_ref[...] + b_ref[...]
        y_ref[...] = y
        yb_ref[...] = y.astype(BF16)

    return pl.pallas_call(
        body, name=name, grid=(t // LN_ROWS,),
        out_shape=(jax.ShapeDtypeStruct((t, d), F32), jax.ShapeDtypeStruct((t, d), BF16)),
        in_specs=[row, row, vec, vec], out_specs=(row, row), compiler_params=_cp("parallel"),
    )(h, sub, g, b)


def _ln_bwd_math(xhat, rstd, dy, g):
    dxhat = dy * g
    m1 = jnp.mean(dxhat, axis=-1, keepdims=True)
    m2 = jnp.mean(dxhat * xhat, axis=-1, keepdims=True)
    ds = rstd * (dxhat - m1 - xhat * m2)
    dg = jnp.sum(dy * xhat, axis=0, keepdims=True)
    db = jnp.sum(dy, axis=0, keepdims=True)
    return ds, dg, db


def _ln_bwd(h, sub, dy_a, dy_b, g, name):
    t, d = h.shape
    row = pl.BlockSpec((LN_ROWS, d), lambda i: (i, 0))
    vec = pl.BlockSpec((1, d), lambda i: (0, 0))

    def body(h_ref, s_ref, da_ref, db_ref, g_ref, ds_ref, dres_ref, dg_ref, dbeta_ref):
        xhat, rstd = _ln_stats(ALPHA * h_ref[...] + s_ref[...])
        ds, dg, db = _ln_bwd_math(xhat, rstd, da_ref[...] + db_ref[...], g_ref[...])
        ds_ref[...] = ds.astype(BF16)
        dres_ref[...] = ALPHA * ds

        @pl.when(pl.program_id(0) == 0)
        def _():
            dg_ref[...] = jnp.zeros_like(dg_ref)
            dbeta_ref[...] = jnp.zeros_like(dbeta_ref)

        dg_ref[...] += dg
        dbeta_ref[...] += db

    return pl.pallas_call(
        body, name=name, grid=(t // LN_ROWS,),
        out_shape=(jax.ShapeDtypeStruct((t, d), BF16), jax.ShapeDtypeStruct((t, d), F32),
                   jax.ShapeDtypeStruct((1, d), F32), jax.ShapeDtypeStruct((1, d), F32)),
        in_specs=[row, row, row, row, vec], out_specs=(row, row, vec, vec),
        compiler_params=_cp("arbitrary"),
    )(h, sub, dy_a, dy_b, g)


def _ln_loss_bwd(h, sub, target, g, b, name):
    t, d = h.shape
    row = pl.BlockSpec((LN_ROWS, d), lambda i: (i, 0))
    vec = pl.BlockSpec((1, d), lambda i: (0, 0))
    lvec = pl.BlockSpec((1, 128), lambda i: (0, 0))

    def body(h_ref, s_ref, t_ref, g_ref, b_ref, loss_ref, ds_ref, dres_ref, dg_ref, dbeta_ref):
        xhat, rstd = _ln_stats(ALPHA * h_ref[...] + s_ref[...])
        y = xhat * g_ref[...] + b_ref[...]
        err = y - t_ref[...]
        part = 0.5 * jnp.sum(jnp.mean(err * err, axis=-1, keepdims=True), axis=0, keepdims=True)
        ds, dg, db = _ln_bwd_math(xhat, rstd, err * (1.0 / d), g_ref[...])
        ds_ref[...] = ds.astype(BF16)
        dres_ref[...] = ALPHA * ds

        @pl.when(pl.program_id(0) == 0)
        def _():
            loss_ref[...] = jnp.zeros_like(loss_ref)
            dg_ref[...] = jnp.zeros_like(dg_ref)
            dbeta_ref[...] = jnp.zeros_like(dbeta_ref)

        loss_ref[...] += jnp.broadcast_to(part, loss_ref.shape)
        dg_ref[...] += dg
        dbeta_ref[...] += db

    return pl.pallas_call(
        body, name=name, grid=(t // LN_ROWS,),
        out_shape=(jax.ShapeDtypeStruct((1, 128), F32), jax.ShapeDtypeStruct((t, d), BF16),
                   jax.ShapeDtypeStruct((t, d), F32), jax.ShapeDtypeStruct((1, d), F32),
                   jax.ShapeDtypeStruct((1, d), F32)),
        in_specs=[row, row, row, vec, vec], out_specs=(lvec, row, row, vec, vec),
        compiler_params=_cp("arbitrary"),
    )(h, sub, target, g, b)


HG_ROWS = 512
HG_LEVELS = 6


def _hg_constants():
    c = HG_CHUNK
    t = np.arange(c)[:, None]
    j = np.arange(c)[None, :]
    blocks, masks = [], [np.eye(c)]
    for lev in range(HG_LEVELS):
        m = 1 << lev
        second = (t % (2 * m)) >= m
        mid = (t // (2 * m)) * (2 * m) + m - 1
        blocks.append((second & (j > mid) & (j <= t)) | ((~second) & (j > t) & (j <= mid)))
        same = (t // (2 * m)) == (j // (2 * m))
        masks.append(same & second & ((j % (2 * m)) < m))
    blocks += [j <= t, j > t, np.ones((c, c), bool)]
    cm = np.concatenate(blocks, axis=0).astype(np.float32)
    mk = np.concatenate(masks, axis=0).astype(np.float32)
    return jnp.asarray(cm, dtype=BF16), jnp.asarray(mk)


def _split3(x):
    hi = x.astype(BF16)
    r = x - hi.astype(F32)
    mid = r.astype(BF16)
    lo = (r - mid.astype(F32)).astype(BF16)
    return hi, mid, lo


def _sum3(x):
    n = x.shape[-1] // 3
    return x[:, :n] + x[:, n:2 * n] + x[:, 2 * n:]


@jax.custom_vjp
def _prefix(cm, lf):
    return _sum3(lax.dot_general(cm, jnp.concatenate(_split3(lf), axis=1), _DIMS["nn"], preferred_element_type=F32))


def _prefix_fwd(cm, lf):
    return _prefix(cm, lf), cm


def _prefix_bwd(cm, d):
    d_lf = _sum3(lax.dot_general(cm, jnp.concatenate(_split3(d), axis=1), _DIMS["tn"], preferred_element_type=F32))
    return jnp.zeros_like(cm), d_lf


_prefix.defvjp(_prefix_fwd, _prefix_bwd)

_BWD = {"nn": (("nt", 0, 1), ("tn", 1, 0)), "nt": (("nn", 0, 1), ("tn", 0, 1)), "tn": (("nt", 1, 0), ("nn", 1, 0))}


def _bdot_raw(a, b, mode):
    return lax.dot_general(a.astype(BF16), b.astype(BF16), _DIMS[mode], preferred_element_type=F32)


@functools.partial(jax.custom_vjp, nondiff_argnums=(2,))
def _bdot(a, b, mode):
    return _bdot_raw(a, b, mode)


def _bdot_fwd(a, b, mode):
    return _bdot_raw(a, b, mode), (a, b)


def _bdot_bwd(mode, res, d):
    a, b = res
    (ma, da_pos, _), (mb, db_pos, _) = _BWD[mode]
    da = _bdot_raw(d, b, ma) if da_pos == 0 else _bdot_raw(b, d, ma)
    db = _bdot_raw(d, a, mb) if db_pos == 0 else _bdot_raw(a, d, mb)
    return da, db


_bdot.defvjp(_bdot_fwd, _bdot_bwd)


def _hg_chunk(pq, pf, pi, pg, l0, l1, l2, ng, s_t, cm, mk):
    c = HG_CHUNK
    mx = jnp.maximum(jnp.maximum(l0, l1), l2)
    e0, e1, e2 = jnp.exp(l0 - mx), jnp.exp(l1 - mx), jnp.exp(l2 - mx)
    lb = e0 / (e0 + e1 + e2)
    q = pq * jax.nn.sigmoid(pq)
    t1 = jnp.log(lb)
    t2 = jnp.log1p(-lb) + jax.nn.log_sigmoid(pf)
    lf = jnp.maximum(t1, t2) + jnp.log1p(jnp.exp(-jnp.abs(t1 - t2)))
    k = (1.0 - lb) * jax.nn.sigmoid(-pf)
    x = jnp.exp(_prefix(cm, lf))
    scores = mk[0:c] * _bdot(q, k, "nt")
    for lev in range(HG_LEVELS):
        xl = x[lev * c:(lev + 1) * c]
        scores = scores + mk[(lev + 1) * c:(lev + 2) * c] * _bdot(q * xl, k * xl, "nt")
    x_incl = x[HG_LEVELS * c:(HG_LEVELS + 1) * c]
    x_after = x[(HG_LEVELS + 1) * c:(HG_LEVELS + 2) * c]
    x_total = x[(HG_LEVELS + 2) * c:(HG_LEVELS + 3) * c]
    o = _bdot(scores, pi, "nn") + _bdot(q * x_incl, s_t, "nt")
    s_new = s_t * jnp.concatenate([x_total, x_total], axis=0) + _bdot(pi, k * x_after, "tn")
    rstd = lax.rsqrt(jnp.mean(o * o, axis=-1, keepdims=True) + RMS_EPS)
    y = o * rstd * ng * (pg * jax.nn.sigmoid(pg))
    return y, s_new


def _hg_specs(t, reverse):
    nrb = t // HG_ROWS
    rb = (lambda r: nrb - 1 - r) if reverse else (lambda r: r)
    proj = pl.BlockSpec((4, HG_ROWS, HG_DIM), lambda h, r: (0, rb(r), h))
    vec = pl.BlockSpec((1, HG_DIM), lambda h, r: (0, h))
    cm = pl.BlockSpec(((HG_LEVELS + 3) * HG_CHUNK, HG_CHUNK), lambda h, r: (0, 0))
    mk = pl.BlockSpec(((HG_LEVELS + 1) * HG_CHUNK, HG_CHUNK), lambda h, r: (0, 0))
    rows = pl.BlockSpec((HG_ROWS, HG_DIM), lambda h, r: (rb(r), h))
    states = pl.BlockSpec((None, HG_ROWS // HG_CHUNK, HG_DIM, HG_DIM), lambda h, r: (h, rb(r), 0, 0))
    return proj, vec, cm, mk, rows, states


def _hg_fwd(proj, l0, l1, l2, ng):
    t = proj.shape[1]
    n_in = HG_ROWS // HG_CHUNK
    cm, mk = _hg_constants()
    p_spec, vec, cm_spec, mk_spec, rows, st_spec = _hg_specs(t, False)

    def body(p_ref, l0_ref, l1_ref, l2_ref, ng_ref, cm_ref, mk_ref, y_ref, st_ref, s_ref):
        @pl.when(pl.program_id(1) == 0)
        def _():
            s_ref[...] = jnp.zeros_like(s_ref)

        def step(ci, carry):
            sl = pl.ds(pl.multiple_of(ci * HG_CHUNK, HG_CHUNK), HG_CHUNK)
            s_t = s_ref[...]
            st_ref[ci] = s_t
            y, s_new = _hg_chunk(p_ref[0, sl, :], p_ref[1, sl, :], p_ref[2, sl, :], p_ref[3, sl, :],
                                 l0_ref[...], l1_ref[...], l2_ref[...], ng_ref[...], s_t,
                                 cm_ref[...], mk_ref[...])
            y_ref[sl, :] = y.astype(BF16)
            s_ref[...] = s_new
            return carry

        lax.fori_loop(0, n_in, step, 0, unroll=8)

    return pl.pallas_call(
        body, name="hg_fwd", grid=(HG_HEADS, t // HG_ROWS),
        out_shape=(jax.ShapeDtypeStruct((t, D_MODEL), BF16),
                   jax.ShapeDtypeStruct((HG_HEADS, t // HG_CHUNK, HG_DIM, HG_DIM), F32)),
        in_specs=[p_spec, vec, vec, vec, vec, cm_spec, mk_spec], out_specs=(rows, st_spec),
        scratch_shapes=[pltpu.VMEM((HG_DIM, HG_DIM), F32)],
        compiler_params=_cp("parallel", "arbitrary"),
    )(proj, l0, l1, l2, ng, cm, mk)


def _hg_bwd(proj, states, dy, l0, l1, l2, ng):
    t = proj.shape[1]
    n_in = HG_ROWS // HG_CHUNK
    cm, mk = _hg_constants()
    p_spec, vec, cm_spec, mk_spec, rows, st_spec = _hg_specs(t, True)

    def body(p_ref, st_ref, dy_ref, l0_ref, l1_ref, l2_ref, ng_ref, cm_ref, mk_ref,
             dp_ref, dl0_ref, dl1_ref, dl2_ref, dng_ref, ds_ref):
        @pl.when(pl.program_id(1) == 0)
        def _():
            ds_ref[...] = jnp.zeros_like(ds_ref)
            for r in (dl0_ref, dl1_ref, dl2_ref, dng_ref):
                r[...] = jnp.zeros_like(r)

        def step(it, carry):
            ci = n_in - 1 - it
            sl = pl.ds(pl.multiple_of(ci * HG_CHUNK, HG_CHUNK), HG_CHUNK)
            fn = functools.partial(_hg_chunk, cm=cm_ref[...], mk=mk_ref[...])
            _, vjp = jax.vjp(fn, p_ref[0, sl, :], p_ref[1, sl, :], p_ref[2, sl, :], p_ref[3, sl, :],
                             l0_ref[...], l1_ref[...], l2_ref[...], ng_ref[...], st_ref[ci])
            dq, df, di, dg, d0, d1, d2, dn, ds = vjp((dy_ref[sl, :], ds_ref[...]))
            dp_ref[0, sl, :] = dq.astype(BF16)
            dp_ref[1, sl, :] = df.astype(BF16)
            dp_ref[2, sl, :] = di.astype(BF16)
            dp_ref[3, sl, :] = dg.astype(BF16)
            dl0_ref[...] += d0
            dl1_ref[...] += d1
            dl2_ref[...] += d2
            dng_ref[...] += dn
            ds_ref[...] = ds
            return carry

        lax.fori_loop(0, n_in, step, 0, unroll=8)

    v_shape = jax.ShapeDtypeStruct((1, D_MODEL), F32)
    return pl.pallas_call(
        body, name="hg_bwd", grid=(HG_HEADS, t // HG_ROWS),
        out_shape=(jax.ShapeDtypeStruct((4, t, D_MODEL), BF16), v_shape, v_shape, v_shape, v_shape),
        in_specs=[p_spec, st_spec, rows, vec, vec, vec, vec, cm_spec, mk_spec],
        out_specs=(p_spec, vec, vec, vec, vec),
        scratch_shapes=[pltpu.VMEM((HG_DIM, HG_DIM), F32)],
        compiler_params=_cp("parallel", "arbitrary"),
    )(proj, states, dy, l0, l1, l2, ng, cm, mk)


_SQRT_HALF = 0.7071067811865476
_INV_SQRT_2PI = 0.3989422804014327


def _gelu(x):
    return 0.5 * x * (1.0 + lax.erf(x * _SQRT_HALF))


def _gelu_grad(x):
    return 0.5 * (1.0 + lax.erf(x * _SQRT_HALF)) + x * (_INV_SQRT_2PI * jnp.exp(-0.5 * x * x))


def _tril(n):
    return (lax.broadcasted_iota(jnp.int32, (n, n), 0) >= lax.broadcasted_iota(jnp.int32, (n, n), 1)).astype(F32)


def _sg_specs(d):
    row = lambda w: pl.BlockSpec((SG_CHUNK, w), lambda i: (i, 0))
    vec = pl.BlockSpec((1, d), lambda i: (0, 0))
    cube = pl.BlockSpec((SG_GROUPS, SG_CHUNK, SG_CHUNK), lambda i: (0, 0, 0))
    return row, vec, cube


def _sg_fwd(pre, ln_g, ln_b, w_s, bias):
    t = pre.shape[0]
    d = D_MODEL
    row, vec, cube = _sg_specs(d)

    def body(pre_ref, g_ref, b_ref, w_ref, bias_ref, y_ref, vln_ref):
        u = _gelu(pre_ref[:, :d])
        vhat, _ = _ln_stats(_gelu(pre_ref[:, d:]))
        vln_ref[...] = vhat * g_ref[...] + b_ref[...]
        tril = _tril(SG_CHUNK)
        for g in range(SG_GROUPS):
            cs = slice(g * SG_DIM, (g + 1) * SG_DIM)
            gate = _bdot_raw(w_ref[g] * tril, vln_ref[:, cs], "nn") + bias_ref[g]
            y_ref[:, cs] = (u[:, cs] * gate).astype(BF16)

    return pl.pallas_call(
        body, name="sg_fwd", grid=(t // SG_CHUNK,), out_shape=jax.ShapeDtypeStruct((t, d), BF16),
        in_specs=[row(2 * d), vec, vec, cube, cube], out_specs=row(d),
        scratch_shapes=[pltpu.VMEM((SG_CHUNK, d), F32)], compiler_params=_cp("parallel"),
    )(pre, ln_g, ln_b, w_s, bias)


def _sg_bwd(pre, dy, ln_g, ln_b, w_s, bias):
    t = pre.shape[0]
    d = D_MODEL
    row, vec, cube = _sg_specs(d)
    dbs_spec = pl.BlockSpec((SG_GROUPS, 8, SG_CHUNK), lambda i: (0, 0, 0))

    def body(pre_ref, dy_ref, g_ref, b_ref, w_ref, bias_ref, dpre_ref, dw_ref, dbs_ref, dlg_ref, dlb_ref,
             vln_ref, dvln_ref):
        @pl.when(pl.program_id(0) == 0)
        def _():
            for r in (dw_ref, dbs_ref, dlg_ref, dlb_ref):
                r[...] = jnp.zeros_like(r)

        pu = pre_ref[:, :d]
        pv = pre_ref[:, d:]
        u = _gelu(pu)
        vhat, rstd = _ln_stats(_gelu(pv))
        vln_ref[...] = vhat * g_ref[...] + b_ref[...]
        tril = _tril(SG_CHUNK)
        ones = jnp.ones((8, SG_DIM), F32)
        for g in range(SG_GROUPS):
            cs = slice(g * SG_DIM, (g + 1) * SG_DIM)
            wc = w_ref[g] * tril
            vg = vln_ref[:, cs]
            gate = _bdot_raw(wc, vg, "nn") + bias_ref[g]
            dyg = dy_ref[:, cs]
            dgate = dyg * u[:, cs]
            dpre_ref[:, cs] = (dyg * gate * _gelu_grad(pu[:, cs])).astype(BF16)
            dw_ref[g] += tril * _bdot_raw(dgate, vg, "nt")
            dbs_ref[g] += lax.dot_general(ones, dgate, _DIMS["nt"], precision=HI, preferred_element_type=F32)
            dvln_ref[:, cs] = _bdot_raw(wc, dgate, "tn")
        dvln = dvln_ref[...]
        dv, dg, db = _ln_bwd_math(vhat, rstd, dvln, g_ref[...])
        dlg_ref[...] += dg
        dlb_ref[...] += db
        dpre_ref[:, d:] = (dv * _gelu_grad(pv)).astype(BF16)

    return pl.pallas_call(
        body, name="sg_bwd", grid=(t // SG_CHUNK,),
        out_shape=(jax.ShapeDtypeStruct((t, 2 * d), BF16), jax.ShapeDtypeStruct(w_s.shape, F32),
                   jax.ShapeDtypeStruct((SG_GROUPS, 8, SG_CHUNK), F32),
                   jax.ShapeDtypeStruct((1, d), F32), jax.ShapeDtypeStruct((1, d), F32)),
        in_specs=[row(2 * d), row(d), vec, vec, cube, cube],
        out_specs=(row(2 * d), cube, dbs_spec, vec, vec),
        scratch_shapes=[pltpu.VMEM((SG_CHUNK, d), F32), pltpu.VMEM((SG_CHUNK, d), F32)],
        compiler_params=_cp("arbitrary"),
    )(pre, dy, ln_g, ln_b, w_s, bias)


FFN_ROWS = 256
FFN_COLS = 1408
HALO = 8


def _ffn_conv(a_ext, w_ref, cb_ref):
    a1 = pltpu.roll(a_ext, 1, 0)
    a2 = pltpu.roll(a_ext, 2, 0)
    return w_ref[0:1, :] * a2 + w_ref[1:2, :] * a1 + w_ref[2:3, :] * a_ext + cb_ref[...], a1, a2


def _ffn_fwd(hh, conv_w, conv_b):
    t = hh.shape[1]
    nb8 = FFN_ROWS // HALO
    main = pl.BlockSpec((2, FFN_ROWS, FFN_COLS), lambda c, r: (0, r, c))
    prev = pl.BlockSpec((None, HALO, FFN_COLS), lambda c, r: (0, jnp.maximum(r * nb8 - 1, 0), c))
    wspec = pl.BlockSpec((3, FFN_COLS), lambda c, r: (0, c))
    bspec = pl.BlockSpec((1, FFN_COLS), lambda c, r: (0, c))

    def body(m_ref, p_ref, w_ref, cb_ref, z_ref):
        prev = jnp.where(pl.program_id(1) == 0, 0.0, p_ref[...])
        a_ext = jnp.concatenate([prev, m_ref[0]], axis=0)
        a, _, _ = _ffn_conv(a_ext, w_ref, cb_ref)
        a = a[HALO:]
        z_ref[...] = (a * jax.nn.sigmoid(a) * m_ref[1]).astype(BF16)

    return pl.pallas_call(
        body, name="ffn_fwd", grid=(D_FF // FFN_COLS, t // FFN_ROWS),
        out_shape=jax.ShapeDtypeStruct((t, D_FF), BF16),
        in_specs=[main, prev, wspec, bspec],
        out_specs=pl.BlockSpec((FFN_ROWS, FFN_COLS), lambda c, r: (r, c)),
        compiler_params=_cp("parallel", "parallel"),
    )(hh, hh, conv_w, conv_b)


def _ffn_bwd(hh, dz, conv_w, conv_b):
    t = hh.shape[1]
    nb8 = FFN_ROWS // HALO
    last8 = t // HALO - 1
    nr = t // FFN_ROWS
    main = pl.BlockSpec((2, FFN_ROWS, FFN_COLS), lambda c, r: (0, r, c))
    prev = pl.BlockSpec((None, HALO, FFN_COLS), lambda c, r: (0, jnp.maximum(r * nb8 - 1, 0), c))
    nxt = pl.BlockSpec((2, HALO, FFN_COLS), lambda c, r: (0, jnp.minimum((r + 1) * nb8, last8), c))
    dmain = pl.BlockSpec((FFN_ROWS, FFN_COLS), lambda c, r: (r, c))
    dnxt = pl.BlockSpec((HALO, FFN_COLS), lambda c, r: (jnp.minimum((r + 1) * nb8, last8), c))
    wspec = pl.BlockSpec((3, FFN_COLS), lambda c, r: (0, c))
    bspec = pl.BlockSpec((1, FFN_COLS), lambda c, r: (0, c))

    def body(m_ref, p_ref, n_ref, dz_ref, dzn_ref, w_ref, cb_ref, dh_ref, dw_ref, dcb_ref):
        r = pl.program_id(1)

        @pl.when(r == 0)
        def _():
            dw_ref[...] = jnp.zeros_like(dw_ref)
            dcb_ref[...] = jnp.zeros_like(dcb_ref)

        prev = jnp.where(r == 0, 0.0, p_ref[...])
        a_ext = jnp.concatenate([prev, m_ref[0], n_ref[0]], axis=0)
        a, a1, a2 = _ffn_conv(a_ext, w_ref, cb_ref)
        a = a[HALO:]
        b_ext = jnp.concatenate([m_ref[1], n_ref[1]], axis=0)
        dz_ext = jnp.concatenate([dz_ref[...], jnp.where(r == nr - 1, 0.0, dzn_ref[...])], axis=0)
        sig = jax.nn.sigmoid(a)
        da = dz_ext * b_ext * (sig * (1.0 + a * (1.0 - sig)))
        n_ext = FFN_ROWS + HALO
        da_p1 = pltpu.roll(da, n_ext - 1, 0)
        da_p2 = pltpu.roll(da, n_ext - 2, 0)
        da_raw = w_ref[2:3, :] * da + w_ref[1:2, :] * da_p1 + w_ref[0:1, :] * da_p2
        dh_ref[0] = da_raw[:FFN_ROWS].astype(BF16)
        dh_ref[1] = (dz_ref[...] * (a * sig)[:FFN_ROWS]).astype(BF16)
        dam = da[:FFN_ROWS]
        rows = slice(HALO, HALO + FFN_ROWS)
        dw_ref[0:1, :] += jnp.sum(dam * a2[rows], axis=0, keepdims=True)
        dw_ref[1:2, :] += jnp.sum(dam * a1[rows], axis=0, keepdims=True)
        dw_ref[2:3, :] += jnp.sum(dam * a_ext[rows], axis=0, keepdims=True)
        dcb_ref[...] += jnp.sum(dam, axis=0, keepdims=True)

    return pl.pallas_call(
        body, name="ffn_bwd", grid=(D_FF // FFN_COLS, nr),
        out_shape=(jax.ShapeDtypeStruct((2, t, D_FF), BF16), jax.ShapeDtypeStruct((3, D_FF), F32),
                   jax.ShapeDtypeStruct((1, D_FF), F32)),
        in_specs=[main, prev, nxt, dmain, dnxt, wspec, bspec],
        out_specs=(main, wspec, bspec), compiler_params=_cp("parallel", "arbitrary"),
    )(hh, hh, hh, dz, dz, conv_w, conv_b)


def _adamw_math(w, g, m, v):
    m = ADAM_B1 * m + (1.0 - ADAM_B1) * g
    v = ADAM_B2 * v + (1.0 - ADAM_B2) * (g * g)
    m_hat = m / (1.0 - ADAM_B1 ** ADAM_STEP)
    v_hat = v / (1.0 - ADAM_B2 ** ADAM_STEP)
    delta = -ADAM_LR * (m_hat / (jnp.sqrt(v_hat) + ADAM_EPS) + ADAM_WD * w)
    return delta, m, v


def _as2d(shape):
    n = int(np.prod(shape))
    c = shape[-1] if shape[-1] % 128 == 0 else (128 if n % 128 == 0 else shape[-1])
    return n // c, c


def _adamw(w, g, m, v, name, parts=None):
    shape = w.shape
    r, c = _as2d(shape)
    br = r if r <= 512 else _pick(r, (256, 176, 128, 64, 8))
    blk = pl.BlockSpec((br, c), lambda i: (i, 0))
    w2, m2, v2 = (a.reshape(r, c) for a in (w, m, v))
    if parts is not None:
        g_in = parts.reshape(N_DEV, r, c)
        g_spec = pl.BlockSpec((N_DEV, br, c), lambda i: (0, i, 0))
    else:
        g_in = g.reshape(r, c)
        g_spec = blk

    def body(w_ref, g_ref, m_ref, v_ref, go_ref, d_ref, mo_ref, vo_ref):
        if parts is not None:
            gsum = g_ref[0].astype(F32)
            for i in range(1, N_DEV):
                gsum = gsum + g_ref[i].astype(F32)
        else:
            gsum = g_ref[...]
        delta, m_new, v_new = _adamw_math(w_ref[...], gsum, m_ref[...], v_ref[...])
        go_ref[...] = gsum
        d_ref[...] = delta
        mo_ref[...] = m_new
        vo_ref[...] = v_new

    o = jax.ShapeDtypeStruct((r, c), F32)
    outs = pl.pallas_call(
        body, name=name, grid=(r // br,), out_shape=(o, o, o, o),
        in_specs=[blk, g_spec, blk, blk], out_specs=(blk, blk, blk, blk), compiler_params=_cp("parallel"),
    )(w2, g_in, m2, v2)
    return tuple(a.reshape(shape) for a in outs)


def _me_and_peers():
    x, y, c = lax.axis_index("x"), lax.axis_index("y"), lax.axis_index("c")
    me = 4 * x + 2 * y + c
    peers = []
    for k in range(1, N_DEV):
        kx, ky, kc = (k >> 2) & 1, (k >> 1) & 1, k & 1
        px, py, pc = x ^ kx, y ^ ky, c ^ kc
        peers.append(((px, py, pc), 4 * px + 2 * py + pc))
    return me, peers


def _all_gather(shard, name):
    def body(x_ref, o_ref, send_sems, recv_sems, local_sem):
        me, peers = _me_and_peers()
        mine = pltpu.make_async_copy(x_ref, o_ref.at[me], local_sem)
        mine.start()
        sends = []
        for k, (dev, _) in enumerate(peers):
            cp = pltpu.make_async_remote_copy(src_ref=x_ref, dst_ref=o_ref.at[me], send_sem=send_sems.at[k],
                                              recv_sem=recv_sems.at[k], device_id=dev, device_id_type=MESH)
            cp.start()
            sends.append(cp)
        for k, (dev, idx) in enumerate(peers):
            pltpu.make_async_remote_copy(src_ref=x_ref, dst_ref=o_ref.at[idx], send_sem=send_sems.at[k],
                                         recv_sem=recv_sems.at[k], device_id=dev, device_id_type=MESH).wait_recv()
        for cp in sends:
            cp.wait_send()
        mine.wait()

    return pl.pallas_call(
        body, name=name, out_shape=jax.ShapeDtypeStruct((N_DEV,) + shard.shape, shard.dtype),
        in_specs=[ANY], out_specs=ANY,
        scratch_shapes=[pltpu.SemaphoreType.DMA((N_DEV - 1,)), pltpu.SemaphoreType.DMA((N_DEV - 1,)),
                        pltpu.SemaphoreType.DMA],
        compiler_params=pltpu.CompilerParams(has_side_effects=True),
    )(shard)


def _exchange_parts(parts, name):
    def body(x_ref, o_ref, send_sems, recv_sems, local_sem):
        me, peers = _me_and_peers()
        mine = pltpu.make_async_copy(x_ref.at[me], o_ref.at[me], local_sem)
        mine.start()
        sends = []
        for k, (dev, idx) in enumerate(peers):
            cp = pltpu.make_async_remote_copy(src_ref=x_ref.at[idx], dst_ref=o_ref.at[me], send_sem=send_sems.at[k],
                                              recv_sem=recv_sems.at[k], device_id=dev, device_id_type=MESH)
            cp.start()
            sends.append(cp)
        for k, (dev, idx) in enumerate(peers):
            pltpu.make_async_remote_copy(src_ref=x_ref.at[idx], dst_ref=o_ref.at[idx], send_sem=send_sems.at[k],
                                         recv_sem=recv_sems.at[k], device_id=dev, device_id_type=MESH).wait_recv()
        for cp in sends:
            cp.wait_send()
        mine.wait()

    return pl.pallas_call(
        body, name=name, out_shape=jax.ShapeDtypeStruct(parts.shape, parts.dtype),
        in_specs=[ANY], out_specs=ANY,
        scratch_shapes=[pltpu.SemaphoreType.DMA((N_DEV - 1,)), pltpu.SemaphoreType.DMA((N_DEV - 1,)),
                        pltpu.SemaphoreType.DMA],
        compiler_params=pltpu.CompilerParams(has_side_effects=True),
    )(parts)


def _all_reduce_small(v, name):
    r, c = v.shape

    def body(x_ref, o_ref, slots, send_sems, recv_sems):
        me, peers = _me_and_peers()
        sends = []
        for k, (dev, _) in enumerate(peers):
            cp = pltpu.make_async_remote_copy(src_ref=x_ref, dst_ref=slots.at[me], send_sem=send_sems.at[k],
                                              recv_sem=recv_sems.at[k], device_id=dev, device_id_type=MESH)
            cp.start()
            sends.append(cp)
        slots[me] = x_ref[...]
        for k, (dev, idx) in enumerate(peers):
            pltpu.make_async_remote_copy(src_ref=x_ref, dst_ref=slots.at[idx], send_sem=send_sems.at[k],
                                         recv_sem=recv_sems.at[k], device_id=dev, device_id_type=MESH).wait_recv()
        for cp in sends:
            cp.wait_send()
        acc = slots[0]
        for i in range(1, N_DEV):
            acc = acc + slots[i]
        o_ref[...] = acc

    return pl.pallas_call(
        body, name=name, out_shape=jax.ShapeDtypeStruct((r, c), F32),
        in_specs=[pl.BlockSpec(memory_space=pltpu.VMEM)], out_specs=pl.BlockSpec(memory_space=pltpu.VMEM),
        scratch_shapes=[pltpu.VMEM((N_DEV, r, c), F32), pltpu.SemaphoreType.DMA((N_DEV - 1,)),
                        pltpu.SemaphoreType.DMA((N_DEV - 1,))],
        compiler_params=pltpu.CompilerParams(has_side_effects=True, vmem_limit_bytes=VMEM_LIMIT),
    )(v)


def _local_step(x, target, lb_logits, p):
    g = {}
    l0, l1, l2 = lb_logits[0:1], lb_logits[1:2], lb_logits[2:3]
    x_bf = x.astype(BF16)

    proj0 = _mm(x_bf, p["hg_w_in"], mode="nn", b_cs=True, o_cs=4, name="mm_hg_in")
    y0, states = _hg_fwd(proj0, l0, l1, l2, p["hg_norm_g"])
    mixed0 = _mm(y0, p["hg_w_out"], mode="nn", name="mm_hg_out")
    h = [x]
    hb = [x_bf]
    subs = [mixed0]
    hh_l, z_l = [], []

    def ffn_forward(layer, h_in_bf):
        hh = _mm(h_in_bf, p["ffn_w_up"][layer], mode="nn", b_cs=True, o_cs=2, name=f"mm_up{layer}")
        z = _ffn_fwd(hh, p["ffn_conv_w"][layer], p["ffn_conv_b"][layer])
        ffn = _mm(z, p["ffn_w_down"][layer], mode="nn", name=f"mm_down{layer}")
        hh_l.append(hh)
        z_l.append(z)
        return ffn

    h1, h1b = _ln_fwd(x, mixed0, p["ln1_g"][0], p["ln1_b"][0], "ln1_0")
    ffn0 = ffn_forward(0, h1b)
    h2, h2b = _ln_fwd(h1, ffn0, p["ln2_g"][0], p["ln2_b"][0], "ln2_0")
    pre1 = _mm(h2b, p["sg_w_in"], mode="nn", b_cs=True, name="mm_sg_in")
    y1 = _sg_fwd(pre1, p["sg_ln_g"], p["sg_ln_b"], p["sg_w_s"], p["sg_bias"])
    mixed1 = _mm(y1, p["sg_w_out"], mode="nn", name="mm_sg_out")
    h3, h3b = _ln_fwd(h2, mixed1, p["ln1_g"][1], p["ln1_b"][1], "ln1_1")
    ffn1 = ffn_forward(1, h3b)

    loss, ds, dres, g["ln2_g1"], g["ln2_b1"] = _ln_loss_bwd(h3, ffn1, target, p["ln2_g"][1], p["ln2_b"][1], "ln2_1_loss")

    def ffn_backward(layer, ds_bf, h_in_bf):
        dz = _mm(ds_bf, p["ffn_w_down"][layer], mode="nt", name=f"mm_d_z{layer}")
        g[f"ffn_w_down{layer}"] = _mm(z_l[layer], ds_bf, mode="tn", out_dtype=BF16, name=f"mm_dw_down{layer}")
        dhh, g[f"conv_w{layer}"], g[f"conv_b{layer}"] = _ffn_bwd(hh_l[layer], dz, p["ffn_conv_w"][layer],
                                                                   p["ffn_conv_b"][layer])
        dh = _mm(dhh, p["ffn_w_up"][layer], mode="nt", a_cs=True, b_cs=True, name=f"mm_d_up{layer}")
        g[f"ffn_w_up{layer}"] = _mm(h_in_bf, dhh, mode="tn", b_cs=True, o_cs=N_DEV, out_dtype=BF16,
                                    name=f"mm_dw_up{layer}")
        return dh

    dh3 = ffn_backward(1, ds, h3b)
    ds, dres, g["ln1_g1"], g["ln1_b1"] = _ln_bwd(h2, mixed1, dres, dh3, p["ln1_g"][1], "ln1_1_bwd")
    dy1 = _mm(ds, p["sg_w_out"], mode="nt", name="mm_d_sg_out")
    g["sg_w_out"] = _mm(y1, ds, mode="tn", out_dtype=BF16, name="mm_dw_sg_out")
    dpre1, g["sg_w_s"], dbs, g["sg_ln_g"], g["sg_ln_b"] = _sg_bwd(pre1, dy1, p["sg_ln_g"], p["sg_ln_b"],
                                                                    p["sg_w_s"], p["sg_bias"])
    g["sg_b_s"] = dbs[:, 0, :]
    dh2 = _mm(dpre1, p["sg_w_in"], mode="nt", b_cs=True, name="mm_d_sg_in")
    g["sg_w_in"] = _mm(h2b, dpre1, mode="tn", o_cs=N_DEV, out_dtype=BF16, name="mm_dw_sg_in")
    ds, dres, g["ln2_g0"], g["ln2_b0"] = _ln_bwd(h1, ffn0, dres, dh2, p["ln2_g"][0], "ln2_0_bwd")
    dh1 = ffn_backward(0, ds, h1b)
    ds, dres, g["ln1_g0"], g["ln1_b0"] = _ln_bwd(x, mixed0, dres, dh1, p["ln1_g"][0], "ln1_0_bwd")
    dy0 = _mm(ds, p["hg_w_out"], mode="nt", name="mm_d_hg_out")
    g["hg_w_out"] = _mm(y0, ds, mode="tn", out_dtype=BF16, name="mm_dw_hg_out")
    dproj, d0, d1, d2, g["hg_norm_g"] = _hg_bwd(proj0, states, dy0, l0, l1, l2, p["hg_norm_g"])
    g["lb_logits"] = jnp.concatenate([d0, d1, d2], axis=0)
    grad_x = _mm(dproj, p["hg_w_in"], mode="nt", a_cs=True, b_cs=True, addend=dres, name="mm_d_hg_in")
    g["hg_w_in"] = _mm(x_bf, dproj, mode="tn", b_cs=True, o_cs=N_DEV, out_dtype=BF16, name="mm_dw_hg_in")
    return loss[0, 0], grad_x, g


_SMALL = ("lb_logits", "hg_norm_g", "sg_ln_g", "sg_ln_b", "sg_w_s", "sg_b_s", "ffn_conv_w", "ffn_conv_b",
          "ln1_g", "ln1_b", "ln2_g", "ln2_b")
_NAMES = ("lb_logits", "hg_w_in", "hg_norm_g", "hg_w_out", "sg_w_in", "sg_ln_g", "sg_ln_b", "sg_w_s", "sg_b_s",
          "sg_w_out", "ffn_w_up", "ffn_conv_w", "ffn_conv_b", "ffn_w_down", "ln1_g", "ln1_b", "ln2_g", "ln2_b")


def kernel(x, lb_logits, hg_w_in, hg_norm_g, hg_w_out, sg_w_in, sg_ln_g, sg_ln_b, sg_w_s, sg_b_s, sg_w_out, ffn_w_up, ffn_conv_w, ffn_conv_b, ffn_w_down, ln1_g, ln1_b, ln2_g, ln2_b, loss_target, m_lb_logits, m_hg_w_in, m_hg_norm_g, m_hg_w_out, m_sg_w_in, m_sg_ln_g, m_sg_ln_b, m_sg_w_s, m_sg_b_s, m_sg_w_out, m_ffn_w_up, m_ffn_conv_w, m_ffn_conv_b, m_ffn_w_down, m_ln1_g, m_ln1_b, m_ln2_g, m_ln2_b, v_lb_logits, v_hg_w_in, v_hg_norm_g, v_hg_w_out, v_sg_w_in, v_sg_ln_g, v_sg_ln_b, v_sg_w_s, v_sg_b_s, v_sg_w_out, v_ffn_w_up, v_ffn_conv_w, v_ffn_conv_b, v_ffn_w_down, v_ln1_g, v_ln1_b, v_ln2_g, v_ln2_b):
    w = dict(lb_logits=lb_logits, hg_w_in=hg_w_in, hg_norm_g=hg_norm_g, hg_w_out=hg_w_out, sg_w_in=sg_w_in,
             sg_ln_g=sg_ln_g, sg_ln_b=sg_ln_b, sg_w_s=sg_w_s, sg_b_s=sg_b_s, sg_w_out=sg_w_out, ffn_w_up=ffn_w_up,
             ffn_conv_w=ffn_conv_w, ffn_conv_b=ffn_conv_b, ffn_w_down=ffn_w_down, ln1_g=ln1_g, ln1_b=ln1_b,
             ln2_g=ln2_g, ln2_b=ln2_b)
    m = dict(lb_logits=m_lb_logits, hg_w_in=m_hg_w_in, hg_norm_g=m_hg_norm_g, hg_w_out=m_hg_w_out,
             sg_w_in=m_sg_w_in, sg_ln_g=m_sg_ln_g, sg_ln_b=m_sg_ln_b, sg_w_s=m_sg_w_s, sg_b_s=m_sg_b_s,
             sg_w_out=m_sg_w_out, ffn_w_up=m_ffn_w_up, ffn_conv_w=m_ffn_conv_w, ffn_conv_b=m_ffn_conv_b,
             ffn_w_down=m_ffn_w_down, ln1_g=m_ln1_g, ln1_b=m_ln1_b, ln2_g=m_ln2_g, ln2_b=m_ln2_b)
    v = dict(lb_logits=v_lb_logits, hg_w_in=v_hg_w_in, hg_norm_g=v_hg_norm_g, hg_w_out=v_hg_w_out,
             sg_w_in=v_sg_w_in, sg_ln_g=v_sg_ln_g, sg_ln_b=v_sg_ln_b, sg_w_s=v_sg_w_s, sg_b_s=v_sg_b_s,
             sg_w_out=v_sg_w_out, ffn_w_up=v_ffn_w_up, ffn_conv_w=v_ffn_conv_w, ffn_conv_b=v_ffn_conv_b,
             ffn_w_down=v_ffn_w_down, ln1_g=v_ln1_g, ln1_b=v_ln1_b, ln2_g=v_ln2_g, ln2_b=v_ln2_b)
    me = 4 * lax.axis_index("x") + 2 * lax.axis_index("y") + lax.axis_index("c")
    d = D_MODEL

    p = {}
    p["hg_w_in"] = _all_gather(hg_w_in[0].astype(BF16), "ag_hg_w_in")
    p["hg_w_out"] = _all_gather(hg_w_out[0].astype(BF16), "ag_hg_w_out").reshape(d, d)
    p["sg_w_in"] = _all_gather(sg_w_in[0].astype(BF16), "ag_sg_w_in")
    p["sg_w_out"] = _all_gather(sg_w_out[0].astype(BF16), "ag_sg_w_out").reshape(d, d)
    p["ffn_w_up"] = [_all_gather(ffn_w_up[l].astype(BF16), f"ag_ffn_w_up{l}") for l in range(DEPTH)]
    p["ffn_w_down"] = [_all_gather(ffn_w_down[l].astype(BF16), f"ag_ffn_w_down{l}").reshape(D_FF, d)
                       for l in range(DEPTH)]
    sv = jnp.zeros((8, 768), F32)
    sv = sv.at[0, :256].set(sg_ln_g[0]).at[1, :256].set(sg_ln_b[0]).at[2:8, :704].set(ffn_conv_w.reshape(6, 704))
    sv = _all_gather(sv, "ag_small")
    p["sg_ln_g"] = sv[:, 0, :256].reshape(1, d)
    p["sg_ln_b"] = sv[:, 1, :256].reshape(1, d)
    conv_w_full = jnp.transpose(sv[:, 2:8, :704].reshape(N_DEV, DEPTH, 3, 704), (1, 2, 0, 3)).reshape(DEPTH, 3, D_FF)
    p["ffn_conv_w"] = [conv_w_full[l] for l in range(DEPTH)]
    p["ffn_conv_b"] = [ffn_conv_b[l:l + 1] for l in range(DEPTH)]
    p["hg_norm_g"] = hg_norm_g
    p["sg_w_s"] = sg_w_s[0]
    p["sg_bias"] = jnp.broadcast_to(sg_b_s[0][:, :, None], (SG_GROUPS, SG_CHUNK, SG_DIM))
    for n in ("ln1_g", "ln1_b", "ln2_g", "ln2_b"):
        p[n] = [w[n][l:l + 1] for l in range(DEPTH)]

    loss_part, grad_x, g = _local_step(x[0], loss_target[0], lb_logits, p)
    loss = lax.psum(loss_part, ("x", "y", "c"))

    small = [g["lb_logits"], g["hg_norm_g"], g["sg_ln_g"], g["sg_ln_b"], g["sg_w_s"], g["sg_b_s"],
             g["conv_w0"], g["conv_w1"], g["conv_b0"], g["conv_b1"],
             g["ln1_g0"], g["ln1_g1"], g["ln1_b0"], g["ln1_b1"], g["ln2_g0"], g["ln2_g1"], g["ln2_b0"], g["ln2_b1"]]
    sizes = [int(np.prod(a.shape)) for a in small]
    flat = jnp.concatenate([a.reshape(-1) for a in small])
    total = flat.shape[0]
    rows = -(-total // 128)
    rows = -(-rows // 8) * 8
    flat = jnp.pad(flat, (0, rows * 128 - total)).reshape(rows, 128)
    red = _all_reduce_small(flat, "ar_small").reshape(-1)
    offs = np.cumsum([0] + sizes)
    r = [red[offs[i]:offs[i + 1]].reshape(small[i].shape) for i in range(len(small))]
    gs = {}
    gs["lb_logits"] = r[0]
    gs["hg_norm_g"] = r[1]
    gs["sg_ln_g"] = lax.dynamic_slice(r[2], (0, me * 256), (1, 256))
    gs["sg_ln_b"] = lax.dynamic_slice(r[3], (0, me * 256), (1, 256))
    gs["sg_w_s"] = r[4][None]
    gs["sg_b_s"] = r[5][None]
    gs["ffn_conv_w"] = lax.dynamic_slice(jnp.stack([r[6], r[7]]), (0, 0, me * 704), (DEPTH, 3, 704))
    gs["ffn_conv_b"] = jnp.concatenate([r[8], r[9]], axis=0)
    gs["ln1_g"] = jnp.concatenate([r[10], r[11]], axis=0)
    gs["ln1_b"] = jnp.concatenate([r[12], r[13]], axis=0)
    gs["ln2_g"] = jnp.concatenate([r[14], r[15]], axis=0)
    gs["ln2_b"] = jnp.concatenate([r[16], r[17]], axis=0)

    out_g, out_d, out_m, out_v = {}, {}, {}, {}
    for n in _SMALL:
        out_g[n], out_d[n], out_m[n], out_v[n] = _adamw(w[n], gs[n], m[n], v[n], f"adamw_{n}")

    def big(n, parts, layer=None):
        sel = (lambda a: a[layer]) if layer is not None else (lambda a: a[0])
        shard_shape = sel(w[n]).shape
        recv = _exchange_parts(parts.reshape((N_DEV,) + shard_shape), f"rs_{n}{'' if layer is None else layer}")
        return _adamw(sel(w[n]), None, sel(m[n]), sel(v[n]), f"adamw_{n}{'' if layer is None else layer}", parts=recv)

    for n in ("hg_w_in", "hg_w_out", "sg_w_in", "sg_w_out"):
        res = big(n, g[n])
        out_g[n], out_d[n], out_m[n], out_v[n] = (a[None] for a in res)
    for n in ("ffn_w_up", "ffn_w_down"):
        res = [big(n, g[f"{n}{l}"], layer=l) for l in range(DEPTH)]
        out_g[n], out_d[n], out_m[n], out_v[n] = (jnp.stack([res[0][i], res[1][i]]) for i in range(4))

    return (loss, grad_x[None], *[out_g[n] for n in _NAMES], *[out_d[n] for n in _NAMES],
            *[out_m[n] for n in _NAMES], *[out_v[n] for n in _NAMES])
```

```python
import functools

import numpy as np
import jax
import jax.numpy as jnp
from jax import lax
from jax.experimental import pallas as pl
from jax.experimental.pallas import tpu as pltpu

F32 = jnp.float32
BF16 = jnp.bfloat16
HI = lax.Precision.HIGHEST

N_DEV = 8
D_MODEL = 2048
HG_HEADS = 16
HG_DIM = 128
HG_CHUNK = 64
SG_GROUPS = 16
SG_DIM = 128
SG_CHUNK = 128
D_FF = 5632
DEPTH = 2
ALPHA = (2 * DEPTH) ** 0.25
LN_EPS = 1e-5
RMS_EPS = 1e-6
ADAM_LR = 0.001
ADAM_B1 = 0.9
ADAM_B2 = 0.999
ADAM_EPS = 1e-08
ADAM_WD = 0.01
ADAM_STEP = 10

VMEM_LIMIT = 56 * 1024 * 1024
MESH = pl.DeviceIdType.MESH
ANY = pl.BlockSpec(memory_space=pl.ANY)


def _cp(*sem):
    return pltpu.CompilerParams(dimension_semantics=sem, vmem_limit_bytes=VMEM_LIMIT)


def _pick(n, cands):
    for c in cands:
        if n % c == 0:
            return c
    raise ValueError(f"no tile for {n} in {cands}")


_DIMS = {"nn": (((1,), (0,)), ((), ())), "nt": (((1,), (1,)), ((), ())), "tn": (((0,), (0,)), ((), ()))}


def _mm(a, b, *, mode, name, out_dtype=F32, a_cs=False, b_cs=False, o_cs=None, addend=None,
        bm=None, bn=None, bk=None, deps=()):
    if mode == "tn":
        kk, m = a.shape
    elif a_cs:
        m, kk = a.shape[1], a.shape[0] * a.shape[2]
    else:
        m, kk = a.shape
    if mode == "nt":
        n = b.shape[1] if b_cs else b.shape[0]
    else:
        n = b.shape[0] * b.shape[2] if b_cs else b.shape[1]
    a_c = a.shape[2] if a_cs else None
    b_c = b.shape[2] if b_cs else None
    o_c = n // o_cs if o_cs else None
    bm = bm or _pick(m, (1024, 1408, 512, 256))
    n_div = [c for c in (b_c if (b_cs and mode != "nt") else None, o_c) if c]
    k_div = [c for c in (a_c, b_c if (b_cs and mode == "nt") else None) if c]

    def pick_with(total, divs, cands):
        for c in cands:
            if total % c == 0 and all(d % c == 0 for d in divs):
                return c
        raise ValueError((total, divs))

    bn = bn or pick_with(n, n_div, (1024, 1408, 512, 256, 128))
    bk = bk or pick_with(kk, k_div, (512, 1408, 1024, 256, 128))
    nk = kk // bk

    def cs_idx(blk, per):
        return (blk * per[0]) // per[1], (blk * per[0] % per[1]) // per[0]

    if mode == "tn":
        a_spec = pl.BlockSpec((bk, bm), lambda i, j, k: (k, i))
    elif a_cs:
        def a_map(i, j, k):
            s, r = cs_idx(k, (bk, a_c))
            return (s, i, r)
        a_spec = pl.BlockSpec((None, bm, bk), a_map)
    else:
        a_spec = pl.BlockSpec((bm, bk), lambda i, j, k: (i, k))
    if mode == "nt":
        if b_cs:
            def b_map(i, j, k):
                s, r = cs_idx(k, (bk, b_c))
                return (s, j, r)
            b_spec = pl.BlockSpec((None, bn, bk), b_map)
        else:
            b_spec = pl.BlockSpec((bn, bk), lambda i, j, k: (j, k))
    else:
        if b_cs:
            def b_map(i, j, k):
                s, r = cs_idx(j, (bn, b_c))
                return (s, k, r)
            b_spec = pl.BlockSpec((None, bk, bn), b_map)
        else:
            b_spec = pl.BlockSpec((bk, bn), lambda i, j, k: (k, j))
    if o_cs:
        def o_map(i, j, k):
            s, r = cs_idx(j, (bn, o_c))
            return (s, i, r)
        o_spec = pl.BlockSpec((None, bm, bn), o_map)
        out_shape = jax.ShapeDtypeStruct((o_cs, m, o_c), out_dtype)
    else:
        o_spec = pl.BlockSpec((bm, bn), lambda i, j, k: (i, j))
        out_shape = jax.ShapeDtypeStruct((m, n), out_dtype)
    in_specs = [a_spec, b_spec]
    args = [a, b]
    if addend is not None:
        in_specs.append(pl.BlockSpec((bm, bn), lambda i, j, k: (i, j)))
        args.append(addend)
    dims = _DIMS[mode]
    in_specs += [ANY] * len(deps)
    args += list(deps)

    def body(*refs):
        refs = refs[:len(refs) - 2 - len(deps)] + refs[len(refs) - 2:]
        if addend is not None:
            a_ref, b_ref, add_ref, o_ref, acc_ref = refs
        else:
            a_ref, b_ref, o_ref, acc_ref = refs
        k = pl.program_id(2)

        @pl.when(k == 0)
        def _():
            acc_ref[...] = jnp.zeros_like(acc_ref)

        acc_ref[...] += lax.dot_general(a_ref[...].astype(BF16), b_ref[...].astype(BF16), dims,
                                        preferred_element_type=F32)

        @pl.when(k == nk - 1)
        def _():
            r = acc_ref[...]
            if addend is not None:
                r = r + add_ref[...]
            o_ref[...] = r.astype(o_ref.dtype)

    return pl.pallas_call(
        body, name=name, out_shape=out_shape, grid=(m // bm, n // bn, nk),
        in_specs=in_specs, out_specs=o_spec, scratch_shapes=[pltpu.VMEM((bm, bn), F32)],
        compiler_params=_cp("parallel", "parallel", "arbitrary"),
    )(*args)


LN_ROWS = 256


def _ln_stats(s):
    mu = jnp.mean(s, axis=-1, keepdims=True)
    sc = s - mu
    var = jnp.mean(sc * sc, axis=-1, keepdims=True)
    rstd = lax.rsqrt(var + LN_EPS)
    return sc * rstd, rstd


def _ln_fwd(h, sub, g, b, name):
    t, d = h.shape
    row = pl.BlockSpec((LN_ROWS, d), lambda i: (i, 0))
    vec = pl.BlockSpec((1, d), lambda i: (0, 0))

    def body(h_ref, s_ref, g_ref, b_ref, y_ref, yb_ref):
        xhat, _ = _ln_stats(ALPHA * h_ref[...] + s_ref[...])
        y = xhat * g_ref[...] + b_ref[...]
        y_ref[...] = y
        yb_ref[...] = y.astype(BF16)

    return pl.pallas_call(
        body, name=name, grid=(t // LN_ROWS,),
        out_shape=(jax.ShapeDtypeStruct((t, d), F32), jax.ShapeDtypeStruct((t, d), BF16)),
        in_specs=[row, row, vec, vec], out_specs=(row, row), compiler_params=_cp("parallel"),
    )(h, sub, g, b)


def _ln_bwd_math(xhat, rstd, dy, g):
    dxhat = dy * g
    m1 = jnp.mean(dxhat, axis=-1, keepdims=True)
    m2 = jnp.mean(dxhat * xhat, axis=-1, keepdims=True)
    ds = rstd * (dxhat - m1 - xhat * m2)
    dg = jnp.sum(dy * xhat, axis=0, keepdims=True)
    db = jnp.sum(dy, axis=0, keepdims=True)
    return ds, dg, db


def _ln_bwd(h, sub, dy_a, dy_b, g, name):
    t, d = h.shape
    row = pl.BlockSpec((LN_ROWS, d), lambda i: (i, 0))
    vec = pl.BlockSpec((1, d), lambda i: (0, 0))

    def body(h_ref, s_ref, da_ref, db_ref, g_ref, ds_ref, dres_ref, dg_ref, dbeta_ref):
        xhat, rstd = _ln_stats(ALPHA * h_ref[...] + s_ref[...])
        ds, dg, db = _ln_bwd_math(xhat, rstd, da_ref[...] + db_ref[...], g_ref[...])
        ds_ref[...] = ds.astype(BF16)
        dres_ref[...] = ALPHA * ds

        @pl.when(pl.program_id(0) == 0)
        def _():
            dg_ref[...] = jnp.zeros_like(dg_ref)
            dbeta_ref[...] = jnp.zeros_like(dbeta_ref)

        dg_ref[...] += dg
        dbeta_ref[...] += db

    return pl.pallas_call(
        body, name=name, grid=(t // LN_ROWS,),
        out_shape=(jax.ShapeDtypeStruct((t, d), BF16), jax.ShapeDtypeStruct((t, d), F32),
                   jax.ShapeDtypeStruct((1, d), F32), jax.ShapeDtypeStruct((1, d), F32)),
        in_specs=[row, row, row, row, vec], out_specs=(row, row, vec, vec),
        compiler_params=_cp("arbitrary"),
    )(h, sub, dy_a, dy_b, g)


def _ln_loss_bwd(h, sub, target, g, b, name):
    t, d = h.shape
    row = pl.BlockSpec((LN_ROWS, d), lambda i: (i, 0))
    vec = pl.BlockSpec((1, d), lambda i: (0, 0))
    lvec = pl.BlockSpec((1, 128), lambda i: (0, 0))

    def body(h_ref, s_ref, t_ref, g_ref, b_ref, loss_ref, ds_ref, dres_ref, dg_ref, dbeta_ref):
        xhat, rstd = _ln_stats(ALPHA * h_ref[...] + s_ref[...])
        y = xhat * g_ref[...] + b_ref[...]
        err = y - t_ref[...]
        part = 0.5 * jnp.sum(jnp.mean(err * err, axis=-1, keepdims=True), axis=0, keepdims=True)
        ds, dg, db = _ln_bwd_math(xhat, rstd, err * (1.0 / d), g_ref[...])
        ds_ref[...] = ds.astype(BF16)
        dres_ref[...] = ALPHA * ds

        @pl.when(pl.program_id(0) == 0)
        def _():
            loss_ref[...] = jnp.zeros_like(loss_ref)
            dg_ref[...] = jnp.zeros_like(dg_ref)
            dbeta_ref[...] = jnp.zeros_like(dbeta_ref)

        loss_ref[...] += jnp.broadcast_to(part, loss_ref.shape)
        dg_ref[...] += dg
        dbeta_ref[...] += db

    return pl.pallas_call(
        body, name=name, grid=(t // LN_ROWS,),
        out_shape=(jax.ShapeDtypeStruct((1, 128), F32), jax.ShapeDtypeStruct((t, d), BF16),
                   jax.ShapeDtypeStruct((t, d), F32), jax.ShapeDtypeStruct((1, d), F32),
                   jax.ShapeDtypeStruct((1, d), F32)),
        in_specs=[row, row, row, vec, vec], out_specs=(lvec, row, row, vec, vec),
        compiler_params=_cp("arbitrary"),
    )(h, sub, target, g, b)


HG_ROWS = 512
HG_LEVELS = 6


def _hg_constants():
    c = HG_CHUNK
    t = np.arange(c)[:, None]
    j = np.arange(c)[None, :]
    blocks, masks = [], [np.eye(c)]
    for lev in range(HG_LEVELS):
        m = 1 << lev
        second = (t % (2 * m)) >= m
        mid = (t // (2 * m)) * (2 * m) + m - 1
        blocks.append((second & (j > mid) & (j <= t)) | ((~second) & (j > t) & (j <= mid)))
        same = (t // (2 * m)) == (j // (2 * m))
        masks.append(same & second & ((j % (2 * m)) < m))
    blocks += [j <= t, j > t, np.ones((c, c), bool)]
    cm = np.concatenate(blocks, axis=0).astype(np.float32)
    mk = np.concatenate(masks, axis=0).astype(np.float32)
    return jnp.asarray(cm, dtype=BF16), jnp.asarray(mk)


def _split3(x):
    hi = x.astype(BF16)
    r = x - hi.astype(F32)
    mid = r.astype(BF16)
    lo = (r - mid.astype(F32)).astype(BF16)
    return hi, mid, lo


def _sum3(x):
    n = x.shape[-1] // 3
    return x[:, :n] + x[:, n:2 * n] + x[:, 2 * n:]


@jax.custom_vjp
def _prefix(cm, lf):
    return _sum3(lax.dot_general(cm, jnp.concatenate(_split3(lf), axis=1), _DIMS["nn"], preferred_element_type=F32))


def _prefix_fwd(cm, lf):
    return _prefix(cm, lf), cm


def _prefix_bwd(cm, d):
    d_lf = _sum3(lax.dot_general(cm, jnp.concatenate(_split3(d), axis=1), _DIMS["tn"], preferred_element_type=F32))
    return jnp.zeros_like(cm), d_lf


_prefix.defvjp(_prefix_fwd, _prefix_bwd)

_BWD = {"nn": (("nt", 0, 1), ("tn", 1, 0)), "nt": (("nn", 0, 1), ("tn", 0, 1)), "tn": (("nt", 1, 0), ("nn", 1, 0))}


def _bdot_raw(a, b, mode):
    return lax.dot_general(a.astype(BF16), b.astype(BF16), _DIMS[mode], preferred_element_type=F32)


@functools.partial(jax.custom_vjp, nondiff_argnums=(2,))
def _bdot(a, b, mode):
    return _bdot_raw(a, b, mode)


def _bdot_fwd(a, b, mode):
    return _bdot_raw(a, b, mode), (a, b)


def _bdot_bwd(mode, res, d):
    a, b = res
    (ma, da_pos, _), (mb, db_pos, _) = _BWD[mode]
    da = _bdot_raw(d, b, ma) if da_pos == 0 else _bdot_raw(b, d, ma)
    db = _bdot_raw(d, a, mb) if db_pos == 0 else _bdot_raw(a, d, mb)
    return da, db


_bdot.defvjp(_bdot_fwd, _bdot_bwd)


def _hg_chunk(pq, pf, pi, pg, l0, l1, l2, ng, s_t, cm, mk):
    c = HG_CHUNK
    mx = jnp.maximum(jnp.maximum(l0, l1), l2)
    e0, e1, e2 = jnp.exp(l0 - mx), jnp.exp(l1 - mx), jnp.exp(l2 - mx)
    lb = e0 / (e0 + e1 + e2)
    q = pq * jax.nn.sigmoid(pq)
    t1 = jnp.log(lb)
    t2 = jnp.log1p(-lb) + jax.nn.log_sigmoid(pf)
    lf = jnp.maximum(t1, t2) + jnp.log1p(jnp.exp(-jnp.abs(t1 - t2)))
    k = (1.0 - lb) * jax.nn.sigmoid(-pf)
    x = jnp.exp(_prefix(cm, lf))
    scores = mk[0:c] * _bdot(q, k, "nt")
    for lev in range(HG_LEVELS):
        xl = x[lev * c:(lev + 1) * c]
        scores = scores + mk[(lev + 1) * c:(lev + 2) * c] * _bdot(q * xl, k * xl, "nt")
    x_incl = x[HG_LEVELS * c:(HG_LEVELS + 1) * c]
    x_after = x[(HG_LEVELS + 1) * c:(HG_LEVELS + 2) * c]
    x_total = x[(HG_LEVELS + 2) * c:(HG_LEVELS + 3) * c]
    o = _bdot(scores, pi, "nn") + _bdot(q * x_incl, s_t, "nt")
    s_new = s_t * jnp.concatenate([x_total, x_total], axis=0) + _bdot(pi, k * x_after, "tn")
    rstd = lax.rsqrt(jnp.mean(o * o, axis=-1, keepdims=True) + RMS_EPS)
    y = o * rstd * ng * (pg * jax.nn.sigmoid(pg))
    return y, s_new


def _hg_specs(t, reverse):
    nrb = t // HG_ROWS
    rb = (lambda r: nrb - 1 - r) if reverse else (lambda r: r)
    proj = pl.BlockSpec((4, HG_ROWS, HG_DIM), lambda h, r: (0, rb(r), h))
    vec = pl.BlockSpec((1, HG_DIM), lambda h, r: (0, h))
    cm = pl.BlockSpec(((HG_LEVELS + 3) * HG_CHUNK, HG_CHUNK), lambda h, r: (0, 0))
    mk = pl.BlockSpec(((HG_LEVELS + 1) * HG_CHUNK, HG_CHUNK), lambda h, r: (0, 0))
    rows = pl.BlockSpec((HG_ROWS, HG_DIM), lambda h, r: (rb(r), h))
    states = pl.BlockSpec((None, HG_ROWS // HG_CHUNK, HG_DIM, HG_DIM), lambda h, r: (h, rb(r), 0, 0))
    return proj, vec, cm, mk, rows, states


def _hg_fwd(proj, l0, l1, l2, ng):
    t = proj.shape[1]
    n_in = HG_ROWS // HG_CHUNK
    cm, mk = _hg_constants()
    p_spec, vec, cm_spec, mk_spec, rows, st_spec = _hg_specs(t, False)

    def body(p_ref, l0_ref, l1_ref, l2_ref, ng_ref, cm_ref, mk_ref, y_ref, st_ref, s_ref):
        @pl.when(pl.program_id(1) == 0)
        def _():
            s_ref[...] = jnp.zeros_like(s_ref)

        def step(ci, carry):
            sl = pl.ds(pl.multiple_of(ci * HG_CHUNK, HG_CHUNK), HG_CHUNK)
            s_t = s_ref[...]
            st_ref[ci] = s_t
            y, s_new = _hg_chunk(p_ref[0, sl, :], p_ref[1, sl, :], p_ref[2, sl, :], p_ref[3, sl, :],
                                 l0_ref[...], l1_ref[...], l2_ref[...], ng_ref[...], s_t,
                                 cm_ref[...], mk_ref[...])
            y_ref[sl, :] = y.astype(BF16)
            s_ref[...] = s_new
            return carry

        lax.fori_loop(0, n_in, step, 0, unroll=8)

    return pl.pallas_call(
        body, name="hg_fwd", grid=(HG_HEADS, t // HG_ROWS),
        out_shape=(jax.ShapeDtypeStruct((t, D_MODEL), BF16),
                   jax.ShapeDtypeStruct((HG_HEADS, t // HG_CHUNK, HG_DIM, HG_DIM), F32)),
        in_specs=[p_spec, vec, vec, vec, vec, cm_spec, mk_spec], out_specs=(rows, st_spec),
        scratch_shapes=[pltpu.VMEM((HG_DIM, HG_DIM), F32)],
        compiler_params=_cp("parallel", "arbitrary"),
    )(proj, l0, l1, l2, ng, cm, mk)


def _hg_bwd(proj, states, dy, l0, l1, l2, ng):
    t = proj.shape[1]
    n_in = HG_ROWS // HG_CHUNK
    cm, mk = _hg_constants()
    p_spec, vec, cm_spec, mk_spec, rows, st_spec = _hg_specs(t, True)

    def body(p_ref, st_ref, dy_ref, l0_ref, l1_ref, l2_ref, ng_ref, cm_ref, mk_ref,
             dp_ref, dl0_ref, dl1_ref, dl2_ref, dng_ref, ds_ref):
        @pl.when(pl.program_id(1) == 0)
        def _():
            ds_ref[...] = jnp.zeros_like(ds_ref)
            for r in (dl0_ref, dl1_ref, dl2_ref, dng_ref):
                r[...] = jnp.zeros_like(r)

        def step(it, carry):
            ci = n_in - 1 - it
            sl = pl.ds(pl.multiple_of(ci * HG_CHUNK, HG_CHUNK), HG_CHUNK)
            fn = functools.partial(_hg_chunk, cm=cm_ref[...], mk=mk_ref[...])
            _, vjp = jax.vjp(fn, p_ref[0, sl, :], p_ref[1, sl, :], p_ref[2, sl, :], p_ref[3, sl, :],
                             l0_ref[...], l1_ref[...], l2_ref[...], ng_ref[...], st_ref[ci])
            dq, df, di, dg, d0, d1, d2, dn, ds = vjp((dy_ref[sl, :], ds_ref[...]))
            dp_ref[0, sl, :] = dq.astype(BF16)
            dp_ref[1, sl, :] = df.astype(BF16)
            dp_ref[2, sl, :] = di.astype(BF16)
            dp_ref[3, sl, :] = dg.astype(BF16)
            dl0_ref[...] += d0
            dl1_ref[...] += d1
            dl2_ref[...] += d2
            dng_ref[...] += dn
            ds_ref[...] = ds
            return carry

        lax.fori_loop(0, n_in, step, 0, unroll=8)

    v_shape = jax.ShapeDtypeStruct((1, D_MODEL), F32)
    return pl.pallas_call(
        body, name="hg_bwd", grid=(HG_HEADS, t // HG_ROWS),
        out_shape=(jax.ShapeDtypeStruct((4, t, D_MODEL), BF16), v_shape, v_shape, v_shape, v_shape),
        in_specs=[p_spec, st_spec, rows, vec, vec, vec, vec, cm_spec, mk_spec],
        out_specs=(p_spec, vec, vec, vec, vec),
        scratch_shapes=[pltpu.VMEM((HG_DIM, HG_DIM), F32)],
        compiler_params=_cp("parallel", "arbitrary"),
    )(proj, states, dy, l0, l1, l2, ng, cm, mk)


_SQRT_HALF = 0.7071067811865476
_INV_SQRT_2PI = 0.3989422804014327


def _gelu(x):
    return 0.5 * x * (1.0 + lax.erf(x * _SQRT_HALF))


def _gelu_grad(x):
    return 0.5 * (1.0 + lax.erf(x * _SQRT_HALF)) + x * (_INV_SQRT_2PI * jnp.exp(-0.5 * x * x))


def _tril(n):
    return (lax.broadcasted_iota(jnp.int32, (n, n), 0) >= lax.broadcasted_iota(jnp.int32, (n, n), 1)).astype(F32)


def _sg_specs(d):
    row = lambda w: pl.BlockSpec((SG_CHUNK, w), lambda i: (i, 0))
    vec = pl.BlockSpec((1, d), lambda i: (0, 0))
    cube = pl.BlockSpec((SG_GROUPS, SG_CHUNK, SG_CHUNK), lambda i: (0, 0, 0))
    return row, vec, cube


def _sg_fwd(pre, ln_g, ln_b, w_s, bias):
    t = pre.shape[0]
    d = D_MODEL
    row, vec, cube = _sg_specs(d)

    def body(pre_ref, g_ref, b_ref, w_ref, bias_ref, y_ref, vln_ref):
        u = _gelu(pre_ref[:, :d])
        vhat, _ = _ln_stats(_gelu(pre_ref[:, d:]))
        vln_ref[...] = vhat * g_ref[...] + b_ref[...]
        tril = _tril(SG_CHUNK)
        for g in range(SG_GROUPS):
            cs = slice(g * SG_DIM, (g + 1) * SG_DIM)
            gate = _bdot_raw(w_ref[g] * tril, vln_ref[:, cs], "nn") + bias_ref[g]
            y_ref[:, cs] = (u[:, cs] * gate).astype(BF16)

    return pl.pallas_call(
        body, name="sg_fwd", grid=(t // SG_CHUNK,), out_shape=jax.ShapeDtypeStruct((t, d), BF16),
        in_specs=[row(2 * d), vec, vec, cube, cube], out_specs=row(d),
        scratch_shapes=[pltpu.VMEM((SG_CHUNK, d), F32)], compiler_params=_cp("parallel"),
    )(pre, ln_g, ln_b, w_s, bias)


def _sg_bwd(pre, dy, ln_g, ln_b, w_s, bias):
    t = pre.shape[0]
    d = D_MODEL
    row, vec, cube = _sg_specs(d)
    dbs_spec = pl.BlockSpec((SG_GROUPS, 8, SG_CHUNK), lambda i: (0, 0, 0))

    def body(pre_ref, dy_ref, g_ref, b_ref, w_ref, bias_ref, dpre_ref, dw_ref, dbs_ref, dlg_ref, dlb_ref,
             vln_ref, dvln_ref):
        @pl.when(pl.program_id(0) == 0)
        def _():
            for r in (dw_ref, dbs_ref, dlg_ref, dlb_ref):
                r[...] = jnp.zeros_like(r)

        pu = pre_ref[:, :d]
        pv = pre_ref[:, d:]
        u = _gelu(pu)
        vhat, rstd = _ln_stats(_gelu(pv))
        vln_ref[...] = vhat * g_ref[...] + b_ref[...]
        tril = _tril(SG_CHUNK)
        ones = jnp.ones((8, SG_DIM), F32)
        for g in range(SG_GROUPS):
            cs = slice(g * SG_DIM, (g + 1) * SG_DIM)
            wc = w_ref[g] * tril
            vg = vln_ref[:, cs]
            gate = _bdot_raw(wc, vg, "nn") + bias_ref[g]
            dyg = dy_ref[:, cs]
            dgate = dyg * u[:, cs]
            dpre_ref[:, cs] = (dyg * gate * _gelu_grad(pu[:, cs])).astype(BF16)
            dw_ref[g] += tril * _bdot_raw(dgate, vg, "nt")
            dbs_ref[g] += lax.dot_general(ones, dgate, _DIMS["nt"], precision=HI, preferred_element_type=F32)
            dvln_ref[:, cs] = _bdot_raw(wc, dgate, "tn")
        dvln = dvln_ref[...]
        dv, dg, db = _ln_bwd_math(vhat, rstd, dvln, g_ref[...])
        dlg_ref[...] += dg
        dlb_ref[...] += db
        dpre_ref[:, d:] = (dv * _gelu_grad(pv)).astype(BF16)

    return pl.pallas_call(
        body, name="sg_bwd", grid=(t // SG_CHUNK,),
        out_shape=(jax.ShapeDtypeStruct((t, 2 * d), BF16), jax.ShapeDtypeStruct(w_s.shape, F32),
                   jax.ShapeDtypeStruct((SG_GROUPS, 8, SG_CHUNK), F32),
                   jax.ShapeDtypeStruct((1, d), F32), jax.ShapeDtypeStruct((1, d), F32)),
        in_specs=[row(2 * d), row(d), vec, vec, cube, cube],
        out_specs=(row(2 * d), cube, dbs_spec, vec, vec),
        scratch_shapes=[pltpu.VMEM((SG_CHUNK, d), F32), pltpu.VMEM((SG_CHUNK, d), F32)],
        compiler_params=_cp("arbitrary"),
    )(pre, dy, ln_g, ln_b, w_s, bias)


FFN_ROWS = 256
FFN_COLS = 1408
HALO = 8


def _ffn_conv(a_ext, w_ref, cb_ref):
    a1 = pltpu.roll(a_ext, 1, 0)
    a2 = pltpu.roll(a_ext, 2, 0)
    return w_ref[0:1, :] * a2 + w_ref[1:2, :] * a1 + w_ref[2:3, :] * a_ext + cb_ref[...], a1, a2


def _ffn_fwd(hh, conv_w, conv_b):
    t = hh.shape[1]
    nb8 = FFN_ROWS // HALO
    main = pl.BlockSpec((2, FFN_ROWS, FFN_COLS), lambda c, r: (0, r, c))
    prev = pl.BlockSpec((None, HALO, FFN_COLS), lambda c, r: (0, jnp.maximum(r * nb8 - 1, 0), c))
    wspec = pl.BlockSpec((3, FFN_COLS), lambda c, r: (0, c))
    bspec = pl.BlockSpec((1, FFN_COLS), lambda c, r: (0, c))

    def body(m_ref, p_ref, w_ref, cb_ref, z_ref):
        prev = jnp.where(pl.program_id(1) == 0, 0.0, p_ref[...])
        a_ext = jnp.concatenate([prev, m_ref[0]], axis=0)
        a, _, _ = _ffn_conv(a_ext, w_ref, cb_ref)
        a = a[HALO:]
        z_ref[...] = (a * jax.nn.sigmoid(a) * m_ref[1]).astype(BF16)

    return pl.pallas_call(
        body, name="ffn_fwd", grid=(D_FF // FFN_COLS, t // FFN_ROWS),
        out_shape=jax.ShapeDtypeStruct((t, D_FF), BF16),
        in_specs=[main, prev, wspec, bspec],
        out_specs=pl.BlockSpec((FFN_ROWS, FFN_COLS), lambda c, r: (r, c)),
        compiler_params=_cp("parallel", "parallel"),
    )(hh, hh, conv_w, conv_b)


def _ffn_bwd(hh, dz, conv_w, conv_b):
    t = hh.shape[1]
    nb8 = FFN_ROWS // HALO
    last8 = t // HALO - 1
    nr = t // FFN_ROWS
    main = pl.BlockSpec((2, FFN_ROWS, FFN_COLS), lambda c, r: (0, r, c))
    prev = pl.BlockSpec((None, HALO, FFN_COLS), lambda c, r: (0, jnp.maximum(r * nb8 - 1, 0), c))
    nxt = pl.BlockSpec((2, HALO, FFN_COLS), lambda c, r: (0, jnp.minimum((r + 1) * nb8, last8), c))
    dmain = pl.BlockSpec((FFN_ROWS, FFN_COLS), lambda c, r: (r, c))
    dnxt = pl.BlockSpec((HALO, FFN_COLS), lambda c, r: (jnp.minimum((r + 1) * nb8, last8), c))
    wspec = pl.BlockSpec((3, FFN_COLS), lambda c, r: (0, c))
    bspec = pl.BlockSpec((1, FFN_COLS), lambda c, r: (0, c))

    def body(m_ref, p_ref, n_ref, dz_ref, dzn_ref, w_ref, cb_ref, dh_ref, dw_ref, dcb_ref):
        r = pl.program_id(1)

        @pl.when(r == 0)
        def _():
            dw_ref[...] = jnp.zeros_like(dw_ref)
            dcb_ref[...] = jnp.zeros_like(dcb_ref)

        prev = jnp.where(r == 0, 0.0, p_ref[...])
        a_ext = jnp.concatenate([prev, m_ref[0], n_ref[0]], axis=0)
        a, a1, a2 = _ffn_conv(a_ext, w_ref, cb_ref)
        a = a[HALO:]
        b_ext = jnp.concatenate([m_ref[1], n_ref[1]], axis=0)
        dz_ext = jnp.concatenate([dz_ref[...], jnp.where(r == nr - 1, 0.0, dzn_ref[...])], axis=0)
        sig = jax.nn.sigmoid(a)
        da = dz_ext * b_ext * (sig * (1.0 + a * (1.0 - sig)))
        n_ext = FFN_ROWS + HALO
        da_p1 = pltpu.roll(da, n_ext - 1, 0)
        da_p2 = pltpu.roll(da, n_ext - 2, 0)
        da_raw = w_ref[2:3, :] * da + w_ref[1:2, :] * da_p1 + w_ref[0:1, :] * da_p2
        dh_ref[0] = da_raw[:FFN_ROWS].astype(BF16)
        dh_ref[1] = (dz_ref[...] * (a * sig)[:FFN_ROWS]).astype(BF16)
        dam = da[:FFN_ROWS]
        rows = slice(HALO, HALO + FFN_ROWS)
        dw_ref[0:1, :] += jnp.sum(dam * a2[rows], axis=0, keepdims=True)
        dw_ref[1:2, :] += jnp.sum(dam * a1[rows], axis=0, keepdims=True)
        dw_ref[2:3, :] += jnp.sum(dam * a_ext[rows], axis=0, keepdims=True)
        dcb_ref[...] += jnp.sum(dam, axis=0, keepdims=True)

    return pl.pallas_call(
        body, name="ffn_bwd", grid=(D_FF // FFN_COLS, nr),
        out_shape=(jax.ShapeDtypeStruct((2, t, D_FF), BF16), jax.ShapeDtypeStruct((3, D_FF), F32),
                   jax.ShapeDtypeStruct((1, D_FF), F32)),
        in_specs=[main, prev, nxt, dmain, dnxt, wspec, bspec],
        out_specs=(main, wspec, bspec), compiler_params=_cp("parallel", "arbitrary"),
    )(hh, hh, hh, dz, dz, conv_w, conv_b)


def _adamw_math(w, g, m, v):
    m = ADAM_B1 * m + (1.0 - ADAM_B1) * g
    v = ADAM_B2 * v + (1.0 - ADAM_B2) * (g * g)
    m_hat = m / (1.0 - ADAM_B1 ** ADAM_STEP)
    v_hat = v / (1.0 - ADAM_B2 ** ADAM_STEP)
    delta = -ADAM_LR * (m_hat / (jnp.sqrt(v_hat) + ADAM_EPS) + ADAM_WD * w)
    return delta, m, v


def _as2d(shape):
    n = int(np.prod(shape))
    c = shape[-1] if shape[-1] % 128 == 0 else (128 if n % 128 == 0 else shape[-1])
    return n // c, c


def _adamw(w, g, m, v, name, parts=None):
    shape = w.shape
    r, c = _as2d(shape)
    br = r if r <= 512 else _pick(r, (256, 176, 128, 64, 8))
    blk = pl.BlockSpec((br, c), lambda i: (i, 0))
    w2, m2, v2 = (a.reshape(r, c) for a in (w, m, v))
    if parts is not None:
        g_in = parts.reshape(N_DEV, r, c)
        g_spec = pl.BlockSpec((N_DEV, br, c), lambda i: (0, i, 0))
    else:
        g_in = g.reshape(r, c)
        g_spec = blk

    def body(w_ref, g_ref, m_ref, v_ref, go_ref, d_ref, mo_ref, vo_ref):
        if parts is not None:
            gsum = g_ref[0].astype(F32)
            for i in range(1, N_DEV):
                gsum = gsum + g_ref[i].astype(F32)
        else:
            gsum = g_ref[...]
        delta, m_new, v_new = _adamw_math(w_ref[...], gsum, m_ref[...], v_ref[...])
        go_ref[...] = gsum
        d_ref[...] = delta
        mo_ref[...] = m_new
        vo_ref[...] = v_new

    o = jax.ShapeDtypeStruct((r, c), F32)
    outs = pl.pallas_call(
        body, name=name, grid=(r // br,), out_shape=(o, o, o, o),
        in_specs=[blk, g_spec, blk, blk], out_specs=(blk, blk, blk, blk), compiler_params=_cp("parallel"),
    )(w2, g_in, m2, v2)
    return tuple(a.reshape(shape) for a in outs)


def _me_and_peers():
    x, y, c = lax.axis_index("x"), lax.axis_index("y"), lax.axis_index("c")
    me = 4 * x + 2 * y + c
    peers = []
    for k in range(1, N_DEV):
        kx, ky, kc = (k >> 2) & 1, (k >> 1) & 1, k & 1
        px, py, pc = x ^ kx, y ^ ky, c ^ kc
        peers.append(((px, py, pc), 4 * px + 2 * py + pc))
    return me, peers


def _all_gather(shard, name):
    def body(x_ref, o_ref, send_sems, recv_sems, local_sem):
        me, peers = _me_and_peers()
        mine = pltpu.make_async_copy(x_ref, o_ref.at[me], local_sem)
        mine.start()
        sends = []
        for k, (dev, _) in enumerate(peers):
            cp = pltpu.make_async_remote_copy(src_ref=x_ref, dst_ref=o_ref.at[me], send_sem=send_sems.at[k],
                                              recv_sem=recv_sems.at[k], device_id=dev, device_id_type=MESH)
            cp.start()
            sends.append(cp)
        for k, (dev, idx) in enumerate(peers):
            pltpu.make_async_remote_copy(src_ref=x_ref, dst_ref=o_ref.at[idx], send_sem=send_sems.at[k],
                                         recv_sem=recv_sems.at[k], device_id=dev, device_id_type=MESH).wait_recv()
        for cp in sends:
            cp.wait_send()
        mine.wait()

    return pl.pallas_call(
        body, name=name, out_shape=jax.ShapeDtypeStruct((N_DEV,) + shard.shape, shard.dtype),
        in_specs=[ANY], out_specs=ANY,
        scratch_shapes=[pltpu.SemaphoreType.DMA((N_DEV - 1,)), pltpu.SemaphoreType.DMA((N_DEV - 1,)),
                        pltpu.SemaphoreType.DMA],
        compiler_params=pltpu.CompilerParams(has_side_effects=True),
    )(shard)


HBM = pl.BlockSpec(memory_space=pltpu.HBM)
SEM = pl.BlockSpec(memory_space=pltpu.SEMAPHORE)
EFFECT = pltpu.SideEffectType.DATAFLOW_SIDE_EFFECTING


def _xchg_start(src, *, gather, dep, name):
    piece = src.shape if gather else src.shape[1:]

    def body(x_ref, land_ref, dep_ref, send_sem, recv_sem, x_thru, land_thru, token, local_sem):
        me, peers = _me_and_peers()
        mine = pltpu.make_async_copy(x_ref if gather else x_ref.at[me], land_ref.at[me], local_sem)
        mine.start()
        for dev, idx in peers:
            pltpu.make_async_remote_copy(src_ref=x_ref if gather else x_ref.at[idx], dst_ref=land_ref.at[me],
                                         send_sem=send_sem, recv_sem=recv_sem, device_id=dev,
                                         device_id_type=MESH).start()
        token[...] = jnp.zeros_like(token)
        mine.wait()

    land = lax.empty((N_DEV,) + tuple(piece), src.dtype)
    return pl.pallas_call(
        body, name=name,
        out_shape=(pltpu.SemaphoreType.DMA(()), pltpu.SemaphoreType.DMA(()), pltpu.HBM(src.shape, src.dtype),
                   pltpu.HBM(land.shape, land.dtype), jax.ShapeDtypeStruct((8, 128), F32)),
        in_specs=(HBM, HBM, ANY), out_specs=(SEM, SEM, HBM, HBM, pl.BlockSpec(memory_space=pltpu.VMEM)),
        input_output_aliases={0: 2, 1: 3}, scratch_shapes=[pltpu.SemaphoreType.DMA],
        compiler_params=pltpu.CompilerParams(has_side_effects=EFFECT),
    )(pltpu.with_memory_space_constraint(src, pltpu.HBM), pltpu.with_memory_space_constraint(land, pltpu.HBM), dep)


def _xchg_wait(handles, after, name):
    send_sem, recv_sem, x_thru, land_thru, _ = handles

    def body(x_ref, land_ref, send_sem, recv_sem, after_ref, x_dead, got_ref):
        x, y, c = lax.axis_index("x"), lax.axis_index("y"), lax.axis_index("c")
        seven = land_ref.at[pl.ds(0, N_DEV - 1)]
        cp = pltpu.make_async_remote_copy(src_ref=seven, dst_ref=seven, send_sem=send_sem, recv_sem=recv_sem,
                                          device_id=(x, y, c), device_id_type=MESH)
        cp.wait_send()
        cp.wait_recv()

    return pl.pallas_call(
        body, name=name,
        out_shape=(pltpu.HBM(x_thru.shape, x_thru.dtype), pltpu.HBM(land_thru.shape, land_thru.dtype)),
        in_specs=(HBM, HBM, SEM, SEM, ANY), out_specs=(HBM, HBM), input_output_aliases={0: 0, 1: 1},
        compiler_params=pltpu.CompilerParams(has_side_effects=EFFECT),
    )(x_thru, land_thru, send_sem, recv_sem, after)[1]


def _all_reduce_small(v, name):
    r, c = v.shape

    def body(x_ref, o_ref, slots, send_sems, recv_sems):
        me, peers = _me_and_peers()
        sends = []
        for k, (dev, _) in enumerate(peers):
            cp = pltpu.make_async_remote_copy(src_ref=x_ref, dst_ref=slots.at[me], send_sem=send_sems.at[k],
                                              recv_sem=recv_sems.at[k], device_id=dev, device_id_type=MESH)
            cp.start()
            sends.append(cp)
        slots[me] = x_ref[...]
        for k, (dev, idx) in enumerate(peers):
            pltpu.make_async_remote_copy(src_ref=x_ref, dst_ref=slots.at[idx], send_sem=send_sems.at[k],
                                         recv_sem=recv_sems.at[k], device_id=dev, device_id_type=MESH).wait_recv()
        for cp in sends:
            cp.wait_send()
        acc = slots[0]
        for i in range(1, N_DEV):
            acc = acc + slots[i]
        o_ref[...] = acc

    return pl.pallas_call(
        body, name=name, out_shape=jax.ShapeDtypeStruct((r, c), F32),
        in_specs=[pl.BlockSpec(memory_space=pltpu.VMEM)], out_specs=pl.BlockSpec(memory_space=pltpu.VMEM),
        scratch_shapes=[pltpu.VMEM((N_DEV, r, c), F32), pltpu.SemaphoreType.DMA((N_DEV - 1,)),
                        pltpu.SemaphoreType.DMA((N_DEV - 1,))],
        compiler_params=pltpu.CompilerParams(has_side_effects=True, vmem_limit_bytes=VMEM_LIMIT),
    )(v)


def _local_step(x, target, lb_logits, p, weight, send_grad):
    g = {}
    l0, l1, l2 = lb_logits[0:1], lb_logits[1:2], lb_logits[2:3]
    x_bf = x.astype(BF16)

    proj0 = _mm(x_bf, weight("hg_w_in", x_bf), mode="nn", b_cs=True, o_cs=4, name="mm_hg_in")
    y0, states = _hg_fwd(proj0, l0, l1, l2, p["hg_norm_g"])
    mixed0 = _mm(y0, weight("hg_w_out", y0), mode="nn", name="mm_hg_out")
    hh_l, z_l = [], []

    def ffn_forward(layer, h_in_bf):
        hh = _mm(h_in_bf, weight(f"ffn_w_up{layer}", h_in_bf), mode="nn", b_cs=True, o_cs=2, name=f"mm_up{layer}")
        z = _ffn_fwd(hh, p["ffn_conv_w"][layer], p["ffn_conv_b"][layer])
        ffn = _mm(z, weight(f"ffn_w_down{layer}", z), mode="nn", name=f"mm_down{layer}")
        hh_l.append(hh)
        z_l.append(z)
        return ffn

    h1, h1b = _ln_fwd(x, mixed0, p["ln1_g"][0], p["ln1_b"][0], "ln1_0")
    ffn0 = ffn_forward(0, h1b)
    h2, h2b = _ln_fwd(h1, ffn0, p["ln2_g"][0], p["ln2_b"][0], "ln2_0")
    pre1 = _mm(h2b, weight("sg_w_in", h2b), mode="nn", b_cs=True, name="mm_sg_in")
    y1 = _sg_fwd(pre1, p["sg_ln_g"], p["sg_ln_b"], p["sg_w_s"], p["sg_bias"])
    mixed1 = _mm(y1, weight("sg_w_out", y1), mode="nn", name="mm_sg_out")
    h3, h3b = _ln_fwd(h2, mixed1, p["ln1_g"][1], p["ln1_b"][1], "ln1_1")
    ffn1 = ffn_forward(1, h3b)

    loss, ds, dres, g["ln2_g1"], g["ln2_b1"] = _ln_loss_bwd(h3, ffn1, target, p["ln2_g"][1], p["ln2_b"][1], "ln2_1_loss")

    def ffn_backward(layer, ds_bf, h_in_bf):
        tok = send_grad(f"ffn_w_down{layer}",
                        _mm(z_l[layer], ds_bf, mode="tn", out_dtype=BF16, name=f"mm_dw_down{layer}"))
        dz = _mm(ds_bf, weight(f"ffn_w_down{layer}", None), mode="nt", name=f"mm_d_z{layer}", deps=(tok,))
        dhh, g[f"conv_w{layer}"], g[f"conv_b{layer}"] = _ffn_bwd(hh_l[layer], dz, p["ffn_conv_w"][layer],
                                                                   p["ffn_conv_b"][layer])
        tok = send_grad(f"ffn_w_up{layer}", _mm(h_in_bf, dhh, mode="tn", b_cs=True, o_cs=N_DEV, out_dtype=BF16,
                                                name=f"mm_dw_up{layer}"))
        return _mm(dhh, weight(f"ffn_w_up{layer}", None), mode="nt", a_cs=True, b_cs=True, name=f"mm_d_up{layer}",
                   deps=(tok,))

    dh3 = ffn_backward(1, ds, h3b)
    ds, dres, g["ln1_g1"], g["ln1_b1"] = _ln_bwd(h2, mixed1, dres, dh3, p["ln1_g"][1], "ln1_1_bwd")
    tok = send_grad("sg_w_out", _mm(y1, ds, mode="tn", out_dtype=BF16, name="mm_dw_sg_out"))
    dy1 = _mm(ds, weight("sg_w_out", None), mode="nt", name="mm_d_sg_out", deps=(tok,))
    dpre1, g["sg_w_s"], dbs, g["sg_ln_g"], g["sg_ln_b"] = _sg_bwd(pre1, dy1, p["sg_ln_g"], p["sg_ln_b"],
                                                                    p["sg_w_s"], p["sg_bias"])
    g["sg_b_s"] = dbs[:, 0, :]
    tok = send_grad("sg_w_in", _mm(h2b, dpre1, mode="tn", o_cs=N_DEV, out_dtype=BF16, name="mm_dw_sg_in"))
    dh2 = _mm(dpre1, weight("sg_w_in", None), mode="nt", b_cs=True, name="mm_d_sg_in", deps=(tok,))
    ds, dres, g["ln2_g0"], g["ln2_b0"] = _ln_bwd(h1, ffn0, dres, dh2, p["ln2_g"][0], "ln2_0_bwd")
    dh1 = ffn_backward(0, ds, h1b)
    ds, dres, g["ln1_g0"], g["ln1_b0"] = _ln_bwd(x, mixed0, dres, dh1, p["ln1_g"][0], "ln1_0_bwd")
    tok = send_grad("hg_w_out", _mm(y0, ds, mode="tn", out_dtype=BF16, name="mm_dw_hg_out"))
    dy0 = _mm(ds, weight("hg_w_out", None), mode="nt", name="mm_d_hg_out", deps=(tok,))
    dproj, d0, d1, d2, g["hg_norm_g"] = _hg_bwd(proj0, states, dy0, l0, l1, l2, p["hg_norm_g"])
    g["lb_logits"] = jnp.concatenate([d0, d1, d2], axis=0)
    tok = send_grad("hg_w_in", _mm(x_bf, dproj, mode="tn", b_cs=True, o_cs=N_DEV, out_dtype=BF16, name="mm_dw_hg_in"))
    grad_x = _mm(dproj, weight("hg_w_in", None), mode="nt", a_cs=True, b_cs=True, addend=dres, name="mm_d_hg_in",
                 deps=(tok,))
    return loss[0, 0], grad_x, g


_SMALL = ("lb_logits", "hg_norm_g", "sg_ln_g", "sg_ln_b", "sg_w_s", "sg_b_s", "ffn_conv_w", "ffn_conv_b",
          "ln1_g", "ln1_b", "ln2_g", "ln2_b")
_NAMES = ("lb_logits", "hg_w_in", "hg_norm_g", "hg_w_out", "sg_w_in", "sg_ln_g", "sg_ln_b", "sg_w_s", "sg_b_s",
          "sg_w_out", "ffn_w_up", "ffn_conv_w", "ffn_conv_b", "ffn_w_down", "ln1_g", "ln1_b", "ln2_g", "ln2_b")


def kernel(x, lb_logits, hg_w_in, hg_norm_g, hg_w_out, sg_w_in, sg_ln_g, sg_ln_b, sg_w_s, sg_b_s, sg_w_out, ffn_w_up, ffn_conv_w, ffn_conv_b, ffn_w_down, ln1_g, ln1_b, ln2_g, ln2_b, loss_target, m_lb_logits, m_hg_w_in, m_hg_norm_g, m_hg_w_out, m_sg_w_in, m_sg_ln_g, m_sg_ln_b, m_sg_w_s, m_sg_b_s, m_sg_w_out, m_ffn_w_up, m_ffn_conv_w, m_ffn_conv_b, m_ffn_w_down, m_ln1_g, m_ln1_b, m_ln2_g, m_ln2_b, v_lb_logits, v_hg_w_in, v_hg_norm_g, v_hg_w_out, v_sg_w_in, v_sg_ln_g, v_sg_ln_b, v_sg_w_s, v_sg_b_s, v_sg_w_out, v_ffn_w_up, v_ffn_conv_w, v_ffn_conv_b, v_ffn_w_down, v_ln1_g, v_ln1_b, v_ln2_g, v_ln2_b):
    w = dict(lb_logits=lb_logits, hg_w_in=hg_w_in, hg_norm_g=hg_norm_g, hg_w_out=hg_w_out, sg_w_in=sg_w_in,
             sg_ln_g=sg_ln_g, sg_ln_b=sg_ln_b, sg_w_s=sg_w_s, sg_b_s=sg_b_s, sg_w_out=sg_w_out, ffn_w_up=ffn_w_up,
             ffn_conv_w=ffn_conv_w, ffn_conv_b=ffn_conv_b, ffn_w_down=ffn_w_down, ln1_g=ln1_g, ln1_b=ln1_b,
             ln2_g=ln2_g, ln2_b=ln2_b)
    m = dict(lb_logits=m_lb_logits, hg_w_in=m_hg_w_in, hg_norm_g=m_hg_norm_g, hg_w_out=m_hg_w_out,
             sg_w_in=m_sg_w_in, sg_ln_g=m_sg_ln_g, sg_ln_b=m_sg_ln_b, sg_w_s=m_sg_w_s, sg_b_s=m_sg_b_s,
             sg_w_out=m_sg_w_out, ffn_w_up=m_ffn_w_up, ffn_conv_w=m_ffn_conv_w, ffn_conv_b=m_ffn_conv_b,
             ffn_w_down=m_ffn_w_down, ln1_g=m_ln1_g, ln1_b=m_ln1_b, ln2_g=m_ln2_g, ln2_b=m_ln2_b)
    v = dict(lb_logits=v_lb_logits, hg_w_in=v_hg_w_in, hg_norm_g=v_hg_norm_g, hg_w_out=v_hg_w_out,
             sg_w_in=v_sg_w_in, sg_ln_g=v_sg_ln_g, sg_ln_b=v_sg_ln_b, sg_w_s=v_sg_w_s, sg_b_s=v_sg_b_s,
             sg_w_out=v_sg_w_out, ffn_w_up=v_ffn_w_up, ffn_conv_w=v_ffn_conv_w, ffn_conv_b=v_ffn_conv_b,
             ffn_w_down=v_ffn_w_down, ln1_g=v_ln1_g, ln1_b=v_ln1_b, ln2_g=v_ln2_g, ln2_b=v_ln2_b)
    me = 4 * lax.axis_index("x") + 2 * lax.axis_index("y") + lax.axis_index("c")
    d = D_MODEL

    shards = [("hg_w_in", hg_w_in[0]), ("hg_w_out", hg_w_out[0]), ("ffn_w_up0", ffn_w_up[0]),
              ("ffn_w_down0", ffn_w_down[0]), ("sg_w_in", sg_w_in[0]), ("sg_w_out", sg_w_out[0]),
              ("ffn_w_up1", ffn_w_up[1]), ("ffn_w_down1", ffn_w_down[1])]
    row_sharded = {"hg_w_out": (d, d), "sg_w_out": (d, d), "ffn_w_down0": (D_FF, d), "ffn_w_down1": (D_FF, d)}
    gathers, gathered, dep = {}, {}, lb_logits
    for n, shard in shards:
        gathers[n] = _xchg_start(shard.astype(BF16), gather=True, dep=dep, name=f"ag_{n}")
        dep = gathers[n][4]

    def weight(n, after):
        if n not in gathered:
            full = _xchg_wait(gathers[n], after, f"agw_{n}")
            gathered[n] = full.reshape(row_sharded[n]) if n in row_sharded else full
        return gathered[n]

    grad_sends = {}

    def send_grad(n, parts):
        shard_shape = dict(shards)[n].shape
        grad_sends[n] = _xchg_start(parts.reshape((N_DEV,) + shard_shape), gather=False, dep=parts, name=f"rs_{n}")
        return grad_sends[n][4]

    p = {}
    sv = jnp.zeros((8, 768), F32)
    sv = sv.at[0, :256].set(sg_ln_g[0]).at[1, :256].set(sg_ln_b[0]).at[2:8, :704].set(ffn_conv_w.reshape(6, 704))
    sv = _all_gather(sv, "ag_small")
    p["sg_ln_g"] = sv[:, 0, :256].reshape(1, d)
    p["sg_ln_b"] = sv[:, 1, :256].reshape(1, d)
    conv_w_full = jnp.transpose(sv[:, 2:8, :704].reshape(N_DEV, DEPTH, 3, 704), (1, 2, 0, 3)).reshape(DEPTH, 3, D_FF)
    p["ffn_conv_w"] = [conv_w_full[l] for l in range(DEPTH)]
    p["ffn_conv_b"] = [ffn_conv_b[l:l + 1] for l in range(DEPTH)]
    p["hg_norm_g"] = hg_norm_g
    p["sg_w_s"] = sg_w_s[0]
    p["sg_bias"] = jnp.broadcast_to(sg_b_s[0][:, :, None], (SG_GROUPS, SG_CHUNK, SG_DIM))
    for n in ("ln1_g", "ln1_b", "ln2_g", "ln2_b"):
        p[n] = [w[n][l:l + 1] for l in range(DEPTH)]

    loss_part, grad_x, g = _local_step(x[0], loss_target[0], lb_logits, p, weight, send_grad)
    loss = lax.psum(loss_part, ("x", "y", "c"))

    small = [g["lb_logits"], g["hg_norm_g"], g["sg_ln_g"], g["sg_ln_b"], g["sg_w_s"], g["sg_b_s"],
             g["conv_w0"], g["conv_w1"], g["conv_b0"], g["conv_b1"],
             g["ln1_g0"], g["ln1_g1"], g["ln1_b0"], g["ln1_b1"], g["ln2_g0"], g["ln2_g1"], g["ln2_b0"], g["ln2_b1"]]
    sizes = [int(np.prod(a.shape)) for a in small]
    flat = jnp.concatenate([a.reshape(-1) for a in small])
    total = flat.shape[0]
    rows = -(-total // 128)
    rows = -(-rows // 8) * 8
    flat = jnp.pad(flat, (0, rows * 128 - total)).reshape(rows, 128)
    red = _all_reduce_small(flat, "ar_small").reshape(-1)
    offs = np.cumsum([0] + sizes)
    r = [red[offs[i]:offs[i + 1]].reshape(small[i].shape) for i in range(len(small))]
    gs = {}
    gs["lb_logits"] = r[0]
    gs["hg_norm_g"] = r[1]
    gs["sg_ln_g"] = lax.dynamic_slice(r[2], (0, me * 256), (1, 256))
    gs["sg_ln_b"] = lax.dynamic_slice(r[3], (0, me * 256), (1, 256))
    gs["sg_w_s"] = r[4][None]
    gs["sg_b_s"] = r[5][None]
    gs["ffn_conv_w"] = lax.dynamic_slice(jnp.stack([r[6], r[7]]), (0, 0, me * 704), (DEPTH, 3, 704))
    gs["ffn_conv_b"] = jnp.concatenate([r[8], r[9]], axis=0)
    gs["ln1_g"] = jnp.concatenate([r[10], r[11]], axis=0)
    gs["ln1_b"] = jnp.concatenate([r[12], r[13]], axis=0)
    gs["ln2_g"] = jnp.concatenate([r[14], r[15]], axis=0)
    gs["ln2_b"] = jnp.concatenate([r[16], r[17]], axis=0)

    out_g, out_d, out_m, out_v = {}, {}, {}, {}
    for n in _SMALL:
        out_g[n], out_d[n], out_m[n], out_v[n] = _adamw(w[n], gs[n], m[n], v[n], f"adamw_{n}")

    res, after = {}, out_v["ln2_b"]
    for n in ("ffn_w_down1", "ffn_w_up1", "sg_w_out", "sg_w_in", "ffn_w_down0", "ffn_w_up0", "hg_w_out", "hg_w_in"):
        recv = _xchg_wait(grad_sends[n], after, f"rsw_{n}")
        base, layer = (n[:-1], int(n[-1])) if n[-1] in "01" else (n, 0)
        res[n] = _adamw(w[base][layer], None, m[base][layer], v[base][layer], f"adamw_{n}", parts=recv)
        after = res[n][3]
    for n in ("hg_w_in", "hg_w_out", "sg_w_in", "sg_w_out"):
        out_g[n], out_d[n], out_m[n], out_v[n] = (a[None] for a in res[n])
    for n in ("ffn_w_up", "ffn_w_down"):
        out_g[n], out_d[n], out_m[n], out_v[n] = (jnp.stack([res[n + "0"][i], res[n + "1"][i]]) for i in range(4))

    return (loss, grad_x[None], *[out_g[n] for n in _NAMES], *[out_d[n] for n in _NAMES],
            *[out_m[n] for n in _NAMES], *[out_v[n] for n in _NAMES])
```

```python
import functools

import numpy as np
import jax
import jax.numpy as jnp
from jax import lax
from jax.experimental import pallas as pl
from jax.experimental.pallas import tpu as pltpu

F32 = jnp.float32
BF16 = jnp.bfloat16
HI = lax.Precision.HIGHEST

N_DEV = 8
D_MODEL = 2048
HG_HEADS = 16
HG_DIM = 128
HG_CHUNK = 64
SG_GROUPS = 16
SG_DIM = 128
SG_CHUNK = 128
D_FF = 5632
DEPTH = 2
ALPHA = (2 * DEPTH) ** 0.25
LN_EPS = 1e-5
RMS_EPS = 1e-6
ADAM_LR = 0.001
ADAM_B1 = 0.9
ADAM_B2 = 0.999
ADAM_EPS = 1e-08
ADAM_WD = 0.01
ADAM_STEP = 10

VMEM_LIMIT = 56 * 1024 * 1024
MESH = pl.DeviceIdType.MESH
ANY = pl.BlockSpec(memory_space=pl.ANY)


def _cp(*sem):
    return pltpu.CompilerParams(dimension_semantics=sem, vmem_limit_bytes=VMEM_LIMIT)


def _pick(n, cands):
    for c in cands:
        if n % c == 0:
            return c
    raise ValueError(f"no tile for {n} in {cands}")


_DIMS = {"nn": (((1,), (0,)), ((), ())), "nt": (((1,), (1,)), ((), ())), "tn": (((0,), (0,)), ((), ()))}


def _mm(a, b, *, mode, name, out_dtype=F32, a_cs=False, b_cs=False, o_cs=None, addend=None,
        bm=None, bn=None, bk=None, deps=()):
    if mode == "tn":
        kk, m = a.shape
    elif a_cs:
        m, kk = a.shape[1], a.shape[0] * a.shape[2]
    else:
        m, kk = a.shape
    if mode == "nt":
        n = b.shape[1] if b_cs else b.shape[0]
    else:
        n = b.shape[0] * b.shape[2] if b_cs else b.shape[1]
    a_c = a.shape[2] if a_cs else None
    b_c = b.shape[2] if b_cs else None
    o_c = n // o_cs if o_cs else None
    bm = bm or _pick(m, (1024, 1408, 512, 256))
    n_div = [c for c in (b_c if (b_cs and mode != "nt") else None, o_c) if c]
    k_div = [c for c in (a_c, b_c if (b_cs and mode == "nt") else None) if c]

    def pick_with(total, divs, cands):
        for c in cands:
            if total % c == 0 and all(d % c == 0 for d in divs):
                return c
        raise ValueError((total, divs))

    bn = bn or pick_with(n, n_div, (1024, 1408, 512, 256, 128))
    bk = bk or pick_with(kk, k_div, (512, 1408, 1024, 256, 128))
    nk = kk // bk

    def cs_idx(blk, per):
        return (blk * per[0]) // per[1], (blk * per[0] % per[1]) // per[0]

    if mode == "tn":
        a_spec = pl.BlockSpec((bk, bm), lambda i, j, k: (k, i))
    elif a_cs:
        def a_map(i, j, k):
            s, r = cs_idx(k, (bk, a_c))
            return (s, i, r)
        a_spec = pl.BlockSpec((None, bm, bk), a_map)
    else:
        a_spec = pl.BlockSpec((bm, bk), lambda i, j, k: (i, k))
    if mode == "nt":
        if b_cs:
            def b_map(i, j, k):
                s, r = cs_idx(k, (bk, b_c))
                return (s, j, r)
            b_spec = pl.BlockSpec((None, bn, bk), b_map)
        else:
            b_spec = pl.BlockSpec((bn, bk), lambda i, j, k: (j, k))
    else:
        if b_cs:
            def b_map(i, j, k):
                s, r = cs_idx(j, (bn, b_c))
                return (s, k, r)
            b_spec = pl.BlockSpec((None, bk, bn), b_map)
        else:
            b_spec = pl.BlockSpec((bk, bn), lambda i, j, k: (k, j))
    if o_cs:
        def o_map(i, j, k):
            s, r = cs_idx(j, (bn, o_c))
            return (s, i, r)
        o_spec = pl.BlockSpec((None, bm, bn), o_map)
        out_shape = jax.ShapeDtypeStruct((o_cs, m, o_c), out_dtype)
    else:
        o_spec = pl.BlockSpec((bm, bn), lambda i, j, k: (i, j))
        out_shape = jax.ShapeDtypeStruct((m, n), out_dtype)
    in_specs = [a_spec, b_spec]
    args = [a, b]
    if addend is not None:
        in_specs.append(pl.BlockSpec((bm, bn), lambda i, j, k: (i, j)))
        args.append(addend)
    dims = _DIMS[mode]
    in_specs += [ANY] * len(deps)
    args += list(deps)

    def body(*refs):
        refs = refs[:len(refs) - 2 - len(deps)] + refs[len(refs) - 2:]
        if addend is not None:
            a_ref, b_ref, add_ref, o_ref, acc_ref = refs
        else:
            a_ref, b_ref, o_ref, acc_ref = refs
        k = pl.program_id(2)

        @pl.when(k == 0)
        def _():
            acc_ref[...] = jnp.zeros_like(acc_ref)

        acc_ref[...] += lax.dot_general(a_ref[...].astype(BF16), b_ref[...].astype(BF16), dims,
                                        preferred_element_type=F32)

        @pl.when(k == nk - 1)
        def _():
            r = acc_ref[...]
            if addend is not None:
                r = r + add_ref[...]
            o_ref[...] = r.astype(o_ref.dtype)

    return pl.pallas_call(
        body, name=name, out_shape=out_shape, grid=(m // bm, n // bn, nk),
        in_specs=in_specs, out_specs=o_spec, scratch_shapes=[pltpu.VMEM((bm, bn), F32)],
        compiler_params=_cp("parallel", "parallel", "arbitrary"),
    )(*args)


LN_ROWS = 256


def _ln_stats(s):
    mu = jnp.mean(s, axis=-1, keepdims=True)
    sc = s - mu
    var = jnp.mean(sc * sc, axis=-1, keepdims=True)
    rstd = lax.rsqrt(var + LN_EPS)
    return sc * rstd, rstd


def _ln_fwd(h, sub, g, b, name):
    t, d = h.shape
    row = pl.BlockSpec((LN_ROWS, d), lambda i: (i, 0))
    vec = pl.BlockSpec((1, d), lambda i: (0, 0))

    def body(h_ref, s_ref, g_ref, b_ref, y_ref, yb_ref):
        xhat, _ = _ln_stats(ALPHA * h_ref[...] + s_ref[...])
        y = xhat * g_ref[...] + b_ref[...]
        y_ref[...] = y
        yb_ref[...] = y.astype(BF16)

    return pl.pallas_call(
        body, name=name, grid=(t // LN_ROWS,),
        out_shape=(jax.ShapeDtypeStruct((t, d), F32), jax.ShapeDtypeStruct((t, d), BF16)),
        in_specs=[row, row, vec, vec], out_specs=(row, row), compiler_params=_cp("parallel"),
    )(h, sub, g, b)


def _ln_bwd_math(xhat, rstd, dy, g):
    dxhat = dy * g
    m1 = jnp.mean(dxhat, axis=-1, keepdims=True)
    m2 = jnp.mean(dxhat * xhat, axis=-1, keepdims=True)
    ds = rstd * (dxhat - m1 - xhat * m2)
    dg = jnp.sum(dy * xhat, axis=0, keepdims=True)
    db = jnp.sum(dy, axis=0, keepdims=True)
    return ds, dg, db


def _ln_bwd(h, sub, dy_a, dy_b, g, name):
    t, d = h.shape
    row = pl.BlockSpec((LN_ROWS, d), lambda i: (i, 0))
    vec = pl.BlockSpec((1, d), lambda i: (0, 0))

    def body(h_ref, s_ref, da_ref, db_ref, g_ref, ds_ref, dres_ref, dg_ref, dbeta_ref):
        xhat, rstd = _ln_stats(ALPHA * h_ref[...] + s_ref[...])
        ds, dg, db = _ln_bwd_math(xhat, rstd, da_ref[...] + db_ref[...], g_ref[...])
        ds_ref[...] = ds.astype(BF16)
        dres_ref[...] = ALPHA * ds

        @pl.when(pl.program_id(0) == 0)
        def _():
            dg_ref[...] = jnp.zeros_like(dg_ref)
            dbeta_ref[...] = jnp.zeros_like(dbeta_ref)

        dg_ref[...] += dg
        dbeta_ref[...] += db

    return pl.pallas_call(
        body, name=name, grid=(t // LN_ROWS,),
        out_shape=(jax.ShapeDtypeStruct((t, d), BF16), jax.ShapeDtypeStruct((t, d), F32),
                   jax.ShapeDtypeStruct((1, d), F32), jax.ShapeDtypeStruct((1, d), F32)),
        in_specs=[row, row, row, row, vec], out_specs=(row, row, vec, vec),
        compiler_params=_cp("arbitrary"),
    )(h, sub, dy_a, dy_b, g)


def _ln_loss_bwd(h, sub, target, g, b, name):
    t, d = h.shape
    row = pl.BlockSpec((LN_ROWS, d), lambda i: (i, 0))
    vec = pl.BlockSpec((1, d), lambda i: (0, 0))
    lvec = pl.BlockSpec((1, 128), lambda i: (0, 0))

    def body(h_ref, s_ref, t_ref, g_ref, b_ref, loss_ref, ds_ref, dres_ref, dg_ref, dbeta_ref):
        xhat, rstd = _ln_stats(ALPHA * h_ref[...] + s_ref[...])
        y = xhat * g_ref[...] + b_ref[...]
        err = y - t_ref[...]
        part = 0.5 * jnp.sum(jnp.mean(err * err, axis=-1, keepdims=True), axis=0, keepdims=True)
        ds, dg, db = _ln_bwd_math(xhat, rstd, err * (1.0 / d), g_ref[...])
        ds_ref[...] = ds.astype(BF16)
        dres_ref[...] = ALPHA * ds

        @pl.when(pl.program_id(0) == 0)
        def _():
            loss_ref[...] = jnp.zeros_like(loss_ref)
            dg_ref[...] = jnp.zeros_like(dg_ref)
            dbeta_ref[...] = jnp.zeros_like(dbeta_ref)

        loss_ref[...] += jnp.broadcast_to(part, loss_ref.shape)
        dg_ref[...] += dg
        dbeta_ref[...] += db

    return pl.pallas_call(
        body, name=name, grid=(t // LN_ROWS,),
        out_shape=(jax.ShapeDtypeStruct((1, 128), F32), jax.ShapeDtypeStruct((t, d), BF16),
                   jax.ShapeDtypeStruct((t, d), F32), jax.ShapeDtypeStruct((1, d), F32),
                   jax.ShapeDtypeStruct((1, d), F32)),
        in_specs=[row, row, row, vec, vec], out_specs=(lvec, row, row, vec, vec),
        compiler_params=_cp("arbitrary"),
    )(h, sub, target, g, b)


HG_ROWS = 512
HG_LEVELS = 6


def _hg_constants():
    c = HG_CHUNK
    t = np.arange(c)[:, None]
    j = np.arange(c)[None, :]
    blocks, masks = [], [np.eye(c)]
    for lev in range(HG_LEVELS):
        m = 1 << lev
        second = (t % (2 * m)) >= m
        mid = (t // (2 * m)) * (2 * m) + m - 1
        blocks.append((second & (j > mid) & (j <= t)) | ((~second) & (j > t) & (j <= mid)))
        same = (t // (2 * m)) == (j // (2 * m))
        masks.append(same & second & ((j % (2 * m)) < m))
    blocks += [j <= t, j > t, np.ones((c, c), bool)]
    cm = np.concatenate(blocks, axis=0).astype(np.float32)
    mk = np.concatenate(masks, axis=0).astype(np.float32)
    return jnp.asarray(cm, dtype=BF16), jnp.asarray(mk)


def _split3(x):
    hi = x.astype(BF16)
    r = x - hi.astype(F32)
    mid = r.astype(BF16)
    lo = (r - mid.astype(F32)).astype(BF16)
    return hi, mid, lo


def _sum3(x):
    n = x.shape[-1] // 3
    return x[:, :n] + x[:, n:2 * n] + x[:, 2 * n:]


@jax.custom_vjp
def _prefix(cm, lf):
    return _sum3(lax.dot_general(cm, jnp.concatenate(_split3(lf), axis=1), _DIMS["nn"], preferred_element_type=F32))


def _prefix_fwd(cm, lf):
    return _prefix(cm, lf), cm


def _prefix_bwd(cm, d):
    d_lf = _sum3(lax.dot_general(cm, jnp.concatenate(_split3(d), axis=1), _DIMS["tn"], preferred_element_type=F32))
    return jnp.zeros_like(cm), d_lf


_prefix.defvjp(_prefix_fwd, _prefix_bwd)

_BWD = {"nn": (("nt", 0, 1), ("tn", 1, 0)), "nt": (("nn", 0, 1), ("tn", 0, 1)), "tn": (("nt", 1, 0), ("nn", 1, 0))}


def _bdot_raw(a, b, mode):
    return lax.dot_general(a.astype(BF16), b.astype(BF16), _DIMS[mode], preferred_element_type=F32)


@functools.partial(jax.custom_vjp, nondiff_argnums=(2,))
def _bdot(a, b, mode):
    return _bdot_raw(a, b, mode)


def _bdot_fwd(a, b, mode):
    return _bdot_raw(a, b, mode), (a, b)


def _bdot_bwd(mode, res, d):
    a, b = res
    (ma, da_pos, _), (mb, db_pos, _) = _BWD[mode]
    da = _bdot_raw(d, b, ma) if da_pos == 0 else _bdot_raw(b, d, ma)
    db = _bdot_raw(d, a, mb) if db_pos == 0 else _bdot_raw(a, d, mb)
    return da, db


_bdot.defvjp(_bdot_fwd, _bdot_bwd)


def _hg_chunk(pq, pf, pi, pg, l0, l1, l2, ng, s_t, cm, mk):
    c = HG_CHUNK
    mx = jnp.maximum(jnp.maximum(l0, l1), l2)
    e0, e1, e2 = jnp.exp(l0 - mx), jnp.exp(l1 - mx), jnp.exp(l2 - mx)
    lb = e0 / (e0 + e1 + e2)
    q = pq * jax.nn.sigmoid(pq)
    t1 = jnp.log(lb)
    t2 = jnp.log1p(-lb) + jax.nn.log_sigmoid(pf)
    lf = jnp.maximum(t1, t2) + jnp.log1p(jnp.exp(-jnp.abs(t1 - t2)))
    k = (1.0 - lb) * jax.nn.sigmoid(-pf)
    x = jnp.exp(_prefix(cm, lf))
    scores = mk[0:c] * _bdot(q, k, "nt")
    for lev in range(HG_LEVELS):
        xl = x[lev * c:(lev + 1) * c]
        scores = scores + mk[(lev + 1) * c:(lev + 2) * c] * _bdot(q * xl, k * xl, "nt")
    x_incl = x[HG_LEVELS * c:(HG_LEVELS + 1) * c]
    x_after = x[(HG_LEVELS + 1) * c:(HG_LEVELS + 2) * c]
    x_total = x[(HG_LEVELS + 2) * c:(HG_LEVELS + 3) * c]
    o = _bdot(scores, pi, "nn") + _bdot(q * x_incl, s_t, "nt")
    s_new = s_t * jnp.concatenate([x_total, x_total], axis=0) + _bdot(pi, k * x_after, "tn")
    rstd = lax.rsqrt(jnp.mean(o * o, axis=-1, keepdims=True) + RMS_EPS)
    y = o * rstd * ng * (pg * jax.nn.sigmoid(pg))
    return y, s_new


def _hg_specs(t, reverse):
    nrb = t // HG_ROWS
    rb = (lambda r: nrb - 1 - r) if reverse else (lambda r: r)
    proj = pl.BlockSpec((4, HG_ROWS, HG_DIM), lambda h, r: (0, rb(r), h))
    vec = pl.BlockSpec((1, HG_DIM), lambda h, r: (0, h))
    cm = pl.BlockSpec(((HG_LEVELS + 3) * HG_CHUNK, HG_CHUNK), lambda h, r: (0, 0))
    mk = pl.BlockSpec(((HG_LEVELS + 1) * HG_CHUNK, HG_CHUNK), lambda h, r: (0, 0))
    rows = pl.BlockSpec((HG_ROWS, HG_DIM), lambda h, r: (rb(r), h))
    states = pl.BlockSpec((None, HG_ROWS // HG_CHUNK, HG_DIM, HG_DIM), lambda h, r: (h, rb(r), 0, 0))
    return proj, vec, cm, mk, rows, states


def _hg_fwd(proj, l0, l1, l2, ng):
    t = proj.shape[1]
    n_in = HG_ROWS // HG_CHUNK
    cm, mk = _hg_constants()
    p_spec, vec, cm_spec, mk_spec, rows, st_spec = _hg_specs(t, False)

    def body(p_ref, l0_ref, l1_ref, l2_ref, ng_ref, cm_ref, mk_ref, y_ref, st_ref, s_ref):
        @pl.when(pl.program_id(1) == 0)
        def _():
            s_ref[...] = jnp.zeros_like(s_ref)

        def step(ci, carry):
            sl = pl.ds(pl.multiple_of(ci * HG_CHUNK, HG_CHUNK), HG_CHUNK)
            s_t = s_ref[...]
            st_ref[ci] = s_t
            y, s_new = _hg_chunk(p_ref[0, sl, :], p_ref[1, sl, :], p_ref[2, sl, :], p_ref[3, sl, :],
                                 l0_ref[...], l1_ref[...], l2_ref[...], ng_ref[...], s_t,
                                 cm_ref[...], mk_ref[...])
            y_ref[sl, :] = y.astype(BF16)
            s_ref[...] = s_new
            return carry

        lax.fori_loop(0, n_in, step, 0, unroll=8)

    return pl.pallas_call(
        body, name="hg_fwd", grid=(HG_HEADS, t // HG_ROWS),
        out_shape=(jax.ShapeDtypeStruct((t, D_MODEL), BF16),
                   jax.ShapeDtypeStruct((HG_HEADS, t // HG_CHUNK, HG_DIM, HG_DIM), F32)),
        in_specs=[p_spec, vec, vec, vec, vec, cm_spec, mk_spec], out_specs=(rows, st_spec),
        scratch_shapes=[pltpu.VMEM((HG_DIM, HG_DIM), F32)],
        compiler_params=_cp("parallel", "arbitrary"),
    )(proj, l0, l1, l2, ng, cm, mk)


def _hg_bwd(proj, states, dy, l0, l1, l2, ng):
    t = proj.shape[1]
    n_in = HG_ROWS // HG_CHUNK
    cm, mk = _hg_constants()
    p_spec, vec, cm_spec, mk_spec, rows, st_spec = _hg_specs(t, True)

    def body(p_ref, st_ref, dy_ref, l0_ref, l1_ref, l2_ref, ng_ref, cm_ref, mk_ref,
             dp_ref, dl0_ref, dl1_ref, dl2_ref, dng_ref, ds_ref):
        @pl.when(pl.program_id(1) == 0)
        def _():
            ds_ref[...] = jnp.zeros_like(ds_ref)
            for r in (dl0_ref, dl1_ref, dl2_ref, dng_ref):
                r[...] = jnp.zeros_like(r)

        def step(it, carry):
            ci = n_in - 1 - it
            sl = pl.ds(pl.multiple_of(ci * HG_CHUNK, HG_CHUNK), HG_CHUNK)
            fn = functools.partial(_hg_chunk, cm=cm_ref[...], mk=mk_ref[...])
            _, vjp = jax.vjp(fn, p_ref[0, sl, :], p_ref[1, sl, :], p_ref[2, sl, :], p_ref[3, sl, :],
                             l0_ref[...], l1_ref[...], l2_ref[...], ng_ref[...], st_ref[ci])
            dq, df, di, dg, d0, d1, d2, dn, ds = vjp((dy_ref[sl, :], ds_ref[...]))
            dp_ref[0, sl, :] = dq.astype(BF16)
            dp_ref[1, sl, :] = df.astype(BF16)
            dp_ref[2, sl, :] = di.astype(BF16)
            dp_ref[3, sl, :] = dg.astype(BF16)
            dl0_ref[...] += d0
            dl1_ref[...] += d1
            dl2_ref[...] += d2
            dng_ref[...] += dn
            ds_ref[...] = ds
            return carry

        lax.fori_loop(0, n_in, step, 0, unroll=8)

    v_shape = jax.ShapeDtypeStruct((1, D_MODEL), F32)
    return pl.pallas_call(
        body, name="hg_bwd", grid=(HG_HEADS, t // HG_ROWS),
        out_shape=(jax.ShapeDtypeStruct((4, t, D_MODEL), BF16), v_shape, v_shape, v_shape, v_shape),
        in_specs=[p_spec, st_spec, rows, vec, vec, vec, vec, cm_spec, mk_spec],
        out_specs=(p_spec, vec, vec, vec, vec),
        scratch_shapes=[pltpu.VMEM((HG_DIM, HG_DIM), F32)],
        compiler_params=_cp("parallel", "arbitrary"),
    )(proj, states, dy, l0, l1, l2, ng, cm, mk)


_SQRT_HALF = 0.7071067811865476
_INV_SQRT_2PI = 0.3989422804014327


def _gelu(x):
    return 0.5 * x * (1.0 + lax.erf(x * _SQRT_HALF))


def _gelu_grad(x):
    return 0.5 * (1.0 + lax.erf(x * _SQRT_HALF)) + x * (_INV_SQRT_2PI * jnp.exp(-0.5 * x * x))


def _tril(n):
    return (lax.broadcasted_iota(jnp.int32, (n, n), 0) >= lax.broadcasted_iota(jnp.int32, (n, n), 1)).astype(F32)


def _sg_specs(d):
    row = lambda w: pl.BlockSpec((SG_CHUNK, w), lambda i: (i, 0))
    vec = pl.BlockSpec((1, d), lambda i: (0, 0))
    cube = pl.BlockSpec((SG_GROUPS, SG_CHUNK, SG_CHUNK), lambda i: (0, 0, 0))
    return row, vec, cube


def _sg_fwd(pre, ln_g, ln_b, w_s, bias):
    t = pre.shape[0]
    d = D_MODEL
    row, vec, cube = _sg_specs(d)

    def body(pre_ref, g_ref, b_ref, w_ref, bias_ref, y_ref, vln_ref):
        u = _gelu(pre_ref[:, :d])
        vhat, _ = _ln_stats(_gelu(pre_ref[:, d:]))
        vln_ref[...] = vhat * g_ref[...] + b_ref[...]
        tril = _tril(SG_CHUNK)
        for g in range(SG_GROUPS):
            cs = slice(g * SG_DIM, (g + 1) * SG_DIM)
            gate = _bdot_raw(w_ref[g] * tril, vln_ref[:, cs], "nn") + bias_ref[g]
            y_ref[:, cs] = (u[:, cs] * gate).astype(BF16)

    return pl.pallas_call(
        body, name="sg_fwd", grid=(t // SG_CHUNK,), out_shape=jax.ShapeDtypeStruct((t, d), BF16),
        in_specs=[row(2 * d), vec, vec, cube, cube], out_specs=row(d),
        scratch_shapes=[pltpu.VMEM((SG_CHUNK, d), F32)], compiler_params=_cp("parallel"),
    )(pre, ln_g, ln_b, w_s, bias)


def _sg_bwd(pre, dy, ln_g, ln_b, w_s, bias):
    t = pre.shape[0]
    d = D_MODEL
    row, vec, cube = _sg_specs(d)
    dbs_spec = pl.BlockSpec((SG_GROUPS, 8, SG_CHUNK), lambda i: (0, 0, 0))

    def body(pre_ref, dy_ref, g_ref, b_ref, w_ref, bias_ref, dpre_ref, dw_ref, dbs_ref, dlg_ref, dlb_ref,
             vln_ref, dvln_ref):
        @pl.when(pl.program_id(0) == 0)
        def _():
            for r in (dw_ref, dbs_ref, dlg_ref, dlb_ref):
                r[...] = jnp.zeros_like(r)

        pu = pre_ref[:, :d]
        pv = pre_ref[:, d:]
        u = _gelu(pu)
        vhat, rstd = _ln_stats(_gelu(pv))
        vln_ref[...] = vhat * g_ref[...] + b_ref[...]
        tril = _tril(SG_CHUNK)
        ones = jnp.ones((8, SG_DIM), F32)
        for g in range(SG_GROUPS):
            cs = slice(g * SG_DIM, (g + 1) * SG_DIM)
            wc = w_ref[g] * tril
            vg = vln_ref[:, cs]
            gate = _bdot_raw(wc, vg, "nn") + bias_ref[g]
            dyg = dy_ref[:, cs]
            dgate = dyg * u[:, cs]
            dpre_ref[:, cs] = (dyg * gate * _gelu_grad(pu[:, cs])).astype(BF16)
            dw_ref[g] += tril * _bdot_raw(dgate, vg, "nt")
            dbs_ref[g] += lax.dot_general(ones, dgate, _DIMS["nt"], precision=HI, preferred_element_type=F32)
            dvln_ref[:, cs] = _bdot_raw(wc, dgate, "tn")
        dvln = dvln_ref[...]
        dv, dg, db = _ln_bwd_math(vhat, rstd, dvln, g_ref[...])
        dlg_ref[...] += dg
        dlb_ref[...] += db
        dpre_ref[:, d:] = (dv * _gelu_grad(pv)).astype(BF16)

    return pl.pallas_call(
        body, name="sg_bwd", grid=(t // SG_CHUNK,),
        out_shape=(jax.ShapeDtypeStruct((t, 2 * d), BF16), jax.ShapeDtypeStruct(w_s.shape, F32),
                   jax.ShapeDtypeStruct((SG_GROUPS, 8, SG_CHUNK), F32),
                   jax.ShapeDtypeStruct((1, d), F32), jax.ShapeDtypeStruct((1, d), F32)),
        in_specs=[row(2 * d), row(d), vec, vec, cube, cube],
        out_specs=(row(2 * d), cube, dbs_spec, vec, vec),
        scratch_shapes=[pltpu.VMEM((SG_CHUNK, d), F32), pltpu.VMEM((SG_CHUNK, d), F32)],
        compiler_params=_cp("arbitrary"),
    )(pre, dy, ln_g, ln_b, w_s, bias)


FFN_ROWS = 256
FFN_COLS = 1408
HALO = 8


def _ffn_conv(a_ext, w_ref, cb_ref):
    a1 = pltpu.roll(a_ext, 1, 0)
    a2 = pltpu.roll(a_ext, 2, 0)
    return w_ref[0:1, :] * a2 + w_ref[1:2, :] * a1 + w_ref[2:3, :] * a_ext + cb_ref[...], a1, a2


def _ffn_fwd(hh, conv_w, conv_b):
    t = hh.shape[1]
    nb8 = FFN_ROWS // HALO
    main = pl.BlockSpec((2, FFN_ROWS, FFN_COLS), lambda c, r: (0, r, c))
    prev = pl.BlockSpec((None, HALO, FFN_COLS), lambda c, r: (0, jnp.maximum(r * nb8 - 1, 0), c))
    wspec = pl.BlockSpec((3, FFN_COLS), lambda c, r: (0, c))
    bspec = pl.BlockSpec((1, FFN_COLS), lambda c, r: (0, c))

    def body(m_ref, p_ref, w_ref, cb_ref, z_ref):
        prev = jnp.where(pl.program_id(1) == 0, 0.0, p_ref[...])
        a_ext = jnp.concatenate([prev, m_ref[0]], axis=0)
        a, _, _ = _ffn_conv(a_ext, w_ref, cb_ref)
        a = a[HALO:]
        z_ref[...] = (a * jax.nn.sigmoid(a) * m_ref[1]).astype(BF16)

    return pl.pallas_call(
        body, name="ffn_fwd", grid=(D_FF // FFN_COLS, t // FFN_ROWS),
        out_shape=jax.ShapeDtypeStruct((t, D_FF), BF16),
        in_specs=[main, prev, wspec, bspec],
        out_specs=pl.BlockSpec((FFN_ROWS, FFN_COLS), lambda c, r: (r, c)),
        compiler_params=_cp("parallel", "parallel"),
    )(hh, hh, conv_w, conv_b)


def _ffn_bwd(hh, dz, conv_w, conv_b):
    t = hh.shape[1]
    nb8 = FFN_ROWS // HALO
    last8 = t // HALO - 1
    nr = t // FFN_ROWS
    main = pl.BlockSpec((2, FFN_ROWS, FFN_COLS), lambda c, r: (0, r, c))
    prev = pl.BlockSpec((None, HALO, FFN_COLS), lambda c, r: (0, jnp.maximum(r * nb8 - 1, 0), c))
    nxt = pl.BlockSpec((2, HALO, FFN_COLS), lambda c, r: (0, jnp.minimum((r + 1) * nb8, last8), c))
    dmain = pl.BlockSpec((FFN_ROWS, FFN_COLS), lambda c, r: (r, c))
    dnxt = pl.BlockSpec((HALO, FFN_COLS), lambda c, r: (jnp.minimum((r + 1) * nb8, last8), c))
    wspec = pl.BlockSpec((3, FFN_COLS), lambda c, r: (0, c))
    bspec = pl.BlockSpec((1, FFN_COLS), lambda c, r: (0, c))

    def body(m_ref, p_ref, n_ref, dz_ref, dzn_ref, w_ref, cb_ref, dh_ref, dw_ref, dcb_ref):
        r = pl.program_id(1)

        @pl.when(r == 0)
        def _():
            dw_ref[...] = jnp.zeros_like(dw_ref)
            dcb_ref[...] = jnp.zeros_like(dcb_ref)

        prev = jnp.where(r == 0, 0.0, p_ref[...])
        a_ext = jnp.concatenate([prev, m_ref[0], n_ref[0]], axis=0)
        a, a1, a2 = _ffn_conv(a_ext, w_ref, cb_ref)
        a = a[HALO:]
        b_ext = jnp.concatenate([m_ref[1], n_ref[1]], axis=0)
        dz_ext = jnp.concatenate([dz_ref[...], jnp.where(r == nr - 1, 0.0, dzn_ref[...])], axis=0)
        sig = jax.nn.sigmoid(a)
        da = dz_ext * b_ext * (sig * (1.0 + a * (1.0 - sig)))
        n_ext = FFN_ROWS + HALO
        da_p1 = pltpu.roll(da, n_ext - 1, 0)
        da_p2 = pltpu.roll(da, n_ext - 2, 0)
        da_raw = w_ref[2:3, :] * da + w_ref[1:2, :] * da_p1 + w_ref[0:1, :] * da_p2
        dh_ref[0] = da_raw[:FFN_ROWS].astype(BF16)
        dh_ref[1] = (dz_ref[...] * (a * sig)[:FFN_ROWS]).astype(BF16)
        dam = da[:FFN_ROWS]
        rows = slice(HALO, HALO + FFN_ROWS)
        dw_ref[0:1, :] += jnp.sum(dam * a2[rows], axis=0, keepdims=True)
        dw_ref[1:2, :] += jnp.sum(dam * a1[rows], axis=0, keepdims=True)
        dw_ref[2:3, :] += jnp.sum(dam * a_ext[rows], axis=0, keepdims=True)
        dcb_ref[...] += jnp.sum(dam, axis=0, keepdims=True)

    return pl.pallas_call(
        body, name="ffn_bwd", grid=(D_FF // FFN_COLS, nr),
        out_shape=(jax.ShapeDtypeStruct((2, t, D_FF), BF16), jax.ShapeDtypeStruct((3, D_FF), F32),
                   jax.ShapeDtypeStruct((1, D_FF), F32)),
        in_specs=[main, prev, nxt, dmain, dnxt, wspec, bspec],
        out_specs=(main, wspec, bspec), compiler_params=_cp("parallel", "arbitrary"),
    )(hh, hh, hh, dz, dz, conv_w, conv_b)


def _adamw_math(w, g, m, v):
    m = ADAM_B1 * m + (1.0 - ADAM_B1) * g
    v = ADAM_B2 * v + (1.0 - ADAM_B2) * (g * g)
    m_hat = m / (1.0 - ADAM_B1 ** ADAM_STEP)
    v_hat = v / (1.0 - ADAM_B2 ** ADAM_STEP)
    delta = -ADAM_LR * (m_hat / (jnp.sqrt(v_hat) + ADAM_EPS) + ADAM_WD * w)
    return delta, m, v


def _as2d(shape):
    n = int(np.prod(shape))
    c = shape[-1] if shape[-1] % 128 == 0 else (128 if n % 128 == 0 else shape[-1])
    return n // c, c


def _adamw(w, g, m, v, name, parts=None, own=None, is_me=None):
    shape = w.shape
    r, c = _as2d(shape)
    br = r if r <= 512 else _pick(r, (256, 176, 128, 64, 8))
    blk = pl.BlockSpec((br, c), lambda i: (i, 0))
    w2, m2, v2 = (a.reshape(r, c) for a in (w, m, v))
    extra, extra_specs = [], []
    if parts is not None:
        g_in = parts.reshape(N_DEV, r, c)
        g_spec = pl.BlockSpec((N_DEV, br, c), lambda i: (0, i, 0))
        extra = [own.reshape(r, c), is_me]
        extra_specs = [blk, pl.BlockSpec(memory_space=pltpu.SMEM)]
    else:
        g_in = g.reshape(r, c)
        g_spec = blk

    def body(w_ref, g_ref, m_ref, v_ref, *rest):
        go_ref, d_ref, mo_ref, vo_ref = rest[-4:]
        if parts is not None:
            own_ref, me_ref = rest[:2]
            gsum = None
            for i in range(N_DEV):
                term = jnp.where(me_ref[i] > 0.5, own_ref[...], g_ref[i]).astype(F32)
                gsum = term if gsum is None else gsum + term
        else:
            gsum = g_ref[...]
        delta, m_new, v_new = _adamw_math(w_ref[...], gsum, m_ref[...], v_ref[...])
        go_ref[...] = gsum
        d_ref[...] = delta
        mo_ref[...] = m_new
        vo_ref[...] = v_new

    o = jax.ShapeDtypeStruct((r, c), F32)
    outs = pl.pallas_call(
        body, name=name, grid=(r // br,), out_shape=(o, o, o, o),
        in_specs=[blk, g_spec, blk, blk] + extra_specs, out_specs=(blk, blk, blk, blk),
        compiler_params=_cp("parallel"),
    )(w2, g_in, m2, v2, *extra)
    return tuple(a.reshape(shape) for a in outs)


def _me_and_peers():
    x, y, c = lax.axis_index("x"), lax.axis_index("y"), lax.axis_index("c")
    me = 4 * x + 2 * y + c
    peers = []
    for k in range(1, N_DEV):
        kx, ky, kc = (k >> 2) & 1, (k >> 1) & 1, k & 1
        px, py, pc = x ^ kx, y ^ ky, c ^ kc
        peers.append(((px, py, pc), 4 * px + 2 * py + pc))
    return me, peers


def _all_gather(shard, name):
    def body(x_ref, o_ref, send_sems, recv_sems, local_sem):
        me, peers = _me_and_peers()
        mine = pltpu.make_async_copy(x_ref, o_ref.at[me], local_sem)
        mine.start()
        sends = []
        for k, (dev, _) in enumerate(peers):
            cp = pltpu.make_async_remote_copy(src_ref=x_ref, dst_ref=o_ref.at[me], send_sem=send_sems.at[k],
                                              recv_sem=recv_sems.at[k], device_id=dev, device_id_type=MESH)
            cp.start()
            sends.append(cp)
        for k, (dev, idx) in enumerate(peers):
            pltpu.make_async_remote_copy(src_ref=x_ref, dst_ref=o_ref.at[idx], send_sem=send_sems.at[k],
                                         recv_sem=recv_sems.at[k], device_id=dev, device_id_type=MESH).wait_recv()
        for cp in sends:
            cp.wait_send()
        mine.wait()

    return pl.pallas_call(
        body, name=name, out_shape=jax.ShapeDtypeStruct((N_DEV,) + shard.shape, shard.dtype),
        in_specs=[ANY], out_specs=ANY,
        scratch_shapes=[pltpu.SemaphoreType.DMA((N_DEV - 1,)), pltpu.SemaphoreType.DMA((N_DEV - 1,)),
                        pltpu.SemaphoreType.DMA],
        compiler_params=pltpu.CompilerParams(has_side_effects=True),
    )(shard)


HBM = pl.BlockSpec(memory_space=pltpu.HBM)
SEM = pl.BlockSpec(memory_space=pltpu.SEMAPHORE)
EFFECT = pltpu.SideEffectType.DATAFLOW_SIDE_EFFECTING


def _xchg_start(parts, land, *, dep, name):
    n_buf = 1 if parts is None else 2

    def body(*refs):
        land_ref = refs[n_buf - 1]
        send_sem, recv_sem = refs[n_buf + 1], refs[n_buf + 2]
        token = refs[2 * n_buf + 3]
        me, peers = _me_and_peers()
        for dev, idx in peers:
            src = land_ref.at[me] if parts is None else refs[0].at[idx]
            pltpu.make_async_remote_copy(src_ref=src, dst_ref=land_ref.at[me], send_sem=send_sem, recv_sem=recv_sem,
                                         device_id=dev, device_id_type=MESH).start()
        token[...] = jnp.zeros_like(token)

    bufs = [land] if parts is None else [parts, land]
    return pl.pallas_call(
        body, name=name,
        out_shape=(pltpu.SemaphoreType.DMA(()), pltpu.SemaphoreType.DMA(()),
                   *[pltpu.HBM(b.shape, b.dtype) for b in bufs], jax.ShapeDtypeStruct((8, 128), F32)),
        in_specs=(*[HBM] * n_buf, ANY), out_specs=(SEM, SEM, *[HBM] * n_buf, pl.BlockSpec(memory_space=pltpu.VMEM)),
        input_output_aliases={i: 2 + i for i in range(n_buf)},
        compiler_params=pltpu.CompilerParams(has_side_effects=EFFECT),
    )(*[pltpu.with_memory_space_constraint(b, pltpu.HBM) for b in bufs], dep)


def _xchg_wait(handles, after, name):
    send_sem, recv_sem, *bufs, _ = handles
    n_buf = len(bufs)

    def body(*refs):
        land_ref = refs[n_buf - 1]
        send_sem, recv_sem = refs[n_buf], refs[n_buf + 1]
        x, y, c = lax.axis_index("x"), lax.axis_index("y"), lax.axis_index("c")
        seven = land_ref.at[pl.ds(0, N_DEV - 1)]
        cp = pltpu.make_async_remote_copy(src_ref=seven, dst_ref=seven, send_sem=send_sem, recv_sem=recv_sem,
                                          device_id=(x, y, c), device_id_type=MESH)
        cp.wait_send()
        cp.wait_recv()

    return pl.pallas_call(
        body, name=name, out_shape=tuple(pltpu.HBM(b.shape, b.dtype) for b in bufs),
        in_specs=(*[HBM] * n_buf, SEM, SEM, ANY), out_specs=tuple([HBM] * n_buf),
        input_output_aliases={i: i for i in range(n_buf)},
        compiler_params=pltpu.CompilerParams(has_side_effects=EFFECT),
    )(*bufs, send_sem, recv_sem, after)


def _all_reduce_small(v, name):
    r, c = v.shape

    def body(x_ref, o_ref, slots, send_sems, recv_sems):
        me, peers = _me_and_peers()
        sends = []
        for k, (dev, _) in enumerate(peers):
            cp = pltpu.make_async_remote_copy(src_ref=x_ref, dst_ref=slots.at[me], send_sem=send_sems.at[k],
                                              recv_sem=recv_sems.at[k], device_id=dev, device_id_type=MESH)
            cp.start()
            sends.append(cp)
        slots[me] = x_ref[...]
        for k, (dev, idx) in enumerate(peers):
            pltpu.make_async_remote_copy(src_ref=x_ref, dst_ref=slots.at[idx], send_sem=send_sems.at[k],
                                         recv_sem=recv_sems.at[k], device_id=dev, device_id_type=MESH).wait_recv()
        for cp in sends:
            cp.wait_send()
        acc = slots[0]
        for i in range(1, N_DEV):
            acc = acc + slots[i]
        o_ref[...] = acc

    return pl.pallas_call(
        body, name=name, out_shape=jax.ShapeDtypeStruct((r, c), F32),
        in_specs=[pl.BlockSpec(memory_space=pltpu.VMEM)], out_specs=pl.BlockSpec(memory_space=pltpu.VMEM),
        scratch_shapes=[pltpu.VMEM((N_DEV, r, c), F32), pltpu.SemaphoreType.DMA((N_DEV - 1,)),
                        pltpu.SemaphoreType.DMA((N_DEV - 1,))],
        compiler_params=pltpu.CompilerParams(has_side_effects=True, vmem_limit_bytes=VMEM_LIMIT),
    )(v)


def _local_step(x, target, lb_logits, p, weight, send_grad):
    g = {}
    l0, l1, l2 = lb_logits[0:1], lb_logits[1:2], lb_logits[2:3]
    x_bf = x.astype(BF16)

    proj0 = _mm(x_bf, weight("hg_w_in", x_bf), mode="nn", b_cs=True, o_cs=4, name="mm_hg_in")
    y0, states = _hg_fwd(proj0, l0, l1, l2, p["hg_norm_g"])
    mixed0 = _mm(y0, weight("hg_w_out", y0), mode="nn", name="mm_hg_out")
    hh_l, z_l = [], []

    def ffn_forward(layer, h_in_bf):
        hh = _mm(h_in_bf, weight(f"ffn_w_up{layer}", h_in_bf), mode="nn", b_cs=True, o_cs=2, name=f"mm_up{layer}")
        z = _ffn_fwd(hh, p["ffn_conv_w"][layer], p["ffn_conv_b"][layer])
        ffn = _mm(z, weight(f"ffn_w_down{layer}", z), mode="nn", name=f"mm_down{layer}")
        hh_l.append(hh)
        z_l.append(z)
        return ffn

    h1, h1b = _ln_fwd(x, mixed0, p["ln1_g"][0], p["ln1_b"][0], "ln1_0")
    ffn0 = ffn_forward(0, h1b)
    h2, h2b = _ln_fwd(h1, ffn0, p["ln2_g"][0], p["ln2_b"][0], "ln2_0")
    pre1 = _mm(h2b, weight("sg_w_in", h2b), mode="nn", b_cs=True, name="mm_sg_in")
    y1 = _sg_fwd(pre1, p["sg_ln_g"], p["sg_ln_b"], p["sg_w_s"], p["sg_bias"])
    mixed1 = _mm(y1, weight("sg_w_out", y1), mode="nn", name="mm_sg_out")
    h3, h3b = _ln_fwd(h2, mixed1, p["ln1_g"][1], p["ln1_b"][1], "ln1_1")
    ffn1 = ffn_forward(1, h3b)

    loss, ds, dres, g["ln2_g1"], g["ln2_b1"] = _ln_loss_bwd(h3, ffn1, target, p["ln2_g"][1], p["ln2_b"][1], "ln2_1_loss")

    def ffn_backward(layer, ds_bf, h_in_bf):
        tok = send_grad(f"ffn_w_down{layer}",
                        _mm(z_l[layer], ds_bf, mode="tn", out_dtype=BF16, name=f"mm_dw_down{layer}"))
        dz = _mm(ds_bf, weight(f"ffn_w_down{layer}", None), mode="nt", name=f"mm_d_z{layer}", deps=(tok,))
        dhh, g[f"conv_w{layer}"], g[f"conv_b{layer}"] = _ffn_bwd(hh_l[layer], dz, p["ffn_conv_w"][layer],
                                                                   p["ffn_conv_b"][layer])
        tok = send_grad(f"ffn_w_up{layer}", _mm(h_in_bf, dhh, mode="tn", b_cs=True, o_cs=N_DEV, out_dtype=BF16,
                                                name=f"mm_dw_up{layer}"))
        return _mm(dhh, weight(f"ffn_w_up{layer}", None), mode="nt", a_cs=True, b_cs=True, name=f"mm_d_up{layer}",
                   deps=(tok,))

    dh3 = ffn_backward(1, ds, h3b)
    ds, dres, g["ln1_g1"], g["ln1_b1"] = _ln_bwd(h2, mixed1, dres, dh3, p["ln1_g"][1], "ln1_1_bwd")
    tok = send_grad("sg_w_out", _mm(y1, ds, mode="tn", out_dtype=BF16, name="mm_dw_sg_out"))
    dy1 = _mm(ds, weight("sg_w_out", None), mode="nt", name="mm_d_sg_out", deps=(tok,))
    dpre1, g["sg_w_s"], dbs, g["sg_ln_g"], g["sg_ln_b"] = _sg_bwd(pre1, dy1, p["sg_ln_g"], p["sg_ln_b"],
                                                                    p["sg_w_s"], p["sg_bias"])
    g["sg_b_s"] = dbs[:, 0, :]
    tok = send_grad("sg_w_in", _mm(h2b, dpre1, mode="tn", o_cs=N_DEV, out_dtype=BF16, name="mm_dw_sg_in"))
    dh2 = _mm(dpre1, weight("sg_w_in", None), mode="nt", b_cs=True, name="mm_d_sg_in", deps=(tok,))
    ds, dres, g["ln2_g0"], g["ln2_b0"] = _ln_bwd(h1, ffn0, dres, dh2, p["ln2_g"][0], "ln2_0_bwd")
    dh1 = ffn_backward(0, ds, h1b)
    ds, dres, g["ln1_g0"], g["ln1_b0"] = _ln_bwd(x, mixed0, dres, dh1, p["ln1_g"][0], "ln1_0_bwd")
    tok = send_grad("hg_w_out", _mm(y0, ds, mode="tn", out_dtype=BF16, name="mm_dw_hg_out"))
    dy0 = _mm(ds, weight("hg_w_out", None), mode="nt", name="mm_d_hg_out", deps=(tok,))
    dproj, d0, d1, d2, g["hg_norm_g"] = _hg_bwd(proj0, states, dy0, l0, l1, l2, p["hg_norm_g"])
    g["lb_logits"] = jnp.concatenate([d0, d1, d2], axis=0)
    tok = send_grad("hg_w_in", _mm(x_bf, dproj, mode="tn", b_cs=True, o_cs=N_DEV, out_dtype=BF16, name="mm_dw_hg_in"))
    grad_x = _mm(dproj, weight("hg_w_in", None), mode="nt", a_cs=True, b_cs=True, addend=dres, name="mm_d_hg_in",
                 deps=(tok,))
    return loss[0, 0], grad_x, g


_SMALL = ("lb_logits", "hg_norm_g", "sg_ln_g", "sg_ln_b", "sg_w_s", "sg_b_s", "ffn_conv_w", "ffn_conv_b",
          "ln1_g", "ln1_b", "ln2_g", "ln2_b")
_NAMES = ("lb_logits", "hg_w_in", "hg_norm_g", "hg_w_out", "sg_w_in", "sg_ln_g", "sg_ln_b", "sg_w_s", "sg_b_s",
          "sg_w_out", "ffn_w_up", "ffn_conv_w", "ffn_conv_b", "ffn_w_down", "ln1_g", "ln1_b", "ln2_g", "ln2_b")


def kernel(x, lb_logits, hg_w_in, hg_norm_g, hg_w_out, sg_w_in, sg_ln_g, sg_ln_b, sg_w_s, sg_b_s, sg_w_out, ffn_w_up, ffn_conv_w, ffn_conv_b, ffn_w_down, ln1_g, ln1_b, ln2_g, ln2_b, loss_target, m_lb_logits, m_hg_w_in, m_hg_norm_g, m_hg_w_out, m_sg_w_in, m_sg_ln_g, m_sg_ln_b, m_sg_w_s, m_sg_b_s, m_sg_w_out, m_ffn_w_up, m_ffn_conv_w, m_ffn_conv_b, m_ffn_w_down, m_ln1_g, m_ln1_b, m_ln2_g, m_ln2_b, v_lb_logits, v_hg_w_in, v_hg_norm_g, v_hg_w_out, v_sg_w_in, v_sg_ln_g, v_sg_ln_b, v_sg_w_s, v_sg_b_s, v_sg_w_out, v_ffn_w_up, v_ffn_conv_w, v_ffn_conv_b, v_ffn_w_down, v_ln1_g, v_ln1_b, v_ln2_g, v_ln2_b):
    w = dict(lb_logits=lb_logits, hg_w_in=hg_w_in, hg_norm_g=hg_norm_g, hg_w_out=hg_w_out, sg_w_in=sg_w_in,
             sg_ln_g=sg_ln_g, sg_ln_b=sg_ln_b, sg_w_s=sg_w_s, sg_b_s=sg_b_s, sg_w_out=sg_w_out, ffn_w_up=ffn_w_up,
             ffn_conv_w=ffn_conv_w, ffn_conv_b=ffn_conv_b, ffn_w_down=ffn_w_down, ln1_g=ln1_g, ln1_b=ln1_b,
             ln2_g=ln2_g, ln2_b=ln2_b)
    m = dict(lb_logits=m_lb_logits, hg_w_in=m_hg_w_in, hg_norm_g=m_hg_norm_g, hg_w_out=m_hg_w_out,
             sg_w_in=m_sg_w_in, sg_ln_g=m_sg_ln_g, sg_ln_b=m_sg_ln_b, sg_w_s=m_sg_w_s, sg_b_s=m_sg_b_s,
             sg_w_out=m_sg_w_out, ffn_w_up=m_ffn_w_up, ffn_conv_w=m_ffn_conv_w, ffn_conv_b=m_ffn_conv_b,
             ffn_w_down=m_ffn_w_down, ln1_g=m_ln1_g, ln1_b=m_ln1_b, ln2_g=m_ln2_g, ln2_b=m_ln2_b)
    v = dict(lb_logits=v_lb_logits, hg_w_in=v_hg_w_in, hg_norm_g=v_hg_norm_g, hg_w_out=v_hg_w_out,
             sg_w_in=v_sg_w_in, sg_ln_g=v_sg_ln_g, sg_ln_b=v_sg_ln_b, sg_w_s=v_sg_w_s, sg_b_s=v_sg_b_s,
             sg_w_out=v_sg_w_out, ffn_w_up=v_ffn_w_up, ffn_conv_w=v_ffn_conv_w, ffn_conv_b=v_ffn_conv_b,
             ffn_w_down=v_ffn_w_down, ln1_g=v_ln1_g, ln1_b=v_ln1_b, ln2_g=v_ln2_g, ln2_b=v_ln2_b)
    me = 4 * lax.axis_index("x") + 2 * lax.axis_index("y") + lax.axis_index("c")
    d = D_MODEL

    shards = [("hg_w_in", hg_w_in[0]), ("hg_w_out", hg_w_out[0]), ("ffn_w_up0", ffn_w_up[0]),
              ("ffn_w_down0", ffn_w_down[0]), ("sg_w_in", sg_w_in[0]), ("sg_w_out", sg_w_out[0]),
              ("ffn_w_up1", ffn_w_up[1]), ("ffn_w_down1", ffn_w_down[1])]
    row_sharded = {"hg_w_out": (d, d), "sg_w_out": (d, d), "ffn_w_down0": (D_FF, d), "ffn_w_down1": (D_FF, d)}
    gathers, gathered, dep = {}, {}, lb_logits
    for n, shard in shards:
        land = lax.dynamic_update_index_in_dim(lax.empty((N_DEV,) + shard.shape, BF16), shard.astype(BF16), me, 0)
        gathers[n] = _xchg_start(None, land, dep=dep, name=f"ag_{n}")
        dep = gathers[n][-1]

    def weight(n, after):
        if n not in gathered:
            (full,) = _xchg_wait(gathers[n], after, f"agw_{n}")
            gathered[n] = full.reshape(row_sharded[n]) if n in row_sharded else full
        return gathered[n]

    grad_sends = {}

    def send_grad(n, parts):
        shape = (N_DEV,) + dict(shards)[n].shape
        grad_sends[n] = _xchg_start(parts.reshape(shape), lax.empty(shape, parts.dtype), dep=parts, name=f"rs_{n}")
        return grad_sends[n][-1]

    p = {}
    sv = jnp.zeros((8, 768), F32)
    sv = sv.at[0, :256].set(sg_ln_g[0]).at[1, :256].set(sg_ln_b[0]).at[2:8, :704].set(ffn_conv_w.reshape(6, 704))
    sv = _all_gather(sv, "ag_small")
    p["sg_ln_g"] = sv[:, 0, :256].reshape(1, d)
    p["sg_ln_b"] = sv[:, 1, :256].reshape(1, d)
    conv_w_full = jnp.transpose(sv[:, 2:8, :704].reshape(N_DEV, DEPTH, 3, 704), (1, 2, 0, 3)).reshape(DEPTH, 3, D_FF)
    p["ffn_conv_w"] = [conv_w_full[l] for l in range(DEPTH)]
    p["ffn_conv_b"] = [ffn_conv_b[l:l + 1] for l in range(DEPTH)]
    p["hg_norm_g"] = hg_norm_g
    p["sg_w_s"] = sg_w_s[0]
    p["sg_bias"] = jnp.broadcast_to(sg_b_s[0][:, :, None], (SG_GROUPS, SG_CHUNK, SG_DIM))
    for n in ("ln1_g", "ln1_b", "ln2_g", "ln2_b"):
        p[n] = [w[n][l:l + 1] for l in range(DEPTH)]

    loss_part, grad_x, g = _local_step(x[0], loss_target[0], lb_logits, p, weight, send_grad)
    loss = lax.psum(loss_part, ("x", "y", "c"))

    small = [g["lb_logits"], g["hg_norm_g"], g["sg_ln_g"], g["sg_ln_b"], g["sg_w_s"], g["sg_b_s"],
             g["conv_w0"], g["conv_w1"], g["conv_b0"], g["conv_b1"],
             g["ln1_g0"], g["ln1_g1"], g["ln1_b0"], g["ln1_b1"], g["ln2_g0"], g["ln2_g1"], g["ln2_b0"], g["ln2_b1"]]
    sizes = [int(np.prod(a.shape)) for a in small]
    flat = jnp.concatenate([a.reshape(-1) for a in small])
    total = flat.shape[0]
    rows = -(-total // 128)
    rows = -(-rows // 8) * 8
    flat = jnp.pad(flat, (0, rows * 128 - total)).reshape(rows, 128)
    red = _all_reduce_small(flat, "ar_small").reshape(-1)
    offs = np.cumsum([0] + sizes)
    r = [red[offs[i]:offs[i + 1]].reshape(small[i].shape) for i in range(len(small))]
    gs = {}
    gs["lb_logits"] = r[0]
    gs["hg_norm_g"] = r[1]
    gs["sg_ln_g"] = lax.dynamic_slice(r[2], (0, me * 256), (1, 256))
    gs["sg_ln_b"] = lax.dynamic_slice(r[3], (0, me * 256), (1, 256))
    gs["sg_w_s"] = r[4][None]
    gs["sg_b_s"] = r[5][None]
    gs["ffn_conv_w"] = lax.dynamic_slice(jnp.stack([r[6], r[7]]), (0, 0, me * 704), (DEPTH, 3, 704))
    gs["ffn_conv_b"] = jnp.concatenate([r[8], r[9]], axis=0)
    gs["ln1_g"] = jnp.concatenate([r[10], r[11]], axis=0)
    gs["ln1_b"] = jnp.concatenate([r[12], r[13]], axis=0)
    gs["ln2_g"] = jnp.concatenate([r[14], r[15]], axis=0)
    gs["ln2_b"] = jnp.concatenate([r[16], r[17]], axis=0)

    out_g, out_d, out_m, out_v = {}, {}, {}, {}
    for n in _SMALL:
        out_g[n], out_d[n], out_m[n], out_v[n] = _adamw(w[n], gs[n], m[n], v[n], f"adamw_{n}")

    res, after = {}, out_v["ln2_b"]
    is_me = (jnp.arange(N_DEV) == me).astype(F32)
    for n in ("ffn_w_down1", "ffn_w_up1", "sg_w_out", "sg_w_in", "ffn_w_down0", "ffn_w_up0", "hg_w_out", "hg_w_in"):
        parts, recv = _xchg_wait(grad_sends[n], after, f"rsw_{n}")
        own = lax.dynamic_index_in_dim(parts, me, 0, keepdims=False)
        base, layer = (n[:-1], int(n[-1])) if n[-1] in "01" else (n, 0)
        res[n] = _adamw(w[base][layer], None, m[base][layer], v[base][layer], f"adamw_{n}", parts=recv, own=own,
                        is_me=is_me)
        after = res[n][3]
    for n in ("hg_w_in", "hg_w_out", "sg_w_in", "sg_w_out"):
        out_g[n], out_d[n], out_m[n], out_v[n] = (a[None] for a in res[n])
    for n in ("ffn_w_up", "ffn_w_down"):
        out_g[n], out_d[n], out_m[n], out_v[n] = (jnp.stack([res[n + "0"][i], res[n + "1"][i]]) for i in range(4))

    return (loss, grad_x[None], *[out_g[n] for n in _NAMES], *[out_d[n] for n in _NAMES],
            *[out_m[n] for n in _NAMES], *[out_v[n] for n in _NAMES])
```

```python
import functools

import numpy as np
import jax
import jax.numpy as jnp
from jax import lax
from jax.experimental import pallas as pl
from jax.experimental.pallas import tpu as pltpu

F32 = jnp.float32
BF16 = jnp.bfloat16
HI = lax.Precision.HIGHEST

N_DEV = 8
D_MODEL = 2048
HG_HEADS = 16
HG_DIM = 128
HG_CHUNK = 64
SG_GROUPS = 16
SG_DIM = 128
SG_CHUNK = 128
D_FF = 5632
DEPTH = 2
ALPHA = (2 * DEPTH) ** 0.25
LN_EPS = 1e-5
RMS_EPS = 1e-6
ADAM_LR = 0.001
ADAM_B1 = 0.9
ADAM_B2 = 0.999
ADAM_EPS = 1e-08
ADAM_WD = 0.01
ADAM_STEP = 10

VMEM_LIMIT = 56 * 1024 * 1024
MESH = pl.DeviceIdType.MESH
ANY = pl.BlockSpec(memory_space=pl.ANY)


def _cp(*sem):
    return pltpu.CompilerParams(dimension_semantics=sem, vmem_limit_bytes=VMEM_LIMIT)


def _pick(n, cands):
    for c in cands:
        if n % c == 0:
            return c
    raise ValueError(f"no tile for {n} in {cands}")


_DIMS = {"nn": (((1,), (0,)), ((), ())), "nt": (((1,), (1,)), ((), ())), "tn": (((0,), (0,)), ((), ()))}


def _mm(a, b, *, mode, name, out_dtype=F32, a_cs=False, b_cs=False, o_cs=None, addend=None,
        bm=None, bn=None, bk=None, deps=()):
    if mode == "tn":
        kk, m = a.shape
    elif a_cs:
        m, kk = a.shape[1], a.shape[0] * a.shape[2]
    else:
        m, kk = a.shape
    if mode == "nt":
        n = b.shape[1] if b_cs else b.shape[0]
    else:
        n = b.shape[0] * b.shape[2] if b_cs else b.shape[1]
    a_c = a.shape[2] if a_cs else None
    b_c = b.shape[2] if b_cs else None
    o_c = n // o_cs if o_cs else None
    bm = bm or _pick(m, (1024, 1408, 512, 256))
    n_div = [c for c in (b_c if (b_cs and mode != "nt") else None, o_c) if c]
    k_div = [c for c in (a_c, b_c if (b_cs and mode == "nt") else None) if c]

    def pick_with(total, divs, cands):
        for c in cands:
            if total % c == 0 and all(d % c == 0 for d in divs):
                return c
        raise ValueError((total, divs))

    bn = bn or pick_with(n, n_div, (1024, 1408, 512, 256, 128))
    bk = bk or pick_with(kk, k_div, (512, 1408, 1024, 256, 128))
    nk = kk // bk

    def cs_idx(blk, per):
        return (blk * per[0]) // per[1], (blk * per[0] % per[1]) // per[0]

    if mode == "tn":
        a_spec = pl.BlockSpec((bk, bm), lambda i, j, k: (k, i))
    elif a_cs:
        def a_map(i, j, k):
            s, r = cs_idx(k, (bk, a_c))
            return (s, i, r)
        a_spec = pl.BlockSpec((None, bm, bk), a_map)
    else:
        a_spec = pl.BlockSpec((bm, bk), lambda i, j, k: (i, k))
    if mode == "nt":
        if b_cs:
            def b_map(i, j, k):
                s, r = cs_idx(k, (bk, b_c))
                return (s, j, r)
            b_spec = pl.BlockSpec((None, bn, bk), b_map)
        else:
            b_spec = pl.BlockSpec((bn, bk), lambda i, j, k: (j, k))
    else:
        if b_cs:
            def b_map(i, j, k):
                s, r = cs_idx(j, (bn, b_c))
                return (s, k, r)
            b_spec = pl.BlockSpec((None, bk, bn), b_map)
        else:
            b_spec = pl.BlockSpec((bk, bn), lambda i, j, k: (k, j))
    if o_cs:
        def o_map(i, j, k):
            s, r = cs_idx(j, (bn, o_c))
            return (s, i, r)
        o_spec = pl.BlockSpec((None, bm, bn), o_map)
        out_shape = jax.ShapeDtypeStruct((o_cs, m, o_c), out_dtype)
    else:
        o_spec = pl.BlockSpec((bm, bn), lambda i, j, k: (i, j))
        out_shape = jax.ShapeDtypeStruct((m, n), out_dtype)
    in_specs = [a_spec, b_spec]
    args = [a, b]
    if addend is not None:
        in_specs.append(pl.BlockSpec((bm, bn), lambda i, j, k: (i, j)))
        args.append(addend)
    dims = _DIMS[mode]
    in_specs += [ANY] * len(deps)
    args += list(deps)

    def body(*refs):
        refs = refs[:len(refs) - 2 - len(deps)] + refs[len(refs) - 2:]
        if addend is not None:
            a_ref, b_ref, add_ref, o_ref, acc_ref = refs
        else:
            a_ref, b_ref, o_ref, acc_ref = refs
        k = pl.program_id(2)

        @pl.when(k == 0)
        def _():
            acc_ref[...] = jnp.zeros_like(acc_ref)

        acc_ref[...] += lax.dot_general(a_ref[...].astype(BF16), b_ref[...].astype(BF16), dims,
                                        preferred_element_type=F32)

        @pl.when(k == nk - 1)
        def _():
            r = acc_ref[...]
            if addend is not None:
                r = r + add_ref[...]
            o_ref[...] = r.astype(o_ref.dtype)

    return pl.pallas_call(
        body, name=name, out_shape=out_shape, grid=(m // bm, n // bn, nk),
        in_specs=in_specs, out_specs=o_spec, scratch_shapes=[pltpu.VMEM((bm, bn), F32)],
        compiler_params=_cp("parallel", "parallel", "arbitrary"),
    )(*args)


LN_ROWS = 256


def _ln_stats(s):
    mu = jnp.mean(s, axis=-1, keepdims=True)
    sc = s - mu
    var = jnp.mean(sc * sc, axis=-1, keepdims=True)
    rstd = lax.rsqrt(var + LN_EPS)
    return sc * rstd, rstd


def _ln_fwd(h, sub, g, b, name):
    t, d = h.shape
    row = pl.BlockSpec((LN_ROWS, d), lambda i: (i, 0))
    vec = pl.BlockSpec((1, d), lambda i: (0, 0))

    def body(h_ref, s_ref, g_ref, b_ref, y_ref, yb_ref):
        xhat, _ = _ln_stats(ALPHA * h_ref[...] + s_ref[...])
        y = xhat * g_ref[...] + b_ref[...]
        y_ref[...] = y
        yb_ref[...] = y.astype(BF16)

    return pl.pallas_call(
        body, name=name, grid=(t // LN_ROWS,),
        out_shape=(jax.ShapeDtypeStruct((t, d), F32), jax.ShapeDtypeStruct((t, d), BF16)),
        in_specs=[row, row, vec, vec], out_specs=(row, row), compiler_params=_cp("parallel"),
    )(h, sub, g, b)


def _ln_bwd_math(xhat, rstd, dy, g):
    dxhat = dy * g
    m1 = jnp.mean(dxhat, axis=-1, keepdims=True)
    m2 = jnp.mean(dxhat * xhat, axis=-1, keepdims=True)
    ds = rstd * (dxhat - m1 - xhat * m2)
    dg = jnp.sum(dy * xhat, axis=0, keepdims=True)
    db = jnp.sum(dy, axis=0, keepdims=True)
    return ds, dg, db


def _ln_bwd(h, sub, dy_a, dy_b, g, name):
    t, d = h.shape
    row = pl.BlockSpec((LN_ROWS, d), lambda i: (i, 0))
    vec = pl.BlockSpec((1, d), lambda i: (0, 0))

    def body(h_ref, s_ref, da_ref, db_ref, g_ref, ds_ref, dres_ref, dg_ref, dbeta_ref):
        xhat, rstd = _ln_stats(ALPHA * h_ref[...] + s_ref[...])
        ds, dg, db = _ln_bwd_math(xhat, rstd, da_ref[...] + db_ref[...], g_ref[...])
        ds_ref[...] = ds.astype(BF16)
        dres_ref[...] = ALPHA * ds

        @pl.when(pl.program_id(0) == 0)
        def _():
            dg_ref[...] = jnp.zeros_like(dg_ref)
            dbeta_ref[...] = jnp.zeros_like(dbeta_ref)

        dg_ref[...] += dg
        dbeta_ref[...] += db

    return pl.pallas_call(
        body, name=name, grid=(t // LN_ROWS,),
        out_shape=(jax.ShapeDtypeStruct((t, d), BF16), jax.ShapeDtypeStruct((t, d), F32),
                   jax.ShapeDtypeStruct((1, d), F32), jax.ShapeDtypeStruct((1, d), F32)),
        in_specs=[row, row, row, row, vec], out_specs=(row, row, vec, vec),
        compiler_params=_cp("arbitrary"),
    )(h, sub, dy_a, dy_b, g)


def _ln_loss_bwd(h, sub, target, g, b, name):
    t, d = h.shape
    row = pl.BlockSpec((LN_ROWS, d), lambda i: (i, 0))
    vec = pl.BlockSpec((1, d), lambda i: (0, 0))
    lvec = pl.BlockSpec((1, 128), lambda i: (0, 0))

    def body(h_ref, s_ref, t_ref, g_ref, b_ref, loss_ref, ds_ref, dres_ref, dg_ref, dbeta_ref):
        xhat, rstd = _ln_stats(ALPHA * h_ref[...] + s_ref[...])
        y = xhat * g_ref[...] + b_ref[...]
        err = y - t_ref[...]
        part = 0.5 * jnp.sum(jnp.mean(err * err, axis=-1, keepdims=True), axis=0, keepdims=True)
        ds, dg, db = _ln_bwd_math(xhat, rstd, err * (1.0 / d), g_ref[...])
        ds_ref[...] = ds.astype(BF16)
        dres_ref[...] = ALPHA * ds

        @pl.when(pl.program_id(0) == 0)
        def _():
            loss_ref[...] = jnp.zeros_like(loss_ref)
            dg_ref[...] = jnp.zeros_like(dg_ref)
            dbeta_ref[...] = jnp.zeros_like(dbeta_ref)

        loss_ref[...] += jnp.broadcast_to(part, loss_ref.shape)
        dg_ref[...] += dg
        dbeta_ref[...] += db

    return pl.pallas_call(
        body, name=name, grid=(t // LN_ROWS,),
        out_shape=(jax.ShapeDtypeStruct((1, 128), F32), jax.ShapeDtypeStruct((t, d), BF16),
                   jax.ShapeDtypeStruct((t, d), F32), jax.ShapeDtypeStruct((1, d), F32),
                   jax.ShapeDtypeStruct((1, d), F32)),
        in_specs=[row, row, row, vec, vec], out_specs=(lvec, row, row, vec, vec),
        compiler_params=_cp("arbitrary"),
    )(h, sub, target, g, b)


HG_ROWS = 512
HG_LEVELS = 6


def _hg_constants():
    c = HG_CHUNK
    t = np.arange(c)[:, None]
    j = np.arange(c)[None, :]
    blocks, masks = [], [np.eye(c)]
    for lev in range(HG_LEVELS):
        m = 1 << lev
        second = (t % (2 * m)) >= m
        mid = (t // (2 * m)) * (2 * m) + m - 1
        blocks.append((second & (j > mid) & (j <= t)) | ((~second) & (j > t) & (j <= mid)))
        same = (t // (2 * m)) == (j // (2 * m))
        masks.append(same & second & ((j % (2 * m)) < m))
    blocks += [j <= t, j > t, np.ones((c, c), bool)]
    cm = np.concatenate(blocks, axis=0).astype(np.float32)
    mk = np.concatenate(masks, axis=0).astype(np.float32)
    return jnp.asarray(cm, dtype=BF16), jnp.asarray(mk)


def _split3(x):
    hi = x.astype(BF16)
    r = x - hi.astype(F32)
    mid = r.astype(BF16)
    lo = (r - mid.astype(F32)).astype(BF16)
    return hi, mid, lo


def _sum3(x):
    n = x.shape[-1] // 3
    return x[:, :n] + x[:, n:2 * n] + x[:, 2 * n:]


@jax.custom_vjp
def _prefix(cm, lf):
    return _sum3(lax.dot_general(cm, jnp.concatenate(_split3(lf), axis=1), _DIMS["nn"], preferred_element_type=F32))


def _prefix_fwd(cm, lf):
    return _prefix(cm, lf), cm


def _prefix_bwd(cm, d):
    d_lf = _sum3(lax.dot_general(cm, jnp.concatenate(_split3(d), axis=1), _DIMS["tn"], preferred_element_type=F32))
    return jnp.zeros_like(cm), d_lf


_prefix.defvjp(_prefix_fwd, _prefix_bwd)

_BWD = {"nn": (("nt", 0, 1), ("tn", 1, 0)), "nt": (("nn", 0, 1), ("tn", 0, 1)), "tn": (("nt", 1, 0), ("nn", 1, 0))}


def _bdot_raw(a, b, mode):
    return lax.dot_general(a.astype(BF16), b.astype(BF16), _DIMS[mode], preferred_element_type=F32)


@functools.partial(jax.custom_vjp, nondiff_argnums=(2,))
def _bdot(a, b, mode):
    return _bdot_raw(a, b, mode)


def _bdot_fwd(a, b, mode):
    return _bdot_raw(a, b, mode), (a, b)


def _bdot_bwd(mode, res, d):
    a, b = res
    (ma, da_pos, _), (mb, db_pos, _) = _BWD[mode]
    da = _bdot_raw(d, b, ma) if da_pos == 0 else _bdot_raw(b, d, ma)
    db = _bdot_raw(d, a, mb) if db_pos == 0 else _bdot_raw(a, d, mb)
    return da, db


_bdot.defvjp(_bdot_fwd, _bdot_bwd)


def _hg_chunk(pq, pf, pi, pg, l0, l1, l2, ng, s_t, cm, mk):
    c = HG_CHUNK
    mx = jnp.maximum(jnp.maximum(l0, l1), l2)
    e0, e1, e2 = jnp.exp(l0 - mx), jnp.exp(l1 - mx), jnp.exp(l2 - mx)
    lb = e0 / (e0 + e1 + e2)
    q = pq * jax.nn.sigmoid(pq)
    t1 = jnp.log(lb)
    t2 = jnp.log1p(-lb) + jax.nn.log_sigmoid(pf)
    lf = jnp.maximum(t1, t2) + jnp.log1p(jnp.exp(-jnp.abs(t1 - t2)))
    k = (1.0 - lb) * jax.nn.sigmoid(-pf)
    x = jnp.exp(_prefix(cm, lf))
    scores = mk[0:c] * _bdot(q, k, "nt")
    for lev in range(HG_LEVELS):
        xl = x[lev * c:(lev + 1) * c]
        scores = scores + mk[(lev + 1) * c:(lev + 2) * c] * _bdot(q * xl, k * xl, "nt")
    x_incl = x[HG_LEVELS * c:(HG_LEVELS + 1) * c]
    x_after = x[(HG_LEVELS + 1) * c:(HG_LEVELS + 2) * c]
    x_total = x[(HG_LEVELS + 2) * c:(HG_LEVELS + 3) * c]
    o = _bdot(scores, pi, "nn") + _bdot(q * x_incl, s_t, "nt")
    s_new = s_t * jnp.concatenate([x_total, x_total], axis=0) + _bdot(pi, k * x_after, "tn")
    rstd = lax.rsqrt(jnp.mean(o * o, axis=-1, keepdims=True) + RMS_EPS)
    y = o * rstd * ng * (pg * jax.nn.sigmoid(pg))
    return y, s_new


def _hg_specs(t, reverse):
    nrb = t // HG_ROWS
    rb = (lambda r: nrb - 1 - r) if reverse else (lambda r: r)
    proj = pl.BlockSpec((4, HG_ROWS, HG_DIM), lambda h, r: (0, rb(r), h))
    vec = pl.BlockSpec((1, HG_DIM), lambda h, r: (0, h))
    cm = pl.BlockSpec(((HG_LEVELS + 3) * HG_CHUNK, HG_CHUNK), lambda h, r: (0, 0))
    mk = pl.BlockSpec(((HG_LEVELS + 1) * HG_CHUNK, HG_CHUNK), lambda h, r: (0, 0))
    rows = pl.BlockSpec((HG_ROWS, HG_DIM), lambda h, r: (rb(r), h))
    states = pl.BlockSpec((None, HG_ROWS // HG_CHUNK, HG_DIM, HG_DIM), lambda h, r: (h, rb(r), 0, 0))
    return proj, vec, cm, mk, rows, states


def _hg_fwd(proj, l0, l1, l2, ng):
    t = proj.shape[1]
    n_in = HG_ROWS // HG_CHUNK
    cm, mk = _hg_constants()
    p_spec, vec, cm_spec, mk_spec, rows, st_spec = _hg_specs(t, False)

    def body(p_ref, l0_ref, l1_ref, l2_ref, ng_ref, cm_ref, mk_ref, y_ref, st_ref, s_ref):
        @pl.when(pl.program_id(1) == 0)
        def _():
            s_ref[...] = jnp.zeros_like(s_ref)

        def step(ci, carry):
            sl = pl.ds(pl.multiple_of(ci * HG_CHUNK, HG_CHUNK), HG_CHUNK)
            s_t = s_ref[...]
            st_ref[ci] = s_t
            y, s_new = _hg_chunk(p_ref[0, sl, :], p_ref[1, sl, :], p_ref[2, sl, :], p_ref[3, sl, :],
                                 l0_ref[...], l1_ref[...], l2_ref[...], ng_ref[...], s_t,
                                 cm_ref[...], mk_ref[...])
            y_ref[sl, :] = y.astype(BF16)
            s_ref[...] = s_new
            return carry

        lax.fori_loop(0, n_in, step, 0, unroll=8)

    return pl.pallas_call(
        body, name="hg_fwd", grid=(HG_HEADS, t // HG_ROWS),
        out_shape=(jax.ShapeDtypeStruct((t, D_MODEL), BF16),
                   jax.ShapeDtypeStruct((HG_HEADS, t // HG_CHUNK, HG_DIM, HG_DIM), F32)),
        in_specs=[p_spec, vec, vec, vec, vec, cm_spec, mk_spec], out_specs=(rows, st_spec),
        scratch_shapes=[pltpu.VMEM((HG_DIM, HG_DIM), F32)],
        compiler_params=_cp("parallel", "arbitrary"),
    )(proj, l0, l1, l2, ng, cm, mk)


def _hg_bwd(proj, states, dy, l0, l1, l2, ng):
    t = proj.shape[1]
    n_in = HG_ROWS // HG_CHUNK
    cm, mk = _hg_constants()
    p_spec, vec, cm_spec, mk_spec, rows, st_spec = _hg_specs(t, True)

    def body(p_ref, st_ref, dy_ref, l0_ref, l1_ref, l2_ref, ng_ref, cm_ref, mk_ref,
             dp_ref, dl0_ref, dl1_ref, dl2_ref, dng_ref, ds_ref):
        @pl.when(pl.program_id(1) == 0)
        def _():
            ds_ref[...] = jnp.zeros_like(ds_ref)
            for r in (dl0_ref, dl1_ref, dl2_ref, dng_ref):
                r[...] = jnp.zeros_like(r)

        def step(it, carry):
            ci = n_in - 1 - it
            sl = pl.ds(pl.multiple_of(ci * HG_CHUNK, HG_CHUNK), HG_CHUNK)
            fn = functools.partial(_hg_chunk, cm=cm_ref[...], mk=mk_ref[...])
            _, vjp = jax.vjp(fn, p_ref[0, sl, :], p_ref[1, sl, :], p_ref[2, sl, :], p_ref[3, sl, :],
                             l0_ref[...], l1_ref[...], l2_ref[...], ng_ref[...], st_ref[ci])
            dq, df, di, dg, d0, d1, d2, dn, ds = vjp((dy_ref[sl, :], ds_ref[...]))
            dp_ref[0, sl, :] = dq.astype(BF16)
            dp_ref[1, sl, :] = df.astype(BF16)
            dp_ref[2, sl, :] = di.astype(BF16)
            dp_ref[3, sl, :] = dg.astype(BF16)
            dl0_ref[...] += d0
            dl1_ref[...] += d1
            dl2_ref[...] += d2
            dng_ref[...] += dn
            ds_ref[...] = ds
            return carry

        lax.fori_loop(0, n_in, step, 0, unroll=8)

    v_shape = jax.ShapeDtypeStruct((1, D_MODEL), F32)
    return pl.pallas_call(
        body, name="hg_bwd", grid=(HG_HEADS, t // HG_ROWS),
        out_shape=(jax.ShapeDtypeStruct((4, t, D_MODEL), BF16), v_shape, v_shape, v_shape, v_shape),
        in_specs=[p_spec, st_spec, rows, vec, vec, vec, vec, cm_spec, mk_spec],
        out_specs=(p_spec, vec, vec, vec, vec),
        scratch_shapes=[pltpu.VMEM((HG_DIM, HG_DIM), F32)],
        compiler_params=_cp("parallel", "arbitrary"),
    )(proj, states, dy, l0, l1, l2, ng, cm, mk)


_SQRT_HALF = 0.7071067811865476
_INV_SQRT_2PI = 0.3989422804014327


def _gelu(x):
    return 0.5 * x * (1.0 + lax.erf(x * _SQRT_HALF))


def _gelu_grad(x):
    return 0.5 * (1.0 + lax.erf(x * _SQRT_HALF)) + x * (_INV_SQRT_2PI * jnp.exp(-0.5 * x * x))


def _tril(n):
    return (lax.broadcasted_iota(jnp.int32, (n, n), 0) >= lax.broadcasted_iota(jnp.int32, (n, n), 1)).astype(F32)


def _sg_specs(d):
    row = lambda w: pl.BlockSpec((SG_CHUNK, w), lambda i: (i, 0))
    vec = pl.BlockSpec((1, d), lambda i: (0, 0))
    cube = pl.BlockSpec((SG_GROUPS, SG_CHUNK, SG_CHUNK), lambda i: (0, 0, 0))
    return row, vec, cube


def _sg_fwd(pre, ln_g, ln_b, w_s, bias):
    t = pre.shape[0]
    d = D_MODEL
    row, vec, cube = _sg_specs(d)

    def body(pre_ref, g_ref, b_ref, w_ref, bias_ref, y_ref, vln_ref):
        u = _gelu(pre_ref[:, :d])
        vhat, _ = _ln_stats(_gelu(pre_ref[:, d:]))
        vln_ref[...] = vhat * g_ref[...] + b_ref[...]
        tril = _tril(SG_CHUNK)
        for g in range(SG_GROUPS):
            cs = slice(g * SG_DIM, (g + 1) * SG_DIM)
            gate = _bdot_raw(w_ref[g] * tril, vln_ref[:, cs], "nn") + bias_ref[g]
            y_ref[:, cs] = (u[:, cs] * gate).astype(BF16)

    return pl.pallas_call(
        body, name="sg_fwd", grid=(t // SG_CHUNK,), out_shape=jax.ShapeDtypeStruct((t, d), BF16),
        in_specs=[row(2 * d), vec, vec, cube, cube], out_specs=row(d),
        scratch_shapes=[pltpu.VMEM((SG_CHUNK, d), F32)], compiler_params=_cp("parallel"),
    )(pre, ln_g, ln_b, w_s, bias)


def _sg_bwd(pre, dy, ln_g, ln_b, w_s, bias):
    t = pre.shape[0]
    d = D_MODEL
    row, vec, cube = _sg_specs(d)
    dbs_spec = pl.BlockSpec((SG_GROUPS, 8, SG_CHUNK), lambda i: (0, 0, 0))

    def body(pre_ref, dy_ref, g_ref, b_ref, w_ref, bias_ref, dpre_ref, dw_ref, dbs_ref, dlg_ref, dlb_ref,
             vln_ref, dvln_ref):
        @pl.when(pl.program_id(0) == 0)
        def _():
            for r in (dw_ref, dbs_ref, dlg_ref, dlb_ref):
                r[...] = jnp.zeros_like(r)

        pu = pre_ref[:, :d]
        pv = pre_ref[:, d:]
        u = _gelu(pu)
        vhat, rstd = _ln_stats(_gelu(pv))
        vln_ref[...] = vhat * g_ref[...] + b_ref[...]
        tril = _tril(SG_CHUNK)
        ones = jnp.ones((8, SG_DIM), F32)
        for g in range(SG_GROUPS):
            cs = slice(g * SG_DIM, (g + 1) * SG_DIM)
            wc = w_ref[g] * tril
            vg = vln_ref[:, cs]
            gate = _bdot_raw(wc, vg, "nn") + bias_ref[g]
            dyg = dy_ref[:, cs]
            dgate = dyg * u[:, cs]
            dpre_ref[:, cs] = (dyg * gate * _gelu_grad(pu[:, cs])).astype(BF16)
            dw_ref[g] += tril * _bdot_raw(dgate, vg, "nt")
            dbs_ref[g] += lax.dot_general(ones, dgate, _DIMS["nt"], precision=HI, preferred_element_type=F32)
            dvln_ref[:, cs] = _bdot_raw(wc, dgate, "tn")
        dvln = dvln_ref[...]
        dv, dg, db = _ln_bwd_math(vhat, rstd, dvln, g_ref[...])
        dlg_ref[...] += dg
        dlb_ref[...] += db
        dpre_ref[:, d:] = (dv * _gelu_grad(pv)).astype(BF16)

    return pl.pallas_call(
        body, name="sg_bwd", grid=(t // SG_CHUNK,),
        out_shape=(jax.ShapeDtypeStruct((t, 2 * d), BF16), jax.ShapeDtypeStruct(w_s.shape, F32),
                   jax.ShapeDtypeStruct((SG_GROUPS, 8, SG_CHUNK), F32),
                   jax.ShapeDtypeStruct((1, d), F32), jax.ShapeDtypeStruct((1, d), F32)),
        in_specs=[row(2 * d), row(d), vec, vec, cube, cube],
        out_specs=(row(2 * d), cube, dbs_spec, vec, vec),
        scratch_shapes=[pltpu.VMEM((SG_CHUNK, d), F32), pltpu.VMEM((SG_CHUNK, d), F32)],
        compiler_params=_cp("arbitrary"),
    )(pre, dy, ln_g, ln_b, w_s, bias)


FFN_ROWS = 256
FFN_COLS = 1408
HALO = 8


def _ffn_conv(a_ext, w_ref, cb_ref):
    a1 = pltpu.roll(a_ext, 1, 0)
    a2 = pltpu.roll(a_ext, 2, 0)
    return w_ref[0:1, :] * a2 + w_ref[1:2, :] * a1 + w_ref[2:3, :] * a_ext + cb_ref[...], a1, a2


def _ffn_fwd(hh, conv_w, conv_b):
    t = hh.shape[1]
    nb8 = FFN_ROWS // HALO
    main = pl.BlockSpec((2, FFN_ROWS, FFN_COLS), lambda c, r: (0, r, c))
    prev = pl.BlockSpec((None, HALO, FFN_COLS), lambda c, r: (0, jnp.maximum(r * nb8 - 1, 0), c))
    wspec = pl.BlockSpec((3, FFN_COLS), lambda c, r: (0, c))
    bspec = pl.BlockSpec((1, FFN_COLS), lambda c, r: (0, c))

    def body(m_ref, p_ref, w_ref, cb_ref, z_ref):
        prev = jnp.where(pl.program_id(1) == 0, 0.0, p_ref[...])
        a_ext = jnp.concatenate([prev, m_ref[0]], axis=0)
        a, _, _ = _ffn_conv(a_ext, w_ref, cb_ref)
        a = a[HALO:]
        z_ref[...] = (a * jax.nn.sigmoid(a) * m_ref[1]).astype(BF16)

    return pl.pallas_call(
        body, name="ffn_fwd", grid=(D_FF // FFN_COLS, t // FFN_ROWS),
        out_shape=jax.ShapeDtypeStruct((t, D_FF), BF16),
        in_specs=[main, prev, wspec, bspec],
        out_specs=pl.BlockSpec((FFN_ROWS, FFN_COLS), lambda c, r: (r, c)),
        compiler_params=_cp("parallel", "parallel"),
    )(hh, hh, conv_w, conv_b)


def _ffn_bwd(hh, dz, conv_w, conv_b):
    t = hh.shape[1]
    nb8 = FFN_ROWS // HALO
    last8 = t // HALO - 1
    nr = t // FFN_ROWS
    main = pl.BlockSpec((2, FFN_ROWS, FFN_COLS), lambda c, r: (0, r, c))
    prev = pl.BlockSpec((None, HALO, FFN_COLS), lambda c, r: (0, jnp.maximum(r * nb8 - 1, 0), c))
    nxt = pl.BlockSpec((2, HALO, FFN_COLS), lambda c, r: (0, jnp.minimum((r + 1) * nb8, last8), c))
    dmain = pl.BlockSpec((FFN_ROWS, FFN_COLS), lambda c, r: (r, c))
    dnxt = pl.BlockSpec((HALO, FFN_COLS), lambda c, r: (jnp.minimum((r + 1) * nb8, last8), c))
    wspec = pl.BlockSpec((3, FFN_COLS), lambda c, r: (0, c))
    bspec = pl.BlockSpec((1, FFN_COLS), lambda c, r: (0, c))

    def body(m_ref, p_ref, n_ref, dz_ref, dzn_ref, w_ref, cb_ref, dh_ref, dw_ref, dcb_ref):
        r = pl.program_id(1)

        @pl.when(r == 0)
        def _():
            dw_ref[...] = jnp.zeros_like(dw_ref)
            dcb_ref[...] = jnp.zeros_like(dcb_ref)

        prev = jnp.where(r == 0, 0.0, p_ref[...])
        a_ext = jnp.concatenate([prev, m_ref[0], n_ref[0]], axis=0)
        a, a1, a2 = _ffn_conv(a_ext, w_ref, cb_ref)
        a = a[HALO:]
        b_ext = jnp.concatenate([m_ref[1], n_ref[1]], axis=0)
        dz_ext = jnp.concatenate([dz_ref[...], jnp.where(r == nr - 1, 0.0, dzn_ref[...])], axis=0)
        sig = jax.nn.sigmoid(a)
        da = dz_ext * b_ext * (sig * (1.0 + a * (1.0 - sig)))
        n_ext = FFN_ROWS + HALO
        da_p1 = pltpu.roll(da, n_ext - 1, 0)
        da_p2 = pltpu.roll(da, n_ext - 2, 0)
        da_raw = w_ref[2:3, :] * da + w_ref[1:2, :] * da_p1 + w_ref[0:1, :] * da_p2
        dh_ref[0] = da_raw[:FFN_ROWS].astype(BF16)
        dh_ref[1] = (dz_ref[...] * (a * sig)[:FFN_ROWS]).astype(BF16)
        dam = da[:FFN_ROWS]
        rows = slice(HALO, HALO + FFN_ROWS)
        dw_ref[0:1, :] += jnp.sum(dam * a2[rows], axis=0, keepdims=True)
        dw_ref[1:2, :] += jnp.sum(dam * a1[rows], axis=0, keepdims=True)
        dw_ref[2:3, :] += jnp.sum(dam * a_ext[rows], axis=0, keepdims=True)
        dcb_ref[...] += jnp.sum(dam, axis=0, keepdims=True)

    return pl.pallas_call(
        body, name="ffn_bwd", grid=(D_FF // FFN_COLS, nr),
        out_shape=(jax.ShapeDtypeStruct((2, t, D_FF), BF16), jax.ShapeDtypeStruct((3, D_FF), F32),
                   jax.ShapeDtypeStruct((1, D_FF), F32)),
        in_specs=[main, prev, nxt, dmain, dnxt, wspec, bspec],
        out_specs=(main, wspec, bspec), compiler_params=_cp("parallel", "arbitrary"),
    )(hh, hh, hh, dz, dz, conv_w, conv_b)


def _adamw_math(w, g, m, v):
    m = ADAM_B1 * m + (1.0 - ADAM_B1) * g
    v = ADAM_B2 * v + (1.0 - ADAM_B2) * (g * g)
    m_hat = m / (1.0 - ADAM_B1 ** ADAM_STEP)
    v_hat = v / (1.0 - ADAM_B2 ** ADAM_STEP)
    delta = -ADAM_LR * (m_hat / (jnp.sqrt(v_hat) + ADAM_EPS) + ADAM_WD * w)
    return delta, m, v


def _as2d(shape):
    n = int(np.prod(shape))
    c = shape[-1] if shape[-1] % 128 == 0 else (128 if n % 128 == 0 else shape[-1])
    return n // c, c


def _adamw(w, g, m, v, name, parts=None, own=None, is_me=None):
    shape = w.shape
    r, c = _as2d(shape)
    br = r if r <= 512 else _pick(r, (256, 176, 128, 64, 8))
    blk = pl.BlockSpec((br, c), lambda i: (i, 0))
    w2, m2, v2 = (a.reshape(r, c) for a in (w, m, v))
    extra, extra_specs = [], []
    if parts is not None:
        g_in = parts.reshape(N_DEV, r, c)
        g_spec = pl.BlockSpec((N_DEV, br, c), lambda i: (0, i, 0))
        extra = [own.reshape(r, c), is_me]
        extra_specs = [blk, pl.BlockSpec(memory_space=pltpu.SMEM)]
    else:
        g_in = g.reshape(r, c)
        g_spec = blk

    def body(w_ref, g_ref, m_ref, v_ref, *rest):
        go_ref, d_ref, mo_ref, vo_ref = rest[-4:]
        if parts is not None:
            own_ref, me_ref = rest[:2]
            gsum = None
            for i in range(N_DEV):
                term = jnp.where(me_ref[i] > 0.5, own_ref[...], g_ref[i]).astype(F32)
                gsum = term if gsum is None else gsum + term
        else:
            gsum = g_ref[...]
        delta, m_new, v_new = _adamw_math(w_ref[...], gsum, m_ref[...], v_ref[...])
        go_ref[...] = gsum
        d_ref[...] = delta
        mo_ref[...] = m_new
        vo_ref[...] = v_new

    o = jax.ShapeDtypeStruct((r, c), F32)
    outs = pl.pallas_call(
        body, name=name, grid=(r // br,), out_shape=(o, o, o, o),
        in_specs=[blk, g_spec, blk, blk] + extra_specs, out_specs=(blk, blk, blk, blk),
        compiler_params=_cp("parallel"),
    )(w2, g_in, m2, v2, *extra)
    return tuple(a.reshape(shape) for a in outs)


def _me_and_peers():
    x, y, c = lax.axis_index("x"), lax.axis_index("y"), lax.axis_index("c")
    me = 4 * x + 2 * y + c
    peers = []
    for k in range(1, N_DEV):
        kx, ky, kc = (k >> 2) & 1, (k >> 1) & 1, k & 1
        px, py, pc = x ^ kx, y ^ ky, c ^ kc
        peers.append(((px, py, pc), 4 * px + 2 * py + pc))
    return me, peers


def _all_gather(shard, name):
    def body(x_ref, o_ref, send_sems, recv_sems, local_sem):
        me, peers = _me_and_peers()
        mine = pltpu.make_async_copy(x_ref, o_ref.at[me], local_sem)
        mine.start()
        sends = []
        for k, (dev, _) in enumerate(peers):
            cp = pltpu.make_async_remote_copy(src_ref=x_ref, dst_ref=o_ref.at[me], send_sem=send_sems.at[k],
                                              recv_sem=recv_sems.at[k], device_id=dev, device_id_type=MESH)
            cp.start()
            sends.append(cp)
        for k, (dev, idx) in enumerate(peers):
            pltpu.make_async_remote_copy(src_ref=x_ref, dst_ref=o_ref.at[idx], send_sem=send_sems.at[k],
                                         recv_sem=recv_sems.at[k], device_id=dev, device_id_type=MESH).wait_recv()
        for cp in sends:
            cp.wait_send()
        mine.wait()

    return pl.pallas_call(
        body, name=name, out_shape=jax.ShapeDtypeStruct((N_DEV,) + shard.shape, shard.dtype),
        in_specs=[ANY], out_specs=ANY,
        scratch_shapes=[pltpu.SemaphoreType.DMA((N_DEV - 1,)), pltpu.SemaphoreType.DMA((N_DEV - 1,)),
                        pltpu.SemaphoreType.DMA],
        compiler_params=pltpu.CompilerParams(has_side_effects=True),
    )(shard)


HBM = pl.BlockSpec(memory_space=pltpu.HBM)
SEM = pl.BlockSpec(memory_space=pltpu.SEMAPHORE)
EFFECT = pltpu.SideEffectType.DATAFLOW_SIDE_EFFECTING


def _xchg_start(parts, land, *, dep, name):
    n_buf = 1 if parts is None else 2

    def body(*refs):
        land_ref = refs[n_buf - 1]
        send_sem, recv_sem = refs[n_buf + 1], refs[n_buf + 2]
        token = refs[2 * n_buf + 3]
        me, peers = _me_and_peers()
        for dev, idx in peers:
            src = land_ref.at[me] if parts is None else refs[0].at[idx]
            pltpu.make_async_remote_copy(src_ref=src, dst_ref=land_ref.at[me], send_sem=send_sem, recv_sem=recv_sem,
                                         device_id=dev, device_id_type=MESH).start()
        token[...] = jnp.zeros_like(token)

    bufs = [land] if parts is None else [parts, land]
    return pl.pallas_call(
        body, name=name,
        out_shape=(pltpu.SemaphoreType.DMA(()), pltpu.SemaphoreType.DMA(()),
                   *[pltpu.HBM(b.shape, b.dtype) for b in bufs], jax.ShapeDtypeStruct((8, 128), F32)),
        in_specs=(*[HBM] * n_buf, ANY), out_specs=(SEM, SEM, *[HBM] * n_buf, pl.BlockSpec(memory_space=pltpu.VMEM)),
        input_output_aliases={i: 2 + i for i in range(n_buf)},
        compiler_params=pltpu.CompilerParams(has_side_effects=EFFECT),
    )(*[pltpu.with_memory_space_constraint(b, pltpu.HBM) for b in bufs], dep)


def _xchg_wait(handles, after, name):
    send_sem, recv_sem, *bufs, _ = handles
    n_buf = len(bufs)

    def body(*refs):
        land_ref = refs[n_buf - 1]
        send_sem, recv_sem = refs[n_buf], refs[n_buf + 1]
        x, y, c = lax.axis_index("x"), lax.axis_index("y"), lax.axis_index("c")
        seven = land_ref.at[pl.ds(0, N_DEV - 1)]
        cp = pltpu.make_async_remote_copy(src_ref=seven, dst_ref=seven, send_sem=send_sem, recv_sem=recv_sem,
                                          device_id=(x, y, c), device_id_type=MESH)
        cp.wait_send()
        cp.wait_recv()

    return pl.pallas_call(
        body, name=name, out_shape=tuple(pltpu.HBM(b.shape, b.dtype) for b in bufs),
        in_specs=(*[HBM] * n_buf, SEM, SEM, ANY), out_specs=tuple([HBM] * n_buf),
        input_output_aliases={i: i for i in range(n_buf)},
        compiler_params=pltpu.CompilerParams(has_side_effects=EFFECT),
    )(*bufs, send_sem, recv_sem, after)


def _all_reduce_small(v, name):
    r, c = v.shape

    def body(x_ref, o_ref, slots, send_sems, recv_sems):
        me, peers = _me_and_peers()
        sends = []
        for k, (dev, _) in enumerate(peers):
            cp = pltpu.make_async_remote_copy(src_ref=x_ref, dst_ref=slots.at[me], send_sem=send_sems.at[k],
                                              recv_sem=recv_sems.at[k], device_id=dev, device_id_type=MESH)
            cp.start()
            sends.append(cp)
        slots[me] = x_ref[...]
        for k, (dev, idx) in enumerate(peers):
            pltpu.make_async_remote_copy(src_ref=x_ref, dst_ref=slots.at[idx], send_sem=send_sems.at[k],
                                         recv_sem=recv_sems.at[k], device_id=dev, device_id_type=MESH).wait_recv()
        for cp in sends:
            cp.wait_send()
        acc = slots[0]
        for i in range(1, N_DEV):
            acc = acc + slots[i]
        o_ref[...] = acc

    return pl.pallas_call(
        body, name=name, out_shape=jax.ShapeDtypeStruct((r, c), F32),
        in_specs=[pl.BlockSpec(memory_space=pltpu.VMEM)], out_specs=pl.BlockSpec(memory_space=pltpu.VMEM),
        scratch_shapes=[pltpu.VMEM((N_DEV, r, c), F32), pltpu.SemaphoreType.DMA((N_DEV - 1,)),
                        pltpu.SemaphoreType.DMA((N_DEV - 1,))],
        compiler_params=pltpu.CompilerParams(has_side_effects=True, vmem_limit_bytes=VMEM_LIMIT),
    )(v)


def _local_step(x, target, lb_logits, p, weight, send_grad):
    g = {}
    l0, l1, l2 = lb_logits[0:1], lb_logits[1:2], lb_logits[2:3]
    x_bf = x.astype(BF16)

    proj0 = _mm(x_bf, weight("hg_w_in", x_bf), mode="nn", b_cs=True, o_cs=4, name="mm_hg_in")
    y0, states = _hg_fwd(proj0, l0, l1, l2, p["hg_norm_g"])
    mixed0 = _mm(y0, weight("hg_w_out", y0), mode="nn", name="mm_hg_out")
    hh_l, z_l = [], []

    def ffn_forward(layer, h_in_bf):
        hh = _mm(h_in_bf, weight(f"ffn_w_up{layer}", h_in_bf), mode="nn", b_cs=True, o_cs=2, name=f"mm_up{layer}")
        z = _ffn_fwd(hh, p["ffn_conv_w"][layer], p["ffn_conv_b"][layer])
        ffn = _mm(z, weight(f"ffn_w_down{layer}", z), mode="nn", name=f"mm_down{layer}")
        hh_l.append(hh)
        z_l.append(z)
        return ffn

    h1, h1b = _ln_fwd(x, mixed0, p["ln1_g"][0], p["ln1_b"][0], "ln1_0")
    ffn0 = ffn_forward(0, h1b)
    h2, h2b = _ln_fwd(h1, ffn0, p["ln2_g"][0], p["ln2_b"][0], "ln2_0")
    pre1 = _mm(h2b, weight("sg_w_in", h2b), mode="nn", b_cs=True, name="mm_sg_in")
    y1 = _sg_fwd(pre1, p["sg_ln_g"], p["sg_ln_b"], p["sg_w_s"], p["sg_bias"])
    mixed1 = _mm(y1, weight("sg_w_out", y1), mode="nn", name="mm_sg_out")
    h3, h3b = _ln_fwd(h2, mixed1, p["ln1_g"][1], p["ln1_b"][1], "ln1_1")
    ffn1 = ffn_forward(1, h3b)

    loss, ds, dres, g["ln2_g1"], g["ln2_b1"] = _ln_loss_bwd(h3, ffn1, target, p["ln2_g"][1], p["ln2_b"][1], "ln2_1_loss")

    def ffn_backward(layer, ds_bf, h_in_bf):
        tok = send_grad(f"ffn_w_down{layer}",
                        _mm(z_l[layer], ds_bf, mode="tn", out_dtype=BF16, name=f"mm_dw_down{layer}"))
        dz = _mm(ds_bf, weight(f"ffn_w_down{layer}", None), mode="nt", name=f"mm_d_z{layer}", deps=(tok,))
        dhh, g[f"conv_w{layer}"], g[f"conv_b{layer}"] = _ffn_bwd(hh_l[layer], dz, p["ffn_conv_w"][layer],
                                                                   p["ffn_conv_b"][layer])
        tok = send_grad(f"ffn_w_up{layer}", _mm(h_in_bf, dhh, mode="tn", b_cs=True, o_cs=N_DEV, out_dtype=BF16,
                                                name=f"mm_dw_up{layer}"))
        return _mm(dhh, weight(f"ffn_w_up{layer}", None), mode="nt", a_cs=True, b_cs=True, name=f"mm_d_up{layer}",
                   deps=(tok,))

    dh3 = ffn_backward(1, ds, h3b)
    ds, dres, g["ln1_g1"], g["ln1_b1"] = _ln_bwd(h2, mixed1, dres, dh3, p["ln1_g"][1], "ln1_1_bwd")
    tok = send_grad("sg_w_out", _mm(y1, ds, mode="tn", out_dtype=BF16, name="mm_dw_sg_out"))
    dy1 = _mm(ds, weight("sg_w_out", None), mode="nt", name="mm_d_sg_out", deps=(tok,))
    dpre1, g["sg_w_s"], dbs, g["sg_ln_g"], g["sg_ln_b"] = _sg_bwd(pre1, dy1, p["sg_ln_g"], p["sg_ln_b"],
                                                                    p["sg_w_s"], p["sg_bias"])
    g["sg_b_s"] = dbs[:, 0, :]
    tok = send_grad("sg_w_in", _mm(h2b, dpre1, mode="tn", o_cs=N_DEV, out_dtype=BF16, name="mm_dw_sg_in"))
    dh2 = _mm(dpre1, weight("sg_w_in", None), mode="nt", b_cs=True, name="mm_d_sg_in", deps=(tok,))
    ds, dres, g["ln2_g0"], g["ln2_b0"] = _ln_bwd(h1, ffn0, dres, dh2, p["ln2_g"][0], "ln2_0_bwd")
    dh1 = ffn_backward(0, ds, h1b)
    ds, dres, g["ln1_g0"], g["ln1_b0"] = _ln_bwd(x, mixed0, dres, dh1, p["ln1_g"][0], "ln1_0_bwd")
    tok = send_grad("hg_w_out", _mm(y0, ds, mode="tn", out_dtype=BF16, name="mm_dw_hg_out"))
    dy0 = _mm(ds, weight("hg_w_out", None), mode="nt", name="mm_d_hg_out", deps=(tok,))
    dproj, d0, d1, d2, g["hg_norm_g"] = _hg_bwd(proj0, states, dy0, l0, l1, l2, p["hg_norm_g"])
    g["lb_logits"] = jnp.concatenate([d0, d1, d2], axis=0)
    tok = send_grad("hg_w_in", _mm(x_bf, dproj, mode="tn", b_cs=True, o_cs=N_DEV, out_dtype=BF16, name="mm_dw_hg_in"))
    grad_x = _mm(dproj, weight("hg_w_in", None), mode="nt", a_cs=True, b_cs=True, addend=dres, name="mm_d_hg_in",
                 deps=(tok,))
    return loss[0, 0], grad_x, g


_SMALL = ("lb_logits", "hg_norm_g", "sg_ln_g", "sg_ln_b", "sg_w_s", "sg_b_s", "ffn_conv_w", "ffn_conv_b",
          "ln1_g", "ln1_b", "ln2_g", "ln2_b")
_NAMES = ("lb_logits", "hg_w_in", "hg_norm_g", "hg_w_out", "sg_w_in", "sg_ln_g", "sg_ln_b", "sg_w_s", "sg_b_s",
          "sg_w_out", "ffn_w_up", "ffn_conv_w", "ffn_conv_b", "ffn_w_down", "ln1_g", "ln1_b", "ln2_g", "ln2_b")


def kernel(x, lb_logits, hg_w_in, hg_norm_g, hg_w_out, sg_w_in, sg_ln_g, sg_ln_b, sg_w_s, sg_b_s, sg_w_out, ffn_w_up, ffn_conv_w, ffn_conv_b, ffn_w_down, ln1_g, ln1_b, ln2_g, ln2_b, loss_target, m_lb_logits, m_hg_w_in, m_hg_norm_g, m_hg_w_out, m_sg_w_in, m_sg_ln_g, m_sg_ln_b, m_sg_w_s, m_sg_b_s, m_sg_w_out, m_ffn_w_up, m_ffn_conv_w, m_ffn_conv_b, m_ffn_w_down, m_ln1_g, m_ln1_b, m_ln2_g, m_ln2_b, v_lb_logits, v_hg_w_in, v_hg_norm_g, v_hg_w_out, v_sg_w_in, v_sg_ln_g, v_sg_ln_b, v_sg_w_s, v_sg_b_s, v_sg_w_out, v_ffn_w_up, v_ffn_conv_w, v_ffn_conv_b, v_ffn_w_down, v_ln1_g, v_ln1_b, v_ln2_g, v_ln2_b):
    w = dict(lb_logits=lb_logits, hg_w_in=hg_w_in, hg_norm_g=hg_norm_g, hg_w_out=hg_w_out, sg_w_in=sg_w_in,
             sg_ln_g=sg_ln_g, sg_ln_b=sg_ln_b, sg_w_s=sg_w_s, sg_b_s=sg_b_s, sg_w_out=sg_w_out, ffn_w_up=ffn_w_up,
             ffn_conv_w=ffn_conv_w, ffn_conv_b=ffn_conv_b, ffn_w_down=ffn_w_down, ln1_g=ln1_g, ln1_b=ln1_b,
             ln2_g=ln2_g, ln2_b=ln2_b)
    m = dict(lb_logits=m_lb_logits, hg_w_in=m_hg_w_in, hg_norm_g=m_hg_norm_g, hg_w_out=m_hg_w_out,
             sg_w_in=m_sg_w_in, sg_ln_g=m_sg_ln_g, sg_ln_b=m_sg_ln_b, sg_w_s=m_sg_w_s, sg_b_s=m_sg_b_s,
             sg_w_out=m_sg_w_out, ffn_w_up=m_ffn_w_up, ffn_conv_w=m_ffn_conv_w, ffn_conv_b=m_ffn_conv_b,
             ffn_w_down=m_ffn_w_down, ln1_g=m_ln1_g, ln1_b=m_ln1_b, ln2_g=m_ln2_g, ln2_b=m_ln2_b)
    v = dict(lb_logits=v_lb_logits, hg_w_in=v_hg_w_in, hg_norm_g=v_hg_norm_g, hg_w_out=v_hg_w_out,
             sg_w_in=v_sg_w_in, sg_ln_g=v_sg_ln_g, sg_ln_b=v_sg_ln_b, sg_w_s=v_sg_w_s, sg_b_s=v_sg_b_s,
             sg_w_out=v_sg_w_out, ffn_w_up=v_ffn_w_up, ffn_conv_w=v_ffn_conv_w, ffn_conv_b=v_ffn_conv_b,
             ffn_w_down=v_ffn_w_down, ln1_g=v_ln1_g, ln1_b=v_ln1_b, ln2_g=v_ln2_g, ln2_b=v_ln2_b)
    me = 4 * lax.axis_index("x") + 2 * lax.axis_index("y") + lax.axis_index("c")
    is_me = (jnp.arange(N_DEV) == me).astype(F32)
    d = D_MODEL

    shards = [("hg_w_in", hg_w_in[0]), ("hg_w_out", hg_w_out[0]), ("ffn_w_up0", ffn_w_up[0]),
              ("ffn_w_down0", ffn_w_down[0]), ("sg_w_in", sg_w_in[0]), ("sg_w_out", sg_w_out[0]),
              ("ffn_w_up1", ffn_w_up[1]), ("ffn_w_down1", ffn_w_down[1])]
    row_sharded = {"hg_w_out": (d, d), "sg_w_out": (d, d), "ffn_w_down0": (D_FF, d), "ffn_w_down1": (D_FF, d)}
    gathers, gathered, dep = {}, {}, lb_logits
    for n, shard in shards:
        land = lax.dynamic_update_index_in_dim(lax.empty((N_DEV,) + shard.shape, BF16), shard.astype(BF16), me, 0)
        gathers[n] = _xchg_start(None, land, dep=dep, name=f"ag_{n}")
        dep = gathers[n][-1]

    last_start = dep

    def weight(n, after):
        if n not in gathered:
            if n == shards[0][0]:
                after = last_start
            (full,) = _xchg_wait(gathers[n], after, f"agw_{n}")
            gathered[n] = full.reshape(row_sharded[n]) if n in row_sharded else full
        return gathered[n]

    grad_sends = {}

    def send_grad(n, parts):
        shape = (N_DEV,) + dict(shards)[n].shape
        grad_sends[n] = _xchg_start(parts.reshape(shape), lax.empty(shape, parts.dtype), dep=is_me, name=f"rs_{n}")
        return grad_sends[n][-1]

    p = {}
    sv = jnp.zeros((8, 768), F32)
    sv = sv.at[0, :256].set(sg_ln_g[0]).at[1, :256].set(sg_ln_b[0]).at[2:8, :704].set(ffn_conv_w.reshape(6, 704))
    sv = _all_gather(sv, "ag_small")
    p["sg_ln_g"] = sv[:, 0, :256].reshape(1, d)
    p["sg_ln_b"] = sv[:, 1, :256].reshape(1, d)
    conv_w_full = jnp.transpose(sv[:, 2:8, :704].reshape(N_DEV, DEPTH, 3, 704), (1, 2, 0, 3)).reshape(DEPTH, 3, D_FF)
    p["ffn_conv_w"] = [conv_w_full[l] for l in range(DEPTH)]
    p["ffn_conv_b"] = [ffn_conv_b[l:l + 1] for l in range(DEPTH)]
    p["hg_norm_g"] = hg_norm_g
    p["sg_w_s"] = sg_w_s[0]
    p["sg_bias"] = jnp.broadcast_to(sg_b_s[0][:, :, None], (SG_GROUPS, SG_CHUNK, SG_DIM))
    for n in ("ln1_g", "ln1_b", "ln2_g", "ln2_b"):
        p[n] = [w[n][l:l + 1] for l in range(DEPTH)]

    loss_part, grad_x, g = _local_step(x[0], loss_target[0], lb_logits, p, weight, send_grad)
    loss = lax.psum(loss_part, ("x", "y", "c"))

    small = [g["lb_logits"], g["hg_norm_g"], g["sg_ln_g"], g["sg_ln_b"], g["sg_w_s"], g["sg_b_s"],
             g["conv_w0"], g["conv_w1"], g["conv_b0"], g["conv_b1"],
             g["ln1_g0"], g["ln1_g1"], g["ln1_b0"], g["ln1_b1"], g["ln2_g0"], g["ln2_g1"], g["ln2_b0"], g["ln2_b1"]]
    sizes = [int(np.prod(a.shape)) for a in small]
    flat = jnp.concatenate([a.reshape(-1) for a in small])
    total = flat.shape[0]
    rows = -(-total // 128)
    rows = -(-rows // 8) * 8
    flat = jnp.pad(flat, (0, rows * 128 - total)).reshape(rows, 128)
    red = _all_reduce_small(flat, "ar_small").reshape(-1)
    offs = np.cumsum([0] + sizes)
    r = [red[offs[i]:offs[i + 1]].reshape(small[i].shape) for i in range(len(small))]
    gs = {}
    gs["lb_logits"] = r[0]
    gs["hg_norm_g"] = r[1]
    gs["sg_ln_g"] = lax.dynamic_slice(r[2], (0, me * 256), (1, 256))
    gs["sg_ln_b"] = lax.dynamic_slice(r[3], (0, me * 256), (1, 256))
    gs["sg_w_s"] = r[4][None]
    gs["sg_b_s"] = r[5][None]
    gs["ffn_conv_w"] = lax.dynamic_slice(jnp.stack([r[6], r[7]]), (0, 0, me * 704), (DEPTH, 3, 704))
    gs["ffn_conv_b"] = jnp.concatenate([r[8], r[9]], axis=0)
    gs["ln1_g"] = jnp.concatenate([r[10], r[11]], axis=0)
    gs["ln1_b"] = jnp.concatenate([r[12], r[13]], axis=0)
    gs["ln2_g"] = jnp.concatenate([r[14], r[15]], axis=0)
    gs["ln2_b"] = jnp.concatenate([r[16], r[17]], axis=0)

    out_g, out_d, out_m, out_v = {}, {}, {}, {}
    for n in _SMALL:
        out_g[n], out_d[n], out_m[n], out_v[n] = _adamw(w[n], gs[n], m[n], v[n], f"adamw_{n}")

    res, after = {}, out_v["ln2_b"]
    for n in ("ffn_w_down1", "ffn_w_up1", "sg_w_out", "sg_w_in", "ffn_w_down0", "ffn_w_up0", "hg_w_out", "hg_w_in"):
        parts, recv = _xchg_wait(grad_sends[n], after, f"rsw_{n}")
        own = lax.dynamic_index_in_dim(parts, me, 0, keepdims=False)
        base, layer = (n[:-1], int(n[-1])) if n[-1] in "01" else (n, 0)
        res[n] = _adamw(w[base][layer], None, m[base][layer], v[base][layer], f"adamw_{n}", parts=recv, own=own,
                        is_me=is_me)
        after = res[n][3]
    for n in ("hg_w_in", "hg_w_out", "sg_w_in", "sg_w_out"):
        out_g[n], out_d[n], out_m[n], out_v[n] = (a[None] for a in res[n])
    for n in ("ffn_w_up", "ffn_w_down"):
        out_g[n], out_d[n], out_m[n], out_v[n] = (jnp.stack([res[n + "0"][i], res[n + "1"][i]]) for i in range(4))

    return (loss, grad_x[None], *[out_g[n] for n in _NAMES], *[out_d[n] for n in _NAMES],
            *[out_m[n] for n in _NAMES], *[out_v[n] for n in _NAMES])
```

```python
import functools

import numpy as np
import jax
import jax.numpy as jnp
from jax import lax
from jax.experimental import pallas as pl
from jax.experimental.pallas import tpu as pltpu

F32 = jnp.float32
BF16 = jnp.bfloat16
HI = lax.Precision.HIGHEST

N_DEV = 8
D_MODEL = 2048
HG_HEADS = 16
HG_DIM = 128
HG_CHUNK = 64
SG_GROUPS = 16
SG_DIM = 128
SG_CHUNK = 128
D_FF = 5632
DEPTH = 2
ALPHA = (2 * DEPTH) ** 0.25
LN_EPS = 1e-5
RMS_EPS = 1e-6
ADAM_LR = 0.001
ADAM_B1 = 0.9
ADAM_B2 = 0.999
ADAM_EPS = 1e-08
ADAM_WD = 0.01
ADAM_STEP = 10

VMEM_LIMIT = 56 * 1024 * 1024
MESH = pl.DeviceIdType.MESH
ANY = pl.BlockSpec(memory_space=pl.ANY)


def _cp(*sem):
    return pltpu.CompilerParams(dimension_semantics=sem, vmem_limit_bytes=VMEM_LIMIT)


def _pick(n, cands):
    for c in cands:
        if n % c == 0:
            return c
    raise ValueError(f"no tile for {n} in {cands}")


_DIMS = {"nn": (((1,), (0,)), ((), ())), "nt": (((1,), (1,)), ((), ())), "tn": (((0,), (0,)), ((), ()))}


MM_VMEM_BUDGET = 44 * 1024 * 1024


def _mm_tiles(m, n, kk, n_div, k_div, out_bytes, has_addend, k_piece, transposed_a):
    best = None
    for bm in (1024, 1408, 512, 256):
        if m % bm:
            continue
        for bn in (1024, 1408, 512, 256, 128):
            if n % bn or any(d % bn for d in n_div):
                continue
            for bk in (kk, 5632, 4096, 2816, 2048, 1408, 1024, 512, 256, 128):
                if bk > kk or kk % bk or any(d % bk for d in k_div):
                    continue
                if k_piece and bk % k_piece:
                    continue
                nk = kk // bk
                vmem = 4 * (bm * bk + bk * bn) + 2 * bm * bn * out_bytes + (nk > 1) * 4 * bm * bn
                vmem += has_addend * 8 * bm * bn + transposed_a * 2 * bm * bk
                vmem += (out_bytes < 4) * 4 * bm * bn
                if vmem > MM_VMEM_BUDGET:
                    continue
                score = (max(nk, 2), -(bm * bn), -bk)
                if best is None or score < best[0]:
                    best = (score, bm, bn, bk)
    if best is None:
        raise ValueError(f"no tiles for {(m, n, kk)}")
    return best[1:]


def _mm(a, b, *, mode, name, out_dtype=F32, a_cs=False, b_cs=False, o_cs=None, addend=None, deps=()):
    if mode == "tn":
        kk, m = a.shape
    elif a_cs:
        m, kk = a.shape[1], a.shape[0] * a.shape[2]
    else:
        m, kk = a.shape
    if mode == "nt":
        n = b.shape[1] if b_cs else b.shape[0]
    else:
        n = b.shape[0] * b.shape[2] if b_cs else b.shape[1]
    a_c = a.shape[2] if a_cs else None
    b_c = b.shape[2] if b_cs else None
    o_c = n // o_cs if o_cs else None
    k_piece = b_c if (b_cs and mode == "nt") else None
    n_div = [c for c in (b_c if (b_cs and mode != "nt") else None, o_c) if c]
    k_div = [a_c] if a_c else []
    bm, bn, bk = _mm_tiles(m, n, kk, n_div, k_div, jnp.dtype(out_dtype).itemsize, addend is not None, k_piece,
                           mode == "tn")
    nk = kk // bk
    kb = bk // k_piece if k_piece else 1

    def cs_idx(blk, per):
        return (blk * per[0]) // per[1], (blk * per[0] % per[1]) // per[0]

    if mode == "tn":
        a_spec = pl.BlockSpec((bk, bm), lambda i, j, k: (k, i))
    elif a_cs:
        def a_map(i, j, k):
            s, r = cs_idx(k, (bk, a_c))
            return (s, i, r)
        a_spec = pl.BlockSpec((None, bm, bk), a_map)
    else:
        a_spec = pl.BlockSpec((bm, bk), lambda i, j, k: (i, k))
    if mode == "nt":
        if b_cs:
            b_spec = pl.BlockSpec((kb, bn, b_c), lambda i, j, k: (k, j, 0))
        else:
            b_spec = pl.BlockSpec((bn, bk), lambda i, j, k: (j, k))
    else:
        if b_cs:
            def b_map(i, j, k):
                s, r = cs_idx(j, (bn, b_c))
                return (s, k, r)
            b_spec = pl.BlockSpec((None, bk, bn), b_map)
        else:
            b_spec = pl.BlockSpec((bk, bn), lambda i, j, k: (k, j))
    if o_cs:
        def o_map(i, j, k):
            s, r = cs_idx(j, (bn, o_c))
            return (s, i, r)
        o_spec = pl.BlockSpec((None, bm, bn), o_map)
        out_shape = jax.ShapeDtypeStruct((o_cs, m, o_c), out_dtype)
    else:
        o_spec = pl.BlockSpec((bm, bn), lambda i, j, k: (i, j))
        out_shape = jax.ShapeDtypeStruct((m, n), out_dtype)
    in_specs = [a_spec, b_spec]
    args = [a, b]
    if addend is not None:
        in_specs.append(pl.BlockSpec((bm, bn), lambda i, j, k: (i, j)))
        args.append(addend)
    dims = _DIMS[mode]
    in_specs += [ANY] * len(deps)
    args += list(deps)
    n_in = len(args)

    def body(*refs):
        a_ref, b_ref = refs[0], refs[1]
        add_ref = refs[2] if addend is not None else None
        o_ref = refs[n_in]
        if k_piece:
            d = None
            for p in range(kb):
                dp = lax.dot_general(a_ref[:, p * b_c:(p + 1) * b_c].astype(BF16), b_ref[p].astype(BF16), dims,
                                     preferred_element_type=F32)
                d = dp if d is None else d + dp
        else:
            d = lax.dot_general(a_ref[...].astype(BF16), b_ref[...].astype(BF16), dims, preferred_element_type=F32)

        def finish(r):
            if addend is not None:
                r = r + add_ref[...]
            o_ref[...] = r.astype(o_ref.dtype)

        if nk == 1:
            finish(d)
            return
        acc_ref = refs[n_in + 1]
        k = pl.program_id(2)

        @pl.when(k == 0)
        def _():
            acc_ref[...] = d

        if nk > 2:
            @pl.when((k > 0) & (k < nk - 1))
            def _():
                acc_ref[...] += d

        @pl.when(k == nk - 1)
        def _():
            finish(acc_ref[...] + d)

    return pl.pallas_call(
        body, name=name, out_shape=out_shape, grid=(m // bm, n // bn, nk),
        in_specs=in_specs, out_specs=o_spec,
        scratch_shapes=[pltpu.VMEM((bm, bn), F32)] if nk > 1 else [],
        compiler_params=_cp("parallel", "parallel", "arbitrary"),
    )(*args)


LN_ROWS = 256


def _ln_stats(s):
    mu = jnp.mean(s, axis=-1, keepdims=True)
    sc = s - mu
    var = jnp.mean(sc * sc, axis=-1, keepdims=True)
    rstd = lax.rsqrt(var + LN_EPS)
    return sc * rstd, rstd


def _ln_fwd(h, sub, g, b, name):
    t, d = h.shape
    row = pl.BlockSpec((LN_ROWS, d), lambda i: (i, 0))
    vec = pl.BlockSpec((1, d), lambda i: (0, 0))

    def body(h_ref, s_ref, g_ref, b_ref, y_ref, yb_ref):
        xhat, _ = _ln_stats(ALPHA * h_ref[...] + s_ref[...])
        y = xhat * g_ref[...] + b_ref[...]
        y_ref[...] = y
        yb_ref[...] = y.astype(BF16)

    return pl.pallas_call(
        body, name=name, grid=(t // LN_ROWS,),
        out_shape=(jax.ShapeDtypeStruct((t, d), F32), jax.ShapeDtypeStruct((t, d), BF16)),
        in_specs=[row, row, vec, vec], out_specs=(row, row), compiler_params=_cp("parallel"),
    )(h, sub, g, b)


def _ln_bwd_math(xhat, rstd, dy, g):
    dxhat = dy * g
    m1 = jnp.mean(dxhat, axis=-1, keepdims=True)
    m2 = jnp.mean(dxhat * xhat, axis=-1, keepdims=True)
    ds = rstd * (dxhat - m1 - xhat * m2)
    dg = jnp.sum(dy * xhat, axis=0, keepdims=True)
    db = jnp.sum(dy, axis=0, keepdims=True)
    return ds, dg, db


def _ln_bwd(h, sub, dy_a, dy_b, g, name):
    t, d = h.shape
    row = pl.BlockSpec((LN_ROWS, d), lambda i: (i, 0))
    vec = pl.BlockSpec((1, d), lambda i: (0, 0))

    def body(h_ref, s_ref, da_ref, db_ref, g_ref, ds_ref, dres_ref, dg_ref, dbeta_ref):
        xhat, rstd = _ln_stats(ALPHA * h_ref[...] + s_ref[...])
        ds, dg, db = _ln_bwd_math(xhat, rstd, da_ref[...] + db_ref[...], g_ref[...])
        ds_ref[...] = ds.astype(BF16)
        dres_ref[...] = ALPHA * ds

        @pl.when(pl.program_id(0) == 0)
        def _():
            dg_ref[...] = jnp.zeros_like(dg_ref)
            dbeta_ref[...] = jnp.zeros_like(dbeta_ref)

        dg_ref[...] += dg
        dbeta_ref[...] += db

    return pl.pallas_call(
        body, name=name, grid=(t // LN_ROWS,),
        out_shape=(jax.ShapeDtypeStruct((t, d), BF16), jax.ShapeDtypeStruct((t, d), F32),
                   jax.ShapeDtypeStruct((1, d), F32), jax.ShapeDtypeStruct((1, d), F32)),
        in_specs=[row, row, row, row, vec], out_specs=(row, row, vec, vec),
        compiler_params=_cp("arbitrary"),
    )(h, sub, dy_a, dy_b, g)


def _ln_loss_bwd(h, sub, target, g, b, name):
    t, d = h.shape
    row = pl.BlockSpec((LN_ROWS, d), lambda i: (i, 0))
    vec = pl.BlockSpec((1, d), lambda i: (0, 0))
    lvec = pl.BlockSpec((1, 128), lambda i: (0, 0))

    def body(h_ref, s_ref, t_ref, g_ref, b_ref, loss_ref, ds_ref, dres_ref, dg_ref, dbeta_ref):
        xhat, rstd = _ln_stats(ALPHA * h_ref[...] + s_ref[...])
        y = xhat * g_ref[...] + b_ref[...]
        err = y - t_ref[...]
        part = 0.5 * jnp.sum(jnp.mean(err * err, axis=-1, keepdims=True), axis=0, keepdims=True)
        ds, dg, db = _ln_bwd_math(xhat, rstd, err * (1.0 / d), g_ref[...])
        ds_ref[...] = ds.astype(BF16)
        dres_ref[...] = ALPHA * ds

        @pl.when(pl.program_id(0) == 0)
        def _():
            loss_ref[...] = jnp.zeros_like(loss_ref)
            dg_ref[...] = jnp.zeros_like(dg_ref)
            dbeta_ref[...] = jnp.zeros_like(dbeta_ref)

        loss_ref[...] += jnp.broadcast_to(part, loss_ref.shape)
        dg_ref[...] += dg
        dbeta_ref[...] += db

    return pl.pallas_call(
        body, name=name, grid=(t // LN_ROWS,),
        out_shape=(jax.ShapeDtypeStruct((1, 128), F32), jax.ShapeDtypeStruct((t, d), BF16),
                   jax.ShapeDtypeStruct((t, d), F32), jax.ShapeDtypeStruct((1, d), F32),
                   jax.ShapeDtypeStruct((1, d), F32)),
        in_specs=[row, row, row, vec, vec], out_specs=(lvec, row, row, vec, vec),
        compiler_params=_cp("arbitrary"),
    )(h, sub, target, g, b)


HG_ROWS = 512
HG_LEVELS = 6


def _hg_constants():
    c = HG_CHUNK
    t = np.arange(c)[:, None]
    j = np.arange(c)[None, :]
    blocks, masks = [], [np.eye(c)]
    for lev in range(HG_LEVELS):
        m = 1 << lev
        second = (t % (2 * m)) >= m
        mid = (t // (2 * m)) * (2 * m) + m - 1
        blocks.append((second & (j > mid) & (j <= t)) | ((~second) & (j > t) & (j <= mid)))
        same = (t // (2 * m)) == (j // (2 * m))
        masks.append(same & second & ((j % (2 * m)) < m))
    blocks += [j <= t, j > t, np.ones((c, c), bool)]
    cm = np.concatenate(blocks, axis=0).astype(np.float32)
    mk = np.concatenate(masks, axis=0).astype(np.float32)
    return jnp.asarray(cm, dtype=BF16), jnp.asarray(mk)


def _split3(x):
    hi = x.astype(BF16)
    r = x - hi.astype(F32)
    mid = r.astype(BF16)
    lo = (r - mid.astype(F32)).astype(BF16)
    return hi, mid, lo


def _sum3(x):
    n = x.shape[-1] // 3
    return x[:, :n] + x[:, n:2 * n] + x[:, 2 * n:]


@jax.custom_vjp
def _prefix(cm, lf):
    return _sum3(lax.dot_general(cm, jnp.concatenate(_split3(lf), axis=1), _DIMS["nn"], preferred_element_type=F32))


def _prefix_fwd(cm, lf):
    return _prefix(cm, lf), cm


def _prefix_bwd(cm, d):
    d_lf = _sum3(lax.dot_general(cm, jnp.concatenate(_split3(d), axis=1), _DIMS["tn"], preferred_element_type=F32))
    return jnp.zeros_like(cm), d_lf


_prefix.defvjp(_prefix_fwd, _prefix_bwd)

_BWD = {"nn": (("nt", 0, 1), ("tn", 1, 0)), "nt": (("nn", 0, 1), ("tn", 0, 1)), "tn": (("nt", 1, 0), ("nn", 1, 0))}


def _bdot_raw(a, b, mode):
    return lax.dot_general(a.astype(BF16), b.astype(BF16), _DIMS[mode], preferred_element_type=F32)


@functools.partial(jax.custom_vjp, nondiff_argnums=(2,))
def _bdot(a, b, mode):
    return _bdot_raw(a, b, mode)


def _bdot_fwd(a, b, mode):
    return _bdot_raw(a, b, mode), (a, b)


def _bdot_bwd(mode, res, d):
    a, b = res
    (ma, da_pos, _), (mb, db_pos, _) = _BWD[mode]
    da = _bdot_raw(d, b, ma) if da_pos == 0 else _bdot_raw(b, d, ma)
    db = _bdot_raw(d, a, mb) if db_pos == 0 else _bdot_raw(a, d, mb)
    return da, db


_bdot.defvjp(_bdot_fwd, _bdot_bwd)


def _hg_chunk(pq, pf, pi, pg, l0, l1, l2, ng, s_t, cm, mk):
    c = HG_CHUNK
    mx = jnp.maximum(jnp.maximum(l0, l1), l2)
    e0, e1, e2 = jnp.exp(l0 - mx), jnp.exp(l1 - mx), jnp.exp(l2 - mx)
    lb = e0 / (e0 + e1 + e2)
    q = pq * jax.nn.sigmoid(pq)
    t1 = jnp.log(lb)
    t2 = jnp.log1p(-lb) + jax.nn.log_sigmoid(pf)
    lf = jnp.maximum(t1, t2) + jnp.log1p(jnp.exp(-jnp.abs(t1 - t2)))
    k = (1.0 - lb) * jax.nn.sigmoid(-pf)
    x = jnp.exp(_prefix(cm, lf))
    scores = mk[0:c] * _bdot(q, k, "nt")
    for lev in range(HG_LEVELS):
        xl = x[lev * c:(lev + 1) * c]
        scores = scores + mk[(lev + 1) * c:(lev + 2) * c] * _bdot(q * xl, k * xl, "nt")
    x_incl = x[HG_LEVELS * c:(HG_LEVELS + 1) * c]
    x_after = x[(HG_LEVELS + 1) * c:(HG_LEVELS + 2) * c]
    x_total = x[(HG_LEVELS + 2) * c:(HG_LEVELS + 3) * c]
    o = _bdot(scores, pi, "nn") + _bdot(q * x_incl, s_t, "nt")
    s_new = s_t * jnp.concatenate([x_total, x_total], axis=0) + _bdot(pi, k * x_after, "tn")
    rstd = lax.rsqrt(jnp.mean(o * o, axis=-1, keepdims=True) + RMS_EPS)
    y = o * rstd * ng * (pg * jax.nn.sigmoid(pg))
    return y, s_new


def _hg_specs(t, reverse):
    nrb = t // HG_ROWS
    rb = (lambda r: nrb - 1 - r) if reverse else (lambda r: r)
    proj = pl.BlockSpec((4, HG_ROWS, HG_DIM), lambda h, r: (0, rb(r), h))
    vec = pl.BlockSpec((1, HG_DIM), lambda h, r: (0, h))
    cm = pl.BlockSpec(((HG_LEVELS + 3) * HG_CHUNK, HG_CHUNK), lambda h, r: (0, 0))
    mk = pl.BlockSpec(((HG_LEVELS + 1) * HG_CHUNK, HG_CHUNK), lambda h, r: (0, 0))
    rows = pl.BlockSpec((HG_ROWS, HG_DIM), lambda h, r: (rb(r), h))
    states = pl.BlockSpec((None, HG_ROWS // HG_CHUNK, HG_DIM, HG_DIM), lambda h, r: (h, rb(r), 0, 0))
    return proj, vec, cm, mk, rows, states


def _hg_fwd(proj, l0, l1, l2, ng):
    t = proj.shape[1]
    n_in = HG_ROWS // HG_CHUNK
    cm, mk = _hg_constants()
    p_spec, vec, cm_spec, mk_spec, rows, st_spec = _hg_specs(t, False)

    def body(p_ref, l0_ref, l1_ref, l2_ref, ng_ref, cm_ref, mk_ref, y_ref, st_ref, s_ref):
        @pl.when(pl.program_id(1) == 0)
        def _():
            s_ref[...] = jnp.zeros_like(s_ref)

        def step(ci, carry):
            sl = pl.ds(pl.multiple_of(ci * HG_CHUNK, HG_CHUNK), HG_CHUNK)
            s_t = s_ref[...]
            st_ref[ci] = s_t
            y, s_new = _hg_chunk(p_ref[0, sl, :], p_ref[1, sl, :], p_ref[2, sl, :], p_ref[3, sl, :],
                                 l0_ref[...], l1_ref[...], l2_ref[...], ng_ref[...], s_t,
                                 cm_ref[...], mk_ref[...])
            y_ref[sl, :] = y.astype(BF16)
            s_ref[...] = s_new
            return carry

        lax.fori_loop(0, n_in, step, 0, unroll=8)

    return pl.pallas_call(
        body, name="hg_fwd", grid=(HG_HEADS, t // HG_ROWS),
        out_shape=(jax.ShapeDtypeStruct((t, D_MODEL), BF16),
                   jax.ShapeDtypeStruct((HG_HEADS, t // HG_CHUNK, HG_DIM, HG_DIM), F32)),
        in_specs=[p_spec, vec, vec, vec, vec, cm_spec, mk_spec], out_specs=(rows, st_spec),
        scratch_shapes=[pltpu.VMEM((HG_DIM, HG_DIM), F32)],
        compiler_params=_cp("parallel", "arbitrary"),
    )(proj, l0, l1, l2, ng, cm, mk)


def _hg_bwd(proj, states, dy, l0, l1, l2, ng):
    t = proj.shape[1]
    n_in = HG_ROWS // HG_CHUNK
    cm, mk = _hg_constants()
    p_spec, vec, cm_spec, mk_spec, rows, st_spec = _hg_specs(t, True)

    def body(p_ref, st_ref, dy_ref, l0_ref, l1_ref, l2_ref, ng_ref, cm_ref, mk_ref,
             dp_ref, dl0_ref, dl1_ref, dl2_ref, dng_ref, ds_ref):
        @pl.when(pl.program_id(1) == 0)
        def _():
            ds_ref[...] = jnp.zeros_like(ds_ref)
            for r in (dl0_ref, dl1_ref, dl2_ref, dng_ref):
                r[...] = jnp.zeros_like(r)

        def step(it, carry):
            ci = n_in - 1 - it
            sl = pl.ds(pl.multiple_of(ci * HG_CHUNK, HG_CHUNK), HG_CHUNK)
            fn = functools.partial(_hg_chunk, cm=cm_ref[...], mk=mk_ref[...])
            _, vjp = jax.vjp(fn, p_ref[0, sl, :], p_ref[1, sl, :], p_ref[2, sl, :], p_ref[3, sl, :],
                             l0_ref[...], l1_ref[...], l2_ref[...], ng_ref[...], st_ref[ci])
            dq, df, di, dg, d0, d1, d2, dn, ds = vjp((dy_ref[sl, :], ds_ref[...]))
            dp_ref[0, sl, :] = dq.astype(BF16)
            dp_ref[1, sl, :] = df.astype(BF16)
            dp_ref[2, sl, :] = di.astype(BF16)
            dp_ref[3, sl, :] = dg.astype(BF16)
            dl0_ref[...] += d0
            dl1_ref[...] += d1
            dl2_ref[...] += d2
            dng_ref[...] += dn
            ds_ref[...] = ds
            return carry

        lax.fori_loop(0, n_in, step, 0, unroll=8)

    v_shape = jax.ShapeDtypeStruct((1, D_MODEL), F32)
    return pl.pallas_call(
        body, name="hg_bwd", grid=(HG_HEADS, t // HG_ROWS),
        out_shape=(jax.ShapeDtypeStruct((4, t, D_MODEL), BF16), v_shape, v_shape, v_shape, v_shape),
        in_specs=[p_spec, st_spec, rows, vec, vec, vec, vec, cm_spec, mk_spec],
        out_specs=(p_spec, vec, vec, vec, vec),
        scratch_shapes=[pltpu.VMEM((HG_DIM, HG_DIM), F32)],
        compiler_params=_cp("parallel", "arbitrary"),
    )(proj, states, dy, l0, l1, l2, ng, cm, mk)


_SQRT_HALF = 0.7071067811865476
_INV_SQRT_2PI = 0.3989422804014327


def _gelu(x):
    return 0.5 * x * (1.0 + lax.erf(x * _SQRT_HALF))


def _gelu_grad(x):
    return 0.5 * (1.0 + lax.erf(x * _SQRT_HALF)) + x * (_INV_SQRT_2PI * jnp.exp(-0.5 * x * x))


def _tril(n):
    return (lax.broadcasted_iota(jnp.int32, (n, n), 0) >= lax.broadcasted_iota(jnp.int32, (n, n), 1)).astype(F32)


def _sg_specs(d):
    row = lambda w: pl.BlockSpec((SG_CHUNK, w), lambda i: (i, 0))
    vec = pl.BlockSpec((1, d), lambda i: (0, 0))
    cube = pl.BlockSpec((SG_GROUPS, SG_CHUNK, SG_CHUNK), lambda i: (0, 0, 0))
    return row, vec, cube


def _sg_fwd(pre, ln_g, ln_b, w_s, bias):
    t = pre.shape[0]
    d = D_MODEL
    row, vec, cube = _sg_specs(d)

    def body(pre_ref, g_ref, b_ref, w_ref, bias_ref, y_ref, vln_ref):
        u = _gelu(pre_ref[:, :d])
        vhat, _ = _ln_stats(_gelu(pre_ref[:, d:]))
        vln_ref[...] = vhat * g_ref[...] + b_ref[...]
        tril = _tril(SG_CHUNK)
        for g in range(SG_GROUPS):
            cs = slice(g * SG_DIM, (g + 1) * SG_DIM)
            gate = _bdot_raw(w_ref[g] * tril, vln_ref[:, cs], "nn") + bias_ref[g]
            y_ref[:, cs] = (u[:, cs] * gate).astype(BF16)

    return pl.pallas_call(
        body, name="sg_fwd", grid=(t // SG_CHUNK,), out_shape=jax.ShapeDtypeStruct((t, d), BF16),
        in_specs=[row(2 * d), vec, vec, cube, cube], out_specs=row(d),
        scratch_shapes=[pltpu.VMEM((SG_CHUNK, d), F32)], compiler_params=_cp("parallel"),
    )(pre, ln_g, ln_b, w_s, bias)


def _sg_bwd(pre, dy, ln_g, ln_b, w_s, bias):
    t = pre.shape[0]
    d = D_MODEL
    row, vec, cube = _sg_specs(d)
    dbs_spec = pl.BlockSpec((SG_GROUPS, 8, SG_CHUNK), lambda i: (0, 0, 0))

    def body(pre_ref, dy_ref, g_ref, b_ref, w_ref, bias_ref, dpre_ref, dw_ref, dbs_ref, dlg_ref, dlb_ref,
             vln_ref, dvln_ref):
        @pl.when(pl.program_id(0) == 0)
        def _():
            for r in (dw_ref, dbs_ref, dlg_ref, dlb_ref):
                r[...] = jnp.zeros_like(r)

        pu = pre_ref[:, :d]
        pv = pre_ref[:, d:]
        u = _gelu(pu)
        vhat, rstd = _ln_stats(_gelu(pv))
        vln_ref[...] = vhat * g_ref[...] + b_ref[...]
        tril = _tril(SG_CHUNK)
        ones = jnp.ones((8, SG_DIM), F32)
        for g in range(SG_GROUPS):
            cs = slice(g * SG_DIM, (g + 1) * SG_DIM)
            wc = w_ref[g] * tril
            vg = vln_ref[:, cs]
            gate = _bdot_raw(wc, vg, "nn") + bias_ref[g]
            dyg = dy_ref[:, cs]
            dgate = dyg * u[:, cs]
            dpre_ref[:, cs] = (dyg * gate * _gelu_grad(pu[:, cs])).astype(BF16)
            dw_ref[g] += tril * _bdot_raw(dgate, vg, "nt")
            dbs_ref[g] += lax.dot_general(ones, dgate, _DIMS["nt"], precision=HI, preferred_element_type=F32)
            dvln_ref[:, cs] = _bdot_raw(wc, dgate, "tn")
        dvln = dvln_ref[...]
        dv, dg, db = _ln_bwd_math(vhat, rstd, dvln, g_ref[...])
        dlg_ref[...] += dg
        dlb_ref[...] += db
        dpre_ref[:, d:] = (dv * _gelu_grad(pv)).astype(BF16)

    return pl.pallas_call(
        body, name="sg_bwd", grid=(t // SG_CHUNK,),
        out_shape=(jax.ShapeDtypeStruct((t, 2 * d), BF16), jax.ShapeDtypeStruct(w_s.shape, F32),
                   jax.ShapeDtypeStruct((SG_GROUPS, 8, SG_CHUNK), F32),
                   jax.ShapeDtypeStruct((1, d), F32), jax.ShapeDtypeStruct((1, d), F32)),
        in_specs=[row(2 * d), row(d), vec, vec, cube, cube],
        out_specs=(row(2 * d), cube, dbs_spec, vec, vec),
        scratch_shapes=[pltpu.VMEM((SG_CHUNK, d), F32), pltpu.VMEM((SG_CHUNK, d), F32)],
        compiler_params=_cp("arbitrary"),
    )(pre, dy, ln_g, ln_b, w_s, bias)


FFN_ROWS = 256
FFN_COLS = 1408
HALO = 8


def _ffn_conv(a_ext, w_ref, cb_ref):
    a1 = pltpu.roll(a_ext, 1, 0)
    a2 = pltpu.roll(a_ext, 2, 0)
    return w_ref[0:1, :] * a2 + w_ref[1:2, :] * a1 + w_ref[2:3, :] * a_ext + cb_ref[...], a1, a2


def _ffn_fwd(hh, conv_w, conv_b):
    t = hh.shape[1]
    nb8 = FFN_ROWS // HALO
    main = pl.BlockSpec((2, FFN_ROWS, FFN_COLS), lambda c, r: (0, r, c))
    prev = pl.BlockSpec((None, HALO, FFN_COLS), lambda c, r: (0, jnp.maximum(r * nb8 - 1, 0), c))
    wspec = pl.BlockSpec((3, FFN_COLS), lambda c, r: (0, c))
    bspec = pl.BlockSpec((1, FFN_COLS), lambda c, r: (0, c))

    def body(m_ref, p_ref, w_ref, cb_ref, z_ref):
        prev = jnp.where(pl.program_id(1) == 0, 0.0, p_ref[...])
        a_ext = jnp.concatenate([prev, m_ref[0]], axis=0)
        a, _, _ = _ffn_conv(a_ext, w_ref, cb_ref)
        a = a[HALO:]
        z_ref[...] = (a * jax.nn.sigmoid(a) * m_ref[1]).astype(BF16)

    return pl.pallas_call(
        body, name="ffn_fwd", grid=(D_FF // FFN_COLS, t // FFN_ROWS),
        out_shape=jax.ShapeDtypeStruct((t, D_FF), BF16),
        in_specs=[main, prev, wspec, bspec],
        out_specs=pl.BlockSpec((FFN_ROWS, FFN_COLS), lambda c, r: (r, c)),
        compiler_params=_cp("parallel", "parallel"),
    )(hh, hh, conv_w, conv_b)


def _ffn_bwd(hh, dz, conv_w, conv_b):
    t = hh.shape[1]
    nb8 = FFN_ROWS // HALO
    last8 = t // HALO - 1
    nr = t // FFN_ROWS
    main = pl.BlockSpec((2, FFN_ROWS, FFN_COLS), lambda c, r: (0, r, c))
    prev = pl.BlockSpec((None, HALO, FFN_COLS), lambda c, r: (0, jnp.maximum(r * nb8 - 1, 0), c))
    nxt = pl.BlockSpec((2, HALO, FFN_COLS), lambda c, r: (0, jnp.minimum((r + 1) * nb8, last8), c))
    dmain = pl.BlockSpec((FFN_ROWS, FFN_COLS), lambda c, r: (r, c))
    dnxt = pl.BlockSpec((HALO, FFN_COLS), lambda c, r: (jnp.minimum((r + 1) * nb8, last8), c))
    wspec = pl.BlockSpec((3, FFN_COLS), lambda c, r: (0, c))
    bspec = pl.BlockSpec((1, FFN_COLS), lambda c, r: (0, c))

    def body(m_ref, p_ref, n_ref, dz_ref, dzn_ref, w_ref, cb_ref, dh_ref, dw_ref, dcb_ref):
        r = pl.program_id(1)

        @pl.when(r == 0)
        def _():
            dw_ref[...] = jnp.zeros_like(dw_ref)
            dcb_ref[...] = jnp.zeros_like(dcb_ref)

        prev = jnp.where(r == 0, 0.0, p_ref[...])
        a_ext = jnp.concatenate([prev, m_ref[0], n_ref[0]], axis=0)
        a, a1, a2 = _ffn_conv(a_ext, w_ref, cb_ref)
        a = a[HALO:]
        b_ext = jnp.concatenate([m_ref[1], n_ref[1]], axis=0)
        dz_ext = jnp.concatenate([dz_ref[...], jnp.where(r == nr - 1, 0.0, dzn_ref[...])], axis=0)
        sig = jax.nn.sigmoid(a)
        da = dz_ext * b_ext * (sig * (1.0 + a * (1.0 - sig)))
        n_ext = FFN_ROWS + HALO
        da_p1 = pltpu.roll(da, n_ext - 1, 0)
        da_p2 = pltpu.roll(da, n_ext - 2, 0)
        da_raw = w_ref[2:3, :] * da + w_ref[1:2, :] * da_p1 + w_ref[0:1, :] * da_p2
        dh_ref[0] = da_raw[:FFN_ROWS].astype(BF16)
        dh_ref[1] = (dz_ref[...] * (a * sig)[:FFN_ROWS]).astype(BF16)
        dam = da[:FFN_ROWS]
        rows = slice(HALO, HALO + FFN_ROWS)
        dw_ref[0:1, :] += jnp.sum(dam * a2[rows], axis=0, keepdims=True)
        dw_ref[1:2, :] += jnp.sum(dam * a1[rows], axis=0, keepdims=True)
        dw_ref[2:3, :] += jnp.sum(dam * a_ext[rows], axis=0, keepdims=True)
        dcb_ref[...] += jnp.sum(dam, axis=0, keepdims=True)

    return pl.pallas_call(
        body, name="ffn_bwd", grid=(D_FF // FFN_COLS, nr),
        out_shape=(jax.ShapeDtypeStruct((2, t, D_FF), BF16), jax.ShapeDtypeStruct((3, D_FF), F32),
                   jax.ShapeDtypeStruct((1, D_FF), F32)),
        in_specs=[main, prev, nxt, dmain, dnxt, wspec, bspec],
        out_specs=(main, wspec, bspec), compiler_params=_cp("parallel", "arbitrary"),
    )(hh, hh, hh, dz, dz, conv_w, conv_b)


def _adamw_math(w, g, m, v):
    m = ADAM_B1 * m + (1.0 - ADAM_B1) * g
    v = ADAM_B2 * v + (1.0 - ADAM_B2) * (g * g)
    m_hat = m / (1.0 - ADAM_B1 ** ADAM_STEP)
    v_hat = v / (1.0 - ADAM_B2 ** ADAM_STEP)
    delta = -ADAM_LR * (m_hat / (jnp.sqrt(v_hat) + ADAM_EPS) + ADAM_WD * w)
    return delta, m, v


def _as2d(shape):
    n = int(np.prod(shape))
    c = shape[-1] if shape[-1] % 128 == 0 else (128 if n % 128 == 0 else shape[-1])
    return n // c, c


def _adamw(w, g, m, v, name, parts=None, own=None, is_me=None):
    shape = w.shape
    r, c = _as2d(shape)
    br = r if r <= 512 else _pick(r, (256, 176, 128, 64, 8))
    blk = pl.BlockSpec((br, c), lambda i: (i, 0))
    w2, m2, v2 = (a.reshape(r, c) for a in (w, m, v))
    extra, extra_specs = [], []
    if parts is not None:
        g_in = parts.reshape(N_DEV, r, c)
        g_spec = pl.BlockSpec((N_DEV, br, c), lambda i: (0, i, 0))
        extra = [own.reshape(r, c), is_me]
        extra_specs = [blk, pl.BlockSpec(memory_space=pltpu.SMEM)]
    else:
        g_in = g.reshape(r, c)
        g_spec = blk

    def body(w_ref, g_ref, m_ref, v_ref, *rest):
        go_ref, d_ref, mo_ref, vo_ref = rest[-4:]
        if parts is not None:
            own_ref, me_ref = rest[:2]
            gsum = None
            for i in range(N_DEV):
                term = jnp.where(me_ref[i] > 0.5, own_ref[...], g_ref[i]).astype(F32)
                gsum = term if gsum is None else gsum + term
        else:
            gsum = g_ref[...]
        delta, m_new, v_new = _adamw_math(w_ref[...], gsum, m_ref[...], v_ref[...])
        go_ref[...] = gsum
        d_ref[...] = delta
        mo_ref[...] = m_new
        vo_ref[...] = v_new

    o = jax.ShapeDtypeStruct((r, c), F32)
    outs = pl.pallas_call(
        body, name=name, grid=(r // br,), out_shape=(o, o, o, o),
        in_specs=[blk, g_spec, blk, blk] + extra_specs, out_specs=(blk, blk, blk, blk),
        compiler_params=_cp("parallel"),
    )(w2, g_in, m2, v2, *extra)
    return tuple(a.reshape(shape) for a in outs)


def _me_and_peers():
    x, y, c = lax.axis_index("x"), lax.axis_index("y"), lax.axis_index("c")
    me = 4 * x + 2 * y + c
    peers = []
    for k in range(1, N_DEV):
        kx, ky, kc = (k >> 2) & 1, (k >> 1) & 1, k & 1
        px, py, pc = x ^ kx, y ^ ky, c ^ kc
        peers.append(((px, py, pc), 4 * px + 2 * py + pc))
    return me, peers


def _all_gather(shard, name):
    def body(x_ref, o_ref, send_sems, recv_sems, local_sem):
        me, peers = _me_and_peers()
        mine = pltpu.make_async_copy(x_ref, o_ref.at[me], local_sem)
        mine.start()
        sends = []
        for k, (dev, _) in enumerate(peers):
            cp = pltpu.make_async_remote_copy(src_ref=x_ref, dst_ref=o_ref.at[me], send_sem=send_sems.at[k],
                                              recv_sem=recv_sems.at[k], device_id=dev, device_id_type=MESH)
            cp.start()
            sends.append(cp)
        for k, (dev, idx) in enumerate(peers):
            pltpu.make_async_remote_copy(src_ref=x_ref, dst_ref=o_ref.at[idx], send_sem=send_sems.at[k],
                                         recv_sem=recv_sems.at[k], device_id=dev, device_id_type=MESH).wait_recv()
        for cp in sends:
            cp.wait_send()
        mine.wait()

    return pl.pallas_call(
        body, name=name, out_shape=jax.ShapeDtypeStruct((N_DEV,) + shard.shape, shard.dtype),
        in_specs=[ANY], out_specs=ANY,
        scratch_shapes=[pltpu.SemaphoreType.DMA((N_DEV - 1,)), pltpu.SemaphoreType.DMA((N_DEV - 1,)),
                        pltpu.SemaphoreType.DMA],
        compiler_params=pltpu.CompilerParams(has_side_effects=True),
    )(shard)


HBM = pl.BlockSpec(memory_space=pltpu.HBM)
SEM = pl.BlockSpec(memory_space=pltpu.SEMAPHORE)
EFFECT = pltpu.SideEffectType.DATAFLOW_SIDE_EFFECTING


def _xchg_start(parts, land, *, dep, name):
    n_buf = 1 if parts is None else 2

    def body(*refs):
        land_ref = refs[n_buf - 1]
        send_sem, recv_sem = refs[n_buf + 1], refs[n_buf + 2]
        token = refs[2 * n_buf + 3]
        me, peers = _me_and_peers()
        for dev, idx in peers:
            src = land_ref.at[me] if parts is None else refs[0].at[idx]
            pltpu.make_async_remote_copy(src_ref=src, dst_ref=land_ref.at[me], send_sem=send_sem, recv_sem=recv_sem,
                                         device_id=dev, device_id_type=MESH).start()
        token[...] = jnp.zeros_like(token)

    bufs = [land] if parts is None else [parts, land]
    return pl.pallas_call(
        body, name=name,
        out_shape=(pltpu.SemaphoreType.DMA(()), pltpu.SemaphoreType.DMA(()),
                   *[pltpu.HBM(b.shape, b.dtype) for b in bufs], jax.ShapeDtypeStruct((8, 128), F32)),
        in_specs=(*[HBM] * n_buf, ANY), out_specs=(SEM, SEM, *[HBM] * n_buf, pl.BlockSpec(memory_space=pltpu.VMEM)),
        input_output_aliases={i: 2 + i for i in range(n_buf)},
        compiler_params=pltpu.CompilerParams(has_side_effects=EFFECT),
    )(*[pltpu.with_memory_space_constraint(b, pltpu.HBM) for b in bufs], dep)


def _xchg_wait(handles, after, name):
    send_sem, recv_sem, *bufs, _ = handles
    n_buf = len(bufs)

    def body(*refs):
        land_ref = refs[n_buf - 1]
        send_sem, recv_sem = refs[n_buf], refs[n_buf + 1]
        x, y, c = lax.axis_index("x"), lax.axis_index("y"), lax.axis_index("c")
        seven = land_ref.at[pl.ds(0, N_DEV - 1)]
        cp = pltpu.make_async_remote_copy(src_ref=seven, dst_ref=seven, send_sem=send_sem, recv_sem=recv_sem,
                                          device_id=(x, y, c), device_id_type=MESH)
        cp.wait_send()
        cp.wait_recv()

    return pl.pallas_call(
        body, name=name, out_shape=tuple(pltpu.HBM(b.shape, b.dtype) for b in bufs),
        in_specs=(*[HBM] * n_buf, SEM, SEM, ANY), out_specs=tuple([HBM] * n_buf),
        input_output_aliases={i: i for i in range(n_buf)},
        compiler_params=pltpu.CompilerParams(has_side_effects=EFFECT),
    )(*bufs, send_sem, recv_sem, after)


def _sum_devices(v, name):
    _, r, c = v.shape

    def body(x_ref, o_ref):
        acc = x_ref[0]
        for i in range(1, N_DEV):
            acc = acc + x_ref[i]
        o_ref[...] = acc

    return pl.pallas_call(
        body, name=name, out_shape=jax.ShapeDtypeStruct((r, c), F32),
        in_specs=[pl.BlockSpec(memory_space=pltpu.VMEM)], out_specs=pl.BlockSpec(memory_space=pltpu.VMEM),
        compiler_params=pltpu.CompilerParams(vmem_limit_bytes=VMEM_LIMIT),
    )(v)


def _local_step(x, target, lb_logits, p, weight, send_grad, send_small):
    g = {}
    l0, l1, l2 = lb_logits[0:1], lb_logits[1:2], lb_logits[2:3]
    x_bf = x.astype(BF16)

    proj0 = _mm(x_bf, weight("hg_w_in", x_bf), mode="nn", b_cs=True, o_cs=4, name="mm_hg_in")
    y0, states = _hg_fwd(proj0, l0, l1, l2, p["hg_norm_g"])
    mixed0 = _mm(y0, weight("hg_w_out", y0), mode="nn", name="mm_hg_out")
    hh_l, z_l = [], []

    def ffn_forward(layer, h_in_bf):
        hh = _mm(h_in_bf, weight(f"ffn_w_up{layer}", h_in_bf), mode="nn", b_cs=True, o_cs=2, name=f"mm_up{layer}")
        z = _ffn_fwd(hh, p["ffn_conv_w"][layer], p["ffn_conv_b"][layer])
        ffn = _mm(z, weight(f"ffn_w_down{layer}", z), mode="nn", name=f"mm_down{layer}")
        hh_l.append(hh)
        z_l.append(z)
        return ffn

    h1, h1b = _ln_fwd(x, mixed0, p["ln1_g"][0], p["ln1_b"][0], "ln1_0")
    ffn0 = ffn_forward(0, h1b)
    h2, h2b = _ln_fwd(h1, ffn0, p["ln2_g"][0], p["ln2_b"][0], "ln2_0")
    pre1 = _mm(h2b, weight("sg_w_in", h2b), mode="nn", b_cs=True, name="mm_sg_in")
    y1 = _sg_fwd(pre1, p["sg_ln_g"], p["sg_ln_b"], p["sg_w_s"], p["sg_bias"])
    mixed1 = _mm(y1, weight("sg_w_out", y1), mode="nn", name="mm_sg_out")
    h3, h3b = _ln_fwd(h2, mixed1, p["ln1_g"][1], p["ln1_b"][1], "ln1_1")
    ffn1 = ffn_forward(1, h3b)

    loss, ds, dres, g["ln2_g1"], g["ln2_b1"] = _ln_loss_bwd(h3, ffn1, target, p["ln2_g"][1], p["ln2_b"][1], "ln2_1_loss")

    def ffn_backward(layer, ds_bf, h_in_bf):
        tok = send_grad(f"ffn_w_down{layer}",
                        _mm(z_l[layer], ds_bf, mode="tn", out_dtype=BF16, name=f"mm_dw_down{layer}"))
        dz = _mm(ds_bf, weight(f"ffn_w_down{layer}", None), mode="nt", name=f"mm_d_z{layer}", deps=(tok,))
        dhh, g[f"conv_w{layer}"], g[f"conv_b{layer}"] = _ffn_bwd(hh_l[layer], dz, p["ffn_conv_w"][layer],
                                                                   p["ffn_conv_b"][layer])
        tok = send_grad(f"ffn_w_up{layer}", _mm(h_in_bf, dhh, mode="tn", b_cs=True, o_cs=N_DEV, out_dtype=BF16,
                                                name=f"mm_dw_up{layer}"))
        return _mm(dhh, weight(f"ffn_w_up{layer}", None), mode="nt", a_cs=True, b_cs=True, name=f"mm_d_up{layer}",
                   deps=(tok,))

    dh3 = ffn_backward(1, ds, h3b)
    ds, dres, g["ln1_g1"], g["ln1_b1"] = _ln_bwd(h2, mixed1, dres, dh3, p["ln1_g"][1], "ln1_1_bwd")
    tok = send_grad("sg_w_out", _mm(y1, ds, mode="tn", out_dtype=BF16, name="mm_dw_sg_out"))
    dy1 = _mm(ds, weight("sg_w_out", None), mode="nt", name="mm_d_sg_out", deps=(tok,))
    dpre1, g["sg_w_s"], dbs, g["sg_ln_g"], g["sg_ln_b"] = _sg_bwd(pre1, dy1, p["sg_ln_g"], p["sg_ln_b"],
                                                                    p["sg_w_s"], p["sg_bias"])
    g["sg_b_s"] = dbs[:, 0, :]
    tok = send_grad("sg_w_in", _mm(h2b, dpre1, mode="tn", o_cs=N_DEV, out_dtype=BF16, name="mm_dw_sg_in"))
    dh2 = _mm(dpre1, weight("sg_w_in", None), mode="nt", b_cs=True, name="mm_d_sg_in", deps=(tok,))
    ds, dres, g["ln2_g0"], g["ln2_b0"] = _ln_bwd(h1, ffn0, dres, dh2, p["ln2_g"][0], "ln2_0_bwd")
    dh1 = ffn_backward(0, ds, h1b)
    ds, dres, g["ln1_g0"], g["ln1_b0"] = _ln_bwd(x, mixed0, dres, dh1, p["ln1_g"][0], "ln1_0_bwd")
    tok = send_grad("hg_w_out", _mm(y0, ds, mode="tn", out_dtype=BF16, name="mm_dw_hg_out"))
    dy0 = _mm(ds, weight("hg_w_out", None), mode="nt", name="mm_d_hg_out", deps=(tok,))
    dproj, d0, d1, d2, g["hg_norm_g"] = _hg_bwd(proj0, states, dy0, l0, l1, l2, p["hg_norm_g"])
    g["lb_logits"] = jnp.concatenate([d0, d1, d2], axis=0)
    tok = send_small(g)
    tok = send_grad("hg_w_in", _mm(x_bf, dproj, mode="tn", b_cs=True, o_cs=N_DEV, out_dtype=BF16, name="mm_dw_hg_in",
                                   deps=(tok,)))
    grad_x = _mm(dproj, weight("hg_w_in", None), mode="nt", a_cs=True, b_cs=True, addend=dres, name="mm_d_hg_in",
                 deps=(tok,))
    return loss[0, 0], grad_x


_SMALL = ("lb_logits", "hg_norm_g", "sg_ln_g", "sg_ln_b", "sg_w_s", "sg_b_s", "ffn_conv_w", "ffn_conv_b",
          "ln1_g", "ln1_b", "ln2_g", "ln2_b")
_NAMES = ("lb_logits", "hg_w_in", "hg_norm_g", "hg_w_out", "sg_w_in", "sg_ln_g", "sg_ln_b", "sg_w_s", "sg_b_s",
          "sg_w_out", "ffn_w_up", "ffn_conv_w", "ffn_conv_b", "ffn_w_down", "ln1_g", "ln1_b", "ln2_g", "ln2_b")


def kernel(x, lb_logits, hg_w_in, hg_norm_g, hg_w_out, sg_w_in, sg_ln_g, sg_ln_b, sg_w_s, sg_b_s, sg_w_out, ffn_w_up, ffn_conv_w, ffn_conv_b, ffn_w_down, ln1_g, ln1_b, ln2_g, ln2_b, loss_target, m_lb_logits, m_hg_w_in, m_hg_norm_g, m_hg_w_out, m_sg_w_in, m_sg_ln_g, m_sg_ln_b, m_sg_w_s, m_sg_b_s, m_sg_w_out, m_ffn_w_up, m_ffn_conv_w, m_ffn_conv_b, m_ffn_w_down, m_ln1_g, m_ln1_b, m_ln2_g, m_ln2_b, v_lb_logits, v_hg_w_in, v_hg_norm_g, v_hg_w_out, v_sg_w_in, v_sg_ln_g, v_sg_ln_b, v_sg_w_s, v_sg_b_s, v_sg_w_out, v_ffn_w_up, v_ffn_conv_w, v_ffn_conv_b, v_ffn_w_down, v_ln1_g, v_ln1_b, v_ln2_g, v_ln2_b):
    w = dict(lb_logits=lb_logits, hg_w_in=hg_w_in, hg_norm_g=hg_norm_g, hg_w_out=hg_w_out, sg_w_in=sg_w_in,
             sg_ln_g=sg_ln_g, sg_ln_b=sg_ln_b, sg_w_s=sg_w_s, sg_b_s=sg_b_s, sg_w_out=sg_w_out, ffn_w_up=ffn_w_up,
             ffn_conv_w=ffn_conv_w, ffn_conv_b=ffn_conv_b, ffn_w_down=ffn_w_down, ln1_g=ln1_g, ln1_b=ln1_b,
             ln2_g=ln2_g, ln2_b=ln2_b)
    m = dict(lb_logits=m_lb_logits, hg_w_in=m_hg_w_in, hg_norm_g=m_hg_norm_g, hg_w_out=m_hg_w_out,
             sg_w_in=m_sg_w_in, sg_ln_g=m_sg_ln_g, sg_ln_b=m_sg_ln_b, sg_w_s=m_sg_w_s, sg_b_s=m_sg_b_s,
             sg_w_out=m_sg_w_out, ffn_w_up=m_ffn_w_up, ffn_conv_w=m_ffn_conv_w, ffn_conv_b=m_ffn_conv_b,
             ffn_w_down=m_ffn_w_down, ln1_g=m_ln1_g, ln1_b=m_ln1_b, ln2_g=m_ln2_g, ln2_b=m_ln2_b)
    v = dict(lb_logits=v_lb_logits, hg_w_in=v_hg_w_in, hg_norm_g=v_hg_norm_g, hg_w_out=v_hg_w_out,
             sg_w_in=v_sg_w_in, sg_ln_g=v_sg_ln_g, sg_ln_b=v_sg_ln_b, sg_w_s=v_sg_w_s, sg_b_s=v_sg_b_s,
             sg_w_out=v_sg_w_out, ffn_w_up=v_ffn_w_up, ffn_conv_w=v_ffn_conv_w, ffn_conv_b=v_ffn_conv_b,
             ffn_w_down=v_ffn_w_down, ln1_g=v_ln1_g, ln1_b=v_ln1_b, ln2_g=v_ln2_g, ln2_b=v_ln2_b)
    me = 4 * lax.axis_index("x") + 2 * lax.axis_index("y") + lax.axis_index("c")
    is_me = (jnp.arange(N_DEV) == me).astype(F32)
    d = D_MODEL

    shards = [("hg_w_in", hg_w_in[0]), ("hg_w_out", hg_w_out[0]), ("ffn_w_up0", ffn_w_up[0]),
              ("ffn_w_down0", ffn_w_down[0]), ("sg_w_in", sg_w_in[0]), ("sg_w_out", sg_w_out[0]),
              ("ffn_w_up1", ffn_w_up[1]), ("ffn_w_down1", ffn_w_down[1])]
    row_sharded = {"hg_w_out": (d, d), "sg_w_out": (d, d), "ffn_w_down0": (D_FF, d), "ffn_w_down1": (D_FF, d)}
    sv = jnp.zeros((8, 768), F32)
    sv = sv.at[0, :256].set(sg_ln_g[0]).at[1, :256].set(sg_ln_b[0]).at[2:8, :704].set(ffn_conv_w.reshape(6, 704))
    sv = _all_gather(sv, "ag_small")
    gathers, gathered, dep = {}, {}, sv
    for n, shard in shards:
        land = lax.dynamic_update_index_in_dim(lax.empty((N_DEV,) + shard.shape, BF16), shard.astype(BF16), me, 0)
        gathers[n] = _xchg_start(None, land, dep=dep, name=f"ag_{n}")
        dep = gathers[n][-1]

    last_start = dep

    def weight(n, after):
        if n not in gathered:
            if n == shards[0][0]:
                after = last_start
            (full,) = _xchg_wait(gathers[n], after, f"agw_{n}")
            gathered[n] = full.reshape(row_sharded[n]) if n in row_sharded else full
        return gathered[n]

    grad_sends = {}

    def send_grad(n, parts):
        shape = (N_DEV,) + dict(shards)[n].shape
        grad_sends[n] = _xchg_start(parts.reshape(shape), lax.empty(shape, parts.dtype), dep=is_me, name=f"rs_{n}")
        return grad_sends[n][-1]

    small_send = {}

    def send_small(g):
        small = [g["lb_logits"], g["hg_norm_g"], g["sg_ln_g"], g["sg_ln_b"], g["sg_w_s"], g["sg_b_s"],
                 g["conv_w0"], g["conv_w1"], g["conv_b0"], g["conv_b1"], g["ln1_g0"], g["ln1_g1"],
                 g["ln1_b0"], g["ln1_b1"], g["ln2_g0"], g["ln2_g1"], g["ln2_b0"], g["ln2_b1"]]
        flat = jnp.concatenate([a.reshape(-1) for a in small])
        rows = -(-flat.shape[0] // (8 * 128)) * 8
        flat = jnp.pad(flat, (0, rows * 128 - flat.shape[0])).reshape(rows, 128)
        land = lax.dynamic_update_index_in_dim(lax.empty((N_DEV, rows, 128), F32), flat, me, 0)
        small_send["shapes"] = [a.shape for a in small]
        small_send["handles"] = _xchg_start(None, land, dep=is_me, name="ar_small")
        return small_send["handles"][-1]

    p = {}
    p["sg_ln_g"] = sv[:, 0, :256].reshape(1, d)
    p["sg_ln_b"] = sv[:, 1, :256].reshape(1, d)
    conv_w_full = jnp.transpose(sv[:, 2:8, :704].reshape(N_DEV, DEPTH, 3, 704), (1, 2, 0, 3)).reshape(DEPTH, 3, D_FF)
    p["ffn_conv_w"] = [conv_w_full[l] for l in range(DEPTH)]
    p["ffn_conv_b"] = [ffn_conv_b[l:l + 1] for l in range(DEPTH)]
    p["hg_norm_g"] = hg_norm_g
    p["sg_w_s"] = sg_w_s[0]
    p["sg_bias"] = jnp.broadcast_to(sg_b_s[0][:, :, None], (SG_GROUPS, SG_CHUNK, SG_DIM))
    for n in ("ln1_g", "ln1_b", "ln2_g", "ln2_b"):
        p[n] = [w[n][l:l + 1] for l in range(DEPTH)]

    loss_part, grad_x = _local_step(x[0], loss_target[0], lb_logits, p, weight, send_grad, send_small)
    loss = lax.psum(loss_part, ("x", "y", "c"))

    (landed,) = _xchg_wait(small_send["handles"], grad_x, "arw_small")
    red = _sum_devices(landed, "sum_small").reshape(-1)
    shapes = small_send["shapes"]
    offs = np.cumsum([0] + [int(np.prod(sh)) for sh in shapes])
    r = [red[offs[i]:offs[i + 1]].reshape(shapes[i]) for i in range(len(shapes))]
    gs = {}
    gs["lb_logits"] = r[0]
    gs["hg_norm_g"] = r[1]
    gs["sg_ln_g"] = lax.dynamic_slice(r[2], (0, me * 256), (1, 256))
    gs["sg_ln_b"] = lax.dynamic_slice(r[3], (0, me * 256), (1, 256))
    gs["sg_w_s"] = r[4][None]
    gs["sg_b_s"] = r[5][None]
    gs["ffn_conv_w"] = lax.dynamic_slice(jnp.stack([r[6], r[7]]), (0, 0, me * 704), (DEPTH, 3, 704))
    gs["ffn_conv_b"] = jnp.concatenate([r[8], r[9]], axis=0)
    gs["ln1_g"] = jnp.concatenate([r[10], r[11]], axis=0)
    gs["ln1_b"] = jnp.concatenate([r[12], r[13]], axis=0)
    gs["ln2_g"] = jnp.concatenate([r[14], r[15]], axis=0)
    gs["ln2_b"] = jnp.concatenate([r[16], r[17]], axis=0)

    out_g, out_d, out_m, out_v = {}, {}, {}, {}
    for n in _SMALL:
        out_g[n], out_d[n], out_m[n], out_v[n] = _adamw(w[n], gs[n], m[n], v[n], f"adamw_{n}")

    res, after = {}, out_v["ln2_b"]
    for n in ("ffn_w_down1", "ffn_w_up1", "sg_w_out", "sg_w_in", "ffn_w_down0", "ffn_w_up0", "hg_w_out", "hg_w_in"):
        parts, recv = _xchg_wait(grad_sends[n], after, f"rsw_{n}")
        own = lax.dynamic_index_in_dim(parts, me, 0, keepdims=False)
        base, layer = (n[:-1], int(n[-1])) if n[-1] in "01" else (n, 0)
        res[n] = _adamw(w[base][layer], None, m[base][layer], v[base][layer], f"adamw_{n}", parts=recv, own=own,
                        is_me=is_me)
        after = res[n][3]
    for n in ("hg_w_in", "hg_w_out", "sg_w_in", "sg_w_out"):
        out_g[n], out_d[n], out_m[n], out_v[n] = (a[None] for a in res[n])
    for n in ("ffn_w_up", "ffn_w_down"):
        out_g[n], out_d[n], out_m[n], out_v[n] = (jnp.stack([res[n + "0"][i], res[n + "1"][i]]) for i in range(4))

    return (loss, grad_x[None], *[out_g[n] for n in _NAMES], *[out_d[n] for n in _NAMES],
            *[out_m[n] for n in _NAMES], *[out_v[n] for n in _NAMES])
```

```python
import functools

import numpy as np
import jax
import jax.numpy as jnp
from jax import lax
from jax.experimental import pallas as pl
from jax.experimental.pallas import tpu as pltpu

F32 = jnp.float32
BF16 = jnp.bfloat16
HI = lax.Precision.HIGHEST

N_DEV = 8
D_MODEL = 2048
HG_HEADS = 16
HG_DIM = 128
HG_CHUNK = 64
SG_GROUPS = 16
SG_DIM = 128
SG_CHUNK = 128
D_FF = 5632
DEPTH = 2
ALPHA = (2 * DEPTH) ** 0.25
LN_EPS = 1e-5
RMS_EPS = 1e-6
ADAM_LR = 0.001
ADAM_B1 = 0.9
ADAM_B2 = 0.999
ADAM_EPS = 1e-08
ADAM_WD = 0.01
ADAM_STEP = 10

VMEM_LIMIT = 56 * 1024 * 1024
MESH = pl.DeviceIdType.MESH
ANY = pl.BlockSpec(memory_space=pl.ANY)


def _cp(*sem):
    return pltpu.CompilerParams(dimension_semantics=sem, vmem_limit_bytes=VMEM_LIMIT)


def _pick(n, cands):
    for c in cands:
        if n % c == 0:
            return c
    raise ValueError(f"no tile for {n} in {cands}")


_DIMS = {"nn": (((1,), (0,)), ((), ())), "nt": (((1,), (1,)), ((), ())), "tn": (((0,), (0,)), ((), ()))}


MM_VMEM_BUDGET = 44 * 1024 * 1024


def _mm_tiles(m, n, kk, n_div, k_div, out_bytes, has_addend, k_piece, transposed_a):
    best = None
    for bm in (1024, 1408, 512, 256):
        if m % bm:
            continue
        for bn in (1024, 1408, 512, 256, 128):
            if n % bn or any(d % bn for d in n_div):
                continue
            for bk in (kk, 5632, 4096, 2816, 2048, 1408, 1024, 512, 256, 128):
                if bk > kk or kk % bk or any(d % bk for d in k_div):
                    continue
                if k_piece and bk % k_piece:
                    continue
                nk = kk // bk
                vmem = 4 * (bm * bk + bk * bn) + 2 * bm * bn * out_bytes + (nk > 1) * 4 * bm * bn
                vmem += has_addend * 8 * bm * bn + transposed_a * 2 * bm * bk
                vmem += (out_bytes < 4) * 4 * bm * bn
                if vmem > MM_VMEM_BUDGET:
                    continue
                score = (max(nk, 2), -(bm * bn), -bk)
                if best is None or score < best[0]:
                    best = (score, bm, bn, bk)
    if best is None:
        raise ValueError(f"no tiles for {(m, n, kk)}")
    return best[1:]


def _mm(a, b, *, mode, name, out_dtype=F32, a_cs=False, b_cs=False, o_cs=None, addend=None, deps=()):
    if mode == "tn":
        kk, m = a.shape
    elif a_cs:
        m, kk = a.shape[1], a.shape[0] * a.shape[2]
    else:
        m, kk = a.shape
    if mode == "nt":
        n = b.shape[1] if b_cs else b.shape[0]
    else:
        n = b.shape[0] * b.shape[2] if b_cs else b.shape[1]
    a_c = a.shape[2] if a_cs else None
    b_c = b.shape[2] if b_cs else None
    o_c = n // o_cs if o_cs else None
    k_piece = b_c if (b_cs and mode == "nt") else None
    n_div = [c for c in (b_c if (b_cs and mode != "nt") else None, o_c) if c]
    k_div = [a_c] if a_c else []
    bm, bn, bk = _mm_tiles(m, n, kk, n_div, k_div, jnp.dtype(out_dtype).itemsize, addend is not None, k_piece,
                           mode == "tn")
    nk = kk // bk
    kb = bk // k_piece if k_piece else 1

    def cs_idx(blk, per):
        return (blk * per[0]) // per[1], (blk * per[0] % per[1]) // per[0]

    if mode == "tn":
        a_spec = pl.BlockSpec((bk, bm), lambda i, j, k: (k, i))
    elif a_cs:
        def a_map(i, j, k):
            s, r = cs_idx(k, (bk, a_c))
            return (s, i, r)
        a_spec = pl.BlockSpec((None, bm, bk), a_map)
    else:
        a_spec = pl.BlockSpec((bm, bk), lambda i, j, k: (i, k))
    if mode == "nt":
        if b_cs:
            b_spec = pl.BlockSpec((kb, bn, b_c), lambda i, j, k: (k, j, 0))
        else:
            b_spec = pl.BlockSpec((bn, bk), lambda i, j, k: (j, k))
    else:
        if b_cs:
            def b_map(i, j, k):
                s, r = cs_idx(j, (bn, b_c))
                return (s, k, r)
            b_spec = pl.BlockSpec((None, bk, bn), b_map)
        else:
            b_spec = pl.BlockSpec((bk, bn), lambda i, j, k: (k, j))
    if o_cs:
        def o_map(i, j, k):
            s, r = cs_idx(j, (bn, o_c))
            return (s, i, r)
        o_spec = pl.BlockSpec((None, bm, bn), o_map)
        out_shape = jax.ShapeDtypeStruct((o_cs, m, o_c), out_dtype)
    else:
        o_spec = pl.BlockSpec((bm, bn), lambda i, j, k: (i, j))
        out_shape = jax.ShapeDtypeStruct((m, n), out_dtype)
    in_specs = [a_spec, b_spec]
    args = [a, b]
    if addend is not None:
        in_specs.append(pl.BlockSpec((bm, bn), lambda i, j, k: (i, j)))
        args.append(addend)
    dims = _DIMS[mode]
    in_specs += [ANY] * len(deps)
    args += list(deps)
    n_in = len(args)

    def body(*refs):
        a_ref, b_ref = refs[0], refs[1]
        add_ref = refs[2] if addend is not None else None
        o_ref = refs[n_in]
        if k_piece:
            d = None
            for p in range(kb):
                dp = lax.dot_general(a_ref[:, p * b_c:(p + 1) * b_c].astype(BF16), b_ref[p].astype(BF16), dims,
                                     preferred_element_type=F32)
                d = dp if d is None else d + dp
        else:
            d = lax.dot_general(a_ref[...].astype(BF16), b_ref[...].astype(BF16), dims, preferred_element_type=F32)

        def finish(r):
            if addend is not None:
                r = r + add_ref[...]
            o_ref[...] = r.astype(o_ref.dtype)

        if nk == 1:
            finish(d)
            return
        acc_ref = refs[n_in + 1]
        k = pl.program_id(2)

        @pl.when(k == 0)
        def _():
            acc_ref[...] = d

        if nk > 2:
            @pl.when((k > 0) & (k < nk - 1))
            def _():
                acc_ref[...] += d

        @pl.when(k == nk - 1)
        def _():
            finish(acc_ref[...] + d)

    return pl.pallas_call(
        body, name=name, out_shape=out_shape, grid=(m // bm, n // bn, nk),
        in_specs=in_specs, out_specs=o_spec,
        scratch_shapes=[pltpu.VMEM((bm, bn), F32)] if nk > 1 else [],
        compiler_params=_cp("parallel", "parallel", "arbitrary"),
    )(*args)


LN_ROWS = 256


def _ln_stats(s):
    mu = jnp.mean(s, axis=-1, keepdims=True)
    sc = s - mu
    var = jnp.mean(sc * sc, axis=-1, keepdims=True)
    rstd = lax.rsqrt(var + LN_EPS)
    return sc * rstd, rstd


def _ln_fwd(h, sub, g, b, name):
    t, d = h.shape
    row = pl.BlockSpec((LN_ROWS, d), lambda i: (i, 0))
    vec = pl.BlockSpec((1, d), lambda i: (0, 0))

    def body(h_ref, s_ref, g_ref, b_ref, y_ref, yb_ref):
        xhat, _ = _ln_stats(ALPHA * h_ref[...] + s_ref[...])
        y = xhat * g_ref[...] + b_ref[...]
        y_ref[...] = y
        yb_ref[...] = y.astype(BF16)

    return pl.pallas_call(
        body, name=name, grid=(t // LN_ROWS,),
        out_shape=(jax.ShapeDtypeStruct((t, d), F32), jax.ShapeDtypeStruct((t, d), BF16)),
        in_specs=[row, row, vec, vec], out_specs=(row, row), compiler_params=_cp("parallel"),
    )(h, sub, g, b)


def _ln_bwd_math(xhat, rstd, dy, g):
    dxhat = dy * g
    m1 = jnp.mean(dxhat, axis=-1, keepdims=True)
    m2 = jnp.mean(dxhat * xhat, axis=-1, keepdims=True)
    ds = rstd * (dxhat - m1 - xhat * m2)
    dg = jnp.sum(dy * xhat, axis=0, keepdims=True)
    db = jnp.sum(dy, axis=0, keepdims=True)
    return ds, dg, db


def _ln_bwd(h, sub, dy_a, dy_b, g, name):
    t, d = h.shape
    row = pl.BlockSpec((LN_ROWS, d), lambda i: (i, 0))
    vec = pl.BlockSpec((1, d), lambda i: (0, 0))

    def body(h_ref, s_ref, da_ref, db_ref, g_ref, ds_ref, dres_ref, dg_ref, dbeta_ref):
        xhat, rstd = _ln_stats(ALPHA * h_ref[...] + s_ref[...])
        ds, dg, db = _ln_bwd_math(xhat, rstd, da_ref[...] + db_ref[...], g_ref[...])
        ds_ref[...] = ds.astype(BF16)
        dres_ref[...] = ALPHA * ds

        @pl.when(pl.program_id(0) == 0)
        def _():
            dg_ref[...] = jnp.zeros_like(dg_ref)
            dbeta_ref[...] = jnp.zeros_like(dbeta_ref)

        dg_ref[...] += dg
        dbeta_ref[...] += db

    return pl.pallas_call(
        body, name=name, grid=(t // LN_ROWS,),
        out_shape=(jax.ShapeDtypeStruct((t, d), BF16), jax.ShapeDtypeStruct((t, d), F32),
                   jax.ShapeDtypeStruct((1, d), F32), jax.ShapeDtypeStruct((1, d), F32)),
        in_specs=[row, row, row, row, vec], out_specs=(row, row, vec, vec),
        compiler_params=_cp("arbitrary"),
    )(h, sub, dy_a, dy_b, g)


def _ln_loss_bwd(h, sub, target, g, b, name):
    t, d = h.shape
    row = pl.BlockSpec((LN_ROWS, d), lambda i: (i, 0))
    vec = pl.BlockSpec((1, d), lambda i: (0, 0))
    lvec = pl.BlockSpec((1, 128), lambda i: (0, 0))

    def body(h_ref, s_ref, t_ref, g_ref, b_ref, loss_ref, ds_ref, dres_ref, dg_ref, dbeta_ref):
        xhat, rstd = _ln_stats(ALPHA * h_ref[...] + s_ref[...])
        y = xhat * g_ref[...] + b_ref[...]
        err = y - t_ref[...]
        part = 0.5 * jnp.sum(jnp.mean(err * err, axis=-1, keepdims=True), axis=0, keepdims=True)
        ds, dg, db = _ln_bwd_math(xhat, rstd, err * (1.0 / d), g_ref[...])
        ds_ref[...] = ds.astype(BF16)
        dres_ref[...] = ALPHA * ds

        @pl.when(pl.program_id(0) == 0)
        def _():
            loss_ref[...] = jnp.zeros_like(loss_ref)
            dg_ref[...] = jnp.zeros_like(dg_ref)
            dbeta_ref[...] = jnp.zeros_like(dbeta_ref)

        loss_ref[...] += jnp.broadcast_to(part, loss_ref.shape)
        dg_ref[...] += dg
        dbeta_ref[...] += db

    return pl.pallas_call(
        body, name=name, grid=(t // LN_ROWS,),
        out_shape=(jax.ShapeDtypeStruct((1, 128), F32), jax.ShapeDtypeStruct((t, d), BF16),
                   jax.ShapeDtypeStruct((t, d), F32), jax.ShapeDtypeStruct((1, d), F32),
                   jax.ShapeDtypeStruct((1, d), F32)),
        in_specs=[row, row, row, vec, vec], out_specs=(lvec, row, row, vec, vec),
        compiler_params=_cp("arbitrary"),
    )(h, sub, target, g, b)


HG_ROWS = 512
HG_LEVELS = 6


def _hg_constants():
    c = HG_CHUNK
    t = np.arange(c)[:, None]
    j = np.arange(c)[None, :]
    blocks, masks = [], [np.eye(c)]
    for lev in range(HG_LEVELS):
        m = 1 << lev
        second = (t % (2 * m)) >= m
        mid = (t // (2 * m)) * (2 * m) + m - 1
        blocks.append((second & (j > mid) & (j <= t)) | ((~second) & (j > t) & (j <= mid)))
        same = (t // (2 * m)) == (j // (2 * m))
        masks.append(same & second & ((j % (2 * m)) < m))
    blocks += [j <= t, j > t, np.ones((c, c), bool)]
    cm = np.concatenate(blocks, axis=0).astype(np.float32)
    mk = np.concatenate(masks, axis=0).astype(np.float32)
    return jnp.asarray(cm, dtype=BF16), jnp.asarray(mk)


def _split3(x):
    hi = x.astype(BF16)
    r = x - hi.astype(F32)
    mid = r.astype(BF16)
    lo = (r - mid.astype(F32)).astype(BF16)
    return hi, mid, lo


def _sum3(x):
    n = x.shape[-1] // 3
    return x[:, :n] + x[:, n:2 * n] + x[:, 2 * n:]


@jax.custom_vjp
def _prefix(cm, lf):
    return _sum3(lax.dot_general(cm, jnp.concatenate(_split3(lf), axis=1), _DIMS["nn"], preferred_element_type=F32))


def _prefix_fwd(cm, lf):
    return _prefix(cm, lf), cm


def _prefix_bwd(cm, d):
    hi = d.astype(BF16)
    lo = (d - hi.astype(F32)).astype(BF16)
    both = lax.dot_general(cm, jnp.concatenate([hi, lo], axis=1), _DIMS["tn"], preferred_element_type=F32)
    n = d.shape[-1]
    return jnp.zeros_like(cm), both[:, :n] + both[:, n:]


_prefix.defvjp(_prefix_fwd, _prefix_bwd)

_BWD = {"nn": (("nt", 0, 1), ("tn", 1, 0)), "nt": (("nn", 0, 1), ("tn", 0, 1)), "tn": (("nt", 1, 0), ("nn", 1, 0))}


def _bdot_raw(a, b, mode):
    return lax.dot_general(a.astype(BF16), b.astype(BF16), _DIMS[mode], preferred_element_type=F32)


@functools.partial(jax.custom_vjp, nondiff_argnums=(2,))
def _bdot(a, b, mode):
    return _bdot_raw(a, b, mode)


def _bdot_fwd(a, b, mode):
    return _bdot_raw(a, b, mode), (a, b)


def _bdot_bwd(mode, res, d):
    a, b = res
    (ma, da_pos, _), (mb, db_pos, _) = _BWD[mode]
    da = _bdot_raw(d, b, ma) if da_pos == 0 else _bdot_raw(b, d, ma)
    db = _bdot_raw(d, a, mb) if db_pos == 0 else _bdot_raw(a, d, mb)
    return da, db


_bdot.defvjp(_bdot_fwd, _bdot_bwd)


def _hg_chunk(pq, pf, pi, pg, l0, l1, l2, ng, s_t, cm, mk):
    c = HG_CHUNK
    mx = jnp.maximum(jnp.maximum(l0, l1), l2)
    e0, e1, e2 = jnp.exp(l0 - mx), jnp.exp(l1 - mx), jnp.exp(l2 - mx)
    lb = e0 / (e0 + e1 + e2)
    q = pq * jax.nn.sigmoid(pq)
    t1 = jnp.log(lb)
    t2 = jnp.log1p(-lb) + jax.nn.log_sigmoid(pf)
    lf = jnp.maximum(t1, t2) + jnp.log1p(jnp.exp(-jnp.abs(t1 - t2)))
    k = (1.0 - lb) * jax.nn.sigmoid(-pf)
    x = jnp.exp(_prefix(cm, lf))
    scores = mk[0:c] * _bdot(q, k, "nt")
    for lev in range(HG_LEVELS):
        xl = x[lev * c:(lev + 1) * c]
        scores = scores + mk[(lev + 1) * c:(lev + 2) * c] * _bdot(q * xl, k * xl, "nt")
    x_incl = x[HG_LEVELS * c:(HG_LEVELS + 1) * c]
    x_after = x[(HG_LEVELS + 1) * c:(HG_LEVELS + 2) * c]
    x_total = x[(HG_LEVELS + 2) * c:(HG_LEVELS + 3) * c]
    o = _bdot(scores, pi, "nn") + _bdot(q * x_incl, s_t, "nt")
    s_new = s_t * jnp.concatenate([x_total, x_total], axis=0) + _bdot(pi, k * x_after, "tn")
    rstd = lax.rsqrt(jnp.mean(o * o, axis=-1, keepdims=True) + RMS_EPS)
    y = o * rstd * ng * (pg * jax.nn.sigmoid(pg))
    return y, s_new


def _hg_specs(t, reverse):
    nrb = t // HG_ROWS
    rb = (lambda r: nrb - 1 - r) if reverse else (lambda r: r)
    proj = pl.BlockSpec((4, HG_ROWS, HG_DIM), lambda h, r: (0, rb(r), h))
    vec = pl.BlockSpec((1, HG_DIM), lambda h, r: (0, h))
    cm = pl.BlockSpec(((HG_LEVELS + 3) * HG_CHUNK, HG_CHUNK), lambda h, r: (0, 0))
    mk = pl.BlockSpec(((HG_LEVELS + 1) * HG_CHUNK, HG_CHUNK), lambda h, r: (0, 0))
    rows = pl.BlockSpec((HG_ROWS, HG_DIM), lambda h, r: (rb(r), h))
    states = pl.BlockSpec((None, HG_ROWS // HG_CHUNK, HG_DIM, HG_DIM), lambda h, r: (h, rb(r), 0, 0))
    return proj, vec, cm, mk, rows, states


def _hg_fwd(proj, l0, l1, l2, ng):
    t = proj.shape[1]
    n_in = HG_ROWS // HG_CHUNK
    cm, mk = _hg_constants()
    p_spec, vec, cm_spec, mk_spec, rows, st_spec = _hg_specs(t, False)

    def body(p_ref, l0_ref, l1_ref, l2_ref, ng_ref, cm_ref, mk_ref, y_ref, st_ref, s_ref):
        @pl.when(pl.program_id(1) == 0)
        def _():
            s_ref[...] = jnp.zeros_like(s_ref)

        def step(ci, carry):
            sl = pl.ds(pl.multiple_of(ci * HG_CHUNK, HG_CHUNK), HG_CHUNK)
            s_t = s_ref[...]
            st_ref[ci] = s_t
            y, s_new = _hg_chunk(p_ref[0, sl, :], p_ref[1, sl, :], p_ref[2, sl, :], p_ref[3, sl, :],
                                 l0_ref[...], l1_ref[...], l2_ref[...], ng_ref[...], s_t,
                                 cm_ref[...], mk_ref[...])
            y_ref[sl, :] = y.astype(BF16)
            s_ref[...] = s_new
            return carry

        lax.fori_loop(0, n_in, step, 0, unroll=8)

    return pl.pallas_call(
        body, name="hg_fwd", grid=(HG_HEADS, t // HG_ROWS),
        out_shape=(jax.ShapeDtypeStruct((t, D_MODEL), BF16),
                   jax.ShapeDtypeStruct((HG_HEADS, t // HG_CHUNK, HG_DIM, HG_DIM), F32)),
        in_specs=[p_spec, vec, vec, vec, vec, cm_spec, mk_spec], out_specs=(rows, st_spec),
        scratch_shapes=[pltpu.VMEM((HG_DIM, HG_DIM), F32)],
        compiler_params=_cp("parallel", "arbitrary"),
    )(proj, l0, l1, l2, ng, cm, mk)


def _hg_bwd(proj, states, dy, l0, l1, l2, ng):
    t = proj.shape[1]
    n_in = HG_ROWS // HG_CHUNK
    cm, mk = _hg_constants()
    p_spec, vec, cm_spec, mk_spec, rows, st_spec = _hg_specs(t, True)

    def body(p_ref, st_ref, dy_ref, l0_ref, l1_ref, l2_ref, ng_ref, cm_ref, mk_ref,
             dp_ref, dl0_ref, dl1_ref, dl2_ref, dng_ref, ds_ref):
        @pl.when(pl.program_id(1) == 0)
        def _():
            ds_ref[...] = jnp.zeros_like(ds_ref)
            for r in (dl0_ref, dl1_ref, dl2_ref, dng_ref):
                r[...] = jnp.zeros_like(r)

        def step(it, carry):
            ci = n_in - 1 - it
            sl = pl.ds(pl.multiple_of(ci * HG_CHUNK, HG_CHUNK), HG_CHUNK)
            fn = functools.partial(_hg_chunk, cm=cm_ref[...], mk=mk_ref[...])
            _, vjp = jax.vjp(fn, p_ref[0, sl, :], p_ref[1, sl, :], p_ref[2, sl, :], p_ref[3, sl, :],
                             l0_ref[...], l1_ref[...], l2_ref[...], ng_ref[...], st_ref[ci])
            dq, df, di, dg, d0, d1, d2, dn, ds = vjp((dy_ref[sl, :], ds_ref[...]))
            dp_ref[0, sl, :] = dq.astype(BF16)
            dp_ref[1, sl, :] = df.astype(BF16)
            dp_ref[2, sl, :] = di.astype(BF16)
            dp_ref[3, sl, :] = dg.astype(BF16)
            dl0_ref[...] += d0
            dl1_ref[...] += d1
            dl2_ref[...] += d2
            dng_ref[...] += dn
            ds_ref[...] = ds
            return carry

        lax.fori_loop(0, n_in, step, 0, unroll=8)

    v_shape = jax.ShapeDtypeStruct((1, D_MODEL), F32)
    return pl.pallas_call(
        body, name="hg_bwd", grid=(HG_HEADS, t // HG_ROWS),
        out_shape=(jax.ShapeDtypeStruct((4, t, D_MODEL), BF16), v_shape, v_shape, v_shape, v_shape),
        in_specs=[p_spec, st_spec, rows, vec, vec, vec, vec, cm_spec, mk_spec],
        out_specs=(p_spec, vec, vec, vec, vec),
        scratch_shapes=[pltpu.VMEM((HG_DIM, HG_DIM), F32)],
        compiler_params=_cp("parallel", "arbitrary"),
    )(proj, states, dy, l0, l1, l2, ng, cm, mk)


_SQRT_HALF = 0.7071067811865476
_INV_SQRT_2PI = 0.3989422804014327


def _gelu(x):
    return 0.5 * x * (1.0 + lax.erf(x * _SQRT_HALF))


def _gelu_grad(x):
    return 0.5 * (1.0 + lax.erf(x * _SQRT_HALF)) + x * (_INV_SQRT_2PI * jnp.exp(-0.5 * x * x))


def _tril(n):
    return (lax.broadcasted_iota(jnp.int32, (n, n), 0) >= lax.broadcasted_iota(jnp.int32, (n, n), 1)).astype(F32)


def _sg_specs(d):
    row = lambda w: pl.BlockSpec((SG_CHUNK, w), lambda i: (i, 0))
    vec = pl.BlockSpec((1, d), lambda i: (0, 0))
    cube = pl.BlockSpec((SG_GROUPS, SG_CHUNK, SG_CHUNK), lambda i: (0, 0, 0))
    return row, vec, cube


def _sg_fwd(pre, ln_g, ln_b, w_s, bias):
    t = pre.shape[0]
    d = D_MODEL
    row, vec, cube = _sg_specs(d)

    def body(pre_ref, g_ref, b_ref, w_ref, bias_ref, y_ref, vln_ref):
        u = _gelu(pre_ref[:, :d])
        vhat, _ = _ln_stats(_gelu(pre_ref[:, d:]))
        vln_ref[...] = vhat * g_ref[...] + b_ref[...]
        tril = _tril(SG_CHUNK)
        for g in range(SG_GROUPS):
            cs = slice(g * SG_DIM, (g + 1) * SG_DIM)
            gate = _bdot_raw(w_ref[g] * tril, vln_ref[:, cs], "nn") + bias_ref[g]
            y_ref[:, cs] = (u[:, cs] * gate).astype(BF16)

    return pl.pallas_call(
        body, name="sg_fwd", grid=(t // SG_CHUNK,), out_shape=jax.ShapeDtypeStruct((t, d), BF16),
        in_specs=[row(2 * d), vec, vec, cube, cube], out_specs=row(d),
        scratch_shapes=[pltpu.VMEM((SG_CHUNK, d), F32)], compiler_params=_cp("parallel"),
    )(pre, ln_g, ln_b, w_s, bias)


def _sg_bwd(pre, dy, ln_g, ln_b, w_s, bias):
    t = pre.shape[0]
    d = D_MODEL
    row, vec, cube = _sg_specs(d)
    dbs_spec = pl.BlockSpec((SG_GROUPS, 8, SG_CHUNK), lambda i: (0, 0, 0))

    def body(pre_ref, dy_ref, g_ref, b_ref, w_ref, bias_ref, dpre_ref, dw_ref, dbs_ref, dlg_ref, dlb_ref,
             vln_ref, dvln_ref):
        @pl.when(pl.program_id(0) == 0)
        def _():
            for r in (dw_ref, dbs_ref, dlg_ref, dlb_ref):
                r[...] = jnp.zeros_like(r)

        pu = pre_ref[:, :d]
        pv = pre_ref[:, d:]
        u = _gelu(pu)
        vhat, rstd = _ln_stats(_gelu(pv))
        vln_ref[...] = vhat * g_ref[...] + b_ref[...]
        tril = _tril(SG_CHUNK)
        ones = jnp.ones((8, SG_DIM), F32)
        for g in range(SG_GROUPS):
            cs = slice(g * SG_DIM, (g + 1) * SG_DIM)
            wc = w_ref[g] * tril
            vg = vln_ref[:, cs]
            gate = _bdot_raw(wc, vg, "nn") + bias_ref[g]
            dyg = dy_ref[:, cs]
            dgate = dyg * u[:, cs]
            dpre_ref[:, cs] = (dyg * gate * _gelu_grad(pu[:, cs])).astype(BF16)
            dw_ref[g] += tril * _bdot_raw(dgate, vg, "nt")
            dbs_ref[g] += lax.dot_general(ones, dgate, _DIMS["nt"], precision=HI, preferred_element_type=F32)
            dvln_ref[:, cs] = _bdot_raw(wc, dgate, "tn")
        dvln = dvln_ref[...]
        dv, dg, db = _ln_bwd_math(vhat, rstd, dvln, g_ref[...])
        dlg_ref[...] += dg
        dlb_ref[...] += db
        dpre_ref[:, d:] = (dv * _gelu_grad(pv)).astype(BF16)

    return pl.pallas_call(
        body, name="sg_bwd", grid=(t // SG_CHUNK,),
        out_shape=(jax.ShapeDtypeStruct((t, 2 * d), BF16), jax.ShapeDtypeStruct(w_s.shape, F32),
                   jax.ShapeDtypeStruct((SG_GROUPS, 8, SG_CHUNK), F32),
                   jax.ShapeDtypeStruct((1, d), F32), jax.ShapeDtypeStruct((1, d), F32)),
        in_specs=[row(2 * d), row(d), vec, vec, cube, cube],
        out_specs=(row(2 * d), cube, dbs_spec, vec, vec),
        scratch_shapes=[pltpu.VMEM((SG_CHUNK, d), F32), pltpu.VMEM((SG_CHUNK, d), F32)],
        compiler_params=_cp("arbitrary"),
    )(pre, dy, ln_g, ln_b, w_s, bias)


FFN_ROWS = 256
FFN_COLS = 1408
HALO = 16


def _ffn_conv(a_ext, w_ref, cb_ref):
    a1 = pltpu.roll(a_ext, 1, 0)
    a2 = pltpu.roll(a_ext, 2, 0)
    return w_ref[0:1, :] * a2 + w_ref[1:2, :] * a1 + w_ref[2:3, :] * a_ext + cb_ref[...], a1, a2


def _ffn_fwd(hh, conv_w, conv_b):
    t = hh.shape[1]
    nb8 = FFN_ROWS // HALO
    main = pl.BlockSpec((2, FFN_ROWS, FFN_COLS), lambda c, r: (0, r, c))
    prev = pl.BlockSpec((None, HALO, FFN_COLS), lambda c, r: (0, jnp.maximum(r * nb8 - 1, 0), c))
    wspec = pl.BlockSpec((3, FFN_COLS), lambda c, r: (0, c))
    bspec = pl.BlockSpec((1, FFN_COLS), lambda c, r: (0, c))

    def body(m_ref, p_ref, w_ref, cb_ref, z_ref):
        prev = jnp.where(pl.program_id(1) == 0, 0.0, p_ref[...].astype(F32))
        a_ext = jnp.concatenate([prev, m_ref[0].astype(F32)], axis=0)
        a, _, _ = _ffn_conv(a_ext, w_ref, cb_ref)
        a = a[HALO:]
        z_ref[...] = (a * jax.nn.sigmoid(a) * m_ref[1].astype(F32)).astype(BF16)

    return pl.pallas_call(
        body, name="ffn_fwd", grid=(D_FF // FFN_COLS, t // FFN_ROWS),
        out_shape=jax.ShapeDtypeStruct((t, D_FF), BF16),
        in_specs=[main, prev, wspec, bspec],
        out_specs=pl.BlockSpec((FFN_ROWS, FFN_COLS), lambda c, r: (r, c)),
        compiler_params=_cp("parallel", "parallel"),
    )(hh, hh, conv_w, conv_b)


def _ffn_bwd(hh, dz, conv_w, conv_b):
    t = hh.shape[1]
    nb8 = FFN_ROWS // HALO
    last8 = t // HALO - 1
    nr = t // FFN_ROWS
    main = pl.BlockSpec((2, FFN_ROWS, FFN_COLS), lambda c, r: (0, r, c))
    prev = pl.BlockSpec((None, HALO, FFN_COLS), lambda c, r: (0, jnp.maximum(r * nb8 - 1, 0), c))
    nxt = pl.BlockSpec((2, HALO, FFN_COLS), lambda c, r: (0, jnp.minimum((r + 1) * nb8, last8), c))
    dmain = pl.BlockSpec((FFN_ROWS, FFN_COLS), lambda c, r: (r, c))
    dnxt = pl.BlockSpec((HALO, FFN_COLS), lambda c, r: (jnp.minimum((r + 1) * nb8, last8), c))
    wspec = pl.BlockSpec((3, FFN_COLS), lambda c, r: (0, c))
    bspec = pl.BlockSpec((1, FFN_COLS), lambda c, r: (0, c))

    def body(m_ref, p_ref, n_ref, dz_ref, dzn_ref, w_ref, cb_ref, dh_ref, dw_ref, dcb_ref):
        r = pl.program_id(1)

        @pl.when(r == 0)
        def _():
            dw_ref[...] = jnp.zeros_like(dw_ref)
            dcb_ref[...] = jnp.zeros_like(dcb_ref)

        prev = jnp.where(r == 0, 0.0, p_ref[...].astype(F32))
        a_ext = jnp.concatenate([prev, m_ref[0].astype(F32), n_ref[0].astype(F32)], axis=0)
        a, a1, a2 = _ffn_conv(a_ext, w_ref, cb_ref)
        a = a[HALO:]
        b_ext = jnp.concatenate([m_ref[1], n_ref[1]], axis=0).astype(F32)
        dz_main = dz_ref[...].astype(F32)
        dz_ext = jnp.concatenate([dz_main, jnp.where(r == nr - 1, 0.0, dzn_ref[...].astype(F32))], axis=0)
        sig = jax.nn.sigmoid(a)
        da = dz_ext * b_ext * (sig * (1.0 + a * (1.0 - sig)))
        n_ext = FFN_ROWS + HALO
        da_p1 = pltpu.roll(da, n_ext - 1, 0)
        da_p2 = pltpu.roll(da, n_ext - 2, 0)
        da_raw = w_ref[2:3, :] * da + w_ref[1:2, :] * da_p1 + w_ref[0:1, :] * da_p2
        dh_ref[0] = da_raw[:FFN_ROWS].astype(BF16)
        dh_ref[1] = (dz_main * (a * sig)[:FFN_ROWS]).astype(BF16)
        dam = da[:FFN_ROWS]
        rows = slice(HALO, HALO + FFN_ROWS)
        dw_ref[0:1, :] += jnp.sum(dam * a2[rows], axis=0, keepdims=True)
        dw_ref[1:2, :] += jnp.sum(dam * a1[rows], axis=0, keepdims=True)
        dw_ref[2:3, :] += jnp.sum(dam * a_ext[rows], axis=0, keepdims=True)
        dcb_ref[...] += jnp.sum(dam, axis=0, keepdims=True)

    return pl.pallas_call(
        body, name="ffn_bwd", grid=(D_FF // FFN_COLS, nr),
        out_shape=(jax.ShapeDtypeStruct((2, t, D_FF), BF16), jax.ShapeDtypeStruct((3, D_FF), F32),
                   jax.ShapeDtypeStruct((1, D_FF), F32)),
        in_specs=[main, prev, nxt, dmain, dnxt, wspec, bspec],
        out_specs=(main, wspec, bspec), compiler_params=_cp("parallel", "arbitrary"),
    )(hh, hh, hh, dz, dz, conv_w, conv_b)


def _adamw_math(w, g, m, v):
    m = ADAM_B1 * m + (1.0 - ADAM_B1) * g
    v = ADAM_B2 * v + (1.0 - ADAM_B2) * (g * g)
    m_hat = m / (1.0 - ADAM_B1 ** ADAM_STEP)
    v_hat = v / (1.0 - ADAM_B2 ** ADAM_STEP)
    delta = -ADAM_LR * (m_hat / (jnp.sqrt(v_hat) + ADAM_EPS) + ADAM_WD * w)
    return delta, m, v


def _as2d(shape):
    n = int(np.prod(shape))
    c = shape[-1] if shape[-1] % 128 == 0 else (128 if n % 128 == 0 else shape[-1])
    return n // c, c


def _adamw(w, g, m, v, name, parts=None, own=None, is_me=None, layer=None, prev=None):
    shape = w.shape if layer is None else w.shape[1:]
    r, c = _as2d(shape)
    br = r if r <= 512 else _pick(r, (256, 176, 128, 64, 8))
    blk = pl.BlockSpec((br, c), lambda i: (i, 0))
    if layer is None:
        wblk, o = blk, jax.ShapeDtypeStruct((r, c), F32)
        w2, m2, v2 = (a.reshape(r, c) for a in (w, m, v))
    else:
        wblk, o = pl.BlockSpec((None, br, c), lambda i: (layer, i, 0)), jax.ShapeDtypeStruct((w.shape[0], r, c), F32)
        w2, m2, v2 = (a.reshape(w.shape[0], r, c) for a in (w, m, v))
    extra, extra_specs = [], []
    if parts is not None:
        g_in = parts.reshape(N_DEV, r, c)
        g_spec = pl.BlockSpec((N_DEV, br, c), lambda i: (0, i, 0))
        extra = [own.reshape(r, c), is_me]
        extra_specs = [blk, pl.BlockSpec(memory_space=pltpu.SMEM)]
    else:
        g_in = g.reshape(r, c)
        g_spec = blk
    aliases = {}
    if prev is not None:
        aliases = {4 + len(extra) + j: j for j in range(4)}
        extra = extra + [a.reshape(o.shape) for a in prev]
        extra_specs = extra_specs + [ANY] * 4

    def body(w_ref, g_ref, m_ref, v_ref, *rest):
        go_ref, d_ref, mo_ref, vo_ref = rest[-4:]
        if parts is not None:
            own_ref, me_ref = rest[:2]
            gsum = None
            for i in range(N_DEV):
                term = jnp.where(me_ref[i] > 0.5, own_ref[...], g_ref[i]).astype(F32)
                gsum = term if gsum is None else gsum + term
        else:
            gsum = g_ref[...]
        delta, m_new, v_new = _adamw_math(w_ref[...], gsum, m_ref[...], v_ref[...])
        go_ref[...] = gsum
        d_ref[...] = delta
        mo_ref[...] = m_new
        vo_ref[...] = v_new

    outs = pl.pallas_call(
        body, name=name, grid=(r // br,), out_shape=(o, o, o, o),
        in_specs=[wblk, g_spec, wblk, wblk] + extra_specs, out_specs=(wblk, wblk, wblk, wblk),
        input_output_aliases=aliases, compiler_params=_cp("parallel"),
    )(w2, g_in, m2, v2, *extra)
    return tuple(a.reshape(w.shape) for a in outs)


def _me_and_peers():
    x, y, c = lax.axis_index("x"), lax.axis_index("y"), lax.axis_index("c")
    me = 4 * x + 2 * y + c
    peers = []
    for k in range(1, N_DEV):
        kx, ky, kc = (k >> 2) & 1, (k >> 1) & 1, k & 1
        px, py, pc = x ^ kx, y ^ ky, c ^ kc
        peers.append(((px, py, pc), 4 * px + 2 * py + pc))
    return me, peers


def _all_gather(shard, name):
    def body(x_ref, o_ref, send_sems, recv_sems, local_sem):
        me, peers = _me_and_peers()
        mine = pltpu.make_async_copy(x_ref, o_ref.at[me], local_sem)
        mine.start()
        sends = []
        for k, (dev, _) in enumerate(peers):
            cp = pltpu.make_async_remote_copy(src_ref=x_ref, dst_ref=o_ref.at[me], send_sem=send_sems.at[k],
                                              recv_sem=recv_sems.at[k], device_id=dev, device_id_type=MESH)
            cp.start()
            sends.append(cp)
        for k, (dev, idx) in enumerate(peers):
            pltpu.make_async_remote_copy(src_ref=x_ref, dst_ref=o_ref.at[idx], send_sem=send_sems.at[k],
                                         recv_sem=recv_sems.at[k], device_id=dev, device_id_type=MESH).wait_recv()
        for cp in sends:
            cp.wait_send()
        mine.wait()

    return pl.pallas_call(
        body, name=name, out_shape=jax.ShapeDtypeStruct((N_DEV,) + shard.shape, shard.dtype),
        in_specs=[ANY], out_specs=ANY,
        scratch_shapes=[pltpu.SemaphoreType.DMA((N_DEV - 1,)), pltpu.SemaphoreType.DMA((N_DEV - 1,)),
                        pltpu.SemaphoreType.DMA],
        compiler_params=pltpu.CompilerParams(has_side_effects=True),
    )(shard)


HBM = pl.BlockSpec(memory_space=pltpu.HBM)
SEM = pl.BlockSpec(memory_space=pltpu.SEMAPHORE)
EFFECT = pltpu.SideEffectType.DATAFLOW_SIDE_EFFECTING


def _xchg_start(parts, land, *, dep, name):
    n_buf = 1 if parts is None else 2

    def body(*refs):
        land_ref = refs[n_buf - 1]
        send_sem, recv_sem = refs[n_buf + 1], refs[n_buf + 2]
        token = refs[2 * n_buf + 3]
        me, peers = _me_and_peers()
        for dev, idx in peers:
            src = land_ref.at[me] if parts is None else refs[0].at[idx]
            pltpu.make_async_remote_copy(src_ref=src, dst_ref=land_ref.at[me], send_sem=send_sem, recv_sem=recv_sem,
                                         device_id=dev, device_id_type=MESH).start()
        token[...] = jnp.zeros_like(token)

    bufs = [land] if parts is None else [parts, land]
    return pl.pallas_call(
        body, name=name,
        out_shape=(pltpu.SemaphoreType.DMA(()), pltpu.SemaphoreType.DMA(()),
                   *[pltpu.HBM(b.shape, b.dtype) for b in bufs], jax.ShapeDtypeStruct((8, 128), F32)),
        in_specs=(*[HBM] * n_buf, ANY), out_specs=(SEM, SEM, *[HBM] * n_buf, pl.BlockSpec(memory_space=pltpu.VMEM)),
        input_output_aliases={i: 2 + i for i in range(n_buf)},
        compiler_params=pltpu.CompilerParams(has_side_effects=EFFECT),
    )(*[pltpu.with_memory_space_constraint(b, pltpu.HBM) for b in bufs], dep)


def _xchg_wait(handles, after, name):
    send_sem, recv_sem, *bufs, _ = handles
    n_buf = len(bufs)

    def body(*refs):
        land_ref = refs[n_buf - 1]
        send_sem, recv_sem = refs[n_buf], refs[n_buf + 1]
        x, y, c = lax.axis_index("x"), lax.axis_index("y"), lax.axis_index("c")
        seven = land_ref.at[pl.ds(0, N_DEV - 1)]
        cp = pltpu.make_async_remote_copy(src_ref=seven, dst_ref=seven, send_sem=send_sem, recv_sem=recv_sem,
                                          device_id=(x, y, c), device_id_type=MESH)
        cp.wait_send()
        cp.wait_recv()

    return pl.pallas_call(
        body, name=name, out_shape=tuple(pltpu.HBM(b.shape, b.dtype) for b in bufs),
        in_specs=(*[HBM] * n_buf, SEM, SEM, ANY), out_specs=tuple([HBM] * n_buf),
        input_output_aliases={i: i for i in range(n_buf)},
        compiler_params=pltpu.CompilerParams(has_side_effects=EFFECT),
    )(*bufs, send_sem, recv_sem, after)


def _sum_devices(v, name):
    _, r, c = v.shape

    def body(x_ref, o_ref):
        acc = x_ref[0]
        for i in range(1, N_DEV):
            acc = acc + x_ref[i]
        o_ref[...] = acc

    return pl.pallas_call(
        body, name=name, out_shape=jax.ShapeDtypeStruct((r, c), F32),
        in_specs=[pl.BlockSpec(memory_space=pltpu.VMEM)], out_specs=pl.BlockSpec(memory_space=pltpu.VMEM),
        compiler_params=pltpu.CompilerParams(vmem_limit_bytes=VMEM_LIMIT),
    )(v)


def _local_step(x, target, lb_logits, p, weight, send_grad, send_small):
    g = {}
    l0, l1, l2 = lb_logits[0:1], lb_logits[1:2], lb_logits[2:3]
    x_bf = x.astype(BF16)

    proj0 = _mm(x_bf, weight("hg_w_in", x_bf), mode="nn", b_cs=True, o_cs=4, name="mm_hg_in")
    y0, states = _hg_fwd(proj0, l0, l1, l2, p["hg_norm_g"])
    mixed0 = _mm(y0, weight("hg_w_out", y0), mode="nn", name="mm_hg_out")
    hh_l, z_l = [], []

    def ffn_forward(layer, h_in_bf):
        hh = _mm(h_in_bf, weight(f"ffn_w_up{layer}", h_in_bf), mode="nn", b_cs=True, o_cs=2, out_dtype=BF16,
                 name=f"mm_up{layer}")
        z = _ffn_fwd(hh, p["ffn_conv_w"][layer], p["ffn_conv_b"][layer])
        ffn = _mm(z, weight(f"ffn_w_down{layer}", z), mode="nn", name=f"mm_down{layer}")
        hh_l.append(hh)
        z_l.append(z)
        return ffn

    h1, h1b = _ln_fwd(x, mixed0, p["ln1_g"][0], p["ln1_b"][0], "ln1_0")
    ffn0 = ffn_forward(0, h1b)
    h2, h2b = _ln_fwd(h1, ffn0, p["ln2_g"][0], p["ln2_b"][0], "ln2_0")
    pre1 = _mm(h2b, weight("sg_w_in", h2b), mode="nn", b_cs=True, name="mm_sg_in")
    y1 = _sg_fwd(pre1, p["sg_ln_g"], p["sg_ln_b"], p["sg_w_s"], p["sg_bias"])
    mixed1 = _mm(y1, weight("sg_w_out", y1), mode="nn", name="mm_sg_out")
    h3, h3b = _ln_fwd(h2, mixed1, p["ln1_g"][1], p["ln1_b"][1], "ln1_1")
    ffn1 = ffn_forward(1, h3b)

    loss, ds, dres, g["ln2_g1"], g["ln2_b1"] = _ln_loss_bwd(h3, ffn1, target, p["ln2_g"][1], p["ln2_b"][1], "ln2_1_loss")

    def ffn_backward(layer, ds_bf, h_in_bf):
        tok = send_grad(f"ffn_w_down{layer}",
                        _mm(z_l[layer], ds_bf, mode="tn", out_dtype=BF16, name=f"mm_dw_down{layer}"))
        dz = _mm(ds_bf, weight(f"ffn_w_down{layer}", None), mode="nt", out_dtype=BF16, name=f"mm_d_z{layer}",
                 deps=(tok,))
        dhh, g[f"conv_w{layer}"], g[f"conv_b{layer}"] = _ffn_bwd(hh_l[layer], dz, p["ffn_conv_w"][layer],
                                                                   p["ffn_conv_b"][layer])
        tok = send_grad(f"ffn_w_up{layer}", _mm(h_in_bf, dhh, mode="tn", b_cs=True, o_cs=N_DEV, out_dtype=BF16,
                                                name=f"mm_dw_up{layer}"))
        return _mm(dhh, weight(f"ffn_w_up{layer}", None), mode="nt", a_cs=True, b_cs=True, name=f"mm_d_up{layer}",
                   deps=(tok,))

    dh3 = ffn_backward(1, ds, h3b)
    ds, dres, g["ln1_g1"], g["ln1_b1"] = _ln_bwd(h2, mixed1, dres, dh3, p["ln1_g"][1], "ln1_1_bwd")
    tok = send_grad("sg_w_out", _mm(y1, ds, mode="tn", out_dtype=BF16, name="mm_dw_sg_out"))
    dy1 = _mm(ds, weight("sg_w_out", None), mode="nt", name="mm_d_sg_out", deps=(tok,))
    dpre1, g["sg_w_s"], dbs, g["sg_ln_g"], g["sg_ln_b"] = _sg_bwd(pre1, dy1, p["sg_ln_g"], p["sg_ln_b"],
                                                                    p["sg_w_s"], p["sg_bias"])
    g["sg_b_s"] = dbs[:, 0, :]
    tok = send_grad("sg_w_in", _mm(h2b, dpre1, mode="tn", o_cs=N_DEV, out_dtype=BF16, name="mm_dw_sg_in"))
    dh2 = _mm(dpre1, weight("sg_w_in", None), mode="nt", b_cs=True, name="mm_d_sg_in", deps=(tok,))
    ds, dres, g["ln2_g0"], g["ln2_b0"] = _ln_bwd(h1, ffn0, dres, dh2, p["ln2_g"][0], "ln2_0_bwd")
    dh1 = ffn_backward(0, ds, h1b)
    ds, dres, g["ln1_g0"], g["ln1_b0"] = _ln_bwd(x, mixed0, dres, dh1, p["ln1_g"][0], "ln1_0_bwd")
    tok = send_grad("hg_w_out", _mm(y0, ds, mode="tn", out_dtype=BF16, name="mm_dw_hg_out"))
    dy0 = _mm(ds, weight("hg_w_out", None), mode="nt", name="mm_d_hg_out", deps=(tok,))
    dproj, d0, d1, d2, g["hg_norm_g"] = _hg_bwd(proj0, states, dy0, l0, l1, l2, p["hg_norm_g"])
    g["lb_logits"] = jnp.concatenate([d0, d1, d2], axis=0)
    tok = send_small(g)
    tok = send_grad("hg_w_in", _mm(x_bf, dproj, mode="tn", b_cs=True, o_cs=N_DEV, out_dtype=BF16, name="mm_dw_hg_in",
                                   deps=(tok,)))
    grad_x = _mm(dproj, weight("hg_w_in", None), mode="nt", a_cs=True, b_cs=True, addend=dres, name="mm_d_hg_in",
                 deps=(tok,))
    return loss[0, 0], grad_x


_SMALL = ("lb_logits", "hg_norm_g", "sg_ln_g", "sg_ln_b", "sg_w_s", "sg_b_s", "ffn_conv_w", "ffn_conv_b",
          "ln1_g", "ln1_b", "ln2_g", "ln2_b")
_NAMES = ("lb_logits", "hg_w_in", "hg_norm_g", "hg_w_out", "sg_w_in", "sg_ln_g", "sg_ln_b", "sg_w_s", "sg_b_s",
          "sg_w_out", "ffn_w_up", "ffn_conv_w", "ffn_conv_b", "ffn_w_down", "ln1_g", "ln1_b", "ln2_g", "ln2_b")


def kernel(x, lb_logits, hg_w_in, hg_norm_g, hg_w_out, sg_w_in, sg_ln_g, sg_ln_b, sg_w_s, sg_b_s, sg_w_out, ffn_w_up, ffn_conv_w, ffn_conv_b, ffn_w_down, ln1_g, ln1_b, ln2_g, ln2_b, loss_target, m_lb_logits, m_hg_w_in, m_hg_norm_g, m_hg_w_out, m_sg_w_in, m_sg_ln_g, m_sg_ln_b, m_sg_w_s, m_sg_b_s, m_sg_w_out, m_ffn_w_up, m_ffn_conv_w, m_ffn_conv_b, m_ffn_w_down, m_ln1_g, m_ln1_b, m_ln2_g, m_ln2_b, v_lb_logits, v_hg_w_in, v_hg_norm_g, v_hg_w_out, v_sg_w_in, v_sg_ln_g, v_sg_ln_b, v_sg_w_s, v_sg_b_s, v_sg_w_out, v_ffn_w_up, v_ffn_conv_w, v_ffn_conv_b, v_ffn_w_down, v_ln1_g, v_ln1_b, v_ln2_g, v_ln2_b):
    w = dict(lb_logits=lb_logits, hg_w_in=hg_w_in, hg_norm_g=hg_norm_g, hg_w_out=hg_w_out, sg_w_in=sg_w_in,
             sg_ln_g=sg_ln_g, sg_ln_b=sg_ln_b, sg_w_s=sg_w_s, sg_b_s=sg_b_s, sg_w_out=sg_w_out, ffn_w_up=ffn_w_up,
             ffn_conv_w=ffn_conv_w, ffn_conv_b=ffn_conv_b, ffn_w_down=ffn_w_down, ln1_g=ln1_g, ln1_b=ln1_b,
             ln2_g=ln2_g, ln2_b=ln2_b)
    m = dict(lb_logits=m_lb_logits, hg_w_in=m_hg_w_in, hg_norm_g=m_hg_norm_g, hg_w_out=m_hg_w_out,
             sg_w_in=m_sg_w_in, sg_ln_g=m_sg_ln_g, sg_ln_b=m_sg_ln_b, sg_w_s=m_sg_w_s, sg_b_s=m_sg_b_s,
             sg_w_out=m_sg_w_out, ffn_w_up=m_ffn_w_up, ffn_conv_w=m_ffn_conv_w, ffn_conv_b=m_ffn_conv_b,
             ffn_w_down=m_ffn_w_down, ln1_g=m_ln1_g, ln1_b=m_ln1_b, ln2_g=m_ln2_g, ln2_b=m_ln2_b)
    v = dict(lb_logits=v_lb_logits, hg_w_in=v_hg_w_in, hg_norm_g=v_hg_norm_g, hg_w_out=v_hg_w_out,
             sg_w_in=v_sg_w_in, sg_ln_g=v_sg_ln_g, sg_ln_b=v_sg_ln_b, sg_w_s=v_sg_w_s, sg_b_s=v_sg_b_s,
             sg_w_out=v_sg_w_out, ffn_w_up=v_ffn_w_up, ffn_conv_w=v_ffn_conv_w, ffn_conv_b=v_ffn_conv_b,
             ffn_w_down=v_ffn_w_down, ln1_g=v_ln1_g, ln1_b=v_ln1_b, ln2_g=v_ln2_g, ln2_b=v_ln2_b)
    me = 4 * lax.axis_index("x") + 2 * lax.axis_index("y") + lax.axis_index("c")
    is_me = (jnp.arange(N_DEV) == me).astype(F32)
    d = D_MODEL

    shards = [("hg_w_in", hg_w_in[0]), ("hg_w_out", hg_w_out[0]), ("ffn_w_up0", ffn_w_up[0]),
              ("ffn_w_down0", ffn_w_down[0]), ("sg_w_in", sg_w_in[0]), ("sg_w_out", sg_w_out[0]),
              ("ffn_w_up1", ffn_w_up[1]), ("ffn_w_down1", ffn_w_down[1])]
    row_sharded = {"hg_w_out": (d, d), "sg_w_out": (d, d), "ffn_w_down0": (D_FF, d), "ffn_w_down1": (D_FF, d)}
    sv = jnp.zeros((8, 768), F32)
    sv = sv.at[0, :256].set(sg_ln_g[0]).at[1, :256].set(sg_ln_b[0]).at[2:8, :704].set(ffn_conv_w.reshape(6, 704))
    sv = _all_gather(sv, "ag_small")
    gathers, gathered, dep = {}, {}, sv
    for n, shard in shards:
        land = lax.dynamic_update_index_in_dim(lax.empty((N_DEV,) + shard.shape, BF16), shard.astype(BF16), me, 0)
        gathers[n] = _xchg_start(None, land, dep=dep, name=f"ag_{n}")
        dep = gathers[n][-1]

    last_start = dep

    def weight(n, after):
        if n not in gathered:
            if n == shards[0][0]:
                after = last_start
            (full,) = _xchg_wait(gathers[n], after, f"agw_{n}")
            gathered[n] = full.reshape(row_sharded[n]) if n in row_sharded else full
        return gathered[n]

    grad_sends = {}

    def send_grad(n, parts):
        shape = (N_DEV,) + dict(shards)[n].shape
        grad_sends[n] = _xchg_start(parts.reshape(shape), lax.empty(shape, parts.dtype), dep=is_me, name=f"rs_{n}")
        return grad_sends[n][-1]

    small_send = {}

    def send_small(g):
        small = [g["lb_logits"], g["hg_norm_g"], g["sg_ln_g"], g["sg_ln_b"], g["sg_w_s"], g["sg_b_s"],
                 g["conv_w0"], g["conv_w1"], g["conv_b0"], g["conv_b1"], g["ln1_g0"], g["ln1_g1"],
                 g["ln1_b0"], g["ln1_b1"], g["ln2_g0"], g["ln2_g1"], g["ln2_b0"], g["ln2_b1"]]
        flat = jnp.concatenate([a.reshape(-1) for a in small])
        rows = -(-flat.shape[0] // (8 * 128)) * 8
        flat = jnp.pad(flat, (0, rows * 128 - flat.shape[0])).reshape(rows, 128)
        land = lax.dynamic_update_index_in_dim(lax.empty((N_DEV, rows, 128), F32), flat, me, 0)
        small_send["shapes"] = [a.shape for a in small]
        small_send["handles"] = _xchg_start(None, land, dep=is_me, name="ar_small")
        return small_send["handles"][-1]

    p = {}
    p["sg_ln_g"] = sv[:, 0, :256].reshape(1, d)
    p["sg_ln_b"] = sv[:, 1, :256].reshape(1, d)
    conv_w_full = jnp.transpose(sv[:, 2:8, :704].reshape(N_DEV, DEPTH, 3, 704), (1, 2, 0, 3)).reshape(DEPTH, 3, D_FF)
    p["ffn_conv_w"] = [conv_w_full[l] for l in range(DEPTH)]
    p["ffn_conv_b"] = [ffn_conv_b[l:l + 1] for l in range(DEPTH)]
    p["hg_norm_g"] = hg_norm_g
    p["sg_w_s"] = sg_w_s[0]
    p["sg_bias"] = jnp.broadcast_to(sg_b_s[0][:, :, None], (SG_GROUPS, SG_CHUNK, SG_DIM))
    for n in ("ln1_g", "ln1_b", "ln2_g", "ln2_b"):
        p[n] = [w[n][l:l + 1] for l in range(DEPTH)]

    loss_part, grad_x = _local_step(x[0], loss_target[0], lb_logits, p, weight, send_grad, send_small)
    loss = lax.psum(loss_part, ("x", "y", "c"))

    (landed,) = _xchg_wait(small_send["handles"], grad_x, "arw_small")
    red = _sum_devices(landed, "sum_small").reshape(-1)
    shapes = small_send["shapes"]
    offs = np.cumsum([0] + [int(np.prod(sh)) for sh in shapes])
    r = [red[offs[i]:offs[i + 1]].reshape(shapes[i]) for i in range(len(shapes))]
    gs = {}
    gs["lb_logits"] = r[0]
    gs["hg_norm_g"] = r[1]
    gs["sg_ln_g"] = lax.dynamic_slice(r[2], (0, me * 256), (1, 256))
    gs["sg_ln_b"] = lax.dynamic_slice(r[3], (0, me * 256), (1, 256))
    gs["sg_w_s"] = r[4][None]
    gs["sg_b_s"] = r[5][None]
    gs["ffn_conv_w"] = lax.dynamic_slice(jnp.stack([r[6], r[7]]), (0, 0, me * 704), (DEPTH, 3, 704))
    gs["ffn_conv_b"] = jnp.concatenate([r[8], r[9]], axis=0)
    gs["ln1_g"] = jnp.concatenate([r[10], r[11]], axis=0)
    gs["ln1_b"] = jnp.concatenate([r[12], r[13]], axis=0)
    gs["ln2_g"] = jnp.concatenate([r[14], r[15]], axis=0)
    gs["ln2_b"] = jnp.concatenate([r[16], r[17]], axis=0)

    out_g, out_d, out_m, out_v = {}, {}, {}, {}
    for n in _SMALL:
        out_g[n], out_d[n], out_m[n], out_v[n] = _adamw(w[n], gs[n], m[n], v[n], f"adamw_{n}")

    res, after = {}, out_v["ln2_b"]
    for n in ("ffn_w_down1", "ffn_w_up1", "sg_w_out", "sg_w_in", "ffn_w_down0", "ffn_w_up0", "hg_w_out", "hg_w_in"):
        parts, recv = _xchg_wait(grad_sends[n], after, f"rsw_{n}")
        own = lax.dynamic_index_in_dim(parts, me, 0, keepdims=False)
        if n[-1] in "01":
            base, layer = n[:-1], int(n[-1])
            res[n] = _adamw(w[base], None, m[base], v[base], f"adamw_{n}", parts=recv, own=own, is_me=is_me,
                            layer=layer, prev=res.get(base + "1"))
        else:
            res[n] = _adamw(w[n][0], None, m[n][0], v[n][0], f"adamw_{n}", parts=recv, own=own, is_me=is_me)
        after = res[n][3]
    for n in ("hg_w_in", "hg_w_out", "sg_w_in", "sg_w_out"):
        out_g[n], out_d[n], out_m[n], out_v[n] = (a[None] for a in res[n])
    for n in ("ffn_w_up", "ffn_w_down"):
        out_g[n], out_d[n], out_m[n], out_v[n] = res[n + "0"]

    return (loss, grad_x[None], *[out_g[n] for n in _NAMES], *[out_d[n] for n in _NAMES],
            *[out_m[n] for n in _NAMES], *[out_v[n] for n in _NAMES])
```

```python
import functools

import numpy as np
import jax
import jax.numpy as jnp
from jax import lax
from jax.experimental import pallas as pl
from jax.experimental.pallas import tpu as pltpu

F32 = jnp.float32
BF16 = jnp.bfloat16
HI = lax.Precision.HIGHEST

N_DEV = 8
D_MODEL = 2048
HG_HEADS = 16
HG_DIM = 128
HG_CHUNK = 64
SG_GROUPS = 16
SG_DIM = 128
SG_CHUNK = 128
D_FF = 5632
DEPTH = 2
ALPHA = (2 * DEPTH) ** 0.25
LN_EPS = 1e-5
RMS_EPS = 1e-6
ADAM_LR = 0.001
ADAM_B1 = 0.9
ADAM_B2 = 0.999
ADAM_EPS = 1e-08
ADAM_WD = 0.01
ADAM_STEP = 10

VMEM_LIMIT = 56 * 1024 * 1024
MESH = pl.DeviceIdType.MESH
ANY = pl.BlockSpec(memory_space=pl.ANY)


def _cp(*sem):
    return pltpu.CompilerParams(dimension_semantics=sem, vmem_limit_bytes=VMEM_LIMIT)


def _pick(n, cands):
    for c in cands:
        if n % c == 0:
            return c
    raise ValueError(f"no tile for {n} in {cands}")


_DIMS = {"nn": (((1,), (0,)), ((), ())), "nt": (((1,), (1,)), ((), ())), "tn": (((0,), (0,)), ((), ()))}


MM_VMEM_BUDGET = 44 * 1024 * 1024


def _mm_tiles(m, n, kk, n_div, k_div, out_bytes, has_addend, k_piece, transposed_a):
    best = None
    for bm in (1024, 1408, 512, 256):
        if m % bm:
            continue
        for bn in (1024, 1408, 512, 256, 128):
            if n % bn or any(d % bn for d in n_div):
                continue
            for bk in (kk, 5632, 4096, 2816, 2048, 1408, 1024, 512, 256, 128):
                if bk > kk or kk % bk or any(d % bk for d in k_div):
                    continue
                if k_piece and bk % k_piece:
                    continue
                nk = kk // bk
                vmem = 4 * (bm * bk + bk * bn) + 2 * bm * bn * out_bytes + (nk > 1) * 4 * bm * bn
                vmem += has_addend * 8 * bm * bn + transposed_a * 2 * bm * bk
                vmem += (out_bytes < 4) * 4 * bm * bn
                if vmem > MM_VMEM_BUDGET:
                    continue
                score = (max(nk, 2), -(bm * bn), -bk)
                if best is None or score < best[0]:
                    best = (score, bm, bn, bk)
    if best is None:
        raise ValueError(f"no tiles for {(m, n, kk)}")
    return best[1:]


def _mm(a, b, *, mode, name, out_dtype=F32, a_cs=False, b_cs=False, o_cs=None, addend=None, deps=()):
    if mode == "tn":
        kk, m = a.shape
    elif a_cs:
        m, kk = a.shape[1], a.shape[0] * a.shape[2]
    else:
        m, kk = a.shape
    if mode == "nt":
        n = b.shape[1] if b_cs else b.shape[0]
    else:
        n = b.shape[0] * b.shape[2] if b_cs else b.shape[1]
    a_c = a.shape[2] if a_cs else None
    b_c = b.shape[2] if b_cs else None
    o_c = n // o_cs if o_cs else None
    k_piece = b_c if (b_cs and mode == "nt") else None
    n_div = [c for c in (b_c if (b_cs and mode != "nt") else None, o_c) if c]
    k_div = [a_c] if a_c else []
    bm, bn, bk = _mm_tiles(m, n, kk, n_div, k_div, jnp.dtype(out_dtype).itemsize, addend is not None, k_piece,
                           mode == "tn")
    nk = kk // bk
    kb = bk // k_piece if k_piece else 1

    def cs_idx(blk, per):
        return (blk * per[0]) // per[1], (blk * per[0] % per[1]) // per[0]

    if mode == "tn":
        a_spec = pl.BlockSpec((bk, bm), lambda i, j, k: (k, i))
    elif a_cs:
        def a_map(i, j, k):
            s, r = cs_idx(k, (bk, a_c))
            return (s, i, r)
        a_spec = pl.BlockSpec((None, bm, bk), a_map)
    else:
        a_spec = pl.BlockSpec((bm, bk), lambda i, j, k: (i, k))
    if mode == "nt":
        if b_cs:
            b_spec = pl.BlockSpec((kb, bn, b_c), lambda i, j, k: (k, j, 0))
        else:
            b_spec = pl.BlockSpec((bn, bk), lambda i, j, k: (j, k))
    else:
        if b_cs:
            def b_map(i, j, k):
                s, r = cs_idx(j, (bn, b_c))
                return (s, k, r)
            b_spec = pl.BlockSpec((None, bk, bn), b_map)
        else:
            b_spec = pl.BlockSpec((bk, bn), lambda i, j, k: (k, j))
    if o_cs:
        def o_map(i, j, k):
            s, r = cs_idx(j, (bn, o_c))
            return (s, i, r)
        o_spec = pl.BlockSpec((None, bm, bn), o_map)
        out_shape = jax.ShapeDtypeStruct((o_cs, m, o_c), out_dtype)
    else:
        o_spec = pl.BlockSpec((bm, bn), lambda i, j, k: (i, j))
        out_shape = jax.ShapeDtypeStruct((m, n), out_dtype)
    in_specs = [a_spec, b_spec]
    args = [a, b]
    if addend is not None:
        in_specs.append(pl.BlockSpec((bm, bn), lambda i, j, k: (i, j)))
        args.append(addend)
    dims = _DIMS[mode]
    in_specs += [ANY] * len(deps)
    args += list(deps)
    n_in = len(args)

    def body(*refs):
        a_ref, b_ref = refs[0], refs[1]
        add_ref = refs[2] if addend is not None else None
        o_ref = refs[n_in]
        if k_piece:
            d = None
            for p in range(kb):
                dp = lax.dot_general(a_ref[:, p * b_c:(p + 1) * b_c].astype(BF16), b_ref[p].astype(BF16), dims,
                                     preferred_element_type=F32)
                d = dp if d is None else d + dp
        else:
            d = lax.dot_general(a_ref[...].astype(BF16), b_ref[...].astype(BF16), dims, preferred_element_type=F32)

        def finish(r):
            if addend is not None:
                r = r + add_ref[...]
            o_ref[...] = r.astype(o_ref.dtype)

        if nk == 1:
            finish(d)
            return
        acc_ref = refs[n_in + 1]
        k = pl.program_id(2)

        @pl.when(k == 0)
        def _():
            acc_ref[...] = d

        if nk > 2:
            @pl.when((k > 0) & (k < nk - 1))
            def _():
                acc_ref[...] += d

        @pl.when(k == nk - 1)
        def _():
            finish(acc_ref[...] + d)

    return pl.pallas_call(
        body, name=name, out_shape=out_shape, grid=(m // bm, n // bn, nk),
        in_specs=in_specs, out_specs=o_spec,
        scratch_shapes=[pltpu.VMEM((bm, bn), F32)] if nk > 1 else [],
        compiler_params=_cp("parallel", "parallel", "arbitrary"),
    )(*args)


LN_ROWS = 256


def _ln_stats(s):
    mu = jnp.mean(s, axis=-1, keepdims=True)
    sc = s - mu
    var = jnp.mean(sc * sc, axis=-1, keepdims=True)
    rstd = lax.rsqrt(var + LN_EPS)
    return sc * rstd, rstd


def _ln_fwd(h, sub, g, b, name):
    t, d = h.shape
    row = pl.BlockSpec((LN_ROWS, d), lambda i: (i, 0))
    vec = pl.BlockSpec((1, d), lambda i: (0, 0))

    def body(h_ref, s_ref, g_ref, b_ref, y_ref, yb_ref):
        xhat, _ = _ln_stats(ALPHA * h_ref[...] + s_ref[...])
        y = xhat * g_ref[...] + b_ref[...]
        y_ref[...] = y
        yb_ref[...] = y.astype(BF16)

    return pl.pallas_call(
        body, name=name, grid=(t // LN_ROWS,),
        out_shape=(jax.ShapeDtypeStruct((t, d), F32), jax.ShapeDtypeStruct((t, d), BF16)),
        in_specs=[row, row, vec, vec], out_specs=(row, row), compiler_params=_cp("parallel"),
    )(h, sub, g, b)


def _ln_bwd_math(xhat, rstd, dy, g):
    dxhat = dy * g
    m1 = jnp.mean(dxhat, axis=-1, keepdims=True)
    m2 = jnp.mean(dxhat * xhat, axis=-1, keepdims=True)
    ds = rstd * (dxhat - m1 - xhat * m2)
    dg = jnp.sum(dy * xhat, axis=0, keepdims=True)
    db = jnp.sum(dy, axis=0, keepdims=True)
    return ds, dg, db


def _ln_bwd(h, sub, dy_a, dy_b, g, name):
    t, d = h.shape
    row = pl.BlockSpec((LN_ROWS, d), lambda i: (i, 0))
    vec = pl.BlockSpec((1, d), lambda i: (0, 0))

    def body(h_ref, s_ref, da_ref, db_ref, g_ref, ds_ref, dres_ref, dg_ref, dbeta_ref):
        xhat, rstd = _ln_stats(ALPHA * h_ref[...] + s_ref[...])
        ds, dg, db = _ln_bwd_math(xhat, rstd, da_ref[...] + db_ref[...], g_ref[...])
        ds_ref[...] = ds.astype(BF16)
        dres_ref[...] = ALPHA * ds

        @pl.when(pl.program_id(0) == 0)
        def _():
            dg_ref[...] = jnp.zeros_like(dg_ref)
            dbeta_ref[...] = jnp.zeros_like(dbeta_ref)

        dg_ref[...] += dg
        dbeta_ref[...] += db

    return pl.pallas_call(
        body, name=name, grid=(t // LN_ROWS,),
        out_shape=(jax.ShapeDtypeStruct((t, d), BF16), jax.ShapeDtypeStruct((t, d), F32),
                   jax.ShapeDtypeStruct((1, d), F32), jax.ShapeDtypeStruct((1, d), F32)),
        in_specs=[row, row, row, row, vec], out_specs=(row, row, vec, vec),
        compiler_params=_cp("arbitrary"),
    )(h, sub, dy_a, dy_b, g)


def _ln_loss_bwd(h, sub, target, g, b, name):
    t, d = h.shape
    row = pl.BlockSpec((LN_ROWS, d), lambda i: (i, 0))
    vec = pl.BlockSpec((1, d), lambda i: (0, 0))
    lvec = pl.BlockSpec((1, 128), lambda i: (0, 0))

    def body(h_ref, s_ref, t_ref, g_ref, b_ref, loss_ref, ds_ref, dres_ref, dg_ref, dbeta_ref):
        xhat, rstd = _ln_stats(ALPHA * h_ref[...] + s_ref[...])
        y = xhat * g_ref[...] + b_ref[...]
        err = y - t_ref[...]
        part = 0.5 * jnp.sum(jnp.mean(err * err, axis=-1, keepdims=True), axis=0, keepdims=True)
        ds, dg, db = _ln_bwd_math(xhat, rstd, err * (1.0 / d), g_ref[...])
        ds_ref[...] = ds.astype(BF16)
        dres_ref[...] = ALPHA * ds

        @pl.when(pl.program_id(0) == 0)
        def _():
            loss_ref[...] = jnp.zeros_like(loss_ref)
            dg_ref[...] = jnp.zeros_like(dg_ref)
            dbeta_ref[...] = jnp.zeros_like(dbeta_ref)

        loss_ref[...] += jnp.broadcast_to(part, loss_ref.shape)
        dg_ref[...] += dg
        dbeta_ref[...] += db

    return pl.pallas_call(
        body, name=name, grid=(t // LN_ROWS,),
        out_shape=(jax.ShapeDtypeStruct((1, 128), F32), jax.ShapeDtypeStruct((t, d), BF16),
                   jax.ShapeDtypeStruct((t, d), F32), jax.ShapeDtypeStruct((1, d), F32),
                   jax.ShapeDtypeStruct((1, d), F32)),
        in_specs=[row, row, row, vec, vec], out_specs=(lvec, row, row, vec, vec),
        compiler_params=_cp("arbitrary"),
    )(h, sub, target, g, b)


HG_ROWS = 512
HG_LEVELS = 6


def _hg_constants():
    c = HG_CHUNK
    t = np.arange(c)[:, None]
    j = np.arange(c)[None, :]
    blocks, masks = [], [np.eye(c)]
    for lev in range(HG_LEVELS):
        m = 1 << lev
        second = (t % (2 * m)) >= m
        mid = (t // (2 * m)) * (2 * m) + m - 1
        blocks.append((second & (j > mid) & (j <= t)) | ((~second) & (j > t) & (j <= mid)))
        same = (t // (2 * m)) == (j // (2 * m))
        masks.append(same & second & ((j % (2 * m)) < m))
    blocks += [j <= t, j > t, np.ones((c, c), bool)]
    cm = np.concatenate(blocks, axis=0).astype(np.float32)
    mk = np.concatenate(masks, axis=0).astype(np.float32)
    return jnp.asarray(cm, dtype=BF16), jnp.asarray(mk)


def _split3(x):
    hi = x.astype(BF16)
    r = x - hi.astype(F32)
    mid = r.astype(BF16)
    lo = (r - mid.astype(F32)).astype(BF16)
    return hi, mid, lo


def _sum3(x):
    n = x.shape[-1] // 3
    return x[:, :n] + x[:, n:2 * n] + x[:, 2 * n:]


@jax.custom_vjp
def _prefix(cm, lf):
    return _sum3(lax.dot_general(cm, jnp.concatenate(_split3(lf), axis=1), _DIMS["nn"], preferred_element_type=F32))


def _prefix_fwd(cm, lf):
    return _prefix(cm, lf), cm


def _prefix_bwd(cm, d):
    hi = d.astype(BF16)
    lo = (d - hi.astype(F32)).astype(BF16)
    both = lax.dot_general(cm, jnp.concatenate([hi, lo], axis=1), _DIMS["tn"], preferred_element_type=F32)
    n = d.shape[-1]
    return jnp.zeros_like(cm), both[:, :n] + both[:, n:]


_prefix.defvjp(_prefix_fwd, _prefix_bwd)

_BWD = {"nn": (("nt", 0, 1), ("tn", 1, 0)), "nt": (("nn", 0, 1), ("tn", 0, 1)), "tn": (("nt", 1, 0), ("nn", 1, 0))}


def _bdot_raw(a, b, mode):
    return lax.dot_general(a.astype(BF16), b.astype(BF16), _DIMS[mode], preferred_element_type=F32)


@functools.partial(jax.custom_vjp, nondiff_argnums=(2,))
def _bdot(a, b, mode):
    return _bdot_raw(a, b, mode)


def _bdot_fwd(a, b, mode):
    return _bdot_raw(a, b, mode), (a, b)


def _bdot_bwd(mode, res, d):
    a, b = res
    (ma, da_pos, _), (mb, db_pos, _) = _BWD[mode]
    da = _bdot_raw(d, b, ma) if da_pos == 0 else _bdot_raw(b, d, ma)
    db = _bdot_raw(d, a, mb) if db_pos == 0 else _bdot_raw(a, d, mb)
    return da, db


_bdot.defvjp(_bdot_fwd, _bdot_bwd)


def _hg_chunk(pq, pf, pi, pg, l0, l1, l2, ng, s_t, cm, mk):
    c = HG_CHUNK
    mx = jnp.maximum(jnp.maximum(l0, l1), l2)
    e0, e1, e2 = jnp.exp(l0 - mx), jnp.exp(l1 - mx), jnp.exp(l2 - mx)
    lb = e0 / (e0 + e1 + e2)
    q = pq * jax.nn.sigmoid(pq)
    t1 = jnp.log(lb)
    t2 = jnp.log1p(-lb) + jax.nn.log_sigmoid(pf)
    lf = jnp.maximum(t1, t2) + jnp.log1p(jnp.exp(-jnp.abs(t1 - t2)))
    k = (1.0 - lb) * jax.nn.sigmoid(-pf)
    x = jnp.exp(_prefix(cm, lf))
    scores = mk[0:c] * _bdot(q, k, "nt")
    for lev in range(HG_LEVELS):
        xl = x[lev * c:(lev + 1) * c]
        scores = scores + mk[(lev + 1) * c:(lev + 2) * c] * _bdot(q * xl, k * xl, "nt")
    x_incl = x[HG_LEVELS * c:(HG_LEVELS + 1) * c]
    x_after = x[(HG_LEVELS + 1) * c:(HG_LEVELS + 2) * c]
    x_total = x[(HG_LEVELS + 2) * c:(HG_LEVELS + 3) * c]
    o = _bdot(scores, pi, "nn") + _bdot(q * x_incl, s_t, "nt")
    s_new = s_t * jnp.concatenate([x_total, x_total], axis=0) + _bdot(pi, k * x_after, "tn")
    rstd = lax.rsqrt(jnp.mean(o * o, axis=-1, keepdims=True) + RMS_EPS)
    y = o * rstd * ng * (pg * jax.nn.sigmoid(pg))
    return y, s_new


def _hg_specs(t, reverse):
    nrb = t // HG_ROWS
    rb = (lambda r: nrb - 1 - r) if reverse else (lambda r: r)
    proj = pl.BlockSpec((4, HG_ROWS, HG_DIM), lambda h, r: (0, rb(r), h))
    vec = pl.BlockSpec((1, HG_DIM), lambda h, r: (0, h))
    cm = pl.BlockSpec(((HG_LEVELS + 3) * HG_CHUNK, HG_CHUNK), lambda h, r: (0, 0))
    mk = pl.BlockSpec(((HG_LEVELS + 1) * HG_CHUNK, HG_CHUNK), lambda h, r: (0, 0))
    rows = pl.BlockSpec((HG_ROWS, HG_DIM), lambda h, r: (rb(r), h))
    states = pl.BlockSpec((None, HG_ROWS // HG_CHUNK, HG_DIM, HG_DIM), lambda h, r: (h, rb(r), 0, 0))
    return proj, vec, cm, mk, rows, states


def _hg_fwd(proj, l0, l1, l2, ng):
    t = proj.shape[1]
    n_in = HG_ROWS // HG_CHUNK
    cm, mk = _hg_constants()
    p_spec, vec, cm_spec, mk_spec, rows, st_spec = _hg_specs(t, False)

    def body(p_ref, l0_ref, l1_ref, l2_ref, ng_ref, cm_ref, mk_ref, y_ref, st_ref, s_ref):
        @pl.when(pl.program_id(1) == 0)
        def _():
            s_ref[...] = jnp.zeros_like(s_ref)

        def step(ci, carry):
            sl = pl.ds(pl.multiple_of(ci * HG_CHUNK, HG_CHUNK), HG_CHUNK)
            s_t = s_ref[...]
            st_ref[ci] = s_t
            y, s_new = _hg_chunk(p_ref[0, sl, :], p_ref[1, sl, :], p_ref[2, sl, :], p_ref[3, sl, :],
                                 l0_ref[...], l1_ref[...], l2_ref[...], ng_ref[...], s_t,
                                 cm_ref[...], mk_ref[...])
            y_ref[sl, :] = y.astype(BF16)
            s_ref[...] = s_new
            return carry

        lax.fori_loop(0, n_in, step, 0, unroll=8)

    return pl.pallas_call(
        body, name="hg_fwd", grid=(HG_HEADS, t // HG_ROWS),
        out_shape=(jax.ShapeDtypeStruct((t, D_MODEL), BF16),
                   jax.ShapeDtypeStruct((HG_HEADS, t // HG_CHUNK, HG_DIM, HG_DIM), F32)),
        in_specs=[p_spec, vec, vec, vec, vec, cm_spec, mk_spec], out_specs=(rows, st_spec),
        scratch_shapes=[pltpu.VMEM((HG_DIM, HG_DIM), F32)],
        compiler_params=_cp("parallel", "arbitrary"),
    )(proj, l0, l1, l2, ng, cm, mk)


def _hg_bwd(proj, states, dy, l0, l1, l2, ng):
    t = proj.shape[1]
    n_in = HG_ROWS // HG_CHUNK
    cm, mk = _hg_constants()
    p_spec, vec, cm_spec, mk_spec, rows, st_spec = _hg_specs(t, True)

    def body(p_ref, st_ref, dy_ref, l0_ref, l1_ref, l2_ref, ng_ref, cm_ref, mk_ref,
             dp_ref, dl0_ref, dl1_ref, dl2_ref, dng_ref, ds_ref):
        @pl.when(pl.program_id(1) == 0)
        def _():
            ds_ref[...] = jnp.zeros_like(ds_ref)
            for r in (dl0_ref, dl1_ref, dl2_ref, dng_ref):
                r[...] = jnp.zeros_like(r)

        def step(it, carry):
            ci = n_in - 1 - it
            sl = pl.ds(pl.multiple_of(ci * HG_CHUNK, HG_CHUNK), HG_CHUNK)
            fn = functools.partial(_hg_chunk, cm=cm_ref[...], mk=mk_ref[...])
            _, vjp = jax.vjp(fn, p_ref[0, sl, :], p_ref[1, sl, :], p_ref[2, sl, :], p_ref[3, sl, :],
                             l0_ref[...], l1_ref[...], l2_ref[...], ng_ref[...], st_ref[ci])
            dq, df, di, dg, d0, d1, d2, dn, ds = vjp((dy_ref[sl, :], ds_ref[...]))
            dp_ref[0, sl, :] = dq.astype(BF16)
            dp_ref[1, sl, :] = df.astype(BF16)
            dp_ref[2, sl, :] = di.astype(BF16)
            dp_ref[3, sl, :] = dg.astype(BF16)
            dl0_ref[...] += d0
            dl1_ref[...] += d1
            dl2_ref[...] += d2
            dng_ref[...] += dn
            ds_ref[...] = ds
            return carry

        lax.fori_loop(0, n_in, step, 0, unroll=8)

    v_shape = jax.ShapeDtypeStruct((1, D_MODEL), F32)
    return pl.pallas_call(
        body, name="hg_bwd", grid=(HG_HEADS, t // HG_ROWS),
        out_shape=(jax.ShapeDtypeStruct((4, t, D_MODEL), BF16), v_shape, v_shape, v_shape, v_shape),
        in_specs=[p_spec, st_spec, rows, vec, vec, vec, vec, cm_spec, mk_spec],
        out_specs=(p_spec, vec, vec, vec, vec),
        scratch_shapes=[pltpu.VMEM((HG_DIM, HG_DIM), F32)],
        compiler_params=_cp("parallel", "arbitrary"),
    )(proj, states, dy, l0, l1, l2, ng, cm, mk)


_SQRT_HALF = 0.7071067811865476
_INV_SQRT_2PI = 0.3989422804014327


def _gelu(x):
    return 0.5 * x * (1.0 + lax.erf(x * _SQRT_HALF))


def _gelu_grad(x):
    return 0.5 * (1.0 + lax.erf(x * _SQRT_HALF)) + x * (_INV_SQRT_2PI * jnp.exp(-0.5 * x * x))


def _tril(n):
    return (lax.broadcasted_iota(jnp.int32, (n, n), 0) >= lax.broadcasted_iota(jnp.int32, (n, n), 1)).astype(F32)


def _sg_specs(d):
    row = lambda w: pl.BlockSpec((SG_CHUNK, w), lambda i: (i, 0))
    vec = pl.BlockSpec((1, d), lambda i: (0, 0))
    cube = pl.BlockSpec((SG_GROUPS, SG_CHUNK, SG_CHUNK), lambda i: (0, 0, 0))
    return row, vec, cube


def _sg_fwd(pre, ln_g, ln_b, w_s, bias):
    t = pre.shape[0]
    d = D_MODEL
    row, vec, cube = _sg_specs(d)

    def body(pre_ref, g_ref, b_ref, w_ref, bias_ref, y_ref, vln_ref):
        u = _gelu(pre_ref[:, :d])
        vhat, _ = _ln_stats(_gelu(pre_ref[:, d:]))
        vln_ref[...] = vhat * g_ref[...] + b_ref[...]
        tril = _tril(SG_CHUNK)
        for g in range(SG_GROUPS):
            cs = slice(g * SG_DIM, (g + 1) * SG_DIM)
            gate = _bdot_raw(w_ref[g] * tril, vln_ref[:, cs], "nn") + bias_ref[g]
            y_ref[:, cs] = (u[:, cs] * gate).astype(BF16)

    return pl.pallas_call(
        body, name="sg_fwd", grid=(t // SG_CHUNK,), out_shape=jax.ShapeDtypeStruct((t, d), BF16),
        in_specs=[row(2 * d), vec, vec, cube, cube], out_specs=row(d),
        scratch_shapes=[pltpu.VMEM((SG_CHUNK, d), F32)], compiler_params=_cp("parallel"),
    )(pre, ln_g, ln_b, w_s, bias)


def _sg_bwd(pre, dy, ln_g, ln_b, w_s, bias):
    t = pre.shape[0]
    d = D_MODEL
    row, vec, cube = _sg_specs(d)
    dbs_spec = pl.BlockSpec((SG_GROUPS, 8, SG_CHUNK), lambda i: (0, 0, 0))

    def body(pre_ref, dy_ref, g_ref, b_ref, w_ref, bias_ref, dpre_ref, dw_ref, dbs_ref, dlg_ref, dlb_ref,
             vln_ref, dvln_ref):
        @pl.when(pl.program_id(0) == 0)
        def _():
            for r in (dw_ref, dbs_ref, dlg_ref, dlb_ref):
                r[...] = jnp.zeros_like(r)

        pu = pre_ref[:, :d]
        pv = pre_ref[:, d:]
        u = _gelu(pu)
        vhat, rstd = _ln_stats(_gelu(pv))
        vln_ref[...] = vhat * g_ref[...] + b_ref[...]
        tril = _tril(SG_CHUNK)
        ones = jnp.ones((8, SG_DIM), F32)
        for g in range(SG_GROUPS):
            cs = slice(g * SG_DIM, (g + 1) * SG_DIM)
            wc = w_ref[g] * tril
            vg = vln_ref[:, cs]
            gate = _bdot_raw(wc, vg, "nn") + bias_ref[g]
            dyg = dy_ref[:, cs]
            dgate = dyg * u[:, cs]
            dpre_ref[:, cs] = (dyg * gate * _gelu_grad(pu[:, cs])).astype(BF16)
            dw_ref[g] += tril * _bdot_raw(dgate, vg, "nt")
            dbs_ref[g] += lax.dot_general(ones, dgate, _DIMS["nt"], precision=HI, preferred_element_type=F32)
            dvln_ref[:, cs] = _bdot_raw(wc, dgate, "tn")
        dvln = dvln_ref[...]
        dv, dg, db = _ln_bwd_math(vhat, rstd, dvln, g_ref[...])
        dlg_ref[...] += dg
        dlb_ref[...] += db
        dpre_ref[:, d:] = (dv * _gelu_grad(pv)).astype(BF16)

    return pl.pallas_call(
        body, name="sg_bwd", grid=(t // SG_CHUNK,),
        out_shape=(jax.ShapeDtypeStruct((t, 2 * d), BF16), jax.ShapeDtypeStruct(w_s.shape, F32),
                   jax.ShapeDtypeStruct((SG_GROUPS, 8, SG_CHUNK), F32),
                   jax.ShapeDtypeStruct((1, d), F32), jax.ShapeDtypeStruct((1, d), F32)),
        in_specs=[row(2 * d), row(d), vec, vec, cube, cube],
        out_specs=(row(2 * d), cube, dbs_spec, vec, vec),
        scratch_shapes=[pltpu.VMEM((SG_CHUNK, d), F32), pltpu.VMEM((SG_CHUNK, d), F32)],
        compiler_params=_cp("arbitrary"),
    )(pre, dy, ln_g, ln_b, w_s, bias)


FFN_ROWS = 256
FFN_COLS = 1408
HALO = 16


def _ffn_conv(a_ext, w_ref, cb_ref):
    a1 = pltpu.roll(a_ext, 1, 0)
    a2 = pltpu.roll(a_ext, 2, 0)
    return w_ref[0:1, :] * a2 + w_ref[1:2, :] * a1 + w_ref[2:3, :] * a_ext + cb_ref[...], a1, a2


def _ffn_fwd(hh, conv_w, conv_b):
    t = hh.shape[1]
    nb8 = FFN_ROWS // HALO
    main = pl.BlockSpec((2, FFN_ROWS, FFN_COLS), lambda c, r: (0, r, c))
    prev = pl.BlockSpec((None, HALO, FFN_COLS), lambda c, r: (0, jnp.maximum(r * nb8 - 1, 0), c))
    wspec = pl.BlockSpec((3, FFN_COLS), lambda c, r: (0, c))
    bspec = pl.BlockSpec((1, FFN_COLS), lambda c, r: (0, c))

    def body(m_ref, p_ref, w_ref, cb_ref, z_ref):
        prev = jnp.where(pl.program_id(1) == 0, 0.0, p_ref[...].astype(F32))
        a_ext = jnp.concatenate([prev, m_ref[0].astype(F32)], axis=0)
        a, _, _ = _ffn_conv(a_ext, w_ref, cb_ref)
        a = a[HALO:]
        z_ref[...] = (a * jax.nn.sigmoid(a) * m_ref[1].astype(F32)).astype(BF16)

    return pl.pallas_call(
        body, name="ffn_fwd", grid=(D_FF // FFN_COLS, t // FFN_ROWS),
        out_shape=jax.ShapeDtypeStruct((t, D_FF), BF16),
        in_specs=[main, prev, wspec, bspec],
        out_specs=pl.BlockSpec((FFN_ROWS, FFN_COLS), lambda c, r: (r, c)),
        compiler_params=_cp("parallel", "parallel"),
    )(hh, hh, conv_w, conv_b)


def _ffn_bwd(hh, dz, conv_w, conv_b):
    t = hh.shape[1]
    nb8 = FFN_ROWS // HALO
    last8 = t // HALO - 1
    nr = t // FFN_ROWS
    main = pl.BlockSpec((2, FFN_ROWS, FFN_COLS), lambda c, r: (0, r, c))
    prev = pl.BlockSpec((None, HALO, FFN_COLS), lambda c, r: (0, jnp.maximum(r * nb8 - 1, 0), c))
    nxt = pl.BlockSpec((2, HALO, FFN_COLS), lambda c, r: (0, jnp.minimum((r + 1) * nb8, last8), c))
    dmain = pl.BlockSpec((FFN_ROWS, FFN_COLS), lambda c, r: (r, c))
    dnxt = pl.BlockSpec((HALO, FFN_COLS), lambda c, r: (jnp.minimum((r + 1) * nb8, last8), c))
    wspec = pl.BlockSpec((3, FFN_COLS), lambda c, r: (0, c))
    bspec = pl.BlockSpec((1, FFN_COLS), lambda c, r: (0, c))

    def body(m_ref, p_ref, n_ref, dz_ref, dzn_ref, w_ref, cb_ref, dh_ref, dw_ref, dcb_ref):
        r = pl.program_id(1)

        @pl.when(r == 0)
        def _():
            dw_ref[...] = jnp.zeros_like(dw_ref)
            dcb_ref[...] = jnp.zeros_like(dcb_ref)

        prev = jnp.where(r == 0, 0.0, p_ref[...].astype(F32))
        a_ext = jnp.concatenate([prev, m_ref[0].astype(F32), n_ref[0].astype(F32)], axis=0)
        a, a1, a2 = _ffn_conv(a_ext, w_ref, cb_ref)
        a = a[HALO:]
        b_ext = jnp.concatenate([m_ref[1], n_ref[1]], axis=0).astype(F32)
        dz_main = dz_ref[...].astype(F32)
        dz_ext = jnp.concatenate([dz_main, jnp.where(r == nr - 1, 0.0, dzn_ref[...].astype(F32))], axis=0)
        sig = jax.nn.sigmoid(a)
        da = dz_ext * b_ext * (sig * (1.0 + a * (1.0 - sig)))
        n_ext = FFN_ROWS + HALO
        da_p1 = pltpu.roll(da, n_ext - 1, 0)
        da_p2 = pltpu.roll(da, n_ext - 2, 0)
        da_raw = w_ref[2:3, :] * da + w_ref[1:2, :] * da_p1 + w_ref[0:1, :] * da_p2
        dh_ref[0] = da_raw[:FFN_ROWS].astype(BF16)
        dh_ref[1] = (dz_main * (a * sig)[:FFN_ROWS]).astype(BF16)
        dam = da[:FFN_ROWS]
        rows = slice(HALO, HALO + FFN_ROWS)
        dw_ref[0:1, :] += jnp.sum(dam * a2[rows], axis=0, keepdims=True)
        dw_ref[1:2, :] += jnp.sum(dam * a1[rows], axis=0, keepdims=True)
        dw_ref[2:3, :] += jnp.sum(dam * a_ext[rows], axis=0, keepdims=True)
        dcb_ref[...] += jnp.sum(dam, axis=0, keepdims=True)

    return pl.pallas_call(
        body, name="ffn_bwd", grid=(D_FF // FFN_COLS, nr),
        out_shape=(jax.ShapeDtypeStruct((2, t, D_FF), BF16), jax.ShapeDtypeStruct((3, D_FF), F32),
                   jax.ShapeDtypeStruct((1, D_FF), F32)),
        in_specs=[main, prev, nxt, dmain, dnxt, wspec, bspec],
        out_specs=(main, wspec, bspec), compiler_params=_cp("parallel", "arbitrary"),
    )(hh, hh, hh, dz, dz, conv_w, conv_b)


def _adamw_math(w, g, m, v):
    m = ADAM_B1 * m + (1.0 - ADAM_B1) * g
    v = ADAM_B2 * v + (1.0 - ADAM_B2) * (g * g)
    m_hat = m / (1.0 - ADAM_B1 ** ADAM_STEP)
    v_hat = v / (1.0 - ADAM_B2 ** ADAM_STEP)
    delta = -ADAM_LR * (m_hat / (jnp.sqrt(v_hat) + ADAM_EPS) + ADAM_WD * w)
    return delta, m, v


def _as2d(shape):
    n = int(np.prod(shape))
    c = shape[-1] if shape[-1] % 128 == 0 else (128 if n % 128 == 0 else shape[-1])
    return n // c, c


def _adamw(w, g, m, v, name, parts=None, own=None, is_me=None, layer=None, prev=None):
    shape = w.shape if layer is None else w.shape[1:]
    r, c = _as2d(shape)
    br = r if r <= 512 else _pick(r, (256, 176, 128, 64, 8))
    blk = pl.BlockSpec((br, c), lambda i: (i, 0))
    if layer is None:
        wblk, o = blk, jax.ShapeDtypeStruct((r, c), F32)
        w2, m2, v2 = (a.reshape(r, c) for a in (w, m, v))
    else:
        wblk, o = pl.BlockSpec((None, br, c), lambda i: (layer, i, 0)), jax.ShapeDtypeStruct((w.shape[0], r, c), F32)
        w2, m2, v2 = (a.reshape(w.shape[0], r, c) for a in (w, m, v))
    extra, extra_specs = [], []
    if parts is not None:
        g_in = parts.reshape(N_DEV, r, c)
        g_spec = pl.BlockSpec((N_DEV, br, c), lambda i: (0, i, 0))
        extra = [own.reshape(r, c), is_me]
        extra_specs = [blk, pl.BlockSpec(memory_space=pltpu.SMEM)]
    else:
        g_in = g.reshape(r, c)
        g_spec = blk
    aliases = {}
    if prev is not None:
        aliases = {4 + len(extra) + j: j for j in range(4)}
        extra = extra + [a.reshape(o.shape) for a in prev]
        extra_specs = extra_specs + [ANY] * 4

    def body(w_ref, g_ref, m_ref, v_ref, *rest):
        go_ref, d_ref, mo_ref, vo_ref = rest[-4:]
        if parts is not None:
            own_ref, me_ref = rest[:2]
            gsum = None
            for i in range(N_DEV):
                term = jnp.where(me_ref[i] > 0.5, own_ref[...], g_ref[i]).astype(F32)
                gsum = term if gsum is None else gsum + term
        else:
            gsum = g_ref[...]
        delta, m_new, v_new = _adamw_math(w_ref[...], gsum, m_ref[...], v_ref[...])
        go_ref[...] = gsum
        d_ref[...] = delta
        mo_ref[...] = m_new
        vo_ref[...] = v_new

    outs = pl.pallas_call(
        body, name=name, grid=(r // br,), out_shape=(o, o, o, o),
        in_specs=[wblk, g_spec, wblk, wblk] + extra_specs, out_specs=(wblk, wblk, wblk, wblk),
        input_output_aliases=aliases, compiler_params=_cp("parallel"),
    )(w2, g_in, m2, v2, *extra)
    return tuple(a.reshape(w.shape) for a in outs)


ALL_PEERS = tuple(range(1, N_DEV))
OTHER_CHIPS = (2, 4, 6)
SIBLING = (1,)


def _me_and_peers(relations=ALL_PEERS):
    x, y, c = lax.axis_index("x"), lax.axis_index("y"), lax.axis_index("c")
    me = 4 * x + 2 * y + c
    peers = []
    for k in relations:
        kx, ky, kc = (k >> 2) & 1, (k >> 1) & 1, k & 1
        px, py, pc = x ^ kx, y ^ ky, c ^ kc
        peers.append(((px, py, pc), 4 * px + 2 * py + pc))
    return me, peers


def _all_gather(shard, name):
    def body(x_ref, o_ref, send_sems, recv_sems, local_sem):
        me, peers = _me_and_peers()
        mine = pltpu.make_async_copy(x_ref, o_ref.at[me], local_sem)
        mine.start()
        sends = []
        for k, (dev, _) in enumerate(peers):
            cp = pltpu.make_async_remote_copy(src_ref=x_ref, dst_ref=o_ref.at[me], send_sem=send_sems.at[k],
                                              recv_sem=recv_sems.at[k], device_id=dev, device_id_type=MESH)
            cp.start()
            sends.append(cp)
        for k, (dev, idx) in enumerate(peers):
            pltpu.make_async_remote_copy(src_ref=x_ref, dst_ref=o_ref.at[idx], send_sem=send_sems.at[k],
                                         recv_sem=recv_sems.at[k], device_id=dev, device_id_type=MESH).wait_recv()
        for cp in sends:
            cp.wait_send()
        mine.wait()

    return pl.pallas_call(
        body, name=name, out_shape=jax.ShapeDtypeStruct((N_DEV,) + shard.shape, shard.dtype),
        in_specs=[ANY], out_specs=ANY,
        scratch_shapes=[pltpu.SemaphoreType.DMA((N_DEV - 1,)), pltpu.SemaphoreType.DMA((N_DEV - 1,)),
                        pltpu.SemaphoreType.DMA],
        compiler_params=pltpu.CompilerParams(has_side_effects=True),
    )(shard)


HBM = pl.BlockSpec(memory_space=pltpu.HBM)
SEM = pl.BlockSpec(memory_space=pltpu.SEMAPHORE)
EFFECT = pltpu.SideEffectType.DATAFLOW_SIDE_EFFECTING


def _xchg_start(parts, land, *, dep, name, peers=ALL_PEERS, pieces=(0,)):
    n_buf = 1 if parts is None else 2

    def body(*refs):
        land_ref = refs[n_buf - 1]
        send_sem, recv_sem = refs[n_buf + 1], refs[n_buf + 2]
        token = refs[2 * n_buf + 3]
        me, to = _me_and_peers(peers)
        for dev, idx in to:
            if parts is None:
                for r in pieces:
                    piece = land_ref.at[me ^ r]
                    pltpu.make_async_remote_copy(src_ref=piece, dst_ref=piece, send_sem=send_sem, recv_sem=recv_sem,
                                                 device_id=dev, device_id_type=MESH).start()
            else:
                pltpu.make_async_remote_copy(src_ref=refs[0].at[idx], dst_ref=land_ref.at[me], send_sem=send_sem,
                                             recv_sem=recv_sem, device_id=dev, device_id_type=MESH).start()
        token[...] = jnp.zeros_like(token)

    bufs = [land] if parts is None else [parts, land]
    return pl.pallas_call(
        body, name=name,
        out_shape=(pltpu.SemaphoreType.DMA(()), pltpu.SemaphoreType.DMA(()),
                   *[pltpu.HBM(b.shape, b.dtype) for b in bufs], jax.ShapeDtypeStruct((8, 128), F32)),
        in_specs=(*[HBM] * n_buf, ANY), out_specs=(SEM, SEM, *[HBM] * n_buf, pl.BlockSpec(memory_space=pltpu.VMEM)),
        input_output_aliases={i: 2 + i for i in range(n_buf)},
        compiler_params=pltpu.CompilerParams(has_side_effects=EFFECT),
    )(*[pltpu.with_memory_space_constraint(b, pltpu.HBM) for b in bufs], dep)


def _xchg_wait(handles, after, name, n_pieces=N_DEV - 1):
    send_sem, recv_sem, *bufs, _ = handles
    n_buf = len(bufs)

    def body(*refs):
        land_ref = refs[n_buf - 1]
        send_sem, recv_sem = refs[n_buf], refs[n_buf + 1]
        x, y, c = lax.axis_index("x"), lax.axis_index("y"), lax.axis_index("c")
        span = land_ref.at[pl.ds(0, n_pieces)]
        cp = pltpu.make_async_remote_copy(src_ref=span, dst_ref=span, send_sem=send_sem, recv_sem=recv_sem,
                                          device_id=(x, y, c), device_id_type=MESH)
        cp.wait_send()
        cp.wait_recv()

    return pl.pallas_call(
        body, name=name, out_shape=tuple(pltpu.HBM(b.shape, b.dtype) for b in bufs),
        in_specs=(*[HBM] * n_buf, SEM, SEM, ANY), out_specs=tuple([HBM] * n_buf),
        input_output_aliases={i: i for i in range(n_buf)},
        compiler_params=pltpu.CompilerParams(has_side_effects=EFFECT),
    )(*bufs, send_sem, recv_sem, after)


def _sum_devices(v, name):
    _, r, c = v.shape

    def body(x_ref, o_ref):
        acc = x_ref[0]
        for i in range(1, N_DEV):
            acc = acc + x_ref[i]
        o_ref[...] = acc

    return pl.pallas_call(
        body, name=name, out_shape=jax.ShapeDtypeStruct((r, c), F32),
        in_specs=[pl.BlockSpec(memory_space=pltpu.VMEM)], out_specs=pl.BlockSpec(memory_space=pltpu.VMEM),
        compiler_params=pltpu.CompilerParams(vmem_limit_bytes=VMEM_LIMIT),
    )(v)


def _local_step(x, target, lb_logits, p, weight, send_grad, send_small):
    g = {}
    l0, l1, l2 = lb_logits[0:1], lb_logits[1:2], lb_logits[2:3]
    x_bf = x.astype(BF16)

    proj0 = _mm(x_bf, weight("hg_w_in", x_bf), mode="nn", b_cs=True, o_cs=4, name="mm_hg_in")
    y0, states = _hg_fwd(proj0, l0, l1, l2, p["hg_norm_g"])
    mixed0 = _mm(y0, weight("hg_w_out", y0), mode="nn", name="mm_hg_out")
    hh_l, z_l = [], []

    def ffn_forward(layer, h_in_bf):
        hh = _mm(h_in_bf, weight(f"ffn_w_up{layer}", h_in_bf), mode="nn", b_cs=True, o_cs=2, out_dtype=BF16,
                 name=f"mm_up{layer}")
        z = _ffn_fwd(hh, p["ffn_conv_w"][layer], p["ffn_conv_b"][layer])
        ffn = _mm(z, weight(f"ffn_w_down{layer}", z), mode="nn", name=f"mm_down{layer}")
        hh_l.append(hh)
        z_l.append(z)
        return ffn

    h1, h1b = _ln_fwd(x, mixed0, p["ln1_g"][0], p["ln1_b"][0], "ln1_0")
    ffn0 = ffn_forward(0, h1b)
    h2, h2b = _ln_fwd(h1, ffn0, p["ln2_g"][0], p["ln2_b"][0], "ln2_0")
    pre1 = _mm(h2b, weight("sg_w_in", h2b), mode="nn", b_cs=True, name="mm_sg_in")
    y1 = _sg_fwd(pre1, p["sg_ln_g"], p["sg_ln_b"], p["sg_w_s"], p["sg_bias"])
    mixed1 = _mm(y1, weight("sg_w_out", y1), mode="nn", name="mm_sg_out")
    h3, h3b = _ln_fwd(h2, mixed1, p["ln1_g"][1], p["ln1_b"][1], "ln1_1")
    ffn1 = ffn_forward(1, h3b)

    loss, ds, dres, g["ln2_g1"], g["ln2_b1"] = _ln_loss_bwd(h3, ffn1, target, p["ln2_g"][1], p["ln2_b"][1], "ln2_1_loss")

    def ffn_backward(layer, ds_bf, h_in_bf):
        tok = send_grad(f"ffn_w_down{layer}",
                        _mm(z_l[layer], ds_bf, mode="tn", out_dtype=BF16, name=f"mm_dw_down{layer}"))
        dz = _mm(ds_bf, weight(f"ffn_w_down{layer}", None), mode="nt", out_dtype=BF16, name=f"mm_d_z{layer}",
                 deps=(tok,))
        dhh, g[f"conv_w{layer}"], g[f"conv_b{layer}"] = _ffn_bwd(hh_l[layer], dz, p["ffn_conv_w"][layer],
                                                                   p["ffn_conv_b"][layer])
        tok = send_grad(f"ffn_w_up{layer}", _mm(h_in_bf, dhh, mode="tn", b_cs=True, o_cs=N_DEV, out_dtype=BF16,
                                                name=f"mm_dw_up{layer}"))
        return _mm(dhh, weight(f"ffn_w_up{layer}", None), mode="nt", a_cs=True, b_cs=True, name=f"mm_d_up{layer}",
                   deps=(tok,))

    dh3 = ffn_backward(1, ds, h3b)
    ds, dres, g["ln1_g1"], g["ln1_b1"] = _ln_bwd(h2, mixed1, dres, dh3, p["ln1_g"][1], "ln1_1_bwd")
    tok = send_grad("sg_w_out", _mm(y1, ds, mode="tn", out_dtype=BF16, name="mm_dw_sg_out"))
    dy1 = _mm(ds, weight("sg_w_out", None), mode="nt", name="mm_d_sg_out", deps=(tok,))
    dpre1, g["sg_w_s"], dbs, g["sg_ln_g"], g["sg_ln_b"] = _sg_bwd(pre1, dy1, p["sg_ln_g"], p["sg_ln_b"],
                                                                    p["sg_w_s"], p["sg_bias"])
    g["sg_b_s"] = dbs[:, 0, :]
    tok = send_grad("sg_w_in", _mm(h2b, dpre1, mode="tn", o_cs=N_DEV, out_dtype=BF16, name="mm_dw_sg_in"))
    dh2 = _mm(dpre1, weight("sg_w_in", None), mode="nt", b_cs=True, name="mm_d_sg_in", deps=(tok,))
    ds, dres, g["ln2_g0"], g["ln2_b0"] = _ln_bwd(h1, ffn0, dres, dh2, p["ln2_g"][0], "ln2_0_bwd")
    dh1 = ffn_backward(0, ds, h1b)
    ds, dres, g["ln1_g0"], g["ln1_b0"] = _ln_bwd(x, mixed0, dres, dh1, p["ln1_g"][0], "ln1_0_bwd")
    tok = send_grad("hg_w_out", _mm(y0, ds, mode="tn", out_dtype=BF16, name="mm_dw_hg_out"))
    dy0 = _mm(ds, weight("hg_w_out", None), mode="nt", name="mm_d_hg_out", deps=(tok,))
    dproj, d0, d1, d2, g["hg_norm_g"] = _hg_bwd(proj0, states, dy0, l0, l1, l2, p["hg_norm_g"])
    g["lb_logits"] = jnp.concatenate([d0, d1, d2], axis=0)
    tok = send_small(g)
    tok = send_grad("hg_w_in", _mm(x_bf, dproj, mode="tn", b_cs=True, o_cs=N_DEV, out_dtype=BF16, name="mm_dw_hg_in",
                                   deps=(tok,)))
    grad_x = _mm(dproj, weight("hg_w_in", None), mode="nt", a_cs=True, b_cs=True, addend=dres, name="mm_d_hg_in",
                 deps=(tok,))
    return loss[0, 0], grad_x


_SMALL = ("lb_logits", "hg_norm_g", "sg_ln_g", "sg_ln_b", "sg_w_s", "sg_b_s", "ffn_conv_w", "ffn_conv_b",
          "ln1_g", "ln1_b", "ln2_g", "ln2_b")
_NAMES = ("lb_logits", "hg_w_in", "hg_norm_g", "hg_w_out", "sg_w_in", "sg_ln_g", "sg_ln_b", "sg_w_s", "sg_b_s",
          "sg_w_out", "ffn_w_up", "ffn_conv_w", "ffn_conv_b", "ffn_w_down", "ln1_g", "ln1_b", "ln2_g", "ln2_b")


def kernel(x, lb_logits, hg_w_in, hg_norm_g, hg_w_out, sg_w_in, sg_ln_g, sg_ln_b, sg_w_s, sg_b_s, sg_w_out, ffn_w_up, ffn_conv_w, ffn_conv_b, ffn_w_down, ln1_g, ln1_b, ln2_g, ln2_b, loss_target, m_lb_logits, m_hg_w_in, m_hg_norm_g, m_hg_w_out, m_sg_w_in, m_sg_ln_g, m_sg_ln_b, m_sg_w_s, m_sg_b_s, m_sg_w_out, m_ffn_w_up, m_ffn_conv_w, m_ffn_conv_b, m_ffn_w_down, m_ln1_g, m_ln1_b, m_ln2_g, m_ln2_b, v_lb_logits, v_hg_w_in, v_hg_norm_g, v_hg_w_out, v_sg_w_in, v_sg_ln_g, v_sg_ln_b, v_sg_w_s, v_sg_b_s, v_sg_w_out, v_ffn_w_up, v_ffn_conv_w, v_ffn_conv_b, v_ffn_w_down, v_ln1_g, v_ln1_b, v_ln2_g, v_ln2_b):
    w = dict(lb_logits=lb_logits, hg_w_in=hg_w_in, hg_norm_g=hg_norm_g, hg_w_out=hg_w_out, sg_w_in=sg_w_in,
             sg_ln_g=sg_ln_g, sg_ln_b=sg_ln_b, sg_w_s=sg_w_s, sg_b_s=sg_b_s, sg_w_out=sg_w_out, ffn_w_up=ffn_w_up,
             ffn_conv_w=ffn_conv_w, ffn_conv_b=ffn_conv_b, ffn_w_down=ffn_w_down, ln1_g=ln1_g, ln1_b=ln1_b,
             ln2_g=ln2_g, ln2_b=ln2_b)
    m = dict(lb_logits=m_lb_logits, hg_w_in=m_hg_w_in, hg_norm_g=m_hg_norm_g, hg_w_out=m_hg_w_out,
             sg_w_in=m_sg_w_in, sg_ln_g=m_sg_ln_g, sg_ln_b=m_sg_ln_b, sg_w_s=m_sg_w_s, sg_b_s=m_sg_b_s,
             sg_w_out=m_sg_w_out, ffn_w_up=m_ffn_w_up, ffn_conv_w=m_ffn_conv_w, ffn_conv_b=m_ffn_conv_b,
             ffn_w_down=m_ffn_w_down, ln1_g=m_ln1_g, ln1_b=m_ln1_b, ln2_g=m_ln2_g, ln2_b=m_ln2_b)
    v = dict(lb_logits=v_lb_logits, hg_w_in=v_hg_w_in, hg_norm_g=v_hg_norm_g, hg_w_out=v_hg_w_out,
             sg_w_in=v_sg_w_in, sg_ln_g=v_sg_ln_g, sg_ln_b=v_sg_ln_b, sg_w_s=v_sg_w_s, sg_b_s=v_sg_b_s,
             sg_w_out=v_sg_w_out, ffn_w_up=v_ffn_w_up, ffn_conv_w=v_ffn_conv_w, ffn_conv_b=v_ffn_conv_b,
             ffn_w_down=v_ffn_w_down, ln1_g=v_ln1_g, ln1_b=v_ln1_b, ln2_g=v_ln2_g, ln2_b=v_ln2_b)
    me = 4 * lax.axis_index("x") + 2 * lax.axis_index("y") + lax.axis_index("c")
    is_me = (jnp.arange(N_DEV) == me).astype(F32)
    d = D_MODEL

    shards = [("hg_w_in", hg_w_in[0]), ("hg_w_out", hg_w_out[0]), ("ffn_w_up0", ffn_w_up[0]),
              ("ffn_w_down0", ffn_w_down[0]), ("sg_w_in", sg_w_in[0]), ("sg_w_out", sg_w_out[0]),
              ("ffn_w_up1", ffn_w_up[1]), ("ffn_w_down1", ffn_w_down[1])]
    row_sharded = {"hg_w_out": (d, d), "sg_w_out": (d, d), "ffn_w_down0": (D_FF, d), "ffn_w_down1": (D_FF, d)}
    sv = jnp.zeros((8, 768), F32)
    sv = sv.at[0, :256].set(sg_ln_g[0]).at[1, :256].set(sg_ln_b[0]).at[2:8, :704].set(ffn_conv_w.reshape(6, 704))
    sv = _all_gather(sv, "ag_small")
    gathers, gathered, dep = {}, {}, sv
    for n, shard in shards:
        land = lax.dynamic_update_index_in_dim(lax.empty((N_DEV,) + shard.shape, BF16), shard.astype(BF16), me, 0)
        gathers[n] = _xchg_start(None, land, dep=dep, name=f"ag_{n}", peers=OTHER_CHIPS)
        dep = gathers[n][-1]
    last_start = dep
    handed = {}
    order = [n for n, _ in shards]

    def hand_over(n, after):
        (land,) = _xchg_wait(gathers[n], after, f"agw_{n}", n_pieces=len(OTHER_CHIPS))
        handed[n] = _xchg_start(None, land, dep=is_me, name=f"ah_{n}", peers=SIBLING, pieces=(0,) + OTHER_CHIPS)

    def weight(n, after):
        if n not in gathered:
            if n == order[0]:
                hand_over(n, last_start)
                after = handed[n][-1]
            nxt = order.index(n) + 1
            if nxt < len(order):
                hand_over(order[nxt], after)
                after = handed[order[nxt]][-1]
            (full,) = _xchg_wait(handed[n], after, f"ahw_{n}", n_pieces=1 + len(OTHER_CHIPS))
            gathered[n] = full.reshape(row_sharded[n]) if n in row_sharded else full
        return gathered[n]

    grad_sends = {}

    def send_grad(n, parts):
        shape = (N_DEV,) + dict(shards)[n].shape
        grad_sends[n] = _xchg_start(parts.reshape(shape), lax.empty(shape, parts.dtype), dep=is_me, name=f"rs_{n}")
        return grad_sends[n][-1]

    small_send = {}

    def send_small(g):
        small = [g["lb_logits"], g["hg_norm_g"], g["sg_ln_g"], g["sg_ln_b"], g["sg_w_s"], g["sg_b_s"],
                 g["conv_w0"], g["conv_w1"], g["conv_b0"], g["conv_b1"], g["ln1_g0"], g["ln1_g1"],
                 g["ln1_b0"], g["ln1_b1"], g["ln2_g0"], g["ln2_g1"], g["ln2_b0"], g["ln2_b1"]]
        flat = jnp.concatenate([a.reshape(-1) for a in small])
        rows = -(-flat.shape[0] // (8 * 128)) * 8
        flat = jnp.pad(flat, (0, rows * 128 - flat.shape[0])).reshape(rows, 128)
        land = lax.dynamic_update_index_in_dim(lax.empty((N_DEV, rows, 128), F32), flat, me, 0)
        small_send["shapes"] = [a.shape for a in small]
        small_send["handles"] = _xchg_start(None, land, dep=is_me, name="ar_small")
        return small_send["handles"][-1]

    p = {}
    p["sg_ln_g"] = sv[:, 0, :256].reshape(1, d)
    p["sg_ln_b"] = sv[:, 1, :256].reshape(1, d)
    conv_w_full = jnp.transpose(sv[:, 2:8, :704].reshape(N_DEV, DEPTH, 3, 704), (1, 2, 0, 3)).reshape(DEPTH, 3, D_FF)
    p["ffn_conv_w"] = [conv_w_full[l] for l in range(DEPTH)]
    p["ffn_conv_b"] = [ffn_conv_b[l:l + 1] for l in range(DEPTH)]
    p["hg_norm_g"] = hg_norm_g
    p["sg_w_s"] = sg_w_s[0]
    p["sg_bias"] = jnp.broadcast_to(sg_b_s[0][:, :, None], (SG_GROUPS, SG_CHUNK, SG_DIM))
    for n in ("ln1_g", "ln1_b", "ln2_g", "ln2_b"):
        p[n] = [w[n][l:l + 1] for l in range(DEPTH)]

    loss_part, grad_x = _local_step(x[0], loss_target[0], lb_logits, p, weight, send_grad, send_small)
    loss = lax.psum(loss_part, ("x", "y", "c"))

    (landed,) = _xchg_wait(small_send["handles"], grad_x, "arw_small")
    red = _sum_devices(landed, "sum_small").reshape(-1)
    shapes = small_send["shapes"]
    offs = np.cumsum([0] + [int(np.prod(sh)) for sh in shapes])
    r = [red[offs[i]:offs[i + 1]].reshape(shapes[i]) for i in range(len(shapes))]
    gs = {}
    gs["lb_logits"] = r[0]
    gs["hg_norm_g"] = r[1]
    gs["sg_ln_g"] = lax.dynamic_slice(r[2], (0, me * 256), (1, 256))
    gs["sg_ln_b"] = lax.dynamic_slice(r[3], (0, me * 256), (1, 256))
    gs["sg_w_s"] = r[4][None]
    gs["sg_b_s"] = r[5][None]
    gs["ffn_conv_w"] = lax.dynamic_slice(jnp.stack([r[6], r[7]]), (0, 0, me * 704), (DEPTH, 3, 704))
    gs["ffn_conv_b"] = jnp.concatenate([r[8], r[9]], axis=0)
    gs["ln1_g"] = jnp.concatenate([r[10], r[11]], axis=0)
    gs["ln1_b"] = jnp.concatenate([r[12], r[13]], axis=0)
    gs["ln2_g"] = jnp.concatenate([r[14], r[15]], axis=0)
    gs["ln2_b"] = jnp.concatenate([r[16], r[17]], axis=0)

    out_g, out_d, out_m, out_v = {}, {}, {}, {}
    for n in _SMALL:
        out_g[n], out_d[n], out_m[n], out_v[n] = _adamw(w[n], gs[n], m[n], v[n], f"adamw_{n}")

    res, after = {}, out_v["ln2_b"]
    for n in ("ffn_w_down1", "ffn_w_up1", "sg_w_out", "sg_w_in", "ffn_w_down0", "ffn_w_up0", "hg_w_out", "hg_w_in"):
        parts, recv = _xchg_wait(grad_sends[n], after, f"rsw_{n}")
        own = lax.dynamic_index_in_dim(parts, me, 0, keepdims=False)
        if n[-1] in "01":
            base, layer = n[:-1], int(n[-1])
            res[n] = _adamw(w[base], None, m[base], v[base], f"adamw_{n}", parts=recv, own=own, is_me=is_me,
                            layer=layer, prev=res.get(base + "1"))
        else:
            res[n] = _adamw(w[n][0], None, m[n][0], v[n][0], f"adamw_{n}", parts=recv, own=own, is_me=is_me)
        after = res[n][3]
    for n in ("hg_w_in", "hg_w_out", "sg_w_in", "sg_w_out"):
        out_g[n], out_d[n], out_m[n], out_v[n] = (a[None] for a in res[n])
    for n in ("ffn_w_up", "ffn_w_down"):
        out_g[n], out_d[n], out_m[n], out_v[n] = res[n + "0"]

    return (loss, grad_x[None], *[out_g[n] for n in _NAMES], *[out_d[n] for n in _NAMES],
            *[out_m[n] for n in _NAMES], *[out_v[n] for n in _NAMES])
```

```python
import functools

import numpy as np
import jax
import jax.numpy as jnp
from jax import lax
from jax.experimental import pallas as pl
from jax.experimental.pallas import tpu as pltpu

F32 = jnp.float32
BF16 = jnp.bfloat16
HI = lax.Precision.HIGHEST

N_DEV = 8
D_MODEL = 2048
HG_HEADS = 16
HG_DIM = 128
HG_CHUNK = 128
SG_GROUPS = 16
SG_DIM = 128
SG_CHUNK = 128
D_FF = 5632
DEPTH = 2
ALPHA = (2 * DEPTH) ** 0.25
LN_EPS = 1e-5
RMS_EPS = 1e-6
ADAM_LR = 0.001
ADAM_B1 = 0.9
ADAM_B2 = 0.999
ADAM_EPS = 1e-08
ADAM_WD = 0.01
ADAM_STEP = 10

VMEM_LIMIT = 56 * 1024 * 1024
MESH = pl.DeviceIdType.MESH
ANY = pl.BlockSpec(memory_space=pl.ANY)


def _cp(*sem):
    return pltpu.CompilerParams(dimension_semantics=sem, vmem_limit_bytes=VMEM_LIMIT)


def _pick(n, cands):
    for c in cands:
        if n % c == 0:
            return c
    raise ValueError(f"no tile for {n} in {cands}")


_DIMS = {"nn": (((1,), (0,)), ((), ())), "nt": (((1,), (1,)), ((), ())), "tn": (((0,), (0,)), ((), ()))}


MM_VMEM_BUDGET = 44 * 1024 * 1024


def _mm_tiles(m, n, kk, n_div, k_div, out_bytes, has_addend, k_piece, transposed_a):
    best = None
    for bm in (1024, 1408, 512, 256):
        if m % bm:
            continue
        for bn in (1024, 1408, 512, 256, 128):
            if n % bn or any(d % bn for d in n_div):
                continue
            for bk in (kk, 5632, 4096, 2816, 2048, 1408, 1024, 512, 256, 128):
                if bk > kk or kk % bk or any(d % bk for d in k_div):
                    continue
                if k_piece and bk % k_piece:
                    continue
                nk = kk // bk
                vmem = 4 * (bm * bk + bk * bn) + 2 * bm * bn * out_bytes + (nk > 1) * 4 * bm * bn
                vmem += has_addend * 8 * bm * bn + transposed_a * 2 * bm * bk
                vmem += (out_bytes < 4) * 4 * bm * bn
                if vmem > MM_VMEM_BUDGET:
                    continue
                score = (max(nk, 2), -(bm * bn), -bk)
                if best is None or score < best[0]:
                    best = (score, bm, bn, bk)
    if best is None:
        raise ValueError(f"no tiles for {(m, n, kk)}")
    return best[1:]


def _mm(a, b, *, mode, name, out_dtype=F32, a_cs=False, b_cs=False, o_cs=None, addend=None, deps=()):
    if mode == "tn":
        kk, m = a.shape
    elif a_cs:
        m, kk = a.shape[1], a.shape[0] * a.shape[2]
    else:
        m, kk = a.shape
    if mode == "nt":
        n = b.shape[1] if b_cs else b.shape[0]
    else:
        n = b.shape[0] * b.shape[2] if b_cs else b.shape[1]
    a_c = a.shape[2] if a_cs else None
    b_c = b.shape[2] if b_cs else None
    o_c = n // o_cs if o_cs else None
    k_piece = b_c if (b_cs and mode == "nt") else None
    n_div = [c for c in (b_c if (b_cs and mode != "nt") else None, o_c) if c]
    k_div = [a_c] if a_c else []
    bm, bn, bk = _mm_tiles(m, n, kk, n_div, k_div, jnp.dtype(out_dtype).itemsize, addend is not None, k_piece,
                           mode == "tn")
    nk = kk // bk
    kb = bk // k_piece if k_piece else 1

    def cs_idx(blk, per):
        return (blk * per[0]) // per[1], (blk * per[0] % per[1]) // per[0]

    if mode == "tn":
        a_spec = pl.BlockSpec((bk, bm), lambda i, j, k: (k, i))
    elif a_cs:
        def a_map(i, j, k):
            s, r = cs_idx(k, (bk, a_c))
            return (s, i, r)
        a_spec = pl.BlockSpec((None, bm, bk), a_map)
    else:
        a_spec = pl.BlockSpec((bm, bk), lambda i, j, k: (i, k))
    if mode == "nt":
        if b_cs:
            b_spec = pl.BlockSpec((kb, bn, b_c), lambda i, j, k: (k, j, 0))
        else:
            b_spec = pl.BlockSpec((bn, bk), lambda i, j, k: (j, k))
    else:
        if b_cs:
            def b_map(i, j, k):
                s, r = cs_idx(j, (bn, b_c))
                return (s, k, r)
            b_spec = pl.BlockSpec((None, bk, bn), b_map)
        else:
            b_spec = pl.BlockSpec((bk, bn), lambda i, j, k: (k, j))
    if o_cs:
        def o_map(i, j, k):
            s, r = cs_idx(j, (bn, o_c))
            return (s, i, r)
        o_spec = pl.BlockSpec((None, bm, bn), o_map)
        out_shape = jax.ShapeDtypeStruct((o_cs, m, o_c), out_dtype)
    else:
        o_spec = pl.BlockSpec((bm, bn), lambda i, j, k: (i, j))
        out_shape = jax.ShapeDtypeStruct((m, n), out_dtype)
    in_specs = [a_spec, b_spec]
    args = [a, b]
    if addend is not None:
        in_specs.append(pl.BlockSpec((bm, bn), lambda i, j, k: (i, j)))
        args.append(addend)
    dims = _DIMS[mode]
    in_specs += [ANY] * len(deps)
    args += list(deps)
    n_in = len(args)

    def body(*refs):
        a_ref, b_ref = refs[0], refs[1]
        add_ref = refs[2] if addend is not None else None
        o_ref = refs[n_in]
        if k_piece:
            d = None
            for p in range(kb):
                dp = lax.dot_general(a_ref[:, p * b_c:(p + 1) * b_c].astype(BF16), b_ref[p].astype(BF16), dims,
                                     preferred_element_type=F32)
                d = dp if d is None else d + dp
        else:
            d = lax.dot_general(a_ref[...].astype(BF16), b_ref[...].astype(BF16), dims, preferred_element_type=F32)

        def finish(r):
            if addend is not None:
                r = r + add_ref[...]
            o_ref[...] = r.astype(o_ref.dtype)

        if nk == 1:
            finish(d)
            return
        acc_ref = refs[n_in + 1]
        k = pl.program_id(2)

        @pl.when(k == 0)
        def _():
            acc_ref[...] = d

        if nk > 2:
            @pl.when((k > 0) & (k < nk - 1))
            def _():
                acc_ref[...] += d

        @pl.when(k == nk - 1)
        def _():
            finish(acc_ref[...] + d)

    return pl.pallas_call(
        body, name=name, out_shape=out_shape, grid=(m // bm, n // bn, nk),
        in_specs=in_specs, out_specs=o_spec,
        scratch_shapes=[pltpu.VMEM((bm, bn), F32)] if nk > 1 else [],
        compiler_params=_cp("parallel", "parallel", "arbitrary"),
    )(*args)


LN_ROWS = 256


def _ln_stats(s):
    mu = jnp.mean(s, axis=-1, keepdims=True)
    sc = s - mu
    var = jnp.mean(sc * sc, axis=-1, keepdims=True)
    rstd = lax.rsqrt(var + LN_EPS)
    return sc * rstd, rstd


def _ln_fwd(h, sub, g, b, name):
    t, d = h.shape
    row = pl.BlockSpec((LN_ROWS, d), lambda i: (i, 0))
    vec = pl.BlockSpec((1, d), lambda i: (0, 0))

    def body(h_ref, s_ref, g_ref, b_ref, y_ref, yb_ref):
        xhat, _ = _ln_stats(ALPHA * h_ref[...] + s_ref[...])
        y = xhat * g_ref[...] + b_ref[...]
        y_ref[...] = y
        yb_ref[...] = y.astype(BF16)

    return pl.pallas_call(
        body, name=name, grid=(t // LN_ROWS,),
        out_shape=(jax.ShapeDtypeStruct((t, d), F32), jax.ShapeDtypeStruct((t, d), BF16)),
        in_specs=[row, row, vec, vec], out_specs=(row, row), compiler_params=_cp("parallel"),
    )(h, sub, g, b)


def _ln_bwd_math(xhat, rstd, dy, g):
    dxhat = dy * g
    m1 = jnp.mean(dxhat, axis=-1, keepdims=True)
    m2 = jnp.mean(dxhat * xhat, axis=-1, keepdims=True)
    ds = rstd * (dxhat - m1 - xhat * m2)
    dg = jnp.sum(dy * xhat, axis=0, keepdims=True)
    db = jnp.sum(dy, axis=0, keepdims=True)
    return ds, dg, db


def _ln_bwd(h, sub, dy_a, dy_b, g, name):
    t, d = h.shape
    row = pl.BlockSpec((LN_ROWS, d), lambda i: (i, 0))
    vec = pl.BlockSpec((1, d), lambda i: (0, 0))

    def body(h_ref, s_ref, da_ref, db_ref, g_ref, ds_ref, dres_ref, dg_ref, dbeta_ref):
        xhat, rstd = _ln_stats(ALPHA * h_ref[...] + s_ref[...])
        ds, dg, db = _ln_bwd_math(xhat, rstd, da_ref[...] + db_ref[...], g_ref[...])
        ds_ref[...] = ds.astype(BF16)
        dres_ref[...] = ALPHA * ds

        @pl.when(pl.program_id(0) == 0)
        def _():
            dg_ref[...] = jnp.zeros_like(dg_ref)
            dbeta_ref[...] = jnp.zeros_like(dbeta_ref)

        dg_ref[...] += dg
        dbeta_ref[...] += db

    return pl.pallas_call(
        body, name=name, grid=(t // LN_ROWS,),
        out_shape=(jax.ShapeDtypeStruct((t, d), BF16), jax.ShapeDtypeStruct((t, d), F32),
                   jax.ShapeDtypeStruct((1, d), F32), jax.ShapeDtypeStruct((1, d), F32)),
        in_specs=[row, row, row, row, vec], out_specs=(row, row, vec, vec),
        compiler_params=_cp("arbitrary"),
    )(h, sub, dy_a, dy_b, g)


def _ln_loss_bwd(h, sub, target, g, b, name):
    t, d = h.shape
    row = pl.BlockSpec((LN_ROWS, d), lambda i: (i, 0))
    vec = pl.BlockSpec((1, d), lambda i: (0, 0))
    lvec = pl.BlockSpec((1, 128), lambda i: (0, 0))

    def body(h_ref, s_ref, t_ref, g_ref, b_ref, loss_ref, ds_ref, dres_ref, dg_ref, dbeta_ref):
        xhat, rstd = _ln_stats(ALPHA * h_ref[...] + s_ref[...])
        y = xhat * g_ref[...] + b_ref[...]
        err = y - t_ref[...]
        part = 0.5 * jnp.sum(jnp.mean(err * err, axis=-1, keepdims=True), axis=0, keepdims=True)
        ds, dg, db = _ln_bwd_math(xhat, rstd, err * (1.0 / d), g_ref[...])
        ds_ref[...] = ds.astype(BF16)
        dres_ref[...] = ALPHA * ds

        @pl.when(pl.program_id(0) == 0)
        def _():
            loss_ref[...] = jnp.zeros_like(loss_ref)
            dg_ref[...] = jnp.zeros_like(dg_ref)
            dbeta_ref[...] = jnp.zeros_like(dbeta_ref)

        loss_ref[...] += jnp.broadcast_to(part, loss_ref.shape)
        dg_ref[...] += dg
        dbeta_ref[...] += db

    return pl.pallas_call(
        body, name=name, grid=(t // LN_ROWS,),
        out_shape=(jax.ShapeDtypeStruct((1, 128), F32), jax.ShapeDtypeStruct((t, d), BF16),
                   jax.ShapeDtypeStruct((t, d), F32), jax.ShapeDtypeStruct((1, d), F32),
                   jax.ShapeDtypeStruct((1, d), F32)),
        in_specs=[row, row, row, vec, vec], out_specs=(lvec, row, row, vec, vec),
        compiler_params=_cp("arbitrary"),
    )(h, sub, target, g, b)


HG_ROWS = 512
HG_LEVELS = 7


def _hg_constants():
    c = HG_CHUNK
    t = np.arange(c)[:, None]
    j = np.arange(c)[None, :]
    blocks, masks = [], [np.eye(c)]
    for lev in range(HG_LEVELS):
        m = 1 << lev
        second = (t % (2 * m)) >= m
        mid = (t // (2 * m)) * (2 * m) + m - 1
        blocks.append((second & (j > mid) & (j <= t)) | ((~second) & (j > t) & (j <= mid)))
        same = (t // (2 * m)) == (j // (2 * m))
        masks.append(same & second & ((j % (2 * m)) < m))
    blocks += [j <= t, j > t, np.ones((c, c), bool)]
    cm = np.concatenate(blocks, axis=0).astype(np.float32)
    mk = np.concatenate(masks, axis=0).astype(np.float32)
    return jnp.asarray(cm, dtype=BF16), jnp.asarray(mk)


def _split3(x):
    hi = x.astype(BF16)
    r = x - hi.astype(F32)
    mid = r.astype(BF16)
    lo = (r - mid.astype(F32)).astype(BF16)
    return hi, mid, lo


def _sum3(x):
    n = x.shape[-1] // 3
    return x[:, :n] + x[:, n:2 * n] + x[:, 2 * n:]


@jax.custom_vjp
def _prefix(cm, lf):
    return _sum3(lax.dot_general(cm, jnp.concatenate(_split3(lf), axis=1), _DIMS["nn"], preferred_element_type=F32))


def _prefix_fwd(cm, lf):
    return _prefix(cm, lf), cm


def _prefix_bwd(cm, d):
    hi = d.astype(BF16)
    lo = (d - hi.astype(F32)).astype(BF16)
    both = lax.dot_general(cm, jnp.concatenate([hi, lo], axis=1), _DIMS["tn"], preferred_element_type=F32)
    n = d.shape[-1]
    return jnp.zeros_like(cm), both[:, :n] + both[:, n:]


_prefix.defvjp(_prefix_fwd, _prefix_bwd)

_BWD = {"nn": (("nt", 0, 1), ("tn", 1, 0)), "nt": (("nn", 0, 1), ("tn", 0, 1)), "tn": (("nt", 1, 0), ("nn", 1, 0))}


def _bdot_raw(a, b, mode):
    return lax.dot_general(a.astype(BF16), b.astype(BF16), _DIMS[mode], preferred_element_type=F32)


@functools.partial(jax.custom_vjp, nondiff_argnums=(2,))
def _bdot(a, b, mode):
    return _bdot_raw(a, b, mode)


def _bdot_fwd(a, b, mode):
    return _bdot_raw(a, b, mode), (a, b)


def _bdot_bwd(mode, res, d):
    a, b = res
    (ma, da_pos, _), (mb, db_pos, _) = _BWD[mode]
    da = _bdot_raw(d, b, ma) if da_pos == 0 else _bdot_raw(b, d, ma)
    db = _bdot_raw(d, a, mb) if db_pos == 0 else _bdot_raw(a, d, mb)
    return da, db


_bdot.defvjp(_bdot_fwd, _bdot_bwd)


def _hg_chunk(pq, pf, pi, pg, l0, l1, l2, ng, s_t, cm, mk):
    c = HG_CHUNK
    mx = jnp.maximum(jnp.maximum(l0, l1), l2)
    e0, e1, e2 = jnp.exp(l0 - mx), jnp.exp(l1 - mx), jnp.exp(l2 - mx)
    lb = e0 / (e0 + e1 + e2)
    q = pq * jax.nn.sigmoid(pq)
    t1 = jnp.log(lb)
    t2 = jnp.log1p(-lb) + jax.nn.log_sigmoid(pf)
    lf = jnp.maximum(t1, t2) + jnp.log1p(jnp.exp(-jnp.abs(t1 - t2)))
    k = (1.0 - lb) * jax.nn.sigmoid(-pf)
    x = jnp.exp(_prefix(cm, lf))
    scores = mk[0:c] * _bdot(q, k, "nt")
    for lev in range(HG_LEVELS):
        xl = x[lev * c:(lev + 1) * c]
        scores = scores + mk[(lev + 1) * c:(lev + 2) * c] * _bdot(q * xl, k * xl, "nt")
    x_incl = x[HG_LEVELS * c:(HG_LEVELS + 1) * c]
    x_after = x[(HG_LEVELS + 1) * c:(HG_LEVELS + 2) * c]
    x_total = x[(HG_LEVELS + 2) * c:(HG_LEVELS + 3) * c]
    o = _bdot(scores, pi, "nn") + _bdot(q * x_incl, s_t, "nt")
    s_new = s_t * jnp.concatenate([x_total] * (HG_DIM // c), axis=0) + _bdot(pi, k * x_after, "tn")
    rstd = lax.rsqrt(jnp.mean(o * o, axis=-1, keepdims=True) + RMS_EPS)
    y = o * rstd * ng * (pg * jax.nn.sigmoid(pg))
    return y, s_new


def _hg_specs(t, reverse):
    nrb = t // HG_ROWS
    rb = (lambda r: nrb - 1 - r) if reverse else (lambda r: r)
    proj = pl.BlockSpec((4, HG_ROWS, HG_DIM), lambda h, r: (0, rb(r), h))
    vec = pl.BlockSpec((1, HG_DIM), lambda h, r: (0, h))
    cm = pl.BlockSpec(((HG_LEVELS + 3) * HG_CHUNK, HG_CHUNK), lambda h, r: (0, 0))
    mk = pl.BlockSpec(((HG_LEVELS + 1) * HG_CHUNK, HG_CHUNK), lambda h, r: (0, 0))
    rows = pl.BlockSpec((HG_ROWS, HG_DIM), lambda h, r: (rb(r), h))
    states = pl.BlockSpec((None, HG_ROWS // HG_CHUNK, HG_DIM, HG_DIM), lambda h, r: (h, rb(r), 0, 0))
    return proj, vec, cm, mk, rows, states


def _hg_fwd(proj, l0, l1, l2, ng):
    t = proj.shape[1]
    n_in = HG_ROWS // HG_CHUNK
    cm, mk = _hg_constants()
    p_spec, vec, cm_spec, mk_spec, rows, st_spec = _hg_specs(t, False)

    def body(p_ref, l0_ref, l1_ref, l2_ref, ng_ref, cm_ref, mk_ref, y_ref, st_ref, s_ref):
        @pl.when(pl.program_id(1) == 0)
        def _():
            s_ref[...] = jnp.zeros_like(s_ref)

        def step(ci, carry):
            sl = pl.ds(pl.multiple_of(ci * HG_CHUNK, HG_CHUNK), HG_CHUNK)
            s_t = s_ref[...]
            st_ref[ci] = s_t
            y, s_new = _hg_chunk(p_ref[0, sl, :], p_ref[1, sl, :], p_ref[2, sl, :], p_ref[3, sl, :],
                                 l0_ref[...], l1_ref[...], l2_ref[...], ng_ref[...], s_t,
                                 cm_ref[...], mk_ref[...])
            y_ref[sl, :] = y.astype(BF16)
            s_ref[...] = s_new
            return carry

        lax.fori_loop(0, n_in, step, 0, unroll=True)

    return pl.pallas_call(
        body, name="hg_fwd", grid=(HG_HEADS, t // HG_ROWS),
        out_shape=(jax.ShapeDtypeStruct((t, D_MODEL), BF16),
                   jax.ShapeDtypeStruct((HG_HEADS, t // HG_CHUNK, HG_DIM, HG_DIM), F32)),
        in_specs=[p_spec, vec, vec, vec, vec, cm_spec, mk_spec], out_specs=(rows, st_spec),
        scratch_shapes=[pltpu.VMEM((HG_DIM, HG_DIM), F32)],
        compiler_params=_cp("parallel", "arbitrary"),
    )(proj, l0, l1, l2, ng, cm, mk)


def _hg_bwd(proj, states, dy, l0, l1, l2, ng):
    t = proj.shape[1]
    n_in = HG_ROWS // HG_CHUNK
    cm, mk = _hg_constants()
    p_spec, vec, cm_spec, mk_spec, rows, st_spec = _hg_specs(t, True)

    def body(p_ref, st_ref, dy_ref, l0_ref, l1_ref, l2_ref, ng_ref, cm_ref, mk_ref,
             dp_ref, dl0_ref, dl1_ref, dl2_ref, dng_ref, ds_ref):
        @pl.when(pl.program_id(1) == 0)
        def _():
            ds_ref[...] = jnp.zeros_like(ds_ref)
            for r in (dl0_ref, dl1_ref, dl2_ref, dng_ref):
                r[...] = jnp.zeros_like(r)

        def step(it, carry):
            ci = n_in - 1 - it
            sl = pl.ds(pl.multiple_of(ci * HG_CHUNK, HG_CHUNK), HG_CHUNK)
            fn = functools.partial(_hg_chunk, cm=cm_ref[...], mk=mk_ref[...])
            _, vjp = jax.vjp(fn, p_ref[0, sl, :], p_ref[1, sl, :], p_ref[2, sl, :], p_ref[3, sl, :],
                             l0_ref[...], l1_ref[...], l2_ref[...], ng_ref[...], st_ref[ci])
            dq, df, di, dg, d0, d1, d2, dn, ds = vjp((dy_ref[sl, :], ds_ref[...]))
            dp_ref[0, sl, :] = dq.astype(BF16)
            dp_ref[1, sl, :] = df.astype(BF16)
            dp_ref[2, sl, :] = di.astype(BF16)
            dp_ref[3, sl, :] = dg.astype(BF16)
            dl0_ref[...] += d0
            dl1_ref[...] += d1
            dl2_ref[...] += d2
            dng_ref[...] += dn
            ds_ref[...] = ds
            return carry

        lax.fori_loop(0, n_in, step, 0, unroll=True)

    v_shape = jax.ShapeDtypeStruct((1, D_MODEL), F32)
    return pl.pallas_call(
        body, name="hg_bwd", grid=(HG_HEADS, t // HG_ROWS),
        out_shape=(jax.ShapeDtypeStruct((4, t, D_MODEL), BF16), v_shape, v_shape, v_shape, v_shape),
        in_specs=[p_spec, st_spec, rows, vec, vec, vec, vec, cm_spec, mk_spec],
        out_specs=(p_spec, vec, vec, vec, vec),
        scratch_shapes=[pltpu.VMEM((HG_DIM, HG_DIM), F32)],
        compiler_params=_cp("parallel", "arbitrary"),
    )(proj, states, dy, l0, l1, l2, ng, cm, mk)


_SQRT_HALF = 0.7071067811865476
_INV_SQRT_2PI = 0.3989422804014327


def _gelu(x):
    return 0.5 * x * (1.0 + lax.erf(x * _SQRT_HALF))


def _gelu_grad(x):
    return 0.5 * (1.0 + lax.erf(x * _SQRT_HALF)) + x * (_INV_SQRT_2PI * jnp.exp(-0.5 * x * x))


def _tril(n):
    return (lax.broadcasted_iota(jnp.int32, (n, n), 0) >= lax.broadcasted_iota(jnp.int32, (n, n), 1)).astype(F32)


def _sg_specs(d):
    row = lambda w: pl.BlockSpec((SG_CHUNK, w), lambda i: (i, 0))
    vec = pl.BlockSpec((1, d), lambda i: (0, 0))
    cube = pl.BlockSpec((SG_GROUPS, SG_CHUNK, SG_CHUNK), lambda i: (0, 0, 0))
    return row, vec, cube


def _sg_fwd(pre, ln_g, ln_b, w_s, bias):
    t = pre.shape[0]
    d = D_MODEL
    row, vec, cube = _sg_specs(d)

    def body(pre_ref, g_ref, b_ref, w_ref, bias_ref, y_ref, vln_ref):
        u = _gelu(pre_ref[:, :d])
        vhat, _ = _ln_stats(_gelu(pre_ref[:, d:]))
        vln_ref[...] = vhat * g_ref[...] + b_ref[...]
        tril = _tril(SG_CHUNK)
        for g in range(SG_GROUPS):
            cs = slice(g * SG_DIM, (g + 1) * SG_DIM)
            gate = _bdot_raw(w_ref[g] * tril, vln_ref[:, cs], "nn") + bias_ref[g]
            y_ref[:, cs] = (u[:, cs] * gate).astype(BF16)

    return pl.pallas_call(
        body, name="sg_fwd", grid=(t // SG_CHUNK,), out_shape=jax.ShapeDtypeStruct((t, d), BF16),
        in_specs=[row(2 * d), vec, vec, cube, cube], out_specs=row(d),
        scratch_shapes=[pltpu.VMEM((SG_CHUNK, d), F32)], compiler_params=_cp("parallel"),
    )(pre, ln_g, ln_b, w_s, bias)


def _sg_bwd(pre, dy, ln_g, ln_b, w_s, bias):
    t = pre.shape[0]
    d = D_MODEL
    row, vec, cube = _sg_specs(d)
    dbs_spec = pl.BlockSpec((SG_GROUPS, 8, SG_CHUNK), lambda i: (0, 0, 0))

    def body(pre_ref, dy_ref, g_ref, b_ref, w_ref, bias_ref, dpre_ref, dw_ref, dbs_ref, dlg_ref, dlb_ref,
             vln_ref, dvln_ref):
        @pl.when(pl.program_id(0) == 0)
        def _():
            for r in (dw_ref, dbs_ref, dlg_ref, dlb_ref):
                r[...] = jnp.zeros_like(r)

        pu = pre_ref[:, :d]
        pv = pre_ref[:, d:]
        u = _gelu(pu)
        vhat, rstd = _ln_stats(_gelu(pv))
        vln_ref[...] = vhat * g_ref[...] + b_ref[...]
        tril = _tril(SG_CHUNK)
        ones = jnp.ones((8, SG_DIM), F32)
        for g in range(SG_GROUPS):
            cs = slice(g * SG_DIM, (g + 1) * SG_DIM)
            wc = w_ref[g] * tril
            vg = vln_ref[:, cs]
            gate = _bdot_raw(wc, vg, "nn") + bias_ref[g]
            dyg = dy_ref[:, cs]
            dgate = dyg * u[:, cs]
            dpre_ref[:, cs] = (dyg * gate * _gelu_grad(pu[:, cs])).astype(BF16)
            dw_ref[g] += tril * _bdot_raw(dgate, vg, "nt")
            dbs_ref[g] += lax.dot_general(ones, dgate, _DIMS["nt"], precision=HI, preferred_element_type=F32)
            dvln_ref[:, cs] = _bdot_raw(wc, dgate, "tn")
        dvln = dvln_ref[...]
        dv, dg, db = _ln_bwd_math(vhat, rstd, dvln, g_ref[...])
        dlg_ref[...] += dg
        dlb_ref[...] += db
        dpre_ref[:, d:] = (dv * _gelu_grad(pv)).astype(BF16)

    return pl.pallas_call(
        body, name="sg_bwd", grid=(t // SG_CHUNK,),
        out_shape=(jax.ShapeDtypeStruct((t, 2 * d), BF16), jax.ShapeDtypeStruct(w_s.shape, F32),
                   jax.ShapeDtypeStruct((SG_GROUPS, 8, SG_CHUNK), F32),
                   jax.ShapeDtypeStruct((1, d), F32), jax.ShapeDtypeStruct((1, d), F32)),
        in_specs=[row(2 * d), row(d), vec, vec, cube, cube],
        out_specs=(row(2 * d), cube, dbs_spec, vec, vec),
        scratch_shapes=[pltpu.VMEM((SG_CHUNK, d), F32), pltpu.VMEM((SG_CHUNK, d), F32)],
        compiler_params=_cp("arbitrary"),
    )(pre, dy, ln_g, ln_b, w_s, bias)


FFN_ROWS = 256
FFN_COLS = 1408
HALO = 16


def _ffn_conv(a_ext, w_ref, cb_ref):
    a1 = pltpu.roll(a_ext, 1, 0)
    a2 = pltpu.roll(a_ext, 2, 0)
    return w_ref[0:1, :] * a2 + w_ref[1:2, :] * a1 + w_ref[2:3, :] * a_ext + cb_ref[...], a1, a2


def _ffn_fwd(hh, conv_w, conv_b):
    t = hh.shape[1]
    nb8 = FFN_ROWS // HALO
    main = pl.BlockSpec((2, FFN_ROWS, FFN_COLS), lambda c, r: (0, r, c))
    prev = pl.BlockSpec((None, HALO, FFN_COLS), lambda c, r: (0, jnp.maximum(r * nb8 - 1, 0), c))
    wspec = pl.BlockSpec((3, FFN_COLS), lambda c, r: (0, c))
    bspec = pl.BlockSpec((1, FFN_COLS), lambda c, r: (0, c))

    def body(m_ref, p_ref, w_ref, cb_ref, z_ref):
        prev = jnp.where(pl.program_id(1) == 0, 0.0, p_ref[...].astype(F32))
        a_ext = jnp.concatenate([prev, m_ref[0].astype(F32)], axis=0)
        a, _, _ = _ffn_conv(a_ext, w_ref, cb_ref)
        a = a[HALO:]
        z_ref[...] = (a * jax.nn.sigmoid(a) * m_ref[1].astype(F32)).astype(BF16)

    return pl.pallas_call(
        body, name="ffn_fwd", grid=(D_FF // FFN_COLS, t // FFN_ROWS),
        out_shape=jax.ShapeDtypeStruct((t, D_FF), BF16),
        in_specs=[main, prev, wspec, bspec],
        out_specs=pl.BlockSpec((FFN_ROWS, FFN_COLS), lambda c, r: (r, c)),
        compiler_params=_cp("parallel", "parallel"),
    )(hh, hh, conv_w, conv_b)


def _ffn_bwd(hh, dz, conv_w, conv_b):
    t = hh.shape[1]
    nb8 = FFN_ROWS // HALO
    last8 = t // HALO - 1
    nr = t // FFN_ROWS
    main = pl.BlockSpec((2, FFN_ROWS, FFN_COLS), lambda c, r: (0, r, c))
    prev = pl.BlockSpec((None, HALO, FFN_COLS), lambda c, r: (0, jnp.maximum(r * nb8 - 1, 0), c))
    nxt = pl.BlockSpec((2, HALO, FFN_COLS), lambda c, r: (0, jnp.minimum((r + 1) * nb8, last8), c))
    dmain = pl.BlockSpec((FFN_ROWS, FFN_COLS), lambda c, r: (r, c))
    dnxt = pl.BlockSpec((HALO, FFN_COLS), lambda c, r: (jnp.minimum((r + 1) * nb8, last8), c))
    wspec = pl.BlockSpec((3, FFN_COLS), lambda c, r: (0, c))
    bspec = pl.BlockSpec((1, FFN_COLS), lambda c, r: (0, c))

    def body(m_ref, p_ref, n_ref, dz_ref, dzn_ref, w_ref, cb_ref, dh_ref, dw_ref, dcb_ref):
        r = pl.program_id(1)

        @pl.when(r == 0)
        def _():
            dw_ref[...] = jnp.zeros_like(dw_ref)
            dcb_ref[...] = jnp.zeros_like(dcb_ref)

        prev = jnp.where(r == 0, 0.0, p_ref[...].astype(F32))
        a_ext = jnp.concatenate([prev, m_ref[0].astype(F32), n_ref[0].astype(F32)], axis=0)
        a, a1, a2 = _ffn_conv(a_ext, w_ref, cb_ref)
        a = a[HALO:]
        b_ext = jnp.concatenate([m_ref[1], n_ref[1]], axis=0).astype(F32)
        dz_main = dz_ref[...].astype(F32)
        dz_ext = jnp.concatenate([dz_main, jnp.where(r == nr - 1, 0.0, dzn_ref[...].astype(F32))], axis=0)
        sig = jax.nn.sigmoid(a)
        da = dz_ext * b_ext * (sig * (1.0 + a * (1.0 - sig)))
        n_ext = FFN_ROWS + HALO
        da_p1 = pltpu.roll(da, n_ext - 1, 0)
        da_p2 = pltpu.roll(da, n_ext - 2, 0)
        da_raw = w_ref[2:3, :] * da + w_ref[1:2, :] * da_p1 + w_ref[0:1, :] * da_p2
        dh_ref[0] = da_raw[:FFN_ROWS].astype(BF16)
        dh_ref[1] = (dz_main * (a * sig)[:FFN_ROWS]).astype(BF16)
        dam = da[:FFN_ROWS]
        rows = slice(HALO, HALO + FFN_ROWS)
        dw_ref[0:1, :] += jnp.sum(dam * a2[rows], axis=0, keepdims=True)
        dw_ref[1:2, :] += jnp.sum(dam * a1[rows], axis=0, keepdims=True)
        dw_ref[2:3, :] += jnp.sum(dam * a_ext[rows], axis=0, keepdims=True)
        dcb_ref[...] += jnp.sum(dam, axis=0, keepdims=True)

    return pl.pallas_call(
        body, name="ffn_bwd", grid=(D_FF // FFN_COLS, nr),
        out_shape=(jax.ShapeDtypeStruct((2, t, D_FF), BF16), jax.ShapeDtypeStruct((3, D_FF), F32),
                   jax.ShapeDtypeStruct((1, D_FF), F32)),
        in_specs=[main, prev, nxt, dmain, dnxt, wspec, bspec],
        out_specs=(main, wspec, bspec), compiler_params=_cp("parallel", "arbitrary"),
    )(hh, hh, hh, dz, dz, conv_w, conv_b)


def _adamw_math(w, g, m, v):
    m = ADAM_B1 * m + (1.0 - ADAM_B1) * g
    v = ADAM_B2 * v + (1.0 - ADAM_B2) * (g * g)
    m_hat = m / (1.0 - ADAM_B1 ** ADAM_STEP)
    v_hat = v / (1.0 - ADAM_B2 ** ADAM_STEP)
    delta = -ADAM_LR * (m_hat / (jnp.sqrt(v_hat) + ADAM_EPS) + ADAM_WD * w)
    return delta, m, v


def _as2d(shape):
    n = int(np.prod(shape))
    c = shape[-1] if shape[-1] % 128 == 0 else (128 if n % 128 == 0 else shape[-1])
    return n // c, c


def _adamw(w, g, m, v, name, parts=None, own=None, is_me=None, layer=None, prev=None):
    shape = w.shape if layer is None else w.shape[1:]
    r, c = _as2d(shape)
    br = r if r <= 512 else _pick(r, (256, 176, 128, 64, 8))
    blk = pl.BlockSpec((br, c), lambda i: (i, 0))
    if layer is None:
        wblk, o = blk, jax.ShapeDtypeStruct((r, c), F32)
        w2, m2, v2 = (a.reshape(r, c) for a in (w, m, v))
    else:
        wblk, o = pl.BlockSpec((None, br, c), lambda i: (layer, i, 0)), jax.ShapeDtypeStruct((w.shape[0], r, c), F32)
        w2, m2, v2 = (a.reshape(w.shape[0], r, c) for a in (w, m, v))
    extra, extra_specs = [], []
    if parts is not None:
        g_in = parts.reshape(N_DEV, r, c)
        g_spec = pl.BlockSpec((N_DEV, br, c), lambda i: (0, i, 0))
        extra = [own.reshape(r, c), is_me]
        extra_specs = [blk, pl.BlockSpec(memory_space=pltpu.SMEM)]
    else:
        g_in = g.reshape(r, c)
        g_spec = blk
    aliases = {}
    if prev is not None:
        aliases = {4 + len(extra) + j: j for j in range(4)}
        extra = extra + [a.reshape(o.shape) for a in prev]
        extra_specs = extra_specs + [ANY] * 4

    def body(w_ref, g_ref, m_ref, v_ref, *rest):
        go_ref, d_ref, mo_ref, vo_ref = rest[-4:]
        if parts is not None:
            own_ref, me_ref = rest[:2]
            gsum = None
            for i in range(N_DEV):
                term = jnp.where(me_ref[i] > 0.5, own_ref[...], g_ref[i]).astype(F32)
                gsum = term if gsum is None else gsum + term
        else:
            gsum = g_ref[...]
        delta, m_new, v_new = _adamw_math(w_ref[...], gsum, m_ref[...], v_ref[...])
        go_ref[...] = gsum
        d_ref[...] = delta
        mo_ref[...] = m_new
        vo_ref[...] = v_new

    outs = pl.pallas_call(
        body, name=name, grid=(r // br,), out_shape=(o, o, o, o),
        in_specs=[wblk, g_spec, wblk, wblk] + extra_specs, out_specs=(wblk, wblk, wblk, wblk),
        input_output_aliases=aliases, compiler_params=_cp("parallel"),
    )(w2, g_in, m2, v2, *extra)
    return tuple(a.reshape(w.shape) for a in outs)


ALL_PEERS = tuple(range(1, N_DEV))
OTHER_CHIPS = (2, 4, 6)
SIBLING = (1,)


def _me_and_peers(relations=ALL_PEERS):
    x, y, c = lax.axis_index("x"), lax.axis_index("y"), lax.axis_index("c")
    me = 4 * x + 2 * y + c
    peers = []
    for k in relations:
        kx, ky, kc = (k >> 2) & 1, (k >> 1) & 1, k & 1
        px, py, pc = x ^ kx, y ^ ky, c ^ kc
        peers.append(((px, py, pc), 4 * px + 2 * py + pc))
    return me, peers


def _all_gather(shard, name):
    def body(x_ref, o_ref, send_sems, recv_sems, local_sem):
        me, peers = _me_and_peers()
        mine = pltpu.make_async_copy(x_ref, o_ref.at[me], local_sem)
        mine.start()
        sends = []
        for k, (dev, _) in enumerate(peers):
            cp = pltpu.make_async_remote_copy(src_ref=x_ref, dst_ref=o_ref.at[me], send_sem=send_sems.at[k],
                                              recv_sem=recv_sems.at[k], device_id=dev, device_id_type=MESH)
            cp.start()
            sends.append(cp)
        for k, (dev, idx) in enumerate(peers):
            pltpu.make_async_remote_copy(src_ref=x_ref, dst_ref=o_ref.at[idx], send_sem=send_sems.at[k],
                                         recv_sem=recv_sems.at[k], device_id=dev, device_id_type=MESH).wait_recv()
        for cp in sends:
            cp.wait_send()
        mine.wait()

    return pl.pallas_call(
        body, name=name, out_shape=jax.ShapeDtypeStruct((N_DEV,) + shard.shape, shard.dtype),
        in_specs=[ANY], out_specs=ANY,
        scratch_shapes=[pltpu.SemaphoreType.DMA((N_DEV - 1,)), pltpu.SemaphoreType.DMA((N_DEV - 1,)),
                        pltpu.SemaphoreType.DMA],
        compiler_params=pltpu.CompilerParams(has_side_effects=True),
    )(shard)


HBM = pl.BlockSpec(memory_space=pltpu.HBM)
SEM = pl.BlockSpec(memory_space=pltpu.SEMAPHORE)
EFFECT = pltpu.SideEffectType.DATAFLOW_SIDE_EFFECTING


def _xchg_start(parts, land, *, dep, name, peers=ALL_PEERS, pieces=(0,)):
    n_buf = 1 if parts is None else 2

    def body(*refs):
        land_ref = refs[n_buf - 1]
        send_sem, recv_sem = refs[n_buf + 1], refs[n_buf + 2]
        token = refs[2 * n_buf + 3]
        me, to = _me_and_peers(peers)
        for dev, idx in to:
            if parts is None:
                for r in pieces:
                    piece = land_ref.at[me ^ r]
                    pltpu.make_async_remote_copy(src_ref=piece, dst_ref=piece, send_sem=send_sem, recv_sem=recv_sem,
                                                 device_id=dev, device_id_type=MESH).start()
            else:
                pltpu.make_async_remote_copy(src_ref=refs[0].at[idx], dst_ref=land_ref.at[me], send_sem=send_sem,
                                             recv_sem=recv_sem, device_id=dev, device_id_type=MESH).start()
        token[...] = jnp.zeros_like(token)

    bufs = [land] if parts is None else [parts, land]
    return pl.pallas_call(
        body, name=name,
        out_shape=(pltpu.SemaphoreType.DMA(()), pltpu.SemaphoreType.DMA(()),
                   *[pltpu.HBM(b.shape, b.dtype) for b in bufs], jax.ShapeDtypeStruct((8, 128), F32)),
        in_specs=(*[HBM] * n_buf, ANY), out_specs=(SEM, SEM, *[HBM] * n_buf, pl.BlockSpec(memory_space=pltpu.VMEM)),
        input_output_aliases={i: 2 + i for i in range(n_buf)},
        compiler_params=pltpu.CompilerParams(has_side_effects=EFFECT),
    )(*[pltpu.with_memory_space_constraint(b, pltpu.HBM) for b in bufs], dep)


def _xchg_wait(handles, after, name, n_pieces=N_DEV - 1):
    send_sem, recv_sem, *bufs, _ = handles
    n_buf = len(bufs)

    def body(*refs):
        land_ref = refs[n_buf - 1]
        send_sem, recv_sem = refs[n_buf], refs[n_buf + 1]
        x, y, c = lax.axis_index("x"), lax.axis_index("y"), lax.axis_index("c")
        span = land_ref.at[pl.ds(0, n_pieces)]
        cp = pltpu.make_async_remote_copy(src_ref=span, dst_ref=span, send_sem=send_sem, recv_sem=recv_sem,
                                          device_id=(x, y, c), device_id_type=MESH)
        cp.wait_send()
        cp.wait_recv()

    return pl.pallas_call(
        body, name=name, out_shape=tuple(pltpu.HBM(b.shape, b.dtype) for b in bufs),
        in_specs=(*[HBM] * n_buf, SEM, SEM, ANY), out_specs=tuple([HBM] * n_buf),
        input_output_aliases={i: i for i in range(n_buf)},
        compiler_params=pltpu.CompilerParams(has_side_effects=EFFECT),
    )(*bufs, send_sem, recv_sem, after)


def _sum_devices(v, name):
    _, r, c = v.shape

    def body(x_ref, o_ref):
        acc = x_ref[0]
        for i in range(1, N_DEV):
            acc = acc + x_ref[i]
        o_ref[...] = acc

    return pl.pallas_call(
        body, name=name, out_shape=jax.ShapeDtypeStruct((r, c), F32),
        in_specs=[pl.BlockSpec(memory_space=pltpu.VMEM)], out_specs=pl.BlockSpec(memory_space=pltpu.VMEM),
        compiler_params=pltpu.CompilerParams(vmem_limit_bytes=VMEM_LIMIT),
    )(v)


def _local_step(x, target, lb_logits, p, weight, send_grad, send_small):
    g = {}
    l0, l1, l2 = lb_logits[0:1], lb_logits[1:2], lb_logits[2:3]
    x_bf = x.astype(BF16)

    proj0 = _mm(x_bf, weight("hg_w_in", x_bf), mode="nn", b_cs=True, o_cs=4, name="mm_hg_in")
    y0, states = _hg_fwd(proj0, l0, l1, l2, p["hg_norm_g"])
    mixed0 = _mm(y0, weight("hg_w_out", y0), mode="nn", name="mm_hg_out")
    hh_l, z_l = [], []

    def ffn_forward(layer, h_in_bf):
        hh = _mm(h_in_bf, weight(f"ffn_w_up{layer}", h_in_bf), mode="nn", b_cs=True, o_cs=2, out_dtype=BF16,
                 name=f"mm_up{layer}")
        z = _ffn_fwd(hh, p["ffn_conv_w"][layer], p["ffn_conv_b"][layer])
        ffn = _mm(z, weight(f"ffn_w_down{layer}", z), mode="nn", name=f"mm_down{layer}")
        hh_l.append(hh)
        z_l.append(z)
        return ffn

    h1, h1b = _ln_fwd(x, mixed0, p["ln1_g"][0], p["ln1_b"][0], "ln1_0")
    ffn0 = ffn_forward(0, h1b)
    h2, h2b = _ln_fwd(h1, ffn0, p["ln2_g"][0], p["ln2_b"][0], "ln2_0")
    pre1 = _mm(h2b, weight("sg_w_in", h2b), mode="nn", b_cs=True, name="mm_sg_in")
    y1 = _sg_fwd(pre1, p["sg_ln_g"], p["sg_ln_b"], p["sg_w_s"], p["sg_bias"])
    mixed1 = _mm(y1, weight("sg_w_out", y1), mode="nn", name="mm_sg_out")
    h3, h3b = _ln_fwd(h2, mixed1, p["ln1_g"][1], p["ln1_b"][1], "ln1_1")
    ffn1 = ffn_forward(1, h3b)

    loss, ds, dres, g["ln2_g1"], g["ln2_b1"] = _ln_loss_bwd(h3, ffn1, target, p["ln2_g"][1], p["ln2_b"][1], "ln2_1_loss")

    def ffn_backward(layer, ds_bf, h_in_bf):
        tok = send_grad(f"ffn_w_down{layer}",
                        _mm(z_l[layer], ds_bf, mode="tn", out_dtype=BF16, name=f"mm_dw_down{layer}"))
        dz = _mm(ds_bf, weight(f"ffn_w_down{layer}", None), mode="nt", out_dtype=BF16, name=f"mm_d_z{layer}",
                 deps=(tok,))
        dhh, g[f"conv_w{layer}"], g[f"conv_b{layer}"] = _ffn_bwd(hh_l[layer], dz, p["ffn_conv_w"][layer],
                                                                   p["ffn_conv_b"][layer])
        tok = send_grad(f"ffn_w_up{layer}", _mm(h_in_bf, dhh, mode="tn", b_cs=True, o_cs=N_DEV, out_dtype=BF16,
                                                name=f"mm_dw_up{layer}"))
        return _mm(dhh, weight(f"ffn_w_up{layer}", None), mode="nt", a_cs=True, b_cs=True, name=f"mm_d_up{layer}",
                   deps=(tok,))

    dh3 = ffn_backward(1, ds, h3b)
    ds, dres, g["ln1_g1"], g["ln1_b1"] = _ln_bwd(h2, mixed1, dres, dh3, p["ln1_g"][1], "ln1_1_bwd")
    tok = send_grad("sg_w_out", _mm(y1, ds, mode="tn", out_dtype=BF16, name="mm_dw_sg_out"))
    dy1 = _mm(ds, weight("sg_w_out", None), mode="nt", name="mm_d_sg_out", deps=(tok,))
    dpre1, g["sg_w_s"], dbs, g["sg_ln_g"], g["sg_ln_b"] = _sg_bwd(pre1, dy1, p["sg_ln_g"], p["sg_ln_b"],
                                                                    p["sg_w_s"], p["sg_bias"])
    g["sg_b_s"] = dbs[:, 0, :]
    tok = send_grad("sg_w_in", _mm(h2b, dpre1, mode="tn", o_cs=N_DEV, out_dtype=BF16, name="mm_dw_sg_in"))
    dh2 = _mm(dpre1, weight("sg_w_in", None), mode="nt", b_cs=True, name="mm_d_sg_in", deps=(tok,))
    ds, dres, g["ln2_g0"], g["ln2_b0"] = _ln_bwd(h1, ffn0, dres, dh2, p["ln2_g"][0], "ln2_0_bwd")
    dh1 = ffn_backward(0, ds, h1b)
    ds, dres, g["ln1_g0"], g["ln1_b0"] = _ln_bwd(x, mixed0, dres, dh1, p["ln1_g"][0], "ln1_0_bwd")
    tok = send_grad("hg_w_out", _mm(y0, ds, mode="tn", out_dtype=BF16, name="mm_dw_hg_out"))
    dy0 = _mm(ds, weight("hg_w_out", None), mode="nt", name="mm_d_hg_out", deps=(tok,))
    dproj, d0, d1, d2, g["hg_norm_g"] = _hg_bwd(proj0, states, dy0, l0, l1, l2, p["hg_norm_g"])
    g["lb_logits"] = jnp.concatenate([d0, d1, d2], axis=0)
    tok = send_small(g)
    tok = send_grad("hg_w_in", _mm(x_bf, dproj, mode="tn", b_cs=True, o_cs=N_DEV, out_dtype=BF16, name="mm_dw_hg_in",
                                   deps=(tok,)))
    grad_x = _mm(dproj, weight("hg_w_in", None), mode="nt", a_cs=True, b_cs=True, addend=dres, name="mm_d_hg_in",
                 deps=(tok,))
    return loss[0, 0], grad_x


_SMALL = ("lb_logits", "hg_norm_g", "sg_ln_g", "sg_ln_b", "sg_w_s", "sg_b_s", "ffn_conv_w", "ffn_conv_b",
          "ln1_g", "ln1_b", "ln2_g", "ln2_b")
_NAMES = ("lb_logits", "hg_w_in", "hg_norm_g", "hg_w_out", "sg_w_in", "sg_ln_g", "sg_ln_b", "sg_w_s", "sg_b_s",
          "sg_w_out", "ffn_w_up", "ffn_conv_w", "ffn_conv_b", "ffn_w_down", "ln1_g", "ln1_b", "ln2_g", "ln2_b")


def kernel(x, lb_logits, hg_w_in, hg_norm_g, hg_w_out, sg_w_in, sg_ln_g, sg_ln_b, sg_w_s, sg_b_s, sg_w_out, ffn_w_up, ffn_conv_w, ffn_conv_b, ffn_w_down, ln1_g, ln1_b, ln2_g, ln2_b, loss_target, m_lb_logits, m_hg_w_in, m_hg_norm_g, m_hg_w_out, m_sg_w_in, m_sg_ln_g, m_sg_ln_b, m_sg_w_s, m_sg_b_s, m_sg_w_out, m_ffn_w_up, m_ffn_conv_w, m_ffn_conv_b, m_ffn_w_down, m_ln1_g, m_ln1_b, m_ln2_g, m_ln2_b, v_lb_logits, v_hg_w_in, v_hg_norm_g, v_hg_w_out, v_sg_w_in, v_sg_ln_g, v_sg_ln_b, v_sg_w_s, v_sg_b_s, v_sg_w_out, v_ffn_w_up, v_ffn_conv_w, v_ffn_conv_b, v_ffn_w_down, v_ln1_g, v_ln1_b, v_ln2_g, v_ln2_b):
    w = dict(lb_logits=lb_logits, hg_w_in=hg_w_in, hg_norm_g=hg_norm_g, hg_w_out=hg_w_out, sg_w_in=sg_w_in,
             sg_ln_g=sg_ln_g, sg_ln_b=sg_ln_b, sg_w_s=sg_w_s, sg_b_s=sg_b_s, sg_w_out=sg_w_out, ffn_w_up=ffn_w_up,
             ffn_conv_w=ffn_conv_w, ffn_conv_b=ffn_conv_b, ffn_w_down=ffn_w_down, ln1_g=ln1_g, ln1_b=ln1_b,
             ln2_g=ln2_g, ln2_b=ln2_b)
    m = dict(lb_logits=m_lb_logits, hg_w_in=m_hg_w_in, hg_norm_g=m_hg_norm_g, hg_w_out=m_hg_w_out,
             sg_w_in=m_sg_w_in, sg_ln_g=m_sg_ln_g, sg_ln_b=m_sg_ln_b, sg_w_s=m_sg_w_s, sg_b_s=m_sg_b_s,
             sg_w_out=m_sg_w_out, ffn_w_up=m_ffn_w_up, ffn_conv_w=m_ffn_conv_w, ffn_conv_b=m_ffn_conv_b,
             ffn_w_down=m_ffn_w_down, ln1_g=m_ln1_g, ln1_b=m_ln1_b, ln2_g=m_ln2_g, ln2_b=m_ln2_b)
    v = dict(lb_logits=v_lb_logits, hg_w_in=v_hg_w_in, hg_norm_g=v_hg_norm_g, hg_w_out=v_hg_w_out,
             sg_w_in=v_sg_w_in, sg_ln_g=v_sg_ln_g, sg_ln_b=v_sg_ln_b, sg_w_s=v_sg_w_s, sg_b_s=v_sg_b_s,
             sg_w_out=v_sg_w_out, ffn_w_up=v_ffn_w_up, ffn_conv_w=v_ffn_conv_w, ffn_conv_b=v_ffn_conv_b,
             ffn_w_down=v_ffn_w_down, ln1_g=v_ln1_g, ln1_b=v_ln1_b, ln2_g=v_ln2_g, ln2_b=v_ln2_b)
    me = 4 * lax.axis_index("x") + 2 * lax.axis_index("y") + lax.axis_index("c")
    is_me = (jnp.arange(N_DEV) == me).astype(F32)
    d = D_MODEL

    shards = [("hg_w_in", hg_w_in[0]), ("hg_w_out", hg_w_out[0]), ("ffn_w_up0", ffn_w_up[0]),
              ("ffn_w_down0", ffn_w_down[0]), ("sg_w_in", sg_w_in[0]), ("sg_w_out", sg_w_out[0]),
              ("ffn_w_up1", ffn_w_up[1]), ("ffn_w_down1", ffn_w_down[1])]
    row_sharded = {"hg_w_out": (d, d), "sg_w_out": (d, d), "ffn_w_down0": (D_FF, d), "ffn_w_down1": (D_FF, d)}
    sv = jnp.zeros((8, 768), F32)
    sv = sv.at[0, :256].set(sg_ln_g[0]).at[1, :256].set(sg_ln_b[0]).at[2:8, :704].set(ffn_conv_w.reshape(6, 704))
    sv = _all_gather(sv, "ag_small")
    gathers, gathered, dep = {}, {}, sv
    for n, shard in shards:
        land = lax.dynamic_update_index_in_dim(lax.empty((N_DEV,) + shard.shape, BF16), shard.astype(BF16), me, 0)
        gathers[n] = _xchg_start(None, land, dep=dep, name=f"ag_{n}", peers=OTHER_CHIPS)
        dep = gathers[n][-1]
    last_start = dep
    handed = {}
    order = [n for n, _ in shards]

    def hand_over(n, after):
        (land,) = _xchg_wait(gathers[n], after, f"agw_{n}", n_pieces=len(OTHER_CHIPS))
        handed[n] = _xchg_start(None, land, dep=is_me, name=f"ah_{n}", peers=SIBLING, pieces=(0,) + OTHER_CHIPS)

    def weight(n, after):
        if n not in gathered:
            if n == order[0]:
                hand_over(n, last_start)
                after = handed[n][-1]
            nxt = order.index(n) + 1
            if nxt < len(order):
                hand_over(order[nxt], after)
                after = handed[order[nxt]][-1]
            (full,) = _xchg_wait(handed[n], after, f"ahw_{n}", n_pieces=1 + len(OTHER_CHIPS))
            gathered[n] = full.reshape(row_sharded[n]) if n in row_sharded else full
        return gathered[n]

    grad_sends = {}

    def send_grad(n, parts):
        shape = (N_DEV,) + dict(shards)[n].shape
        grad_sends[n] = _xchg_start(parts.reshape(shape), lax.empty(shape, parts.dtype), dep=is_me, name=f"rs_{n}")
        return grad_sends[n][-1]

    small_send = {}

    def send_small(g):
        small = [g["lb_logits"], g["hg_norm_g"], g["sg_ln_g"], g["sg_ln_b"], g["sg_w_s"], g["sg_b_s"],
                 g["conv_w0"], g["conv_w1"], g["conv_b0"], g["conv_b1"], g["ln1_g0"], g["ln1_g1"],
                 g["ln1_b0"], g["ln1_b1"], g["ln2_g0"], g["ln2_g1"], g["ln2_b0"], g["ln2_b1"]]
        flat = jnp.concatenate([a.reshape(-1) for a in small])
        rows = -(-flat.shape[0] // (8 * 128)) * 8
        flat = jnp.pad(flat, (0, rows * 128 - flat.shape[0])).reshape(rows, 128)
        land = lax.dynamic_update_index_in_dim(lax.empty((N_DEV, rows, 128), F32), flat, me, 0)
        small_send["shapes"] = [a.shape for a in small]
        small_send["handles"] = _xchg_start(None, land, dep=is_me, name="ar_small")
        return small_send["handles"][-1]

    p = {}
    p["sg_ln_g"] = sv[:, 0, :256].reshape(1, d)
    p["sg_ln_b"] = sv[:, 1, :256].reshape(1, d)
    conv_w_full = jnp.transpose(sv[:, 2:8, :704].reshape(N_DEV, DEPTH, 3, 704), (1, 2, 0, 3)).reshape(DEPTH, 3, D_FF)
    p["ffn_conv_w"] = [conv_w_full[l] for l in range(DEPTH)]
    p["ffn_conv_b"] = [ffn_conv_b[l:l + 1] for l in range(DEPTH)]
    p["hg_norm_g"] = hg_norm_g
    p["sg_w_s"] = sg_w_s[0]
    p["sg_bias"] = jnp.broadcast_to(sg_b_s[0][:, :, None], (SG_GROUPS, SG_CHUNK, SG_DIM))
    for n in ("ln1_g", "ln1_b", "ln2_g", "ln2_b"):
        p[n] = [w[n][l:l + 1] for l in range(DEPTH)]

    loss_part, grad_x = _local_step(x[0], loss_target[0], lb_logits, p, weight, send_grad, send_small)
    loss = lax.psum(loss_part, ("x", "y", "c"))

    (landed,) = _xchg_wait(small_send["handles"], grad_x, "arw_small")
    red = _sum_devices(landed, "sum_small").reshape(-1)
    shapes = small_send["shapes"]
    offs = np.cumsum([0] + [int(np.prod(sh)) for sh in shapes])
    r = [red[offs[i]:offs[i + 1]].reshape(shapes[i]) for i in range(len(shapes))]
    gs = {}
    gs["lb_logits"] = r[0]
    gs["hg_norm_g"] = r[1]
    gs["sg_ln_g"] = lax.dynamic_slice(r[2], (0, me * 256), (1, 256))
    gs["sg_ln_b"] = lax.dynamic_slice(r[3], (0, me * 256), (1, 256))
    gs["sg_w_s"] = r[4][None]
    gs["sg_b_s"] = r[5][None]
    gs["ffn_conv_w"] = lax.dynamic_slice(jnp.stack([r[6], r[7]]), (0, 0, me * 704), (DEPTH, 3, 704))
    gs["ffn_conv_b"] = jnp.concatenate([r[8], r[9]], axis=0)
    gs["ln1_g"] = jnp.concatenate([r[10], r[11]], axis=0)
    gs["ln1_b"] = jnp.concatenate([r[12], r[13]], axis=0)
    gs["ln2_g"] = jnp.concatenate([r[14], r[15]], axis=0)
    gs["ln2_b"] = jnp.concatenate([r[16], r[17]], axis=0)

    out_g, out_d, out_m, out_v = {}, {}, {}, {}
    for n in _SMALL:
        out_g[n], out_d[n], out_m[n], out_v[n] = _adamw(w[n], gs[n], m[n], v[n], f"adamw_{n}")

    res, after = {}, out_v["ln2_b"]
    for n in ("ffn_w_down1", "ffn_w_up1", "sg_w_out", "sg_w_in", "ffn_w_down0", "ffn_w_up0", "hg_w_out", "hg_w_in"):
        parts, recv = _xchg_wait(grad_sends[n], after, f"rsw_{n}")
        own = lax.dynamic_index_in_dim(parts, me, 0, keepdims=False)
        if n[-1] in "01":
            base, layer = n[:-1], int(n[-1])
            res[n] = _adamw(w[base], None, m[base], v[base], f"adamw_{n}", parts=recv, own=own, is_me=is_me,
                            layer=layer, prev=res.get(base + "1"))
        else:
            res[n] = _adamw(w[n][0], None, m[n][0], v[n][0], f"adamw_{n}", parts=recv, own=own, is_me=is_me)
        after = res[n][3]
    for n in ("hg_w_in", "hg_w_out", "sg_w_in", "sg_w_out"):
        out_g[n], out_d[n], out_m[n], out_v[n] = (a[None] for a in res[n])
    for n in ("ffn_w_up", "ffn_w_down"):
        out_g[n], out_d[n], out_m[n], out_v[n] = res[n + "0"]

    return (loss, grad_x[None], *[out_g[n] for n in _NAMES], *[out_d[n] for n in _NAMES],
            *[out_m[n] for n in _NAMES], *[out_v[n] for n in _NAMES])
```

```python
import functools

import numpy as np
import jax
import jax.numpy as jnp
from jax import lax
from jax.experimental import pallas as pl
from jax.experimental.pallas import tpu as pltpu

F32 = jnp.float32
BF16 = jnp.bfloat16
HI = lax.Precision.HIGHEST

N_DEV = 8
D_MODEL = 2048
HG_HEADS = 16
HG_DIM = 128
HG_CHUNK = 128
SG_GROUPS = 16
SG_DIM = 128
SG_CHUNK = 128
D_FF = 5632
DEPTH = 2
ALPHA = (2 * DEPTH) ** 0.25
LN_EPS = 1e-5
RMS_EPS = 1e-6
ADAM_LR = 0.001
ADAM_B1 = 0.9
ADAM_B2 = 0.999
ADAM_EPS = 1e-08
ADAM_WD = 0.01
ADAM_STEP = 10

VMEM_LIMIT = 56 * 1024 * 1024
MESH = pl.DeviceIdType.MESH
ANY = pl.BlockSpec(memory_space=pl.ANY)


def _cp(*sem):
    return pltpu.CompilerParams(dimension_semantics=sem, vmem_limit_bytes=VMEM_LIMIT)


def _pick(n, cands):
    for c in cands:
        if n % c == 0:
            return c
    raise ValueError(f"no tile for {n} in {cands}")


_DIMS = {"nn": (((1,), (0,)), ((), ())), "nt": (((1,), (1,)), ((), ())), "tn": (((0,), (0,)), ((), ()))}


MM_VMEM_BUDGET = 44 * 1024 * 1024


def _mm_tiles(m, n, kk, n_div, k_div, out_bytes, has_addend, k_piece, transposed_a):
    best = None
    for bm in (1024, 1408, 512, 256):
        if m % bm:
            continue
        for bn in (1024, 1408, 512, 256, 128):
            if n % bn or any(d % bn for d in n_div):
                continue
            for bk in (kk, 5632, 4096, 2816, 2048, 1408, 1024, 512, 256, 128):
                if bk > kk or kk % bk or any(d % bk for d in k_div):
                    continue
                if k_piece and bk % k_piece:
                    continue
                nk = kk // bk
                vmem = 4 * (bm * bk + bk * bn) + 2 * bm * bn * out_bytes + (nk > 1) * 4 * bm * bn
                vmem += has_addend * 8 * bm * bn + transposed_a * 2 * bm * bk
                vmem += (out_bytes < 4) * 4 * bm * bn
                if vmem > MM_VMEM_BUDGET:
                    continue
                score = (max(nk, 2), -(bm * bn), -bk)
                if best is None or score < best[0]:
                    best = (score, bm, bn, bk)
    if best is None:
        raise ValueError(f"no tiles for {(m, n, kk)}")
    return best[1:]


def _mm(a, b, *, mode, name, out_dtype=F32, a_cs=False, b_cs=False, o_cs=None, addend=None, deps=()):
    if mode == "tn":
        kk, m = a.shape
    elif a_cs:
        m, kk = a.shape[1], a.shape[0] * a.shape[2]
    else:
        m, kk = a.shape
    if mode == "nt":
        n = b.shape[1] if b_cs else b.shape[0]
    else:
        n = b.shape[0] * b.shape[2] if b_cs else b.shape[1]
    a_c = a.shape[2] if a_cs else None
    b_c = b.shape[2] if b_cs else None
    o_c = n // o_cs if o_cs else None
    k_piece = b_c if (b_cs and mode == "nt") else None
    n_div = [c for c in (b_c if (b_cs and mode != "nt") else None, o_c) if c]
    k_div = [a_c] if a_c else []
    bm, bn, bk = _mm_tiles(m, n, kk, n_div, k_div, jnp.dtype(out_dtype).itemsize, addend is not None, k_piece,
                           mode == "tn")
    nk = kk // bk
    kb = bk // k_piece if k_piece else 1

    def cs_idx(blk, per):
        return (blk * per[0]) // per[1], (blk * per[0] % per[1]) // per[0]

    if mode == "tn":
        a_spec = pl.BlockSpec((bk, bm), lambda i, j, k: (k, i))
    elif a_cs:
        def a_map(i, j, k):
            s, r = cs_idx(k, (bk, a_c))
            return (s, i, r)
        a_spec = pl.BlockSpec((None, bm, bk), a_map)
    else:
        a_spec = pl.BlockSpec((bm, bk), lambda i, j, k: (i, k))
    if mode == "nt":
        if b_cs:
            b_spec = pl.BlockSpec((kb, bn, b_c), lambda i, j, k: (k, j, 0))
        else:
            b_spec = pl.BlockSpec((bn, bk), lambda i, j, k: (j, k))
    else:
        if b_cs:
            def b_map(i, j, k):
                s, r = cs_idx(j, (bn, b_c))
                return (s, k, r)
            b_spec = pl.BlockSpec((None, bk, bn), b_map)
        else:
            b_spec = pl.BlockSpec((bk, bn), lambda i, j, k: (k, j))
    if o_cs:
        def o_map(i, j, k):
            s, r = cs_idx(j, (bn, o_c))
            return (s, i, r)
        o_spec = pl.BlockSpec((None, bm, bn), o_map)
        out_shape = jax.ShapeDtypeStruct((o_cs, m, o_c), out_dtype)
    else:
        o_spec = pl.BlockSpec((bm, bn), lambda i, j, k: (i, j))
        out_shape = jax.ShapeDtypeStruct((m, n), out_dtype)
    in_specs = [a_spec, b_spec]
    args = [a, b]
    if addend is not None:
        in_specs.append(pl.BlockSpec((bm, bn), lambda i, j, k: (i, j)))
        args.append(addend)
    dims = _DIMS[mode]
    in_specs += [ANY] * len(deps)
    args += list(deps)
    n_in = len(args)

    def body(*refs):
        a_ref, b_ref = refs[0], refs[1]
        add_ref = refs[2] if addend is not None else None
        o_ref = refs[n_in]
        if k_piece:
            d = None
            for p in range(kb):
                dp = lax.dot_general(a_ref[:, p * b_c:(p + 1) * b_c].astype(BF16), b_ref[p].astype(BF16), dims,
                                     preferred_element_type=F32)
                d = dp if d is None else d + dp
        else:
            d = lax.dot_general(a_ref[...].astype(BF16), b_ref[...].astype(BF16), dims, preferred_element_type=F32)

        def finish(r):
            if addend is not None:
                r = r + add_ref[...]
            o_ref[...] = r.astype(o_ref.dtype)

        if nk == 1:
            finish(d)
            return
        acc_ref = refs[n_in + 1]
        k = pl.program_id(2)

        @pl.when(k == 0)
        def _():
            acc_ref[...] = d

        if nk > 2:
            @pl.when((k > 0) & (k < nk - 1))
            def _():
                acc_ref[...] += d

        @pl.when(k == nk - 1)
        def _():
            finish(acc_ref[...] + d)

    return pl.pallas_call(
        body, name=name, out_shape=out_shape, grid=(m // bm, n // bn, nk),
        in_specs=in_specs, out_specs=o_spec,
        scratch_shapes=[pltpu.VMEM((bm, bn), F32)] if nk > 1 else [],
        compiler_params=_cp("parallel", "parallel", "arbitrary"),
    )(*args)


LN_ROWS = 256


def _ln_stats(s):
    mu = jnp.mean(s, axis=-1, keepdims=True)
    sc = s - mu
    var = jnp.mean(sc * sc, axis=-1, keepdims=True)
    rstd = lax.rsqrt(var + LN_EPS)
    return sc * rstd, rstd


def _ln_fwd(h, sub, g, b, name):
    t, d = h.shape
    row = pl.BlockSpec((LN_ROWS, d), lambda i: (i, 0))
    vec = pl.BlockSpec((1, d), lambda i: (0, 0))

    def body(h_ref, s_ref, g_ref, b_ref, y_ref, yb_ref):
        xhat, _ = _ln_stats(ALPHA * h_ref[...] + s_ref[...])
        y = xhat * g_ref[...] + b_ref[...]
        y_ref[...] = y
        yb_ref[...] = y.astype(BF16)

    return pl.pallas_call(
        body, name=name, grid=(t // LN_ROWS,),
        out_shape=(jax.ShapeDtypeStruct((t, d), F32), jax.ShapeDtypeStruct((t, d), BF16)),
        in_specs=[row, row, vec, vec], out_specs=(row, row), compiler_params=_cp("parallel"),
    )(h, sub, g, b)


def _ln_bwd_math(xhat, rstd, dy, g):
    dxhat = dy * g
    m1 = jnp.mean(dxhat, axis=-1, keepdims=True)
    m2 = jnp.mean(dxhat * xhat, axis=-1, keepdims=True)
    ds = rstd * (dxhat - m1 - xhat * m2)
    dg = jnp.sum(dy * xhat, axis=0, keepdims=True)
    db = jnp.sum(dy, axis=0, keepdims=True)
    return ds, dg, db


def _ln_bwd(h, sub, dy_a, dy_b, g, name):
    t, d = h.shape
    row = pl.BlockSpec((LN_ROWS, d), lambda i: (i, 0))
    vec = pl.BlockSpec((1, d), lambda i: (0, 0))

    def body(h_ref, s_ref, da_ref, db_ref, g_ref, ds_ref, dres_ref, dg_ref, dbeta_ref):
        xhat, rstd = _ln_stats(ALPHA * h_ref[...] + s_ref[...])
        ds, dg, db = _ln_bwd_math(xhat, rstd, da_ref[...] + db_ref[...], g_ref[...])
        ds_ref[...] = ds.astype(BF16)
        dres_ref[...] = ALPHA * ds

        @pl.when(pl.program_id(0) == 0)
        def _():
            dg_ref[...] = jnp.zeros_like(dg_ref)
            dbeta_ref[...] = jnp.zeros_like(dbeta_ref)

        dg_ref[...] += dg
        dbeta_ref[...] += db

    return pl.pallas_call(
        body, name=name, grid=(t // LN_ROWS,),
        out_shape=(jax.ShapeDtypeStruct((t, d), BF16), jax.ShapeDtypeStruct((t, d), F32),
                   jax.ShapeDtypeStruct((1, d), F32), jax.ShapeDtypeStruct((1, d), F32)),
        in_specs=[row, row, row, row, vec], out_specs=(row, row, vec, vec),
        compiler_params=_cp("arbitrary"),
    )(h, sub, dy_a, dy_b, g)


def _ln_loss_bwd(h, sub, target, g, b, name):
    t, d = h.shape
    row = pl.BlockSpec((LN_ROWS, d), lambda i: (i, 0))
    vec = pl.BlockSpec((1, d), lambda i: (0, 0))
    lvec = pl.BlockSpec((1, 128), lambda i: (0, 0))

    def body(h_ref, s_ref, t_ref, g_ref, b_ref, loss_ref, ds_ref, dres_ref, dg_ref, dbeta_ref):
        xhat, rstd = _ln_stats(ALPHA * h_ref[...] + s_ref[...])
        y = xhat * g_ref[...] + b_ref[...]
        err = y - t_ref[...]
        part = 0.5 * jnp.sum(jnp.mean(err * err, axis=-1, keepdims=True), axis=0, keepdims=True)
        ds, dg, db = _ln_bwd_math(xhat, rstd, err * (1.0 / d), g_ref[...])
        ds_ref[...] = ds.astype(BF16)
        dres_ref[...] = ALPHA * ds

        @pl.when(pl.program_id(0) == 0)
        def _():
            loss_ref[...] = jnp.zeros_like(loss_ref)
            dg_ref[...] = jnp.zeros_like(dg_ref)
            dbeta_ref[...] = jnp.zeros_like(dbeta_ref)

        loss_ref[...] += jnp.broadcast_to(part, loss_ref.shape)
        dg_ref[...] += dg
        dbeta_ref[...] += db

    return pl.pallas_call(
        body, name=name, grid=(t // LN_ROWS,),
        out_shape=(jax.ShapeDtypeStruct((1, 128), F32), jax.ShapeDtypeStruct((t, d), BF16),
                   jax.ShapeDtypeStruct((t, d), F32), jax.ShapeDtypeStruct((1, d), F32),
                   jax.ShapeDtypeStruct((1, d), F32)),
        in_specs=[row, row, row, vec, vec], out_specs=(lvec, row, row, vec, vec),
        compiler_params=_cp("arbitrary"),
    )(h, sub, target, g, b)


HG_ROWS = 2048
HG_LEVELS = 7


def _hg_constants():
    c = HG_CHUNK
    t = np.arange(c)[:, None]
    j = np.arange(c)[None, :]
    blocks, masks = [], [np.eye(c)]
    for lev in range(HG_LEVELS):
        m = 1 << lev
        second = (t % (2 * m)) >= m
        mid = (t // (2 * m)) * (2 * m) + m - 1
        blocks.append((second & (j > mid) & (j <= t)) | ((~second) & (j > t) & (j <= mid)))
        same = (t // (2 * m)) == (j // (2 * m))
        masks.append(same & second & ((j % (2 * m)) < m))
    blocks += [j <= t, j > t, np.ones((c, c), bool)]
    cm = np.concatenate(blocks, axis=0).astype(np.float32)
    mk = np.concatenate(masks, axis=0).astype(np.float32)
    return jnp.asarray(cm, dtype=BF16), jnp.asarray(mk)


def _dot_01(cm, x, mode):
    hi = x.astype(BF16)
    lo = (x - hi.astype(F32)).astype(BF16)
    both = lax.dot_general(cm, jnp.concatenate([hi, lo], axis=1), _DIMS[mode], preferred_element_type=F32)
    n = x.shape[-1]
    return both[:, :n] + both[:, n:]


@jax.custom_vjp
def _prefix(cm, lf):
    return _dot_01(cm, lf, "nn")


def _prefix_fwd(cm, lf):
    return _prefix(cm, lf), cm


def _prefix_bwd(cm, d):
    return jnp.zeros_like(cm), _dot_01(cm, d, "tn")


_prefix.defvjp(_prefix_fwd, _prefix_bwd)

_BWD = {"nn": (("nt", 0, 1), ("tn", 1, 0)), "nt": (("nn", 0, 1), ("tn", 0, 1)), "tn": (("nt", 1, 0), ("nn", 1, 0))}


def _bdot_raw(a, b, mode):
    return lax.dot_general(a.astype(BF16), b.astype(BF16), _DIMS[mode], preferred_element_type=F32)


@functools.partial(jax.custom_vjp, nondiff_argnums=(2,))
def _bdot(a, b, mode):
    return _bdot_raw(a, b, mode)


def _bdot_fwd(a, b, mode):
    return _bdot_raw(a, b, mode), (a, b)


def _bdot_bwd(mode, res, d):
    a, b = res
    (ma, da_pos, _), (mb, db_pos, _) = _BWD[mode]
    da = _bdot_raw(d, b, ma) if da_pos == 0 else _bdot_raw(b, d, ma)
    db = _bdot_raw(d, a, mb) if db_pos == 0 else _bdot_raw(a, d, mb)
    return da, db


_bdot.defvjp(_bdot_fwd, _bdot_bwd)


def _hg_chunk(pq, pf, pi, pg, l0, l1, l2, ng, s_t, cm, mk):
    c = HG_CHUNK
    mx = jnp.maximum(jnp.maximum(l0, l1), l2)
    e0, e1, e2 = jnp.exp(l0 - mx), jnp.exp(l1 - mx), jnp.exp(l2 - mx)
    lb = e0 / (e0 + e1 + e2)
    q = pq * jax.nn.sigmoid(pq)
    t1 = jnp.log(lb)
    t2 = jnp.log1p(-lb) + jax.nn.log_sigmoid(pf)
    lf = jnp.maximum(t1, t2) + jnp.log1p(jnp.exp(-jnp.abs(t1 - t2)))
    k = (1.0 - lb) * jax.nn.sigmoid(-pf)
    x = jnp.exp(_prefix(cm, lf))
    scores = mk[0:c] * _bdot(q, k, "nt")
    for lev in range(HG_LEVELS):
        xl = x[lev * c:(lev + 1) * c]
        scores = scores + mk[(lev + 1) * c:(lev + 2) * c] * _bdot(q * xl, k * xl, "nt")
    x_incl = x[HG_LEVELS * c:(HG_LEVELS + 1) * c]
    x_after = x[(HG_LEVELS + 1) * c:(HG_LEVELS + 2) * c]
    x_total = x[(HG_LEVELS + 2) * c:(HG_LEVELS + 3) * c]
    o = _bdot(scores, pi, "nn") + _bdot(q * x_incl, s_t, "nt")
    s_new = s_t * jnp.concatenate([x_total] * (HG_DIM // c), axis=0) + _bdot(pi, k * x_after, "tn")
    rstd = lax.rsqrt(jnp.mean(o * o, axis=-1, keepdims=True) + RMS_EPS)
    y = o * rstd * ng * (pg * jax.nn.sigmoid(pg))
    return y, s_new


def _hg_specs(t, reverse):
    nrb = t // HG_ROWS
    rb = (lambda r: nrb - 1 - r) if reverse else (lambda r: r)
    proj = pl.BlockSpec((4, HG_ROWS, HG_DIM), lambda h, r: (0, rb(r), h))
    vec = pl.BlockSpec((1, HG_DIM), lambda h, r: (0, h))
    cm = pl.BlockSpec(((HG_LEVELS + 3) * HG_CHUNK, HG_CHUNK), lambda h, r: (0, 0))
    mk = pl.BlockSpec(((HG_LEVELS + 1) * HG_CHUNK, HG_CHUNK), lambda h, r: (0, 0))
    rows = pl.BlockSpec((HG_ROWS, HG_DIM), lambda h, r: (rb(r), h))
    states = pl.BlockSpec((None, HG_ROWS // HG_CHUNK, HG_DIM, HG_DIM), lambda h, r: (h, rb(r), 0, 0))
    return proj, vec, cm, mk, rows, states


def _hg_fwd(proj, l0, l1, l2, ng):
    t = proj.shape[1]
    n_in = HG_ROWS // HG_CHUNK
    cm, mk = _hg_constants()
    p_spec, vec, cm_spec, mk_spec, rows, st_spec = _hg_specs(t, False)

    def body(p_ref, l0_ref, l1_ref, l2_ref, ng_ref, cm_ref, mk_ref, y_ref, st_ref, s_ref):
        @pl.when(pl.program_id(1) == 0)
        def _():
            s_ref[...] = jnp.zeros_like(s_ref)

        def step(ci, carry):
            sl = pl.ds(pl.multiple_of(ci * HG_CHUNK, HG_CHUNK), HG_CHUNK)
            s_t = s_ref[...]
            st_ref[ci] = s_t
            y, s_new = _hg_chunk(p_ref[0, sl, :], p_ref[1, sl, :], p_ref[2, sl, :], p_ref[3, sl, :],
                                 l0_ref[...], l1_ref[...], l2_ref[...], ng_ref[...], s_t,
                                 cm_ref[...], mk_ref[...])
            y_ref[sl, :] = y.astype(BF16)
            s_ref[...] = s_new
            return carry

        lax.fori_loop(0, n_in, step, 0, unroll=True)

    return pl.pallas_call(
        body, name="hg_fwd", grid=(HG_HEADS, t // HG_ROWS),
        out_shape=(jax.ShapeDtypeStruct((t, D_MODEL), BF16),
                   jax.ShapeDtypeStruct((HG_HEADS, t // HG_CHUNK, HG_DIM, HG_DIM), F32)),
        in_specs=[p_spec, vec, vec, vec, vec, cm_spec, mk_spec], out_specs=(rows, st_spec),
        scratch_shapes=[pltpu.VMEM((HG_DIM, HG_DIM), F32)],
        compiler_params=_cp("parallel", "arbitrary"),
    )(proj, l0, l1, l2, ng, cm, mk)


def _hg_bwd(proj, states, dy, l0, l1, l2, ng):
    t = proj.shape[1]
    n_in = HG_ROWS // HG_CHUNK
    cm, mk = _hg_constants()
    p_spec, vec, cm_spec, mk_spec, rows, st_spec = _hg_specs(t, True)

    def body(p_ref, st_ref, dy_ref, l0_ref, l1_ref, l2_ref, ng_ref, cm_ref, mk_ref,
             dp_ref, dl0_ref, dl1_ref, dl2_ref, dng_ref, ds_ref):
        @pl.when(pl.program_id(1) == 0)
        def _():
            ds_ref[...] = jnp.zeros_like(ds_ref)
            for r in (dl0_ref, dl1_ref, dl2_ref, dng_ref):
                r[...] = jnp.zeros_like(r)

        def step(it, carry):
            ci = n_in - 1 - it
            sl = pl.ds(pl.multiple_of(ci * HG_CHUNK, HG_CHUNK), HG_CHUNK)
            fn = functools.partial(_hg_chunk, cm=cm_ref[...], mk=mk_ref[...])
            _, vjp = jax.vjp(fn, p_ref[0, sl, :], p_ref[1, sl, :], p_ref[2, sl, :], p_ref[3, sl, :],
                             l0_ref[...], l1_ref[...], l2_ref[...], ng_ref[...], st_ref[ci])
            dq, df, di, dg, d0, d1, d2, dn, ds = vjp((dy_ref[sl, :], ds_ref[...]))
            dp_ref[0, sl, :] = dq.astype(BF16)
            dp_ref[1, sl, :] = df.astype(BF16)
            dp_ref[2, sl, :] = di.astype(BF16)
            dp_ref[3, sl, :] = dg.astype(BF16)
            dl0_ref[...] += d0
            dl1_ref[...] += d1
            dl2_ref[...] += d2
            dng_ref[...] += dn
            ds_ref[...] = ds
            return carry

        lax.fori_loop(0, n_in, step, 0, unroll=True)

    v_shape = jax.ShapeDtypeStruct((1, D_MODEL), F32)
    return pl.pallas_call(
        body, name="hg_bwd", grid=(HG_HEADS, t // HG_ROWS),
        out_shape=(jax.ShapeDtypeStruct((4, t, D_MODEL), BF16), v_shape, v_shape, v_shape, v_shape),
        in_specs=[p_spec, st_spec, rows, vec, vec, vec, vec, cm_spec, mk_spec],
        out_specs=(p_spec, vec, vec, vec, vec),
        scratch_shapes=[pltpu.VMEM((HG_DIM, HG_DIM), F32)],
        compiler_params=_cp("parallel", "arbitrary"),
    )(proj, states, dy, l0, l1, l2, ng, cm, mk)


_SQRT_HALF = 0.7071067811865476
_INV_SQRT_2PI = 0.3989422804014327


def _gelu(x):
    return 0.5 * x * (1.0 + lax.erf(x * _SQRT_HALF))


def _gelu_grad(x):
    return 0.5 * (1.0 + lax.erf(x * _SQRT_HALF)) + x * (_INV_SQRT_2PI * jnp.exp(-0.5 * x * x))


def _tril(n):
    return (lax.broadcasted_iota(jnp.int32, (n, n), 0) >= lax.broadcasted_iota(jnp.int32, (n, n), 1)).astype(F32)


def _sg_specs(d):
    row = lambda w: pl.BlockSpec((SG_CHUNK, w), lambda i: (i, 0))
    vec = pl.BlockSpec((1, d), lambda i: (0, 0))
    cube = pl.BlockSpec((SG_GROUPS, SG_CHUNK, SG_CHUNK), lambda i: (0, 0, 0))
    return row, vec, cube


def _sg_fwd(pre, ln_g, ln_b, w_s, bias):
    t = pre.shape[0]
    d = D_MODEL
    row, vec, cube = _sg_specs(d)

    def body(pre_ref, g_ref, b_ref, w_ref, bias_ref, y_ref, vln_ref):
        u = _gelu(pre_ref[:, :d])
        vhat, _ = _ln_stats(_gelu(pre_ref[:, d:]))
        vln_ref[...] = vhat * g_ref[...] + b_ref[...]
        tril = _tril(SG_CHUNK)
        for g in range(SG_GROUPS):
            cs = slice(g * SG_DIM, (g + 1) * SG_DIM)
            gate = _bdot_raw(w_ref[g] * tril, vln_ref[:, cs], "nn") + bias_ref[g]
            y_ref[:, cs] = (u[:, cs] * gate).astype(BF16)

    return pl.pallas_call(
        body, name="sg_fwd", grid=(t // SG_CHUNK,), out_shape=jax.ShapeDtypeStruct((t, d), BF16),
        in_specs=[row(2 * d), vec, vec, cube, cube], out_specs=row(d),
        scratch_shapes=[pltpu.VMEM((SG_CHUNK, d), F32)], compiler_params=_cp("parallel"),
    )(pre, ln_g, ln_b, w_s, bias)


def _sg_bwd(pre, dy, ln_g, ln_b, w_s, bias):
    t = pre.shape[0]
    d = D_MODEL
    row, vec, cube = _sg_specs(d)
    dbs_spec = pl.BlockSpec((SG_GROUPS, 8, SG_CHUNK), lambda i: (0, 0, 0))

    def body(pre_ref, dy_ref, g_ref, b_ref, w_ref, bias_ref, dpre_ref, dw_ref, dbs_ref, dlg_ref, dlb_ref,
             vln_ref, dvln_ref):
        @pl.when(pl.program_id(0) == 0)
        def _():
            for r in (dw_ref, dbs_ref, dlg_ref, dlb_ref):
                r[...] = jnp.zeros_like(r)

        pu = pre_ref[:, :d]
        pv = pre_ref[:, d:]
        u = _gelu(pu)
        vhat, rstd = _ln_stats(_gelu(pv))
        vln_ref[...] = vhat * g_ref[...] + b_ref[...]
        tril = _tril(SG_CHUNK)
        ones = jnp.ones((8, SG_DIM), F32)
        for g in range(SG_GROUPS):
            cs = slice(g * SG_DIM, (g + 1) * SG_DIM)
            wc = w_ref[g] * tril
            vg = vln_ref[:, cs]
            gate = _bdot_raw(wc, vg, "nn") + bias_ref[g]
            dyg = dy_ref[:, cs]
            dgate = dyg * u[:, cs]
            dpre_ref[:, cs] = (dyg * gate * _gelu_grad(pu[:, cs])).astype(BF16)
            dw_ref[g] += tril * _bdot_raw(dgate, vg, "nt")
            dbs_ref[g] += lax.dot_general(ones, dgate, _DIMS["nt"], precision=HI, preferred_element_type=F32)
            dvln_ref[:, cs] = _bdot_raw(wc, dgate, "tn")
        dvln = dvln_ref[...]
        dv, dg, db = _ln_bwd_math(vhat, rstd, dvln, g_ref[...])
        dlg_ref[...] += dg
        dlb_ref[...] += db
        dpre_ref[:, d:] = (dv * _gelu_grad(pv)).astype(BF16)

    return pl.pallas_call(
        body, name="sg_bwd", grid=(t // SG_CHUNK,),
        out_shape=(jax.ShapeDtypeStruct((t, 2 * d), BF16), jax.ShapeDtypeStruct(w_s.shape, F32),
                   jax.ShapeDtypeStruct((SG_GROUPS, 8, SG_CHUNK), F32),
                   jax.ShapeDtypeStruct((1, d), F32), jax.ShapeDtypeStruct((1, d), F32)),
        in_specs=[row(2 * d), row(d), vec, vec, cube, cube],
        out_specs=(row(2 * d), cube, dbs_spec, vec, vec),
        scratch_shapes=[pltpu.VMEM((SG_CHUNK, d), F32), pltpu.VMEM((SG_CHUNK, d), F32)],
        compiler_params=_cp("arbitrary"),
    )(pre, dy, ln_g, ln_b, w_s, bias)


FFN_ROWS = 256
FFN_COLS = 1408
HALO = 16


def _ffn_conv(a_ext, w_ref, cb_ref):
    a1 = pltpu.roll(a_ext, 1, 0)
    a2 = pltpu.roll(a_ext, 2, 0)
    return w_ref[0:1, :] * a2 + w_ref[1:2, :] * a1 + w_ref[2:3, :] * a_ext + cb_ref[...], a1, a2


def _ffn_fwd(hh, conv_w, conv_b):
    t = hh.shape[1]
    nb8 = FFN_ROWS // HALO
    main = pl.BlockSpec((2, FFN_ROWS, FFN_COLS), lambda c, r: (0, r, c))
    prev = pl.BlockSpec((None, HALO, FFN_COLS), lambda c, r: (0, jnp.maximum(r * nb8 - 1, 0), c))
    wspec = pl.BlockSpec((3, FFN_COLS), lambda c, r: (0, c))
    bspec = pl.BlockSpec((1, FFN_COLS), lambda c, r: (0, c))

    def body(m_ref, p_ref, w_ref, cb_ref, z_ref):
        prev = jnp.where(pl.program_id(1) == 0, 0.0, p_ref[...].astype(F32))
        a_ext = jnp.concatenate([prev, m_ref[0].astype(F32)], axis=0)
        a, _, _ = _ffn_conv(a_ext, w_ref, cb_ref)
        a = a[HALO:]
        z_ref[...] = (a * jax.nn.sigmoid(a) * m_ref[1].astype(F32)).astype(BF16)

    return pl.pallas_call(
        body, name="ffn_fwd", grid=(D_FF // FFN_COLS, t // FFN_ROWS),
        out_shape=jax.ShapeDtypeStruct((t, D_FF), BF16),
        in_specs=[main, prev, wspec, bspec],
        out_specs=pl.BlockSpec((FFN_ROWS, FFN_COLS), lambda c, r: (r, c)),
        compiler_params=_cp("parallel", "parallel"),
    )(hh, hh, conv_w, conv_b)


def _ffn_bwd(hh, dz, conv_w, conv_b):
    t = hh.shape[1]
    nb8 = FFN_ROWS // HALO
    last8 = t // HALO - 1
    nr = t // FFN_ROWS
    main = pl.BlockSpec((2, FFN_ROWS, FFN_COLS), lambda c, r: (0, r, c))
    prev = pl.BlockSpec((None, HALO, FFN_COLS), lambda c, r: (0, jnp.maximum(r * nb8 - 1, 0), c))
    nxt = pl.BlockSpec((2, HALO, FFN_COLS), lambda c, r: (0, jnp.minimum((r + 1) * nb8, last8), c))
    dmain = pl.BlockSpec((FFN_ROWS, FFN_COLS), lambda c, r: (r, c))
    dnxt = pl.BlockSpec((HALO, FFN_COLS), lambda c, r: (jnp.minimum((r + 1) * nb8, last8), c))
    wspec = pl.BlockSpec((3, FFN_COLS), lambda c, r: (0, c))
    bspec = pl.BlockSpec((1, FFN_COLS), lambda c, r: (0, c))

    def body(m_ref, p_ref, n_ref, dz_ref, dzn_ref, w_ref, cb_ref, dh_ref, dw_ref, dcb_ref):
        r = pl.program_id(1)

        @pl.when(r == 0)
        def _():
            dw_ref[...] = jnp.zeros_like(dw_ref)
            dcb_ref[...] = jnp.zeros_like(dcb_ref)

        prev = jnp.where(r == 0, 0.0, p_ref[...].astype(F32))
        a_ext = jnp.concatenate([prev, m_ref[0].astype(F32), n_ref[0].astype(F32)], axis=0)
        a, a1, a2 = _ffn_conv(a_ext, w_ref, cb_ref)
        a = a[HALO:]
        b_ext = jnp.concatenate([m_ref[1], n_ref[1]], axis=0).astype(F32)
        dz_main = dz_ref[...].astype(F32)
        dz_ext = jnp.concatenate([dz_main, jnp.where(r == nr - 1, 0.0, dzn_ref[...].astype(F32))], axis=0)
        sig = jax.nn.sigmoid(a)
        da = dz_ext * b_ext * (sig * (1.0 + a * (1.0 - sig)))
        n_ext = FFN_ROWS + HALO
        da_p1 = pltpu.roll(da, n_ext - 1, 0)
        da_p2 = pltpu.roll(da, n_ext - 2, 0)
        da_raw = w_ref[2:3, :] * da + w_ref[1:2, :] * da_p1 + w_ref[0:1, :] * da_p2
        dh_ref[0] = da_raw[:FFN_ROWS].astype(BF16)
        dh_ref[1] = (dz_main * (a * sig)[:FFN_ROWS]).astype(BF16)
        dam = da[:FFN_ROWS]
        rows = slice(HALO, HALO + FFN_ROWS)
        dw_ref[0:1, :] += jnp.sum(dam * a2[rows], axis=0, keepdims=True)
        dw_ref[1:2, :] += jnp.sum(dam * a1[rows], axis=0, keepdims=True)
        dw_ref[2:3, :] += jnp.sum(dam * a_ext[rows], axis=0, keepdims=True)
        dcb_ref[...] += jnp.sum(dam, axis=0, keepdims=True)

    return pl.pallas_call(
        body, name="ffn_bwd", grid=(D_FF // FFN_COLS, nr),
        out_shape=(jax.ShapeDtypeStruct((2, t, D_FF), BF16), jax.ShapeDtypeStruct((3, D_FF), F32),
                   jax.ShapeDtypeStruct((1, D_FF), F32)),
        in_specs=[main, prev, nxt, dmain, dnxt, wspec, bspec],
        out_specs=(main, wspec, bspec), compiler_params=_cp("parallel", "arbitrary"),
    )(hh, hh, hh, dz, dz, conv_w, conv_b)


def _adamw_math(w, g, m, v):
    m = ADAM_B1 * m + (1.0 - ADAM_B1) * g
    v = ADAM_B2 * v + (1.0 - ADAM_B2) * (g * g)
    m_hat = m / (1.0 - ADAM_B1 ** ADAM_STEP)
    v_hat = v / (1.0 - ADAM_B2 ** ADAM_STEP)
    delta = -ADAM_LR * (m_hat / (jnp.sqrt(v_hat) + ADAM_EPS) + ADAM_WD * w)
    return delta, m, v


def _as2d(shape):
    n = int(np.prod(shape))
    c = shape[-1] if shape[-1] % 128 == 0 else (128 if n % 128 == 0 else shape[-1])
    return n // c, c


def _adamw(w, g, m, v, name, parts=None, own=None, is_me=None, layer=None, prev=None):
    shape = w.shape if layer is None else w.shape[1:]
    r, c = _as2d(shape)
    br = r if r <= 512 else _pick(r, (256, 176, 128, 64, 8))
    blk = pl.BlockSpec((br, c), lambda i: (i, 0))
    if layer is None:
        wblk, o = blk, jax.ShapeDtypeStruct((r, c), F32)
        w2, m2, v2 = (a.reshape(r, c) for a in (w, m, v))
    else:
        wblk, o = pl.BlockSpec((None, br, c), lambda i: (layer, i, 0)), jax.ShapeDtypeStruct((w.shape[0], r, c), F32)
        w2, m2, v2 = (a.reshape(w.shape[0], r, c) for a in (w, m, v))
    extra, extra_specs = [], []
    if parts is not None:
        g_in = parts.reshape(N_DEV, r, c)
        g_spec = pl.BlockSpec((N_DEV, br, c), lambda i: (0, i, 0))
        extra = [own.reshape(r, c), is_me]
        extra_specs = [blk, pl.BlockSpec(memory_space=pltpu.SMEM)]
    else:
        g_in = g.reshape(r, c)
        g_spec = blk
    aliases = {}
    if prev is not None:
        aliases = {4 + len(extra) + j: j for j in range(4)}
        extra = extra + [a.reshape(o.shape) for a in prev]
        extra_specs = extra_specs + [ANY] * 4

    def body(w_ref, g_ref, m_ref, v_ref, *rest):
        go_ref, d_ref, mo_ref, vo_ref = rest[-4:]
        if parts is not None:
            own_ref, me_ref = rest[:2]
            gsum = None
            for i in range(N_DEV):
                term = jnp.where(me_ref[i] > 0.5, own_ref[...], g_ref[i]).astype(F32)
                gsum = term if gsum is None else gsum + term
        else:
            gsum = g_ref[...]
        delta, m_new, v_new = _adamw_math(w_ref[...], gsum, m_ref[...], v_ref[...])
        go_ref[...] = gsum
        d_ref[...] = delta
        mo_ref[...] = m_new
        vo_ref[...] = v_new

    outs = pl.pallas_call(
        body, name=name, grid=(r // br,), out_shape=(o, o, o, o),
        in_specs=[wblk, g_spec, wblk, wblk] + extra_specs, out_specs=(wblk, wblk, wblk, wblk),
        input_output_aliases=aliases, compiler_params=_cp("parallel"),
    )(w2, g_in, m2, v2, *extra)
    return tuple(a.reshape(w.shape) for a in outs)


ALL_PEERS = tuple(range(1, N_DEV))
OTHER_CHIPS = (2, 4, 6)
SIBLING = (1,)


def _me_and_peers(relations=ALL_PEERS):
    x, y, c = lax.axis_index("x"), lax.axis_index("y"), lax.axis_index("c")
    me = 4 * x + 2 * y + c
    peers = []
    for k in relations:
        kx, ky, kc = (k >> 2) & 1, (k >> 1) & 1, k & 1
        px, py, pc = x ^ kx, y ^ ky, c ^ kc
        peers.append(((px, py, pc), 4 * px + 2 * py + pc))
    return me, peers


def _all_gather(shard, name):
    def body(x_ref, o_ref, send_sems, recv_sems, local_sem):
        me, peers = _me_and_peers()
        mine = pltpu.make_async_copy(x_ref, o_ref.at[me], local_sem)
        mine.start()
        sends = []
        for k, (dev, _) in enumerate(peers):
            cp = pltpu.make_async_remote_copy(src_ref=x_ref, dst_ref=o_ref.at[me], send_sem=send_sems.at[k],
                                              recv_sem=recv_sems.at[k], device_id=dev, device_id_type=MESH)
            cp.start()
            sends.append(cp)
        for k, (dev, idx) in enumerate(peers):
            pltpu.make_async_remote_copy(src_ref=x_ref, dst_ref=o_ref.at[idx], send_sem=send_sems.at[k],
                                         recv_sem=recv_sems.at[k], device_id=dev, device_id_type=MESH).wait_recv()
        for cp in sends:
            cp.wait_send()
        mine.wait()

    return pl.pallas_call(
        body, name=name, out_shape=jax.ShapeDtypeStruct((N_DEV,) + shard.shape, shard.dtype),
        in_specs=[ANY], out_specs=ANY,
        scratch_shapes=[pltpu.SemaphoreType.DMA((N_DEV - 1,)), pltpu.SemaphoreType.DMA((N_DEV - 1,)),
                        pltpu.SemaphoreType.DMA],
        compiler_params=pltpu.CompilerParams(has_side_effects=True),
    )(shard)


HBM = pl.BlockSpec(memory_space=pltpu.HBM)
SEM = pl.BlockSpec(memory_space=pltpu.SEMAPHORE)
EFFECT = pltpu.SideEffectType.DATAFLOW_SIDE_EFFECTING


def _xchg_start(parts, land, *, dep, name, peers=ALL_PEERS, pieces=(0,)):
    n_buf = 1 if parts is None else 2

    def body(*refs):
        land_ref = refs[n_buf - 1]
        send_sem, recv_sem = refs[n_buf + 1], refs[n_buf + 2]
        token = refs[2 * n_buf + 3]
        me, to = _me_and_peers(peers)
        for dev, idx in to:
            if parts is None:
                for r in pieces:
                    piece = land_ref.at[me ^ r]
                    pltpu.make_async_remote_copy(src_ref=piece, dst_ref=piece, send_sem=send_sem, recv_sem=recv_sem,
                                                 device_id=dev, device_id_type=MESH).start()
            else:
                pltpu.make_async_remote_copy(src_ref=refs[0].at[idx], dst_ref=land_ref.at[me], send_sem=send_sem,
                                             recv_sem=recv_sem, device_id=dev, device_id_type=MESH).start()
        token[...] = jnp.zeros_like(token)

    bufs = [land] if parts is None else [parts, land]
    return pl.pallas_call(
        body, name=name,
        out_shape=(pltpu.SemaphoreType.DMA(()), pltpu.SemaphoreType.DMA(()),
                   *[pltpu.HBM(b.shape, b.dtype) for b in bufs], jax.ShapeDtypeStruct((8, 128), F32)),
        in_specs=(*[HBM] * n_buf, ANY), out_specs=(SEM, SEM, *[HBM] * n_buf, pl.BlockSpec(memory_space=pltpu.VMEM)),
        input_output_aliases={i: 2 + i for i in range(n_buf)},
        compiler_params=pltpu.CompilerParams(has_side_effects=EFFECT),
    )(*[pltpu.with_memory_space_constraint(b, pltpu.HBM) for b in bufs], dep)


def _xchg_wait(handles, after, name, n_pieces=N_DEV - 1):
    send_sem, recv_sem, *bufs, _ = handles
    n_buf = len(bufs)

    def body(*refs):
        land_ref = refs[n_buf - 1]
        send_sem, recv_sem = refs[n_buf], refs[n_buf + 1]
        x, y, c = lax.axis_index("x"), lax.axis_index("y"), lax.axis_index("c")
        span = land_ref.at[pl.ds(0, n_pieces)]
        cp = pltpu.make_async_remote_copy(src_ref=span, dst_ref=span, send_sem=send_sem, recv_sem=recv_sem,
                                          device_id=(x, y, c), device_id_type=MESH)
        cp.wait_send()
        cp.wait_recv()

    return pl.pallas_call(
        body, name=name, out_shape=tuple(pltpu.HBM(b.shape, b.dtype) for b in bufs),
        in_specs=(*[HBM] * n_buf, SEM, SEM, ANY), out_specs=tuple([HBM] * n_buf),
        input_output_aliases={i: i for i in range(n_buf)},
        compiler_params=pltpu.CompilerParams(has_side_effects=EFFECT),
    )(*bufs, send_sem, recv_sem, after)


def _sum_devices(v, name):
    _, r, c = v.shape

    def body(x_ref, o_ref):
        acc = x_ref[0]
        for i in range(1, N_DEV):
            acc = acc + x_ref[i]
        o_ref[...] = acc

    return pl.pallas_call(
        body, name=name, out_shape=jax.ShapeDtypeStruct((r, c), F32),
        in_specs=[pl.BlockSpec(memory_space=pltpu.VMEM)], out_specs=pl.BlockSpec(memory_space=pltpu.VMEM),
        compiler_params=pltpu.CompilerParams(vmem_limit_bytes=VMEM_LIMIT),
    )(v)


def _local_step(x, target, lb_logits, p, weight, send_grad, send_small):
    g = {}
    l0, l1, l2 = lb_logits[0:1], lb_logits[1:2], lb_logits[2:3]
    x_bf = x.astype(BF16)

    proj0 = _mm(x_bf, weight("hg_w_in", x_bf), mode="nn", b_cs=True, o_cs=4, name="mm_hg_in")
    y0, states = _hg_fwd(proj0, l0, l1, l2, p["hg_norm_g"])
    mixed0 = _mm(y0, weight("hg_w_out", y0), mode="nn", name="mm_hg_out")
    hh_l, z_l = [], []

    def ffn_forward(layer, h_in_bf):
        hh = _mm(h_in_bf, weight(f"ffn_w_up{layer}", h_in_bf), mode="nn", b_cs=True, o_cs=2, out_dtype=BF16,
                 name=f"mm_up{layer}")
        z = _ffn_fwd(hh, p["ffn_conv_w"][layer], p["ffn_conv_b"][layer])
        ffn = _mm(z, weight(f"ffn_w_down{layer}", z), mode="nn", name=f"mm_down{layer}")
        hh_l.append(hh)
        z_l.append(z)
        return ffn

    h1, h1b = _ln_fwd(x, mixed0, p["ln1_g"][0], p["ln1_b"][0], "ln1_0")
    ffn0 = ffn_forward(0, h1b)
    h2, h2b = _ln_fwd(h1, ffn0, p["ln2_g"][0], p["ln2_b"][0], "ln2_0")
    pre1 = _mm(h2b, weight("sg_w_in", h2b), mode="nn", b_cs=True, name="mm_sg_in")
    y1 = _sg_fwd(pre1, p["sg_ln_g"], p["sg_ln_b"], p["sg_w_s"], p["sg_bias"])
    mixed1 = _mm(y1, weight("sg_w_out", y1), mode="nn", name="mm_sg_out")
    h3, h3b = _ln_fwd(h2, mixed1, p["ln1_g"][1], p["ln1_b"][1], "ln1_1")
    ffn1 = ffn_forward(1, h3b)

    loss, ds, dres, g["ln2_g1"], g["ln2_b1"] = _ln_loss_bwd(h3, ffn1, target, p["ln2_g"][1], p["ln2_b"][1], "ln2_1_loss")

    def ffn_backward(layer, ds_bf, h_in_bf):
        tok = send_grad(f"ffn_w_down{layer}",
                        _mm(z_l[layer], ds_bf, mode="tn", out_dtype=BF16, name=f"mm_dw_down{layer}"))
        dz = _mm(ds_bf, weight(f"ffn_w_down{layer}", None), mode="nt", out_dtype=BF16, name=f"mm_d_z{layer}",
                 deps=(tok,))
        dhh, g[f"conv_w{layer}"], g[f"conv_b{layer}"] = _ffn_bwd(hh_l[layer], dz, p["ffn_conv_w"][layer],
                                                                   p["ffn_conv_b"][layer])
        tok = send_grad(f"ffn_w_up{layer}", _mm(h_in_bf, dhh, mode="tn", b_cs=True, o_cs=N_DEV, out_dtype=BF16,
                                                name=f"mm_dw_up{layer}"))
        return _mm(dhh, weight(f"ffn_w_up{layer}", None), mode="nt", a_cs=True, b_cs=True, name=f"mm_d_up{layer}",
                   deps=(tok,))

    dh3 = ffn_backward(1, ds, h3b)
    ds, dres, g["ln1_g1"], g["ln1_b1"] = _ln_bwd(h2, mixed1, dres, dh3, p["ln1_g"][1], "ln1_1_bwd")
    tok = send_grad("sg_w_out", _mm(y1, ds, mode="tn", out_dtype=BF16, name="mm_dw_sg_out"))
    dy1 = _mm(ds, weight("sg_w_out", None), mode="nt", name="mm_d_sg_out", deps=(tok,))
    dpre1, g["sg_w_s"], dbs, g["sg_ln_g"], g["sg_ln_b"] = _sg_bwd(pre1, dy1, p["sg_ln_g"], p["sg_ln_b"],
                                                                    p["sg_w_s"], p["sg_bias"])
    g["sg_b_s"] = dbs[:, 0, :]
    tok = send_grad("sg_w_in", _mm(h2b, dpre1, mode="tn", o_cs=N_DEV, out_dtype=BF16, name="mm_dw_sg_in"))
    dh2 = _mm(dpre1, weight("sg_w_in", None), mode="nt", b_cs=True, name="mm_d_sg_in", deps=(tok,))
    ds, dres, g["ln2_g0"], g["ln2_b0"] = _ln_bwd(h1, ffn0, dres, dh2, p["ln2_g"][0], "ln2_0_bwd")
    dh1 = ffn_backward(0, ds, h1b)
    ds, dres, g["ln1_g0"], g["ln1_b0"] = _ln_bwd(x, mixed0, dres, dh1, p["ln1_g"][0], "ln1_0_bwd")
    tok = send_grad("hg_w_out", _mm(y0, ds, mode="tn", out_dtype=BF16, name="mm_dw_hg_out"))
    dy0 = _mm(ds, weight("hg_w_out", None), mode="nt", name="mm_d_hg_out", deps=(tok,))
    dproj, d0, d1, d2, g["hg_norm_g"] = _hg_bwd(proj0, states, dy0, l0, l1, l2, p["hg_norm_g"])
    g["lb_logits"] = jnp.concatenate([d0, d1, d2], axis=0)
    tok = send_small(g)
    tok = send_grad("hg_w_in", _mm(x_bf, dproj, mode="tn", b_cs=True, o_cs=N_DEV, out_dtype=BF16, name="mm_dw_hg_in",
                                   deps=(tok,)))
    grad_x = _mm(dproj, weight("hg_w_in", None), mode="nt", a_cs=True, b_cs=True, addend=dres, name="mm_d_hg_in",
                 deps=(tok,))
    return loss[0, 0], grad_x


_SMALL = ("lb_logits", "hg_norm_g", "sg_ln_g", "sg_ln_b", "sg_w_s", "sg_b_s", "ffn_conv_w", "ffn_conv_b",
          "ln1_g", "ln1_b", "ln2_g", "ln2_b")
_NAMES = ("lb_logits", "hg_w_in", "hg_norm_g", "hg_w_out", "sg_w_in", "sg_ln_g", "sg_ln_b", "sg_w_s", "sg_b_s",
          "sg_w_out", "ffn_w_up", "ffn_conv_w", "ffn_conv_b", "ffn_w_down", "ln1_g", "ln1_b", "ln2_g", "ln2_b")


def kernel(x, lb_logits, hg_w_in, hg_norm_g, hg_w_out, sg_w_in, sg_ln_g, sg_ln_b, sg_w_s, sg_b_s, sg_w_out, ffn_w_up, ffn_conv_w, ffn_conv_b, ffn_w_down, ln1_g, ln1_b, ln2_g, ln2_b, loss_target, m_lb_logits, m_hg_w_in, m_hg_norm_g, m_hg_w_out, m_sg_w_in, m_sg_ln_g, m_sg_ln_b, m_sg_w_s, m_sg_b_s, m_sg_w_out, m_ffn_w_up, m_ffn_conv_w, m_ffn_conv_b, m_ffn_w_down, m_ln1_g, m_ln1_b, m_ln2_g, m_ln2_b, v_lb_logits, v_hg_w_in, v_hg_norm_g, v_hg_w_out, v_sg_w_in, v_sg_ln_g, v_sg_ln_b, v_sg_w_s, v_sg_b_s, v_sg_w_out, v_ffn_w_up, v_ffn_conv_w, v_ffn_conv_b, v_ffn_w_down, v_ln1_g, v_ln1_b, v_ln2_g, v_ln2_b):
    w = dict(lb_logits=lb_logits, hg_w_in=hg_w_in, hg_norm_g=hg_norm_g, hg_w_out=hg_w_out, sg_w_in=sg_w_in,
             sg_ln_g=sg_ln_g, sg_ln_b=sg_ln_b, sg_w_s=sg_w_s, sg_b_s=sg_b_s, sg_w_out=sg_w_out, ffn_w_up=ffn_w_up,
             ffn_conv_w=ffn_conv_w, ffn_conv_b=ffn_conv_b, ffn_w_down=ffn_w_down, ln1_g=ln1_g, ln1_b=ln1_b,
             ln2_g=ln2_g, ln2_b=ln2_b)
    m = dict(lb_logits=m_lb_logits, hg_w_in=m_hg_w_in, hg_norm_g=m_hg_norm_g, hg_w_out=m_hg_w_out,
             sg_w_in=m_sg_w_in, sg_ln_g=m_sg_ln_g, sg_ln_b=m_sg_ln_b, sg_w_s=m_sg_w_s, sg_b_s=m_sg_b_s,
             sg_w_out=m_sg_w_out, ffn_w_up=m_ffn_w_up, ffn_conv_w=m_ffn_conv_w, ffn_conv_b=m_ffn_conv_b,
             ffn_w_down=m_ffn_w_down, ln1_g=m_ln1_g, ln1_b=m_ln1_b, ln2_g=m_ln2_g, ln2_b=m_ln2_b)
    v = dict(lb_logits=v_lb_logits, hg_w_in=v_hg_w_in, hg_norm_g=v_hg_norm_g, hg_w_out=v_hg_w_out,
             sg_w_in=v_sg_w_in, sg_ln_g=v_sg_ln_g, sg_ln_b=v_sg_ln_b, sg_w_s=v_sg_w_s, sg_b_s=v_sg_b_s,
             sg_w_out=v_sg_w_out, ffn_w_up=v_ffn_w_up, ffn_conv_w=v_ffn_conv_w, ffn_conv_b=v_ffn_conv_b,
             ffn_w_down=v_ffn_w_down, ln1_g=v_ln1_g, ln1_b=v_ln1_b, ln2_g=v_ln2_g, ln2_b=v_ln2_b)
    me = 4 * lax.axis_index("x") + 2 * lax.axis_index("y") + lax.axis_index("c")
    is_me = (jnp.arange(N_DEV) == me).astype(F32)
    d = D_MODEL

    shards = [("hg_w_in", hg_w_in[0]), ("hg_w_out", hg_w_out[0]), ("ffn_w_up0", ffn_w_up[0]),
              ("ffn_w_down0", ffn_w_down[0]), ("sg_w_in", sg_w_in[0]), ("sg_w_out", sg_w_out[0]),
              ("ffn_w_up1", ffn_w_up[1]), ("ffn_w_down1", ffn_w_down[1])]
    row_sharded = {"hg_w_out": (d, d), "sg_w_out": (d, d), "ffn_w_down0": (D_FF, d), "ffn_w_down1": (D_FF, d)}
    sv = jnp.zeros((8, 768), F32)
    sv = sv.at[0, :256].set(sg_ln_g[0]).at[1, :256].set(sg_ln_b[0]).at[2:8, :704].set(ffn_conv_w.reshape(6, 704))
    sv = _all_gather(sv, "ag_small")
    gathers, gathered, dep = {}, {}, sv
    for n, shard in shards:
        land = lax.dynamic_update_index_in_dim(lax.empty((N_DEV,) + shard.shape, BF16), shard.astype(BF16), me, 0)
        gathers[n] = _xchg_start(None, land, dep=dep, name=f"ag_{n}", peers=OTHER_CHIPS)
        dep = gathers[n][-1]
    last_start = dep
    handed = {}
    order = [n for n, _ in shards]

    def hand_over(n, after):
        (land,) = _xchg_wait(gathers[n], after, f"agw_{n}", n_pieces=len(OTHER_CHIPS))
        handed[n] = _xchg_start(None, land, dep=is_me, name=f"ah_{n}", peers=SIBLING, pieces=(0,) + OTHER_CHIPS)

    def weight(n, after):
        if n not in gathered:
            if n == order[0]:
                hand_over(n, last_start)
                after = handed[n][-1]
            nxt = order.index(n) + 1
            if nxt < len(order):
                hand_over(order[nxt], after)
                after = handed[order[nxt]][-1]
            (full,) = _xchg_wait(handed[n], after, f"ahw_{n}", n_pieces=1 + len(OTHER_CHIPS))
            gathered[n] = full.reshape(row_sharded[n]) if n in row_sharded else full
        return gathered[n]

    grad_sends = {}

    def send_grad(n, parts):
        shape = (N_DEV,) + dict(shards)[n].shape
        grad_sends[n] = _xchg_start(parts.reshape(shape), lax.empty(shape, parts.dtype), dep=is_me, name=f"rs_{n}")
        return grad_sends[n][-1]

    small_send = {}

    def send_small(g):
        small = [g["lb_logits"], g["hg_norm_g"], g["sg_ln_g"], g["sg_ln_b"], g["sg_w_s"], g["sg_b_s"],
                 g["conv_w0"], g["conv_w1"], g["conv_b0"], g["conv_b1"], g["ln1_g0"], g["ln1_g1"],
                 g["ln1_b0"], g["ln1_b1"], g["ln2_g0"], g["ln2_g1"], g["ln2_b0"], g["ln2_b1"]]
        flat = jnp.concatenate([a.reshape(-1) for a in small])
        rows = -(-flat.shape[0] // (8 * 128)) * 8
        flat = jnp.pad(flat, (0, rows * 128 - flat.shape[0])).reshape(rows, 128)
        land = lax.dynamic_update_index_in_dim(lax.empty((N_DEV, rows, 128), F32), flat, me, 0)
        small_send["shapes"] = [a.shape for a in small]
        small_send["handles"] = _xchg_start(None, land, dep=is_me, name="ar_small")
        return small_send["handles"][-1]

    p = {}
    p["sg_ln_g"] = sv[:, 0, :256].reshape(1, d)
    p["sg_ln_b"] = sv[:, 1, :256].reshape(1, d)
    conv_w_full = jnp.transpose(sv[:, 2:8, :704].reshape(N_DEV, DEPTH, 3, 704), (1, 2, 0, 3)).reshape(DEPTH, 3, D_FF)
    p["ffn_conv_w"] = [conv_w_full[l] for l in range(DEPTH)]
    p["ffn_conv_b"] = [ffn_conv_b[l:l + 1] for l in range(DEPTH)]
    p["hg_norm_g"] = hg_norm_g
    p["sg_w_s"] = sg_w_s[0]
    p["sg_bias"] = jnp.broadcast_to(sg_b_s[0][:, :, None], (SG_GROUPS, SG_CHUNK, SG_DIM))
    for n in ("ln1_g", "ln1_b", "ln2_g", "ln2_b"):
        p[n] = [w[n][l:l + 1] for l in range(DEPTH)]

    loss_part, grad_x = _local_step(x[0], loss_target[0], lb_logits, p, weight, send_grad, send_small)
    loss = lax.psum(loss_part, ("x", "y", "c"))

    (landed,) = _xchg_wait(small_send["handles"], grad_x, "arw_small")
    red = _sum_devices(landed, "sum_small").reshape(-1)
    shapes = small_send["shapes"]
    offs = np.cumsum([0] + [int(np.prod(sh)) for sh in shapes])
    r = [red[offs[i]:offs[i + 1]].reshape(shapes[i]) for i in range(len(shapes))]
    gs = {}
    gs["lb_logits"] = r[0]
    gs["hg_norm_g"] = r[1]
    gs["sg_ln_g"] = lax.dynamic_slice(r[2], (0, me * 256), (1, 256))
    gs["sg_ln_b"] = lax.dynamic_slice(r[3], (0, me * 256), (1, 256))
    gs["sg_w_s"] = r[4][None]
    gs["sg_b_s"] = r[5][None]
    gs["ffn_conv_w"] = lax.dynamic_slice(jnp.stack([r[6], r[7]]), (0, 0, me * 704), (DEPTH, 3, 704))
    gs["ffn_conv_b"] = jnp.concatenate([r[8], r[9]], axis=0)
    gs["ln1_g"] = jnp.concatenate([r[10], r[11]], axis=0)
    gs["ln1_b"] = jnp.concatenate([r[12], r[13]], axis=0)
    gs["ln2_g"] = jnp.concatenate([r[14], r[15]], axis=0)
    gs["ln2_b"] = jnp.concatenate([r[16], r[17]], axis=0)

    out_g, out_d, out_m, out_v = {}, {}, {}, {}
    for n in _SMALL:
        out_g[n], out_d[n], out_m[n], out_v[n] = _adamw(w[n], gs[n], m[n], v[n], f"adamw_{n}")

    res, after = {}, out_v["ln2_b"]
    for n in ("ffn_w_down1", "ffn_w_up1", "sg_w_out", "sg_w_in", "ffn_w_down0", "ffn_w_up0", "hg_w_out", "hg_w_in"):
        parts, recv = _xchg_wait(grad_sends[n], after, f"rsw_{n}")
        own = lax.dynamic_index_in_dim(parts, me, 0, keepdims=False)
        if n[-1] in "01":
            base, layer = n[:-1], int(n[-1])
            res[n] = _adamw(w[base], None, m[base], v[base], f"adamw_{n}", parts=recv, own=own, is_me=is_me,
                            layer=layer, prev=res.get(base + "1"))
        else:
            res[n] = _adamw(w[n][0], None, m[n][0], v[n][0], f"adamw_{n}", parts=recv, own=own, is_me=is_me)
        after = res[n][3]
    for n in ("hg_w_in", "hg_w_out", "sg_w_in", "sg_w_out"):
        out_g[n], out_d[n], out_m[n], out_v[n] = (a[None] for a in res[n])
    for n in ("ffn_w_up", "ffn_w_down"):
        out_g[n], out_d[n], out_m[n], out_v[n] = res[n + "0"]

    return (loss, grad_x[None], *[out_g[n] for n in _NAMES], *[out_d[n] for n in _NAMES],
            *[out_m[n] for n in _NAMES], *[out_v[n] for n in _NAMES])
```

```python
import functools

import numpy as np
import jax
import jax.numpy as jnp
from jax import lax
from jax.experimental import pallas as pl
from jax.experimental.pallas import tpu as pltpu

F32 = jnp.float32
BF16 = jnp.bfloat16
HI = lax.Precision.HIGHEST

N_DEV = 8
D_MODEL = 2048
HG_HEADS = 16
HG_DIM = 128
HG_CHUNK = 128
SG_GROUPS = 16
SG_DIM = 128
SG_CHUNK = 128
D_FF = 5632
DEPTH = 2
ALPHA = (2 * DEPTH) ** 0.25
LN_EPS = 1e-5
RMS_EPS = 1e-6
ADAM_LR = 0.001
ADAM_B1 = 0.9
ADAM_B2 = 0.999
ADAM_EPS = 1e-08
ADAM_WD = 0.01
ADAM_STEP = 10

VMEM_LIMIT = 56 * 1024 * 1024
MESH = pl.DeviceIdType.MESH
ANY = pl.BlockSpec(memory_space=pl.ANY)


def _cp(*sem):
    return pltpu.CompilerParams(dimension_semantics=sem, vmem_limit_bytes=VMEM_LIMIT)


def _pick(n, cands):
    for c in cands:
        if n % c == 0:
            return c
    raise ValueError(f"no tile for {n} in {cands}")


_DIMS = {"nn": (((1,), (0,)), ((), ())), "nt": (((1,), (1,)), ((), ())), "tn": (((0,), (0,)), ((), ()))}


MM_VMEM_BUDGET = 44 * 1024 * 1024


def _mm_tiles(m, n, kk, n_div, k_div, out_bytes, has_addend, k_piece, transposed_a):
    best = None
    for bm in (1024, 1408, 512, 256):
        if m % bm:
            continue
        for bn in (1024, 1408, 512, 256, 128):
            if n % bn or any(d % bn for d in n_div):
                continue
            for bk in (kk, 5632, 4096, 2816, 2048, 1408, 1024, 512, 256, 128):
                if bk > kk or kk % bk or any(d % bk for d in k_div):
                    continue
                if k_piece and bk % k_piece:
                    continue
                nk = kk // bk
                vmem = 4 * (bm * bk + bk * bn) + 2 * bm * bn * out_bytes + (nk > 1) * 4 * bm * bn
                vmem += has_addend * 8 * bm * bn + transposed_a * 2 * bm * bk
                vmem += (out_bytes < 4) * 4 * bm * bn
                if vmem > MM_VMEM_BUDGET:
                    continue
                score = (max(nk, 2), -(bm * bn), -bk)
                if best is None or score < best[0]:
                    best = (score, bm, bn, bk)
    if best is None:
        raise ValueError(f"no tiles for {(m, n, kk)}")
    return best[1:]


def _mm(a, b, *, mode, name, out_dtype=F32, a_cs=False, b_cs=False, o_cs=None, addend=None, deps=()):
    if mode == "tn":
        kk, m = a.shape
    elif a_cs:
        m, kk = a.shape[1], a.shape[0] * a.shape[2]
    else:
        m, kk = a.shape
    if mode == "nt":
        n = b.shape[1] if b_cs else b.shape[0]
    else:
        n = b.shape[0] * b.shape[2] if b_cs else b.shape[1]
    a_c = a.shape[2] if a_cs else None
    b_c = b.shape[2] if b_cs else None
    o_c = n // o_cs if o_cs else None
    k_piece = b_c if (b_cs and mode == "nt") else None
    n_div = [c for c in (b_c if (b_cs and mode != "nt") else None, o_c) if c]
    k_div = [a_c] if a_c else []
    bm, bn, bk = _mm_tiles(m, n, kk, n_div, k_div, jnp.dtype(out_dtype).itemsize, addend is not None, k_piece,
                           mode == "tn")
    nk = kk // bk
    kb = bk // k_piece if k_piece else 1

    def cs_idx(blk, per):
        return (blk * per[0]) // per[1], (blk * per[0] % per[1]) // per[0]

    if mode == "tn":
        a_spec = pl.BlockSpec((bk, bm), lambda i, j, k: (k, i))
    elif a_cs:
        def a_map(i, j, k):
            s, r = cs_idx(k, (bk, a_c))
            return (s, i, r)
        a_spec = pl.BlockSpec((None, bm, bk), a_map)
    else:
        a_spec = pl.BlockSpec((bm, bk), lambda i, j, k: (i, k))
    if mode == "nt":
        if b_cs:
            b_spec = pl.BlockSpec((kb, bn, b_c), lambda i, j, k: (k, j, 0))
        else:
            b_spec = pl.BlockSpec((bn, bk), lambda i, j, k: (j, k))
    else:
        if b_cs:
            def b_map(i, j, k):
                s, r = cs_idx(j, (bn, b_c))
                return (s, k, r)
            b_spec = pl.BlockSpec((None, bk, bn), b_map)
        else:
            b_spec = pl.BlockSpec((bk, bn), lambda i, j, k: (k, j))
    if o_cs:
        def o_map(i, j, k):
            s, r = cs_idx(j, (bn, o_c))
            return (s, i, r)
        o_spec = pl.BlockSpec((None, bm, bn), o_map)
        out_shape = jax.ShapeDtypeStruct((o_cs, m, o_c), out_dtype)
    else:
        o_spec = pl.BlockSpec((bm, bn), lambda i, j, k: (i, j))
        out_shape = jax.ShapeDtypeStruct((m, n), out_dtype)
    in_specs = [a_spec, b_spec]
    args = [a, b]
    if addend is not None:
        in_specs.append(pl.BlockSpec((bm, bn), lambda i, j, k: (i, j)))
        args.append(addend)
    dims = _DIMS[mode]
    in_specs += [ANY] * len(deps)
    args += list(deps)
    n_in = len(args)

    def body(*refs):
        a_ref, b_ref = refs[0], refs[1]
        add_ref = refs[2] if addend is not None else None
        o_ref = refs[n_in]
        if k_piece:
            d = None
            for p in range(kb):
                dp = lax.dot_general(a_ref[:, p * b_c:(p + 1) * b_c].astype(BF16), b_ref[p].astype(BF16), dims,
                                     preferred_element_type=F32)
                d = dp if d is None else d + dp
        else:
            d = lax.dot_general(a_ref[...].astype(BF16), b_ref[...].astype(BF16), dims, preferred_element_type=F32)

        def finish(r):
            if addend is not None:
                r = r + add_ref[...]
            o_ref[...] = r.astype(o_ref.dtype)

        if nk == 1:
            finish(d)
            return
        acc_ref = refs[n_in + 1]
        k = pl.program_id(2)

        @pl.when(k == 0)
        def _():
            acc_ref[...] = d

        if nk > 2:
            @pl.when((k > 0) & (k < nk - 1))
            def _():
                acc_ref[...] += d

        @pl.when(k == nk - 1)
        def _():
            finish(acc_ref[...] + d)

    return pl.pallas_call(
        body, name=name, out_shape=out_shape, grid=(m // bm, n // bn, nk),
        in_specs=in_specs, out_specs=o_spec,
        scratch_shapes=[pltpu.VMEM((bm, bn), F32)] if nk > 1 else [],
        compiler_params=_cp("parallel", "parallel", "arbitrary"),
    )(*args)


LN_ROWS = 256


def _ln_stats(s):
    mu = jnp.mean(s, axis=-1, keepdims=True)
    sc = s - mu
    var = jnp.mean(sc * sc, axis=-1, keepdims=True)
    rstd = lax.rsqrt(var + LN_EPS)
    return sc * rstd, rstd


def _ln_fwd(h, sub, g, b, name):
    t, d = h.shape
    row = pl.BlockSpec((LN_ROWS, d), lambda i: (i, 0))
    vec = pl.BlockSpec((1, d), lambda i: (0, 0))

    def body(h_ref, s_ref, g_ref, b_ref, y_ref, yb_ref):
        xhat, _ = _ln_stats(ALPHA * h_ref[...] + s_ref[...])
        y = xhat * g_ref[...] + b_ref[...]
        y_ref[...] = y
        yb_ref[...] = y.astype(BF16)

    return pl.pallas_call(
        body, name=name, grid=(t // LN_ROWS,),
        out_shape=(jax.ShapeDtypeStruct((t, d), F32), jax.ShapeDtypeStruct((t, d), BF16)),
        in_specs=[row, row, vec, vec], out_specs=(row, row), compiler_params=_cp("parallel"),
    )(h, sub, g, b)


def _ln_bwd_math(xhat, rstd, dy, g):
    dxhat = dy * g
    m1 = jnp.mean(dxhat, axis=-1, keepdims=True)
    m2 = jnp.mean(dxhat * xhat, axis=-1, keepdims=True)
    ds = rstd * (dxhat - m1 - xhat * m2)
    dg = jnp.sum(dy * xhat, axis=0, keepdims=True)
    db = jnp.sum(dy, axis=0, keepdims=True)
    return ds, dg, db


def _ln_bwd(h, sub, dy, g, name):
    t, d = h.shape
    row = pl.BlockSpec((LN_ROWS, d), lambda i: (i, 0))
    vec = pl.BlockSpec((1, d), lambda i: (0, 0))

    def body(h_ref, s_ref, dy_ref, g_ref, ds_ref, dres_ref, dg_ref, dbeta_ref):
        xhat, rstd = _ln_stats(ALPHA * h_ref[...] + s_ref[...])
        ds, dg, db = _ln_bwd_math(xhat, rstd, dy_ref[...], g_ref[...])
        ds_ref[...] = ds.astype(BF16)
        dres_ref[...] = ALPHA * ds

        @pl.when(pl.program_id(0) == 0)
        def _():
            dg_ref[...] = jnp.zeros_like(dg_ref)
            dbeta_ref[...] = jnp.zeros_like(dbeta_ref)

        dg_ref[...] += dg
        dbeta_ref[...] += db

    return pl.pallas_call(
        body, name=name, grid=(t // LN_ROWS,),
        out_shape=(jax.ShapeDtypeStruct((t, d), BF16), jax.ShapeDtypeStruct((t, d), F32),
                   jax.ShapeDtypeStruct((1, d), F32), jax.ShapeDtypeStruct((1, d), F32)),
        in_specs=[row, row, row, vec], out_specs=(row, row, vec, vec),
        compiler_params=_cp("arbitrary"),
    )(h, sub, dy, g)


def _ln_loss_bwd(h, sub, target, g, b, name):
    t, d = h.shape
    row = pl.BlockSpec((LN_ROWS, d), lambda i: (i, 0))
    vec = pl.BlockSpec((1, d), lambda i: (0, 0))
    lvec = pl.BlockSpec((1, 128), lambda i: (0, 0))

    def body(h_ref, s_ref, t_ref, g_ref, b_ref, loss_ref, ds_ref, dres_ref, dg_ref, dbeta_ref):
        xhat, rstd = _ln_stats(ALPHA * h_ref[...] + s_ref[...])
        y = xhat * g_ref[...] + b_ref[...]
        err = y - t_ref[...]
        part = 0.5 * jnp.sum(jnp.mean(err * err, axis=-1, keepdims=True), axis=0, keepdims=True)
        ds, dg, db = _ln_bwd_math(xhat, rstd, err * (1.0 / d), g_ref[...])
        ds_ref[...] = ds.astype(BF16)
        dres_ref[...] = ALPHA * ds

        @pl.when(pl.program_id(0) == 0)
        def _():
            loss_ref[...] = jnp.zeros_like(loss_ref)
            dg_ref[...] = jnp.zeros_like(dg_ref)
            dbeta_ref[...] = jnp.zeros_like(dbeta_ref)

        loss_ref[...] += jnp.broadcast_to(part, loss_ref.shape)
        dg_ref[...] += dg
        dbeta_ref[...] += db

    return pl.pallas_call(
        body, name=name, grid=(t // LN_ROWS,),
        out_shape=(jax.ShapeDtypeStruct((1, 128), F32), jax.ShapeDtypeStruct((t, d), BF16),
                   jax.ShapeDtypeStruct((t, d), F32), jax.ShapeDtypeStruct((1, d), F32),
                   jax.ShapeDtypeStruct((1, d), F32)),
        in_specs=[row, row, row, vec, vec], out_specs=(lvec, row, row, vec, vec),
        compiler_params=_cp("arbitrary"),
    )(h, sub, target, g, b)


HG_ROWS = 2048
HG_LEVELS = 7


def _hg_constants():
    c = HG_CHUNK
    t = np.arange(c)[:, None]
    j = np.arange(c)[None, :]
    blocks, masks = [], [np.eye(c)]
    for lev in range(HG_LEVELS):
        m = 1 << lev
        second = (t % (2 * m)) >= m
        mid = (t // (2 * m)) * (2 * m) + m - 1
        blocks.append((second & (j > mid) & (j <= t)) | ((~second) & (j > t) & (j <= mid)))
        same = (t // (2 * m)) == (j // (2 * m))
        masks.append(same & second & ((j % (2 * m)) < m))
    blocks += [j <= t, j > t, np.ones((c, c), bool)]
    cm = np.concatenate(blocks, axis=0).astype(np.float32)
    mk = np.concatenate(masks, axis=0).astype(np.float32)
    return jnp.asarray(cm, dtype=BF16), jnp.asarray(mk)


def _dot_01(cm, x, mode):
    hi = x.astype(BF16)
    lo = (x - hi.astype(F32)).astype(BF16)
    both = lax.dot_general(cm, jnp.concatenate([hi, lo], axis=1), _DIMS[mode], preferred_element_type=F32)
    n = x.shape[-1]
    return both[:, :n] + both[:, n:]


@jax.custom_vjp
def _prefix(cm, lf):
    return _dot_01(cm, lf, "nn")


def _prefix_fwd(cm, lf):
    return _prefix(cm, lf), cm


def _prefix_bwd(cm, d):
    return jnp.zeros_like(cm), _dot_01(cm, d, "tn")


_prefix.defvjp(_prefix_fwd, _prefix_bwd)

_BWD = {"nn": (("nt", 0, 1), ("tn", 1, 0)), "nt": (("nn", 0, 1), ("tn", 0, 1)), "tn": (("nt", 1, 0), ("nn", 1, 0))}


def _bdot_raw(a, b, mode):
    return lax.dot_general(a.astype(BF16), b.astype(BF16), _DIMS[mode], preferred_element_type=F32)


@functools.partial(jax.custom_vjp, nondiff_argnums=(2,))
def _bdot(a, b, mode):
    return _bdot_raw(a, b, mode)


def _bdot_fwd(a, b, mode):
    return _bdot_raw(a, b, mode), (a, b)


def _bdot_bwd(mode, res, d):
    a, b = res
    (ma, da_pos, _), (mb, db_pos, _) = _BWD[mode]
    da = _bdot_raw(d, b, ma) if da_pos == 0 else _bdot_raw(b, d, ma)
    db = _bdot_raw(d, a, mb) if db_pos == 0 else _bdot_raw(a, d, mb)
    return da, db


_bdot.defvjp(_bdot_fwd, _bdot_bwd)


def _hg_chunk(pq, pf, pi, pg, l0, l1, l2, ng, s_t, cm, mk):
    c = HG_CHUNK
    mx = jnp.maximum(jnp.maximum(l0, l1), l2)
    e0, e1, e2 = jnp.exp(l0 - mx), jnp.exp(l1 - mx), jnp.exp(l2 - mx)
    lb = e0 / (e0 + e1 + e2)
    q = pq * jax.nn.sigmoid(pq)
    t1 = jnp.log(lb)
    t2 = jnp.log1p(-lb) + jax.nn.log_sigmoid(pf)
    lf = jnp.maximum(t1, t2) + jnp.log1p(jnp.exp(-jnp.abs(t1 - t2)))
    k = (1.0 - lb) * jax.nn.sigmoid(-pf)
    x = jnp.exp(_prefix(cm, lf))
    scores = mk[0:c] * _bdot(q, k, "nt")
    for lev in range(HG_LEVELS):
        xl = x[lev * c:(lev + 1) * c]
        scores = scores + mk[(lev + 1) * c:(lev + 2) * c] * _bdot(q * xl, k * xl, "nt")
    x_incl = x[HG_LEVELS * c:(HG_LEVELS + 1) * c]
    x_after = x[(HG_LEVELS + 1) * c:(HG_LEVELS + 2) * c]
    x_total = x[(HG_LEVELS + 2) * c:(HG_LEVELS + 3) * c]
    o = _bdot(scores, pi, "nn") + _bdot(q * x_incl, s_t, "nt")
    s_new = s_t * jnp.concatenate([x_total] * (HG_DIM // c), axis=0) + _bdot(pi, k * x_after, "tn")
    rstd = lax.rsqrt(jnp.mean(o * o, axis=-1, keepdims=True) + RMS_EPS)
    y = o * rstd * ng * (pg * jax.nn.sigmoid(pg))
    return y, s_new


def _hg_specs(t, reverse):
    nrb = t // HG_ROWS
    rb = (lambda r: nrb - 1 - r) if reverse else (lambda r: r)
    proj = pl.BlockSpec((4, HG_ROWS, HG_DIM), lambda h, r: (0, rb(r), h))
    vec = pl.BlockSpec((1, HG_DIM), lambda h, r: (0, h))
    cm = pl.BlockSpec(((HG_LEVELS + 3) * HG_CHUNK, HG_CHUNK), lambda h, r: (0, 0))
    mk = pl.BlockSpec(((HG_LEVELS + 1) * HG_CHUNK, HG_CHUNK), lambda h, r: (0, 0))
    rows = pl.BlockSpec((HG_ROWS, HG_DIM), lambda h, r: (rb(r), h))
    states = pl.BlockSpec((None, HG_ROWS // HG_CHUNK, HG_DIM, HG_DIM), lambda h, r: (h, rb(r), 0, 0))
    return proj, vec, cm, mk, rows, states


def _hg_fwd(proj, l0, l1, l2, ng):
    t = proj.shape[1]
    n_in = HG_ROWS // HG_CHUNK
    cm, mk = _hg_constants()
    p_spec, vec, cm_spec, mk_spec, rows, st_spec = _hg_specs(t, False)

    def body(p_ref, l0_ref, l1_ref, l2_ref, ng_ref, cm_ref, mk_ref, y_ref, st_ref, s_ref):
        @pl.when(pl.program_id(1) == 0)
        def _():
            s_ref[...] = jnp.zeros_like(s_ref)

        def step(ci, carry):
            sl = pl.ds(pl.multiple_of(ci * HG_CHUNK, HG_CHUNK), HG_CHUNK)
            s_t = s_ref[...]
            st_ref[ci] = s_t
            y, s_new = _hg_chunk(p_ref[0, sl, :], p_ref[1, sl, :], p_ref[2, sl, :], p_ref[3, sl, :],
                                 l0_ref[...], l1_ref[...], l2_ref[...], ng_ref[...], s_t,
                                 cm_ref[...], mk_ref[...])
            y_ref[sl, :] = y.astype(BF16)
            s_ref[...] = s_new
            return carry

        lax.fori_loop(0, n_in, step, 0, unroll=True)

    return pl.pallas_call(
        body, name="hg_fwd", grid=(HG_HEADS, t // HG_ROWS),
        out_shape=(jax.ShapeDtypeStruct((t, D_MODEL), BF16),
                   jax.ShapeDtypeStruct((HG_HEADS, t // HG_CHUNK, HG_DIM, HG_DIM), F32)),
        in_specs=[p_spec, vec, vec, vec, vec, cm_spec, mk_spec], out_specs=(rows, st_spec),
        scratch_shapes=[pltpu.VMEM((HG_DIM, HG_DIM), F32)],
        compiler_params=_cp("parallel", "arbitrary"),
    )(proj, l0, l1, l2, ng, cm, mk)


def _hg_bwd(proj, states, dy, l0, l1, l2, ng):
    t = proj.shape[1]
    n_in = HG_ROWS // HG_CHUNK
    cm, mk = _hg_constants()
    p_spec, vec, cm_spec, mk_spec, rows, st_spec = _hg_specs(t, True)

    def body(p_ref, st_ref, dy_ref, l0_ref, l1_ref, l2_ref, ng_ref, cm_ref, mk_ref,
             dp_ref, dl0_ref, dl1_ref, dl2_ref, dng_ref, ds_ref):
        @pl.when(pl.program_id(1) == 0)
        def _():
            ds_ref[...] = jnp.zeros_like(ds_ref)
            for r in (dl0_ref, dl1_ref, dl2_ref, dng_ref):
                r[...] = jnp.zeros_like(r)

        def step(it, carry):
            ci = n_in - 1 - it
            sl = pl.ds(pl.multiple_of(ci * HG_CHUNK, HG_CHUNK), HG_CHUNK)
            fn = functools.partial(_hg_chunk, cm=cm_ref[...], mk=mk_ref[...])
            _, vjp = jax.vjp(fn, p_ref[0, sl, :], p_ref[1, sl, :], p_ref[2, sl, :], p_ref[3, sl, :],
                             l0_ref[...], l1_ref[...], l2_ref[...], ng_ref[...], st_ref[ci])
            dq, df, di, dg, d0, d1, d2, dn, ds = vjp((dy_ref[sl, :], ds_ref[...]))
            dp_ref[0, sl, :] = dq.astype(BF16)
            dp_ref[1, sl, :] = df.astype(BF16)
            dp_ref[2, sl, :] = di.astype(BF16)
            dp_ref[3, sl, :] = dg.astype(BF16)
            dl0_ref[...] += d0
            dl1_ref[...] += d1
            dl2_ref[...] += d2
            dng_ref[...] += dn
            ds_ref[...] = ds
            return carry

        lax.fori_loop(0, n_in, step, 0, unroll=True)

    v_shape = jax.ShapeDtypeStruct((1, D_MODEL), F32)
    return pl.pallas_call(
        body, name="hg_bwd", grid=(HG_HEADS, t // HG_ROWS),
        out_shape=(jax.ShapeDtypeStruct((4, t, D_MODEL), BF16), v_shape, v_shape, v_shape, v_shape),
        in_specs=[p_spec, st_spec, rows, vec, vec, vec, vec, cm_spec, mk_spec],
        out_specs=(p_spec, vec, vec, vec, vec),
        scratch_shapes=[pltpu.VMEM((HG_DIM, HG_DIM), F32)],
        compiler_params=_cp("parallel", "arbitrary"),
    )(proj, states, dy, l0, l1, l2, ng, cm, mk)


_SQRT_HALF = 0.7071067811865476
_INV_SQRT_2PI = 0.3989422804014327


def _gelu(x):
    return 0.5 * x * (1.0 + lax.erf(x * _SQRT_HALF))


def _gelu_grad(x):
    return 0.5 * (1.0 + lax.erf(x * _SQRT_HALF)) + x * (_INV_SQRT_2PI * jnp.exp(-0.5 * x * x))


def _tril(n):
    return (lax.broadcasted_iota(jnp.int32, (n, n), 0) >= lax.broadcasted_iota(jnp.int32, (n, n), 1)).astype(F32)


def _sg_specs(d):
    row = lambda w: pl.BlockSpec((SG_CHUNK, w), lambda i: (i, 0))
    vec = pl.BlockSpec((1, d), lambda i: (0, 0))
    cube = pl.BlockSpec((SG_GROUPS, SG_CHUNK, SG_CHUNK), lambda i: (0, 0, 0))
    return row, vec, cube


def _sg_fwd(pre, ln_g, ln_b, w_s, bias):
    t = pre.shape[0]
    d = D_MODEL
    row, vec, cube = _sg_specs(d)

    def body(pre_ref, g_ref, b_ref, w_ref, bias_ref, y_ref, vln_ref):
        u = _gelu(pre_ref[:, :d])
        vhat, _ = _ln_stats(_gelu(pre_ref[:, d:]))
        vln_ref[...] = vhat * g_ref[...] + b_ref[...]
        tril = _tril(SG_CHUNK)
        for g in range(SG_GROUPS):
            cs = slice(g * SG_DIM, (g + 1) * SG_DIM)
            gate = _bdot_raw(w_ref[g] * tril, vln_ref[:, cs], "nn") + bias_ref[g]
            y_ref[:, cs] = (u[:, cs] * gate).astype(BF16)

    return pl.pallas_call(
        body, name="sg_fwd", grid=(t // SG_CHUNK,), out_shape=jax.ShapeDtypeStruct((t, d), BF16),
        in_specs=[row(2 * d), vec, vec, cube, cube], out_specs=row(d),
        scratch_shapes=[pltpu.VMEM((SG_CHUNK, d), F32)], compiler_params=_cp("parallel"),
    )(pre, ln_g, ln_b, w_s, bias)


def _sg_bwd(pre, dy, ln_g, ln_b, w_s, bias):
    t = pre.shape[0]
    d = D_MODEL
    row, vec, cube = _sg_specs(d)
    dbs_spec = pl.BlockSpec((SG_GROUPS, 8, SG_CHUNK), lambda i: (0, 0, 0))

    def body(pre_ref, dy_ref, g_ref, b_ref, w_ref, bias_ref, dpre_ref, dw_ref, dbs_ref, dlg_ref, dlb_ref,
             vln_ref, dvln_ref):
        @pl.when(pl.program_id(0) == 0)
        def _():
            for r in (dw_ref, dbs_ref, dlg_ref, dlb_ref):
                r[...] = jnp.zeros_like(r)

        pu = pre_ref[:, :d]
        pv = pre_ref[:, d:]
        u = _gelu(pu)
        vhat, rstd = _ln_stats(_gelu(pv))
        vln_ref[...] = vhat * g_ref[...] + b_ref[...]
        tril = _tril(SG_CHUNK)
        ones = jnp.ones((8, SG_DIM), F32)
        for g in range(SG_GROUPS):
            cs = slice(g * SG_DIM, (g + 1) * SG_DIM)
            wc = w_ref[g] * tril
            vg = vln_ref[:, cs]
            gate = _bdot_raw(wc, vg, "nn") + bias_ref[g]
            dyg = dy_ref[:, cs]
            dgate = dyg * u[:, cs]
            dpre_ref[:, cs] = (dyg * gate * _gelu_grad(pu[:, cs])).astype(BF16)
            dw_ref[g] += tril * _bdot_raw(dgate, vg, "nt")
            dbs_ref[g] += lax.dot_general(ones, dgate, _DIMS["nt"], precision=HI, preferred_element_type=F32)
            dvln_ref[:, cs] = _bdot_raw(wc, dgate, "tn")
        dvln = dvln_ref[...]
        dv, dg, db = _ln_bwd_math(vhat, rstd, dvln, g_ref[...])
        dlg_ref[...] += dg
        dlb_ref[...] += db
        dpre_ref[:, d:] = (dv * _gelu_grad(pv)).astype(BF16)

    return pl.pallas_call(
        body, name="sg_bwd", grid=(t // SG_CHUNK,),
        out_shape=(jax.ShapeDtypeStruct((t, 2 * d), BF16), jax.ShapeDtypeStruct(w_s.shape, F32),
                   jax.ShapeDtypeStruct((SG_GROUPS, 8, SG_CHUNK), F32),
                   jax.ShapeDtypeStruct((1, d), F32), jax.ShapeDtypeStruct((1, d), F32)),
        in_specs=[row(2 * d), row(d), vec, vec, cube, cube],
        out_specs=(row(2 * d), cube, dbs_spec, vec, vec),
        scratch_shapes=[pltpu.VMEM((SG_CHUNK, d), F32), pltpu.VMEM((SG_CHUNK, d), F32)],
        compiler_params=_cp("arbitrary"),
    )(pre, dy, ln_g, ln_b, w_s, bias)


FFN_ROWS = 256
FFN_COLS = 1408
HALO = 16


def _ffn_conv(a_ext, w_ref, cb_ref):
    a1 = pltpu.roll(a_ext, 1, 0)
    a2 = pltpu.roll(a_ext, 2, 0)
    return w_ref[0:1, :] * a2 + w_ref[1:2, :] * a1 + w_ref[2:3, :] * a_ext + cb_ref[...], a1, a2


def _ffn_fwd(hh, conv_w, conv_b):
    t = hh.shape[1]
    nb8 = FFN_ROWS // HALO
    main = pl.BlockSpec((2, FFN_ROWS, FFN_COLS), lambda c, r: (0, r, c))
    prev = pl.BlockSpec((None, HALO, FFN_COLS), lambda c, r: (0, jnp.maximum(r * nb8 - 1, 0), c))
    wspec = pl.BlockSpec((3, FFN_COLS), lambda c, r: (0, c))
    bspec = pl.BlockSpec((1, FFN_COLS), lambda c, r: (0, c))

    def body(m_ref, p_ref, w_ref, cb_ref, z_ref):
        prev = jnp.where(pl.program_id(1) == 0, 0.0, p_ref[...].astype(F32))
        a_ext = jnp.concatenate([prev, m_ref[0].astype(F32)], axis=0)
        a, _, _ = _ffn_conv(a_ext, w_ref, cb_ref)
        a = a[HALO:]
        z_ref[...] = (a * jax.nn.sigmoid(a) * m_ref[1].astype(F32)).astype(BF16)

    return pl.pallas_call(
        body, name="ffn_fwd", grid=(D_FF // FFN_COLS, t // FFN_ROWS),
        out_shape=jax.ShapeDtypeStruct((t, D_FF), BF16),
        in_specs=[main, prev, wspec, bspec],
        out_specs=pl.BlockSpec((FFN_ROWS, FFN_COLS), lambda c, r: (r, c)),
        compiler_params=_cp("parallel", "parallel"),
    )(hh, hh, conv_w, conv_b)


def _ffn_bwd(hh, dz, conv_w, conv_b):
    t = hh.shape[1]
    nb8 = FFN_ROWS // HALO
    last8 = t // HALO - 1
    nr = t // FFN_ROWS
    main = pl.BlockSpec((2, FFN_ROWS, FFN_COLS), lambda c, r: (0, r, c))
    prev = pl.BlockSpec((None, HALO, FFN_COLS), lambda c, r: (0, jnp.maximum(r * nb8 - 1, 0), c))
    nxt = pl.BlockSpec((2, HALO, FFN_COLS), lambda c, r: (0, jnp.minimum((r + 1) * nb8, last8), c))
    dmain = pl.BlockSpec((FFN_ROWS, FFN_COLS), lambda c, r: (r, c))
    dnxt = pl.BlockSpec((HALO, FFN_COLS), lambda c, r: (jnp.minimum((r + 1) * nb8, last8), c))
    wspec = pl.BlockSpec((3, FFN_COLS), lambda c, r: (0, c))
    bspec = pl.BlockSpec((1, FFN_COLS), lambda c, r: (0, c))

    def body(m_ref, p_ref, n_ref, dz_ref, dzn_ref, w_ref, cb_ref, dh_ref, dw_ref, dcb_ref):
        r = pl.program_id(1)

        @pl.when(r == 0)
        def _():
            dw_ref[...] = jnp.zeros_like(dw_ref)
            dcb_ref[...] = jnp.zeros_like(dcb_ref)

        prev = jnp.where(r == 0, 0.0, p_ref[...].astype(F32))
        a_ext = jnp.concatenate([prev, m_ref[0].astype(F32), n_ref[0].astype(F32)], axis=0)
        a, a1, a2 = _ffn_conv(a_ext, w_ref, cb_ref)
        a = a[HALO:]
        b_ext = jnp.concatenate([m_ref[1], n_ref[1]], axis=0).astype(F32)
        dz_main = dz_ref[...].astype(F32)
        dz_ext = jnp.concatenate([dz_main, jnp.where(r == nr - 1, 0.0, dzn_ref[...].astype(F32))], axis=0)
        sig = jax.nn.sigmoid(a)
        da = dz_ext * b_ext * (sig * (1.0 + a * (1.0 - sig)))
        n_ext = FFN_ROWS + HALO
        da_p1 = pltpu.roll(da, n_ext - 1, 0)
        da_p2 = pltpu.roll(da, n_ext - 2, 0)
        da_raw = w_ref[2:3, :] * da + w_ref[1:2, :] * da_p1 + w_ref[0:1, :] * da_p2
        dh_ref[0] = da_raw[:FFN_ROWS].astype(BF16)
        dh_ref[1] = (dz_main * (a * sig)[:FFN_ROWS]).astype(BF16)
        dam = da[:FFN_ROWS]
        rows = slice(HALO, HALO + FFN_ROWS)
        dw_ref[0:1, :] += jnp.sum(dam * a2[rows], axis=0, keepdims=True)
        dw_ref[1:2, :] += jnp.sum(dam * a1[rows], axis=0, keepdims=True)
        dw_ref[2:3, :] += jnp.sum(dam * a_ext[rows], axis=0, keepdims=True)
        dcb_ref[...] += jnp.sum(dam, axis=0, keepdims=True)

    return pl.pallas_call(
        body, name="ffn_bwd", grid=(D_FF // FFN_COLS, nr),
        out_shape=(jax.ShapeDtypeStruct((2, t, D_FF), BF16), jax.ShapeDtypeStruct((3, D_FF), F32),
                   jax.ShapeDtypeStruct((1, D_FF), F32)),
        in_specs=[main, prev, nxt, dmain, dnxt, wspec, bspec],
        out_specs=(main, wspec, bspec), compiler_params=_cp("parallel", "arbitrary"),
    )(hh, hh, hh, dz, dz, conv_w, conv_b)


def _adamw_math(w, g, m, v):
    m = ADAM_B1 * m + (1.0 - ADAM_B1) * g
    v = ADAM_B2 * v + (1.0 - ADAM_B2) * (g * g)
    m_hat = m / (1.0 - ADAM_B1 ** ADAM_STEP)
    v_hat = v / (1.0 - ADAM_B2 ** ADAM_STEP)
    delta = -ADAM_LR * (m_hat / (jnp.sqrt(v_hat) + ADAM_EPS) + ADAM_WD * w)
    return delta, m, v


def _as2d(shape):
    n = int(np.prod(shape))
    c = shape[-1] if shape[-1] % 128 == 0 else (128 if n % 128 == 0 else shape[-1])
    return n // c, c


def _adamw(w, g, m, v, name, parts=None, own=None, is_me=None, layer=None, prev=None):
    shape = w.shape if layer is None else w.shape[1:]
    r, c = _as2d(shape)
    br = r if r <= 512 else _pick(r, (256, 176, 128, 64, 8))
    blk = pl.BlockSpec((br, c), lambda i: (i, 0))
    if layer is None:
        wblk, o = blk, jax.ShapeDtypeStruct((r, c), F32)
        w2, m2, v2 = (a.reshape(r, c) for a in (w, m, v))
    else:
        wblk, o = pl.BlockSpec((None, br, c), lambda i: (layer, i, 0)), jax.ShapeDtypeStruct((w.shape[0], r, c), F32)
        w2, m2, v2 = (a.reshape(w.shape[0], r, c) for a in (w, m, v))
    extra, extra_specs = [], []
    if parts is not None:
        g_in = parts.reshape(N_DEV, r, c)
        g_spec = pl.BlockSpec((N_DEV, br, c), lambda i: (0, i, 0))
        extra = [own.reshape(r, c), is_me]
        extra_specs = [blk, pl.BlockSpec(memory_space=pltpu.SMEM)]
    else:
        g_in = g.reshape(r, c)
        g_spec = blk
    aliases = {}
    if prev is not None:
        aliases = {4 + len(extra) + j: j for j in range(4)}
        extra = extra + [a.reshape(o.shape) for a in prev]
        extra_specs = extra_specs + [ANY] * 4

    def body(w_ref, g_ref, m_ref, v_ref, *rest):
        go_ref, d_ref, mo_ref, vo_ref = rest[-4:]
        if parts is not None:
            own_ref, me_ref = rest[:2]
            gsum = None
            for i in range(N_DEV):
                term = jnp.where(me_ref[i] > 0.5, own_ref[...], g_ref[i]).astype(F32)
                gsum = term if gsum is None else gsum + term
        else:
            gsum = g_ref[...]
        delta, m_new, v_new = _adamw_math(w_ref[...], gsum, m_ref[...], v_ref[...])
        go_ref[...] = gsum
        d_ref[...] = delta
        mo_ref[...] = m_new
        vo_ref[...] = v_new

    outs = pl.pallas_call(
        body, name=name, grid=(r // br,), out_shape=(o, o, o, o),
        in_specs=[wblk, g_spec, wblk, wblk] + extra_specs, out_specs=(wblk, wblk, wblk, wblk),
        input_output_aliases=aliases, compiler_params=_cp("parallel"),
    )(w2, g_in, m2, v2, *extra)
    return tuple(a.reshape(w.shape) for a in outs)


ALL_PEERS = tuple(range(1, N_DEV))
OTHER_CHIPS = (2, 4, 6)
SIBLING = (1,)


def _me_and_peers(relations=ALL_PEERS):
    x, y, c = lax.axis_index("x"), lax.axis_index("y"), lax.axis_index("c")
    me = 4 * x + 2 * y + c
    peers = []
    for k in relations:
        kx, ky, kc = (k >> 2) & 1, (k >> 1) & 1, k & 1
        px, py, pc = x ^ kx, y ^ ky, c ^ kc
        peers.append(((px, py, pc), 4 * px + 2 * py + pc))
    return me, peers


def _all_gather(shard, name):
    def body(x_ref, o_ref, send_sems, recv_sems, local_sem):
        me, peers = _me_and_peers()
        mine = pltpu.make_async_copy(x_ref, o_ref.at[me], local_sem)
        mine.start()
        sends = []
        for k, (dev, _) in enumerate(peers):
            cp = pltpu.make_async_remote_copy(src_ref=x_ref, dst_ref=o_ref.at[me], send_sem=send_sems.at[k],
                                              recv_sem=recv_sems.at[k], device_id=dev, device_id_type=MESH)
            cp.start()
            sends.append(cp)
        for k, (dev, idx) in enumerate(peers):
            pltpu.make_async_remote_copy(src_ref=x_ref, dst_ref=o_ref.at[idx], send_sem=send_sems.at[k],
                                         recv_sem=recv_sems.at[k], device_id=dev, device_id_type=MESH).wait_recv()
        for cp in sends:
            cp.wait_send()
        mine.wait()

    return pl.pallas_call(
        body, name=name, out_shape=jax.ShapeDtypeStruct((N_DEV,) + shard.shape, shard.dtype),
        in_specs=[ANY], out_specs=ANY,
        scratch_shapes=[pltpu.SemaphoreType.DMA((N_DEV - 1,)), pltpu.SemaphoreType.DMA((N_DEV - 1,)),
                        pltpu.SemaphoreType.DMA],
        compiler_params=pltpu.CompilerParams(has_side_effects=True),
    )(shard)


HBM = pl.BlockSpec(memory_space=pltpu.HBM)
SEM = pl.BlockSpec(memory_space=pltpu.SEMAPHORE)
EFFECT = pltpu.SideEffectType.DATAFLOW_SIDE_EFFECTING


def _xchg_start(parts, land, *, dep, name, peers=ALL_PEERS, pieces=(0,)):
    n_buf = 1 if parts is None else 2

    def body(*refs):
        land_ref = refs[n_buf - 1]
        send_sem, recv_sem = refs[n_buf + 1], refs[n_buf + 2]
        token = refs[2 * n_buf + 3]
        me, to = _me_and_peers(peers)
        for dev, idx in to:
            if parts is None:
                for r in pieces:
                    piece = land_ref.at[me ^ r]
                    pltpu.make_async_remote_copy(src_ref=piece, dst_ref=piece, send_sem=send_sem, recv_sem=recv_sem,
                                                 device_id=dev, device_id_type=MESH).start()
            else:
                pltpu.make_async_remote_copy(src_ref=refs[0].at[idx], dst_ref=land_ref.at[me], send_sem=send_sem,
                                             recv_sem=recv_sem, device_id=dev, device_id_type=MESH).start()
        token[...] = jnp.zeros_like(token)

    bufs = [land] if parts is None else [parts, land]
    return pl.pallas_call(
        body, name=name,
        out_shape=(pltpu.SemaphoreType.DMA(()), pltpu.SemaphoreType.DMA(()),
                   *[pltpu.HBM(b.shape, b.dtype) for b in bufs], jax.ShapeDtypeStruct((8, 128), F32)),
        in_specs=(*[HBM] * n_buf, ANY), out_specs=(SEM, SEM, *[HBM] * n_buf, pl.BlockSpec(memory_space=pltpu.VMEM)),
        input_output_aliases={i: 2 + i for i in range(n_buf)},
        compiler_params=pltpu.CompilerParams(has_side_effects=EFFECT),
    )(*[pltpu.with_memory_space_constraint(b, pltpu.HBM) for b in bufs], dep)


def _xchg_wait(handles, after, name, n_pieces=N_DEV - 1):
    send_sem, recv_sem, *bufs, _ = handles
    n_buf = len(bufs)

    def body(*refs):
        land_ref = refs[n_buf - 1]
        send_sem, recv_sem = refs[n_buf], refs[n_buf + 1]
        x, y, c = lax.axis_index("x"), lax.axis_index("y"), lax.axis_index("c")
        span = land_ref.at[pl.ds(0, n_pieces)]
        cp = pltpu.make_async_remote_copy(src_ref=span, dst_ref=span, send_sem=send_sem, recv_sem=recv_sem,
                                          device_id=(x, y, c), device_id_type=MESH)
        cp.wait_send()
        cp.wait_recv()

    return pl.pallas_call(
        body, name=name, out_shape=tuple(pltpu.HBM(b.shape, b.dtype) for b in bufs),
        in_specs=(*[HBM] * n_buf, SEM, SEM, ANY), out_specs=tuple([HBM] * n_buf),
        input_output_aliases={i: i for i in range(n_buf)},
        compiler_params=pltpu.CompilerParams(has_side_effects=EFFECT),
    )(*bufs, send_sem, recv_sem, after)


def _sum_devices(v, name):
    _, r, c = v.shape

    def body(x_ref, o_ref):
        acc = x_ref[0]
        for i in range(1, N_DEV):
            acc = acc + x_ref[i]
        o_ref[...] = acc

    return pl.pallas_call(
        body, name=name, out_shape=jax.ShapeDtypeStruct((r, c), F32),
        in_specs=[pl.BlockSpec(memory_space=pltpu.VMEM)], out_specs=pl.BlockSpec(memory_space=pltpu.VMEM),
        compiler_params=pltpu.CompilerParams(vmem_limit_bytes=VMEM_LIMIT),
    )(v)


def _local_step(x, target, lb_logits, p, weight, send_grad, send_small):
    g = {}
    l0, l1, l2 = lb_logits[0:1], lb_logits[1:2], lb_logits[2:3]
    x_bf = x.astype(BF16)

    proj0 = _mm(x_bf, weight("hg_w_in", x_bf), mode="nn", b_cs=True, o_cs=4, name="mm_hg_in")
    y0, states = _hg_fwd(proj0, l0, l1, l2, p["hg_norm_g"])
    mixed0 = _mm(y0, weight("hg_w_out", y0), mode="nn", name="mm_hg_out")
    hh_l, z_l = [], []

    def ffn_forward(layer, h_in_bf):
        hh = _mm(h_in_bf, weight(f"ffn_w_up{layer}", h_in_bf), mode="nn", b_cs=True, o_cs=2, out_dtype=BF16,
                 name=f"mm_up{layer}")
        z = _ffn_fwd(hh, p["ffn_conv_w"][layer], p["ffn_conv_b"][layer])
        ffn = _mm(z, weight(f"ffn_w_down{layer}", z), mode="nn", name=f"mm_down{layer}")
        hh_l.append(hh)
        z_l.append(z)
        return ffn

    h1, h1b = _ln_fwd(x, mixed0, p["ln1_g"][0], p["ln1_b"][0], "ln1_0")
    ffn0 = ffn_forward(0, h1b)
    h2, h2b = _ln_fwd(h1, ffn0, p["ln2_g"][0], p["ln2_b"][0], "ln2_0")
    pre1 = _mm(h2b, weight("sg_w_in", h2b), mode="nn", b_cs=True, name="mm_sg_in")
    y1 = _sg_fwd(pre1, p["sg_ln_g"], p["sg_ln_b"], p["sg_w_s"], p["sg_bias"])
    mixed1 = _mm(y1, weight("sg_w_out", y1), mode="nn", name="mm_sg_out")
    h3, h3b = _ln_fwd(h2, mixed1, p["ln1_g"][1], p["ln1_b"][1], "ln1_1")
    ffn1 = ffn_forward(1, h3b)

    loss, ds, dres, g["ln2_g1"], g["ln2_b1"] = _ln_loss_bwd(h3, ffn1, target, p["ln2_g"][1], p["ln2_b"][1], "ln2_1_loss")

    def ffn_backward(layer, ds_bf, h_in_bf, dres):
        tok = send_grad(f"ffn_w_down{layer}",
                        _mm(z_l[layer], ds_bf, mode="tn", out_dtype=BF16, name=f"mm_dw_down{layer}"))
        dz = _mm(ds_bf, weight(f"ffn_w_down{layer}", None), mode="nt", out_dtype=BF16, name=f"mm_d_z{layer}",
                 deps=(tok,))
        dhh, g[f"conv_w{layer}"], g[f"conv_b{layer}"] = _ffn_bwd(hh_l[layer], dz, p["ffn_conv_w"][layer],
                                                                   p["ffn_conv_b"][layer])
        tok = send_grad(f"ffn_w_up{layer}", _mm(h_in_bf, dhh, mode="tn", b_cs=True, o_cs=N_DEV, out_dtype=BF16,
                                                name=f"mm_dw_up{layer}"))
        return _mm(dhh, weight(f"ffn_w_up{layer}", None), mode="nt", a_cs=True, b_cs=True, addend=dres,
                   name=f"mm_d_up{layer}", deps=(tok,))

    dh3 = ffn_backward(1, ds, h3b, dres)
    ds, dres, g["ln1_g1"], g["ln1_b1"] = _ln_bwd(h2, mixed1, dh3, p["ln1_g"][1], "ln1_1_bwd")
    tok = send_grad("sg_w_out", _mm(y1, ds, mode="tn", out_dtype=BF16, name="mm_dw_sg_out"))
    dy1 = _mm(ds, weight("sg_w_out", None), mode="nt", name="mm_d_sg_out", deps=(tok,))
    dpre1, g["sg_w_s"], dbs, g["sg_ln_g"], g["sg_ln_b"] = _sg_bwd(pre1, dy1, p["sg_ln_g"], p["sg_ln_b"],
                                                                    p["sg_w_s"], p["sg_bias"])
    g["sg_b_s"] = dbs[:, 0, :]
    tok = send_grad("sg_w_in", _mm(h2b, dpre1, mode="tn", o_cs=N_DEV, out_dtype=BF16, name="mm_dw_sg_in"))
    dh2 = _mm(dpre1, weight("sg_w_in", None), mode="nt", b_cs=True, addend=dres, name="mm_d_sg_in", deps=(tok,))
    ds, dres, g["ln2_g0"], g["ln2_b0"] = _ln_bwd(h1, ffn0, dh2, p["ln2_g"][0], "ln2_0_bwd")
    dh1 = ffn_backward(0, ds, h1b, dres)
    ds, dres, g["ln1_g0"], g["ln1_b0"] = _ln_bwd(x, mixed0, dh1, p["ln1_g"][0], "ln1_0_bwd")
    tok = send_grad("hg_w_out", _mm(y0, ds, mode="tn", out_dtype=BF16, name="mm_dw_hg_out"))
    dy0 = _mm(ds, weight("hg_w_out", None), mode="nt", name="mm_d_hg_out", deps=(tok,))
    dproj, d0, d1, d2, g["hg_norm_g"] = _hg_bwd(proj0, states, dy0, l0, l1, l2, p["hg_norm_g"])
    g["lb_logits"] = jnp.concatenate([d0, d1, d2], axis=0)
    tok = send_small(g)
    tok = send_grad("hg_w_in", _mm(x_bf, dproj, mode="tn", b_cs=True, o_cs=N_DEV, out_dtype=BF16, name="mm_dw_hg_in",
                                   deps=(tok,)))
    grad_x = _mm(dproj, weight("hg_w_in", None), mode="nt", a_cs=True, b_cs=True, addend=dres, name="mm_d_hg_in",
                 deps=(tok,))
    return loss[0, 0], grad_x


_SMALL = ("lb_logits", "hg_norm_g", "sg_ln_g", "sg_ln_b", "sg_w_s", "sg_b_s", "ffn_conv_w", "ffn_conv_b",
          "ln1_g", "ln1_b", "ln2_g", "ln2_b")
_NAMES = ("lb_logits", "hg_w_in", "hg_norm_g", "hg_w_out", "sg_w_in", "sg_ln_g", "sg_ln_b", "sg_w_s", "sg_b_s",
          "sg_w_out", "ffn_w_up", "ffn_conv_w", "ffn_conv_b", "ffn_w_down", "ln1_g", "ln1_b", "ln2_g", "ln2_b")


def kernel(x, lb_logits, hg_w_in, hg_norm_g, hg_w_out, sg_w_in, sg_ln_g, sg_ln_b, sg_w_s, sg_b_s, sg_w_out, ffn_w_up, ffn_conv_w, ffn_conv_b, ffn_w_down, ln1_g, ln1_b, ln2_g, ln2_b, loss_target, m_lb_logits, m_hg_w_in, m_hg_norm_g, m_hg_w_out, m_sg_w_in, m_sg_ln_g, m_sg_ln_b, m_sg_w_s, m_sg_b_s, m_sg_w_out, m_ffn_w_up, m_ffn_conv_w, m_ffn_conv_b, m_ffn_w_down, m_ln1_g, m_ln1_b, m_ln2_g, m_ln2_b, v_lb_logits, v_hg_w_in, v_hg_norm_g, v_hg_w_out, v_sg_w_in, v_sg_ln_g, v_sg_ln_b, v_sg_w_s, v_sg_b_s, v_sg_w_out, v_ffn_w_up, v_ffn_conv_w, v_ffn_conv_b, v_ffn_w_down, v_ln1_g, v_ln1_b, v_ln2_g, v_ln2_b):
    w = dict(lb_logits=lb_logits, hg_w_in=hg_w_in, hg_norm_g=hg_norm_g, hg_w_out=hg_w_out, sg_w_in=sg_w_in,
             sg_ln_g=sg_ln_g, sg_ln_b=sg_ln_b, sg_w_s=sg_w_s, sg_b_s=sg_b_s, sg_w_out=sg_w_out, ffn_w_up=ffn_w_up,
             ffn_conv_w=ffn_conv_w, ffn_conv_b=ffn_conv_b, ffn_w_down=ffn_w_down, ln1_g=ln1_g, ln1_b=ln1_b,
             ln2_g=ln2_g, ln2_b=ln2_b)
    m = dict(lb_logits=m_lb_logits, hg_w_in=m_hg_w_in, hg_norm_g=m_hg_norm_g, hg_w_out=m_hg_w_out,
             sg_w_in=m_sg_w_in, sg_ln_g=m_sg_ln_g, sg_ln_b=m_sg_ln_b, sg_w_s=m_sg_w_s, sg_b_s=m_sg_b_s,
             sg_w_out=m_sg_w_out, ffn_w_up=m_ffn_w_up, ffn_conv_w=m_ffn_conv_w, ffn_conv_b=m_ffn_conv_b,
             ffn_w_down=m_ffn_w_down, ln1_g=m_ln1_g, ln1_b=m_ln1_b, ln2_g=m_ln2_g, ln2_b=m_ln2_b)
    v = dict(lb_logits=v_lb_logits, hg_w_in=v_hg_w_in, hg_norm_g=v_hg_norm_g, hg_w_out=v_hg_w_out,
             sg_w_in=v_sg_w_in, sg_ln_g=v_sg_ln_g, sg_ln_b=v_sg_ln_b, sg_w_s=v_sg_w_s, sg_b_s=v_sg_b_s,
             sg_w_out=v_sg_w_out, ffn_w_up=v_ffn_w_up, ffn_conv_w=v_ffn_conv_w, ffn_conv_b=v_ffn_conv_b,
             ffn_w_down=v_ffn_w_down, ln1_g=v_ln1_g, ln1_b=v_ln1_b, ln2_g=v_ln2_g, ln2_b=v_ln2_b)
    me = 4 * lax.axis_index("x") + 2 * lax.axis_index("y") + lax.axis_index("c")
    is_me = (jnp.arange(N_DEV) == me).astype(F32)
    d = D_MODEL

    shards = [("hg_w_in", hg_w_in[0]), ("hg_w_out", hg_w_out[0]), ("ffn_w_up0", ffn_w_up[0]),
              ("ffn_w_down0", ffn_w_down[0]), ("sg_w_in", sg_w_in[0]), ("sg_w_out", sg_w_out[0]),
              ("ffn_w_up1", ffn_w_up[1]), ("ffn_w_down1", ffn_w_down[1])]
    row_sharded = {"hg_w_out": (d, d), "sg_w_out": (d, d), "ffn_w_down0": (D_FF, d), "ffn_w_down1": (D_FF, d)}
    sv = jnp.zeros((8, 768), F32)
    sv = sv.at[0, :256].set(sg_ln_g[0]).at[1, :256].set(sg_ln_b[0]).at[2:8, :704].set(ffn_conv_w.reshape(6, 704))
    sv = _all_gather(sv, "ag_small")
    gathers, gathered, dep = {}, {}, sv
    for n, shard in shards:
        land = lax.dynamic_update_index_in_dim(lax.empty((N_DEV,) + shard.shape, BF16), shard.astype(BF16), me, 0)
        gathers[n] = _xchg_start(None, land, dep=dep, name=f"ag_{n}", peers=OTHER_CHIPS)
        dep = gathers[n][-1]
    last_start = dep
    handed = {}
    order = [n for n, _ in shards]

    def hand_over(n, after):
        (land,) = _xchg_wait(gathers[n], after, f"agw_{n}", n_pieces=len(OTHER_CHIPS))
        handed[n] = _xchg_start(None, land, dep=is_me, name=f"ah_{n}", peers=SIBLING, pieces=(0,) + OTHER_CHIPS)

    def weight(n, after):
        if n not in gathered:
            if n == order[0]:
                hand_over(n, last_start)
                after = handed[n][-1]
            nxt = order.index(n) + 1
            if nxt < len(order):
                hand_over(order[nxt], after)
                after = handed[order[nxt]][-1]
            (full,) = _xchg_wait(handed[n], after, f"ahw_{n}", n_pieces=1 + len(OTHER_CHIPS))
            gathered[n] = full.reshape(row_sharded[n]) if n in row_sharded else full
        return gathered[n]

    grad_sends = {}

    def send_grad(n, parts):
        shape = (N_DEV,) + dict(shards)[n].shape
        grad_sends[n] = _xchg_start(parts.reshape(shape), lax.empty(shape, parts.dtype), dep=is_me, name=f"rs_{n}")
        return grad_sends[n][-1]

    small_send = {}

    def send_small(g):
        small = [g["lb_logits"], g["hg_norm_g"], g["sg_ln_g"], g["sg_ln_b"], g["sg_w_s"], g["sg_b_s"],
                 g["conv_w0"], g["conv_w1"], g["conv_b0"], g["conv_b1"], g["ln1_g0"], g["ln1_g1"],
                 g["ln1_b0"], g["ln1_b1"], g["ln2_g0"], g["ln2_g1"], g["ln2_b0"], g["ln2_b1"]]
        flat = jnp.concatenate([a.reshape(-1) for a in small])
        rows = -(-flat.shape[0] // (8 * 128)) * 8
        flat = jnp.pad(flat, (0, rows * 128 - flat.shape[0])).reshape(rows, 128)
        land = lax.dynamic_update_index_in_dim(lax.empty((N_DEV, rows, 128), F32), flat, me, 0)
        small_send["shapes"] = [a.shape for a in small]
        small_send["handles"] = _xchg_start(None, land, dep=is_me, name="ar_small")
        return small_send["handles"][-1]

    p = {}
    p["sg_ln_g"] = sv[:, 0, :256].reshape(1, d)
    p["sg_ln_b"] = sv[:, 1, :256].reshape(1, d)
    conv_w_full = jnp.transpose(sv[:, 2:8, :704].reshape(N_DEV, DEPTH, 3, 704), (1, 2, 0, 3)).reshape(DEPTH, 3, D_FF)
    p["ffn_conv_w"] = [conv_w_full[l] for l in range(DEPTH)]
    p["ffn_conv_b"] = [ffn_conv_b[l:l + 1] for l in range(DEPTH)]
    p["hg_norm_g"] = hg_norm_g
    p["sg_w_s"] = sg_w_s[0]
    p["sg_bias"] = jnp.broadcast_to(sg_b_s[0][:, :, None], (SG_GROUPS, SG_CHUNK, SG_DIM))
    for n in ("ln1_g", "ln1_b", "ln2_g", "ln2_b"):
        p[n] = [w[n][l:l + 1] for l in range(DEPTH)]

    loss_part, grad_x = _local_step(x[0], loss_target[0], lb_logits, p, weight, send_grad, send_small)
    loss = lax.psum(loss_part, ("x", "y", "c"))

    (landed,) = _xchg_wait(small_send["handles"], grad_x, "arw_small")
    red = _sum_devices(landed, "sum_small").reshape(-1)
    shapes = small_send["shapes"]
    offs = np.cumsum([0] + [int(np.prod(sh)) for sh in shapes])
    r = [red[offs[i]:offs[i + 1]].reshape(shapes[i]) for i in range(len(shapes))]
    gs = {}
    gs["lb_logits"] = r[0]
    gs["hg_norm_g"] = r[1]
    gs["sg_ln_g"] = lax.dynamic_slice(r[2], (0, me * 256), (1, 256))
    gs["sg_ln_b"] = lax.dynamic_slice(r[3], (0, me * 256), (1, 256))
    gs["sg_w_s"] = r[4][None]
    gs["sg_b_s"] = r[5][None]
    gs["ffn_conv_w"] = lax.dynamic_slice(jnp.stack([r[6], r[7]]), (0, 0, me * 704), (DEPTH, 3, 704))
    gs["ffn_conv_b"] = jnp.concatenate([r[8], r[9]], axis=0)
    gs["ln1_g"] = jnp.concatenate([r[10], r[11]], axis=0)
    gs["ln1_b"] = jnp.concatenate([r[12], r[13]], axis=0)
    gs["ln2_g"] = jnp.concatenate([r[14], r[15]], axis=0)
    gs["ln2_b"] = jnp.concatenate([r[16], r[17]], axis=0)

    out_g, out_d, out_m, out_v = {}, {}, {}, {}
    for n in _SMALL:
        out_g[n], out_d[n], out_m[n], out_v[n] = _adamw(w[n], gs[n], m[n], v[n], f"adamw_{n}")

    res, after = {}, out_v["ln2_b"]
    for n in ("ffn_w_down1", "ffn_w_up1", "sg_w_out", "sg_w_in", "ffn_w_down0", "ffn_w_up0", "hg_w_out", "hg_w_in"):
        parts, recv = _xchg_wait(grad_sends[n], after, f"rsw_{n}")
        own = lax.dynamic_index_in_dim(parts, me, 0, keepdims=False)
        if n[-1] in "01":
            base, layer = n[:-1], int(n[-1])
            res[n] = _adamw(w[base], None, m[base], v[base], f"adamw_{n}", parts=recv, own=own, is_me=is_me,
                            layer=layer, prev=res.get(base + "1"))
        else:
            res[n] = _adamw(w[n][0], None, m[n][0], v[n][0], f"adamw_{n}", parts=recv, own=own, is_me=is_me)
        after = res[n][3]
    for n in ("hg_w_in", "hg_w_out", "sg_w_in", "sg_w_out"):
        out_g[n], out_d[n], out_m[n], out_v[n] = (a[None] for a in res[n])
    for n in ("ffn_w_up", "ffn_w_down"):
        out_g[n], out_d[n], out_m[n], out_v[n] = res[n + "0"]

    return (loss, grad_x[None], *[out_g[n] for n in _NAMES], *[out_d[n] for n in _NAMES],
            *[out_m[n] for n in _NAMES], *[out_v[n] for n in _NAMES])
```

```python
import functools

import numpy as np
import jax
import jax.numpy as jnp
from jax import lax
from jax.experimental import pallas as pl
from jax.experimental.pallas import tpu as pltpu

F32 = jnp.float32
BF16 = jnp.bfloat16
HI = lax.Precision.HIGHEST

N_DEV = 8
D_MODEL = 2048
HG_HEADS = 16
HG_DIM = 128
HG_CHUNK = 128
SG_GROUPS = 16
SG_DIM = 128
SG_CHUNK = 128
D_FF = 5632
DEPTH = 2
ALPHA = (2 * DEPTH) ** 0.25
LN_EPS = 1e-5
RMS_EPS = 1e-6
ADAM_LR = 0.001
ADAM_B1 = 0.9
ADAM_B2 = 0.999
ADAM_EPS = 1e-08
ADAM_WD = 0.01
ADAM_STEP = 10

VMEM_LIMIT = 56 * 1024 * 1024
MESH = pl.DeviceIdType.MESH
ANY = pl.BlockSpec(memory_space=pl.ANY)


def _cp(*sem):
    return pltpu.CompilerParams(dimension_semantics=sem, vmem_limit_bytes=VMEM_LIMIT)


def _pick(n, cands):
    for c in cands:
        if n % c == 0:
            return c
    raise ValueError(f"no tile for {n} in {cands}")


_DIMS = {"nn": (((1,), (0,)), ((), ())), "nt": (((1,), (1,)), ((), ())), "tn": (((0,), (0,)), ((), ()))}


MM_VMEM_BUDGET = 44 * 1024 * 1024


def _mm_tiles(m, n, kk, n_div, k_div, out_bytes, has_addend, k_piece, transposed_a):
    best = None
    for bm in (1024, 1408, 512, 256):
        if m % bm:
            continue
        for bn in (1024, 1408, 512, 256, 128):
            if n % bn or any(d % bn for d in n_div):
                continue
            for bk in (kk, 5632, 4096, 2816, 2048, 1408, 1024, 512, 256, 128):
                if bk > kk or kk % bk or any(d % bk for d in k_div):
                    continue
                if k_piece and bk % k_piece:
                    continue
                nk = kk // bk
                vmem = 4 * (bm * bk + bk * bn) + 2 * bm * bn * out_bytes + (nk > 1) * 4 * bm * bn
                vmem += has_addend * 8 * bm * bn + transposed_a * 2 * bm * bk
                vmem += (out_bytes < 4) * 4 * bm * bn
                if vmem > MM_VMEM_BUDGET:
                    continue
                score = (max(nk, 2), -(bm * bn), -bk)
                if best is None or score < best[0]:
                    best = (score, bm, bn, bk)
    if best is None:
        raise ValueError(f"no tiles for {(m, n, kk)}")
    return best[1:]


def _mm(a, b, *, mode, name, out_dtype=F32, a_cs=False, b_cs=False, o_cs=None, addend=None, deps=()):
    if mode == "tn":
        kk, m = a.shape
    elif a_cs:
        m, kk = a.shape[1], a.shape[0] * a.shape[2]
    else:
        m, kk = a.shape
    if mode == "nt":
        n = b.shape[1] if b_cs else b.shape[0]
    else:
        n = b.shape[0] * b.shape[2] if b_cs else b.shape[1]
    a_c = a.shape[2] if a_cs else None
    b_c = b.shape[2] if b_cs else None
    o_c = n // o_cs if o_cs else None
    k_piece = b_c if (b_cs and mode == "nt") else None
    n_div = [c for c in (b_c if (b_cs and mode != "nt") else None, o_c) if c]
    k_div = [a_c] if a_c else []
    bm, bn, bk = _mm_tiles(m, n, kk, n_div, k_div, jnp.dtype(out_dtype).itemsize, addend is not None, k_piece,
                           mode == "tn")
    nk = kk // bk
    kb = bk // k_piece if k_piece else 1

    def cs_idx(blk, per):
        return (blk * per[0]) // per[1], (blk * per[0] % per[1]) // per[0]

    if mode == "tn":
        a_spec = pl.BlockSpec((bk, bm), lambda i, j, k: (k, i))
    elif a_cs:
        def a_map(i, j, k):
            s, r = cs_idx(k, (bk, a_c))
            return (s, i, r)
        a_spec = pl.BlockSpec((None, bm, bk), a_map)
    else:
        a_spec = pl.BlockSpec((bm, bk), lambda i, j, k: (i, k))
    if mode == "nt":
        if b_cs:
            b_spec = pl.BlockSpec((kb, bn, b_c), lambda i, j, k: (k, j, 0))
        else:
            b_spec = pl.BlockSpec((bn, bk), lambda i, j, k: (j, k))
    else:
        if b_cs:
            def b_map(i, j, k):
                s, r = cs_idx(j, (bn, b_c))
                return (s, k, r)
            b_spec = pl.BlockSpec((None, bk, bn), b_map)
        else:
            b_spec = pl.BlockSpec((bk, bn), lambda i, j, k: (k, j))
    if o_cs:
        def o_map(i, j, k):
            s, r = cs_idx(j, (bn, o_c))
            return (s, i, r)
        o_spec = pl.BlockSpec((None, bm, bn), o_map)
        out_shape = jax.ShapeDtypeStruct((o_cs, m, o_c), out_dtype)
    else:
        o_spec = pl.BlockSpec((bm, bn), lambda i, j, k: (i, j))
        out_shape = jax.ShapeDtypeStruct((m, n), out_dtype)
    in_specs = [a_spec, b_spec]
    args = [a, b]
    if addend is not None:
        in_specs.append(pl.BlockSpec((bm, bn), lambda i, j, k: (i, j)))
        args.append(addend)
    dims = _DIMS[mode]
    in_specs += [ANY] * len(deps)
    args += list(deps)
    n_in = len(args)

    def body(*refs):
        a_ref, b_ref = refs[0], refs[1]
        add_ref = refs[2] if addend is not None else None
        o_ref = refs[n_in]
        if k_piece:
            d = None
            for p in range(kb):
                dp = lax.dot_general(a_ref[:, p * b_c:(p + 1) * b_c].astype(BF16), b_ref[p].astype(BF16), dims,
                                     preferred_element_type=F32)
                d = dp if d is None else d + dp
        else:
            d = lax.dot_general(a_ref[...].astype(BF16), b_ref[...].astype(BF16), dims, preferred_element_type=F32)

        def finish(r):
            if addend is not None:
                r = r + add_ref[...]
            o_ref[...] = r.astype(o_ref.dtype)

        if nk == 1:
            finish(d)
            return
        acc_ref = refs[n_in + 1]
        k = pl.program_id(2)

        @pl.when(k == 0)
        def _():
            acc_ref[...] = d

        if nk > 2:
            @pl.when((k > 0) & (k < nk - 1))
            def _():
                acc_ref[...] += d

        @pl.when(k == nk - 1)
        def _():
            finish(acc_ref[...] + d)

    return pl.pallas_call(
        body, name=name, out_shape=out_shape, grid=(m // bm, n // bn, nk),
        in_specs=in_specs, out_specs=o_spec,
        scratch_shapes=[pltpu.VMEM((bm, bn), F32)] if nk > 1 else [],
        compiler_params=_cp("parallel", "parallel", "arbitrary"),
    )(*args)


LN_ROWS = 256


def _ln_stats(s):
    mu = jnp.mean(s, axis=-1, keepdims=True)
    sc = s - mu
    var = jnp.mean(sc * sc, axis=-1, keepdims=True)
    rstd = lax.rsqrt(var + LN_EPS)
    return sc * rstd, rstd


def _ln_fwd(h, sub, g, b, name):
    t, d = h.shape
    row = pl.BlockSpec((LN_ROWS, d), lambda i: (i, 0))
    vec = pl.BlockSpec((1, d), lambda i: (0, 0))

    def body(h_ref, s_ref, g_ref, b_ref, y_ref, yb_ref):
        xhat, _ = _ln_stats(ALPHA * h_ref[...] + s_ref[...])
        y = xhat * g_ref[...] + b_ref[...]
        y_ref[...] = y
        yb_ref[...] = y.astype(BF16)

    return pl.pallas_call(
        body, name=name, grid=(t // LN_ROWS,),
        out_shape=(jax.ShapeDtypeStruct((t, d), F32), jax.ShapeDtypeStruct((t, d), BF16)),
        in_specs=[row, row, vec, vec], out_specs=(row, row), compiler_params=_cp("parallel"),
    )(h, sub, g, b)


def _ln_bwd_math(xhat, rstd, dy, g):
    dxhat = dy * g
    m1 = jnp.mean(dxhat, axis=-1, keepdims=True)
    m2 = jnp.mean(dxhat * xhat, axis=-1, keepdims=True)
    ds = rstd * (dxhat - m1 - xhat * m2)
    dg = jnp.sum(dy * xhat, axis=0, keepdims=True)
    db = jnp.sum(dy, axis=0, keepdims=True)
    return ds, dg, db


def _ln_bwd(h, sub, dy, g, name):
    t, d = h.shape
    row = pl.BlockSpec((LN_ROWS, d), lambda i: (i, 0))
    vec = pl.BlockSpec((1, d), lambda i: (0, 0))

    def body(h_ref, s_ref, dy_ref, g_ref, ds_ref, dres_ref, dg_ref, dbeta_ref):
        xhat, rstd = _ln_stats(ALPHA * h_ref[...] + s_ref[...])
        ds, dg, db = _ln_bwd_math(xhat, rstd, dy_ref[...], g_ref[...])
        ds_ref[...] = ds.astype(BF16)
        dres_ref[...] = ALPHA * ds

        @pl.when(pl.program_id(0) == 0)
        def _():
            dg_ref[...] = jnp.zeros_like(dg_ref)
            dbeta_ref[...] = jnp.zeros_like(dbeta_ref)

        dg_ref[...] += dg
        dbeta_ref[...] += db

    return pl.pallas_call(
        body, name=name, grid=(t // LN_ROWS,),
        out_shape=(jax.ShapeDtypeStruct((t, d), BF16), jax.ShapeDtypeStruct((t, d), F32),
                   jax.ShapeDtypeStruct((1, d), F32), jax.ShapeDtypeStruct((1, d), F32)),
        in_specs=[row, row, row, vec], out_specs=(row, row, vec, vec),
        compiler_params=_cp("arbitrary"),
    )(h, sub, dy, g)


def _ln_loss_bwd(h, sub, target, g, b, name):
    t, d = h.shape
    row = pl.BlockSpec((LN_ROWS, d), lambda i: (i, 0))
    vec = pl.BlockSpec((1, d), lambda i: (0, 0))
    lvec = pl.BlockSpec((1, 128), lambda i: (0, 0))

    def body(h_ref, s_ref, t_ref, g_ref, b_ref, loss_ref, ds_ref, dres_ref, dg_ref, dbeta_ref):
        xhat, rstd = _ln_stats(ALPHA * h_ref[...] + s_ref[...])
        y = xhat * g_ref[...] + b_ref[...]
        err = y - t_ref[...]
        part = 0.5 * jnp.sum(jnp.mean(err * err, axis=-1, keepdims=True), axis=0, keepdims=True)
        ds, dg, db = _ln_bwd_math(xhat, rstd, err * (1.0 / d), g_ref[...])
        ds_ref[...] = ds.astype(BF16)
        dres_ref[...] = ALPHA * ds

        @pl.when(pl.program_id(0) == 0)
        def _():
            loss_ref[...] = jnp.zeros_like(loss_ref)
            dg_ref[...] = jnp.zeros_like(dg_ref)
            dbeta_ref[...] = jnp.zeros_like(dbeta_ref)

        loss_ref[...] += jnp.broadcast_to(part, loss_ref.shape)
        dg_ref[...] += dg
        dbeta_ref[...] += db

    return pl.pallas_call(
        body, name=name, grid=(t // LN_ROWS,),
        out_shape=(jax.ShapeDtypeStruct((1, 128), F32), jax.ShapeDtypeStruct((t, d), BF16),
                   jax.ShapeDtypeStruct((t, d), F32), jax.ShapeDtypeStruct((1, d), F32),
                   jax.ShapeDtypeStruct((1, d), F32)),
        in_specs=[row, row, row, vec, vec], out_specs=(lvec, row, row, vec, vec),
        compiler_params=_cp("arbitrary"),
    )(h, sub, target, g, b)


HG_ROWS = 2048
HG_LEVELS = 7


def _hg_constants():
    c = HG_CHUNK
    t = np.arange(c)[:, None]
    j = np.arange(c)[None, :]
    blocks, masks = [], [np.eye(c)]
    for lev in range(HG_LEVELS):
        m = 1 << lev
        second = (t % (2 * m)) >= m
        mid = (t // (2 * m)) * (2 * m) + m - 1
        blocks.append((second & (j > mid) & (j <= t)) | ((~second) & (j > t) & (j <= mid)))
        same = (t // (2 * m)) == (j // (2 * m))
        masks.append(same & second & ((j % (2 * m)) < m))
    blocks += [j <= t, j > t, np.ones((c, c), bool)]
    cm = np.concatenate(blocks, axis=0).astype(np.float32)
    mk = np.concatenate(masks, axis=0).astype(np.float32)
    return jnp.asarray(cm, dtype=BF16), jnp.asarray(mk)


def _dot_01(cm, x, mode):
    hi = x.astype(BF16)
    lo = (x - hi.astype(F32)).astype(BF16)
    both = lax.dot_general(cm, jnp.concatenate([hi, lo], axis=1), _DIMS[mode], preferred_element_type=F32)
    n = x.shape[-1]
    return both[:, :n] + both[:, n:]


@jax.custom_vjp
def _prefix(cm, lf):
    return _dot_01(cm, lf, "nn")


def _prefix_fwd(cm, lf):
    return _prefix(cm, lf), cm


def _prefix_bwd(cm, d):
    return jnp.zeros_like(cm), _dot_01(cm, d, "tn")


_prefix.defvjp(_prefix_fwd, _prefix_bwd)

_BWD = {"nn": (("nt", 0, 1), ("tn", 1, 0)), "nt": (("nn", 0, 1), ("tn", 0, 1)), "tn": (("nt", 1, 0), ("nn", 1, 0))}


def _bdot_raw(a, b, mode):
    return lax.dot_general(a.astype(BF16), b.astype(BF16), _DIMS[mode], preferred_element_type=F32)


@functools.partial(jax.custom_vjp, nondiff_argnums=(2,))
def _bdot(a, b, mode):
    return _bdot_raw(a, b, mode)


def _bdot_fwd(a, b, mode):
    return _bdot_raw(a, b, mode), (a, b)


def _bdot_bwd(mode, res, d):
    a, b = res
    (ma, da_pos, _), (mb, db_pos, _) = _BWD[mode]
    da = _bdot_raw(d, b, ma) if da_pos == 0 else _bdot_raw(b, d, ma)
    db = _bdot_raw(d, a, mb) if db_pos == 0 else _bdot_raw(a, d, mb)
    return da, db


_bdot.defvjp(_bdot_fwd, _bdot_bwd)


def _hg_chunk(pq, pf, pi, pg, l0, l1, l2, ng, s_t, cm, mk):
    c = HG_CHUNK
    mx = jnp.maximum(jnp.maximum(l0, l1), l2)
    e0, e1, e2 = jnp.exp(l0 - mx), jnp.exp(l1 - mx), jnp.exp(l2 - mx)
    lb = e0 / (e0 + e1 + e2)
    q = pq * jax.nn.sigmoid(pq)
    t1 = jnp.log(lb)
    t2 = jnp.log1p(-lb) + jax.nn.log_sigmoid(pf)
    lf = jnp.maximum(t1, t2) + jnp.log1p(jnp.exp(-jnp.abs(t1 - t2)))
    k = (1.0 - lb) * jax.nn.sigmoid(-pf)
    x = jnp.exp(_prefix(cm, lf))
    scores = mk[0:c] * _bdot(q, k, "nt")
    for lev in range(HG_LEVELS):
        xl = x[lev * c:(lev + 1) * c]
        scores = scores + mk[(lev + 1) * c:(lev + 2) * c] * _bdot(q * xl, k * xl, "nt")
    x_incl = x[HG_LEVELS * c:(HG_LEVELS + 1) * c]
    x_after = x[(HG_LEVELS + 1) * c:(HG_LEVELS + 2) * c]
    x_total = x[(HG_LEVELS + 2) * c:(HG_LEVELS + 3) * c]
    o = _bdot(scores, pi, "nn") + _bdot(q * x_incl, s_t, "nt")
    s_new = s_t * jnp.concatenate([x_total] * (HG_DIM // c), axis=0) + _bdot(pi, k * x_after, "tn")
    rstd = lax.rsqrt(jnp.mean(o * o, axis=-1, keepdims=True) + RMS_EPS)
    y = o * rstd * ng * (pg * jax.nn.sigmoid(pg))
    return y, s_new


def _hg_specs(t, reverse):
    nrb = t // HG_ROWS
    rb = (lambda r: nrb - 1 - r) if reverse else (lambda r: r)
    proj = pl.BlockSpec((4, HG_ROWS, HG_DIM), lambda h, r: (0, rb(r), h))
    vec = pl.BlockSpec((1, HG_DIM), lambda h, r: (0, h))
    cm = pl.BlockSpec(((HG_LEVELS + 3) * HG_CHUNK, HG_CHUNK), lambda h, r: (0, 0))
    mk = pl.BlockSpec(((HG_LEVELS + 1) * HG_CHUNK, HG_CHUNK), lambda h, r: (0, 0))
    rows = pl.BlockSpec((HG_ROWS, HG_DIM), lambda h, r: (rb(r), h))
    states = pl.BlockSpec((None, HG_ROWS // HG_CHUNK, HG_DIM, HG_DIM), lambda h, r: (h, rb(r), 0, 0))
    return proj, vec, cm, mk, rows, states


def _hg_fwd(proj, l0, l1, l2, ng):
    t = proj.shape[1]
    n_in = HG_ROWS // HG_CHUNK
    cm, mk = _hg_constants()
    p_spec, vec, cm_spec, mk_spec, rows, st_spec = _hg_specs(t, False)

    def body(p_ref, l0_ref, l1_ref, l2_ref, ng_ref, cm_ref, mk_ref, y_ref, st_ref, s_ref):
        @pl.when(pl.program_id(1) == 0)
        def _():
            s_ref[...] = jnp.zeros_like(s_ref)

        def step(ci, carry):
            sl = pl.ds(pl.multiple_of(ci * HG_CHUNK, HG_CHUNK), HG_CHUNK)
            s_t = s_ref[...]
            st_ref[ci] = s_t
            y, s_new = _hg_chunk(p_ref[0, sl, :], p_ref[1, sl, :], p_ref[2, sl, :], p_ref[3, sl, :],
                                 l0_ref[...], l1_ref[...], l2_ref[...], ng_ref[...], s_t,
                                 cm_ref[...], mk_ref[...])
            y_ref[sl, :] = y.astype(BF16)
            s_ref[...] = s_new
            return carry

        lax.fori_loop(0, n_in, step, 0, unroll=True)

    return pl.pallas_call(
        body, name="hg_fwd", grid=(HG_HEADS, t // HG_ROWS),
        out_shape=(jax.ShapeDtypeStruct((t, D_MODEL), BF16),
                   jax.ShapeDtypeStruct((HG_HEADS, t // HG_CHUNK, HG_DIM, HG_DIM), F32)),
        in_specs=[p_spec, vec, vec, vec, vec, cm_spec, mk_spec], out_specs=(rows, st_spec),
        scratch_shapes=[pltpu.VMEM((HG_DIM, HG_DIM), F32)],
        compiler_params=_cp("parallel", "arbitrary"),
    )(proj, l0, l1, l2, ng, cm, mk)


def _hg_bwd(proj, states, dy, l0, l1, l2, ng):
    t = proj.shape[1]
    n_in = HG_ROWS // HG_CHUNK
    cm, mk = _hg_constants()
    p_spec, vec, cm_spec, mk_spec, rows, st_spec = _hg_specs(t, True)

    def body(p_ref, st_ref, dy_ref, l0_ref, l1_ref, l2_ref, ng_ref, cm_ref, mk_ref,
             dp_ref, dl0_ref, dl1_ref, dl2_ref, dng_ref, ds_ref):
        @pl.when(pl.program_id(1) == 0)
        def _():
            ds_ref[...] = jnp.zeros_like(ds_ref)
            for r in (dl0_ref, dl1_ref, dl2_ref, dng_ref):
                r[...] = jnp.zeros_like(r)

        def step(it, carry):
            ci = n_in - 1 - it
            sl = pl.ds(pl.multiple_of(ci * HG_CHUNK, HG_CHUNK), HG_CHUNK)
            fn = functools.partial(_hg_chunk, cm=cm_ref[...], mk=mk_ref[...])
            _, vjp = jax.vjp(fn, p_ref[0, sl, :], p_ref[1, sl, :], p_ref[2, sl, :], p_ref[3, sl, :],
                             l0_ref[...], l1_ref[...], l2_ref[...], ng_ref[...], st_ref[ci])
            dq, df, di, dg, d0, d1, d2, dn, ds = vjp((dy_ref[sl, :], ds_ref[...]))
            dp_ref[0, sl, :] = dq.astype(BF16)
            dp_ref[1, sl, :] = df.astype(BF16)
            dp_ref[2, sl, :] = di.astype(BF16)
            dp_ref[3, sl, :] = dg.astype(BF16)
            dl0_ref[...] += d0
            dl1_ref[...] += d1
            dl2_ref[...] += d2
            dng_ref[...] += dn
            ds_ref[...] = ds
            return carry

        lax.fori_loop(0, n_in, step, 0, unroll=True)

    v_shape = jax.ShapeDtypeStruct((1, D_MODEL), F32)
    return pl.pallas_call(
        body, name="hg_bwd", grid=(HG_HEADS, t // HG_ROWS),
        out_shape=(jax.ShapeDtypeStruct((4, t, D_MODEL), BF16), v_shape, v_shape, v_shape, v_shape),
        in_specs=[p_spec, st_spec, rows, vec, vec, vec, vec, cm_spec, mk_spec],
        out_specs=(p_spec, vec, vec, vec, vec),
        scratch_shapes=[pltpu.VMEM((HG_DIM, HG_DIM), F32)],
        compiler_params=_cp("parallel", "arbitrary"),
    )(proj, states, dy, l0, l1, l2, ng, cm, mk)


_SQRT_HALF = 0.7071067811865476
_INV_SQRT_2PI = 0.3989422804014327


def _gelu(x):
    return 0.5 * x * (1.0 + lax.erf(x * _SQRT_HALF))


def _gelu_grad(x):
    return 0.5 * (1.0 + lax.erf(x * _SQRT_HALF)) + x * (_INV_SQRT_2PI * jnp.exp(-0.5 * x * x))


def _tril(n):
    return (lax.broadcasted_iota(jnp.int32, (n, n), 0) >= lax.broadcasted_iota(jnp.int32, (n, n), 1)).astype(F32)


def _sg_specs(d):
    row = lambda w: pl.BlockSpec((SG_CHUNK, w), lambda i: (i, 0))
    vec = pl.BlockSpec((1, d), lambda i: (0, 0))
    cube = pl.BlockSpec((SG_GROUPS, SG_CHUNK, SG_CHUNK), lambda i: (0, 0, 0))
    return row, vec, cube


def _sg_fwd(pre, ln_g, ln_b, w_s, bias):
    t = pre.shape[0]
    d = D_MODEL
    row, vec, cube = _sg_specs(d)

    def body(pre_ref, g_ref, b_ref, w_ref, bias_ref, y_ref, vln_ref):
        u = _gelu(pre_ref[:, :d])
        vhat, _ = _ln_stats(_gelu(pre_ref[:, d:]))
        vln_ref[...] = vhat * g_ref[...] + b_ref[...]
        tril = _tril(SG_CHUNK)
        for g in range(SG_GROUPS):
            cs = slice(g * SG_DIM, (g + 1) * SG_DIM)
            gate = _bdot_raw(w_ref[g] * tril, vln_ref[:, cs], "nn") + bias_ref[g]
            y_ref[:, cs] = (u[:, cs] * gate).astype(BF16)

    return pl.pallas_call(
        body, name="sg_fwd", grid=(t // SG_CHUNK,), out_shape=jax.ShapeDtypeStruct((t, d), BF16),
        in_specs=[row(2 * d), vec, vec, cube, cube], out_specs=row(d),
        scratch_shapes=[pltpu.VMEM((SG_CHUNK, d), F32)], compiler_params=_cp("parallel"),
    )(pre, ln_g, ln_b, w_s, bias)


def _sg_bwd(pre, dy, ln_g, ln_b, w_s, bias):
    t = pre.shape[0]
    d = D_MODEL
    row, vec, cube = _sg_specs(d)
    dbs_spec = pl.BlockSpec((SG_GROUPS, 8, SG_CHUNK), lambda i: (0, 0, 0))

    def body(pre_ref, dy_ref, g_ref, b_ref, w_ref, bias_ref, dpre_ref, dw_ref, dbs_ref, dlg_ref, dlb_ref,
             vln_ref, dvln_ref):
        @pl.when(pl.program_id(0) == 0)
        def _():
            for r in (dw_ref, dbs_ref, dlg_ref, dlb_ref):
                r[...] = jnp.zeros_like(r)

        pu = pre_ref[:, :d]
        pv = pre_ref[:, d:]
        u = _gelu(pu)
        vhat, rstd = _ln_stats(_gelu(pv))
        vln_ref[...] = vhat * g_ref[...] + b_ref[...]
        tril = _tril(SG_CHUNK)
        ones = jnp.ones((8, SG_DIM), F32)
        for g in range(SG_GROUPS):
            cs = slice(g * SG_DIM, (g + 1) * SG_DIM)
            wc = w_ref[g] * tril
            vg = vln_ref[:, cs]
            gate = _bdot_raw(wc, vg, "nn") + bias_ref[g]
            dyg = dy_ref[:, cs]
            dgate = dyg * u[:, cs]
            dpre_ref[:, cs] = (dyg * gate * _gelu_grad(pu[:, cs])).astype(BF16)
            dw_ref[g] += tril * _bdot_raw(dgate, vg, "nt")
            dbs_ref[g] += lax.dot_general(ones, dgate, _DIMS["nt"], precision=HI, preferred_element_type=F32)
            dvln_ref[:, cs] = _bdot_raw(wc, dgate, "tn")
        dvln = dvln_ref[...]
        dv, dg, db = _ln_bwd_math(vhat, rstd, dvln, g_ref[...])
        dlg_ref[...] += dg
        dlb_ref[...] += db
        dpre_ref[:, d:] = (dv * _gelu_grad(pv)).astype(BF16)

    return pl.pallas_call(
        body, name="sg_bwd", grid=(t // SG_CHUNK,),
        out_shape=(jax.ShapeDtypeStruct((t, 2 * d), BF16), jax.ShapeDtypeStruct(w_s.shape, F32),
                   jax.ShapeDtypeStruct((SG_GROUPS, 8, SG_CHUNK), F32),
                   jax.ShapeDtypeStruct((1, d), F32), jax.ShapeDtypeStruct((1, d), F32)),
        in_specs=[row(2 * d), row(d), vec, vec, cube, cube],
        out_specs=(row(2 * d), cube, dbs_spec, vec, vec),
        scratch_shapes=[pltpu.VMEM((SG_CHUNK, d), F32), pltpu.VMEM((SG_CHUNK, d), F32)],
        compiler_params=_cp("arbitrary"),
    )(pre, dy, ln_g, ln_b, w_s, bias)


FFN_ROWS = 1024
FFN_COLS = 1408
HALO = 16


def _ffn_conv(a_ext, w_ref, cb_ref):
    a1 = pltpu.roll(a_ext, 1, 0)
    a2 = pltpu.roll(a_ext, 2, 0)
    return w_ref[0:1, :] * a2 + w_ref[1:2, :] * a1 + w_ref[2:3, :] * a_ext + cb_ref[...], a1, a2


def _ffn_fwd(hh, conv_w, conv_b):
    t = hh.shape[1]
    nb8 = FFN_ROWS // HALO
    main = pl.BlockSpec((2, FFN_ROWS, FFN_COLS), lambda c, r: (0, r, c))
    prev = pl.BlockSpec((None, HALO, FFN_COLS), lambda c, r: (0, jnp.maximum(r * nb8 - 1, 0), c))
    wspec = pl.BlockSpec((3, FFN_COLS), lambda c, r: (0, c))
    bspec = pl.BlockSpec((1, FFN_COLS), lambda c, r: (0, c))

    def body(m_ref, p_ref, w_ref, cb_ref, z_ref):
        prev = jnp.where(pl.program_id(1) == 0, 0.0, p_ref[...].astype(F32))
        a_ext = jnp.concatenate([prev, m_ref[0].astype(F32)], axis=0)
        a, _, _ = _ffn_conv(a_ext, w_ref, cb_ref)
        a = a[HALO:]
        z_ref[...] = (a * jax.nn.sigmoid(a) * m_ref[1].astype(F32)).astype(BF16)

    return pl.pallas_call(
        body, name="ffn_fwd", grid=(D_FF // FFN_COLS, t // FFN_ROWS),
        out_shape=jax.ShapeDtypeStruct((t, D_FF), BF16),
        in_specs=[main, prev, wspec, bspec],
        out_specs=pl.BlockSpec((FFN_ROWS, FFN_COLS), lambda c, r: (r, c)),
        compiler_params=_cp("parallel", "parallel"),
    )(hh, hh, conv_w, conv_b)


def _ffn_bwd(hh, dz, conv_w, conv_b):
    t = hh.shape[1]
    nb8 = FFN_ROWS // HALO
    last8 = t // HALO - 1
    nr = t // FFN_ROWS
    main = pl.BlockSpec((2, FFN_ROWS, FFN_COLS), lambda c, r: (0, r, c))
    prev = pl.BlockSpec((None, HALO, FFN_COLS), lambda c, r: (0, jnp.maximum(r * nb8 - 1, 0), c))
    nxt = pl.BlockSpec((2, HALO, FFN_COLS), lambda c, r: (0, jnp.minimum((r + 1) * nb8, last8), c))
    dmain = pl.BlockSpec((FFN_ROWS, FFN_COLS), lambda c, r: (r, c))
    dnxt = pl.BlockSpec((HALO, FFN_COLS), lambda c, r: (jnp.minimum((r + 1) * nb8, last8), c))
    wspec = pl.BlockSpec((3, FFN_COLS), lambda c, r: (0, c))
    bspec = pl.BlockSpec((1, FFN_COLS), lambda c, r: (0, c))

    def body(m_ref, p_ref, n_ref, dz_ref, dzn_ref, w_ref, cb_ref, dh_ref, dw_ref, dcb_ref):
        r = pl.program_id(1)

        @pl.when(r == 0)
        def _():
            dw_ref[...] = jnp.zeros_like(dw_ref)
            dcb_ref[...] = jnp.zeros_like(dcb_ref)

        prev = jnp.where(r == 0, 0.0, p_ref[...].astype(F32))
        a_ext = jnp.concatenate([prev, m_ref[0].astype(F32), n_ref[0].astype(F32)], axis=0)
        a, a1, a2 = _ffn_conv(a_ext, w_ref, cb_ref)
        a = a[HALO:]
        b_ext = jnp.concatenate([m_ref[1], n_ref[1]], axis=0).astype(F32)
        dz_main = dz_ref[...].astype(F32)
        dz_ext = jnp.concatenate([dz_main, jnp.where(r == nr - 1, 0.0, dzn_ref[...].astype(F32))], axis=0)
        sig = jax.nn.sigmoid(a)
        da = dz_ext * b_ext * (sig * (1.0 + a * (1.0 - sig)))
        n_ext = FFN_ROWS + HALO
        da_p1 = pltpu.roll(da, n_ext - 1, 0)
        da_p2 = pltpu.roll(da, n_ext - 2, 0)
        da_raw = w_ref[2:3, :] * da + w_ref[1:2, :] * da_p1 + w_ref[0:1, :] * da_p2
        dh_ref[0] = da_raw[:FFN_ROWS].astype(BF16)
        dh_ref[1] = (dz_main * (a * sig)[:FFN_ROWS]).astype(BF16)
        dam = da[:FFN_ROWS]
        rows = slice(HALO, HALO + FFN_ROWS)
        dw_ref[0:1, :] += jnp.sum(dam * a2[rows], axis=0, keepdims=True)
        dw_ref[1:2, :] += jnp.sum(dam * a1[rows], axis=0, keepdims=True)
        dw_ref[2:3, :] += jnp.sum(dam * a_ext[rows], axis=0, keepdims=True)
        dcb_ref[...] += jnp.sum(dam, axis=0, keepdims=True)

    return pl.pallas_call(
        body, name="ffn_bwd", grid=(D_FF // FFN_COLS, nr),
        out_shape=(jax.ShapeDtypeStruct((2, t, D_FF), BF16), jax.ShapeDtypeStruct((3, D_FF), F32),
                   jax.ShapeDtypeStruct((1, D_FF), F32)),
        in_specs=[main, prev, nxt, dmain, dnxt, wspec, bspec],
        out_specs=(main, wspec, bspec), compiler_params=_cp("parallel", "arbitrary"),
    )(hh, hh, hh, dz, dz, conv_w, conv_b)


def _adamw_math(w, g, m, v):
    m = ADAM_B1 * m + (1.0 - ADAM_B1) * g
    v = ADAM_B2 * v + (1.0 - ADAM_B2) * (g * g)
    m_hat = m / (1.0 - ADAM_B1 ** ADAM_STEP)
    v_hat = v / (1.0 - ADAM_B2 ** ADAM_STEP)
    delta = -ADAM_LR * (m_hat / (jnp.sqrt(v_hat) + ADAM_EPS) + ADAM_WD * w)
    return delta, m, v


def _as2d(shape):
    n = int(np.prod(shape))
    c = shape[-1] if shape[-1] % 128 == 0 else (128 if n % 128 == 0 else shape[-1])
    return n // c, c


def _adamw(w, g, m, v, name, parts=None, own=None, is_me=None, layer=None, prev=None):
    shape = w.shape if layer is None else w.shape[1:]
    r, c = _as2d(shape)
    br = r if r <= 512 else _pick(r, (256, 176, 128, 64, 8))
    blk = pl.BlockSpec((br, c), lambda i: (i, 0))
    if layer is None:
        wblk, o = blk, jax.ShapeDtypeStruct((r, c), F32)
        w2, m2, v2 = (a.reshape(r, c) for a in (w, m, v))
    else:
        wblk, o = pl.BlockSpec((None, br, c), lambda i: (layer, i, 0)), jax.ShapeDtypeStruct((w.shape[0], r, c), F32)
        w2, m2, v2 = (a.reshape(w.shape[0], r, c) for a in (w, m, v))
    extra, extra_specs = [], []
    if parts is not None:
        g_in = parts.reshape(N_DEV, r, c)
        g_spec = pl.BlockSpec((N_DEV, br, c), lambda i: (0, i, 0))
        extra = [own.reshape(r, c), is_me]
        extra_specs = [blk, pl.BlockSpec(memory_space=pltpu.SMEM)]
    else:
        g_in = g.reshape(r, c)
        g_spec = blk
    aliases = {}
    if prev is not None:
        aliases = {4 + len(extra) + j: j for j in range(4)}
        extra = extra + [a.reshape(o.shape) for a in prev]
        extra_specs = extra_specs + [ANY] * 4

    def body(w_ref, g_ref, m_ref, v_ref, *rest):
        go_ref, d_ref, mo_ref, vo_ref = rest[-4:]
        if parts is not None:
            own_ref, me_ref = rest[:2]
            gsum = None
            for i in range(N_DEV):
                term = jnp.where(me_ref[i] > 0.5, own_ref[...], g_ref[i]).astype(F32)
                gsum = term if gsum is None else gsum + term
        else:
            gsum = g_ref[...]
        delta, m_new, v_new = _adamw_math(w_ref[...], gsum, m_ref[...], v_ref[...])
        go_ref[...] = gsum
        d_ref[...] = delta
        mo_ref[...] = m_new
        vo_ref[...] = v_new

    outs = pl.pallas_call(
        body, name=name, grid=(r // br,), out_shape=(o, o, o, o),
        in_specs=[wblk, g_spec, wblk, wblk] + extra_specs, out_specs=(wblk, wblk, wblk, wblk),
        input_output_aliases=aliases, compiler_params=_cp("parallel"),
    )(w2, g_in, m2, v2, *extra)
    return tuple(a.reshape(w.shape) for a in outs)


ALL_PEERS = tuple(range(1, N_DEV))
OTHER_CHIPS = (2, 4, 6)
SIBLING = (1,)


def _me_and_peers(relations=ALL_PEERS):
    x, y, c = lax.axis_index("x"), lax.axis_index("y"), lax.axis_index("c")
    me = 4 * x + 2 * y + c
    peers = []
    for k in relations:
        kx, ky, kc = (k >> 2) & 1, (k >> 1) & 1, k & 1
        px, py, pc = x ^ kx, y ^ ky, c ^ kc
        peers.append(((px, py, pc), 4 * px + 2 * py + pc))
    return me, peers


def _all_gather(shard, name):
    def body(x_ref, o_ref, send_sems, recv_sems, local_sem):
        me, peers = _me_and_peers()
        mine = pltpu.make_async_copy(x_ref, o_ref.at[me], local_sem)
        mine.start()
        sends = []
        for k, (dev, _) in enumerate(peers):
            cp = pltpu.make_async_remote_copy(src_ref=x_ref, dst_ref=o_ref.at[me], send_sem=send_sems.at[k],
                                              recv_sem=recv_sems.at[k], device_id=dev, device_id_type=MESH)
            cp.start()
            sends.append(cp)
        for k, (dev, idx) in enumerate(peers):
            pltpu.make_async_remote_copy(src_ref=x_ref, dst_ref=o_ref.at[idx], send_sem=send_sems.at[k],
                                         recv_sem=recv_sems.at[k], device_id=dev, device_id_type=MESH).wait_recv()
        for cp in sends:
            cp.wait_send()
        mine.wait()

    return pl.pallas_call(
        body, name=name, out_shape=jax.ShapeDtypeStruct((N_DEV,) + shard.shape, shard.dtype),
        in_specs=[ANY], out_specs=ANY,
        scratch_shapes=[pltpu.SemaphoreType.DMA((N_DEV - 1,)), pltpu.SemaphoreType.DMA((N_DEV - 1,)),
                        pltpu.SemaphoreType.DMA],
        compiler_params=pltpu.CompilerParams(has_side_effects=True),
    )(shard)


HBM = pl.BlockSpec(memory_space=pltpu.HBM)
SEM = pl.BlockSpec(memory_space=pltpu.SEMAPHORE)
EFFECT = pltpu.SideEffectType.DATAFLOW_SIDE_EFFECTING


def _xchg_start(parts, land, *, dep, name, peers=ALL_PEERS, pieces=(0,)):
    n_buf = 1 if parts is None else 2

    def body(*refs):
        land_ref = refs[n_buf - 1]
        send_sem, recv_sem = refs[n_buf + 1], refs[n_buf + 2]
        token = refs[2 * n_buf + 3]
        me, to = _me_and_peers(peers)
        for dev, idx in to:
            if parts is None:
                for r in pieces:
                    piece = land_ref.at[me ^ r]
                    pltpu.make_async_remote_copy(src_ref=piece, dst_ref=piece, send_sem=send_sem, recv_sem=recv_sem,
                                                 device_id=dev, device_id_type=MESH).start()
            else:
                pltpu.make_async_remote_copy(src_ref=refs[0].at[idx], dst_ref=land_ref.at[me], send_sem=send_sem,
                                             recv_sem=recv_sem, device_id=dev, device_id_type=MESH).start()
        token[...] = jnp.zeros_like(token)

    bufs = [land] if parts is None else [parts, land]
    return pl.pallas_call(
        body, name=name,
        out_shape=(pltpu.SemaphoreType.DMA(()), pltpu.SemaphoreType.DMA(()),
                   *[pltpu.HBM(b.shape, b.dtype) for b in bufs], jax.ShapeDtypeStruct((8, 128), F32)),
        in_specs=(*[HBM] * n_buf, ANY), out_specs=(SEM, SEM, *[HBM] * n_buf, pl.BlockSpec(memory_space=pltpu.VMEM)),
        input_output_aliases={i: 2 + i for i in range(n_buf)},
        compiler_params=pltpu.CompilerParams(has_side_effects=EFFECT),
    )(*[pltpu.with_memory_space_constraint(b, pltpu.HBM) for b in bufs], dep)


def _xchg_wait(handles, after, name, n_pieces=N_DEV - 1):
    send_sem, recv_sem, *bufs, _ = handles
    n_buf = len(bufs)

    def body(*refs):
        land_ref = refs[n_buf - 1]
        send_sem, recv_sem = refs[n_buf], refs[n_buf + 1]
        x, y, c = lax.axis_index("x"), lax.axis_index("y"), lax.axis_index("c")
        span = land_ref.at[pl.ds(0, n_pieces)]
        cp = pltpu.make_async_remote_copy(src_ref=span, dst_ref=span, send_sem=send_sem, recv_sem=recv_sem,
                                          device_id=(x, y, c), device_id_type=MESH)
        cp.wait_send()
        cp.wait_recv()

    return pl.pallas_call(
        body, name=name, out_shape=tuple(pltpu.HBM(b.shape, b.dtype) for b in bufs),
        in_specs=(*[HBM] * n_buf, SEM, SEM, ANY), out_specs=tuple([HBM] * n_buf),
        input_output_aliases={i: i for i in range(n_buf)},
        compiler_params=pltpu.CompilerParams(has_side_effects=EFFECT),
    )(*bufs, send_sem, recv_sem, after)


def _sum_devices(v, name):
    _, r, c = v.shape

    def body(x_ref, o_ref):
        acc = x_ref[0]
        for i in range(1, N_DEV):
            acc = acc + x_ref[i]
        o_ref[...] = acc

    return pl.pallas_call(
        body, name=name, out_shape=jax.ShapeDtypeStruct((r, c), F32),
        in_specs=[pl.BlockSpec(memory_space=pltpu.VMEM)], out_specs=pl.BlockSpec(memory_space=pltpu.VMEM),
        compiler_params=pltpu.CompilerParams(vmem_limit_bytes=VMEM_LIMIT),
    )(v)


def _local_step(x, target, lb_logits, p, weight, send_grad, send_small):
    g = {}
    l0, l1, l2 = lb_logits[0:1], lb_logits[1:2], lb_logits[2:3]
    x_bf = x.astype(BF16)

    proj0 = _mm(x_bf, weight("hg_w_in", x_bf), mode="nn", b_cs=True, o_cs=4, name="mm_hg_in")
    y0, states = _hg_fwd(proj0, l0, l1, l2, p["hg_norm_g"])
    mixed0 = _mm(y0, weight("hg_w_out", y0), mode="nn", name="mm_hg_out")
    hh_l, z_l = [], []

    def ffn_forward(layer, h_in_bf):
        hh = _mm(h_in_bf, weight(f"ffn_w_up{layer}", h_in_bf), mode="nn", b_cs=True, o_cs=2, out_dtype=BF16,
                 name=f"mm_up{layer}")
        z = _ffn_fwd(hh, p["ffn_conv_w"][layer], p["ffn_conv_b"][layer])
        ffn = _mm(z, weight(f"ffn_w_down{layer}", z), mode="nn", name=f"mm_down{layer}")
        hh_l.append(hh)
        z_l.append(z)
        return ffn

    h1, h1b = _ln_fwd(x, mixed0, p["ln1_g"][0], p["ln1_b"][0], "ln1_0")
    ffn0 = ffn_forward(0, h1b)
    h2, h2b = _ln_fwd(h1, ffn0, p["ln2_g"][0], p["ln2_b"][0], "ln2_0")
    pre1 = _mm(h2b, weight("sg_w_in", h2b), mode="nn", b_cs=True, name="mm_sg_in")
    y1 = _sg_fwd(pre1, p["sg_ln_g"], p["sg_ln_b"], p["sg_w_s"], p["sg_bias"])
    mixed1 = _mm(y1, weight("sg_w_out", y1), mode="nn", name="mm_sg_out")
    h3, h3b = _ln_fwd(h2, mixed1, p["ln1_g"][1], p["ln1_b"][1], "ln1_1")
    ffn1 = ffn_forward(1, h3b)

    loss, ds, dres, g["ln2_g1"], g["ln2_b1"] = _ln_loss_bwd(h3, ffn1, target, p["ln2_g"][1], p["ln2_b"][1], "ln2_1_loss")

    def ffn_backward(layer, ds_bf, h_in_bf, dres):
        tok = send_grad(f"ffn_w_down{layer}",
                        _mm(z_l[layer], ds_bf, mode="tn", out_dtype=BF16, name=f"mm_dw_down{layer}"))
        dz = _mm(ds_bf, weight(f"ffn_w_down{layer}", None), mode="nt", out_dtype=BF16, name=f"mm_d_z{layer}",
                 deps=(tok,))
        dhh, g[f"conv_w{layer}"], g[f"conv_b{layer}"] = _ffn_bwd(hh_l[layer], dz, p["ffn_conv_w"][layer],
                                                                   p["ffn_conv_b"][layer])
        tok = send_grad(f"ffn_w_up{layer}", _mm(h_in_bf, dhh, mode="tn", b_cs=True, o_cs=N_DEV, out_dtype=BF16,
                                                name=f"mm_dw_up{layer}"))
        return _mm(dhh, weight(f"ffn_w_up{layer}", None), mode="nt", a_cs=True, b_cs=True, addend=dres,
                   name=f"mm_d_up{layer}", deps=(tok,))

    dh3 = ffn_backward(1, ds, h3b, dres)
    ds, dres, g["ln1_g1"], g["ln1_b1"] = _ln_bwd(h2, mixed1, dh3, p["ln1_g"][1], "ln1_1_bwd")
    tok = send_grad("sg_w_out", _mm(y1, ds, mode="tn", out_dtype=BF16, name="mm_dw_sg_out"))
    dy1 = _mm(ds, weight("sg_w_out", None), mode="nt", name="mm_d_sg_out", deps=(tok,))
    dpre1, g["sg_w_s"], dbs, g["sg_ln_g"], g["sg_ln_b"] = _sg_bwd(pre1, dy1, p["sg_ln_g"], p["sg_ln_b"],
                                                                    p["sg_w_s"], p["sg_bias"])
    g["sg_b_s"] = dbs[:, 0, :]
    tok = send_grad("sg_w_in", _mm(h2b, dpre1, mode="tn", o_cs=N_DEV, out_dtype=BF16, name="mm_dw_sg_in"))
    dh2 = _mm(dpre1, weight("sg_w_in", None), mode="nt", b_cs=True, addend=dres, name="mm_d_sg_in", deps=(tok,))
    ds, dres, g["ln2_g0"], g["ln2_b0"] = _ln_bwd(h1, ffn0, dh2, p["ln2_g"][0], "ln2_0_bwd")
    dh1 = ffn_backward(0, ds, h1b, dres)
    ds, dres, g["ln1_g0"], g["ln1_b0"] = _ln_bwd(x, mixed0, dh1, p["ln1_g"][0], "ln1_0_bwd")
    tok = send_grad("hg_w_out", _mm(y0, ds, mode="tn", out_dtype=BF16, name="mm_dw_hg_out"))
    dy0 = _mm(ds, weight("hg_w_out", None), mode="nt", name="mm_d_hg_out", deps=(tok,))
    dproj, d0, d1, d2, g["hg_norm_g"] = _hg_bwd(proj0, states, dy0, l0, l1, l2, p["hg_norm_g"])
    g["lb_logits"] = jnp.concatenate([d0, d1, d2], axis=0)
    tok = send_small(g)
    tok = send_grad("hg_w_in", _mm(x_bf, dproj, mode="tn", b_cs=True, o_cs=N_DEV, out_dtype=BF16, name="mm_dw_hg_in",
                                   deps=(tok,)))
    grad_x = _mm(dproj, weight("hg_w_in", None), mode="nt", a_cs=True, b_cs=True, addend=dres, name="mm_d_hg_in",
                 deps=(tok,))
    return loss[0, 0], grad_x


_SMALL = ("lb_logits", "hg_norm_g", "sg_ln_g", "sg_ln_b", "sg_w_s", "sg_b_s", "ffn_conv_w", "ffn_conv_b",
          "ln1_g", "ln1_b", "ln2_g", "ln2_b")
_NAMES = ("lb_logits", "hg_w_in", "hg_norm_g", "hg_w_out", "sg_w_in", "sg_ln_g", "sg_ln_b", "sg_w_s", "sg_b_s",
          "sg_w_out", "ffn_w_up", "ffn_conv_w", "ffn_conv_b", "ffn_w_down", "ln1_g", "ln1_b", "ln2_g", "ln2_b")


def kernel(x, lb_logits, hg_w_in, hg_norm_g, hg_w_out, sg_w_in, sg_ln_g, sg_ln_b, sg_w_s, sg_b_s, sg_w_out, ffn_w_up, ffn_conv_w, ffn_conv_b, ffn_w_down, ln1_g, ln1_b, ln2_g, ln2_b, loss_target, m_lb_logits, m_hg_w_in, m_hg_norm_g, m_hg_w_out, m_sg_w_in, m_sg_ln_g, m_sg_ln_b, m_sg_w_s, m_sg_b_s, m_sg_w_out, m_ffn_w_up, m_ffn_conv_w, m_ffn_conv_b, m_ffn_w_down, m_ln1_g, m_ln1_b, m_ln2_g, m_ln2_b, v_lb_logits, v_hg_w_in, v_hg_norm_g, v_hg_w_out, v_sg_w_in, v_sg_ln_g, v_sg_ln_b, v_sg_w_s, v_sg_b_s, v_sg_w_out, v_ffn_w_up, v_ffn_conv_w, v_ffn_conv_b, v_ffn_w_down, v_ln1_g, v_ln1_b, v_ln2_g, v_ln2_b):
    w = dict(lb_logits=lb_logits, hg_w_in=hg_w_in, hg_norm_g=hg_norm_g, hg_w_out=hg_w_out, sg_w_in=sg_w_in,
             sg_ln_g=sg_ln_g, sg_ln_b=sg_ln_b, sg_w_s=sg_w_s, sg_b_s=sg_b_s, sg_w_out=sg_w_out, ffn_w_up=ffn_w_up,
             ffn_conv_w=ffn_conv_w, ffn_conv_b=ffn_conv_b, ffn_w_down=ffn_w_down, ln1_g=ln1_g, ln1_b=ln1_b,
             ln2_g=ln2_g, ln2_b=ln2_b)
    m = dict(lb_logits=m_lb_logits, hg_w_in=m_hg_w_in, hg_norm_g=m_hg_norm_g, hg_w_out=m_hg_w_out,
             sg_w_in=m_sg_w_in, sg_ln_g=m_sg_ln_g, sg_ln_b=m_sg_ln_b, sg_w_s=m_sg_w_s, sg_b_s=m_sg_b_s,
             sg_w_out=m_sg_w_out, ffn_w_up=m_ffn_w_up, ffn_conv_w=m_ffn_conv_w, ffn_conv_b=m_ffn_conv_b,
             ffn_w_down=m_ffn_w_down, ln1_g=m_ln1_g, ln1_b=m_ln1_b, ln2_g=m_ln2_g, ln2_b=m_ln2_b)
    v = dict(lb_logits=v_lb_logits, hg_w_in=v_hg_w_in, hg_norm_g=v_hg_norm_g, hg_w_out=v_hg_w_out,
             sg_w_in=v_sg_w_in, sg_ln_g=v_sg_ln_g, sg_ln_b=v_sg_ln_b, sg_w_s=v_sg_w_s, sg_b_s=v_sg_b_s,
             sg_w_out=v_sg_w_out, ffn_w_up=v_ffn_w_up, ffn_conv_w=v_ffn_conv_w, ffn_conv_b=v_ffn_conv_b,
             ffn_w_down=v_ffn_w_down, ln1_g=v_ln1_g, ln1_b=v_ln1_b, ln2_g=v_ln2_g, ln2_b=v_ln2_b)
    me = 4 * lax.axis_index("x") + 2 * lax.axis_index("y") + lax.axis_index("c")
    is_me = (jnp.arange(N_DEV) == me).astype(F32)
    d = D_MODEL

    shards = [("hg_w_in", hg_w_in[0]), ("hg_w_out", hg_w_out[0]), ("ffn_w_up0", ffn_w_up[0]),
              ("ffn_w_down0", ffn_w_down[0]), ("sg_w_in", sg_w_in[0]), ("sg_w_out", sg_w_out[0]),
              ("ffn_w_up1", ffn_w_up[1]), ("ffn_w_down1", ffn_w_down[1])]
    row_sharded = {"hg_w_out": (d, d), "sg_w_out": (d, d), "ffn_w_down0": (D_FF, d), "ffn_w_down1": (D_FF, d)}
    sv = jnp.zeros((8, 768), F32)
    sv = sv.at[0, :256].set(sg_ln_g[0]).at[1, :256].set(sg_ln_b[0]).at[2:8, :704].set(ffn_conv_w.reshape(6, 704))
    sv = _all_gather(sv, "ag_small")
    gathers, gathered, dep = {}, {}, sv
    for n, shard in shards:
        land = lax.dynamic_update_index_in_dim(lax.empty((N_DEV,) + shard.shape, BF16), shard.astype(BF16), me, 0)
        gathers[n] = _xchg_start(None, land, dep=dep, name=f"ag_{n}", peers=OTHER_CHIPS)
        dep = gathers[n][-1]
    last_start = dep
    handed = {}
    order = [n for n, _ in shards]

    def hand_over(n, after):
        (land,) = _xchg_wait(gathers[n], after, f"agw_{n}", n_pieces=len(OTHER_CHIPS))
        handed[n] = _xchg_start(None, land, dep=is_me, name=f"ah_{n}", peers=SIBLING, pieces=(0,) + OTHER_CHIPS)

    def weight(n, after):
        if n not in gathered:
            if n == order[0]:
                hand_over(n, last_start)
                after = handed[n][-1]
            nxt = order.index(n) + 1
            if nxt < len(order):
                hand_over(order[nxt], after)
                after = handed[order[nxt]][-1]
            (full,) = _xchg_wait(handed[n], after, f"ahw_{n}", n_pieces=1 + len(OTHER_CHIPS))
            gathered[n] = full.reshape(row_sharded[n]) if n in row_sharded else full
        return gathered[n]

    grad_sends = {}

    def send_grad(n, parts):
        shape = (N_DEV,) + dict(shards)[n].shape
        grad_sends[n] = _xchg_start(parts.reshape(shape), lax.empty(shape, parts.dtype), dep=is_me, name=f"rs_{n}")
        return grad_sends[n][-1]

    small_send = {}

    def send_small(g):
        small = [g["lb_logits"], g["hg_norm_g"], g["sg_ln_g"], g["sg_ln_b"], g["sg_w_s"], g["sg_b_s"],
                 g["conv_w0"], g["conv_w1"], g["conv_b0"], g["conv_b1"], g["ln1_g0"], g["ln1_g1"],
                 g["ln1_b0"], g["ln1_b1"], g["ln2_g0"], g["ln2_g1"], g["ln2_b0"], g["ln2_b1"]]
        flat = jnp.concatenate([a.reshape(-1) for a in small])
        rows = -(-flat.shape[0] // (8 * 128)) * 8
        flat = jnp.pad(flat, (0, rows * 128 - flat.shape[0])).reshape(rows, 128)
        land = lax.dynamic_update_index_in_dim(lax.empty((N_DEV, rows, 128), F32), flat, me, 0)
        small_send["shapes"] = [a.shape for a in small]
        small_send["handles"] = _xchg_start(None, land, dep=is_me, name="ar_small")
        return small_send["handles"][-1]

    p = {}
    p["sg_ln_g"] = sv[:, 0, :256].reshape(1, d)
    p["sg_ln_b"] = sv[:, 1, :256].reshape(1, d)
    conv_w_full = jnp.transpose(sv[:, 2:8, :704].reshape(N_DEV, DEPTH, 3, 704), (1, 2, 0, 3)).reshape(DEPTH, 3, D_FF)
    p["ffn_conv_w"] = [conv_w_full[l] for l in range(DEPTH)]
    p["ffn_conv_b"] = [ffn_conv_b[l:l + 1] for l in range(DEPTH)]
    p["hg_norm_g"] = hg_norm_g
    p["sg_w_s"] = sg_w_s[0]
    p["sg_bias"] = jnp.broadcast_to(sg_b_s[0][:, :, None], (SG_GROUPS, SG_CHUNK, SG_DIM))
    for n in ("ln1_g", "ln1_b", "ln2_g", "ln2_b"):
        p[n] = [w[n][l:l + 1] for l in range(DEPTH)]

    loss_part, grad_x = _local_step(x[0], loss_target[0], lb_logits, p, weight, send_grad, send_small)
    loss = lax.psum(loss_part, ("x", "y", "c"))

    (landed,) = _xchg_wait(small_send["handles"], grad_x, "arw_small")
    red = _sum_devices(landed, "sum_small").reshape(-1)
    shapes = small_send["shapes"]
    offs = np.cumsum([0] + [int(np.prod(sh)) for sh in shapes])
    r = [red[offs[i]:offs[i + 1]].reshape(shapes[i]) for i in range(len(shapes))]
    gs = {}
    gs["lb_logits"] = r[0]
    gs["hg_norm_g"] = r[1]
    gs["sg_ln_g"] = lax.dynamic_slice(r[2], (0, me * 256), (1, 256))
    gs["sg_ln_b"] = lax.dynamic_slice(r[3], (0, me * 256), (1, 256))
    gs["sg_w_s"] = r[4][None]
    gs["sg_b_s"] = r[5][None]
    gs["ffn_conv_w"] = lax.dynamic_slice(jnp.stack([r[6], r[7]]), (0, 0, me * 704), (DEPTH, 3, 704))
    gs["ffn_conv_b"] = jnp.concatenate([r[8], r[9]], axis=0)
    gs["ln1_g"] = jnp.concatenate([r[10], r[11]], axis=0)
    gs["ln1_b"] = jnp.concatenate([r[12], r[13]], axis=0)
    gs["ln2_g"] = jnp.concatenate([r[14], r[15]], axis=0)
    gs["ln2_b"] = jnp.concatenate([r[16], r[17]], axis=0)

    out_g, out_d, out_m, out_v = {}, {}, {}, {}
    for n in _SMALL:
        out_g[n], out_d[n], out_m[n], out_v[n] = _adamw(w[n], gs[n], m[n], v[n], f"adamw_{n}")

    res, after = {}, out_v["ln2_b"]
    for n in ("ffn_w_down1", "ffn_w_up1", "sg_w_out", "sg_w_in", "ffn_w_down0", "ffn_w_up0", "hg_w_out", "hg_w_in"):
        parts, recv = _xchg_wait(grad_sends[n], after, f"rsw_{n}")
        own = lax.dynamic_index_in_dim(parts, me, 0, keepdims=False)
        if n[-1] in "01":
            base, layer = n[:-1], int(n[-1])
            res[n] = _adamw(w[base], None, m[base], v[base], f"adamw_{n}", parts=recv, own=own, is_me=is_me,
                            layer=layer, prev=res.get(base + "1"))
        else:
            res[n] = _adamw(w[n][0], None, m[n][0], v[n][0], f"adamw_{n}", parts=recv, own=own, is_me=is_me)
        after = res[n][3]
    for n in ("hg_w_in", "hg_w_out", "sg_w_in", "sg_w_out"):
        out_g[n], out_d[n], out_m[n], out_v[n] = (a[None] for a in res[n])
    for n in ("ffn_w_up", "ffn_w_down"):
        out_g[n], out_d[n], out_m[n], out_v[n] = res[n + "0"]

    return (loss, grad_x[None], *[out_g[n] for n in _NAMES], *[out_d[n] for n in _NAMES],
            *[out_m[n] for n in _NAMES], *[out_v[n] for n in _NAMES])
```

```python
import functools

import numpy as np
import jax
import jax.numpy as jnp
from jax import lax
from jax.experimental import pallas as pl
from jax.experimental.pallas import tpu as pltpu

F32 = jnp.float32
BF16 = jnp.bfloat16
HI = lax.Precision.HIGHEST

N_DEV = 8
D_MODEL = 2048
HG_HEADS = 16
HG_DIM = 128
HG_CHUNK = 128
SG_GROUPS = 16
SG_DIM = 128
SG_CHUNK = 128
D_FF = 5632
DEPTH = 2
ALPHA = (2 * DEPTH) ** 0.25
LN_EPS = 1e-5
RMS_EPS = 1e-6
ADAM_LR = 0.001
ADAM_B1 = 0.9
ADAM_B2 = 0.999
ADAM_EPS = 1e-08
ADAM_WD = 0.01
ADAM_STEP = 10

VMEM_LIMIT = 56 * 1024 * 1024
MESH = pl.DeviceIdType.MESH
ANY = pl.BlockSpec(memory_space=pl.ANY)


def _cp(*sem):
    return pltpu.CompilerParams(dimension_semantics=sem, vmem_limit_bytes=VMEM_LIMIT)


def _pick(n, cands):
    for c in cands:
        if n % c == 0:
            return c
    raise ValueError(f"no tile for {n} in {cands}")


_DIMS = {"nn": (((1,), (0,)), ((), ())), "nt": (((1,), (1,)), ((), ())), "tn": (((0,), (0,)), ((), ()))}


MM_VMEM_BUDGET = 44 * 1024 * 1024


def _mm_tiles(m, n, kk, n_div, k_div, out_bytes, has_addend, k_piece, transposed_a):
    best = None
    for bm in (1024, 1408, 512, 256):
        if m % bm:
            continue
        for bn in (1024, 1408, 512, 256, 128):
            if n % bn or any(d % bn for d in n_div):
                continue
            for bk in (kk, 5632, 4096, 2816, 2048, 1408, 1024, 512, 256, 128):
                if bk > kk or kk % bk or any(d % bk for d in k_div):
                    continue
                if k_piece and bk % k_piece:
                    continue
                nk = kk // bk
                vmem = 4 * (bm * bk + bk * bn) + 2 * bm * bn * out_bytes + (nk > 1) * 4 * bm * bn
                vmem += has_addend * 8 * bm * bn + transposed_a * 2 * bm * bk
                vmem += (out_bytes < 4) * 4 * bm * bn
                if vmem > MM_VMEM_BUDGET:
                    continue
                score = (max(nk, 2), -(bm * bn), -bk)
                if best is None or score < best[0]:
                    best = (score, bm, bn, bk)
    if best is None:
        raise ValueError(f"no tiles for {(m, n, kk)}")
    return best[1:]


def _mm(a, b, *, mode, name, out_dtype=F32, a_cs=False, b_cs=False, o_cs=None, addend=None, deps=()):
    if mode == "tn":
        kk, m = a.shape
    elif a_cs:
        m, kk = a.shape[1], a.shape[0] * a.shape[2]
    else:
        m, kk = a.shape
    if mode == "nt":
        n = b.shape[1] if b_cs else b.shape[0]
    else:
        n = b.shape[0] * b.shape[2] if b_cs else b.shape[1]
    a_c = a.shape[2] if a_cs else None
    b_c = b.shape[2] if b_cs else None
    o_c = n // o_cs if o_cs else None
    k_piece = b_c if (b_cs and mode == "nt") else None
    n_div = [c for c in (b_c if (b_cs and mode != "nt") else None, o_c) if c]
    k_div = [a_c] if a_c else []
    bm, bn, bk = _mm_tiles(m, n, kk, n_div, k_div, jnp.dtype(out_dtype).itemsize, addend is not None, k_piece,
                           mode == "tn")
    nk = kk // bk
    kb = bk // k_piece if k_piece else 1

    def cs_idx(blk, per):
        return (blk * per[0]) // per[1], (blk * per[0] % per[1]) // per[0]

    if mode == "tn":
        a_spec = pl.BlockSpec((bk, bm), lambda i, j, k: (k, i))
    elif a_cs:
        def a_map(i, j, k):
            s, r = cs_idx(k, (bk, a_c))
            return (s, i, r)
        a_spec = pl.BlockSpec((None, bm, bk), a_map)
    else:
        a_spec = pl.BlockSpec((bm, bk), lambda i, j, k: (i, k))
    if mode == "nt":
        if b_cs:
            b_spec = pl.BlockSpec((kb, bn, b_c), lambda i, j, k: (k, j, 0))
        else:
            b_spec = pl.BlockSpec((bn, bk), lambda i, j, k: (j, k))
    else:
        if b_cs:
            def b_map(i, j, k):
                s, r = cs_idx(j, (bn, b_c))
                return (s, k, r)
            b_spec = pl.BlockSpec((None, bk, bn), b_map)
        else:
            b_spec = pl.BlockSpec((bk, bn), lambda i, j, k: (k, j))
    if o_cs:
        def o_map(i, j, k):
            s, r = cs_idx(j, (bn, o_c))
            return (s, i, r)
        o_spec = pl.BlockSpec((None, bm, bn), o_map)
        out_shape = jax.ShapeDtypeStruct((o_cs, m, o_c), out_dtype)
    else:
        o_spec = pl.BlockSpec((bm, bn), lambda i, j, k: (i, j))
        out_shape = jax.ShapeDtypeStruct((m, n), out_dtype)
    in_specs = [a_spec, b_spec]
    args = [a, b]
    if addend is not None:
        in_specs.append(pl.BlockSpec((bm, bn), lambda i, j, k: (i, j)))
        args.append(addend)
    dims = _DIMS[mode]
    in_specs += [ANY] * len(deps)
    args += list(deps)
    n_in = len(args)

    def body(*refs):
        a_ref, b_ref = refs[0], refs[1]
        add_ref = refs[2] if addend is not None else None
        o_ref = refs[n_in]
        if k_piece:
            d = None
            for p in range(kb):
                dp = lax.dot_general(a_ref[:, p * b_c:(p + 1) * b_c].astype(BF16), b_ref[p].astype(BF16), dims,
                                     preferred_element_type=F32)
                d = dp if d is None else d + dp
        else:
            d = lax.dot_general(a_ref[...].astype(BF16), b_ref[...].astype(BF16), dims, preferred_element_type=F32)

        def finish(r):
            if addend is not None:
                r = r + add_ref[...]
            o_ref[...] = r.astype(o_ref.dtype)

        if nk == 1:
            finish(d)
            return
        acc_ref = refs[n_in + 1]
        k = pl.program_id(2)

        @pl.when(k == 0)
        def _():
            acc_ref[...] = d

        if nk > 2:
            @pl.when((k > 0) & (k < nk - 1))
            def _():
                acc_ref[...] += d

        @pl.when(k == nk - 1)
        def _():
            finish(acc_ref[...] + d)

    return pl.pallas_call(
        body, name=name, out_shape=out_shape, grid=(m // bm, n // bn, nk),
        in_specs=in_specs, out_specs=o_spec,
        scratch_shapes=[pltpu.VMEM((bm, bn), F32)] if nk > 1 else [],
        compiler_params=_cp("parallel", "parallel", "arbitrary"),
    )(*args)


def _to_bf16(x, dep, name):
    t, d = x.shape
    row = pl.BlockSpec((LN_ROWS, d), lambda i: (i, 0))

    def body(x_ref, dep_ref, o_ref):
        o_ref[...] = x_ref[...].astype(BF16)

    return pl.pallas_call(body, name=name, grid=(t // LN_ROWS,), out_shape=jax.ShapeDtypeStruct((t, d), BF16),
                          in_specs=[row, ANY], out_specs=row, compiler_params=_cp("parallel"))(x, dep)


LN_ROWS = 256


def _ln_stats(s):
    mu = jnp.mean(s, axis=-1, keepdims=True)
    sc = s - mu
    var = jnp.mean(sc * sc, axis=-1, keepdims=True)
    rstd = lax.rsqrt(var + LN_EPS)
    return sc * rstd, rstd


def _ln_fwd(h, sub, g, b, name):
    t, d = h.shape
    row = pl.BlockSpec((LN_ROWS, d), lambda i: (i, 0))
    vec = pl.BlockSpec((1, d), lambda i: (0, 0))

    def body(h_ref, s_ref, g_ref, b_ref, y_ref, yb_ref):
        xhat, _ = _ln_stats(ALPHA * h_ref[...] + s_ref[...])
        y = xhat * g_ref[...] + b_ref[...]
        y_ref[...] = y
        yb_ref[...] = y.astype(BF16)

    return pl.pallas_call(
        body, name=name, grid=(t // LN_ROWS,),
        out_shape=(jax.ShapeDtypeStruct((t, d), F32), jax.ShapeDtypeStruct((t, d), BF16)),
        in_specs=[row, row, vec, vec], out_specs=(row, row), compiler_params=_cp("parallel"),
    )(h, sub, g, b)


def _ln_bwd_math(xhat, rstd, dy, g):
    dxhat = dy * g
    m1 = jnp.mean(dxhat, axis=-1, keepdims=True)
    m2 = jnp.mean(dxhat * xhat, axis=-1, keepdims=True)
    ds = rstd * (dxhat - m1 - xhat * m2)
    dg = jnp.sum(dy * xhat, axis=0, keepdims=True)
    db = jnp.sum(dy, axis=0, keepdims=True)
    return ds, dg, db


def _ln_bwd(h, sub, dy, g, name):
    t, d = h.shape
    row = pl.BlockSpec((LN_ROWS, d), lambda i: (i, 0))
    vec = pl.BlockSpec((1, d), lambda i: (0, 0))

    def body(h_ref, s_ref, dy_ref, g_ref, ds_ref, dres_ref, dg_ref, dbeta_ref):
        xhat, rstd = _ln_stats(ALPHA * h_ref[...] + s_ref[...])
        ds, dg, db = _ln_bwd_math(xhat, rstd, dy_ref[...], g_ref[...])
        ds_ref[...] = ds.astype(BF16)
        dres_ref[...] = ALPHA * ds

        @pl.when(pl.program_id(0) == 0)
        def _():
            dg_ref[...] = jnp.zeros_like(dg_ref)
            dbeta_ref[...] = jnp.zeros_like(dbeta_ref)

        dg_ref[...] += dg
        dbeta_ref[...] += db

    return pl.pallas_call(
        body, name=name, grid=(t // LN_ROWS,),
        out_shape=(jax.ShapeDtypeStruct((t, d), BF16), jax.ShapeDtypeStruct((t, d), F32),
                   jax.ShapeDtypeStruct((1, d), F32), jax.ShapeDtypeStruct((1, d), F32)),
        in_specs=[row, row, row, vec], out_specs=(row, row, vec, vec),
        compiler_params=_cp("arbitrary"),
    )(h, sub, dy, g)


def _ln_loss_bwd(h, sub, target, g, b, name):
    t, d = h.shape
    row = pl.BlockSpec((LN_ROWS, d), lambda i: (i, 0))
    vec = pl.BlockSpec((1, d), lambda i: (0, 0))
    lvec = pl.BlockSpec((1, 128), lambda i: (0, 0))

    def body(h_ref, s_ref, t_ref, g_ref, b_ref, loss_ref, ds_ref, dres_ref, dg_ref, dbeta_ref):
        xhat, rstd = _ln_stats(ALPHA * h_ref[...] + s_ref[...])
        y = xhat * g_ref[...] + b_ref[...]
        err = y - t_ref[...]
        part = 0.5 * jnp.sum(jnp.mean(err * err, axis=-1, keepdims=True), axis=0, keepdims=True)
        ds, dg, db = _ln_bwd_math(xhat, rstd, err * (1.0 / d), g_ref[...])
        ds_ref[...] = ds.astype(BF16)
        dres_ref[...] = ALPHA * ds

        @pl.when(pl.program_id(0) == 0)
        def _():
            loss_ref[...] = jnp.zeros_like(loss_ref)
            dg_ref[...] = jnp.zeros_like(dg_ref)
            dbeta_ref[...] = jnp.zeros_like(dbeta_ref)

        loss_ref[...] += jnp.broadcast_to(part, loss_ref.shape)
        dg_ref[...] += dg
        dbeta_ref[...] += db

    return pl.pallas_call(
        body, name=name, grid=(t // LN_ROWS,),
        out_shape=(jax.ShapeDtypeStruct((1, 128), F32), jax.ShapeDtypeStruct((t, d), BF16),
                   jax.ShapeDtypeStruct((t, d), F32), jax.ShapeDtypeStruct((1, d), F32),
                   jax.ShapeDtypeStruct((1, d), F32)),
        in_specs=[row, row, row, vec, vec], out_specs=(lvec, row, row, vec, vec),
        compiler_params=_cp("arbitrary"),
    )(h, sub, target, g, b)


HG_ROWS = 2048
HG_LEVELS = 7


def _hg_constants():
    c = HG_CHUNK
    t = np.arange(c)[:, None]
    j = np.arange(c)[None, :]
    blocks, masks = [], [np.eye(c)]
    for lev in range(HG_LEVELS):
        m = 1 << lev
        second = (t % (2 * m)) >= m
        mid = (t // (2 * m)) * (2 * m) + m - 1
        blocks.append((second & (j > mid) & (j <= t)) | ((~second) & (j > t) & (j <= mid)))
        same = (t // (2 * m)) == (j // (2 * m))
        masks.append(same & second & ((j % (2 * m)) < m))
    blocks += [j <= t, j > t, np.ones((c, c), bool)]
    cm = np.concatenate(blocks, axis=0).astype(np.float32)
    mk = np.concatenate(masks, axis=0).astype(np.float32)
    return jnp.asarray(cm, dtype=BF16), jnp.asarray(mk)


def _dot_01(cm, x, mode):
    hi = x.astype(BF16)
    lo = (x - hi.astype(F32)).astype(BF16)
    both = lax.dot_general(cm, jnp.concatenate([hi, lo], axis=1), _DIMS[mode], preferred_element_type=F32)
    n = x.shape[-1]
    return both[:, :n] + both[:, n:]


@jax.custom_vjp
def _prefix(cm, lf):
    return _dot_01(cm, lf, "nn")


def _prefix_fwd(cm, lf):
    return _prefix(cm, lf), cm


def _prefix_bwd(cm, d):
    return jnp.zeros_like(cm), _dot_01(cm, d, "tn")


_prefix.defvjp(_prefix_fwd, _prefix_bwd)

_BWD = {"nn": (("nt", 0, 1), ("tn", 1, 0)), "nt": (("nn", 0, 1), ("tn", 0, 1)), "tn": (("nt", 1, 0), ("nn", 1, 0))}


def _bdot_raw(a, b, mode):
    return lax.dot_general(a.astype(BF16), b.astype(BF16), _DIMS[mode], preferred_element_type=F32)


@functools.partial(jax.custom_vjp, nondiff_argnums=(2,))
def _bdot(a, b, mode):
    return _bdot_raw(a, b, mode)


def _bdot_fwd(a, b, mode):
    return _bdot_raw(a, b, mode), (a, b)


def _bdot_bwd(mode, res, d):
    a, b = res
    (ma, da_pos, _), (mb, db_pos, _) = _BWD[mode]
    da = _bdot_raw(d, b, ma) if da_pos == 0 else _bdot_raw(b, d, ma)
    db = _bdot_raw(d, a, mb) if db_pos == 0 else _bdot_raw(a, d, mb)
    return da, db


_bdot.defvjp(_bdot_fwd, _bdot_bwd)


def _hg_chunk(pq, pf, pi, pg, l0, l1, l2, ng, s_t, cm, mk):
    c = HG_CHUNK
    mx = jnp.maximum(jnp.maximum(l0, l1), l2)
    e0, e1, e2 = jnp.exp(l0 - mx), jnp.exp(l1 - mx), jnp.exp(l2 - mx)
    lb = e0 / (e0 + e1 + e2)
    q = pq * jax.nn.sigmoid(pq)
    t1 = jnp.log(lb)
    t2 = jnp.log1p(-lb) + jax.nn.log_sigmoid(pf)
    lf = jnp.maximum(t1, t2) + jnp.log1p(jnp.exp(-jnp.abs(t1 - t2)))
    k = (1.0 - lb) * jax.nn.sigmoid(-pf)
    x = jnp.exp(_prefix(cm, lf))
    scores = mk[0:c] * _bdot(q, k, "nt")
    for lev in range(HG_LEVELS):
        xl = x[lev * c:(lev + 1) * c]
        scores = scores + mk[(lev + 1) * c:(lev + 2) * c] * _bdot(q * xl, k * xl, "nt")
    x_incl = x[HG_LEVELS * c:(HG_LEVELS + 1) * c]
    x_after = x[(HG_LEVELS + 1) * c:(HG_LEVELS + 2) * c]
    x_total = x[(HG_LEVELS + 2) * c:(HG_LEVELS + 3) * c]
    o = _bdot(scores, pi, "nn") + _bdot(q * x_incl, s_t, "nt")
    s_new = s_t * jnp.concatenate([x_total] * (HG_DIM // c), axis=0) + _bdot(pi, k * x_after, "tn")
    rstd = lax.rsqrt(jnp.mean(o * o, axis=-1, keepdims=True) + RMS_EPS)
    y = o * rstd * ng * (pg * jax.nn.sigmoid(pg))
    return y, s_new


def _hg_specs(t, reverse):
    nrb = t // HG_ROWS
    rb = (lambda r: nrb - 1 - r) if reverse else (lambda r: r)
    proj = pl.BlockSpec((4, HG_ROWS, HG_DIM), lambda h, r: (0, rb(r), h))
    vec = pl.BlockSpec((1, HG_DIM), lambda h, r: (0, h))
    cm = pl.BlockSpec(((HG_LEVELS + 3) * HG_CHUNK, HG_CHUNK), lambda h, r: (0, 0))
    mk = pl.BlockSpec(((HG_LEVELS + 1) * HG_CHUNK, HG_CHUNK), lambda h, r: (0, 0))
    rows = pl.BlockSpec((HG_ROWS, HG_DIM), lambda h, r: (rb(r), h))
    states = pl.BlockSpec((None, HG_ROWS // HG_CHUNK, HG_DIM, HG_DIM), lambda h, r: (h, rb(r), 0, 0))
    return proj, vec, cm, mk, rows, states


def _hg_fwd(proj, l0, l1, l2, ng):
    t = proj.shape[1]
    n_in = HG_ROWS // HG_CHUNK
    cm, mk = _hg_constants()
    p_spec, vec, cm_spec, mk_spec, rows, st_spec = _hg_specs(t, False)

    def body(p_ref, l0_ref, l1_ref, l2_ref, ng_ref, cm_ref, mk_ref, y_ref, st_ref, s_ref):
        @pl.when(pl.program_id(1) == 0)
        def _():
            s_ref[...] = jnp.zeros_like(s_ref)

        def step(ci, carry):
            sl = pl.ds(pl.multiple_of(ci * HG_CHUNK, HG_CHUNK), HG_CHUNK)
            s_t = s_ref[...]
            st_ref[ci] = s_t
            y, s_new = _hg_chunk(p_ref[0, sl, :], p_ref[1, sl, :], p_ref[2, sl, :], p_ref[3, sl, :],
                                 l0_ref[...], l1_ref[...], l2_ref[...], ng_ref[...], s_t,
                                 cm_ref[...], mk_ref[...])
            y_ref[sl, :] = y.astype(BF16)
            s_ref[...] = s_new
            return carry

        lax.fori_loop(0, n_in, step, 0, unroll=True)

    return pl.pallas_call(
        body, name="hg_fwd", grid=(HG_HEADS, t // HG_ROWS),
        out_shape=(jax.ShapeDtypeStruct((t, D_MODEL), BF16),
                   jax.ShapeDtypeStruct((HG_HEADS, t // HG_CHUNK, HG_DIM, HG_DIM), F32)),
        in_specs=[p_spec, vec, vec, vec, vec, cm_spec, mk_spec], out_specs=(rows, st_spec),
        scratch_shapes=[pltpu.VMEM((HG_DIM, HG_DIM), F32)],
        compiler_params=_cp("parallel", "arbitrary"),
    )(proj, l0, l1, l2, ng, cm, mk)


def _hg_bwd(proj, states, dy, l0, l1, l2, ng):
    t = proj.shape[1]
    n_in = HG_ROWS // HG_CHUNK
    cm, mk = _hg_constants()
    p_spec, vec, cm_spec, mk_spec, rows, st_spec = _hg_specs(t, True)

    def body(p_ref, st_ref, dy_ref, l0_ref, l1_ref, l2_ref, ng_ref, cm_ref, mk_ref,
             dp_ref, dl0_ref, dl1_ref, dl2_ref, dng_ref, ds_ref):
        @pl.when(pl.program_id(1) == 0)
        def _():
            ds_ref[...] = jnp.zeros_like(ds_ref)
            for r in (dl0_ref, dl1_ref, dl2_ref, dng_ref):
                r[...] = jnp.zeros_like(r)

        def step(it, carry):
            ci = n_in - 1 - it
            sl = pl.ds(pl.multiple_of(ci * HG_CHUNK, HG_CHUNK), HG_CHUNK)
            fn = functools.partial(_hg_chunk, cm=cm_ref[...], mk=mk_ref[...])
            _, vjp = jax.vjp(fn, p_ref[0, sl, :], p_ref[1, sl, :], p_ref[2, sl, :], p_ref[3, sl, :],
                             l0_ref[...], l1_ref[...], l2_ref[...], ng_ref[...], st_ref[ci])
            dq, df, di, dg, d0, d1, d2, dn, ds = vjp((dy_ref[sl, :], ds_ref[...]))
            dp_ref[0, sl, :] = dq.astype(BF16)
            dp_ref[1, sl, :] = df.astype(BF16)
            dp_ref[2, sl, :] = di.astype(BF16)
            dp_ref[3, sl, :] = dg.astype(BF16)
            dl0_ref[...] += d0
            dl1_ref[...] += d1
            dl2_ref[...] += d2
            dng_ref[...] += dn
            ds_ref[...] = ds
            return carry

        lax.fori_loop(0, n_in, step, 0, unroll=True)

    v_shape = jax.ShapeDtypeStruct((1, D_MODEL), F32)
    return pl.pallas_call(
        body, name="hg_bwd", grid=(HG_HEADS, t // HG_ROWS),
        out_shape=(jax.ShapeDtypeStruct((4, t, D_MODEL), BF16), v_shape, v_shape, v_shape, v_shape),
        in_specs=[p_spec, st_spec, rows, vec, vec, vec, vec, cm_spec, mk_spec],
        out_specs=(p_spec, vec, vec, vec, vec),
        scratch_shapes=[pltpu.VMEM((HG_DIM, HG_DIM), F32)],
        compiler_params=_cp("parallel", "arbitrary"),
    )(proj, states, dy, l0, l1, l2, ng, cm, mk)


_SQRT_HALF = 0.7071067811865476
_INV_SQRT_2PI = 0.3989422804014327


def _gelu(x):
    return 0.5 * x * (1.0 + lax.erf(x * _SQRT_HALF))


def _gelu_and_grad(x):
    cdf = 0.5 * (1.0 + lax.erf(x * _SQRT_HALF))
    return x * cdf, cdf + x * (_INV_SQRT_2PI * jnp.exp(-0.5 * x * x))


def _tril(n):
    return (lax.broadcasted_iota(jnp.int32, (n, n), 0) >= lax.broadcasted_iota(jnp.int32, (n, n), 1)).astype(F32)


def _sg_specs(d):
    row = lambda w: pl.BlockSpec((SG_CHUNK, w), lambda i: (i, 0))
    vec = pl.BlockSpec((1, d), lambda i: (0, 0))
    cube = pl.BlockSpec((SG_GROUPS, SG_CHUNK, SG_CHUNK), lambda i: (0, 0, 0))
    return row, vec, cube


def _sg_fwd(pre, ln_g, ln_b, w_s, bias):
    t = pre.shape[0]
    d = D_MODEL
    row, vec, cube = _sg_specs(d)

    def body(pre_ref, g_ref, b_ref, w_ref, bias_ref, y_ref, vln_ref):
        u = _gelu(pre_ref[:, :d])
        vhat, _ = _ln_stats(_gelu(pre_ref[:, d:]))
        vln_ref[...] = vhat * g_ref[...] + b_ref[...]
        tril = _tril(SG_CHUNK)
        for g in range(SG_GROUPS):
            cs = slice(g * SG_DIM, (g + 1) * SG_DIM)
            gate = _bdot_raw(w_ref[g] * tril, vln_ref[:, cs], "nn") + bias_ref[g]
            y_ref[:, cs] = (u[:, cs] * gate).astype(BF16)

    return pl.pallas_call(
        body, name="sg_fwd", grid=(t // SG_CHUNK,), out_shape=jax.ShapeDtypeStruct((t, d), BF16),
        in_specs=[row(2 * d), vec, vec, cube, cube], out_specs=row(d),
        scratch_shapes=[pltpu.VMEM((SG_CHUNK, d), F32)], compiler_params=_cp("parallel"),
    )(pre, ln_g, ln_b, w_s, bias)


def _sg_bwd(pre, dy, ln_g, ln_b, w_s, bias):
    t = pre.shape[0]
    d = D_MODEL
    row, vec, cube = _sg_specs(d)
    dbs_spec = pl.BlockSpec((SG_GROUPS, 8, SG_CHUNK), lambda i: (0, 0, 0))

    def body(pre_ref, dy_ref, g_ref, b_ref, w_ref, bias_ref, dpre_ref, dw_ref, dbs_ref, dlg_ref, dlb_ref,
             vln_ref, dvln_ref):
        @pl.when(pl.program_id(0) == 0)
        def _():
            for r in (dw_ref, dbs_ref, dlg_ref, dlb_ref):
                r[...] = jnp.zeros_like(r)

        pu = pre_ref[:, :d]
        pv = pre_ref[:, d:]
        u, u_grad = _gelu_and_grad(pu)
        v, v_grad = _gelu_and_grad(pv)
        vhat, rstd = _ln_stats(v)
        vln_ref[...] = vhat * g_ref[...] + b_ref[...]
        tril = _tril(SG_CHUNK)
        ones = jnp.ones((8, SG_DIM), F32)
        for g in range(SG_GROUPS):
            cs = slice(g * SG_DIM, (g + 1) * SG_DIM)
            wc = w_ref[g] * tril
            vg = vln_ref[:, cs]
            gate = _bdot_raw(wc, vg, "nn") + bias_ref[g]
            dyg = dy_ref[:, cs]
            dgate = dyg * u[:, cs]
            dpre_ref[:, cs] = (dyg * gate * u_grad[:, cs]).astype(BF16)
            dw_ref[g] += tril * _bdot_raw(dgate, vg, "nt")
            dbs_ref[g] += lax.dot_general(ones, dgate, _DIMS["nt"], precision=HI, preferred_element_type=F32)
            dvln_ref[:, cs] = _bdot_raw(wc, dgate, "tn")
        dvln = dvln_ref[...]
        dv, dg, db = _ln_bwd_math(vhat, rstd, dvln, g_ref[...])
        dlg_ref[...] += dg
        dlb_ref[...] += db
        dpre_ref[:, d:] = (dv * v_grad).astype(BF16)

    return pl.pallas_call(
        body, name="sg_bwd", grid=(t // SG_CHUNK,),
        out_shape=(jax.ShapeDtypeStruct((t, 2 * d), BF16), jax.ShapeDtypeStruct(w_s.shape, F32),
                   jax.ShapeDtypeStruct((SG_GROUPS, 8, SG_CHUNK), F32),
                   jax.ShapeDtypeStruct((1, d), F32), jax.ShapeDtypeStruct((1, d), F32)),
        in_specs=[row(2 * d), row(d), vec, vec, cube, cube],
        out_specs=(row(2 * d), cube, dbs_spec, vec, vec),
        scratch_shapes=[pltpu.VMEM((SG_CHUNK, d), F32), pltpu.VMEM((SG_CHUNK, d), F32)],
        compiler_params=_cp("arbitrary"),
    )(pre, dy, ln_g, ln_b, w_s, bias)


FFN_ROWS = 1024
FFN_COLS = 1408
HALO = 16


def _ffn_conv(a_ext, w_ref, cb_ref):
    a1 = pltpu.roll(a_ext, 1, 0)
    a2 = pltpu.roll(a_ext, 2, 0)
    return w_ref[0:1, :] * a2 + w_ref[1:2, :] * a1 + w_ref[2:3, :] * a_ext + cb_ref[...], a1, a2


def _ffn_fwd(hh, conv_w, conv_b):
    t = hh.shape[1]
    nb8 = FFN_ROWS // HALO
    main = pl.BlockSpec((2, FFN_ROWS, FFN_COLS), lambda c, r: (0, r, c))
    prev = pl.BlockSpec((None, HALO, FFN_COLS), lambda c, r: (0, jnp.maximum(r * nb8 - 1, 0), c))
    wspec = pl.BlockSpec((3, FFN_COLS), lambda c, r: (0, c))
    bspec = pl.BlockSpec((1, FFN_COLS), lambda c, r: (0, c))

    def body(m_ref, p_ref, w_ref, cb_ref, z_ref):
        prev = jnp.where(pl.program_id(1) == 0, 0.0, p_ref[...].astype(F32))
        a_ext = jnp.concatenate([prev, m_ref[0].astype(F32)], axis=0)
        a, _, _ = _ffn_conv(a_ext, w_ref, cb_ref)
        a = a[HALO:]
        z_ref[...] = (a * jax.nn.sigmoid(a) * m_ref[1].astype(F32)).astype(BF16)

    return pl.pallas_call(
        body, name="ffn_fwd", grid=(D_FF // FFN_COLS, t // FFN_ROWS),
        out_shape=jax.ShapeDtypeStruct((t, D_FF), BF16),
        in_specs=[main, prev, wspec, bspec],
        out_specs=pl.BlockSpec((FFN_ROWS, FFN_COLS), lambda c, r: (r, c)),
        compiler_params=_cp("parallel", "parallel"),
    )(hh, hh, conv_w, conv_b)


def _ffn_bwd(hh, dz, conv_w, conv_b):
    t = hh.shape[1]
    nb8 = FFN_ROWS // HALO
    last8 = t // HALO - 1
    nr = t // FFN_ROWS
    main = pl.BlockSpec((2, FFN_ROWS, FFN_COLS), lambda c, r: (0, r, c))
    prev = pl.BlockSpec((None, HALO, FFN_COLS), lambda c, r: (0, jnp.maximum(r * nb8 - 1, 0), c))
    nxt = pl.BlockSpec((2, HALO, FFN_COLS), lambda c, r: (0, jnp.minimum((r + 1) * nb8, last8), c))
    dmain = pl.BlockSpec((FFN_ROWS, FFN_COLS), lambda c, r: (r, c))
    dnxt = pl.BlockSpec((HALO, FFN_COLS), lambda c, r: (jnp.minimum((r + 1) * nb8, last8), c))
    wspec = pl.BlockSpec((3, FFN_COLS), lambda c, r: (0, c))
    bspec = pl.BlockSpec((1, FFN_COLS), lambda c, r: (0, c))

    def body(m_ref, p_ref, n_ref, dz_ref, dzn_ref, w_ref, cb_ref, dh_ref, dw_ref, dcb_ref):
        r = pl.program_id(1)

        @pl.when(r == 0)
        def _():
            dw_ref[...] = jnp.zeros_like(dw_ref)
            dcb_ref[...] = jnp.zeros_like(dcb_ref)

        prev = jnp.where(r == 0, 0.0, p_ref[...].astype(F32))
        a_ext = jnp.concatenate([prev, m_ref[0].astype(F32), n_ref[0].astype(F32)], axis=0)
        a, a1, a2 = _ffn_conv(a_ext, w_ref, cb_ref)
        a = a[HALO:]
        b_ext = jnp.concatenate([m_ref[1], n_ref[1]], axis=0).astype(F32)
        dz_main = dz_ref[...].astype(F32)
        dz_ext = jnp.concatenate([dz_main, jnp.where(r == nr - 1, 0.0, dzn_ref[...].astype(F32))], axis=0)
        sig = jax.nn.sigmoid(a)
        da = dz_ext * b_ext * (sig * (1.0 + a * (1.0 - sig)))
        n_ext = FFN_ROWS + HALO
        da_p1 = pltpu.roll(da, n_ext - 1, 0)
        da_p2 = pltpu.roll(da, n_ext - 2, 0)
        da_raw = w_ref[2:3, :] * da + w_ref[1:2, :] * da_p1 + w_ref[0:1, :] * da_p2
        dh_ref[0] = da_raw[:FFN_ROWS].astype(BF16)
        dh_ref[1] = (dz_main * (a * sig)[:FFN_ROWS]).astype(BF16)
        dam = da[:FFN_ROWS]
        rows = slice(HALO, HALO + FFN_ROWS)
        dw_ref[0:1, :] += jnp.sum(dam * a2[rows], axis=0, keepdims=True)
        dw_ref[1:2, :] += jnp.sum(dam * a1[rows], axis=0, keepdims=True)
        dw_ref[2:3, :] += jnp.sum(dam * a_ext[rows], axis=0, keepdims=True)
        dcb_ref[...] += jnp.sum(dam, axis=0, keepdims=True)

    return pl.pallas_call(
        body, name="ffn_bwd", grid=(D_FF // FFN_COLS, nr),
        out_shape=(jax.ShapeDtypeStruct((2, t, D_FF), BF16), jax.ShapeDtypeStruct((3, D_FF), F32),
                   jax.ShapeDtypeStruct((1, D_FF), F32)),
        in_specs=[main, prev, nxt, dmain, dnxt, wspec, bspec],
        out_specs=(main, wspec, bspec), compiler_params=_cp("parallel", "arbitrary"),
    )(hh, hh, hh, dz, dz, conv_w, conv_b)


def _adamw_math(w, g, m, v):
    m = ADAM_B1 * m + (1.0 - ADAM_B1) * g
    v = ADAM_B2 * v + (1.0 - ADAM_B2) * (g * g)
    m_hat = m / (1.0 - ADAM_B1 ** ADAM_STEP)
    v_hat = v / (1.0 - ADAM_B2 ** ADAM_STEP)
    delta = -ADAM_LR * (m_hat / (jnp.sqrt(v_hat) + ADAM_EPS) + ADAM_WD * w)
    return delta, m, v


def _as2d(shape):
    n = int(np.prod(shape))
    c = shape[-1] if shape[-1] % 128 == 0 else (128 if n % 128 == 0 else shape[-1])
    return n // c, c


def _adamw(w, g, m, v, name, parts=None, own=None, is_me=None, layer=None, prev=None):
    shape = w.shape if layer is None else w.shape[1:]
    r, c = _as2d(shape)
    br = r if r <= 512 else _pick(r, (256, 176, 128, 64, 8))
    blk = pl.BlockSpec((br, c), lambda i: (i, 0))
    if layer is None:
        wblk, o = blk, jax.ShapeDtypeStruct((r, c), F32)
        w2, m2, v2 = (a.reshape(r, c) for a in (w, m, v))
    else:
        wblk, o = pl.BlockSpec((None, br, c), lambda i: (layer, i, 0)), jax.ShapeDtypeStruct((w.shape[0], r, c), F32)
        w2, m2, v2 = (a.reshape(w.shape[0], r, c) for a in (w, m, v))
    extra, extra_specs = [], []
    if parts is not None:
        g_in = parts.reshape(N_DEV, r, c)
        g_spec = pl.BlockSpec((N_DEV, br, c), lambda i: (0, i, 0))
        extra = [own.reshape(r, c), is_me]
        extra_specs = [blk, pl.BlockSpec(memory_space=pltpu.SMEM)]
    else:
        g_in = g.reshape(r, c)
        g_spec = blk
    aliases = {}
    if prev is not None:
        aliases = {4 + len(extra) + j: j for j in range(4)}
        extra = extra + [a.reshape(o.shape) for a in prev]
        extra_specs = extra_specs + [ANY] * 4

    def body(w_ref, g_ref, m_ref, v_ref, *rest):
        go_ref, d_ref, mo_ref, vo_ref = rest[-4:]
        if parts is not None:
            own_ref, me_ref = rest[:2]
            gsum = None
            for i in range(N_DEV):
                term = jnp.where(me_ref[i] > 0.5, own_ref[...], g_ref[i]).astype(F32)
                gsum = term if gsum is None else gsum + term
        else:
            gsum = g_ref[...]
        delta, m_new, v_new = _adamw_math(w_ref[...], gsum, m_ref[...], v_ref[...])
        go_ref[...] = gsum
        d_ref[...] = delta
        mo_ref[...] = m_new
        vo_ref[...] = v_new

    outs = pl.pallas_call(
        body, name=name, grid=(r // br,), out_shape=(o, o, o, o),
        in_specs=[wblk, g_spec, wblk, wblk] + extra_specs, out_specs=(wblk, wblk, wblk, wblk),
        input_output_aliases=aliases, compiler_params=_cp("parallel"),
    )(w2, g_in, m2, v2, *extra)
    return tuple(a.reshape(w.shape) for a in outs)


ALL_PEERS = tuple(range(1, N_DEV))
OTHER_CHIPS = (2, 4, 6)
SIBLING = (1,)


def _me_and_peers(relations=ALL_PEERS):
    x, y, c = lax.axis_index("x"), lax.axis_index("y"), lax.axis_index("c")
    me = 4 * x + 2 * y + c
    peers = []
    for k in relations:
        kx, ky, kc = (k >> 2) & 1, (k >> 1) & 1, k & 1
        px, py, pc = x ^ kx, y ^ ky, c ^ kc
        peers.append(((px, py, pc), 4 * px + 2 * py + pc))
    return me, peers


HBM = pl.BlockSpec(memory_space=pltpu.HBM)
SEM = pl.BlockSpec(memory_space=pltpu.SEMAPHORE)
EFFECT = pltpu.SideEffectType.DATAFLOW_SIDE_EFFECTING


def _xchg_start(parts, land, *, dep, name, peers=ALL_PEERS, pieces=(0,)):
    n_buf = 1 if parts is None else 2

    def body(*refs):
        land_ref = refs[n_buf - 1]
        send_sem, recv_sem = refs[n_buf + 1], refs[n_buf + 2]
        token = refs[2 * n_buf + 3]
        me, to = _me_and_peers(peers)
        for dev, idx in to:
            if parts is None:
                for r in pieces:
                    piece = land_ref.at[me ^ r]
                    pltpu.make_async_remote_copy(src_ref=piece, dst_ref=piece, send_sem=send_sem, recv_sem=recv_sem,
                                                 device_id=dev, device_id_type=MESH).start()
            else:
                pltpu.make_async_remote_copy(src_ref=refs[0].at[idx], dst_ref=land_ref.at[me], send_sem=send_sem,
                                             recv_sem=recv_sem, device_id=dev, device_id_type=MESH).start()
        token[...] = jnp.zeros_like(token)

    bufs = [land] if parts is None else [parts, land]
    return pl.pallas_call(
        body, name=name,
        out_shape=(pltpu.SemaphoreType.DMA(()), pltpu.SemaphoreType.DMA(()),
                   *[pltpu.HBM(b.shape, b.dtype) for b in bufs], jax.ShapeDtypeStruct((8, 128), F32)),
        in_specs=(*[HBM] * n_buf, ANY), out_specs=(SEM, SEM, *[HBM] * n_buf, pl.BlockSpec(memory_space=pltpu.VMEM)),
        input_output_aliases={i: 2 + i for i in range(n_buf)},
        compiler_params=pltpu.CompilerParams(has_side_effects=EFFECT),
    )(*[pltpu.with_memory_space_constraint(b, pltpu.HBM) for b in bufs], dep)


def _xchg_wait(handles, after, name, n_pieces=N_DEV - 1):
    send_sem, recv_sem, *bufs, _ = handles
    n_buf = len(bufs)

    def body(*refs):
        land_ref = refs[n_buf - 1]
        send_sem, recv_sem = refs[n_buf], refs[n_buf + 1]
        x, y, c = lax.axis_index("x"), lax.axis_index("y"), lax.axis_index("c")
        span = land_ref.at[pl.ds(0, n_pieces)]
        cp = pltpu.make_async_remote_copy(src_ref=span, dst_ref=span, send_sem=send_sem, recv_sem=recv_sem,
                                          device_id=(x, y, c), device_id_type=MESH)
        cp.wait_send()
        cp.wait_recv()

    return pl.pallas_call(
        body, name=name, out_shape=tuple(pltpu.HBM(b.shape, b.dtype) for b in bufs),
        in_specs=(*[HBM] * n_buf, SEM, SEM, ANY), out_specs=tuple([HBM] * n_buf),
        input_output_aliases={i: i for i in range(n_buf)},
        compiler_params=pltpu.CompilerParams(has_side_effects=EFFECT),
    )(*bufs, send_sem, recv_sem, after)


def _sum_devices(v, name):
    _, r, c = v.shape

    def body(x_ref, o_ref):
        acc = x_ref[0]
        for i in range(1, N_DEV):
            acc = acc + x_ref[i]
        o_ref[...] = acc

    return pl.pallas_call(
        body, name=name, out_shape=jax.ShapeDtypeStruct((r, c), F32),
        in_specs=[pl.BlockSpec(memory_space=pltpu.VMEM)], out_specs=pl.BlockSpec(memory_space=pltpu.VMEM),
        compiler_params=pltpu.CompilerParams(vmem_limit_bytes=VMEM_LIMIT),
    )(v)


def _local_step(x, x_bf, target, lb_logits, p, weight, send_grad, send_small):
    g = {}
    l0, l1, l2 = lb_logits[0:1], lb_logits[1:2], lb_logits[2:3]

    proj0 = _mm(x_bf, weight("hg_w_in", x_bf), mode="nn", b_cs=True, o_cs=4, name="mm_hg_in")
    y0, states = _hg_fwd(proj0, l0, l1, l2, p["hg_norm_g"])
    mixed0 = _mm(y0, weight("hg_w_out", y0), mode="nn", name="mm_hg_out")
    hh_l, z_l = [], []

    def ffn_forward(layer, h_in_bf):
        hh = _mm(h_in_bf, weight(f"ffn_w_up{layer}", h_in_bf), mode="nn", b_cs=True, o_cs=2, out_dtype=BF16,
                 name=f"mm_up{layer}")
        z = _ffn_fwd(hh, p["ffn_conv_w"][layer], p["ffn_conv_b"][layer])
        ffn = _mm(z, weight(f"ffn_w_down{layer}", z), mode="nn", name=f"mm_down{layer}")
        hh_l.append(hh)
        z_l.append(z)
        return ffn

    h1, h1b = _ln_fwd(x, mixed0, p["ln1_g"][0], p["ln1_b"][0], "ln1_0")
    ffn0 = ffn_forward(0, h1b)
    h2, h2b = _ln_fwd(h1, ffn0, p["ln2_g"][0], p["ln2_b"][0], "ln2_0")
    pre1 = _mm(h2b, weight("sg_w_in", h2b), mode="nn", b_cs=True, name="mm_sg_in")
    y1 = _sg_fwd(pre1, p["sg_ln_g"], p["sg_ln_b"], p["sg_w_s"], p["sg_bias"])
    mixed1 = _mm(y1, weight("sg_w_out", y1), mode="nn", name="mm_sg_out")
    h3, h3b = _ln_fwd(h2, mixed1, p["ln1_g"][1], p["ln1_b"][1], "ln1_1")
    ffn1 = ffn_forward(1, h3b)

    loss, ds, dres, g["ln2_g1"], g["ln2_b1"] = _ln_loss_bwd(h3, ffn1, target, p["ln2_g"][1], p["ln2_b"][1], "ln2_1_loss")

    def ffn_backward(layer, ds_bf, h_in_bf, dres):
        tok = send_grad(f"ffn_w_down{layer}",
                        _mm(z_l[layer], ds_bf, mode="tn", out_dtype=BF16, name=f"mm_dw_down{layer}"))
        dz = _mm(ds_bf, weight(f"ffn_w_down{layer}", None), mode="nt", out_dtype=BF16, name=f"mm_d_z{layer}",
                 deps=(tok,))
        dhh, g[f"conv_w{layer}"], g[f"conv_b{layer}"] = _ffn_bwd(hh_l[layer], dz, p["ffn_conv_w"][layer],
                                                                   p["ffn_conv_b"][layer])
        tok = send_grad(f"ffn_w_up{layer}", _mm(h_in_bf, dhh, mode="tn", b_cs=True, o_cs=N_DEV, out_dtype=BF16,
                                                name=f"mm_dw_up{layer}"))
        return _mm(dhh, weight(f"ffn_w_up{layer}", None), mode="nt", a_cs=True, b_cs=True, addend=dres,
                   name=f"mm_d_up{layer}", deps=(tok,))

    dh3 = ffn_backward(1, ds, h3b, dres)
    ds, dres, g["ln1_g1"], g["ln1_b1"] = _ln_bwd(h2, mixed1, dh3, p["ln1_g"][1], "ln1_1_bwd")
    tok = send_grad("sg_w_out", _mm(y1, ds, mode="tn", out_dtype=BF16, name="mm_dw_sg_out"))
    dy1 = _mm(ds, weight("sg_w_out", None), mode="nt", name="mm_d_sg_out", deps=(tok,))
    dpre1, g["sg_w_s"], dbs, g["sg_ln_g"], g["sg_ln_b"] = _sg_bwd(pre1, dy1, p["sg_ln_g"], p["sg_ln_b"],
                                                                    p["sg_w_s"], p["sg_bias"])
    g["sg_b_s"] = dbs[:, 0, :]
    tok = send_grad("sg_w_in", _mm(h2b, dpre1, mode="tn", o_cs=N_DEV, out_dtype=BF16, name="mm_dw_sg_in"))
    dh2 = _mm(dpre1, weight("sg_w_in", None), mode="nt", b_cs=True, addend=dres, name="mm_d_sg_in", deps=(tok,))
    ds, dres, g["ln2_g0"], g["ln2_b0"] = _ln_bwd(h1, ffn0, dh2, p["ln2_g"][0], "ln2_0_bwd")
    dh1 = ffn_backward(0, ds, h1b, dres)
    ds, dres, g["ln1_g0"], g["ln1_b0"] = _ln_bwd(x, mixed0, dh1, p["ln1_g"][0], "ln1_0_bwd")
    tok = send_grad("hg_w_out", _mm(y0, ds, mode="tn", out_dtype=BF16, name="mm_dw_hg_out"))
    dy0 = _mm(ds, weight("hg_w_out", None), mode="nt", name="mm_d_hg_out", deps=(tok,))
    dproj, d0, d1, d2, g["hg_norm_g"] = _hg_bwd(proj0, states, dy0, l0, l1, l2, p["hg_norm_g"])
    g["lb_logits"] = jnp.concatenate([d0, d1, d2], axis=0)
    tok = send_small(g)
    tok = send_grad("hg_w_in", _mm(x_bf, dproj, mode="tn", b_cs=True, o_cs=N_DEV, out_dtype=BF16, name="mm_dw_hg_in",
                                   deps=(tok,)))
    grad_x = _mm(dproj, weight("hg_w_in", None), mode="nt", a_cs=True, b_cs=True, addend=dres, name="mm_d_hg_in",
                 deps=(tok,))
    return loss[0, 0], grad_x


_SMALL = ("lb_logits", "hg_norm_g", "sg_ln_g", "sg_ln_b", "sg_w_s", "sg_b_s", "ffn_conv_w", "ffn_conv_b",
          "ln1_g", "ln1_b", "ln2_g", "ln2_b")
_NAMES = ("lb_logits", "hg_w_in", "hg_norm_g", "hg_w_out", "sg_w_in", "sg_ln_g", "sg_ln_b", "sg_w_s", "sg_b_s",
          "sg_w_out", "ffn_w_up", "ffn_conv_w", "ffn_conv_b", "ffn_w_down", "ln1_g", "ln1_b", "ln2_g", "ln2_b")


def kernel(x, lb_logits, hg_w_in, hg_norm_g, hg_w_out, sg_w_in, sg_ln_g, sg_ln_b, sg_w_s, sg_b_s, sg_w_out, ffn_w_up, ffn_conv_w, ffn_conv_b, ffn_w_down, ln1_g, ln1_b, ln2_g, ln2_b, loss_target, m_lb_logits, m_hg_w_in, m_hg_norm_g, m_hg_w_out, m_sg_w_in, m_sg_ln_g, m_sg_ln_b, m_sg_w_s, m_sg_b_s, m_sg_w_out, m_ffn_w_up, m_ffn_conv_w, m_ffn_conv_b, m_ffn_w_down, m_ln1_g, m_ln1_b, m_ln2_g, m_ln2_b, v_lb_logits, v_hg_w_in, v_hg_norm_g, v_hg_w_out, v_sg_w_in, v_sg_ln_g, v_sg_ln_b, v_sg_w_s, v_sg_b_s, v_sg_w_out, v_ffn_w_up, v_ffn_conv_w, v_ffn_conv_b, v_ffn_w_down, v_ln1_g, v_ln1_b, v_ln2_g, v_ln2_b):
    w = dict(lb_logits=lb_logits, hg_w_in=hg_w_in, hg_norm_g=hg_norm_g, hg_w_out=hg_w_out, sg_w_in=sg_w_in,
             sg_ln_g=sg_ln_g, sg_ln_b=sg_ln_b, sg_w_s=sg_w_s, sg_b_s=sg_b_s, sg_w_out=sg_w_out, ffn_w_up=ffn_w_up,
             ffn_conv_w=ffn_conv_w, ffn_conv_b=ffn_conv_b, ffn_w_down=ffn_w_down, ln1_g=ln1_g, ln1_b=ln1_b,
             ln2_g=ln2_g, ln2_b=ln2_b)
    m = dict(lb_logits=m_lb_logits, hg_w_in=m_hg_w_in, hg_norm_g=m_hg_norm_g, hg_w_out=m_hg_w_out,
             sg_w_in=m_sg_w_in, sg_ln_g=m_sg_ln_g, sg_ln_b=m_sg_ln_b, sg_w_s=m_sg_w_s, sg_b_s=m_sg_b_s,
             sg_w_out=m_sg_w_out, ffn_w_up=m_ffn_w_up, ffn_conv_w=m_ffn_conv_w, ffn_conv_b=m_ffn_conv_b,
             ffn_w_down=m_ffn_w_down, ln1_g=m_ln1_g, ln1_b=m_ln1_b, ln2_g=m_ln2_g, ln2_b=m_ln2_b)
    v = dict(lb_logits=v_lb_logits, hg_w_in=v_hg_w_in, hg_norm_g=v_hg_norm_g, hg_w_out=v_hg_w_out,
             sg_w_in=v_sg_w_in, sg_ln_g=v_sg_ln_g, sg_ln_b=v_sg_ln_b, sg_w_s=v_sg_w_s, sg_b_s=v_sg_b_s,
             sg_w_out=v_sg_w_out, ffn_w_up=v_ffn_w_up, ffn_conv_w=v_ffn_conv_w, ffn_conv_b=v_ffn_conv_b,
             ffn_w_down=v_ffn_w_down, ln1_g=v_ln1_g, ln1_b=v_ln1_b, ln2_g=v_ln2_g, ln2_b=v_ln2_b)
    me = 4 * lax.axis_index("x") + 2 * lax.axis_index("y") + lax.axis_index("c")
    is_me = (jnp.arange(N_DEV) == me).astype(F32)
    d = D_MODEL

    shards = [("hg_w_in", hg_w_in[0]), ("hg_w_out", hg_w_out[0]), ("ffn_w_up0", ffn_w_up[0]),
              ("ffn_w_down0", ffn_w_down[0]), ("sg_w_in", sg_w_in[0]), ("sg_w_out", sg_w_out[0]),
              ("ffn_w_up1", ffn_w_up[1]), ("ffn_w_down1", ffn_w_down[1])]
    row_sharded = {"hg_w_out": (d, d), "sg_w_out": (d, d), "ffn_w_down0": (D_FF, d), "ffn_w_down1": (D_FF, d)}
    sv = jnp.zeros((8, 768), F32)
    sv = sv.at[0, :256].set(sg_ln_g[0]).at[1, :256].set(sg_ln_b[0]).at[2:8, :704].set(ffn_conv_w.reshape(6, 704))
    small_land = lax.dynamic_update_index_in_dim(lax.empty((N_DEV,) + sv.shape, F32), sv, me, 0)
    gathers, gathered, dep = {}, {}, is_me
    for n, shard in shards:
        land = lax.dynamic_update_index_in_dim(lax.empty((N_DEV,) + shard.shape, BF16), shard.astype(BF16), me, 0)
        gathers[n] = _xchg_start(None, land, dep=dep, name=f"ag_{n}", peers=OTHER_CHIPS)
        dep = gathers[n][-1]
        if n == shards[0][0]:
            small_gather = _xchg_start(None, small_land, dep=dep, name="ag_small")
            dep = small_gather[-1]
    x_bf = _to_bf16(x[0], dep, "x_bf16")
    (sv,) = _xchg_wait(small_gather, x_bf, "agw_small")
    last_start = sv
    handed = {}
    order = [n for n, _ in shards]

    def hand_over(n, after):
        (land,) = _xchg_wait(gathers[n], after, f"agw_{n}", n_pieces=len(OTHER_CHIPS))
        handed[n] = _xchg_start(None, land, dep=is_me, name=f"ah_{n}", peers=SIBLING, pieces=(0,) + OTHER_CHIPS)

    def weight(n, after):
        if n not in gathered:
            if n == order[0]:
                hand_over(n, last_start)
                after = handed[n][-1]
            nxt = order.index(n) + 1
            if nxt < len(order):
                hand_over(order[nxt], after)
                after = handed[order[nxt]][-1]
            (full,) = _xchg_wait(handed[n], after, f"ahw_{n}", n_pieces=1 + len(OTHER_CHIPS))
            gathered[n] = full.reshape(row_sharded[n]) if n in row_sharded else full
        return gathered[n]

    grad_sends = {}

    def send_grad(n, parts):
        shape = (N_DEV,) + dict(shards)[n].shape
        grad_sends[n] = _xchg_start(parts.reshape(shape), lax.empty(shape, parts.dtype), dep=is_me, name=f"rs_{n}")
        return grad_sends[n][-1]

    small_send = {}

    def send_small(g):
        small = [g["lb_logits"], g["hg_norm_g"], g["sg_ln_g"], g["sg_ln_b"], g["sg_w_s"], g["sg_b_s"],
                 g["conv_w0"], g["conv_w1"], g["conv_b0"], g["conv_b1"], g["ln1_g0"], g["ln1_g1"],
                 g["ln1_b0"], g["ln1_b1"], g["ln2_g0"], g["ln2_g1"], g["ln2_b0"], g["ln2_b1"]]
        flat = jnp.concatenate([a.reshape(-1) for a in small])
        rows = -(-flat.shape[0] // (8 * 128)) * 8
        flat = jnp.pad(flat, (0, rows * 128 - flat.shape[0])).reshape(rows, 128)
        land = lax.dynamic_update_index_in_dim(lax.empty((N_DEV, rows, 128), F32), flat, me, 0)
        small_send["shapes"] = [a.shape for a in small]
        small_send["handles"] = _xchg_start(None, land, dep=is_me, name="ar_small")
        return small_send["handles"][-1]

    p = {}
    p["sg_ln_g"] = sv[:, 0, :256].reshape(1, d)
    p["sg_ln_b"] = sv[:, 1, :256].reshape(1, d)
    conv_w_full = jnp.transpose(sv[:, 2:8, :704].reshape(N_DEV, DEPTH, 3, 704), (1, 2, 0, 3)).reshape(DEPTH, 3, D_FF)
    p["ffn_conv_w"] = [conv_w_full[l] for l in range(DEPTH)]
    p["ffn_conv_b"] = [ffn_conv_b[l:l + 1] for l in range(DEPTH)]
    p["hg_norm_g"] = hg_norm_g
    p["sg_w_s"] = sg_w_s[0]
    p["sg_bias"] = jnp.broadcast_to(sg_b_s[0][:, :, None], (SG_GROUPS, SG_CHUNK, SG_DIM))
    for n in ("ln1_g", "ln1_b", "ln2_g", "ln2_b"):
        p[n] = [w[n][l:l + 1] for l in range(DEPTH)]

    loss_part, grad_x = _local_step(x[0], x_bf, loss_target[0], lb_logits, p, weight, send_grad, send_small)
    loss = lax.psum(loss_part, ("x", "y", "c"))

    (landed,) = _xchg_wait(small_send["handles"], grad_x, "arw_small")
    red = _sum_devices(landed, "sum_small").reshape(-1)
    shapes = small_send["shapes"]
    offs = np.cumsum([0] + [int(np.prod(sh)) for sh in shapes])
    r = [red[offs[i]:offs[i + 1]].reshape(shapes[i]) for i in range(len(shapes))]
    gs = {}
    gs["lb_logits"] = r[0]
    gs["hg_norm_g"] = r[1]
    gs["sg_ln_g"] = lax.dynamic_slice(r[2], (0, me * 256), (1, 256))
    gs["sg_ln_b"] = lax.dynamic_slice(r[3], (0, me * 256), (1, 256))
    gs["sg_w_s"] = r[4][None]
    gs["sg_b_s"] = r[5][None]
    gs["ffn_conv_w"] = lax.dynamic_slice(jnp.stack([r[6], r[7]]), (0, 0, me * 704), (DEPTH, 3, 704))
    gs["ffn_conv_b"] = jnp.concatenate([r[8], r[9]], axis=0)
    gs["ln1_g"] = jnp.concatenate([r[10], r[11]], axis=0)
    gs["ln1_b"] = jnp.concatenate([r[12], r[13]], axis=0)
    gs["ln2_g"] = jnp.concatenate([r[14], r[15]], axis=0)
    gs["ln2_b"] = jnp.concatenate([r[16], r[17]], axis=0)

    out_g, out_d, out_m, out_v = {}, {}, {}, {}
    for n in _SMALL:
        out_g[n], out_d[n], out_m[n], out_v[n] = _adamw(w[n], gs[n], m[n], v[n], f"adamw_{n}")

    res, after = {}, out_v["ln2_b"]
    for n in ("ffn_w_down1", "ffn_w_up1", "sg_w_out", "sg_w_in", "ffn_w_down0", "ffn_w_up0", "hg_w_out", "hg_w_in"):
        parts, recv = _xchg_wait(grad_sends[n], after, f"rsw_{n}")
        own = lax.dynamic_index_in_dim(parts, me, 0, keepdims=False)
        if n[-1] in "01":
            base, layer = n[:-1], int(n[-1])
            res[n] = _adamw(w[base], None, m[base], v[base], f"adamw_{n}", parts=recv, own=own, is_me=is_me,
                            layer=layer, prev=res.get(base + "1"))
        else:
            res[n] = _adamw(w[n][0], None, m[n][0], v[n][0], f"adamw_{n}", parts=recv, own=own, is_me=is_me)
        after = res[n][3]
    for n in ("hg_w_in", "hg_w_out", "sg_w_in", "sg_w_out"):
        out_g[n], out_d[n], out_m[n], out_v[n] = (a[None] for a in res[n])
    for n in ("ffn_w_up", "ffn_w_down"):
        out_g[n], out_d[n], out_m[n], out_v[n] = res[n + "0"]

    return (loss, grad_x[None], *[out_g[n] for n in _NAMES], *[out_d[n] for n in _NAMES],
            *[out_m[n] for n in _NAMES], *[out_v[n] for n in _NAMES])
```

```python
import functools

import numpy as np
import jax
import jax.numpy as jnp
from jax import lax
from jax.experimental import pallas as pl
from jax.experimental.pallas import tpu as pltpu

F32 = jnp.float32
BF16 = jnp.bfloat16
HI = lax.Precision.HIGHEST

N_DEV = 8
D_MODEL = 2048
HG_HEADS = 16
HG_DIM = 128
HG_CHUNK = 128
SG_GROUPS = 16
SG_DIM = 128
SG_CHUNK = 128
D_FF = 5632
DEPTH = 2
ALPHA = (2 * DEPTH) ** 0.25
LN_EPS = 1e-5
RMS_EPS = 1e-6
ADAM_LR = 0.001
ADAM_B1 = 0.9
ADAM_B2 = 0.999
ADAM_EPS = 1e-08
ADAM_WD = 0.01
ADAM_STEP = 10

VMEM_LIMIT = 56 * 1024 * 1024
MESH = pl.DeviceIdType.MESH
ANY = pl.BlockSpec(memory_space=pl.ANY)


def _cp(*sem):
    return pltpu.CompilerParams(dimension_semantics=sem, vmem_limit_bytes=VMEM_LIMIT)


def _pick(n, cands):
    for c in cands:
        if n % c == 0:
            return c
    raise ValueError(f"no tile for {n} in {cands}")


_DIMS = {"nn": (((1,), (0,)), ((), ())), "nt": (((1,), (1,)), ((), ())), "tn": (((0,), (0,)), ((), ()))}


MM_VMEM_BUDGET = 44 * 1024 * 1024


def _mm_tiles(m, n, kk, n_div, k_div, out_bytes, has_addend, k_piece, transposed_a):
    best = None
    for bm in (1024, 1408, 512, 256):
        if m % bm:
            continue
        for bn in (1024, 1408, 512, 256, 128):
            if n % bn or any(d % bn for d in n_div):
                continue
            for bk in (kk, 5632, 4096, 2816, 2048, 1408, 1024, 512, 256, 128):
                if bk > kk or kk % bk or any(d % bk for d in k_div):
                    continue
                if k_piece and bk % k_piece:
                    continue
                nk = kk // bk
                vmem = 4 * (bm * bk + bk * bn) + 2 * bm * bn * out_bytes + (nk > 1) * 4 * bm * bn
                vmem += has_addend * 8 * bm * bn + transposed_a * 2 * bm * bk
                vmem += (out_bytes < 4) * 4 * bm * bn
                if vmem > MM_VMEM_BUDGET:
                    continue
                score = (max(nk, 2), -(bm * bn), -bk)
                if best is None or score < best[0]:
                    best = (score, bm, bn, bk)
    if best is None:
        raise ValueError(f"no tiles for {(m, n, kk)}")
    return best[1:]


def _mm(a, b, *, mode, name, out_dtype=F32, a_cs=False, b_cs=False, o_cs=None, addend=None, deps=()):
    if mode == "tn":
        kk, m = a.shape
    elif a_cs:
        m, kk = a.shape[1], a.shape[0] * a.shape[2]
    else:
        m, kk = a.shape
    if mode == "nt":
        n = b.shape[1] if b_cs else b.shape[0]
    else:
        n = b.shape[0] * b.shape[2] if b_cs else b.shape[1]
    a_c = a.shape[2] if a_cs else None
    b_c = b.shape[2] if b_cs else None
    o_c = n // o_cs if o_cs else None
    k_piece = b_c if (b_cs and mode == "nt") else None
    n_div = [c for c in (b_c if (b_cs and mode != "nt") else None, o_c) if c]
    k_div = [a_c] if a_c else []
    bm, bn, bk = _mm_tiles(m, n, kk, n_div, k_div, jnp.dtype(out_dtype).itemsize, addend is not None, k_piece,
                           mode == "tn")
    nk = kk // bk
    kb = bk // k_piece if k_piece else 1

    def cs_idx(blk, per):
        return (blk * per[0]) // per[1], (blk * per[0] % per[1]) // per[0]

    if mode == "tn":
        a_spec = pl.BlockSpec((bk, bm), lambda i, j, k: (k, i))
    elif a_cs:
        def a_map(i, j, k):
            s, r = cs_idx(k, (bk, a_c))
            return (s, i, r)
        a_spec = pl.BlockSpec((None, bm, bk), a_map)
    else:
        a_spec = pl.BlockSpec((bm, bk), lambda i, j, k: (i, k))
    if mode == "nt":
        if b_cs:
            b_spec = pl.BlockSpec((kb, bn, b_c), lambda i, j, k: (k, j, 0))
        else:
            b_spec = pl.BlockSpec((bn, bk), lambda i, j, k: (j, k))
    else:
        if b_cs:
            def b_map(i, j, k):
                s, r = cs_idx(j, (bn, b_c))
                return (s, k, r)
            b_spec = pl.BlockSpec((None, bk, bn), b_map)
        else:
            b_spec = pl.BlockSpec((bk, bn), lambda i, j, k: (k, j))
    if o_cs:
        def o_map(i, j, k):
            s, r = cs_idx(j, (bn, o_c))
            return (s, i, r)
        o_spec = pl.BlockSpec((None, bm, bn), o_map)
        out_shape = jax.ShapeDtypeStruct((o_cs, m, o_c), out_dtype)
    else:
        o_spec = pl.BlockSpec((bm, bn), lambda i, j, k: (i, j))
        out_shape = jax.ShapeDtypeStruct((m, n), out_dtype)
    in_specs = [a_spec, b_spec]
    args = [a, b]
    if addend is not None:
        in_specs.append(pl.BlockSpec((bm, bn), lambda i, j, k: (i, j)))
        args.append(addend)
    dims = _DIMS[mode]
    in_specs += [ANY] * len(deps)
    args += list(deps)
    n_in = len(args)

    def body(*refs):
        a_ref, b_ref = refs[0], refs[1]
        add_ref = refs[2] if addend is not None else None
        o_ref = refs[n_in]
        if k_piece:
            d = None
            for p in range(kb):
                dp = lax.dot_general(a_ref[:, p * b_c:(p + 1) * b_c].astype(BF16), b_ref[p].astype(BF16), dims,
                                     preferred_element_type=F32)
                d = dp if d is None else d + dp
        else:
            d = lax.dot_general(a_ref[...].astype(BF16), b_ref[...].astype(BF16), dims, preferred_element_type=F32)

        def finish(r):
            if addend is not None:
                r = r + add_ref[...]
            o_ref[...] = r.astype(o_ref.dtype)

        if nk == 1:
            finish(d)
            return
        acc_ref = refs[n_in + 1]
        k = pl.program_id(2)

        @pl.when(k == 0)
        def _():
            acc_ref[...] = d

        if nk > 2:
            @pl.when((k > 0) & (k < nk - 1))
            def _():
                acc_ref[...] += d

        @pl.when(k == nk - 1)
        def _():
            finish(acc_ref[...] + d)

    return pl.pallas_call(
        body, name=name, out_shape=out_shape, grid=(m // bm, n // bn, nk),
        in_specs=in_specs, out_specs=o_spec,
        scratch_shapes=[pltpu.VMEM((bm, bn), F32)] if nk > 1 else [],
        compiler_params=_cp("parallel", "parallel", "arbitrary"),
    )(*args)


LN_ROWS = 256


def _ln_stats(s):
    mu = jnp.mean(s, axis=-1, keepdims=True)
    sc = s - mu
    var = jnp.mean(sc * sc, axis=-1, keepdims=True)
    rstd = lax.rsqrt(var + LN_EPS)
    return sc * rstd, rstd


def _ln_fwd(h, sub, g, b, name):
    t, d = h.shape
    row = pl.BlockSpec((LN_ROWS, d), lambda i: (i, 0))
    vec = pl.BlockSpec((1, d), lambda i: (0, 0))

    def body(h_ref, s_ref, g_ref, b_ref, y_ref, yb_ref):
        xhat, _ = _ln_stats(ALPHA * h_ref[...] + s_ref[...])
        y = xhat * g_ref[...] + b_ref[...]
        y_ref[...] = y
        yb_ref[...] = y.astype(BF16)

    return pl.pallas_call(
        body, name=name, grid=(t // LN_ROWS,),
        out_shape=(jax.ShapeDtypeStruct((t, d), F32), jax.ShapeDtypeStruct((t, d), BF16)),
        in_specs=[row, row, vec, vec], out_specs=(row, row), compiler_params=_cp("parallel"),
    )(h, sub, g, b)


def _ln_bwd_math(xhat, rstd, dy, g):
    dxhat = dy * g
    m1 = jnp.mean(dxhat, axis=-1, keepdims=True)
    m2 = jnp.mean(dxhat * xhat, axis=-1, keepdims=True)
    ds = rstd * (dxhat - m1 - xhat * m2)
    dg = jnp.sum(dy * xhat, axis=0, keepdims=True)
    db = jnp.sum(dy, axis=0, keepdims=True)
    return ds, dg, db


def _ln_bwd(h, sub, dy, g, name):
    t, d = h.shape
    row = pl.BlockSpec((LN_ROWS, d), lambda i: (i, 0))
    vec = pl.BlockSpec((1, d), lambda i: (0, 0))

    def body(h_ref, s_ref, dy_ref, g_ref, ds_ref, dres_ref, dg_ref, dbeta_ref):
        xhat, rstd = _ln_stats(ALPHA * h_ref[...] + s_ref[...])
        ds, dg, db = _ln_bwd_math(xhat, rstd, dy_ref[...], g_ref[...])
        ds_ref[...] = ds.astype(BF16)
        dres_ref[...] = ALPHA * ds

        @pl.when(pl.program_id(0) == 0)
        def _():
            dg_ref[...] = jnp.zeros_like(dg_ref)
            dbeta_ref[...] = jnp.zeros_like(dbeta_ref)

        dg_ref[...] += dg
        dbeta_ref[...] += db

    return pl.pallas_call(
        body, name=name, grid=(t // LN_ROWS,),
        out_shape=(jax.ShapeDtypeStruct((t, d), BF16), jax.ShapeDtypeStruct((t, d), F32),
                   jax.ShapeDtypeStruct((1, d), F32), jax.ShapeDtypeStruct((1, d), F32)),
        in_specs=[row, row, row, vec], out_specs=(row, row, vec, vec),
        compiler_params=_cp("arbitrary"),
    )(h, sub, dy, g)


def _ln_loss_bwd(h, sub, target, g, b, name):
    t, d = h.shape
    row = pl.BlockSpec((LN_ROWS, d), lambda i: (i, 0))
    vec = pl.BlockSpec((1, d), lambda i: (0, 0))
    lvec = pl.BlockSpec((1, 128), lambda i: (0, 0))

    def body(h_ref, s_ref, t_ref, g_ref, b_ref, loss_ref, ds_ref, dres_ref, dg_ref, dbeta_ref):
        xhat, rstd = _ln_stats(ALPHA * h_ref[...] + s_ref[...])
        y = xhat * g_ref[...] + b_ref[...]
        err = y - t_ref[...]
        part = 0.5 * jnp.sum(jnp.mean(err * err, axis=-1, keepdims=True), axis=0, keepdims=True)
        ds, dg, db = _ln_bwd_math(xhat, rstd, err * (1.0 / d), g_ref[...])
        ds_ref[...] = ds.astype(BF16)
        dres_ref[...] = ALPHA * ds

        @pl.when(pl.program_id(0) == 0)
        def _():
            loss_ref[...] = jnp.zeros_like(loss_ref)
            dg_ref[...] = jnp.zeros_like(dg_ref)
            dbeta_ref[...] = jnp.zeros_like(dbeta_ref)

        loss_ref[...] += jnp.broadcast_to(part, loss_ref.shape)
        dg_ref[...] += dg
        dbeta_ref[...] += db

    return pl.pallas_call(
        body, name=name, grid=(t // LN_ROWS,),
        out_shape=(jax.ShapeDtypeStruct((1, 128), F32), jax.ShapeDtypeStruct((t, d), BF16),
                   jax.ShapeDtypeStruct((t, d), F32), jax.ShapeDtypeStruct((1, d), F32),
                   jax.ShapeDtypeStruct((1, d), F32)),
        in_specs=[row, row, row, vec, vec], out_specs=(lvec, row, row, vec, vec),
        compiler_params=_cp("arbitrary"),
    )(h, sub, target, g, b)


HG_ROWS = 2048
HG_LEVELS = 7


def _hg_constants(with_total):
    c = HG_CHUNK
    t = np.arange(c)[:, None]
    j = np.arange(c)[None, :]
    blocks, masks = [], [np.eye(c)]
    for lev in range(HG_LEVELS):
        m = 1 << lev
        second = (t % (2 * m)) >= m
        mid = (t // (2 * m)) * (2 * m) + m - 1
        blocks.append((second & (j > mid) & (j <= t)) | ((~second) & (j > t) & (j <= mid)))
        same = (t // (2 * m)) == (j // (2 * m))
        masks.append(same & second & ((j % (2 * m)) < m))
    blocks += [j <= t, j > t] + ([np.ones((c, c), bool)] if with_total else [])
    cm = np.concatenate(blocks, axis=0).astype(np.float32)
    mk = np.concatenate(masks, axis=0).astype(np.float32)
    return jnp.asarray(cm, dtype=BF16), jnp.asarray(mk)


def _dot_01(cm, x, mode):
    hi = x.astype(BF16)
    lo = (x - hi.astype(F32)).astype(BF16)
    both = lax.dot_general(cm, jnp.concatenate([hi, lo], axis=1), _DIMS[mode], preferred_element_type=F32)
    n = x.shape[-1]
    return both[:, :n] + both[:, n:]


@jax.custom_vjp
def _prefix(cm, lf):
    return _dot_01(cm, lf, "nn")


def _prefix_fwd(cm, lf):
    return _prefix(cm, lf), cm


def _prefix_bwd(cm, d):
    return jnp.zeros_like(cm), _dot_01(cm, d, "tn")


_prefix.defvjp(_prefix_fwd, _prefix_bwd)

_BWD = {"nn": (("nt", 0, 1), ("tn", 1, 0)), "nt": (("nn", 0, 1), ("tn", 0, 1)), "tn": (("nt", 1, 0), ("nn", 1, 0))}


def _bdot_raw(a, b, mode):
    return lax.dot_general(a.astype(BF16), b.astype(BF16), _DIMS[mode], preferred_element_type=F32)


@functools.partial(jax.custom_vjp, nondiff_argnums=(2,))
def _bdot(a, b, mode):
    return _bdot_raw(a, b, mode)


def _bdot_fwd(a, b, mode):
    return _bdot_raw(a, b, mode), (a, b)


def _bdot_bwd(mode, res, d):
    a, b = res
    (ma, da_pos, _), (mb, db_pos, _) = _BWD[mode]
    da = _bdot_raw(d, b, ma) if da_pos == 0 else _bdot_raw(b, d, ma)
    db = _bdot_raw(d, a, mb) if db_pos == 0 else _bdot_raw(a, d, mb)
    return da, db


_bdot.defvjp(_bdot_fwd, _bdot_bwd)


def _hg_chunk(pq, pf, pi, pg, l0, l1, l2, ng, s_t, cm, mk, with_total):
    c = HG_CHUNK
    mx = jnp.maximum(jnp.maximum(l0, l1), l2)
    e0, e1, e2 = jnp.exp(l0 - mx), jnp.exp(l1 - mx), jnp.exp(l2 - mx)
    lb = e0 / (e0 + e1 + e2)
    q = pq * jax.nn.sigmoid(pq)
    t1 = jnp.log(lb)
    t2 = jnp.log1p(-lb) + jax.nn.log_sigmoid(pf)
    lf = jnp.maximum(t1, t2) + jnp.log1p(jnp.exp(-jnp.abs(t1 - t2)))
    k = (1.0 - lb) * jax.nn.sigmoid(-pf)
    x = jnp.exp(_prefix(cm, lf))
    scores = mk[0:c] * _bdot(q, k, "nt")
    for lev in range(HG_LEVELS):
        xl = x[lev * c:(lev + 1) * c]
        scores = scores + mk[(lev + 1) * c:(lev + 2) * c] * _bdot(q * xl, k * xl, "nt")
    x_incl = x[HG_LEVELS * c:(HG_LEVELS + 1) * c]
    x_after = x[(HG_LEVELS + 1) * c:(HG_LEVELS + 2) * c]
    if with_total:
        x_total = jnp.concatenate([x[(HG_LEVELS + 2) * c:(HG_LEVELS + 3) * c]] * (HG_DIM // c), axis=0)
    else:
        x_total = jnp.exp(jnp.sum(lf, axis=0, keepdims=True))
    o = _bdot(scores, pi, "nn") + _bdot(q * x_incl, s_t, "nt")
    s_new = s_t * x_total + _bdot(pi, k * x_after, "tn")
    rstd = lax.rsqrt(jnp.mean(o * o, axis=-1, keepdims=True) + RMS_EPS)
    y = o * rstd * ng * (pg * jax.nn.sigmoid(pg))
    return y, s_new


def _hg_specs(t, reverse, with_total):
    nrb = t // HG_ROWS
    rb = (lambda r: nrb - 1 - r) if reverse else (lambda r: r)
    proj = pl.BlockSpec((4, HG_ROWS, HG_DIM), lambda h, r: (0, rb(r), h))
    vec = pl.BlockSpec((1, HG_DIM), lambda h, r: (0, h))
    cm = pl.BlockSpec(((HG_LEVELS + 2 + with_total) * HG_CHUNK, HG_CHUNK), lambda h, r: (0, 0))
    mk = pl.BlockSpec(((HG_LEVELS + 1) * HG_CHUNK, HG_CHUNK), lambda h, r: (0, 0))
    rows = pl.BlockSpec((HG_ROWS, HG_DIM), lambda h, r: (rb(r), h))
    states = pl.BlockSpec((None, HG_ROWS // HG_CHUNK, HG_DIM, HG_DIM), lambda h, r: (h, rb(r), 0, 0))
    return proj, vec, cm, mk, rows, states


def _hg_fwd(proj, l0, l1, l2, ng):
    t = proj.shape[1]
    n_in = HG_ROWS // HG_CHUNK
    cm, mk = _hg_constants(True)
    p_spec, vec, cm_spec, mk_spec, rows, st_spec = _hg_specs(t, False, True)

    def body(p_ref, l0_ref, l1_ref, l2_ref, ng_ref, cm_ref, mk_ref, y_ref, st_ref, s_ref):
        @pl.when(pl.program_id(1) == 0)
        def _():
            s_ref[...] = jnp.zeros_like(s_ref)

        def step(ci, carry):
            sl = pl.ds(pl.multiple_of(ci * HG_CHUNK, HG_CHUNK), HG_CHUNK)
            s_t = s_ref[...]
            st_ref[ci] = s_t
            y, s_new = _hg_chunk(p_ref[0, sl, :], p_ref[1, sl, :], p_ref[2, sl, :], p_ref[3, sl, :],
                                 l0_ref[...], l1_ref[...], l2_ref[...], ng_ref[...], s_t,
                                 cm_ref[...], mk_ref[...], True)
            y_ref[sl, :] = y.astype(BF16)
            s_ref[...] = s_new
            return carry

        lax.fori_loop(0, n_in, step, 0, unroll=True)

    return pl.pallas_call(
        body, name="hg_fwd", grid=(HG_HEADS, t // HG_ROWS),
        out_shape=(jax.ShapeDtypeStruct((t, D_MODEL), BF16),
                   jax.ShapeDtypeStruct((HG_HEADS, t // HG_CHUNK, HG_DIM, HG_DIM), F32)),
        in_specs=[p_spec, vec, vec, vec, vec, cm_spec, mk_spec], out_specs=(rows, st_spec),
        scratch_shapes=[pltpu.VMEM((HG_DIM, HG_DIM), F32)],
        compiler_params=_cp("parallel", "arbitrary"),
    )(proj, l0, l1, l2, ng, cm, mk)


def _hg_bwd(proj, states, dy, l0, l1, l2, ng):
    t = proj.shape[1]
    n_in = HG_ROWS // HG_CHUNK
    cm, mk = _hg_constants(False)
    p_spec, vec, cm_spec, mk_spec, rows, st_spec = _hg_specs(t, True, False)

    def body(p_ref, st_ref, dy_ref, l0_ref, l1_ref, l2_ref, ng_ref, cm_ref, mk_ref,
             dp_ref, dl0_ref, dl1_ref, dl2_ref, dng_ref, ds_ref):
        @pl.when(pl.program_id(1) == 0)
        def _():
            ds_ref[...] = jnp.zeros_like(ds_ref)
            for r in (dl0_ref, dl1_ref, dl2_ref, dng_ref):
                r[...] = jnp.zeros_like(r)

        def step(it, carry):
            ci = n_in - 1 - it
            sl = pl.ds(pl.multiple_of(ci * HG_CHUNK, HG_CHUNK), HG_CHUNK)
            fn = functools.partial(_hg_chunk, cm=cm_ref[...], mk=mk_ref[...], with_total=False)
            _, vjp = jax.vjp(fn, p_ref[0, sl, :], p_ref[1, sl, :], p_ref[2, sl, :], p_ref[3, sl, :],
                             l0_ref[...], l1_ref[...], l2_ref[...], ng_ref[...], st_ref[ci])
            dq, df, di, dg, d0, d1, d2, dn, ds = vjp((dy_ref[sl, :], ds_ref[...]))
            dp_ref[0, sl, :] = dq.astype(BF16)
            dp_ref[1, sl, :] = df.astype(BF16)
            dp_ref[2, sl, :] = di.astype(BF16)
            dp_ref[3, sl, :] = dg.astype(BF16)
            dl0_ref[...] += d0
            dl1_ref[...] += d1
            dl2_ref[...] += d2
            dng_ref[...] += dn
            ds_ref[...] = ds
            return carry

        lax.fori_loop(0, n_in, step, 0, unroll=True)

    v_shape = jax.ShapeDtypeStruct((1, D_MODEL), F32)
    return pl.pallas_call(
        body, name="hg_bwd", grid=(HG_HEADS, t // HG_ROWS),
        out_shape=(jax.ShapeDtypeStruct((4, t, D_MODEL), BF16), v_shape, v_shape, v_shape, v_shape),
        in_specs=[p_spec, st_spec, rows, vec, vec, vec, vec, cm_spec, mk_spec],
        out_specs=(p_spec, vec, vec, vec, vec),
        scratch_shapes=[pltpu.VMEM((HG_DIM, HG_DIM), F32)],
        compiler_params=_cp("parallel", "arbitrary"),
    )(proj, states, dy, l0, l1, l2, ng, cm, mk)


_SQRT_HALF = 0.7071067811865476
_INV_SQRT_2PI = 0.3989422804014327


def _gelu(x):
    return 0.5 * x * (1.0 + lax.erf(x * _SQRT_HALF))


def _gelu_grad(x):
    return 0.5 * (1.0 + lax.erf(x * _SQRT_HALF)) + x * (_INV_SQRT_2PI * jnp.exp(-0.5 * x * x))


def _tril(n):
    return (lax.broadcasted_iota(jnp.int32, (n, n), 0) >= lax.broadcasted_iota(jnp.int32, (n, n), 1)).astype(F32)


def _sg_specs(d):
    row = lambda w: pl.BlockSpec((SG_CHUNK, w), lambda i: (i, 0))
    vec = pl.BlockSpec((1, d), lambda i: (0, 0))
    cube = pl.BlockSpec((SG_GROUPS, SG_CHUNK, SG_CHUNK), lambda i: (0, 0, 0))
    return row, vec, cube


def _sg_fwd(pre, ln_g, ln_b, w_s, bias):
    t = pre.shape[0]
    d = D_MODEL
    row, vec, cube = _sg_specs(d)

    def body(pre_ref, g_ref, b_ref, w_ref, bias_ref, y_ref, vln_ref):
        u = _gelu(pre_ref[:, :d])
        vhat, _ = _ln_stats(_gelu(pre_ref[:, d:]))
        vln_ref[...] = vhat * g_ref[...] + b_ref[...]
        tril = _tril(SG_CHUNK)
        for g in range(SG_GROUPS):
            cs = slice(g * SG_DIM, (g + 1) * SG_DIM)
            gate = _bdot_raw(w_ref[g] * tril, vln_ref[:, cs], "nn") + bias_ref[g]
            y_ref[:, cs] = (u[:, cs] * gate).astype(BF16)

    return pl.pallas_call(
        body, name="sg_fwd", grid=(t // SG_CHUNK,), out_shape=jax.ShapeDtypeStruct((t, d), BF16),
        in_specs=[row(2 * d), vec, vec, cube, cube], out_specs=row(d),
        scratch_shapes=[pltpu.VMEM((SG_CHUNK, d), F32)], compiler_params=_cp("parallel"),
    )(pre, ln_g, ln_b, w_s, bias)


def _sg_bwd(pre, dy, ln_g, ln_b, w_s, bias):
    t = pre.shape[0]
    d = D_MODEL
    row, vec, cube = _sg_specs(d)
    dbs_spec = pl.BlockSpec((SG_GROUPS, 8, SG_CHUNK), lambda i: (0, 0, 0))

    def body(pre_ref, dy_ref, g_ref, b_ref, w_ref, bias_ref, dpre_ref, dw_ref, dbs_ref, dlg_ref, dlb_ref,
             vln_ref, dvln_ref):
        @pl.when(pl.program_id(0) == 0)
        def _():
            for r in (dw_ref, dbs_ref, dlg_ref, dlb_ref):
                r[...] = jnp.zeros_like(r)

        pu = pre_ref[:, :d]
        pv = pre_ref[:, d:]
        u = _gelu(pu)
        vhat, rstd = _ln_stats(_gelu(pv))
        vln_ref[...] = vhat * g_ref[...] + b_ref[...]
        tril = _tril(SG_CHUNK)
        ones = jnp.ones((8, SG_DIM), F32)
        for g in range(SG_GROUPS):
            cs = slice(g * SG_DIM, (g + 1) * SG_DIM)
            wc = w_ref[g] * tril
            vg = vln_ref[:, cs]
            gate = _bdot_raw(wc, vg, "nn") + bias_ref[g]
            dyg = dy_ref[:, cs]
            dgate = dyg * u[:, cs]
            dpre_ref[:, cs] = (dyg * gate * _gelu_grad(pu[:, cs])).astype(BF16)
            dw_ref[g] += tril * _bdot_raw(dgate, vg, "nt")
            dbs_ref[g] += lax.dot_general(ones, dgate, _DIMS["nt"], precision=HI, preferred_element_type=F32)
            dvln_ref[:, cs] = _bdot_raw(wc, dgate, "tn")
        dvln = dvln_ref[...]
        dv, dg, db = _ln_bwd_math(vhat, rstd, dvln, g_ref[...])
        dlg_ref[...] += dg
        dlb_ref[...] += db
        dpre_ref[:, d:] = (dv * _gelu_grad(pv)).astype(BF16)

    return pl.pallas_call(
        body, name="sg_bwd", grid=(t // SG_CHUNK,),
        out_shape=(jax.ShapeDtypeStruct((t, 2 * d), BF16), jax.ShapeDtypeStruct(w_s.shape, F32),
                   jax.ShapeDtypeStruct((SG_GROUPS, 8, SG_CHUNK), F32),
                   jax.ShapeDtypeStruct((1, d), F32), jax.ShapeDtypeStruct((1, d), F32)),
        in_specs=[row(2 * d), row(d), vec, vec, cube, cube],
        out_specs=(row(2 * d), cube, dbs_spec, vec, vec),
        scratch_shapes=[pltpu.VMEM((SG_CHUNK, d), F32), pltpu.VMEM((SG_CHUNK, d), F32)],
        compiler_params=_cp("arbitrary"),
    )(pre, dy, ln_g, ln_b, w_s, bias)


FFN_ROWS = 1024
FFN_COLS = 1408
HALO = 16


def _ffn_conv(a_ext, w_ref, cb_ref):
    a1 = pltpu.roll(a_ext, 1, 0)
    a2 = pltpu.roll(a_ext, 2, 0)
    return w_ref[0:1, :] * a2 + w_ref[1:2, :] * a1 + w_ref[2:3, :] * a_ext + cb_ref[...], a1, a2


def _ffn_fwd(hh, conv_w, conv_b):
    t = hh.shape[1]
    nb8 = FFN_ROWS // HALO
    main = pl.BlockSpec((2, FFN_ROWS, FFN_COLS), lambda c, r: (0, r, c))
    prev = pl.BlockSpec((None, HALO, FFN_COLS), lambda c, r: (0, jnp.maximum(r * nb8 - 1, 0), c))
    wspec = pl.BlockSpec((3, FFN_COLS), lambda c, r: (0, c))
    bspec = pl.BlockSpec((1, FFN_COLS), lambda c, r: (0, c))

    def body(m_ref, p_ref, w_ref, cb_ref, z_ref):
        prev = jnp.where(pl.program_id(1) == 0, 0.0, p_ref[...].astype(F32))
        a_ext = jnp.concatenate([prev, m_ref[0].astype(F32)], axis=0)
        a, _, _ = _ffn_conv(a_ext, w_ref, cb_ref)
        a = a[HALO:]
        z_ref[...] = (a * jax.nn.sigmoid(a) * m_ref[1].astype(F32)).astype(BF16)

    return pl.pallas_call(
        body, name="ffn_fwd", grid=(D_FF // FFN_COLS, t // FFN_ROWS),
        out_shape=jax.ShapeDtypeStruct((t, D_FF), BF16),
        in_specs=[main, prev, wspec, bspec],
        out_specs=pl.BlockSpec((FFN_ROWS, FFN_COLS), lambda c, r: (r, c)),
        compiler_params=_cp("parallel", "parallel"),
    )(hh, hh, conv_w, conv_b)


def _ffn_bwd(hh, dz, conv_w, conv_b):
    t = hh.shape[1]
    nb8 = FFN_ROWS // HALO
    last8 = t // HALO - 1
    nr = t // FFN_ROWS
    main = pl.BlockSpec((2, FFN_ROWS, FFN_COLS), lambda c, r: (0, r, c))
    prev = pl.BlockSpec((None, HALO, FFN_COLS), lambda c, r: (0, jnp.maximum(r * nb8 - 1, 0), c))
    nxt = pl.BlockSpec((2, HALO, FFN_COLS), lambda c, r: (0, jnp.minimum((r + 1) * nb8, last8), c))
    dmain = pl.BlockSpec((FFN_ROWS, FFN_COLS), lambda c, r: (r, c))
    dnxt = pl.BlockSpec((HALO, FFN_COLS), lambda c, r: (jnp.minimum((r + 1) * nb8, last8), c))
    wspec = pl.BlockSpec((3, FFN_COLS), lambda c, r: (0, c))
    bspec = pl.BlockSpec((1, FFN_COLS), lambda c, r: (0, c))

    def body(m_ref, p_ref, n_ref, dz_ref, dzn_ref, w_ref, cb_ref, dh_ref, dw_ref, dcb_ref):
        r = pl.program_id(1)

        @pl.when(r == 0)
        def _():
            dw_ref[...] = jnp.zeros_like(dw_ref)
            dcb_ref[...] = jnp.zeros_like(dcb_ref)

        prev = jnp.where(r == 0, 0.0, p_ref[...].astype(F32))
        a_ext = jnp.concatenate([prev, m_ref[0].astype(F32), n_ref[0].astype(F32)], axis=0)
        a, a1, a2 = _ffn_conv(a_ext, w_ref, cb_ref)
        a = a[HALO:]
        b_ext = jnp.concatenate([m_ref[1], n_ref[1]], axis=0).astype(F32)
        dz_main = dz_ref[...].astype(F32)
        dz_ext = jnp.concatenate([dz_main, jnp.where(r == nr - 1, 0.0, dzn_ref[...].astype(F32))], axis=0)
        sig = jax.nn.sigmoid(a)
        da = dz_ext * b_ext * (sig * (1.0 + a * (1.0 - sig)))
        n_ext = FFN_ROWS + HALO
        da_p1 = pltpu.roll(da, n_ext - 1, 0)
        da_p2 = pltpu.roll(da, n_ext - 2, 0)
        da_raw = w_ref[2:3, :] * da + w_ref[1:2, :] * da_p1 + w_ref[0:1, :] * da_p2
        dh_ref[0] = da_raw[:FFN_ROWS].astype(BF16)
        dh_ref[1] = (dz_main * (a * sig)[:FFN_ROWS]).astype(BF16)
        dam = da[:FFN_ROWS]
        rows = slice(HALO, HALO + FFN_ROWS)
        dw_ref[0:1, :] += jnp.sum(dam * a2[rows], axis=0, keepdims=True)
        dw_ref[1:2, :] += jnp.sum(dam * a1[rows], axis=0, keepdims=True)
        dw_ref[2:3, :] += jnp.sum(dam * a_ext[rows], axis=0, keepdims=True)
        dcb_ref[...] += jnp.sum(dam, axis=0, keepdims=True)

    return pl.pallas_call(
        body, name="ffn_bwd", grid=(D_FF // FFN_COLS, nr),
        out_shape=(jax.ShapeDtypeStruct((2, t, D_FF), BF16), jax.ShapeDtypeStruct((3, D_FF), F32),
                   jax.ShapeDtypeStruct((1, D_FF), F32)),
        in_specs=[main, prev, nxt, dmain, dnxt, wspec, bspec],
        out_specs=(main, wspec, bspec), compiler_params=_cp("parallel", "arbitrary"),
    )(hh, hh, hh, dz, dz, conv_w, conv_b)


def _adamw_math(w, g, m, v):
    m = ADAM_B1 * m + (1.0 - ADAM_B1) * g
    v = ADAM_B2 * v + (1.0 - ADAM_B2) * (g * g)
    m_hat = m / (1.0 - ADAM_B1 ** ADAM_STEP)
    v_hat = v / (1.0 - ADAM_B2 ** ADAM_STEP)
    delta = -ADAM_LR * (m_hat / (jnp.sqrt(v_hat) + ADAM_EPS) + ADAM_WD * w)
    return delta, m, v


def _as2d(shape):
    n = int(np.prod(shape))
    c = shape[-1] if shape[-1] % 128 == 0 else (128 if n % 128 == 0 else shape[-1])
    return n // c, c


def _adamw(w, g, m, v, name, parts=None, own=None, is_me=None, layer=None, prev=None):
    shape = w.shape if layer is None else w.shape[1:]
    r, c = _as2d(shape)
    br = r if r <= 512 else _pick(r, (256, 176, 128, 64, 8))
    blk = pl.BlockSpec((br, c), lambda i: (i, 0))
    if layer is None:
        wblk, o = blk, jax.ShapeDtypeStruct((r, c), F32)
        w2, m2, v2 = (a.reshape(r, c) for a in (w, m, v))
    else:
        wblk, o = pl.BlockSpec((None, br, c), lambda i: (layer, i, 0)), jax.ShapeDtypeStruct((w.shape[0], r, c), F32)
        w2, m2, v2 = (a.reshape(w.shape[0], r, c) for a in (w, m, v))
    extra, extra_specs = [], []
    if parts is not None:
        g_in = parts.reshape(N_DEV, r, c)
        g_spec = pl.BlockSpec((N_DEV, br, c), lambda i: (0, i, 0))
        extra = [own.reshape(r, c), is_me]
        extra_specs = [blk, pl.BlockSpec(memory_space=pltpu.SMEM)]
    else:
        g_in = g.reshape(r, c)
        g_spec = blk
    aliases = {}
    if prev is not None:
        aliases = {4 + len(extra) + j: j for j in range(4)}
        extra = extra + [a.reshape(o.shape) for a in prev]
        extra_specs = extra_specs + [ANY] * 4

    def body(w_ref, g_ref, m_ref, v_ref, *rest):
        go_ref, d_ref, mo_ref, vo_ref = rest[-4:]
        if parts is not None:
            own_ref, me_ref = rest[:2]
            gsum = None
            for i in range(N_DEV):
                term = jnp.where(me_ref[i] > 0.5, own_ref[...], g_ref[i]).astype(F32)
                gsum = term if gsum is None else gsum + term
        else:
            gsum = g_ref[...]
        delta, m_new, v_new = _adamw_math(w_ref[...], gsum, m_ref[...], v_ref[...])
        go_ref[...] = gsum
        d_ref[...] = delta
        mo_ref[...] = m_new
        vo_ref[...] = v_new

    outs = pl.pallas_call(
        body, name=name, grid=(r // br,), out_shape=(o, o, o, o),
        in_specs=[wblk, g_spec, wblk, wblk] + extra_specs, out_specs=(wblk, wblk, wblk, wblk),
        input_output_aliases=aliases, compiler_params=_cp("parallel"),
    )(w2, g_in, m2, v2, *extra)
    return tuple(a.reshape(w.shape) for a in outs)


ALL_PEERS = tuple(range(1, N_DEV))
OTHER_CHIPS = (2, 4, 6)
SIBLING = (1,)


def _me_and_peers(relations=ALL_PEERS):
    x, y, c = lax.axis_index("x"), lax.axis_index("y"), lax.axis_index("c")
    me = 4 * x + 2 * y + c
    peers = []
    for k in relations:
        kx, ky, kc = (k >> 2) & 1, (k >> 1) & 1, k & 1
        px, py, pc = x ^ kx, y ^ ky, c ^ kc
        peers.append(((px, py, pc), 4 * px + 2 * py + pc))
    return me, peers


def _all_gather(shard, name):
    def body(x_ref, o_ref, send_sems, recv_sems, local_sem):
        me, peers = _me_and_peers()
        mine = pltpu.make_async_copy(x_ref, o_ref.at[me], local_sem)
        mine.start()
        sends = []
        for k, (dev, _) in enumerate(peers):
            cp = pltpu.make_async_remote_copy(src_ref=x_ref, dst_ref=o_ref.at[me], send_sem=send_sems.at[k],
                                              recv_sem=recv_sems.at[k], device_id=dev, device_id_type=MESH)
            cp.start()
            sends.append(cp)
        for k, (dev, idx) in enumerate(peers):
            pltpu.make_async_remote_copy(src_ref=x_ref, dst_ref=o_ref.at[idx], send_sem=send_sems.at[k],
                                         recv_sem=recv_sems.at[k], device_id=dev, device_id_type=MESH).wait_recv()
        for cp in sends:
            cp.wait_send()
        mine.wait()

    return pl.pallas_call(
        body, name=name, out_shape=jax.ShapeDtypeStruct((N_DEV,) + shard.shape, shard.dtype),
        in_specs=[ANY], out_specs=ANY,
        scratch_shapes=[pltpu.SemaphoreType.DMA((N_DEV - 1,)), pltpu.SemaphoreType.DMA((N_DEV - 1,)),
                        pltpu.SemaphoreType.DMA],
        compiler_params=pltpu.CompilerParams(has_side_effects=True),
    )(shard)


HBM = pl.BlockSpec(memory_space=pltpu.HBM)
SEM = pl.BlockSpec(memory_space=pltpu.SEMAPHORE)
EFFECT = pltpu.SideEffectType.DATAFLOW_SIDE_EFFECTING


def _xchg_start(parts, land, *, dep, name, peers=ALL_PEERS, pieces=(0,)):
    n_buf = 1 if parts is None else 2

    def body(*refs):
        land_ref = refs[n_buf - 1]
        send_sem, recv_sem = refs[n_buf + 1], refs[n_buf + 2]
        token = refs[2 * n_buf + 3]
        me, to = _me_and_peers(peers)
        for dev, idx in to:
            if parts is None:
                for r in pieces:
                    piece = land_ref.at[me ^ r]
                    pltpu.make_async_remote_copy(src_ref=piece, dst_ref=piece, send_sem=send_sem, recv_sem=recv_sem,
                                                 device_id=dev, device_id_type=MESH).start()
            else:
                pltpu.make_async_remote_copy(src_ref=refs[0].at[idx], dst_ref=land_ref.at[me], send_sem=send_sem,
                                             recv_sem=recv_sem, device_id=dev, device_id_type=MESH).start()
        token[...] = jnp.zeros_like(token)

    bufs = [land] if parts is None else [parts, land]
    return pl.pallas_call(
        body, name=name,
        out_shape=(pltpu.SemaphoreType.DMA(()), pltpu.SemaphoreType.DMA(()),
                   *[pltpu.HBM(b.shape, b.dtype) for b in bufs], jax.ShapeDtypeStruct((8, 128), F32)),
        in_specs=(*[HBM] * n_buf, ANY), out_specs=(SEM, SEM, *[HBM] * n_buf, pl.BlockSpec(memory_space=pltpu.VMEM)),
        input_output_aliases={i: 2 + i for i in range(n_buf)},
        compiler_params=pltpu.CompilerParams(has_side_effects=EFFECT),
    )(*[pltpu.with_memory_space_constraint(b, pltpu.HBM) for b in bufs], dep)


def _xchg_wait(handles, after, name, n_pieces=N_DEV - 1):
    send_sem, recv_sem, *bufs, _ = handles
    n_buf = len(bufs)

    def body(*refs):
        land_ref = refs[n_buf - 1]
        send_sem, recv_sem = refs[n_buf], refs[n_buf + 1]
        x, y, c = lax.axis_index("x"), lax.axis_index("y"), lax.axis_index("c")
        span = land_ref.at[pl.ds(0, n_pieces)]
        cp = pltpu.make_async_remote_copy(src_ref=span, dst_ref=span, send_sem=send_sem, recv_sem=recv_sem,
                                          device_id=(x, y, c), device_id_type=MESH)
        cp.wait_send()
        cp.wait_recv()

    return pl.pallas_call(
        body, name=name, out_shape=tuple(pltpu.HBM(b.shape, b.dtype) for b in bufs),
        in_specs=(*[HBM] * n_buf, SEM, SEM, ANY), out_specs=tuple([HBM] * n_buf),
        input_output_aliases={i: i for i in range(n_buf)},
        compiler_params=pltpu.CompilerParams(has_side_effects=EFFECT),
    )(*bufs, send_sem, recv_sem, after)


def _sum_devices(v, name):
    _, r, c = v.shape

    def body(x_ref, o_ref):
        acc = x_ref[0]
        for i in range(1, N_DEV):
            acc = acc + x_ref[i]
        o_ref[...] = acc

    return pl.pallas_call(
        body, name=name, out_shape=jax.ShapeDtypeStruct((r, c), F32),
        in_specs=[pl.BlockSpec(memory_space=pltpu.VMEM)], out_specs=pl.BlockSpec(memory_space=pltpu.VMEM),
        compiler_params=pltpu.CompilerParams(vmem_limit_bytes=VMEM_LIMIT),
    )(v)


def _local_step(x, target, lb_logits, p, weight, send_grad, send_small):
    g = {}
    l0, l1, l2 = lb_logits[0:1], lb_logits[1:2], lb_logits[2:3]
    x_bf = x.astype(BF16)

    proj0 = _mm(x_bf, weight("hg_w_in", x_bf), mode="nn", b_cs=True, o_cs=4, name="mm_hg_in")
    y0, states = _hg_fwd(proj0, l0, l1, l2, p["hg_norm_g"])
    mixed0 = _mm(y0, weight("hg_w_out", y0), mode="nn", name="mm_hg_out")
    hh_l, z_l = [], []

    def ffn_forward(layer, h_in_bf):
        hh = _mm(h_in_bf, weight(f"ffn_w_up{layer}", h_in_bf), mode="nn", b_cs=True, o_cs=2, out_dtype=BF16,
                 name=f"mm_up{layer}")
        z = _ffn_fwd(hh, p["ffn_conv_w"][layer], p["ffn_conv_b"][layer])
        ffn = _mm(z, weight(f"ffn_w_down{layer}", z), mode="nn", name=f"mm_down{layer}")
        hh_l.append(hh)
        z_l.append(z)
        return ffn

    h1, h1b = _ln_fwd(x, mixed0, p["ln1_g"][0], p["ln1_b"][0], "ln1_0")
    ffn0 = ffn_forward(0, h1b)
    h2, h2b = _ln_fwd(h1, ffn0, p["ln2_g"][0], p["ln2_b"][0], "ln2_0")
    pre1 = _mm(h2b, weight("sg_w_in", h2b), mode="nn", b_cs=True, name="mm_sg_in")
    y1 = _sg_fwd(pre1, p["sg_ln_g"], p["sg_ln_b"], p["sg_w_s"], p["sg_bias"])
    mixed1 = _mm(y1, weight("sg_w_out", y1), mode="nn", name="mm_sg_out")
    h3, h3b = _ln_fwd(h2, mixed1, p["ln1_g"][1], p["ln1_b"][1], "ln1_1")
    ffn1 = ffn_forward(1, h3b)

    loss, ds, dres, g["ln2_g1"], g["ln2_b1"] = _ln_loss_bwd(h3, ffn1, target, p["ln2_g"][1], p["ln2_b"][1], "ln2_1_loss")

    def ffn_backward(layer, ds_bf, h_in_bf, dres):
        tok = send_grad(f"ffn_w_down{layer}",
                        _mm(z_l[layer], ds_bf, mode="tn", out_dtype=BF16, name=f"mm_dw_down{layer}"))
        dz = _mm(ds_bf, weight(f"ffn_w_down{layer}", None), mode="nt", out_dtype=BF16, name=f"mm_d_z{layer}",
                 deps=(tok,))
        dhh, g[f"conv_w{layer}"], g[f"conv_b{layer}"] = _ffn_bwd(hh_l[layer], dz, p["ffn_conv_w"][layer],
                                                                   p["ffn_conv_b"][layer])
        tok = send_grad(f"ffn_w_up{layer}", _mm(h_in_bf, dhh, mode="tn", b_cs=True, o_cs=N_DEV, out_dtype=BF16,
                                                name=f"mm_dw_up{layer}"))
        return _mm(dhh, weight(f"ffn_w_up{layer}", None), mode="nt", a_cs=True, b_cs=True, addend=dres,
                   name=f"mm_d_up{layer}", deps=(tok,))

    dh3 = ffn_backward(1, ds, h3b, dres)
    ds, dres, g["ln1_g1"], g["ln1_b1"] = _ln_bwd(h2, mixed1, dh3, p["ln1_g"][1], "ln1_1_bwd")
    tok = send_grad("sg_w_out", _mm(y1, ds, mode="tn", out_dtype=BF16, name="mm_dw_sg_out"))
    dy1 = _mm(ds, weight("sg_w_out", None), mode="nt", name="mm_d_sg_out", deps=(tok,))
    dpre1, g["sg_w_s"], dbs, g["sg_ln_g"], g["sg_ln_b"] = _sg_bwd(pre1, dy1, p["sg_ln_g"], p["sg_ln_b"],
                                                                    p["sg_w_s"], p["sg_bias"])
    g["sg_b_s"] = dbs[:, 0, :]
    tok = send_grad("sg_w_in", _mm(h2b, dpre1, mode="tn", o_cs=N_DEV, out_dtype=BF16, name="mm_dw_sg_in"))
    dh2 = _mm(dpre1, weight("sg_w_in", None), mode="nt", b_cs=True, addend=dres, name="mm_d_sg_in", deps=(tok,))
    ds, dres, g["ln2_g0"], g["ln2_b0"] = _ln_bwd(h1, ffn0, dh2, p["ln2_g"][0], "ln2_0_bwd")
    dh1 = ffn_backward(0, ds, h1b, dres)
    ds, dres, g["ln1_g0"], g["ln1_b0"] = _ln_bwd(x, mixed0, dh1, p["ln1_g"][0], "ln1_0_bwd")
    tok = send_grad("hg_w_out", _mm(y0, ds, mode="tn", out_dtype=BF16, name="mm_dw_hg_out"))
    dy0 = _mm(ds, weight("hg_w_out", None), mode="nt", name="mm_d_hg_out", deps=(tok,))
    dproj, d0, d1, d2, g["hg_norm_g"] = _hg_bwd(proj0, states, dy0, l0, l1, l2, p["hg_norm_g"])
    g["lb_logits"] = jnp.concatenate([d0, d1, d2], axis=0)
    tok = send_small(g)
    tok = send_grad("hg_w_in", _mm(x_bf, dproj, mode="tn", b_cs=True, o_cs=N_DEV, out_dtype=BF16, name="mm_dw_hg_in",
                                   deps=(tok,)))
    grad_x = _mm(dproj, weight("hg_w_in", None), mode="nt", a_cs=True, b_cs=True, addend=dres, name="mm_d_hg_in",
                 deps=(tok,))
    return loss[0, 0], grad_x


_SMALL = ("lb_logits", "hg_norm_g", "sg_ln_g", "sg_ln_b", "sg_w_s", "sg_b_s", "ffn_conv_w", "ffn_conv_b",
          "ln1_g", "ln1_b", "ln2_g", "ln2_b")
_NAMES = ("lb_logits", "hg_w_in", "hg_norm_g", "hg_w_out", "sg_w_in", "sg_ln_g", "sg_ln_b", "sg_w_s", "sg_b_s",
          "sg_w_out", "ffn_w_up", "ffn_conv_w", "ffn_conv_b", "ffn_w_down", "ln1_g", "ln1_b", "ln2_g", "ln2_b")


def kernel(x, lb_logits, hg_w_in, hg_norm_g, hg_w_out, sg_w_in, sg_ln_g, sg_ln_b, sg_w_s, sg_b_s, sg_w_out, ffn_w_up, ffn_conv_w, ffn_conv_b, ffn_w_down, ln1_g, ln1_b, ln2_g, ln2_b, loss_target, m_lb_logits, m_hg_w_in, m_hg_norm_g, m_hg_w_out, m_sg_w_in, m_sg_ln_g, m_sg_ln_b, m_sg_w_s, m_sg_b_s, m_sg_w_out, m_ffn_w_up, m_ffn_conv_w, m_ffn_conv_b, m_ffn_w_down, m_ln1_g, m_ln1_b, m_ln2_g, m_ln2_b, v_lb_logits, v_hg_w_in, v_hg_norm_g, v_hg_w_out, v_sg_w_in, v_sg_ln_g, v_sg_ln_b, v_sg_w_s, v_sg_b_s, v_sg_w_out, v_ffn_w_up, v_ffn_conv_w, v_ffn_conv_b, v_ffn_w_down, v_ln1_g, v_ln1_b, v_ln2_g, v_ln2_b):
    w = dict(lb_logits=lb_logits, hg_w_in=hg_w_in, hg_norm_g=hg_norm_g, hg_w_out=hg_w_out, sg_w_in=sg_w_in,
             sg_ln_g=sg_ln_g, sg_ln_b=sg_ln_b, sg_w_s=sg_w_s, sg_b_s=sg_b_s, sg_w_out=sg_w_out, ffn_w_up=ffn_w_up,
             ffn_conv_w=ffn_conv_w, ffn_conv_b=ffn_conv_b, ffn_w_down=ffn_w_down, ln1_g=ln1_g, ln1_b=ln1_b,
             ln2_g=ln2_g, ln2_b=ln2_b)
    m = dict(lb_logits=m_lb_logits, hg_w_in=m_hg_w_in, hg_norm_g=m_hg_norm_g, hg_w_out=m_hg_w_out,
             sg_w_in=m_sg_w_in, sg_ln_g=m_sg_ln_g, sg_ln_b=m_sg_ln_b, sg_w_s=m_sg_w_s, sg_b_s=m_sg_b_s,
             sg_w_out=m_sg_w_out, ffn_w_up=m_ffn_w_up, ffn_conv_w=m_ffn_conv_w, ffn_conv_b=m_ffn_conv_b,
             ffn_w_down=m_ffn_w_down, ln1_g=m_ln1_g, ln1_b=m_ln1_b, ln2_g=m_ln2_g, ln2_b=m_ln2_b)
    v = dict(lb_logits=v_lb_logits, hg_w_in=v_hg_w_in, hg_norm_g=v_hg_norm_g, hg_w_out=v_hg_w_out,
             sg_w_in=v_sg_w_in, sg_ln_g=v_sg_ln_g, sg_ln_b=v_sg_ln_b, sg_w_s=v_sg_w_s, sg_b_s=v_sg_b_s,
             sg_w_out=v_sg_w_out, ffn_w_up=v_ffn_w_up, ffn_conv_w=v_ffn_conv_w, ffn_conv_b=v_ffn_conv_b,
             ffn_w_down=v_ffn_w_down, ln1_g=v_ln1_g, ln1_b=v_ln1_b, ln2_g=v_ln2_g, ln2_b=v_ln2_b)
    me = 4 * lax.axis_index("x") + 2 * lax.axis_index("y") + lax.axis_index("c")
    is_me = (jnp.arange(N_DEV) == me).astype(F32)
    d = D_MODEL

    shards = [("hg_w_in", hg_w_in[0]), ("hg_w_out", hg_w_out[0]), ("ffn_w_up0", ffn_w_up[0]),
              ("ffn_w_down0", ffn_w_down[0]), ("sg_w_in", sg_w_in[0]), ("sg_w_out", sg_w_out[0]),
              ("ffn_w_up1", ffn_w_up[1]), ("ffn_w_down1", ffn_w_down[1])]
    row_sharded = {"hg_w_out": (d, d), "sg_w_out": (d, d), "ffn_w_down0": (D_FF, d), "ffn_w_down1": (D_FF, d)}
    sv = jnp.zeros((8, 768), F32)
    sv = sv.at[0, :256].set(sg_ln_g[0]).at[1, :256].set(sg_ln_b[0]).at[2:8, :704].set(ffn_conv_w.reshape(6, 704))
    sv = _all_gather(sv, "ag_small")
    gathers, gathered, dep = {}, {}, sv
    for n, shard in shards:
        land = lax.dynamic_update_index_in_dim(lax.empty((N_DEV,) + shard.shape, BF16), shard.astype(BF16), me, 0)
        gathers[n] = _xchg_start(None, land, dep=dep, name=f"ag_{n}", peers=OTHER_CHIPS)
        dep = gathers[n][-1]
    last_start = dep
    handed = {}
    order = [n for n, _ in shards]

    def hand_over(n, after):
        (land,) = _xchg_wait(gathers[n], after, f"agw_{n}", n_pieces=len(OTHER_CHIPS))
        handed[n] = _xchg_start(None, land, dep=is_me, name=f"ah_{n}", peers=SIBLING, pieces=(0,) + OTHER_CHIPS)

    def weight(n, after):
        if n not in gathered:
            if n == order[0]:
                hand_over(n, last_start)
                after = handed[n][-1]
            nxt = order.index(n) + 1
            if nxt < len(order):
                hand_over(order[nxt], after)
                after = handed[order[nxt]][-1]
            (full,) = _xchg_wait(handed[n], after, f"ahw_{n}", n_pieces=1 + len(OTHER_CHIPS))
            gathered[n] = full.reshape(row_sharded[n]) if n in row_sharded else full
        return gathered[n]

    grad_sends = {}

    def send_grad(n, parts):
        shape = (N_DEV,) + dict(shards)[n].shape
        grad_sends[n] = _xchg_start(parts.reshape(shape), lax.empty(shape, parts.dtype), dep=is_me, name=f"rs_{n}")
        return grad_sends[n][-1]

    small_send = {}

    def send_small(g):
        small = [g["lb_logits"], g["hg_norm_g"], g["sg_ln_g"], g["sg_ln_b"], g["sg_w_s"], g["sg_b_s"],
                 g["conv_w0"], g["conv_w1"], g["conv_b0"], g["conv_b1"], g["ln1_g0"], g["ln1_g1"],
                 g["ln1_b0"], g["ln1_b1"], g["ln2_g0"], g["ln2_g1"], g["ln2_b0"], g["ln2_b1"]]
        flat = jnp.concatenate([a.reshape(-1) for a in small])
        rows = -(-flat.shape[0] // (8 * 128)) * 8
        flat = jnp.pad(flat, (0, rows * 128 - flat.shape[0])).reshape(rows, 128)
        land = lax.dynamic_update_index_in_dim(lax.empty((N_DEV, rows, 128), F32), flat, me, 0)
        small_send["shapes"] = [a.shape for a in small]
        small_send["handles"] = _xchg_start(None, land, dep=is_me, name="ar_small")
        return small_send["handles"][-1]

    p = {}
    p["sg_ln_g"] = sv[:, 0, :256].reshape(1, d)
    p["sg_ln_b"] = sv[:, 1, :256].reshape(1, d)
    conv_w_full = jnp.transpose(sv[:, 2:8, :704].reshape(N_DEV, DEPTH, 3, 704), (1, 2, 0, 3)).reshape(DEPTH, 3, D_FF)
    p["ffn_conv_w"] = [conv_w_full[l] for l in range(DEPTH)]
    p["ffn_conv_b"] = [ffn_conv_b[l:l + 1] for l in range(DEPTH)]
    p["hg_norm_g"] = hg_norm_g
    p["sg_w_s"] = sg_w_s[0]
    p["sg_bias"] = jnp.broadcast_to(sg_b_s[0][:, :, None], (SG_GROUPS, SG_CHUNK, SG_DIM))
    for n in ("ln1_g", "ln1_b", "ln2_g", "ln2_b"):
        p[n] = [w[n][l:l + 1] for l in range(DEPTH)]

    loss_part, grad_x = _local_step(x[0], loss_target[0], lb_logits, p, weight, send_grad, send_small)
    loss = lax.psum(loss_part, ("x", "y", "c"))

    (landed,) = _xchg_wait(small_send["handles"], grad_x, "arw_small")
    red = _sum_devices(landed, "sum_small").reshape(-1)
    shapes = small_send["shapes"]
    offs = np.cumsum([0] + [int(np.prod(sh)) for sh in shapes])
    r = [red[offs[i]:offs[i + 1]].reshape(shapes[i]) for i in range(len(shapes))]
    gs = {}
    gs["lb_logits"] = r[0]
    gs["hg_norm_g"] = r[1]
    gs["sg_ln_g"] = lax.dynamic_slice(r[2], (0, me * 256), (1, 256))
    gs["sg_ln_b"] = lax.dynamic_slice(r[3], (0, me * 256), (1, 256))
    gs["sg_w_s"] = r[4][None]
    gs["sg_b_s"] = r[5][None]
    gs["ffn_conv_w"] = lax.dynamic_slice(jnp.stack([r[6], r[7]]), (0, 0, me * 704), (DEPTH, 3, 704))
    gs["ffn_conv_b"] = jnp.concatenate([r[8], r[9]], axis=0)
    gs["ln1_g"] = jnp.concatenate([r[10], r[11]], axis=0)
    gs["ln1_b"] = jnp.concatenate([r[12], r[13]], axis=0)
    gs["ln2_g"] = jnp.concatenate([r[14], r[15]], axis=0)
    gs["ln2_b"] = jnp.concatenate([r[16], r[17]], axis=0)

    out_g, out_d, out_m, out_v = {}, {}, {}, {}
    for n in _SMALL:
        out_g[n], out_d[n], out_m[n], out_v[n] = _adamw(w[n], gs[n], m[n], v[n], f"adamw_{n}")

    res, after = {}, out_v["ln2_b"]
    for n in ("ffn_w_down1", "ffn_w_up1", "sg_w_out", "sg_w_in", "ffn_w_down0", "ffn_w_up0", "hg_w_out", "hg_w_in"):
        parts, recv = _xchg_wait(grad_sends[n], after, f"rsw_{n}")
        own = lax.dynamic_index_in_dim(parts, me, 0, keepdims=False)
        if n[-1] in "01":
            base, layer = n[:-1], int(n[-1])
            res[n] = _adamw(w[base], None, m[base], v[base], f"adamw_{n}", parts=recv, own=own, is_me=is_me,
                            layer=layer, prev=res.get(base + "1"))
        else:
            res[n] = _adamw(w[n][0], None, m[n][0], v[n][0], f"adamw_{n}", parts=recv, own=own, is_me=is_me)
        after = res[n][3]
    for n in ("hg_w_in", "hg_w_out", "sg_w_in", "sg_w_out"):
        out_g[n], out_d[n], out_m[n], out_v[n] = (a[None] for a in res[n])
    for n in ("ffn_w_up", "ffn_w_down"):
        out_g[n], out_d[n], out_m[n], out_v[n] = res[n + "0"]

    return (loss, grad_x[None], *[out_g[n] for n in _NAMES], *[out_d[n] for n in _NAMES],
            *[out_m[n] for n in _NAMES], *[out_v[n] for n in _NAMES])
```

```python
import functools

import numpy as np
import jax
import jax.numpy as jnp
from jax import lax
from jax.experimental import pallas as pl
from jax.experimental.pallas import tpu as pltpu

F32 = jnp.float32
BF16 = jnp.bfloat16
HI = lax.Precision.HIGHEST

N_DEV = 8
D_MODEL = 2048
HG_HEADS = 16
HG_DIM = 128
HG_CHUNK = 128
SG_GROUPS = 16
SG_DIM = 128
SG_CHUNK = 128
D_FF = 5632
DEPTH = 2
ALPHA = (2 * DEPTH) ** 0.25
LN_EPS = 1e-5
RMS_EPS = 1e-6
ADAM_LR = 0.001
ADAM_B1 = 0.9
ADAM_B2 = 0.999
ADAM_EPS = 1e-08
ADAM_WD = 0.01
ADAM_STEP = 10

VMEM_LIMIT = 56 * 1024 * 1024
MESH = pl.DeviceIdType.MESH
ANY = pl.BlockSpec(memory_space=pl.ANY)


def _cp(*sem):
    return pltpu.CompilerParams(dimension_semantics=sem, vmem_limit_bytes=VMEM_LIMIT)


def _pick(n, cands):
    for c in cands:
        if n % c == 0:
            return c
    raise ValueError(f"no tile for {n} in {cands}")


_DIMS = {"nn": (((1,), (0,)), ((), ())), "nt": (((1,), (1,)), ((), ())), "tn": (((0,), (0,)), ((), ()))}


MM_VMEM_BUDGET = 44 * 1024 * 1024


def _mm_tiles(m, n, kk, n_div, k_div, out_bytes, has_addend, k_piece, transposed_a):
    best = None
    for bm in (1024, 1408, 512, 256):
        if m % bm:
            continue
        for bn in (1024, 1408, 512, 256, 128):
            if n % bn or any(d % bn for d in n_div):
                continue
            for bk in (kk, 5632, 4096, 2816, 2048, 1408, 1024, 512, 256, 128):
                if bk > kk or kk % bk or any(d % bk for d in k_div):
                    continue
                if k_piece and bk % k_piece:
                    continue
                nk = kk // bk
                vmem = 4 * (bm * bk + bk * bn) + 2 * bm * bn * out_bytes + (nk > 1) * 4 * bm * bn
                vmem += has_addend * 8 * bm * bn + transposed_a * 2 * bm * bk
                vmem += (out_bytes < 4) * 4 * bm * bn
                if vmem > MM_VMEM_BUDGET:
                    continue
                score = (max(nk, 2), -(bm * bn), -bk)
                if best is None or score < best[0]:
                    best = (score, bm, bn, bk)
    if best is None:
        raise ValueError(f"no tiles for {(m, n, kk)}")
    return best[1:]


def _mm(a, b, *, mode, name, out_dtype=F32, a_cs=False, b_cs=False, o_cs=None, addend=None, deps=()):
    if mode == "tn":
        kk, m = a.shape
    elif a_cs:
        m, kk = a.shape[1], a.shape[0] * a.shape[2]
    else:
        m, kk = a.shape
    if mode == "nt":
        n = b.shape[1] if b_cs else b.shape[0]
    else:
        n = b.shape[0] * b.shape[2] if b_cs else b.shape[1]
    a_c = a.shape[2] if a_cs else None
    b_c = b.shape[2] if b_cs else None
    o_c = n // o_cs if o_cs else None
    k_piece = b_c if (b_cs and mode == "nt") else None
    n_div = [c for c in (b_c if (b_cs and mode != "nt") else None, o_c) if c]
    k_div = [a_c] if a_c else []
    bm, bn, bk = _mm_tiles(m, n, kk, n_div, k_div, jnp.dtype(out_dtype).itemsize, addend is not None, k_piece,
                           mode == "tn")
    nk = kk // bk
    kb = bk // k_piece if k_piece else 1

    def cs_idx(blk, per):
        return (blk * per[0]) // per[1], (blk * per[0] % per[1]) // per[0]

    if mode == "tn":
        a_spec = pl.BlockSpec((bk, bm), lambda i, j, k: (k, i))
    elif a_cs:
        def a_map(i, j, k):
            s, r = cs_idx(k, (bk, a_c))
            return (s, i, r)
        a_spec = pl.BlockSpec((None, bm, bk), a_map)
    else:
        a_spec = pl.BlockSpec((bm, bk), lambda i, j, k: (i, k))
    if mode == "nt":
        if b_cs:
            b_spec = pl.BlockSpec((kb, bn, b_c), lambda i, j, k: (k, j, 0))
        else:
            b_spec = pl.BlockSpec((bn, bk), lambda i, j, k: (j, k))
    else:
        if b_cs:
            def b_map(i, j, k):
                s, r = cs_idx(j, (bn, b_c))
                return (s, k, r)
            b_spec = pl.BlockSpec((None, bk, bn), b_map)
        else:
            b_spec = pl.BlockSpec((bk, bn), lambda i, j, k: (k, j))
    if o_cs:
        def o_map(i, j, k):
            s, r = cs_idx(j, (bn, o_c))
            return (s, i, r)
        o_spec = pl.BlockSpec((None, bm, bn), o_map)
        out_shape = jax.ShapeDtypeStruct((o_cs, m, o_c), out_dtype)
    else:
        o_spec = pl.BlockSpec((bm, bn), lambda i, j, k: (i, j))
        out_shape = jax.ShapeDtypeStruct((m, n), out_dtype)
    in_specs = [a_spec, b_spec]
    args = [a, b]
    if addend is not None:
        in_specs.append(pl.BlockSpec((bm, bn), lambda i, j, k: (i, j)))
        args.append(addend)
    dims = _DIMS[mode]
    in_specs += [ANY] * len(deps)
    args += list(deps)
    n_in = len(args)

    def body(*refs):
        a_ref, b_ref = refs[0], refs[1]
        add_ref = refs[2] if addend is not None else None
        o_ref = refs[n_in]
        if k_piece:
            d = None
            for p in range(kb):
                dp = lax.dot_general(a_ref[:, p * b_c:(p + 1) * b_c].astype(BF16), b_ref[p].astype(BF16), dims,
                                     preferred_element_type=F32)
                d = dp if d is None else d + dp
        else:
            d = lax.dot_general(a_ref[...].astype(BF16), b_ref[...].astype(BF16), dims, preferred_element_type=F32)

        def finish(r):
            if addend is not None:
                r = r + add_ref[...]
            o_ref[...] = r.astype(o_ref.dtype)

        if nk == 1:
            finish(d)
            return
        acc_ref = refs[n_in + 1]
        k = pl.program_id(2)

        @pl.when(k == 0)
        def _():
            acc_ref[...] = d

        if nk > 2:
            @pl.when((k > 0) & (k < nk - 1))
            def _():
                acc_ref[...] += d

        @pl.when(k == nk - 1)
        def _():
            finish(acc_ref[...] + d)

    return pl.pallas_call(
        body, name=name, out_shape=out_shape, grid=(m // bm, n // bn, nk),
        in_specs=in_specs, out_specs=o_spec,
        scratch_shapes=[pltpu.VMEM((bm, bn), F32)] if nk > 1 else [],
        compiler_params=_cp("parallel", "parallel", "arbitrary"),
    )(*args)


LN_ROWS = 256


def _ln_stats(s):
    mu = jnp.mean(s, axis=-1, keepdims=True)
    sc = s - mu
    var = jnp.mean(sc * sc, axis=-1, keepdims=True)
    rstd = lax.rsqrt(var + LN_EPS)
    return sc * rstd, rstd


def _ln_fwd(h, sub, g, b, name):
    t, d = h.shape
    row = pl.BlockSpec((LN_ROWS, d), lambda i: (i, 0))
    vec = pl.BlockSpec((1, d), lambda i: (0, 0))

    def body(h_ref, s_ref, g_ref, b_ref, y_ref, yb_ref):
        xhat, _ = _ln_stats(ALPHA * h_ref[...] + s_ref[...])
        y = xhat * g_ref[...] + b_ref[...]
        y_ref[...] = y
        yb_ref[...] = y.astype(BF16)

    return pl.pallas_call(
        body, name=name, grid=(t // LN_ROWS,),
        out_shape=(jax.ShapeDtypeStruct((t, d), F32), jax.ShapeDtypeStruct((t, d), BF16)),
        in_specs=[row, row, vec, vec], out_specs=(row, row), compiler_params=_cp("parallel"),
    )(h, sub, g, b)


def _ln_bwd_math(xhat, rstd, dy, g):
    dxhat = dy * g
    m1 = jnp.mean(dxhat, axis=-1, keepdims=True)
    m2 = jnp.mean(dxhat * xhat, axis=-1, keepdims=True)
    ds = rstd * (dxhat - m1 - xhat * m2)
    dg = jnp.sum(dy * xhat, axis=0, keepdims=True)
    db = jnp.sum(dy, axis=0, keepdims=True)
    return ds, dg, db


def _ln_bwd(h, sub, dy, g, name):
    t, d = h.shape
    row = pl.BlockSpec((LN_ROWS, d), lambda i: (i, 0))
    vec = pl.BlockSpec((1, d), lambda i: (0, 0))

    def body(h_ref, s_ref, dy_ref, g_ref, ds_ref, dres_ref, dg_ref, dbeta_ref):
        xhat, rstd = _ln_stats(ALPHA * h_ref[...] + s_ref[...])
        ds, dg, db = _ln_bwd_math(xhat, rstd, dy_ref[...], g_ref[...])
        ds_ref[...] = ds.astype(BF16)
        dres_ref[...] = ALPHA * ds

        @pl.when(pl.program_id(0) == 0)
        def _():
            dg_ref[...] = jnp.zeros_like(dg_ref)
            dbeta_ref[...] = jnp.zeros_like(dbeta_ref)

        dg_ref[...] += dg
        dbeta_ref[...] += db

    return pl.pallas_call(
        body, name=name, grid=(t // LN_ROWS,),
        out_shape=(jax.ShapeDtypeStruct((t, d), BF16), jax.ShapeDtypeStruct((t, d), F32),
                   jax.ShapeDtypeStruct((1, d), F32), jax.ShapeDtypeStruct((1, d), F32)),
        in_specs=[row, row, row, vec], out_specs=(row, row, vec, vec),
        compiler_params=_cp("arbitrary"),
    )(h, sub, dy, g)


def _ln_loss_bwd(h, sub, target, g, b, name):
    t, d = h.shape
    row = pl.BlockSpec((LN_ROWS, d), lambda i: (i, 0))
    vec = pl.BlockSpec((1, d), lambda i: (0, 0))
    lvec = pl.BlockSpec((1, 128), lambda i: (0, 0))

    def body(h_ref, s_ref, t_ref, g_ref, b_ref, loss_ref, ds_ref, dres_ref, dg_ref, dbeta_ref):
        xhat, rstd = _ln_stats(ALPHA * h_ref[...] + s_ref[...])
        y = xhat * g_ref[...] + b_ref[...]
        err = y - t_ref[...]
        part = 0.5 * jnp.sum(jnp.mean(err * err, axis=-1, keepdims=True), axis=0, keepdims=True)
        ds, dg, db = _ln_bwd_math(xhat, rstd, err * (1.0 / d), g_ref[...])
        ds_ref[...] = ds.astype(BF16)
        dres_ref[...] = ALPHA * ds

        @pl.when(pl.program_id(0) == 0)
        def _():
            loss_ref[...] = jnp.zeros_like(loss_ref)
            dg_ref[...] = jnp.zeros_like(dg_ref)
            dbeta_ref[...] = jnp.zeros_like(dbeta_ref)

        loss_ref[...] += jnp.broadcast_to(part, loss_ref.shape)
        dg_ref[...] += dg
        dbeta_ref[...] += db

    return pl.pallas_call(
        body, name=name, grid=(t // LN_ROWS,),
        out_shape=(jax.ShapeDtypeStruct((1, 128), F32), jax.ShapeDtypeStruct((t, d), BF16),
                   jax.ShapeDtypeStruct((t, d), F32), jax.ShapeDtypeStruct((1, d), F32),
                   jax.ShapeDtypeStruct((1, d), F32)),
        in_specs=[row, row, row, vec, vec], out_specs=(lvec, row, row, vec, vec),
        compiler_params=_cp("arbitrary"),
    )(h, sub, target, g, b)


HG_ROWS = 2048
HG_LEVELS = 7


def _hg_constants(with_total):
    c = HG_CHUNK
    t = np.arange(c)[:, None]
    j = np.arange(c)[None, :]
    blocks, masks = [], [np.eye(c)]
    for lev in range(HG_LEVELS):
        m = 1 << lev
        second = (t % (2 * m)) >= m
        mid = (t // (2 * m)) * (2 * m) + m - 1
        blocks.append((second & (j > mid) & (j <= t)) | ((~second) & (j > t) & (j <= mid)))
        same = (t // (2 * m)) == (j // (2 * m))
        masks.append(same & second & ((j % (2 * m)) < m))
    blocks += [j <= t, j > t] + ([np.ones((c, c), bool)] if with_total else [])
    cm = np.concatenate(blocks, axis=0).astype(np.float32)
    mk = np.concatenate(masks, axis=0).astype(np.float32)
    return jnp.asarray(cm, dtype=BF16), jnp.asarray(mk)


def _dot_01(cm, x, mode):
    hi = x.astype(BF16)
    lo = (x - hi.astype(F32)).astype(BF16)
    both = lax.dot_general(cm, jnp.concatenate([hi, lo], axis=1), _DIMS[mode], preferred_element_type=F32)
    n = x.shape[-1]
    return both[:, :n] + both[:, n:]


@jax.custom_vjp
def _prefix(cm, lf):
    return _dot_01(cm, lf, "nn")


def _prefix_fwd(cm, lf):
    return _prefix(cm, lf), cm


def _prefix_bwd(cm, d):
    return jnp.zeros_like(cm), _dot_01(cm, d, "tn")


_prefix.defvjp(_prefix_fwd, _prefix_bwd)

_BWD = {"nn": (("nt", 0, 1), ("tn", 1, 0)), "nt": (("nn", 0, 1), ("tn", 0, 1)), "tn": (("nt", 1, 0), ("nn", 1, 0))}


def _bdot_raw(a, b, mode):
    return lax.dot_general(a.astype(BF16), b.astype(BF16), _DIMS[mode], preferred_element_type=F32)


@functools.partial(jax.custom_vjp, nondiff_argnums=(2,))
def _bdot(a, b, mode):
    return _bdot_raw(a, b, mode)


def _bdot_fwd(a, b, mode):
    return _bdot_raw(a, b, mode), (a, b)


def _bdot_bwd(mode, res, d):
    a, b = res
    (ma, da_pos, _), (mb, db_pos, _) = _BWD[mode]
    da = _bdot_raw(d, b, ma) if da_pos == 0 else _bdot_raw(b, d, ma)
    db = _bdot_raw(d, a, mb) if db_pos == 0 else _bdot_raw(a, d, mb)
    return da, db


_bdot.defvjp(_bdot_fwd, _bdot_bwd)


def _hg_chunk(pq, pf, pi, pg, l0, l1, l2, ng, s_t, cm, mk, with_total):
    c = HG_CHUNK
    mx = jnp.maximum(jnp.maximum(l0, l1), l2)
    e0, e1, e2 = jnp.exp(l0 - mx), jnp.exp(l1 - mx), jnp.exp(l2 - mx)
    lb = e0 / (e0 + e1 + e2)
    q = pq * jax.nn.sigmoid(pq)
    t1 = jnp.log(lb)
    t2 = jnp.log1p(-lb) + jax.nn.log_sigmoid(pf)
    lf = jnp.maximum(t1, t2) + jnp.log1p(jnp.exp(-jnp.abs(t1 - t2)))
    k = (1.0 - lb) * jax.nn.sigmoid(-pf)
    x = jnp.exp(_prefix(cm, lf))
    scores = mk[0:c] * _bdot(q, k, "nt")
    for lev in range(HG_LEVELS if with_total else HG_LEVELS - 1):
        xl = x[lev * c:(lev + 1) * c]
        scores = scores + mk[(lev + 1) * c:(lev + 2) * c] * _bdot(q * xl, k * xl, "nt")
    if not with_total:
        xl = x[(HG_LEVELS - 1) * c:HG_LEVELS * c]
        h = c // 2
        quad = _bdot(q[h:] * xl[h:], k[:h] * xl[:h], "nt")
        zero = jnp.zeros((h, h), F32)
        scores = scores + jnp.concatenate([jnp.concatenate([zero, zero], axis=1),
                                           jnp.concatenate([quad, zero], axis=1)], axis=0)
    x_incl = x[HG_LEVELS * c:(HG_LEVELS + 1) * c]
    x_after = x[(HG_LEVELS + 1) * c:(HG_LEVELS + 2) * c]
    if with_total:
        x_total = jnp.concatenate([x[(HG_LEVELS + 2) * c:(HG_LEVELS + 3) * c]] * (HG_DIM // c), axis=0)
    else:
        x_total = jnp.exp(jnp.sum(lf, axis=0, keepdims=True))
    o = _bdot(scores, pi, "nn") + _bdot(q * x_incl, s_t, "nt")
    s_new = s_t * x_total + _bdot(pi, k * x_after, "tn")
    rstd = lax.rsqrt(jnp.mean(o * o, axis=-1, keepdims=True) + RMS_EPS)
    y = o * rstd * ng * (pg * jax.nn.sigmoid(pg))
    return y, s_new


def _hg_specs(t, reverse, with_total):
    nrb = t // HG_ROWS
    rb = (lambda r: nrb - 1 - r) if reverse else (lambda r: r)
    proj = pl.BlockSpec((4, HG_ROWS, HG_DIM), lambda h, r: (0, rb(r), h))
    vec = pl.BlockSpec((1, HG_DIM), lambda h, r: (0, h))
    cm = pl.BlockSpec(((HG_LEVELS + 2 + with_total) * HG_CHUNK, HG_CHUNK), lambda h, r: (0, 0))
    mk = pl.BlockSpec(((HG_LEVELS + 1) * HG_CHUNK, HG_CHUNK), lambda h, r: (0, 0))
    rows = pl.BlockSpec((HG_ROWS, HG_DIM), lambda h, r: (rb(r), h))
    states = pl.BlockSpec((None, HG_ROWS // HG_CHUNK, HG_DIM, HG_DIM), lambda h, r: (h, rb(r), 0, 0))
    return proj, vec, cm, mk, rows, states


def _hg_fwd(proj, l0, l1, l2, ng):
    t = proj.shape[1]
    n_in = HG_ROWS // HG_CHUNK
    cm, mk = _hg_constants(True)
    p_spec, vec, cm_spec, mk_spec, rows, st_spec = _hg_specs(t, False, True)

    def body(p_ref, l0_ref, l1_ref, l2_ref, ng_ref, cm_ref, mk_ref, y_ref, st_ref, s_ref):
        @pl.when(pl.program_id(1) == 0)
        def _():
            s_ref[...] = jnp.zeros_like(s_ref)

        def step(ci, carry):
            sl = pl.ds(pl.multiple_of(ci * HG_CHUNK, HG_CHUNK), HG_CHUNK)
            s_t = s_ref[...]
            st_ref[ci] = s_t
            y, s_new = _hg_chunk(p_ref[0, sl, :], p_ref[1, sl, :], p_ref[2, sl, :], p_ref[3, sl, :],
                                 l0_ref[...], l1_ref[...], l2_ref[...], ng_ref[...], s_t,
                                 cm_ref[...], mk_ref[...], True)
            y_ref[sl, :] = y.astype(BF16)
            s_ref[...] = s_new
            return carry

        lax.fori_loop(0, n_in, step, 0, unroll=True)

    return pl.pallas_call(
        body, name="hg_fwd", grid=(HG_HEADS, t // HG_ROWS),
        out_shape=(jax.ShapeDtypeStruct((t, D_MODEL), BF16),
                   jax.ShapeDtypeStruct((HG_HEADS, t // HG_CHUNK, HG_DIM, HG_DIM), F32)),
        in_specs=[p_spec, vec, vec, vec, vec, cm_spec, mk_spec], out_specs=(rows, st_spec),
        scratch_shapes=[pltpu.VMEM((HG_DIM, HG_DIM), F32)],
        compiler_params=_cp("parallel", "arbitrary"),
    )(proj, l0, l1, l2, ng, cm, mk)


def _hg_bwd(proj, states, dy, l0, l1, l2, ng):
    t = proj.shape[1]
    n_in = HG_ROWS // HG_CHUNK
    cm, mk = _hg_constants(False)
    p_spec, vec, cm_spec, mk_spec, rows, st_spec = _hg_specs(t, True, False)

    def body(p_ref, st_ref, dy_ref, l0_ref, l1_ref, l2_ref, ng_ref, cm_ref, mk_ref,
             dp_ref, dl0_ref, dl1_ref, dl2_ref, dng_ref, ds_ref):
        @pl.when(pl.program_id(1) == 0)
        def _():
            ds_ref[...] = jnp.zeros_like(ds_ref)
            for r in (dl0_ref, dl1_ref, dl2_ref, dng_ref):
                r[...] = jnp.zeros_like(r)

        def step(it, carry):
            ci = n_in - 1 - it
            sl = pl.ds(pl.multiple_of(ci * HG_CHUNK, HG_CHUNK), HG_CHUNK)
            fn = functools.partial(_hg_chunk, cm=cm_ref[...], mk=mk_ref[...], with_total=False)
            _, vjp = jax.vjp(fn, p_ref[0, sl, :], p_ref[1, sl, :], p_ref[2, sl, :], p_ref[3, sl, :],
                             l0_ref[...], l1_ref[...], l2_ref[...], ng_ref[...], st_ref[ci])
            dq, df, di, dg, d0, d1, d2, dn, ds = vjp((dy_ref[sl, :], ds_ref[...]))
            dp_ref[0, sl, :] = dq.astype(BF16)
            dp_ref[1, sl, :] = df.astype(BF16)
            dp_ref[2, sl, :] = di.astype(BF16)
            dp_ref[3, sl, :] = dg.astype(BF16)
            dl0_ref[...] += d0
            dl1_ref[...] += d1
            dl2_ref[...] += d2
            dng_ref[...] += dn
            ds_ref[...] = ds
            return carry

        lax.fori_loop(0, n_in, step, 0, unroll=True)

    v_shape = jax.ShapeDtypeStruct((1, D_MODEL), F32)
    return pl.pallas_call(
        body, name="hg_bwd", grid=(HG_HEADS, t // HG_ROWS),
        out_shape=(jax.ShapeDtypeStruct((4, t, D_MODEL), BF16), v_shape, v_shape, v_shape, v_shape),
        in_specs=[p_spec, st_spec, rows, vec, vec, vec, vec, cm_spec, mk_spec],
        out_specs=(p_spec, vec, vec, vec, vec),
        scratch_shapes=[pltpu.VMEM((HG_DIM, HG_DIM), F32)],
        compiler_params=_cp("parallel", "arbitrary"),
    )(proj, states, dy, l0, l1, l2, ng, cm, mk)


_SQRT_HALF = 0.7071067811865476
_INV_SQRT_2PI = 0.3989422804014327


def _gelu(x):
    return 0.5 * x * (1.0 + lax.erf(x * _SQRT_HALF))


def _gelu_grad(x):
    return 0.5 * (1.0 + lax.erf(x * _SQRT_HALF)) + x * (_INV_SQRT_2PI * jnp.exp(-0.5 * x * x))


def _tril(n):
    return (lax.broadcasted_iota(jnp.int32, (n, n), 0) >= lax.broadcasted_iota(jnp.int32, (n, n), 1)).astype(F32)


SG_STEP = 4


def _sg_specs(d):
    row = lambda w: pl.BlockSpec((SG_STEP * SG_CHUNK, w), lambda i: (i, 0))
    vec = pl.BlockSpec((1, d), lambda i: (0, 0))
    cube = pl.BlockSpec((SG_GROUPS, SG_CHUNK, SG_CHUNK), lambda i: (0, 0, 0))
    return row, vec, cube


def _sg_fwd(pre, ln_g, ln_b, w_s, bias):
    t = pre.shape[0]
    d = D_MODEL
    row, vec, cube = _sg_specs(d)

    def body(pre_ref, g_ref, b_ref, w_ref, bias_ref, y_ref, vln_ref):
        tril = _tril(SG_CHUNK)
        for st in range(SG_STEP):
            rs = slice(st * SG_CHUNK, (st + 1) * SG_CHUNK)
            u = _gelu(pre_ref[rs, :d])
            vhat, _ = _ln_stats(_gelu(pre_ref[rs, d:]))
            vln_ref[rs, :] = vhat * g_ref[...] + b_ref[...]
            for g in range(SG_GROUPS):
                cs = slice(g * SG_DIM, (g + 1) * SG_DIM)
                gate = _bdot_raw(w_ref[g] * tril, vln_ref[rs, cs], "nn") + bias_ref[g]
                y_ref[rs, cs] = (u[:, cs] * gate).astype(BF16)

    return pl.pallas_call(
        body, name="sg_fwd", grid=(t // (SG_STEP * SG_CHUNK),), out_shape=jax.ShapeDtypeStruct((t, d), BF16),
        in_specs=[row(2 * d), vec, vec, cube, cube], out_specs=row(d),
        scratch_shapes=[pltpu.VMEM((SG_STEP * SG_CHUNK, d), F32)], compiler_params=_cp("parallel"),
    )(pre, ln_g, ln_b, w_s, bias)


def _sg_bwd(pre, dy, ln_g, ln_b, w_s, bias):
    t = pre.shape[0]
    d = D_MODEL
    row, vec, cube = _sg_specs(d)
    dbs_spec = pl.BlockSpec((SG_GROUPS, 8, SG_CHUNK), lambda i: (0, 0, 0))

    def body(pre_ref, dy_ref, g_ref, b_ref, w_ref, bias_ref, dpre_ref, dw_ref, dbs_ref, dlg_ref, dlb_ref,
             vln_ref, dvln_ref):
        @pl.when(pl.program_id(0) == 0)
        def _():
            for r in (dw_ref, dbs_ref, dlg_ref, dlb_ref):
                r[...] = jnp.zeros_like(r)

        tril = _tril(SG_CHUNK)
        ones = jnp.ones((8, SG_DIM), F32)
        for st in range(SG_STEP):
            rs = slice(st * SG_CHUNK, (st + 1) * SG_CHUNK)
            pu = pre_ref[rs, :d]
            pv = pre_ref[rs, d:]
            u = _gelu(pu)
            vhat, rstd = _ln_stats(_gelu(pv))
            vln_ref[rs, :] = vhat * g_ref[...] + b_ref[...]
            for g in range(SG_GROUPS):
                cs = slice(g * SG_DIM, (g + 1) * SG_DIM)
                wc = w_ref[g] * tril
                vg = vln_ref[rs, cs]
                gate = _bdot_raw(wc, vg, "nn") + bias_ref[g]
                dyg = dy_ref[rs, cs]
                dgate = dyg * u[:, cs]
                dpre_ref[rs, cs] = (dyg * gate * _gelu_grad(pu[:, cs])).astype(BF16)
                dw_ref[g] += tril * _bdot_raw(dgate, vg, "nt")
                dbs_ref[g] += lax.dot_general(ones, dgate, _DIMS["nt"], precision=HI, preferred_element_type=F32)
                dvln_ref[rs, cs] = _bdot_raw(wc, dgate, "tn")
            dvln = dvln_ref[rs, :]
            dv, dg, db = _ln_bwd_math(vhat, rstd, dvln, g_ref[...])
            dlg_ref[...] += dg
            dlb_ref[...] += db
            dpre_ref[rs, d:] = (dv * _gelu_grad(pv)).astype(BF16)

    return pl.pallas_call(
        body, name="sg_bwd", grid=(t // (SG_STEP * SG_CHUNK),),
        out_shape=(jax.ShapeDtypeStruct((t, 2 * d), BF16), jax.ShapeDtypeStruct(w_s.shape, F32),
                   jax.ShapeDtypeStruct((SG_GROUPS, 8, SG_CHUNK), F32),
                   jax.ShapeDtypeStruct((1, d), F32), jax.ShapeDtypeStruct((1, d), F32)),
        in_specs=[row(2 * d), row(d), vec, vec, cube, cube],
        out_specs=(row(2 * d), cube, dbs_spec, vec, vec),
        scratch_shapes=[pltpu.VMEM((SG_STEP * SG_CHUNK, d), F32), pltpu.VMEM((SG_STEP * SG_CHUNK, d), F32)],
        compiler_params=_cp("arbitrary"),
    )(pre, dy, ln_g, ln_b, w_s, bias)


FFN_ROWS = 1024
FFN_COLS = 1408
HALO = 16


def _ffn_conv(a_ext, w_ref, cb_ref):
    a1 = pltpu.roll(a_ext, 1, 0)
    a2 = pltpu.roll(a_ext, 2, 0)
    return w_ref[0:1, :] * a2 + w_ref[1:2, :] * a1 + w_ref[2:3, :] * a_ext + cb_ref[...], a1, a2


def _ffn_fwd(hh, conv_w, conv_b):
    t = hh.shape[1]
    nb8 = FFN_ROWS // HALO
    main = pl.BlockSpec((2, FFN_ROWS, FFN_COLS), lambda c, r: (0, r, c))
    prev = pl.BlockSpec((None, HALO, FFN_COLS), lambda c, r: (0, jnp.maximum(r * nb8 - 1, 0), c))
    wspec = pl.BlockSpec((3, FFN_COLS), lambda c, r: (0, c))
    bspec = pl.BlockSpec((1, FFN_COLS), lambda c, r: (0, c))

    def body(m_ref, p_ref, w_ref, cb_ref, z_ref):
        prev = jnp.where(pl.program_id(1) == 0, 0.0, p_ref[...].astype(F32))
        a_ext = jnp.concatenate([prev, m_ref[0].astype(F32)], axis=0)
        a, _, _ = _ffn_conv(a_ext, w_ref, cb_ref)
        a = a[HALO:]
        z_ref[...] = (a * jax.nn.sigmoid(a) * m_ref[1].astype(F32)).astype(BF16)

    return pl.pallas_call(
        body, name="ffn_fwd", grid=(D_FF // FFN_COLS, t // FFN_ROWS),
        out_shape=jax.ShapeDtypeStruct((t, D_FF), BF16),
        in_specs=[main, prev, wspec, bspec],
        out_specs=pl.BlockSpec((FFN_ROWS, FFN_COLS), lambda c, r: (r, c)),
        compiler_params=_cp("parallel", "parallel"),
    )(hh, hh, conv_w, conv_b)


def _ffn_bwd(hh, dz, conv_w, conv_b):
    t = hh.shape[1]
    nb8 = FFN_ROWS // HALO
    last8 = t // HALO - 1
    nr = t // FFN_ROWS
    main = pl.BlockSpec((2, FFN_ROWS, FFN_COLS), lambda c, r: (0, r, c))
    prev = pl.BlockSpec((None, HALO, FFN_COLS), lambda c, r: (0, jnp.maximum(r * nb8 - 1, 0), c))
    nxt = pl.BlockSpec((2, HALO, FFN_COLS), lambda c, r: (0, jnp.minimum((r + 1) * nb8, last8), c))
    dmain = pl.BlockSpec((FFN_ROWS, FFN_COLS), lambda c, r: (r, c))
    dnxt = pl.BlockSpec((HALO, FFN_COLS), lambda c, r: (jnp.minimum((r + 1) * nb8, last8), c))
    wspec = pl.BlockSpec((3, FFN_COLS), lambda c, r: (0, c))
    bspec = pl.BlockSpec((1, FFN_COLS), lambda c, r: (0, c))

    def body(m_ref, p_ref, n_ref, dz_ref, dzn_ref, w_ref, cb_ref, dh_ref, dw_ref, dcb_ref):
        r = pl.program_id(1)

        @pl.when(r == 0)
        def _():
            dw_ref[...] = jnp.zeros_like(dw_ref)
            dcb_ref[...] = jnp.zeros_like(dcb_ref)

        prev = jnp.where(r == 0, 0.0, p_ref[...].astype(F32))
        a_ext = jnp.concatenate([prev, m_ref[0].astype(F32), n_ref[0].astype(F32)], axis=0)
        a, a1, a2 = _ffn_conv(a_ext, w_ref, cb_ref)
        a = a[HALO:]
        b_ext = jnp.concatenate([m_ref[1], n_ref[1]], axis=0).astype(F32)
        dz_main = dz_ref[...].astype(F32)
        dz_ext = jnp.concatenate([dz_main, jnp.where(r == nr - 1, 0.0, dzn_ref[...].astype(F32))], axis=0)
        sig = jax.nn.sigmoid(a)
        da = dz_ext * b_ext * (sig * (1.0 + a * (1.0 - sig)))
        n_ext = FFN_ROWS + HALO
        da_p1 = pltpu.roll(da, n_ext - 1, 0)
        da_p2 = pltpu.roll(da, n_ext - 2, 0)
        da_raw = w_ref[2:3, :] * da + w_ref[1:2, :] * da_p1 + w_ref[0:1, :] * da_p2
        dh_ref[0] = da_raw[:FFN_ROWS].astype(BF16)
        dh_ref[1] = (dz_main * (a * sig)[:FFN_ROWS]).astype(BF16)
        dam = da[:FFN_ROWS]
        rows = slice(HALO, HALO + FFN_ROWS)
        dw_ref[0:1, :] += jnp.sum(dam * a2[rows], axis=0, keepdims=True)
        dw_ref[1:2, :] += jnp.sum(dam * a1[rows], axis=0, keepdims=True)
        dw_ref[2:3, :] += jnp.sum(dam * a_ext[rows], axis=0, keepdims=True)
        dcb_ref[...] += jnp.sum(dam, axis=0, keepdims=True)

    return pl.pallas_call(
        body, name="ffn_bwd", grid=(D_FF // FFN_COLS, nr),
        out_shape=(jax.ShapeDtypeStruct((2, t, D_FF), BF16), jax.ShapeDtypeStruct((3, D_FF), F32),
                   jax.ShapeDtypeStruct((1, D_FF), F32)),
        in_specs=[main, prev, nxt, dmain, dnxt, wspec, bspec],
        out_specs=(main, wspec, bspec), compiler_params=_cp("parallel", "arbitrary"),
    )(hh, hh, hh, dz, dz, conv_w, conv_b)


def _adamw_math(w, g, m, v):
    m = ADAM_B1 * m + (1.0 - ADAM_B1) * g
    v = ADAM_B2 * v + (1.0 - ADAM_B2) * (g * g)
    m_hat = m / (1.0 - ADAM_B1 ** ADAM_STEP)
    v_hat = v / (1.0 - ADAM_B2 ** ADAM_STEP)
    delta = -ADAM_LR * (m_hat / (jnp.sqrt(v_hat) + ADAM_EPS) + ADAM_WD * w)
    return delta, m, v


def _as2d(shape):
    n = int(np.prod(shape))
    c = shape[-1] if shape[-1] % 128 == 0 else (128 if n % 128 == 0 else shape[-1])
    return n // c, c


def _adamw(w, g, m, v, name, parts=None, own=None, is_me=None, layer=None, prev=None):
    shape = w.shape if layer is None else w.shape[1:]
    r, c = _as2d(shape)
    br = r if r <= 512 else _pick(r, (256, 176, 128, 64, 8))
    blk = pl.BlockSpec((br, c), lambda i: (i, 0))
    if layer is None:
        wblk, o = blk, jax.ShapeDtypeStruct((r, c), F32)
        w2, m2, v2 = (a.reshape(r, c) for a in (w, m, v))
    else:
        wblk, o = pl.BlockSpec((None, br, c), lambda i: (layer, i, 0)), jax.ShapeDtypeStruct((w.shape[0], r, c), F32)
        w2, m2, v2 = (a.reshape(w.shape[0], r, c) for a in (w, m, v))
    extra, extra_specs = [], []
    if parts is not None:
        g_in = parts.reshape(N_DEV, r, c)
        g_spec = pl.BlockSpec((N_DEV, br, c), lambda i: (0, i, 0))
        extra = [own.reshape(r, c), is_me]
        extra_specs = [blk, pl.BlockSpec(memory_space=pltpu.SMEM)]
    else:
        g_in = g.reshape(r, c)
        g_spec = blk
    aliases = {}
    if prev is not None:
        aliases = {4 + len(extra) + j: j for j in range(4)}
        extra = extra + [a.reshape(o.shape) for a in prev]
        extra_specs = extra_specs + [ANY] * 4

    def body(w_ref, g_ref, m_ref, v_ref, *rest):
        go_ref, d_ref, mo_ref, vo_ref = rest[-4:]
        if parts is not None:
            own_ref, me_ref = rest[:2]
            gsum = None
            for i in range(N_DEV):
                term = jnp.where(me_ref[i] > 0.5, own_ref[...], g_ref[i]).astype(F32)
                gsum = term if gsum is None else gsum + term
        else:
            gsum = g_ref[...]
        delta, m_new, v_new = _adamw_math(w_ref[...], gsum, m_ref[...], v_ref[...])
        go_ref[...] = gsum
        d_ref[...] = delta
        mo_ref[...] = m_new
        vo_ref[...] = v_new

    outs = pl.pallas_call(
        body, name=name, grid=(r // br,), out_shape=(o, o, o, o),
        in_specs=[wblk, g_spec, wblk, wblk] + extra_specs, out_specs=(wblk, wblk, wblk, wblk),
        input_output_aliases=aliases, compiler_params=_cp("parallel"),
    )(w2, g_in, m2, v2, *extra)
    return tuple(a.reshape(w.shape) for a in outs)


ALL_PEERS = tuple(range(1, N_DEV))
OTHER_CHIPS = (2, 4, 6)
SIBLING = (1,)


def _me_and_peers(relations=ALL_PEERS):
    x, y, c = lax.axis_index("x"), lax.axis_index("y"), lax.axis_index("c")
    me = 4 * x + 2 * y + c
    peers = []
    for k in relations:
        kx, ky, kc = (k >> 2) & 1, (k >> 1) & 1, k & 1
        px, py, pc = x ^ kx, y ^ ky, c ^ kc
        peers.append(((px, py, pc), 4 * px + 2 * py + pc))
    return me, peers


def _all_gather(shard, name):
    def body(x_ref, o_ref, send_sems, recv_sems, local_sem):
        me, peers = _me_and_peers()
        mine = pltpu.make_async_copy(x_ref, o_ref.at[me], local_sem)
        mine.start()
        sends = []
        for k, (dev, _) in enumerate(peers):
            cp = pltpu.make_async_remote_copy(src_ref=x_ref, dst_ref=o_ref.at[me], send_sem=send_sems.at[k],
                                              recv_sem=recv_sems.at[k], device_id=dev, device_id_type=MESH)
            cp.start()
            sends.append(cp)
        for k, (dev, idx) in enumerate(peers):
            pltpu.make_async_remote_copy(src_ref=x_ref, dst_ref=o_ref.at[idx], send_sem=send_sems.at[k],
                                         recv_sem=recv_sems.at[k], device_id=dev, device_id_type=MESH).wait_recv()
        for cp in sends:
            cp.wait_send()
        mine.wait()

    return pl.pallas_call(
        body, name=name, out_shape=jax.ShapeDtypeStruct((N_DEV,) + shard.shape, shard.dtype),
        in_specs=[ANY], out_specs=ANY,
        scratch_shapes=[pltpu.SemaphoreType.DMA((N_DEV - 1,)), pltpu.SemaphoreType.DMA((N_DEV - 1,)),
                        pltpu.SemaphoreType.DMA],
        compiler_params=pltpu.CompilerParams(has_side_effects=True),
    )(shard)


HBM = pl.BlockSpec(memory_space=pltpu.HBM)
SEM = pl.BlockSpec(memory_space=pltpu.SEMAPHORE)
EFFECT = pltpu.SideEffectType.DATAFLOW_SIDE_EFFECTING


def _xchg_start(parts, land, *, dep, name, peers=ALL_PEERS, pieces=(0,)):
    n_buf = 1 if parts is None else 2

    def body(*refs):
        land_ref = refs[n_buf - 1]
        send_sem, recv_sem = refs[n_buf + 1], refs[n_buf + 2]
        token = refs[2 * n_buf + 3]
        me, to = _me_and_peers(peers)
        for dev, idx in to:
            if parts is None:
                for r in pieces:
                    piece = land_ref.at[me ^ r]
                    pltpu.make_async_remote_copy(src_ref=piece, dst_ref=piece, send_sem=send_sem, recv_sem=recv_sem,
                                                 device_id=dev, device_id_type=MESH).start()
            else:
                pltpu.make_async_remote_copy(src_ref=refs[0].at[idx], dst_ref=land_ref.at[me], send_sem=send_sem,
                                             recv_sem=recv_sem, device_id=dev, device_id_type=MESH).start()
        token[...] = jnp.zeros_like(token)

    bufs = [land] if parts is None else [parts, land]
    return pl.pallas_call(
        body, name=name,
        out_shape=(pltpu.SemaphoreType.DMA(()), pltpu.SemaphoreType.DMA(()),
                   *[pltpu.HBM(b.shape, b.dtype) for b in bufs], jax.ShapeDtypeStruct((8, 128), F32)),
        in_specs=(*[HBM] * n_buf, ANY), out_specs=(SEM, SEM, *[HBM] * n_buf, pl.BlockSpec(memory_space=pltpu.VMEM)),
        input_output_aliases={i: 2 + i for i in range(n_buf)},
        compiler_params=pltpu.CompilerParams(has_side_effects=EFFECT),
    )(*[pltpu.with_memory_space_constraint(b, pltpu.HBM) for b in bufs], dep)


def _xchg_wait(handles, after, name, n_pieces=N_DEV - 1):
    send_sem, recv_sem, *bufs, _ = handles
    n_buf = len(bufs)

    def body(*refs):
        land_ref = refs[n_buf - 1]
        send_sem, recv_sem = refs[n_buf], refs[n_buf + 1]
        x, y, c = lax.axis_index("x"), lax.axis_index("y"), lax.axis_index("c")
        span = land_ref.at[pl.ds(0, n_pieces)]
        cp = pltpu.make_async_remote_copy(src_ref=span, dst_ref=span, send_sem=send_sem, recv_sem=recv_sem,
                                          device_id=(x, y, c), device_id_type=MESH)
        cp.wait_send()
        cp.wait_recv()

    return pl.pallas_call(
        body, name=name, out_shape=tuple(pltpu.HBM(b.shape, b.dtype) for b in bufs),
        in_specs=(*[HBM] * n_buf, SEM, SEM, ANY), out_specs=tuple([HBM] * n_buf),
        input_output_aliases={i: i for i in range(n_buf)},
        compiler_params=pltpu.CompilerParams(has_side_effects=EFFECT),
    )(*bufs, send_sem, recv_sem, after)


def _sum_devices(v, name):
    _, r, c = v.shape

    def body(x_ref, o_ref):
        acc = x_ref[0]
        for i in range(1, N_DEV):
            acc = acc + x_ref[i]
        o_ref[...] = acc

    return pl.pallas_call(
        body, name=name, out_shape=jax.ShapeDtypeStruct((r, c), F32),
        in_specs=[pl.BlockSpec(memory_space=pltpu.VMEM)], out_specs=pl.BlockSpec(memory_space=pltpu.VMEM),
        compiler_params=pltpu.CompilerParams(vmem_limit_bytes=VMEM_LIMIT),
    )(v)


def _local_step(x, target, lb_logits, p, weight, send_grad, send_small):
    g = {}
    l0, l1, l2 = lb_logits[0:1], lb_logits[1:2], lb_logits[2:3]
    x_bf = x.astype(BF16)

    proj0 = _mm(x_bf, weight("hg_w_in", x_bf), mode="nn", b_cs=True, o_cs=4, name="mm_hg_in")
    y0, states = _hg_fwd(proj0, l0, l1, l2, p["hg_norm_g"])
    mixed0 = _mm(y0, weight("hg_w_out", y0), mode="nn", name="mm_hg_out")
    hh_l, z_l = [], []

    def ffn_forward(layer, h_in_bf):
        hh = _mm(h_in_bf, weight(f"ffn_w_up{layer}", h_in_bf), mode="nn", b_cs=True, o_cs=2, out_dtype=BF16,
                 name=f"mm_up{layer}")
        z = _ffn_fwd(hh, p["ffn_conv_w"][layer], p["ffn_conv_b"][layer])
        ffn = _mm(z, weight(f"ffn_w_down{layer}", z), mode="nn", name=f"mm_down{layer}")
        hh_l.append(hh)
        z_l.append(z)
        return ffn

    h1, h1b = _ln_fwd(x, mixed0, p["ln1_g"][0], p["ln1_b"][0], "ln1_0")
    ffn0 = ffn_forward(0, h1b)
    h2, h2b = _ln_fwd(h1, ffn0, p["ln2_g"][0], p["ln2_b"][0], "ln2_0")
    pre1 = _mm(h2b, weight("sg_w_in", h2b), mode="nn", b_cs=True, name="mm_sg_in")
    y1 = _sg_fwd(pre1, p["sg_ln_g"], p["sg_ln_b"], p["sg_w_s"], p["sg_bias"])
    mixed1 = _mm(y1, weight("sg_w_out", y1), mode="nn", name="mm_sg_out")
    h3, h3b = _ln_fwd(h2, mixed1, p["ln1_g"][1], p["ln1_b"][1], "ln1_1")
    ffn1 = ffn_forward(1, h3b)

    loss, ds, dres, g["ln2_g1"], g["ln2_b1"] = _ln_loss_bwd(h3, ffn1, target, p["ln2_g"][1], p["ln2_b"][1], "ln2_1_loss")

    def ffn_backward(layer, ds_bf, h_in_bf, dres):
        tok = send_grad(f"ffn_w_down{layer}",
                        _mm(z_l[layer], ds_bf, mode="tn", out_dtype=BF16, name=f"mm_dw_down{layer}"))
        dz = _mm(ds_bf, weight(f"ffn_w_down{layer}", None), mode="nt", out_dtype=BF16, name=f"mm_d_z{layer}",
                 deps=(tok,))
        dhh, g[f"conv_w{layer}"], g[f"conv_b{layer}"] = _ffn_bwd(hh_l[layer], dz, p["ffn_conv_w"][layer],
                                                                   p["ffn_conv_b"][layer])
        tok = send_grad(f"ffn_w_up{layer}", _mm(h_in_bf, dhh, mode="tn", b_cs=True, o_cs=N_DEV, out_dtype=BF16,
                                                name=f"mm_dw_up{layer}"))
        return _mm(dhh, weight(f"ffn_w_up{layer}", None), mode="nt", a_cs=True, b_cs=True, addend=dres,
                   name=f"mm_d_up{layer}", deps=(tok,))

    dh3 = ffn_backward(1, ds, h3b, dres)
    ds, dres, g["ln1_g1"], g["ln1_b1"] = _ln_bwd(h2, mixed1, dh3, p["ln1_g"][1], "ln1_1_bwd")
    tok = send_grad("sg_w_out", _mm(y1, ds, mode="tn", out_dtype=BF16, name="mm_dw_sg_out"))
    dy1 = _mm(ds, weight("sg_w_out", None), mode="nt", name="mm_d_sg_out", deps=(tok,))
    dpre1, g["sg_w_s"], dbs, g["sg_ln_g"], g["sg_ln_b"] = _sg_bwd(pre1, dy1, p["sg_ln_g"], p["sg_ln_b"],
                                                                    p["sg_w_s"], p["sg_bias"])
    g["sg_b_s"] = dbs[:, 0, :]
    tok = send_grad("sg_w_in", _mm(h2b, dpre1, mode="tn", o_cs=N_DEV, out_dtype=BF16, name="mm_dw_sg_in"))
    dh2 = _mm(dpre1, weight("sg_w_in", None), mode="nt", b_cs=True, addend=dres, name="mm_d_sg_in", deps=(tok,))
    ds, dres, g["ln2_g0"], g["ln2_b0"] = _ln_bwd(h1, ffn0, dh2, p["ln2_g"][0], "ln2_0_bwd")
    dh1 = ffn_backward(0, ds, h1b, dres)
    ds, dres, g["ln1_g0"], g["ln1_b0"] = _ln_bwd(x, mixed0, dh1, p["ln1_g"][0], "ln1_0_bwd")
    tok = send_grad("hg_w_out", _mm(y0, ds, mode="tn", out_dtype=BF16, name="mm_dw_hg_out"))
    dy0 = _mm(ds, weight("hg_w_out", None), mode="nt", name="mm_d_hg_out", deps=(tok,))
    dproj, d0, d1, d2, g["hg_norm_g"] = _hg_bwd(proj0, states, dy0, l0, l1, l2, p["hg_norm_g"])
    g["lb_logits"] = jnp.concatenate([d0, d1, d2], axis=0)
    tok = send_small(g)
    tok = send_grad("hg_w_in", _mm(x_bf, dproj, mode="tn", b_cs=True, o_cs=N_DEV, out_dtype=BF16, name="mm_dw_hg_in",
                                   deps=(tok,)))
    grad_x = _mm(dproj, weight("hg_w_in", None), mode="nt", a_cs=True, b_cs=True, addend=dres, name="mm_d_hg_in",
                 deps=(tok,))
    return loss[0, 0], grad_x


_SMALL = ("lb_logits", "hg_norm_g", "sg_ln_g", "sg_ln_b", "sg_w_s", "sg_b_s", "ffn_conv_w", "ffn_conv_b",
          "ln1_g", "ln1_b", "ln2_g", "ln2_b")
_NAMES = ("lb_logits", "hg_w_in", "hg_norm_g", "hg_w_out", "sg_w_in", "sg_ln_g", "sg_ln_b", "sg_w_s", "sg_b_s",
          "sg_w_out", "ffn_w_up", "ffn_conv_w", "ffn_conv_b", "ffn_w_down", "ln1_g", "ln1_b", "ln2_g", "ln2_b")


def kernel(x, lb_logits, hg_w_in, hg_norm_g, hg_w_out, sg_w_in, sg_ln_g, sg_ln_b, sg_w_s, sg_b_s, sg_w_out, ffn_w_up, ffn_conv_w, ffn_conv_b, ffn_w_down, ln1_g, ln1_b, ln2_g, ln2_b, loss_target, m_lb_logits, m_hg_w_in, m_hg_norm_g, m_hg_w_out, m_sg_w_in, m_sg_ln_g, m_sg_ln_b, m_sg_w_s, m_sg_b_s, m_sg_w_out, m_ffn_w_up, m_ffn_conv_w, m_ffn_conv_b, m_ffn_w_down, m_ln1_g, m_ln1_b, m_ln2_g, m_ln2_b, v_lb_logits, v_hg_w_in, v_hg_norm_g, v_hg_w_out, v_sg_w_in, v_sg_ln_g, v_sg_ln_b, v_sg_w_s, v_sg_b_s, v_sg_w_out, v_ffn_w_up, v_ffn_conv_w, v_ffn_conv_b, v_ffn_w_down, v_ln1_g, v_ln1_b, v_ln2_g, v_ln2_b):
    w = dict(lb_logits=lb_logits, hg_w_in=hg_w_in, hg_norm_g=hg_norm_g, hg_w_out=hg_w_out, sg_w_in=sg_w_in,
             sg_ln_g=sg_ln_g, sg_ln_b=sg_ln_b, sg_w_s=sg_w_s, sg_b_s=sg_b_s, sg_w_out=sg_w_out, ffn_w_up=ffn_w_up,
             ffn_conv_w=ffn_conv_w, ffn_conv_b=ffn_conv_b, ffn_w_down=ffn_w_down, ln1_g=ln1_g, ln1_b=ln1_b,
             ln2_g=ln2_g, ln2_b=ln2_b)
    m = dict(lb_logits=m_lb_logits, hg_w_in=m_hg_w_in, hg_norm_g=m_hg_norm_g, hg_w_out=m_hg_w_out,
             sg_w_in=m_sg_w_in, sg_ln_g=m_sg_ln_g, sg_ln_b=m_sg_ln_b, sg_w_s=m_sg_w_s, sg_b_s=m_sg_b_s,
             sg_w_out=m_sg_w_out, ffn_w_up=m_ffn_w_up, ffn_conv_w=m_ffn_conv_w, ffn_conv_b=m_ffn_conv_b,
             ffn_w_down=m_ffn_w_down, ln1_g=m_ln1_g, ln1_b=m_ln1_b, ln2_g=m_ln2_g, ln2_b=m_ln2_b)
    v = dict(lb_logits=v_lb_logits, hg_w_in=v_hg_w_in, hg_norm_g=v_hg_norm_g, hg_w_out=v_hg_w_out,
             sg_w_in=v_sg_w_in, sg_ln_g=v_sg_ln_g, sg_ln_b=v_sg_ln_b, sg_w_s=v_sg_w_s, sg_b_s=v_sg_b_s,
             sg_w_out=v_sg_w_out, ffn_w_up=v_ffn_w_up, ffn_conv_w=v_ffn_conv_w, ffn_conv_b=v_ffn_conv_b,
             ffn_w_down=v_ffn_w_down, ln1_g=v_ln1_g, ln1_b=v_ln1_b, ln2_g=v_ln2_g, ln2_b=v_ln2_b)
    me = 4 * lax.axis_index("x") + 2 * lax.axis_index("y") + lax.axis_index("c")
    is_me = (jnp.arange(N_DEV) == me).astype(F32)
    d = D_MODEL

    shards = [("hg_w_in", hg_w_in[0]), ("hg_w_out", hg_w_out[0]), ("ffn_w_up0", ffn_w_up[0]),
              ("ffn_w_down0", ffn_w_down[0]), ("sg_w_in", sg_w_in[0]), ("sg_w_out", sg_w_out[0]),
              ("ffn_w_up1", ffn_w_up[1]), ("ffn_w_down1", ffn_w_down[1])]
    row_sharded = {"hg_w_out": (d, d), "sg_w_out": (d, d), "ffn_w_down0": (D_FF, d), "ffn_w_down1": (D_FF, d)}
    sv = jnp.zeros((8, 768), F32)
    sv = sv.at[0, :256].set(sg_ln_g[0]).at[1, :256].set(sg_ln_b[0]).at[2:8, :704].set(ffn_conv_w.reshape(6, 704))
    sv = _all_gather(sv, "ag_small")
    gathers, gathered, dep = {}, {}, sv
    for n, shard in shards:
        land = lax.dynamic_update_index_in_dim(lax.empty((N_DEV,) + shard.shape, BF16), shard.astype(BF16), me, 0)
        gathers[n] = _xchg_start(None, land, dep=dep, name=f"ag_{n}", peers=OTHER_CHIPS)
        dep = gathers[n][-1]
    last_start = dep
    handed = {}
    order = [n for n, _ in shards]

    def hand_over(n, after):
        (land,) = _xchg_wait(gathers[n], after, f"agw_{n}", n_pieces=len(OTHER_CHIPS))
        handed[n] = _xchg_start(None, land, dep=is_me, name=f"ah_{n}", peers=SIBLING, pieces=(0,) + OTHER_CHIPS)

    def weight(n, after):
        if n not in gathered:
            if n == order[0]:
                hand_over(n, last_start)
                after = handed[n][-1]
            nxt = order.index(n) + 1
            if nxt < len(order):
                hand_over(order[nxt], after)
                after = handed[order[nxt]][-1]
            (full,) = _xchg_wait(handed[n], after, f"ahw_{n}", n_pieces=1 + len(OTHER_CHIPS))
            gathered[n] = full.reshape(row_sharded[n]) if n in row_sharded else full
        return gathered[n]

    grad_sends = {}

    def send_grad(n, parts):
        shape = (N_DEV,) + dict(shards)[n].shape
        grad_sends[n] = _xchg_start(parts.reshape(shape), lax.empty(shape, parts.dtype), dep=is_me, name=f"rs_{n}")
        return grad_sends[n][-1]

    small_send = {}

    def send_small(g):
        small = [g["lb_logits"], g["hg_norm_g"], g["sg_ln_g"], g["sg_ln_b"], g["sg_w_s"], g["sg_b_s"],
                 g["conv_w0"], g["conv_w1"], g["conv_b0"], g["conv_b1"], g["ln1_g0"], g["ln1_g1"],
                 g["ln1_b0"], g["ln1_b1"], g["ln2_g0"], g["ln2_g1"], g["ln2_b0"], g["ln2_b1"]]
        flat = jnp.concatenate([a.reshape(-1) for a in small])
        rows = -(-flat.shape[0] // (8 * 128)) * 8
        flat = jnp.pad(flat, (0, rows * 128 - flat.shape[0])).reshape(rows, 128)
        land = lax.dynamic_update_index_in_dim(lax.empty((N_DEV, rows, 128), F32), flat, me, 0)
        small_send["shapes"] = [a.shape for a in small]
        small_send["handles"] = _xchg_start(None, land, dep=is_me, name="ar_small")
        return small_send["handles"][-1]

    p = {}
    p["sg_ln_g"] = sv[:, 0, :256].reshape(1, d)
    p["sg_ln_b"] = sv[:, 1, :256].reshape(1, d)
    conv_w_full = jnp.transpose(sv[:, 2:8, :704].reshape(N_DEV, DEPTH, 3, 704), (1, 2, 0, 3)).reshape(DEPTH, 3, D_FF)
    p["ffn_conv_w"] = [conv_w_full[l] for l in range(DEPTH)]
    p["ffn_conv_b"] = [ffn_conv_b[l:l + 1] for l in range(DEPTH)]
    p["hg_norm_g"] = hg_norm_g
    p["sg_w_s"] = sg_w_s[0]
    p["sg_bias"] = jnp.broadcast_to(sg_b_s[0][:, :, None], (SG_GROUPS, SG_CHUNK, SG_DIM))
    for n in ("ln1_g", "ln1_b", "ln2_g", "ln2_b"):
        p[n] = [w[n][l:l + 1] for l in range(DEPTH)]

    loss_part, grad_x = _local_step(x[0], loss_target[0], lb_logits, p, weight, send_grad, send_small)
    loss = lax.psum(loss_part, ("x", "y", "c"))

    (landed,) = _xchg_wait(small_send["handles"], grad_x, "arw_small")
    red = _sum_devices(landed, "sum_small").reshape(-1)
    shapes = small_send["shapes"]
    offs = np.cumsum([0] + [int(np.prod(sh)) for sh in shapes])
    r = [red[offs[i]:offs[i + 1]].reshape(shapes[i]) for i in range(len(shapes))]
    gs = {}
    gs["lb_logits"] = r[0]
    gs["hg_norm_g"] = r[1]
    gs["sg_ln_g"] = lax.dynamic_slice(r[2], (0, me * 256), (1, 256))
    gs["sg_ln_b"] = lax.dynamic_slice(r[3], (0, me * 256), (1, 256))
    gs["sg_w_s"] = r[4][None]
    gs["sg_b_s"] = r[5][None]
    gs["ffn_conv_w"] = lax.dynamic_slice(jnp.stack([r[6], r[7]]), (0, 0, me * 704), (DEPTH, 3, 704))
    gs["ffn_conv_b"] = jnp.concatenate([r[8], r[9]], axis=0)
    gs["ln1_g"] = jnp.concatenate([r[10], r[11]], axis=0)
    gs["ln1_b"] = jnp.concatenate([r[12], r[13]], axis=0)
    gs["ln2_g"] = jnp.concatenate([r[14], r[15]], axis=0)
    gs["ln2_b"] = jnp.concatenate([r[16], r[17]], axis=0)

    out_g, out_d, out_m, out_v = {}, {}, {}, {}
    for n in _SMALL:
        out_g[n], out_d[n], out_m[n], out_v[n] = _adamw(w[n], gs[n], m[n], v[n], f"adamw_{n}")

    res, after = {}, out_v["ln2_b"]
    for n in ("ffn_w_down1", "ffn_w_up1", "sg_w_out", "sg_w_in", "ffn_w_down0", "ffn_w_up0", "hg_w_out", "hg_w_in"):
        parts, recv = _xchg_wait(grad_sends[n], after, f"rsw_{n}")
        own = lax.dynamic_index_in_dim(parts, me, 0, keepdims=False)
        if n[-1] in "01":
            base, layer = n[:-1], int(n[-1])
            res[n] = _adamw(w[base], None, m[base], v[base], f"adamw_{n}", parts=recv, own=own, is_me=is_me,
                            layer=layer, prev=res.get(base + "1"))
        else:
            res[n] = _adamw(w[n][0], None, m[n][0], v[n][0], f"adamw_{n}", parts=recv, own=own, is_me=is_me)
        after = res[n][3]
    for n in ("hg_w_in", "hg_w_out", "sg_w_in", "sg_w_out"):
        out_g[n], out_d[n], out_m[n], out_v[n] = (a[None] for a in res[n])
    for n in ("ffn_w_up", "ffn_w_down"):
        out_g[n], out_d[n], out_m[n], out_v[n] = res[n + "0"]

    return (loss, grad_x[None], *[out_g[n] for n in _NAMES], *[out_d[n] for n in _NAMES],
            *[out_m[n] for n in _NAMES], *[out_v[n] for n in _NAMES])
```

```python
import functools

import numpy as np
import jax
import jax.numpy as jnp
from jax import lax
from jax.experimental import pallas as pl
from jax.experimental.pallas import tpu as pltpu

F32 = jnp.float32
BF16 = jnp.bfloat16
HI = lax.Precision.HIGHEST

N_DEV = 8
D_MODEL = 2048
HG_HEADS = 16
HG_DIM = 128
HG_CHUNK = 128
SG_GROUPS = 16
SG_DIM = 128
SG_CHUNK = 128
D_FF = 5632
DEPTH = 2
ALPHA = (2 * DEPTH) ** 0.25
LN_EPS = 1e-5
RMS_EPS = 1e-6
ADAM_LR = 0.001
ADAM_B1 = 0.9
ADAM_B2 = 0.999
ADAM_EPS = 1e-08
ADAM_WD = 0.01
ADAM_STEP = 10

VMEM_LIMIT = 56 * 1024 * 1024
MESH = pl.DeviceIdType.MESH
ANY = pl.BlockSpec(memory_space=pl.ANY)


def _cp(*sem):
    return pltpu.CompilerParams(dimension_semantics=sem, vmem_limit_bytes=VMEM_LIMIT)


def _pick(n, cands):
    for c in cands:
        if n % c == 0:
            return c
    raise ValueError(f"no tile for {n} in {cands}")


_DIMS = {"nn": (((1,), (0,)), ((), ())), "nt": (((1,), (1,)), ((), ())), "tn": (((0,), (0,)), ((), ()))}


MM_VMEM_BUDGET = 44 * 1024 * 1024


def _mm_tiles(m, n, kk, n_div, k_div, out_bytes, has_addend, k_piece, transposed_a):
    best = None
    for bm in (1024, 1408, 512, 256):
        if m % bm:
            continue
        for bn in (1024, 1408, 512, 256, 128):
            if n % bn or any(d % bn for d in n_div):
                continue
            for bk in (kk, 5632, 4096, 2816, 2048, 1408, 1024, 512, 256, 128):
                if bk > kk or kk % bk or any(d % bk for d in k_div):
                    continue
                if k_piece and bk % k_piece:
                    continue
                nk = kk // bk
                vmem = 4 * (bm * bk + bk * bn) + 2 * bm * bn * out_bytes + (nk > 1) * 4 * bm * bn
                vmem += has_addend * 8 * bm * bn + transposed_a * 2 * bm * bk
                vmem += (out_bytes < 4) * 4 * bm * bn
                if vmem > MM_VMEM_BUDGET:
                    continue
                score = (max(nk, 2), -(bm * bn), -bk)
                if best is None or score < best[0]:
                    best = (score, bm, bn, bk)
    if best is None:
        raise ValueError(f"no tiles for {(m, n, kk)}")
    return best[1:]


def _mm(a, b, *, mode, name, out_dtype=F32, a_cs=False, b_cs=False, o_cs=None, addend=None, deps=()):
    if mode == "tn":
        kk, m = a.shape
    elif a_cs:
        m, kk = a.shape[1], a.shape[0] * a.shape[2]
    else:
        m, kk = a.shape
    if mode == "nt":
        n = b.shape[1] if b_cs else b.shape[0]
    else:
        n = b.shape[0] * b.shape[2] if b_cs else b.shape[1]
    a_c = a.shape[2] if a_cs else None
    b_c = b.shape[2] if b_cs else None
    o_c = n // o_cs if o_cs else None
    k_piece = b_c if (b_cs and mode == "nt") else None
    n_div = [c for c in (b_c if (b_cs and mode != "nt") else None, o_c) if c]
    k_div = [a_c] if a_c else []
    bm, bn, bk = _mm_tiles(m, n, kk, n_div, k_div, jnp.dtype(out_dtype).itemsize, addend is not None, k_piece,
                           mode == "tn")
    nk = kk // bk
    kb = bk // k_piece if k_piece else 1

    def cs_idx(blk, per):
        return (blk * per[0]) // per[1], (blk * per[0] % per[1]) // per[0]

    if mode == "tn":
        a_spec = pl.BlockSpec((bk, bm), lambda i, j, k: (k, i))
    elif a_cs:
        def a_map(i, j, k):
            s, r = cs_idx(k, (bk, a_c))
            return (s, i, r)
        a_spec = pl.BlockSpec((None, bm, bk), a_map)
    else:
        a_spec = pl.BlockSpec((bm, bk), lambda i, j, k: (i, k))
    if mode == "nt":
        if b_cs:
            b_spec = pl.BlockSpec((kb, bn, b_c), lambda i, j, k: (k, j, 0))
        else:
            b_spec = pl.BlockSpec((bn, bk), lambda i, j, k: (j, k))
    else:
        if b_cs:
            def b_map(i, j, k):
                s, r = cs_idx(j, (bn, b_c))
                return (s, k, r)
            b_spec = pl.BlockSpec((None, bk, bn), b_map)
        else:
            b_spec = pl.BlockSpec((bk, bn), lambda i, j, k: (k, j))
    if o_cs:
        def o_map(i, j, k):
            s, r = cs_idx(j, (bn, o_c))
            return (s, i, r)
        o_spec = pl.BlockSpec((None, bm, bn), o_map)
        out_shape = jax.ShapeDtypeStruct((o_cs, m, o_c), out_dtype)
    else:
        o_spec = pl.BlockSpec((bm, bn), lambda i, j, k: (i, j))
        out_shape = jax.ShapeDtypeStruct((m, n), out_dtype)
    in_specs = [a_spec, b_spec]
    args = [a, b]
    if addend is not None:
        in_specs.append(pl.BlockSpec((bm, bn), lambda i, j, k: (i, j)))
        args.append(addend)
    dims = _DIMS[mode]
    in_specs += [ANY] * len(deps)
    args += list(deps)
    n_in = len(args)

    def body(*refs):
        a_ref, b_ref = refs[0], refs[1]
        add_ref = refs[2] if addend is not None else None
        o_ref = refs[n_in]
        if k_piece:
            d = None
            for p in range(kb):
                dp = lax.dot_general(a_ref[:, p * b_c:(p + 1) * b_c].astype(BF16), b_ref[p].astype(BF16), dims,
                                     preferred_element_type=F32)
                d = dp if d is None else d + dp
        else:
            d = lax.dot_general(a_ref[...].astype(BF16), b_ref[...].astype(BF16), dims, preferred_element_type=F32)

        def finish(r):
            if addend is not None:
                r = r + add_ref[...]
            o_ref[...] = r.astype(o_ref.dtype)

        if nk == 1:
            finish(d)
            return
        acc_ref = refs[n_in + 1]
        k = pl.program_id(2)

        @pl.when(k == 0)
        def _():
            acc_ref[...] = d

        if nk > 2:
            @pl.when((k > 0) & (k < nk - 1))
            def _():
                acc_ref[...] += d

        @pl.when(k == nk - 1)
        def _():
            finish(acc_ref[...] + d)

    return pl.pallas_call(
        body, name=name, out_shape=out_shape, grid=(m // bm, n // bn, nk),
        in_specs=in_specs, out_specs=o_spec,
        scratch_shapes=[pltpu.VMEM((bm, bn), F32)] if nk > 1 else [],
        compiler_params=_cp("parallel", "parallel", "arbitrary"),
    )(*args)


LN_ROWS = 512


def _ln_stats(s):
    mu = jnp.mean(s, axis=-1, keepdims=True)
    sc = s - mu
    var = jnp.mean(sc * sc, axis=-1, keepdims=True)
    rstd = lax.rsqrt(var + LN_EPS)
    return sc * rstd, rstd


def _ln_fwd(h, sub, g, b, name):
    t, d = h.shape
    row = pl.BlockSpec((LN_ROWS, d), lambda i: (i, 0))
    vec = pl.BlockSpec((1, d), lambda i: (0, 0))

    def body(h_ref, s_ref, g_ref, b_ref, y_ref, yb_ref):
        xhat, _ = _ln_stats(ALPHA * h_ref[...] + s_ref[...])
        y = xhat * g_ref[...] + b_ref[...]
        y_ref[...] = y
        yb_ref[...] = y.astype(BF16)

    return pl.pallas_call(
        body, name=name, grid=(t // LN_ROWS,),
        out_shape=(jax.ShapeDtypeStruct((t, d), F32), jax.ShapeDtypeStruct((t, d), BF16)),
        in_specs=[row, row, vec, vec], out_specs=(row, row), compiler_params=_cp("parallel"),
    )(h, sub, g, b)


def _ln_bwd_math(xhat, rstd, dy, g):
    dxhat = dy * g
    m1 = jnp.mean(dxhat, axis=-1, keepdims=True)
    m2 = jnp.mean(dxhat * xhat, axis=-1, keepdims=True)
    ds = rstd * (dxhat - m1 - xhat * m2)
    dg = jnp.sum(dy * xhat, axis=0, keepdims=True)
    db = jnp.sum(dy, axis=0, keepdims=True)
    return ds, dg, db


def _ln_bwd(h, sub, dy, g, name):
    t, d = h.shape
    row = pl.BlockSpec((LN_ROWS, d), lambda i: (i, 0))
    vec = pl.BlockSpec((1, d), lambda i: (0, 0))

    def body(h_ref, s_ref, dy_ref, g_ref, ds_ref, dres_ref, dg_ref, dbeta_ref):
        xhat, rstd = _ln_stats(ALPHA * h_ref[...] + s_ref[...])
        ds, dg, db = _ln_bwd_math(xhat, rstd, dy_ref[...], g_ref[...])
        ds_ref[...] = ds.astype(BF16)
        dres_ref[...] = ALPHA * ds

        @pl.when(pl.program_id(0) == 0)
        def _():
            dg_ref[...] = jnp.zeros_like(dg_ref)
            dbeta_ref[...] = jnp.zeros_like(dbeta_ref)

        dg_ref[...] += dg
        dbeta_ref[...] += db

    return pl.pallas_call(
        body, name=name, grid=(t // LN_ROWS,),
        out_shape=(jax.ShapeDtypeStruct((t, d), BF16), jax.ShapeDtypeStruct((t, d), F32),
                   jax.ShapeDtypeStruct((1, d), F32), jax.ShapeDtypeStruct((1, d), F32)),
        in_specs=[row, row, row, vec], out_specs=(row, row, vec, vec),
        compiler_params=_cp("arbitrary"),
    )(h, sub, dy, g)


def _ln_loss_bwd(h, sub, target, g, b, name):
    t, d = h.shape
    row = pl.BlockSpec((LN_ROWS, d), lambda i: (i, 0))
    vec = pl.BlockSpec((1, d), lambda i: (0, 0))
    lvec = pl.BlockSpec((1, 128), lambda i: (0, 0))

    def body(h_ref, s_ref, t_ref, g_ref, b_ref, loss_ref, ds_ref, dres_ref, dg_ref, dbeta_ref):
        xhat, rstd = _ln_stats(ALPHA * h_ref[...] + s_ref[...])
        y = xhat * g_ref[...] + b_ref[...]
        err = y - t_ref[...]
        part = 0.5 * jnp.sum(jnp.mean(err * err, axis=-1, keepdims=True), axis=0, keepdims=True)
        ds, dg, db = _ln_bwd_math(xhat, rstd, err * (1.0 / d), g_ref[...])
        ds_ref[...] = ds.astype(BF16)
        dres_ref[...] = ALPHA * ds

        @pl.when(pl.program_id(0) == 0)
        def _():
            loss_ref[...] = jnp.zeros_like(loss_ref)
            dg_ref[...] = jnp.zeros_like(dg_ref)
            dbeta_ref[...] = jnp.zeros_like(dbeta_ref)

        loss_ref[...] += jnp.broadcast_to(part, loss_ref.shape)
        dg_ref[...] += dg
        dbeta_ref[...] += db

    return pl.pallas_call(
        body, name=name, grid=(t // LN_ROWS,),
        out_shape=(jax.ShapeDtypeStruct((1, 128), F32), jax.ShapeDtypeStruct((t, d), BF16),
                   jax.ShapeDtypeStruct((t, d), F32), jax.ShapeDtypeStruct((1, d), F32),
                   jax.ShapeDtypeStruct((1, d), F32)),
        in_specs=[row, row, row, vec, vec], out_specs=(lvec, row, row, vec, vec),
        compiler_params=_cp("arbitrary"),
    )(h, sub, target, g, b)


HG_ROWS = 2048
HG_LEVELS = 7


def _hg_constants(with_total):
    c = HG_CHUNK
    t = np.arange(c)[:, None]
    j = np.arange(c)[None, :]
    blocks, masks = [], [np.eye(c)]
    for lev in range(HG_LEVELS):
        m = 1 << lev
        second = (t % (2 * m)) >= m
        mid = (t // (2 * m)) * (2 * m) + m - 1
        blocks.append((second & (j > mid) & (j <= t)) | ((~second) & (j > t) & (j <= mid)))
        same = (t // (2 * m)) == (j // (2 * m))
        masks.append(same & second & ((j % (2 * m)) < m))
    blocks += [j <= t, j > t] + ([np.ones((c, c), bool)] if with_total else [])
    cm = np.concatenate(blocks, axis=0).astype(np.float32)
    mk = np.concatenate(masks, axis=0).astype(np.float32)
    return jnp.asarray(cm, dtype=BF16), jnp.asarray(mk)


def _dot_01(cm, x, mode):
    hi = x.astype(BF16)
    lo = (x - hi.astype(F32)).astype(BF16)
    both = lax.dot_general(cm, jnp.concatenate([hi, lo], axis=1), _DIMS[mode], preferred_element_type=F32)
    n = x.shape[-1]
    return both[:, :n] + both[:, n:]


@jax.custom_vjp
def _prefix(cm, lf):
    return _dot_01(cm, lf, "nn")


def _prefix_fwd(cm, lf):
    return _prefix(cm, lf), cm


def _prefix_bwd(cm, d):
    return jnp.zeros_like(cm), _dot_01(cm, d, "tn")


_prefix.defvjp(_prefix_fwd, _prefix_bwd)

_BWD = {"nn": (("nt", 0, 1), ("tn", 1, 0)), "nt": (("nn", 0, 1), ("tn", 0, 1)), "tn": (("nt", 1, 0), ("nn", 1, 0))}


def _bdot_raw(a, b, mode):
    return lax.dot_general(a.astype(BF16), b.astype(BF16), _DIMS[mode], preferred_element_type=F32)


@functools.partial(jax.custom_vjp, nondiff_argnums=(2,))
def _bdot(a, b, mode):
    return _bdot_raw(a, b, mode)


def _bdot_fwd(a, b, mode):
    return _bdot_raw(a, b, mode), (a, b)


def _bdot_bwd(mode, res, d):
    a, b = res
    (ma, da_pos, _), (mb, db_pos, _) = _BWD[mode]
    da = _bdot_raw(d, b, ma) if da_pos == 0 else _bdot_raw(b, d, ma)
    db = _bdot_raw(d, a, mb) if db_pos == 0 else _bdot_raw(a, d, mb)
    return da, db


_bdot.defvjp(_bdot_fwd, _bdot_bwd)


def _hg_chunk(pq, pf, pi, pg, l0, l1, l2, ng, s_t, cm, mk, with_total):
    c = HG_CHUNK
    mx = jnp.maximum(jnp.maximum(l0, l1), l2)
    e0, e1, e2 = jnp.exp(l0 - mx), jnp.exp(l1 - mx), jnp.exp(l2 - mx)
    lb = e0 / (e0 + e1 + e2)
    q = pq * jax.nn.sigmoid(pq)
    t1 = jnp.log(lb)
    t2 = jnp.log1p(-lb) + jax.nn.log_sigmoid(pf)
    lf = jnp.maximum(t1, t2) + jnp.log1p(jnp.exp(-jnp.abs(t1 - t2)))
    k = (1.0 - lb) * jax.nn.sigmoid(-pf)
    x = jnp.exp(_prefix(cm, lf))
    scores = mk[0:c] * _bdot(q, k, "nt")
    for lev in range(HG_LEVELS - 1):
        xl = x[lev * c:(lev + 1) * c]
        scores = scores + mk[(lev + 1) * c:(lev + 2) * c] * _bdot(q * xl, k * xl, "nt")
    if with_total:
        xl = x[(HG_LEVELS - 1) * c:HG_LEVELS * c]
        h = c // 2
        low = mk[HG_LEVELS * c + h:(HG_LEVELS + 1) * c] * _bdot(q[h:] * xl[h:], k * xl, "nt")
        scores = scores + jnp.concatenate([jnp.zeros((h, c), F32), low], axis=0)
    else:
        xl = x[(HG_LEVELS - 1) * c:HG_LEVELS * c]
        h = c // 2
        quad = _bdot(q[h:] * xl[h:], k[:h] * xl[:h], "nt")
        zero = jnp.zeros((h, h), F32)
        scores = scores + jnp.concatenate([jnp.concatenate([zero, zero], axis=1),
                                           jnp.concatenate([quad, zero], axis=1)], axis=0)
    x_incl = x[HG_LEVELS * c:(HG_LEVELS + 1) * c]
    x_after = x[(HG_LEVELS + 1) * c:(HG_LEVELS + 2) * c]
    if with_total:
        x_total = jnp.concatenate([x[(HG_LEVELS + 2) * c:(HG_LEVELS + 3) * c]] * (HG_DIM // c), axis=0)
    else:
        x_total = jnp.exp(jnp.sum(lf, axis=0, keepdims=True))
    o = _bdot(scores, pi, "nn") + _bdot(q * x_incl, s_t, "nt")
    s_new = s_t * x_total + _bdot(pi, k * x_after, "tn")
    rstd = lax.rsqrt(jnp.mean(o * o, axis=-1, keepdims=True) + RMS_EPS)
    y = o * rstd * ng * (pg * jax.nn.sigmoid(pg))
    return y, s_new


def _hg_specs(t, reverse, with_total):
    nrb = t // HG_ROWS
    rb = (lambda r: nrb - 1 - r) if reverse else (lambda r: r)
    proj = pl.BlockSpec((4, HG_ROWS, HG_DIM), lambda h, r: (0, rb(r), h))
    vec = pl.BlockSpec((1, HG_DIM), lambda h, r: (0, h))
    cm = pl.BlockSpec(((HG_LEVELS + 2 + with_total) * HG_CHUNK, HG_CHUNK), lambda h, r: (0, 0))
    mk = pl.BlockSpec(((HG_LEVELS + 1) * HG_CHUNK, HG_CHUNK), lambda h, r: (0, 0))
    rows = pl.BlockSpec((HG_ROWS, HG_DIM), lambda h, r: (rb(r), h))
    states = pl.BlockSpec((None, HG_ROWS // HG_CHUNK, HG_DIM, HG_DIM), lambda h, r: (h, rb(r), 0, 0))
    return proj, vec, cm, mk, rows, states


def _hg_fwd(proj, l0, l1, l2, ng):
    t = proj.shape[1]
    n_in = HG_ROWS // HG_CHUNK
    cm, mk = _hg_constants(True)
    p_spec, vec, cm_spec, mk_spec, rows, st_spec = _hg_specs(t, False, True)

    def body(p_ref, l0_ref, l1_ref, l2_ref, ng_ref, cm_ref, mk_ref, y_ref, st_ref, s_ref):
        @pl.when(pl.program_id(1) == 0)
        def _():
            s_ref[...] = jnp.zeros_like(s_ref)

        def step(ci, carry):
            sl = pl.ds(pl.multiple_of(ci * HG_CHUNK, HG_CHUNK), HG_CHUNK)
            s_t = s_ref[...]
            st_ref[ci] = s_t
            y, s_new = _hg_chunk(p_ref[0, sl, :], p_ref[1, sl, :], p_ref[2, sl, :], p_ref[3, sl, :],
                                 l0_ref[...], l1_ref[...], l2_ref[...], ng_ref[...], s_t,
                                 cm_ref[...], mk_ref[...], True)
            y_ref[sl, :] = y.astype(BF16)
            s_ref[...] = s_new
            return carry

        lax.fori_loop(0, n_in, step, 0, unroll=True)

    return pl.pallas_call(
        body, name="hg_fwd", grid=(HG_HEADS, t // HG_ROWS),
        out_shape=(jax.ShapeDtypeStruct((t, D_MODEL), BF16),
                   jax.ShapeDtypeStruct((HG_HEADS, t // HG_CHUNK, HG_DIM, HG_DIM), F32)),
        in_specs=[p_spec, vec, vec, vec, vec, cm_spec, mk_spec], out_specs=(rows, st_spec),
        scratch_shapes=[pltpu.VMEM((HG_DIM, HG_DIM), F32)],
        compiler_params=_cp("parallel", "arbitrary"),
    )(proj, l0, l1, l2, ng, cm, mk)


def _hg_bwd(proj, states, dy, l0, l1, l2, ng):
    t = proj.shape[1]
    n_in = HG_ROWS // HG_CHUNK
    cm, mk = _hg_constants(False)
    p_spec, vec, cm_spec, mk_spec, rows, st_spec = _hg_specs(t, True, False)

    def body(p_ref, st_ref, dy_ref, l0_ref, l1_ref, l2_ref, ng_ref, cm_ref, mk_ref,
             dp_ref, dl0_ref, dl1_ref, dl2_ref, dng_ref, ds_ref):
        @pl.when(pl.program_id(1) == 0)
        def _():
            ds_ref[...] = jnp.zeros_like(ds_ref)
            for r in (dl0_ref, dl1_ref, dl2_ref, dng_ref):
                r[...] = jnp.zeros_like(r)

        def step(it, carry):
            ci = n_in - 1 - it
            sl = pl.ds(pl.multiple_of(ci * HG_CHUNK, HG_CHUNK), HG_CHUNK)
            fn = functools.partial(_hg_chunk, cm=cm_ref[...], mk=mk_ref[...], with_total=False)
            _, vjp = jax.vjp(fn, p_ref[0, sl, :], p_ref[1, sl, :], p_ref[2, sl, :], p_ref[3, sl, :],
                             l0_ref[...], l1_ref[...], l2_ref[...], ng_ref[...], st_ref[ci])
            dq, df, di, dg, d0, d1, d2, dn, ds = vjp((dy_ref[sl, :], ds_ref[...]))
            dp_ref[0, sl, :] = dq.astype(BF16)
            dp_ref[1, sl, :] = df.astype(BF16)
            dp_ref[2, sl, :] = di.astype(BF16)
            dp_ref[3, sl, :] = dg.astype(BF16)
            dl0_ref[...] += d0
            dl1_ref[...] += d1
            dl2_ref[...] += d2
            dng_ref[...] += dn
            ds_ref[...] = ds
            return carry

        lax.fori_loop(0, n_in, step, 0, unroll=True)

    v_shape = jax.ShapeDtypeStruct((1, D_MODEL), F32)
    return pl.pallas_call(
        body, name="hg_bwd", grid=(HG_HEADS, t // HG_ROWS),
        out_shape=(jax.ShapeDtypeStruct((4, t, D_MODEL), BF16), v_shape, v_shape, v_shape, v_shape),
        in_specs=[p_spec, st_spec, rows, vec, vec, vec, vec, cm_spec, mk_spec],
        out_specs=(p_spec, vec, vec, vec, vec),
        scratch_shapes=[pltpu.VMEM((HG_DIM, HG_DIM), F32)],
        compiler_params=_cp("parallel", "arbitrary"),
    )(proj, states, dy, l0, l1, l2, ng, cm, mk)


_SQRT_HALF = 0.7071067811865476
_INV_SQRT_2PI = 0.3989422804014327


def _gelu(x):
    return 0.5 * x * (1.0 + lax.erf(x * _SQRT_HALF))


def _gelu_grad(x):
    return 0.5 * (1.0 + lax.erf(x * _SQRT_HALF)) + x * (_INV_SQRT_2PI * jnp.exp(-0.5 * x * x))


def _tril(n):
    return (lax.broadcasted_iota(jnp.int32, (n, n), 0) >= lax.broadcasted_iota(jnp.int32, (n, n), 1)).astype(F32)


SG_STEP = 4


def _sg_specs(d):
    row = lambda w: pl.BlockSpec((SG_STEP * SG_CHUNK, w), lambda i: (i, 0))
    vec = pl.BlockSpec((1, d), lambda i: (0, 0))
    cube = pl.BlockSpec((SG_GROUPS, SG_CHUNK, SG_CHUNK), lambda i: (0, 0, 0))
    return row, vec, cube


def _sg_fwd(pre, ln_g, ln_b, w_s, bias):
    t = pre.shape[0]
    d = D_MODEL
    row, vec, cube = _sg_specs(d)

    def body(pre_ref, g_ref, b_ref, w_ref, bias_ref, y_ref, vln_ref):
        tril = _tril(SG_CHUNK)
        for st in range(SG_STEP):
            rs = slice(st * SG_CHUNK, (st + 1) * SG_CHUNK)
            u = _gelu(pre_ref[rs, :d])
            vhat, _ = _ln_stats(_gelu(pre_ref[rs, d:]))
            vln_ref[rs, :] = vhat * g_ref[...] + b_ref[...]
            for g in range(SG_GROUPS):
                cs = slice(g * SG_DIM, (g + 1) * SG_DIM)
                gate = _bdot_raw(w_ref[g] * tril, vln_ref[rs, cs], "nn") + bias_ref[g]
                y_ref[rs, cs] = (u[:, cs] * gate).astype(BF16)

    return pl.pallas_call(
        body, name="sg_fwd", grid=(t // (SG_STEP * SG_CHUNK),), out_shape=jax.ShapeDtypeStruct((t, d), BF16),
        in_specs=[row(2 * d), vec, vec, cube, cube], out_specs=row(d),
        scratch_shapes=[pltpu.VMEM((SG_STEP * SG_CHUNK, d), F32)], compiler_params=_cp("parallel"),
    )(pre, ln_g, ln_b, w_s, bias)


def _sg_bwd(pre, dy, ln_g, ln_b, w_s, bias):
    t = pre.shape[0]
    d = D_MODEL
    row, vec, cube = _sg_specs(d)
    dbs_spec = pl.BlockSpec((SG_GROUPS, 8, SG_CHUNK), lambda i: (0, 0, 0))

    def body(pre_ref, dy_ref, g_ref, b_ref, w_ref, bias_ref, dpre_ref, dw_ref, dbs_ref, dlg_ref, dlb_ref,
             vln_ref, dvln_ref):
        @pl.when(pl.program_id(0) == 0)
        def _():
            for r in (dw_ref, dbs_ref, dlg_ref, dlb_ref):
                r[...] = jnp.zeros_like(r)

        tril = _tril(SG_CHUNK)
        ones = jnp.ones((8, SG_DIM), F32)
        for st in range(SG_STEP):
            rs = slice(st * SG_CHUNK, (st + 1) * SG_CHUNK)
            pu = pre_ref[rs, :d]
            pv = pre_ref[rs, d:]
            u = _gelu(pu)
            vhat, rstd = _ln_stats(_gelu(pv))
            vln_ref[rs, :] = vhat * g_ref[...] + b_ref[...]
            for g in range(SG_GROUPS):
                cs = slice(g * SG_DIM, (g + 1) * SG_DIM)
                wc = w_ref[g] * tril
                vg = vln_ref[rs, cs]
                gate = _bdot_raw(wc, vg, "nn") + bias_ref[g]
                dyg = dy_ref[rs, cs]
                dgate = dyg * u[:, cs]
                dpre_ref[rs, cs] = (dyg * gate * _gelu_grad(pu[:, cs])).astype(BF16)
                dw_ref[g] += tril * _bdot_raw(dgate, vg, "nt")
                dbs_ref[g] += lax.dot_general(ones, dgate, _DIMS["nt"], precision=HI, preferred_element_type=F32)
                dvln_ref[rs, cs] = _bdot_raw(wc, dgate, "tn")
            dvln = dvln_ref[rs, :]
            dv, dg, db = _ln_bwd_math(vhat, rstd, dvln, g_ref[...])
            dlg_ref[...] += dg
            dlb_ref[...] += db
            dpre_ref[rs, d:] = (dv * _gelu_grad(pv)).astype(BF16)

    return pl.pallas_call(
        body, name="sg_bwd", grid=(t // (SG_STEP * SG_CHUNK),),
        out_shape=(jax.ShapeDtypeStruct((t, 2 * d), BF16), jax.ShapeDtypeStruct(w_s.shape, F32),
                   jax.ShapeDtypeStruct((SG_GROUPS, 8, SG_CHUNK), F32),
                   jax.ShapeDtypeStruct((1, d), F32), jax.ShapeDtypeStruct((1, d), F32)),
        in_specs=[row(2 * d), row(d), vec, vec, cube, cube],
        out_specs=(row(2 * d), cube, dbs_spec, vec, vec),
        scratch_shapes=[pltpu.VMEM((SG_STEP * SG_CHUNK, d), F32), pltpu.VMEM((SG_STEP * SG_CHUNK, d), F32)],
        compiler_params=_cp("arbitrary"),
    )(pre, dy, ln_g, ln_b, w_s, bias)


FFN_ROWS = 1024
FFN_COLS = 1408
HALO = 16


def _ffn_conv(a_ext, w_ref, cb_ref):
    a1 = pltpu.roll(a_ext, 1, 0)
    a2 = pltpu.roll(a_ext, 2, 0)
    return w_ref[0:1, :] * a2 + w_ref[1:2, :] * a1 + w_ref[2:3, :] * a_ext + cb_ref[...], a1, a2


def _ffn_fwd(hh, conv_w, conv_b):
    t = hh.shape[1]
    nb8 = FFN_ROWS // HALO
    main = pl.BlockSpec((2, FFN_ROWS, FFN_COLS), lambda c, r: (0, r, c))
    prev = pl.BlockSpec((None, HALO, FFN_COLS), lambda c, r: (0, jnp.maximum(r * nb8 - 1, 0), c))
    wspec = pl.BlockSpec((3, FFN_COLS), lambda c, r: (0, c))
    bspec = pl.BlockSpec((1, FFN_COLS), lambda c, r: (0, c))

    def body(m_ref, p_ref, w_ref, cb_ref, z_ref):
        prev = jnp.where(pl.program_id(1) == 0, 0.0, p_ref[...].astype(F32))
        a_ext = jnp.concatenate([prev, m_ref[0].astype(F32)], axis=0)
        a, _, _ = _ffn_conv(a_ext, w_ref, cb_ref)
        a = a[HALO:]
        z_ref[...] = (a * jax.nn.sigmoid(a) * m_ref[1].astype(F32)).astype(BF16)

    return pl.pallas_call(
        body, name="ffn_fwd", grid=(D_FF // FFN_COLS, t // FFN_ROWS),
        out_shape=jax.ShapeDtypeStruct((t, D_FF), BF16),
        in_specs=[main, prev, wspec, bspec],
        out_specs=pl.BlockSpec((FFN_ROWS, FFN_COLS), lambda c, r: (r, c)),
        compiler_params=_cp("parallel", "parallel"),
    )(hh, hh, conv_w, conv_b)


def _ffn_bwd(hh, dz, conv_w, conv_b):
    t = hh.shape[1]
    nb8 = FFN_ROWS // HALO
    last8 = t // HALO - 1
    nr = t // FFN_ROWS
    main = pl.BlockSpec((2, FFN_ROWS, FFN_COLS), lambda c, r: (0, r, c))
    prev = pl.BlockSpec((None, HALO, FFN_COLS), lambda c, r: (0, jnp.maximum(r * nb8 - 1, 0), c))
    nxt = pl.BlockSpec((2, HALO, FFN_COLS), lambda c, r: (0, jnp.minimum((r + 1) * nb8, last8), c))
    dmain = pl.BlockSpec((FFN_ROWS, FFN_COLS), lambda c, r: (r, c))
    dnxt = pl.BlockSpec((HALO, FFN_COLS), lambda c, r: (jnp.minimum((r + 1) * nb8, last8), c))
    wspec = pl.BlockSpec((3, FFN_COLS), lambda c, r: (0, c))
    bspec = pl.BlockSpec((1, FFN_COLS), lambda c, r: (0, c))

    def body(m_ref, p_ref, n_ref, dz_ref, dzn_ref, w_ref, cb_ref, dh_ref, dw_ref, dcb_ref):
        r = pl.program_id(1)

        @pl.when(r == 0)
        def _():
            dw_ref[...] = jnp.zeros_like(dw_ref)
            dcb_ref[...] = jnp.zeros_like(dcb_ref)

        prev = jnp.where(r == 0, 0.0, p_ref[...].astype(F32))
        a_ext = jnp.concatenate([prev, m_ref[0].astype(F32), n_ref[0].astype(F32)], axis=0)
        a, a1, a2 = _ffn_conv(a_ext, w_ref, cb_ref)
        a = a[HALO:]
        b_ext = jnp.concatenate([m_ref[1], n_ref[1]], axis=0).astype(F32)
        dz_main = dz_ref[...].astype(F32)
        dz_ext = jnp.concatenate([dz_main, jnp.where(r == nr - 1, 0.0, dzn_ref[...].astype(F32))], axis=0)
        sig = jax.nn.sigmoid(a)
        da = dz_ext * b_ext * (sig * (1.0 + a * (1.0 - sig)))
        n_ext = FFN_ROWS + HALO
        da_p1 = pltpu.roll(da, n_ext - 1, 0)
        da_p2 = pltpu.roll(da, n_ext - 2, 0)
        da_raw = w_ref[2:3, :] * da + w_ref[1:2, :] * da_p1 + w_ref[0:1, :] * da_p2
        dh_ref[0] = da_raw[:FFN_ROWS].astype(BF16)
        dh_ref[1] = (dz_main * (a * sig)[:FFN_ROWS]).astype(BF16)
        dam = da[:FFN_ROWS]
        rows = slice(HALO, HALO + FFN_ROWS)
        dw_ref[0:1, :] += jnp.sum(dam * a2[rows], axis=0, keepdims=True)
        dw_ref[1:2, :] += jnp.sum(dam * a1[rows], axis=0, keepdims=True)
        dw_ref[2:3, :] += jnp.sum(dam * a_ext[rows], axis=0, keepdims=True)
        dcb_ref[...] += jnp.sum(dam, axis=0, keepdims=True)

    return pl.pallas_call(
        body, name="ffn_bwd", grid=(D_FF // FFN_COLS, nr),
        out_shape=(jax.ShapeDtypeStruct((2, t, D_FF), BF16), jax.ShapeDtypeStruct((3, D_FF), F32),
                   jax.ShapeDtypeStruct((1, D_FF), F32)),
        in_specs=[main, prev, nxt, dmain, dnxt, wspec, bspec],
        out_specs=(main, wspec, bspec), compiler_params=_cp("parallel", "arbitrary"),
    )(hh, hh, hh, dz, dz, conv_w, conv_b)


def _adamw_math(w, g, m, v):
    m = ADAM_B1 * m + (1.0 - ADAM_B1) * g
    v = ADAM_B2 * v + (1.0 - ADAM_B2) * (g * g)
    m_hat = m / (1.0 - ADAM_B1 ** ADAM_STEP)
    v_hat = v / (1.0 - ADAM_B2 ** ADAM_STEP)
    delta = -ADAM_LR * (m_hat / (jnp.sqrt(v_hat) + ADAM_EPS) + ADAM_WD * w)
    return delta, m, v


def _as2d(shape):
    n = int(np.prod(shape))
    c = shape[-1] if shape[-1] % 128 == 0 else (128 if n % 128 == 0 else shape[-1])
    return n // c, c


def _adamw(w, g, m, v, name, parts=None, own=None, is_me=None, layer=None, prev=None):
    shape = w.shape if layer is None else w.shape[1:]
    r, c = _as2d(shape)
    br = r if r <= 512 else _pick(r, (256, 176, 128, 64, 8))
    blk = pl.BlockSpec((br, c), lambda i: (i, 0))
    if layer is None:
        wblk, o = blk, jax.ShapeDtypeStruct((r, c), F32)
        w2, m2, v2 = (a.reshape(r, c) for a in (w, m, v))
    else:
        wblk, o = pl.BlockSpec((None, br, c), lambda i: (layer, i, 0)), jax.ShapeDtypeStruct((w.shape[0], r, c), F32)
        w2, m2, v2 = (a.reshape(w.shape[0], r, c) for a in (w, m, v))
    extra, extra_specs = [], []
    if parts is not None:
        g_in = parts.reshape(N_DEV, r, c)
        g_spec = pl.BlockSpec((N_DEV, br, c), lambda i: (0, i, 0))
        extra = [own.reshape(r, c), is_me]
        extra_specs = [blk, pl.BlockSpec(memory_space=pltpu.SMEM)]
    else:
        g_in = g.reshape(r, c)
        g_spec = blk
    aliases = {}
    if prev is not None:
        aliases = {4 + len(extra) + j: j for j in range(4)}
        extra = extra + [a.reshape(o.shape) for a in prev]
        extra_specs = extra_specs + [ANY] * 4

    def body(w_ref, g_ref, m_ref, v_ref, *rest):
        go_ref, d_ref, mo_ref, vo_ref = rest[-4:]
        if parts is not None:
            own_ref, me_ref = rest[:2]
            gsum = None
            for i in range(N_DEV):
                term = jnp.where(me_ref[i] > 0.5, own_ref[...], g_ref[i]).astype(F32)
                gsum = term if gsum is None else gsum + term
        else:
            gsum = g_ref[...]
        delta, m_new, v_new = _adamw_math(w_ref[...], gsum, m_ref[...], v_ref[...])
        go_ref[...] = gsum
        d_ref[...] = delta
        mo_ref[...] = m_new
        vo_ref[...] = v_new

    outs = pl.pallas_call(
        body, name=name, grid=(r // br,), out_shape=(o, o, o, o),
        in_specs=[wblk, g_spec, wblk, wblk] + extra_specs, out_specs=(wblk, wblk, wblk, wblk),
        input_output_aliases=aliases, compiler_params=_cp("parallel"),
    )(w2, g_in, m2, v2, *extra)
    return tuple(a.reshape(w.shape) for a in outs)


ALL_PEERS = tuple(range(1, N_DEV))
OTHER_CHIPS = (2, 4, 6)
SIBLING = (1,)


def _me_and_peers(relations=ALL_PEERS):
    x, y, c = lax.axis_index("x"), lax.axis_index("y"), lax.axis_index("c")
    me = 4 * x + 2 * y + c
    peers = []
    for k in relations:
        kx, ky, kc = (k >> 2) & 1, (k >> 1) & 1, k & 1
        px, py, pc = x ^ kx, y ^ ky, c ^ kc
        peers.append(((px, py, pc), 4 * px + 2 * py + pc))
    return me, peers


def _all_gather(shard, name):
    def body(x_ref, o_ref, send_sems, recv_sems, local_sem):
        me, peers = _me_and_peers()
        mine = pltpu.make_async_copy(x_ref, o_ref.at[me], local_sem)
        mine.start()
        sends = []
        for k, (dev, _) in enumerate(peers):
            cp = pltpu.make_async_remote_copy(src_ref=x_ref, dst_ref=o_ref.at[me], send_sem=send_sems.at[k],
                                              recv_sem=recv_sems.at[k], device_id=dev, device_id_type=MESH)
            cp.start()
            sends.append(cp)
        for k, (dev, idx) in enumerate(peers):
            pltpu.make_async_remote_copy(src_ref=x_ref, dst_ref=o_ref.at[idx], send_sem=send_sems.at[k],
                                         recv_sem=recv_sems.at[k], device_id=dev, device_id_type=MESH).wait_recv()
        for cp in sends:
            cp.wait_send()
        mine.wait()

    return pl.pallas_call(
        body, name=name, out_shape=jax.ShapeDtypeStruct((N_DEV,) + shard.shape, shard.dtype),
        in_specs=[ANY], out_specs=ANY,
        scratch_shapes=[pltpu.SemaphoreType.DMA((N_DEV - 1,)), pltpu.SemaphoreType.DMA((N_DEV - 1,)),
                        pltpu.SemaphoreType.DMA],
        compiler_params=pltpu.CompilerParams(has_side_effects=True),
    )(shard)


HBM = pl.BlockSpec(memory_space=pltpu.HBM)
SEM = pl.BlockSpec(memory_space=pltpu.SEMAPHORE)
EFFECT = pltpu.SideEffectType.DATAFLOW_SIDE_EFFECTING


def _xchg_start(parts, land, *, dep, name, peers=ALL_PEERS, pieces=(0,)):
    n_buf = 1 if parts is None else 2

    def body(*refs):
        land_ref = refs[n_buf - 1]
        send_sem, recv_sem = refs[n_buf + 1], refs[n_buf + 2]
        token = refs[2 * n_buf + 3]
        me, to = _me_and_peers(peers)
        for dev, idx in to:
            if parts is None:
                for r in pieces:
                    piece = land_ref.at[me ^ r]
                    pltpu.make_async_remote_copy(src_ref=piece, dst_ref=piece, send_sem=send_sem, recv_sem=recv_sem,
                                                 device_id=dev, device_id_type=MESH).start()
            else:
                pltpu.make_async_remote_copy(src_ref=refs[0].at[idx], dst_ref=land_ref.at[me], send_sem=send_sem,
                                             recv_sem=recv_sem, device_id=dev, device_id_type=MESH).start()
        token[...] = jnp.zeros_like(token)

    bufs = [land] if parts is None else [parts, land]
    return pl.pallas_call(
        body, name=name,
        out_shape=(pltpu.SemaphoreType.DMA(()), pltpu.SemaphoreType.DMA(()),
                   *[pltpu.HBM(b.shape, b.dtype) for b in bufs], jax.ShapeDtypeStruct((8, 128), F32)),
        in_specs=(*[HBM] * n_buf, ANY), out_specs=(SEM, SEM, *[HBM] * n_buf, pl.BlockSpec(memory_space=pltpu.VMEM)),
        input_output_aliases={i: 2 + i for i in range(n_buf)},
        compiler_params=pltpu.CompilerParams(has_side_effects=EFFECT),
    )(*[pltpu.with_memory_space_constraint(b, pltpu.HBM) for b in bufs], dep)


def _xchg_wait(handles, after, name, n_pieces=N_DEV - 1):
    send_sem, recv_sem, *bufs, _ = handles
    n_buf = len(bufs)

    def body(*refs):
        land_ref = refs[n_buf - 1]
        send_sem, recv_sem = refs[n_buf], refs[n_buf + 1]
        x, y, c = lax.axis_index("x"), lax.axis_index("y"), lax.axis_index("c")
        span = land_ref.at[pl.ds(0, n_pieces)]
        cp = pltpu.make_async_remote_copy(src_ref=span, dst_ref=span, send_sem=send_sem, recv_sem=recv_sem,
                                          device_id=(x, y, c), device_id_type=MESH)
        cp.wait_send()
        cp.wait_recv()

    return pl.pallas_call(
        body, name=name, out_shape=tuple(pltpu.HBM(b.shape, b.dtype) for b in bufs),
        in_specs=(*[HBM] * n_buf, SEM, SEM, ANY), out_specs=tuple([HBM] * n_buf),
        input_output_aliases={i: i for i in range(n_buf)},
        compiler_params=pltpu.CompilerParams(has_side_effects=EFFECT),
    )(*bufs, send_sem, recv_sem, after)


def _sum_devices(v, name):
    _, r, c = v.shape

    def body(x_ref, o_ref):
        acc = x_ref[0]
        for i in range(1, N_DEV):
            acc = acc + x_ref[i]
        o_ref[...] = acc

    return pl.pallas_call(
        body, name=name, out_shape=jax.ShapeDtypeStruct((r, c), F32),
        in_specs=[pl.BlockSpec(memory_space=pltpu.VMEM)], out_specs=pl.BlockSpec(memory_space=pltpu.VMEM),
        compiler_params=pltpu.CompilerParams(vmem_limit_bytes=VMEM_LIMIT),
    )(v)


def _local_step(x, target, lb_logits, p, weight, send_grad, send_small):
    g = {}
    l0, l1, l2 = lb_logits[0:1], lb_logits[1:2], lb_logits[2:3]
    x_bf = x.astype(BF16)

    proj0 = _mm(x_bf, weight("hg_w_in", x_bf), mode="nn", b_cs=True, o_cs=4, name="mm_hg_in")
    y0, states = _hg_fwd(proj0, l0, l1, l2, p["hg_norm_g"])
    mixed0 = _mm(y0, weight("hg_w_out", y0), mode="nn", name="mm_hg_out")
    hh_l, z_l = [], []

    def ffn_forward(layer, h_in_bf):
        hh = _mm(h_in_bf, weight(f"ffn_w_up{layer}", h_in_bf), mode="nn", b_cs=True, o_cs=2, out_dtype=BF16,
                 name=f"mm_up{layer}")
        z = _ffn_fwd(hh, p["ffn_conv_w"][layer], p["ffn_conv_b"][layer])
        ffn = _mm(z, weight(f"ffn_w_down{layer}", z), mode="nn", name=f"mm_down{layer}")
        hh_l.append(hh)
        z_l.append(z)
        return ffn

    h1, h1b = _ln_fwd(x, mixed0, p["ln1_g"][0], p["ln1_b"][0], "ln1_0")
    ffn0 = ffn_forward(0, h1b)
    h2, h2b = _ln_fwd(h1, ffn0, p["ln2_g"][0], p["ln2_b"][0], "ln2_0")
    pre1 = _mm(h2b, weight("sg_w_in", h2b), mode="nn", b_cs=True, name="mm_sg_in")
    y1 = _sg_fwd(pre1, p["sg_ln_g"], p["sg_ln_b"], p["sg_w_s"], p["sg_bias"])
    mixed1 = _mm(y1, weight("sg_w_out", y1), mode="nn", name="mm_sg_out")
    h3, h3b = _ln_fwd(h2, mixed1, p["ln1_g"][1], p["ln1_b"][1], "ln1_1")
    ffn1 = ffn_forward(1, h3b)

    loss, ds, dres, g["ln2_g1"], g["ln2_b1"] = _ln_loss_bwd(h3, ffn1, target, p["ln2_g"][1], p["ln2_b"][1], "ln2_1_loss")

    def ffn_backward(layer, ds_bf, h_in_bf, dres):
        tok = send_grad(f"ffn_w_down{layer}",
                        _mm(z_l[layer], ds_bf, mode="tn", out_dtype=BF16, name=f"mm_dw_down{layer}"))
        dz = _mm(ds_bf, weight(f"ffn_w_down{layer}", None), mode="nt", out_dtype=BF16, name=f"mm_d_z{layer}",
                 deps=(tok,))
        dhh, g[f"conv_w{layer}"], g[f"conv_b{layer}"] = _ffn_bwd(hh_l[layer], dz, p["ffn_conv_w"][layer],
                                                                   p["ffn_conv_b"][layer])
        tok = send_grad(f"ffn_w_up{layer}", _mm(h_in_bf, dhh, mode="tn", b_cs=True, o_cs=N_DEV, out_dtype=BF16,
                                                name=f"mm_dw_up{layer}"))
        return _mm(dhh, weight(f"ffn_w_up{layer}", None), mode="nt", a_cs=True, b_cs=True, addend=dres,
                   name=f"mm_d_up{layer}", deps=(tok,))

    dh3 = ffn_backward(1, ds, h3b, dres)
    ds, dres, g["ln1_g1"], g["ln1_b1"] = _ln_bwd(h2, mixed1, dh3, p["ln1_g"][1], "ln1_1_bwd")
    tok = send_grad("sg_w_out", _mm(y1, ds, mode="tn", out_dtype=BF16, name="mm_dw_sg_out"))
    dy1 = _mm(ds, weight("sg_w_out", None), mode="nt", name="mm_d_sg_out", deps=(tok,))
    dpre1, g["sg_w_s"], dbs, g["sg_ln_g"], g["sg_ln_b"] = _sg_bwd(pre1, dy1, p["sg_ln_g"], p["sg_ln_b"],
                                                                    p["sg_w_s"], p["sg_bias"])
    g["sg_b_s"] = dbs[:, 0, :]
    tok = send_grad("sg_w_in", _mm(h2b, dpre1, mode="tn", o_cs=N_DEV, out_dtype=BF16, name="mm_dw_sg_in"))
    dh2 = _mm(dpre1, weight("sg_w_in", None), mode="nt", b_cs=True, addend=dres, name="mm_d_sg_in", deps=(tok,))
    ds, dres, g["ln2_g0"], g["ln2_b0"] = _ln_bwd(h1, ffn0, dh2, p["ln2_g"][0], "ln2_0_bwd")
    dh1 = ffn_backward(0, ds, h1b, dres)
    ds, dres, g["ln1_g0"], g["ln1_b0"] = _ln_bwd(x, mixed0, dh1, p["ln1_g"][0], "ln1_0_bwd")
    tok = send_grad("hg_w_out", _mm(y0, ds, mode="tn", out_dtype=BF16, name="mm_dw_hg_out"))
    dy0 = _mm(ds, weight("hg_w_out", None), mode="nt", name="mm_d_hg_out", deps=(tok,))
    dproj, d0, d1, d2, g["hg_norm_g"] = _hg_bwd(proj0, states, dy0, l0, l1, l2, p["hg_norm_g"])
    g["lb_logits"] = jnp.concatenate([d0, d1, d2], axis=0)
    tok = send_small(g)
    tok = send_grad("hg_w_in", _mm(x_bf, dproj, mode="tn", b_cs=True, o_cs=N_DEV, out_dtype=BF16, name="mm_dw_hg_in",
                                   deps=(tok,)))
    grad_x = _mm(dproj, weight("hg_w_in", None), mode="nt", a_cs=True, b_cs=True, addend=dres, name="mm_d_hg_in",
                 deps=(tok,))
    return loss[0, 0], grad_x


_SMALL = ("lb_logits", "hg_norm_g", "sg_ln_g", "sg_ln_b", "sg_w_s", "sg_b_s", "ffn_conv_w", "ffn_conv_b",
          "ln1_g", "ln1_b", "ln2_g", "ln2_b")
_NAMES = ("lb_logits", "hg_w_in", "hg_norm_g", "hg_w_out", "sg_w_in", "sg_ln_g", "sg_ln_b", "sg_w_s", "sg_b_s",
          "sg_w_out", "ffn_w_up", "ffn_conv_w", "ffn_conv_b", "ffn_w_down", "ln1_g", "ln1_b", "ln2_g", "ln2_b")


def kernel(x, lb_logits, hg_w_in, hg_norm_g, hg_w_out, sg_w_in, sg_ln_g, sg_ln_b, sg_w_s, sg_b_s, sg_w_out, ffn_w_up, ffn_conv_w, ffn_conv_b, ffn_w_down, ln1_g, ln1_b, ln2_g, ln2_b, loss_target, m_lb_logits, m_hg_w_in, m_hg_norm_g, m_hg_w_out, m_sg_w_in, m_sg_ln_g, m_sg_ln_b, m_sg_w_s, m_sg_b_s, m_sg_w_out, m_ffn_w_up, m_ffn_conv_w, m_ffn_conv_b, m_ffn_w_down, m_ln1_g, m_ln1_b, m_ln2_g, m_ln2_b, v_lb_logits, v_hg_w_in, v_hg_norm_g, v_hg_w_out, v_sg_w_in, v_sg_ln_g, v_sg_ln_b, v_sg_w_s, v_sg_b_s, v_sg_w_out, v_ffn_w_up, v_ffn_conv_w, v_ffn_conv_b, v_ffn_w_down, v_ln1_g, v_ln1_b, v_ln2_g, v_ln2_b):
    w = dict(lb_logits=lb_logits, hg_w_in=hg_w_in, hg_norm_g=hg_norm_g, hg_w_out=hg_w_out, sg_w_in=sg_w_in,
             sg_ln_g=sg_ln_g, sg_ln_b=sg_ln_b, sg_w_s=sg_w_s, sg_b_s=sg_b_s, sg_w_out=sg_w_out, ffn_w_up=ffn_w_up,
             ffn_conv_w=ffn_conv_w, ffn_conv_b=ffn_conv_b, ffn_w_down=ffn_w_down, ln1_g=ln1_g, ln1_b=ln1_b,
             ln2_g=ln2_g, ln2_b=ln2_b)
    m = dict(lb_logits=m_lb_logits, hg_w_in=m_hg_w_in, hg_norm_g=m_hg_norm_g, hg_w_out=m_hg_w_out,
             sg_w_in=m_sg_w_in, sg_ln_g=m_sg_ln_g, sg_ln_b=m_sg_ln_b, sg_w_s=m_sg_w_s, sg_b_s=m_sg_b_s,
             sg_w_out=m_sg_w_out, ffn_w_up=m_ffn_w_up, ffn_conv_w=m_ffn_conv_w, ffn_conv_b=m_ffn_conv_b,
             ffn_w_down=m_ffn_w_down, ln1_g=m_ln1_g, ln1_b=m_ln1_b, ln2_g=m_ln2_g, ln2_b=m_ln2_b)
    v = dict(lb_logits=v_lb_logits, hg_w_in=v_hg_w_in, hg_norm_g=v_hg_norm_g, hg_w_out=v_hg_w_out,
             sg_w_in=v_sg_w_in, sg_ln_g=v_sg_ln_g, sg_ln_b=v_sg_ln_b, sg_w_s=v_sg_w_s, sg_b_s=v_sg_b_s,
             sg_w_out=v_sg_w_out, ffn_w_up=v_ffn_w_up, ffn_conv_w=v_ffn_conv_w, ffn_conv_b=v_ffn_conv_b,
             ffn_w_down=v_ffn_w_down, ln1_g=v_ln1_g, ln1_b=v_ln1_b, ln2_g=v_ln2_g, ln2_b=v_ln2_b)
    me = 4 * lax.axis_index("x") + 2 * lax.axis_index("y") + lax.axis_index("c")
    is_me = (jnp.arange(N_DEV) == me).astype(F32)
    d = D_MODEL

    shards = [("hg_w_in", hg_w_in[0]), ("hg_w_out", hg_w_out[0]), ("ffn_w_up0", ffn_w_up[0]),
              ("ffn_w_down0", ffn_w_down[0]), ("sg_w_in", sg_w_in[0]), ("sg_w_out", sg_w_out[0]),
              ("ffn_w_up1", ffn_w_up[1]), ("ffn_w_down1", ffn_w_down[1])]
    row_sharded = {"hg_w_out": (d, d), "sg_w_out": (d, d), "ffn_w_down0": (D_FF, d), "ffn_w_down1": (D_FF, d)}
    sv = jnp.zeros((8, 768), F32)
    sv = sv.at[0, :256].set(sg_ln_g[0]).at[1, :256].set(sg_ln_b[0]).at[2:8, :704].set(ffn_conv_w.reshape(6, 704))
    sv = _all_gather(sv, "ag_small")
    gathers, gathered, dep = {}, {}, sv
    for n, shard in shards:
        land = lax.dynamic_update_index_in_dim(lax.empty((N_DEV,) + shard.shape, BF16), shard.astype(BF16), me, 0)
        gathers[n] = _xchg_start(None, land, dep=dep, name=f"ag_{n}", peers=OTHER_CHIPS)
        dep = gathers[n][-1]
    last_start = dep
    handed = {}
    order = [n for n, _ in shards]

    def hand_over(n, after):
        (land,) = _xchg_wait(gathers[n], after, f"agw_{n}", n_pieces=len(OTHER_CHIPS))
        handed[n] = _xchg_start(None, land, dep=is_me, name=f"ah_{n}", peers=SIBLING, pieces=(0,) + OTHER_CHIPS)

    def weight(n, after):
        if n not in gathered:
            if n == order[0]:
                hand_over(n, last_start)
                after = handed[n][-1]
            nxt = order.index(n) + 1
            if nxt < len(order):
                hand_over(order[nxt], after)
                after = handed[order[nxt]][-1]
            (full,) = _xchg_wait(handed[n], after, f"ahw_{n}", n_pieces=1 + len(OTHER_CHIPS))
            gathered[n] = full.reshape(row_sharded[n]) if n in row_sharded else full
        return gathered[n]

    grad_sends = {}

    def send_grad(n, parts):
        shape = (N_DEV,) + dict(shards)[n].shape
        grad_sends[n] = _xchg_start(parts.reshape(shape), lax.empty(shape, parts.dtype), dep=is_me, name=f"rs_{n}")
        return grad_sends[n][-1]

    small_send = {}

    def send_small(g):
        small = [g["lb_logits"], g["hg_norm_g"], g["sg_ln_g"], g["sg_ln_b"], g["sg_w_s"], g["sg_b_s"],
                 g["conv_w0"], g["conv_w1"], g["conv_b0"], g["conv_b1"], g["ln1_g0"], g["ln1_g1"],
                 g["ln1_b0"], g["ln1_b1"], g["ln2_g0"], g["ln2_g1"], g["ln2_b0"], g["ln2_b1"]]
        flat = jnp.concatenate([a.reshape(-1) for a in small])
        rows = -(-flat.shape[0] // (8 * 128)) * 8
        flat = jnp.pad(flat, (0, rows * 128 - flat.shape[0])).reshape(rows, 128)
        land = lax.dynamic_update_index_in_dim(lax.empty((N_DEV, rows, 128), F32), flat, me, 0)
        small_send["shapes"] = [a.shape for a in small]
        small_send["handles"] = _xchg_start(None, land, dep=is_me, name="ar_small")
        return small_send["handles"][-1]

    p = {}
    p["sg_ln_g"] = sv[:, 0, :256].reshape(1, d)
    p["sg_ln_b"] = sv[:, 1, :256].reshape(1, d)
    conv_w_full = jnp.transpose(sv[:, 2:8, :704].reshape(N_DEV, DEPTH, 3, 704), (1, 2, 0, 3)).reshape(DEPTH, 3, D_FF)
    p["ffn_conv_w"] = [conv_w_full[l] for l in range(DEPTH)]
    p["ffn_conv_b"] = [ffn_conv_b[l:l + 1] for l in range(DEPTH)]
    p["hg_norm_g"] = hg_norm_g
    p["sg_w_s"] = sg_w_s[0]
    p["sg_bias"] = jnp.broadcast_to(sg_b_s[0][:, :, None], (SG_GROUPS, SG_CHUNK, SG_DIM))
    for n in ("ln1_g", "ln1_b", "ln2_g", "ln2_b"):
        p[n] = [w[n][l:l + 1] for l in range(DEPTH)]

    loss_part, grad_x = _local_step(x[0], loss_target[0], lb_logits, p, weight, send_grad, send_small)
    loss = lax.psum(loss_part, ("x", "y", "c"))

    (landed,) = _xchg_wait(small_send["handles"], grad_x, "arw_small")
    red = _sum_devices(landed, "sum_small").reshape(-1)
    shapes = small_send["shapes"]
    offs = np.cumsum([0] + [int(np.prod(sh)) for sh in shapes])
    r = [red[offs[i]:offs[i + 1]].reshape(shapes[i]) for i in range(len(shapes))]
    gs = {}
    gs["lb_logits"] = r[0]
    gs["hg_norm_g"] = r[1]
    gs["sg_ln_g"] = lax.dynamic_slice(r[2], (0, me * 256), (1, 256))
    gs["sg_ln_b"] = lax.dynamic_slice(r[3], (0, me * 256), (1, 256))
    gs["sg_w_s"] = r[4][None]
    gs["sg_b_s"] = r[5][None]
    gs["ffn_conv_w"] = lax.dynamic_slice(jnp.stack([r[6], r[7]]), (0, 0, me * 704), (DEPTH, 3, 704))
    gs["ffn_conv_b"] = jnp.concatenate([r[8], r[9]], axis=0)
    gs["ln1_g"] = jnp.concatenate([r[10], r[11]], axis=0)
    gs["ln1_b"] = jnp.concatenate([r[12], r[13]], axis=0)
    gs["ln2_g"] = jnp.concatenate([r[14], r[15]], axis=0)
    gs["ln2_b"] = jnp.concatenate([r[16], r[17]], axis=0)

    out_g, out_d, out_m, out_v = {}, {}, {}, {}
    for n in _SMALL:
        out_g[n], out_d[n], out_m[n], out_v[n] = _adamw(w[n], gs[n], m[n], v[n], f"adamw_{n}")

    res, after = {}, out_v["ln2_b"]
    for n in ("ffn_w_down1", "ffn_w_up1", "sg_w_out", "sg_w_in", "ffn_w_down0", "ffn_w_up0", "hg_w_out", "hg_w_in"):
        parts, recv = _xchg_wait(grad_sends[n], after, f"rsw_{n}")
        own = lax.dynamic_index_in_dim(parts, me, 0, keepdims=False)
        if n[-1] in "01":
            base, layer = n[:-1], int(n[-1])
            res[n] = _adamw(w[base], None, m[base], v[base], f"adamw_{n}", parts=recv, own=own, is_me=is_me,
                            layer=layer, prev=res.get(base + "1"))
        else:
            res[n] = _adamw(w[n][0], None, m[n][0], v[n][0], f"adamw_{n}", parts=recv, own=own, is_me=is_me)
        after = res[n][3]
    for n in ("hg_w_in", "hg_w_out", "sg_w_in", "sg_w_out"):
        out_g[n], out_d[n], out_m[n], out_v[n] = (a[None] for a in res[n])
    for n in ("ffn_w_up", "ffn_w_down"):
        out_g[n], out_d[n], out_m[n], out_v[n] = res[n + "0"]

    return (loss, grad_x[None], *[out_g[n] for n in _NAMES], *[out_d[n] for n in _NAMES],
            *[out_m[n] for n in _NAMES], *[out_v[n] for n in _NAMES])
```

```python
import functools

import numpy as np
import jax
import jax.numpy as jnp
from jax import lax
from jax.experimental import pallas as pl
from jax.experimental.pallas import tpu as pltpu

F32 = jnp.float32
BF16 = jnp.bfloat16
HI = lax.Precision.HIGHEST

N_DEV = 8
D_MODEL = 2048
HG_HEADS = 16
HG_DIM = 128
HG_CHUNK = 128
SG_GROUPS = 16
SG_DIM = 128
SG_CHUNK = 128
D_FF = 5632
DEPTH = 2
ALPHA = (2 * DEPTH) ** 0.25
LN_EPS = 1e-5
RMS_EPS = 1e-6
ADAM_LR = 0.001
ADAM_B1 = 0.9
ADAM_B2 = 0.999
ADAM_EPS = 1e-08
ADAM_WD = 0.01
ADAM_STEP = 10

VMEM_LIMIT = 56 * 1024 * 1024
MESH = pl.DeviceIdType.MESH
ANY = pl.BlockSpec(memory_space=pl.ANY)


def _cp(*sem):
    return pltpu.CompilerParams(dimension_semantics=sem, vmem_limit_bytes=VMEM_LIMIT)


def _pick(n, cands):
    for c in cands:
        if n % c == 0:
            return c
    raise ValueError(f"no tile for {n} in {cands}")


_DIMS = {"nn": (((1,), (0,)), ((), ())), "nt": (((1,), (1,)), ((), ())), "tn": (((0,), (0,)), ((), ()))}


MM_VMEM_BUDGET = 44 * 1024 * 1024


def _mm_tiles(m, n, kk, n_div, k_div, out_bytes, has_addend, k_piece, transposed_a):
    best = None
    for bm in (1024, 1408, 512, 256):
        if m % bm:
            continue
        for bn in (2048, 1024, 1408, 512, 256, 128):
            if n % bn or any(d % bn for d in n_div):
                continue
            for bk in (kk, 5632, 4096, 2816, 2048, 1408, 1024, 512, 256, 128):
                if bk > kk or kk % bk or any(d % bk for d in k_div):
                    continue
                if k_piece and bk % k_piece:
                    continue
                nk = kk // bk
                vmem = 4 * (bm * bk + bk * bn) + 2 * bm * bn * out_bytes + (nk > 1) * 4 * bm * bn
                vmem += has_addend * 8 * bm * bn + transposed_a * 2 * bm * bk
                vmem += (out_bytes < 4) * 4 * bm * bn
                if vmem > MM_VMEM_BUDGET:
                    continue
                score = (max(nk, 2), -(bm * bn), -bk)
                if best is None or score < best[0]:
                    best = (score, bm, bn, bk)
    if best is None:
        raise ValueError(f"no tiles for {(m, n, kk)}")
    return best[1:]


def _mm(a, b, *, mode, name, out_dtype=F32, a_cs=False, b_cs=False, o_cs=None, addend=None, deps=()):
    if mode == "tn":
        kk, m = a.shape
    elif a_cs:
        m, kk = a.shape[1], a.shape[0] * a.shape[2]
    else:
        m, kk = a.shape
    if mode == "nt":
        n = b.shape[1] if b_cs else b.shape[0]
    else:
        n = b.shape[0] * b.shape[2] if b_cs else b.shape[1]
    a_c = a.shape[2] if a_cs else None
    b_c = b.shape[2] if b_cs else None
    o_c = n // o_cs if o_cs else None
    k_piece = b_c if (b_cs and mode == "nt") else None
    n_div = [c for c in (b_c if (b_cs and mode != "nt") else None, o_c) if c]
    k_div = [a_c] if a_c else []
    bm, bn, bk = _mm_tiles(m, n, kk, n_div, k_div, jnp.dtype(out_dtype).itemsize, addend is not None, k_piece,
                           mode == "tn")
    nk = kk // bk
    kb = bk // k_piece if k_piece else 1

    def cs_idx(blk, per):
        return (blk * per[0]) // per[1], (blk * per[0] % per[1]) // per[0]

    if mode == "tn":
        a_spec = pl.BlockSpec((bk, bm), lambda i, j, k: (k, i))
    elif a_cs:
        def a_map(i, j, k):
            s, r = cs_idx(k, (bk, a_c))
            return (s, i, r)
        a_spec = pl.BlockSpec((None, bm, bk), a_map)
    else:
        a_spec = pl.BlockSpec((bm, bk), lambda i, j, k: (i, k))
    if mode == "nt":
        if b_cs:
            b_spec = pl.BlockSpec((kb, bn, b_c), lambda i, j, k: (k, j, 0))
        else:
            b_spec = pl.BlockSpec((bn, bk), lambda i, j, k: (j, k))
    else:
        if b_cs:
            def b_map(i, j, k):
                s, r = cs_idx(j, (bn, b_c))
                return (s, k, r)
            b_spec = pl.BlockSpec((None, bk, bn), b_map)
        else:
            b_spec = pl.BlockSpec((bk, bn), lambda i, j, k: (k, j))
    if o_cs:
        def o_map(i, j, k):
            s, r = cs_idx(j, (bn, o_c))
            return (s, i, r)
        o_spec = pl.BlockSpec((None, bm, bn), o_map)
        out_shape = jax.ShapeDtypeStruct((o_cs, m, o_c), out_dtype)
    else:
        o_spec = pl.BlockSpec((bm, bn), lambda i, j, k: (i, j))
        out_shape = jax.ShapeDtypeStruct((m, n), out_dtype)
    in_specs = [a_spec, b_spec]
    args = [a, b]
    if addend is not None:
        in_specs.append(pl.BlockSpec((bm, bn), lambda i, j, k: (i, j)))
        args.append(addend)
    dims = _DIMS[mode]
    in_specs += [ANY] * len(deps)
    args += list(deps)
    n_in = len(args)

    def body(*refs):
        a_ref, b_ref = refs[0], refs[1]
        add_ref = refs[2] if addend is not None else None
        o_ref = refs[n_in]
        if k_piece:
            d = None
            for p in range(kb):
                dp = lax.dot_general(a_ref[:, p * b_c:(p + 1) * b_c].astype(BF16), b_ref[p].astype(BF16), dims,
                                     preferred_element_type=F32)
                d = dp if d is None else d + dp
        else:
            d = lax.dot_general(a_ref[...].astype(BF16), b_ref[...].astype(BF16), dims, preferred_element_type=F32)

        def finish(r):
            if addend is not None:
                r = r + add_ref[...]
            o_ref[...] = r.astype(o_ref.dtype)

        if nk == 1:
            finish(d)
            return
        acc_ref = refs[n_in + 1]
        k = pl.program_id(2)

        @pl.when(k == 0)
        def _():
            acc_ref[...] = d

        if nk > 2:
            @pl.when((k > 0) & (k < nk - 1))
            def _():
                acc_ref[...] += d

        @pl.when(k == nk - 1)
        def _():
            finish(acc_ref[...] + d)

    return pl.pallas_call(
        body, name=name, out_shape=out_shape, grid=(m // bm, n // bn, nk),
        in_specs=in_specs, out_specs=o_spec,
        scratch_shapes=[pltpu.VMEM((bm, bn), F32)] if nk > 1 else [],
        compiler_params=_cp("parallel", "parallel", "arbitrary"),
    )(*args)


LN_ROWS = 512


def _ln_stats(s):
    mu = jnp.mean(s, axis=-1, keepdims=True)
    sc = s - mu
    var = jnp.mean(sc * sc, axis=-1, keepdims=True)
    rstd = lax.rsqrt(var + LN_EPS)
    return sc * rstd, rstd


def _ln_fwd(h, sub, g, b, name):
    t, d = h.shape
    row = pl.BlockSpec((LN_ROWS, d), lambda i: (i, 0))
    vec = pl.BlockSpec((1, d), lambda i: (0, 0))

    def body(h_ref, s_ref, g_ref, b_ref, y_ref, yb_ref):
        xhat, _ = _ln_stats(ALPHA * h_ref[...] + s_ref[...])
        y = xhat * g_ref[...] + b_ref[...]
        y_ref[...] = y
        yb_ref[...] = y.astype(BF16)

    return pl.pallas_call(
        body, name=name, grid=(t // LN_ROWS,),
        out_shape=(jax.ShapeDtypeStruct((t, d), F32), jax.ShapeDtypeStruct((t, d), BF16)),
        in_specs=[row, row, vec, vec], out_specs=(row, row), compiler_params=_cp("parallel"),
    )(h, sub, g, b)


def _ln_bwd_math(xhat, rstd, dy, g):
    dxhat = dy * g
    m1 = jnp.mean(dxhat, axis=-1, keepdims=True)
    m2 = jnp.mean(dxhat * xhat, axis=-1, keepdims=True)
    ds = rstd * (dxhat - m1 - xhat * m2)
    dg = jnp.sum(dy * xhat, axis=0, keepdims=True)
    db = jnp.sum(dy, axis=0, keepdims=True)
    return ds, dg, db


def _ln_bwd(h, sub, dy, g, name):
    t, d = h.shape
    row = pl.BlockSpec((LN_ROWS, d), lambda i: (i, 0))
    vec = pl.BlockSpec((1, d), lambda i: (0, 0))

    def body(h_ref, s_ref, dy_ref, g_ref, ds_ref, dres_ref, dg_ref, dbeta_ref):
        xhat, rstd = _ln_stats(ALPHA * h_ref[...] + s_ref[...])
        ds, dg, db = _ln_bwd_math(xhat, rstd, dy_ref[...], g_ref[...])
        ds_ref[...] = ds.astype(BF16)
        dres_ref[...] = ALPHA * ds

        @pl.when(pl.program_id(0) == 0)
        def _():
            dg_ref[...] = jnp.zeros_like(dg_ref)
            dbeta_ref[...] = jnp.zeros_like(dbeta_ref)

        dg_ref[...] += dg
        dbeta_ref[...] += db

    return pl.pallas_call(
        body, name=name, grid=(t // LN_ROWS,),
        out_shape=(jax.ShapeDtypeStruct((t, d), BF16), jax.ShapeDtypeStruct((t, d), F32),
                   jax.ShapeDtypeStruct((1, d), F32), jax.ShapeDtypeStruct((1, d), F32)),
        in_specs=[row, row, row, vec], out_specs=(row, row, vec, vec),
        compiler_params=_cp("arbitrary"),
    )(h, sub, dy, g)


def _ln_loss_bwd(h, sub, target, g, b, name):
    t, d = h.shape
    row = pl.BlockSpec((LN_ROWS, d), lambda i: (i, 0))
    vec = pl.BlockSpec((1, d), lambda i: (0, 0))
    lvec = pl.BlockSpec((1, 128), lambda i: (0, 0))

    def body(h_ref, s_ref, t_ref, g_ref, b_ref, loss_ref, ds_ref, dres_ref, dg_ref, dbeta_ref):
        xhat, rstd = _ln_stats(ALPHA * h_ref[...] + s_ref[...])
        y = xhat * g_ref[...] + b_ref[...]
        err = y - t_ref[...]
        part = 0.5 * jnp.sum(jnp.mean(err * err, axis=-1, keepdims=True), axis=0, keepdims=True)
        ds, dg, db = _ln_bwd_math(xhat, rstd, err * (1.0 / d), g_ref[...])
        ds_ref[...] = ds.astype(BF16)
        dres_ref[...] = ALPHA * ds

        @pl.when(pl.program_id(0) == 0)
        def _():
            loss_ref[...] = jnp.zeros_like(loss_ref)
            dg_ref[...] = jnp.zeros_like(dg_ref)
            dbeta_ref[...] = jnp.zeros_like(dbeta_ref)

        loss_ref[...] += jnp.broadcast_to(part, loss_ref.shape)
        dg_ref[...] += dg
        dbeta_ref[...] += db

    return pl.pallas_call(
        body, name=name, grid=(t // LN_ROWS,),
        out_shape=(jax.ShapeDtypeStruct((1, 128), F32), jax.ShapeDtypeStruct((t, d), BF16),
                   jax.ShapeDtypeStruct((t, d), F32), jax.ShapeDtypeStruct((1, d), F32),
                   jax.ShapeDtypeStruct((1, d), F32)),
        in_specs=[row, row, row, vec, vec], out_specs=(lvec, row, row, vec, vec),
        compiler_params=_cp("arbitrary"),
    )(h, sub, target, g, b)


HG_ROWS = 2048
HG_LEVELS = 7


def _hg_constants(with_total):
    c = HG_CHUNK
    t = np.arange(c)[:, None]
    j = np.arange(c)[None, :]
    blocks, masks = [], [np.eye(c)]
    for lev in range(HG_LEVELS):
        m = 1 << lev
        second = (t % (2 * m)) >= m
        mid = (t // (2 * m)) * (2 * m) + m - 1
        blocks.append((second & (j > mid) & (j <= t)) | ((~second) & (j > t) & (j <= mid)))
        same = (t // (2 * m)) == (j // (2 * m))
        masks.append(same & second & ((j % (2 * m)) < m))
    blocks += [j <= t, j > t] + ([np.ones((c, c), bool)] if with_total else [])
    cm = np.concatenate(blocks, axis=0).astype(np.float32)
    mk = np.concatenate(masks, axis=0).astype(np.float32)
    return jnp.asarray(cm, dtype=BF16), jnp.asarray(mk)


def _dot_01(cm, x, mode):
    hi = x.astype(BF16)
    lo = (x - hi.astype(F32)).astype(BF16)
    both = lax.dot_general(cm, jnp.concatenate([hi, lo], axis=1), _DIMS[mode], preferred_element_type=F32)
    n = x.shape[-1]
    return both[:, :n] + both[:, n:]


@jax.custom_vjp
def _prefix(cm, lf):
    return _dot_01(cm, lf, "nn")


def _prefix_fwd(cm, lf):
    return _prefix(cm, lf), cm


def _prefix_bwd(cm, d):
    return jnp.zeros_like(cm), _dot_01(cm, d, "tn")


_prefix.defvjp(_prefix_fwd, _prefix_bwd)

_BWD = {"nn": (("nt", 0, 1), ("tn", 1, 0)), "nt": (("nn", 0, 1), ("tn", 0, 1)), "tn": (("nt", 1, 0), ("nn", 1, 0))}


def _bdot_raw(a, b, mode):
    return lax.dot_general(a.astype(BF16), b.astype(BF16), _DIMS[mode], preferred_element_type=F32)


@functools.partial(jax.custom_vjp, nondiff_argnums=(2,))
def _bdot(a, b, mode):
    return _bdot_raw(a, b, mode)


def _bdot_fwd(a, b, mode):
    return _bdot_raw(a, b, mode), (a, b)


def _bdot_bwd(mode, res, d):
    a, b = res
    (ma, da_pos, _), (mb, db_pos, _) = _BWD[mode]
    da = _bdot_raw(d, b, ma) if da_pos == 0 else _bdot_raw(b, d, ma)
    db = _bdot_raw(d, a, mb) if db_pos == 0 else _bdot_raw(a, d, mb)
    return da, db


_bdot.defvjp(_bdot_fwd, _bdot_bwd)


def _hg_chunk(pq, pf, pi, pg, l0, l1, l2, ng, s_t, cm, mk, with_total):
    c = HG_CHUNK
    mx = jnp.maximum(jnp.maximum(l0, l1), l2)
    e0, e1, e2 = jnp.exp(l0 - mx), jnp.exp(l1 - mx), jnp.exp(l2 - mx)
    lb = e0 / (e0 + e1 + e2)
    q = pq * jax.nn.sigmoid(pq)
    t1 = jnp.log(lb)
    t2 = jnp.log1p(-lb) + jax.nn.log_sigmoid(pf)
    lf = jnp.maximum(t1, t2) + jnp.log1p(jnp.exp(-jnp.abs(t1 - t2)))
    k = (1.0 - lb) * jax.nn.sigmoid(-pf)
    x = jnp.exp(_prefix(cm, lf))
    scores = mk[0:c] * _bdot(q, k, "nt")
    for lev in range(HG_LEVELS - 1):
        xl = x[lev * c:(lev + 1) * c]
        scores = scores + mk[(lev + 1) * c:(lev + 2) * c] * _bdot(q * xl, k * xl, "nt")
    if with_total:
        xl = x[(HG_LEVELS - 1) * c:HG_LEVELS * c]
        h = c // 2
        low = mk[HG_LEVELS * c + h:(HG_LEVELS + 1) * c] * _bdot(q[h:] * xl[h:], k * xl, "nt")
        scores = scores + jnp.concatenate([jnp.zeros((h, c), F32), low], axis=0)
    else:
        xl = x[(HG_LEVELS - 1) * c:HG_LEVELS * c]
        h = c // 2
        quad = _bdot(q[h:] * xl[h:], k[:h] * xl[:h], "nt")
        zero = jnp.zeros((h, h), F32)
        scores = scores + jnp.concatenate([jnp.concatenate([zero, zero], axis=1),
                                           jnp.concatenate([quad, zero], axis=1)], axis=0)
    x_incl = x[HG_LEVELS * c:(HG_LEVELS + 1) * c]
    x_after = x[(HG_LEVELS + 1) * c:(HG_LEVELS + 2) * c]
    if with_total:
        x_total = jnp.concatenate([x[(HG_LEVELS + 2) * c:(HG_LEVELS + 3) * c]] * (HG_DIM // c), axis=0)
    else:
        x_total = jnp.exp(jnp.sum(lf, axis=0, keepdims=True))
    o = _bdot(scores, pi, "nn") + _bdot(q * x_incl, s_t, "nt")
    s_new = s_t * x_total + _bdot(pi, k * x_after, "tn")
    rstd = lax.rsqrt(jnp.mean(o * o, axis=-1, keepdims=True) + RMS_EPS)
    y = o * rstd * ng * (pg * jax.nn.sigmoid(pg))
    return y, s_new


def _hg_specs(t, reverse, with_total):
    nrb = t // HG_ROWS
    rb = (lambda r: nrb - 1 - r) if reverse else (lambda r: r)
    proj = pl.BlockSpec((4, HG_ROWS, HG_DIM), lambda h, r: (0, rb(r), h))
    vec = pl.BlockSpec((1, HG_DIM), lambda h, r: (0, h))
    cm = pl.BlockSpec(((HG_LEVELS + 2 + with_total) * HG_CHUNK, HG_CHUNK), lambda h, r: (0, 0))
    mk = pl.BlockSpec(((HG_LEVELS + 1) * HG_CHUNK, HG_CHUNK), lambda h, r: (0, 0))
    rows = pl.BlockSpec((HG_ROWS, HG_DIM), lambda h, r: (rb(r), h))
    states = pl.BlockSpec((None, HG_ROWS // HG_CHUNK, HG_DIM, HG_DIM), lambda h, r: (h, rb(r), 0, 0))
    return proj, vec, cm, mk, rows, states


def _hg_fwd(proj, l0, l1, l2, ng):
    t = proj.shape[1]
    n_in = HG_ROWS // HG_CHUNK
    cm, mk = _hg_constants(True)
    p_spec, vec, cm_spec, mk_spec, rows, st_spec = _hg_specs(t, False, True)

    def body(p_ref, l0_ref, l1_ref, l2_ref, ng_ref, cm_ref, mk_ref, y_ref, st_ref, s_ref):
        @pl.when(pl.program_id(1) == 0)
        def _():
            s_ref[...] = jnp.zeros_like(s_ref)

        def step(ci, carry):
            sl = pl.ds(pl.multiple_of(ci * HG_CHUNK, HG_CHUNK), HG_CHUNK)
            s_t = s_ref[...]
            st_ref[ci] = s_t
            y, s_new = _hg_chunk(p_ref[0, sl, :], p_ref[1, sl, :], p_ref[2, sl, :], p_ref[3, sl, :],
                                 l0_ref[...], l1_ref[...], l2_ref[...], ng_ref[...], s_t,
                                 cm_ref[...], mk_ref[...], True)
            y_ref[sl, :] = y.astype(BF16)
            s_ref[...] = s_new
            return carry

        lax.fori_loop(0, n_in, step, 0, unroll=True)

    return pl.pallas_call(
        body, name="hg_fwd", grid=(HG_HEADS, t // HG_ROWS),
        out_shape=(jax.ShapeDtypeStruct((t, D_MODEL), BF16),
                   jax.ShapeDtypeStruct((HG_HEADS, t // HG_CHUNK, HG_DIM, HG_DIM), F32)),
        in_specs=[p_spec, vec, vec, vec, vec, cm_spec, mk_spec], out_specs=(rows, st_spec),
        scratch_shapes=[pltpu.VMEM((HG_DIM, HG_DIM), F32)],
        compiler_params=_cp("parallel", "arbitrary"),
    )(proj, l0, l1, l2, ng, cm, mk)


def _hg_bwd(proj, states, dy, l0, l1, l2, ng):
    t = proj.shape[1]
    n_in = HG_ROWS // HG_CHUNK
    cm, mk = _hg_constants(False)
    p_spec, vec, cm_spec, mk_spec, rows, st_spec = _hg_specs(t, True, False)

    def body(p_ref, st_ref, dy_ref, l0_ref, l1_ref, l2_ref, ng_ref, cm_ref, mk_ref,
             dp_ref, dl0_ref, dl1_ref, dl2_ref, dng_ref, ds_ref):
        @pl.when(pl.program_id(1) == 0)
        def _():
            ds_ref[...] = jnp.zeros_like(ds_ref)
            for r in (dl0_ref, dl1_ref, dl2_ref, dng_ref):
                r[...] = jnp.zeros_like(r)

        def step(it, carry):
            ci = n_in - 1 - it
            sl = pl.ds(pl.multiple_of(ci * HG_CHUNK, HG_CHUNK), HG_CHUNK)
            fn = functools.partial(_hg_chunk, cm=cm_ref[...], mk=mk_ref[...], with_total=False)
            _, vjp = jax.vjp(fn, p_ref[0, sl, :], p_ref[1, sl, :], p_ref[2, sl, :], p_ref[3, sl, :],
                             l0_ref[...], l1_ref[...], l2_ref[...], ng_ref[...], st_ref[ci])
            dq, df, di, dg, d0, d1, d2, dn, ds = vjp((dy_ref[sl, :], ds_ref[...]))
            dp_ref[0, sl, :] = dq.astype(BF16)
            dp_ref[1, sl, :] = df.astype(BF16)
            dp_ref[2, sl, :] = di.astype(BF16)
            dp_ref[3, sl, :] = dg.astype(BF16)
            dl0_ref[...] += d0
            dl1_ref[...] += d1
            dl2_ref[...] += d2
            dng_ref[...] += dn
            ds_ref[...] = ds
            return carry

        lax.fori_loop(0, n_in, step, 0, unroll=True)

    v_shape = jax.ShapeDtypeStruct((1, D_MODEL), F32)
    return pl.pallas_call(
        body, name="hg_bwd", grid=(HG_HEADS, t // HG_ROWS),
        out_shape=(jax.ShapeDtypeStruct((4, t, D_MODEL), BF16), v_shape, v_shape, v_shape, v_shape),
        in_specs=[p_spec, st_spec, rows, vec, vec, vec, vec, cm_spec, mk_spec],
        out_specs=(p_spec, vec, vec, vec, vec),
        scratch_shapes=[pltpu.VMEM((HG_DIM, HG_DIM), F32)],
        compiler_params=_cp("parallel", "arbitrary"),
    )(proj, states, dy, l0, l1, l2, ng, cm, mk)


_SQRT_HALF = 0.7071067811865476
_INV_SQRT_2PI = 0.3989422804014327


def _gelu(x):
    return 0.5 * x * (1.0 + lax.erf(x * _SQRT_HALF))


def _gelu_grad(x):
    return 0.5 * (1.0 + lax.erf(x * _SQRT_HALF)) + x * (_INV_SQRT_2PI * jnp.exp(-0.5 * x * x))


def _tril(n):
    return (lax.broadcasted_iota(jnp.int32, (n, n), 0) >= lax.broadcasted_iota(jnp.int32, (n, n), 1)).astype(F32)


SG_STEP = 4


def _sg_specs(d):
    row = lambda w: pl.BlockSpec((SG_STEP * SG_CHUNK, w), lambda i: (i, 0))
    vec = pl.BlockSpec((1, d), lambda i: (0, 0))
    cube = pl.BlockSpec((SG_GROUPS, SG_CHUNK, SG_CHUNK), lambda i: (0, 0, 0))
    return row, vec, cube


def _sg_fwd(pre, ln_g, ln_b, w_s, bias):
    t = pre.shape[0]
    d = D_MODEL
    row, vec, cube = _sg_specs(d)

    def body(pre_ref, g_ref, b_ref, w_ref, bias_ref, y_ref, vln_ref):
        tril = _tril(SG_CHUNK)
        for st in range(SG_STEP):
            rs = slice(st * SG_CHUNK, (st + 1) * SG_CHUNK)
            u = _gelu(pre_ref[rs, :d])
            vhat, _ = _ln_stats(_gelu(pre_ref[rs, d:]))
            vln_ref[rs, :] = vhat * g_ref[...] + b_ref[...]
            for g in range(SG_GROUPS):
                cs = slice(g * SG_DIM, (g + 1) * SG_DIM)
                gate = _bdot_raw(w_ref[g] * tril, vln_ref[rs, cs], "nn") + bias_ref[g]
                y_ref[rs, cs] = (u[:, cs] * gate).astype(BF16)

    return pl.pallas_call(
        body, name="sg_fwd", grid=(t // (SG_STEP * SG_CHUNK),), out_shape=jax.ShapeDtypeStruct((t, d), BF16),
        in_specs=[row(2 * d), vec, vec, cube, cube], out_specs=row(d),
        scratch_shapes=[pltpu.VMEM((SG_STEP * SG_CHUNK, d), F32)], compiler_params=_cp("parallel"),
    )(pre, ln_g, ln_b, w_s, bias)


def _sg_bwd(pre, dy, ln_g, ln_b, w_s, bias):
    t = pre.shape[0]
    d = D_MODEL
    row, vec, cube = _sg_specs(d)
    dbs_spec = pl.BlockSpec((SG_GROUPS, 8, SG_CHUNK), lambda i: (0, 0, 0))

    def body(pre_ref, dy_ref, g_ref, b_ref, w_ref, bias_ref, dpre_ref, dw_ref, dbs_ref, dlg_ref, dlb_ref,
             vln_ref, dvln_ref):
        @pl.when(pl.program_id(0) == 0)
        def _():
            for r in (dw_ref, dbs_ref, dlg_ref, dlb_ref):
                r[...] = jnp.zeros_like(r)

        tril = _tril(SG_CHUNK)
        ones = jnp.ones((8, SG_DIM), F32)
        for st in range(SG_STEP):
            rs = slice(st * SG_CHUNK, (st + 1) * SG_CHUNK)
            pu = pre_ref[rs, :d]
            pv = pre_ref[rs, d:]
            u = _gelu(pu)
            vhat, rstd = _ln_stats(_gelu(pv))
            vln_ref[rs, :] = vhat * g_ref[...] + b_ref[...]
            for g in range(SG_GROUPS):
                cs = slice(g * SG_DIM, (g + 1) * SG_DIM)
                wc = w_ref[g] * tril
                vg = vln_ref[rs, cs]
                gate = _bdot_raw(wc, vg, "nn") + bias_ref[g]
                dyg = dy_ref[rs, cs]
                dgate = dyg * u[:, cs]
                dpre_ref[rs, cs] = (dyg * gate * _gelu_grad(pu[:, cs])).astype(BF16)
                dw_ref[g] += tril * _bdot_raw(dgate, vg, "nt")
                dbs_ref[g] += lax.dot_general(ones, dgate, _DIMS["nt"], precision=HI, preferred_element_type=F32)
                dvln_ref[rs, cs] = _bdot_raw(wc, dgate, "tn")
            dvln = dvln_ref[rs, :]
            dv, dg, db = _ln_bwd_math(vhat, rstd, dvln, g_ref[...])
            dlg_ref[...] += dg
            dlb_ref[...] += db
            dpre_ref[rs, d:] = (dv * _gelu_grad(pv)).astype(BF16)

    return pl.pallas_call(
        body, name="sg_bwd", grid=(t // (SG_STEP * SG_CHUNK),),
        out_shape=(jax.ShapeDtypeStruct((t, 2 * d), BF16), jax.ShapeDtypeStruct(w_s.shape, F32),
                   jax.ShapeDtypeStruct((SG_GROUPS, 8, SG_CHUNK), F32),
                   jax.ShapeDtypeStruct((1, d), F32), jax.ShapeDtypeStruct((1, d), F32)),
        in_specs=[row(2 * d), row(d), vec, vec, cube, cube],
        out_specs=(row(2 * d), cube, dbs_spec, vec, vec),
        scratch_shapes=[pltpu.VMEM((SG_STEP * SG_CHUNK, d), F32), pltpu.VMEM((SG_STEP * SG_CHUNK, d), F32)],
        compiler_params=_cp("arbitrary"),
    )(pre, dy, ln_g, ln_b, w_s, bias)


FFN_ROWS = 1024
FFN_COLS = 1408
HALO = 16


def _ffn_conv(a_ext, w_ref, cb_ref):
    a1 = pltpu.roll(a_ext, 1, 0)
    a2 = pltpu.roll(a_ext, 2, 0)
    return w_ref[0:1, :] * a2 + w_ref[1:2, :] * a1 + w_ref[2:3, :] * a_ext + cb_ref[...], a1, a2


def _ffn_fwd(hh, conv_w, conv_b):
    t = hh.shape[1]
    nb8 = FFN_ROWS // HALO
    main = pl.BlockSpec((2, FFN_ROWS, FFN_COLS), lambda c, r: (0, r, c))
    prev = pl.BlockSpec((None, HALO, FFN_COLS), lambda c, r: (0, jnp.maximum(r * nb8 - 1, 0), c))
    wspec = pl.BlockSpec((3, FFN_COLS), lambda c, r: (0, c))
    bspec = pl.BlockSpec((1, FFN_COLS), lambda c, r: (0, c))

    def body(m_ref, p_ref, w_ref, cb_ref, z_ref):
        prev = jnp.where(pl.program_id(1) == 0, 0.0, p_ref[...].astype(F32))
        a_ext = jnp.concatenate([prev, m_ref[0].astype(F32)], axis=0)
        a, _, _ = _ffn_conv(a_ext, w_ref, cb_ref)
        a = a[HALO:]
        z_ref[...] = (a * jax.nn.sigmoid(a) * m_ref[1].astype(F32)).astype(BF16)

    return pl.pallas_call(
        body, name="ffn_fwd", grid=(D_FF // FFN_COLS, t // FFN_ROWS),
        out_shape=jax.ShapeDtypeStruct((t, D_FF), BF16),
        in_specs=[main, prev, wspec, bspec],
        out_specs=pl.BlockSpec((FFN_ROWS, FFN_COLS), lambda c, r: (r, c)),
        compiler_params=_cp("parallel", "parallel"),
    )(hh, hh, conv_w, conv_b)


def _ffn_bwd(hh, dz, conv_w, conv_b):
    t = hh.shape[1]
    nb8 = FFN_ROWS // HALO
    last8 = t // HALO - 1
    nr = t // FFN_ROWS
    main = pl.BlockSpec((2, FFN_ROWS, FFN_COLS), lambda c, r: (0, r, c))
    prev = pl.BlockSpec((None, HALO, FFN_COLS), lambda c, r: (0, jnp.maximum(r * nb8 - 1, 0), c))
    nxt = pl.BlockSpec((2, HALO, FFN_COLS), lambda c, r: (0, jnp.minimum((r + 1) * nb8, last8), c))
    dmain = pl.BlockSpec((FFN_ROWS, FFN_COLS), lambda c, r: (r, c))
    dnxt = pl.BlockSpec((HALO, FFN_COLS), lambda c, r: (jnp.minimum((r + 1) * nb8, last8), c))
    wspec = pl.BlockSpec((3, FFN_COLS), lambda c, r: (0, c))
    bspec = pl.BlockSpec((1, FFN_COLS), lambda c, r: (0, c))

    def body(m_ref, p_ref, n_ref, dz_ref, dzn_ref, w_ref, cb_ref, dh_ref, dw_ref, dcb_ref):
        r = pl.program_id(1)

        @pl.when(r == 0)
        def _():
            dw_ref[...] = jnp.zeros_like(dw_ref)
            dcb_ref[...] = jnp.zeros_like(dcb_ref)

        prev = jnp.where(r == 0, 0.0, p_ref[...].astype(F32))
        a_ext = jnp.concatenate([prev, m_ref[0].astype(F32), n_ref[0].astype(F32)], axis=0)
        a, a1, a2 = _ffn_conv(a_ext, w_ref, cb_ref)
        a = a[HALO:]
        b_ext = jnp.concatenate([m_ref[1], n_ref[1]], axis=0).astype(F32)
        dz_main = dz_ref[...].astype(F32)
        dz_ext = jnp.concatenate([dz_main, jnp.where(r == nr - 1, 0.0, dzn_ref[...].astype(F32))], axis=0)
        sig = jax.nn.sigmoid(a)
        da = dz_ext * b_ext * (sig * (1.0 + a * (1.0 - sig)))
        n_ext = FFN_ROWS + HALO
        da_p1 = pltpu.roll(da, n_ext - 1, 0)
        da_p2 = pltpu.roll(da, n_ext - 2, 0)
        da_raw = w_ref[2:3, :] * da + w_ref[1:2, :] * da_p1 + w_ref[0:1, :] * da_p2
        dh_ref[0] = da_raw[:FFN_ROWS].astype(BF16)
        dh_ref[1] = (dz_main * (a * sig)[:FFN_ROWS]).astype(BF16)
        dam = da[:FFN_ROWS]
        rows = slice(HALO, HALO + FFN_ROWS)
        dw_ref[0:1, :] += jnp.sum(dam * a2[rows], axis=0, keepdims=True)
        dw_ref[1:2, :] += jnp.sum(dam * a1[rows], axis=0, keepdims=True)
        dw_ref[2:3, :] += jnp.sum(dam * a_ext[rows], axis=0, keepdims=True)
        dcb_ref[...] += jnp.sum(dam, axis=0, keepdims=True)

    return pl.pallas_call(
        body, name="ffn_bwd", grid=(D_FF // FFN_COLS, nr),
        out_shape=(jax.ShapeDtypeStruct((2, t, D_FF), BF16), jax.ShapeDtypeStruct((3, D_FF), F32),
                   jax.ShapeDtypeStruct((1, D_FF), F32)),
        in_specs=[main, prev, nxt, dmain, dnxt, wspec, bspec],
        out_specs=(main, wspec, bspec), compiler_params=_cp("parallel", "arbitrary"),
    )(hh, hh, hh, dz, dz, conv_w, conv_b)


def _adamw_math(w, g, m, v):
    m = ADAM_B1 * m + (1.0 - ADAM_B1) * g
    v = ADAM_B2 * v + (1.0 - ADAM_B2) * (g * g)
    m_hat = m / (1.0 - ADAM_B1 ** ADAM_STEP)
    v_hat = v / (1.0 - ADAM_B2 ** ADAM_STEP)
    delta = -ADAM_LR * (m_hat / (jnp.sqrt(v_hat) + ADAM_EPS) + ADAM_WD * w)
    return delta, m, v


def _as2d(shape):
    n = int(np.prod(shape))
    c = shape[-1] if shape[-1] % 128 == 0 else (128 if n % 128 == 0 else shape[-1])
    return n // c, c


def _adamw(w, g, m, v, name, parts=None, own=None, is_me=None, layer=None, prev=None):
    shape = w.shape if layer is None else w.shape[1:]
    r, c = _as2d(shape)
    br = r if r <= 512 else _pick(r, (256, 176, 128, 64, 8))
    blk = pl.BlockSpec((br, c), lambda i: (i, 0))
    if layer is None:
        wblk, o = blk, jax.ShapeDtypeStruct((r, c), F32)
        w2, m2, v2 = (a.reshape(r, c) for a in (w, m, v))
    else:
        wblk, o = pl.BlockSpec((None, br, c), lambda i: (layer, i, 0)), jax.ShapeDtypeStruct((w.shape[0], r, c), F32)
        w2, m2, v2 = (a.reshape(w.shape[0], r, c) for a in (w, m, v))
    extra, extra_specs = [], []
    if parts is not None:
        g_in = parts.reshape(N_DEV, r, c)
        g_spec = pl.BlockSpec((N_DEV, br, c), lambda i: (0, i, 0))
        extra = [own.reshape(r, c), is_me]
        extra_specs = [blk, pl.BlockSpec(memory_space=pltpu.SMEM)]
    else:
        g_in = g.reshape(r, c)
        g_spec = blk
    aliases = {}
    if prev is not None:
        aliases = {4 + len(extra) + j: j for j in range(4)}
        extra = extra + [a.reshape(o.shape) for a in prev]
        extra_specs = extra_specs + [ANY] * 4

    def body(w_ref, g_ref, m_ref, v_ref, *rest):
        go_ref, d_ref, mo_ref, vo_ref = rest[-4:]
        if parts is not None:
            own_ref, me_ref = rest[:2]
            gsum = None
            for i in range(N_DEV):
                term = jnp.where(me_ref[i] > 0.5, own_ref[...], g_ref[i]).astype(F32)
                gsum = term if gsum is None else gsum + term
        else:
            gsum = g_ref[...]
        delta, m_new, v_new = _adamw_math(w_ref[...], gsum, m_ref[...], v_ref[...])
        go_ref[...] = gsum
        d_ref[...] = delta
        mo_ref[...] = m_new
        vo_ref[...] = v_new

    outs = pl.pallas_call(
        body, name=name, grid=(r // br,), out_shape=(o, o, o, o),
        in_specs=[wblk, g_spec, wblk, wblk] + extra_specs, out_specs=(wblk, wblk, wblk, wblk),
        input_output_aliases=aliases, compiler_params=_cp("parallel"),
    )(w2, g_in, m2, v2, *extra)
    return tuple(a.reshape(w.shape) for a in outs)


ALL_PEERS = tuple(range(1, N_DEV))
OTHER_CHIPS = (2, 4, 6)
SIBLING = (1,)


def _me_and_peers(relations=ALL_PEERS):
    x, y, c = lax.axis_index("x"), lax.axis_index("y"), lax.axis_index("c")
    me = 4 * x + 2 * y + c
    peers = []
    for k in relations:
        kx, ky, kc = (k >> 2) & 1, (k >> 1) & 1, k & 1
        px, py, pc = x ^ kx, y ^ ky, c ^ kc
        peers.append(((px, py, pc), 4 * px + 2 * py + pc))
    return me, peers


def _all_gather(shard, name):
    def body(x_ref, o_ref, send_sems, recv_sems, local_sem):
        me, peers = _me_and_peers()
        mine = pltpu.make_async_copy(x_ref, o_ref.at[me], local_sem)
        mine.start()
        sends = []
        for k, (dev, _) in enumerate(peers):
            cp = pltpu.make_async_remote_copy(src_ref=x_ref, dst_ref=o_ref.at[me], send_sem=send_sems.at[k],
                                              recv_sem=recv_sems.at[k], device_id=dev, device_id_type=MESH)
            cp.start()
            sends.append(cp)
        for k, (dev, idx) in enumerate(peers):
            pltpu.make_async_remote_copy(src_ref=x_ref, dst_ref=o_ref.at[idx], send_sem=send_sems.at[k],
                                         recv_sem=recv_sems.at[k], device_id=dev, device_id_type=MESH).wait_recv()
        for cp in sends:
            cp.wait_send()
        mine.wait()

    return pl.pallas_call(
        body, name=name, out_shape=jax.ShapeDtypeStruct((N_DEV,) + shard.shape, shard.dtype),
        in_specs=[ANY], out_specs=ANY,
        scratch_shapes=[pltpu.SemaphoreType.DMA((N_DEV - 1,)), pltpu.SemaphoreType.DMA((N_DEV - 1,)),
                        pltpu.SemaphoreType.DMA],
        compiler_params=pltpu.CompilerParams(has_side_effects=True),
    )(shard)


HBM = pl.BlockSpec(memory_space=pltpu.HBM)
SEM = pl.BlockSpec(memory_space=pltpu.SEMAPHORE)
EFFECT = pltpu.SideEffectType.DATAFLOW_SIDE_EFFECTING


def _xchg_start(parts, land, *, dep, name, peers=ALL_PEERS, pieces=(0,)):
    n_buf = 1 if parts is None else 2

    def body(*refs):
        land_ref = refs[n_buf - 1]
        send_sem, recv_sem = refs[n_buf + 1], refs[n_buf + 2]
        token = refs[2 * n_buf + 3]
        me, to = _me_and_peers(peers)
        for dev, idx in to:
            if parts is None:
                for r in pieces:
                    piece = land_ref.at[me ^ r]
                    pltpu.make_async_remote_copy(src_ref=piece, dst_ref=piece, send_sem=send_sem, recv_sem=recv_sem,
                                                 device_id=dev, device_id_type=MESH).start()
            else:
                pltpu.make_async_remote_copy(src_ref=refs[0].at[idx], dst_ref=land_ref.at[me], send_sem=send_sem,
                                             recv_sem=recv_sem, device_id=dev, device_id_type=MESH).start()
        token[...] = jnp.zeros_like(token)

    bufs = [land] if parts is None else [parts, land]
    return pl.pallas_call(
        body, name=name,
        out_shape=(pltpu.SemaphoreType.DMA(()), pltpu.SemaphoreType.DMA(()),
                   *[pltpu.HBM(b.shape, b.dtype) for b in bufs], jax.ShapeDtypeStruct((8, 128), F32)),
        in_specs=(*[HBM] * n_buf, ANY), out_specs=(SEM, SEM, *[HBM] * n_buf, pl.BlockSpec(memory_space=pltpu.VMEM)),
        input_output_aliases={i: 2 + i for i in range(n_buf)},
        compiler_params=pltpu.CompilerParams(has_side_effects=EFFECT),
    )(*[pltpu.with_memory_space_constraint(b, pltpu.HBM) for b in bufs], dep)


def _xchg_wait(handles, after, name, n_pieces=N_DEV - 1):
    send_sem, recv_sem, *bufs, _ = handles
    n_buf = len(bufs)

    def body(*refs):
        land_ref = refs[n_buf - 1]
        send_sem, recv_sem = refs[n_buf], refs[n_buf + 1]
        x, y, c = lax.axis_index("x"), lax.axis_index("y"), lax.axis_index("c")
        span = land_ref.at[pl.ds(0, n_pieces)]
        cp = pltpu.make_async_remote_copy(src_ref=span, dst_ref=span, send_sem=send_sem, recv_sem=recv_sem,
                                          device_id=(x, y, c), device_id_type=MESH)
        cp.wait_send()
        cp.wait_recv()

    return pl.pallas_call(
        body, name=name, out_shape=tuple(pltpu.HBM(b.shape, b.dtype) for b in bufs),
        in_specs=(*[HBM] * n_buf, SEM, SEM, ANY), out_specs=tuple([HBM] * n_buf),
        input_output_aliases={i: i for i in range(n_buf)},
        compiler_params=pltpu.CompilerParams(has_side_effects=EFFECT),
    )(*bufs, send_sem, recv_sem, after)


def _sum_devices(v, name):
    _, r, c = v.shape

    def body(x_ref, o_ref):
        acc = x_ref[0]
        for i in range(1, N_DEV):
            acc = acc + x_ref[i]
        o_ref[...] = acc

    return pl.pallas_call(
        body, name=name, out_shape=jax.ShapeDtypeStruct((r, c), F32),
        in_specs=[pl.BlockSpec(memory_space=pltpu.VMEM)], out_specs=pl.BlockSpec(memory_space=pltpu.VMEM),
        compiler_params=pltpu.CompilerParams(vmem_limit_bytes=VMEM_LIMIT),
    )(v)


def _local_step(x, target, lb_logits, p, weight, send_grad, send_small):
    g = {}
    l0, l1, l2 = lb_logits[0:1], lb_logits[1:2], lb_logits[2:3]
    x_bf = x.astype(BF16)

    proj0 = _mm(x_bf, weight("hg_w_in", x_bf), mode="nn", b_cs=True, o_cs=4, name="mm_hg_in")
    y0, states = _hg_fwd(proj0, l0, l1, l2, p["hg_norm_g"])
    mixed0 = _mm(y0, weight("hg_w_out", y0), mode="nn", name="mm_hg_out")
    hh_l, z_l = [], []

    def ffn_forward(layer, h_in_bf):
        hh = _mm(h_in_bf, weight(f"ffn_w_up{layer}", h_in_bf), mode="nn", b_cs=True, o_cs=2, out_dtype=BF16,
                 name=f"mm_up{layer}")
        z = _ffn_fwd(hh, p["ffn_conv_w"][layer], p["ffn_conv_b"][layer])
        ffn = _mm(z, weight(f"ffn_w_down{layer}", z), mode="nn", name=f"mm_down{layer}")
        hh_l.append(hh)
        z_l.append(z)
        return ffn

    h1, h1b = _ln_fwd(x, mixed0, p["ln1_g"][0], p["ln1_b"][0], "ln1_0")
    ffn0 = ffn_forward(0, h1b)
    h2, h2b = _ln_fwd(h1, ffn0, p["ln2_g"][0], p["ln2_b"][0], "ln2_0")
    pre1 = _mm(h2b, weight("sg_w_in", h2b), mode="nn", b_cs=True, name="mm_sg_in")
    y1 = _sg_fwd(pre1, p["sg_ln_g"], p["sg_ln_b"], p["sg_w_s"], p["sg_bias"])
    mixed1 = _mm(y1, weight("sg_w_out", y1), mode="nn", name="mm_sg_out")
    h3, h3b = _ln_fwd(h2, mixed1, p["ln1_g"][1], p["ln1_b"][1], "ln1_1")
    ffn1 = ffn_forward(1, h3b)

    loss, ds, dres, g["ln2_g1"], g["ln2_b1"] = _ln_loss_bwd(h3, ffn1, target, p["ln2_g"][1], p["ln2_b"][1], "ln2_1_loss")

    def ffn_backward(layer, ds_bf, h_in_bf, dres):
        tok = send_grad(f"ffn_w_down{layer}",
                        _mm(z_l[layer], ds_bf, mode="tn", out_dtype=BF16, name=f"mm_dw_down{layer}"))
        dz = _mm(ds_bf, weight(f"ffn_w_down{layer}", None), mode="nt", out_dtype=BF16, name=f"mm_d_z{layer}",
                 deps=(tok,))
        dhh, g[f"conv_w{layer}"], g[f"conv_b{layer}"] = _ffn_bwd(hh_l[layer], dz, p["ffn_conv_w"][layer],
                                                                   p["ffn_conv_b"][layer])
        tok = send_grad(f"ffn_w_up{layer}", _mm(h_in_bf, dhh, mode="tn", b_cs=True, o_cs=N_DEV, out_dtype=BF16,
                                                name=f"mm_dw_up{layer}"))
        return _mm(dhh, weight(f"ffn_w_up{layer}", None), mode="nt", a_cs=True, b_cs=True, addend=dres,
                   name=f"mm_d_up{layer}", deps=(tok,))

    dh3 = ffn_backward(1, ds, h3b, dres)
    ds, dres, g["ln1_g1"], g["ln1_b1"] = _ln_bwd(h2, mixed1, dh3, p["ln1_g"][1], "ln1_1_bwd")
    tok = send_grad("sg_w_out", _mm(y1, ds, mode="tn", out_dtype=BF16, name="mm_dw_sg_out"))
    dy1 = _mm(ds, weight("sg_w_out", None), mode="nt", name="mm_d_sg_out", deps=(tok,))
    dpre1, g["sg_w_s"], dbs, g["sg_ln_g"], g["sg_ln_b"] = _sg_bwd(pre1, dy1, p["sg_ln_g"], p["sg_ln_b"],
                                                                    p["sg_w_s"], p["sg_bias"])
    g["sg_b_s"] = dbs[:, 0, :]
    tok = send_grad("sg_w_in", _mm(h2b, dpre1, mode="tn", o_cs=N_DEV, out_dtype=BF16, name="mm_dw_sg_in"))
    dh2 = _mm(dpre1, weight("sg_w_in", None), mode="nt", b_cs=True, addend=dres, name="mm_d_sg_in", deps=(tok,))
    ds, dres, g["ln2_g0"], g["ln2_b0"] = _ln_bwd(h1, ffn0, dh2, p["ln2_g"][0], "ln2_0_bwd")
    dh1 = ffn_backward(0, ds, h1b, dres)
    ds, dres, g["ln1_g0"], g["ln1_b0"] = _ln_bwd(x, mixed0, dh1, p["ln1_g"][0], "ln1_0_bwd")
    tok = send_grad("hg_w_out", _mm(y0, ds, mode="tn", out_dtype=BF16, name="mm_dw_hg_out"))
    dy0 = _mm(ds, weight("hg_w_out", None), mode="nt", name="mm_d_hg_out", deps=(tok,))
    dproj, d0, d1, d2, g["hg_norm_g"] = _hg_bwd(proj0, states, dy0, l0, l1, l2, p["hg_norm_g"])
    g["lb_logits"] = jnp.concatenate([d0, d1, d2], axis=0)
    tok = send_small(g)
    tok = send_grad("hg_w_in", _mm(x_bf, dproj, mode="tn", b_cs=True, o_cs=N_DEV, out_dtype=BF16, name="mm_dw_hg_in",
                                   deps=(tok,)))
    grad_x = _mm(dproj, weight("hg_w_in", None), mode="nt", a_cs=True, b_cs=True, addend=dres, name="mm_d_hg_in",
                 deps=(tok,))
    return loss[0, 0], grad_x


_SMALL = ("lb_logits", "hg_norm_g", "sg_ln_g", "sg_ln_b", "sg_w_s", "sg_b_s", "ffn_conv_w", "ffn_conv_b",
          "ln1_g", "ln1_b", "ln2_g", "ln2_b")
_NAMES = ("lb_logits", "hg_w_in", "hg_norm_g", "hg_w_out", "sg_w_in", "sg_ln_g", "sg_ln_b", "sg_w_s", "sg_b_s",
          "sg_w_out", "ffn_w_up", "ffn_conv_w", "ffn_conv_b", "ffn_w_down", "ln1_g", "ln1_b", "ln2_g", "ln2_b")


def kernel(x, lb_logits, hg_w_in, hg_norm_g, hg_w_out, sg_w_in, sg_ln_g, sg_ln_b, sg_w_s, sg_b_s, sg_w_out, ffn_w_up, ffn_conv_w, ffn_conv_b, ffn_w_down, ln1_g, ln1_b, ln2_g, ln2_b, loss_target, m_lb_logits, m_hg_w_in, m_hg_norm_g, m_hg_w_out, m_sg_w_in, m_sg_ln_g, m_sg_ln_b, m_sg_w_s, m_sg_b_s, m_sg_w_out, m_ffn_w_up, m_ffn_conv_w, m_ffn_conv_b, m_ffn_w_down, m_ln1_g, m_ln1_b, m_ln2_g, m_ln2_b, v_lb_logits, v_hg_w_in, v_hg_norm_g, v_hg_w_out, v_sg_w_in, v_sg_ln_g, v_sg_ln_b, v_sg_w_s, v_sg_b_s, v_sg_w_out, v_ffn_w_up, v_ffn_conv_w, v_ffn_conv_b, v_ffn_w_down, v_ln1_g, v_ln1_b, v_ln2_g, v_ln2_b):
    w = dict(lb_logits=lb_logits, hg_w_in=hg_w_in, hg_norm_g=hg_norm_g, hg_w_out=hg_w_out, sg_w_in=sg_w_in,
             sg_ln_g=sg_ln_g, sg_ln_b=sg_ln_b, sg_w_s=sg_w_s, sg_b_s=sg_b_s, sg_w_out=sg_w_out, ffn_w_up=ffn_w_up,
             ffn_conv_w=ffn_conv_w, ffn_conv_b=ffn_conv_b, ffn_w_down=ffn_w_down, ln1_g=ln1_g, ln1_b=ln1_b,
             ln2_g=ln2_g, ln2_b=ln2_b)
    m = dict(lb_logits=m_lb_logits, hg_w_in=m_hg_w_in, hg_norm_g=m_hg_norm_g, hg_w_out=m_hg_w_out,
             sg_w_in=m_sg_w_in, sg_ln_g=m_sg_ln_g, sg_ln_b=m_sg_ln_b, sg_w_s=m_sg_w_s, sg_b_s=m_sg_b_s,
             sg_w_out=m_sg_w_out, ffn_w_up=m_ffn_w_up, ffn_conv_w=m_ffn_conv_w, ffn_conv_b=m_ffn_conv_b,
             ffn_w_down=m_ffn_w_down, ln1_g=m_ln1_g, ln1_b=m_ln1_b, ln2_g=m_ln2_g, ln2_b=m_ln2_b)
    v = dict(lb_logits=v_lb_logits, hg_w_in=v_hg_w_in, hg_norm_g=v_hg_norm_g, hg_w_out=v_hg_w_out,
             sg_w_in=v_sg_w_in, sg_ln_g=v_sg_ln_g, sg_ln_b=v_sg_ln_b, sg_w_s=v_sg_w_s, sg_b_s=v_sg_b_s,
             sg_w_out=v_sg_w_out, ffn_w_up=v_ffn_w_up, ffn_conv_w=v_ffn_conv_w, ffn_conv_b=v_ffn_conv_b,
             ffn_w_down=v_ffn_w_down, ln1_g=v_ln1_g, ln1_b=v_ln1_b, ln2_g=v_ln2_g, ln2_b=v_ln2_b)
    me = 4 * lax.axis_index("x") + 2 * lax.axis_index("y") + lax.axis_index("c")
    is_me = (jnp.arange(N_DEV) == me).astype(F32)
    d = D_MODEL

    shards = [("hg_w_in", hg_w_in[0]), ("hg_w_out", hg_w_out[0]), ("ffn_w_up0", ffn_w_up[0]),
              ("ffn_w_down0", ffn_w_down[0]), ("sg_w_in", sg_w_in[0]), ("sg_w_out", sg_w_out[0]),
              ("ffn_w_up1", ffn_w_up[1]), ("ffn_w_down1", ffn_w_down[1])]
    row_sharded = {"hg_w_out": (d, d), "sg_w_out": (d, d), "ffn_w_down0": (D_FF, d), "ffn_w_down1": (D_FF, d)}
    sv = jnp.zeros((8, 768), F32)
    sv = sv.at[0, :256].set(sg_ln_g[0]).at[1, :256].set(sg_ln_b[0]).at[2:8, :704].set(ffn_conv_w.reshape(6, 704))
    sv = _all_gather(sv, "ag_small")
    gathers, gathered, dep = {}, {}, sv
    for n, shard in shards:
        land = lax.dynamic_update_index_in_dim(lax.empty((N_DEV,) + shard.shape, BF16), shard.astype(BF16), me, 0)
        gathers[n] = _xchg_start(None, land, dep=dep, name=f"ag_{n}", peers=OTHER_CHIPS)
        dep = gathers[n][-1]
    last_start = dep
    handed = {}
    order = [n for n, _ in shards]

    def hand_over(n, after):
        (land,) = _xchg_wait(gathers[n], after, f"agw_{n}", n_pieces=len(OTHER_CHIPS))
        handed[n] = _xchg_start(None, land, dep=is_me, name=f"ah_{n}", peers=SIBLING, pieces=(0,) + OTHER_CHIPS)

    def weight(n, after):
        if n not in gathered:
            if n == order[0]:
                hand_over(n, last_start)
                after = handed[n][-1]
            nxt = order.index(n) + 1
            if nxt < len(order):
                hand_over(order[nxt], after)
                after = handed[order[nxt]][-1]
            (full,) = _xchg_wait(handed[n], after, f"ahw_{n}", n_pieces=1 + len(OTHER_CHIPS))
            gathered[n] = full.reshape(row_sharded[n]) if n in row_sharded else full
        return gathered[n]

    grad_sends = {}

    def send_grad(n, parts):
        shape = (N_DEV,) + dict(shards)[n].shape
        grad_sends[n] = _xchg_start(parts.reshape(shape), lax.empty(shape, parts.dtype), dep=is_me, name=f"rs_{n}")
        return grad_sends[n][-1]

    small_send = {}

    def send_small(g):
        small = [g["lb_logits"], g["hg_norm_g"], g["sg_ln_g"], g["sg_ln_b"], g["sg_w_s"], g["sg_b_s"],
                 g["conv_w0"], g["conv_w1"], g["conv_b0"], g["conv_b1"], g["ln1_g0"], g["ln1_g1"],
                 g["ln1_b0"], g["ln1_b1"], g["ln2_g0"], g["ln2_g1"], g["ln2_b0"], g["ln2_b1"]]
        flat = jnp.concatenate([a.reshape(-1) for a in small])
        rows = -(-flat.shape[0] // (8 * 128)) * 8
        flat = jnp.pad(flat, (0, rows * 128 - flat.shape[0])).reshape(rows, 128)
        land = lax.dynamic_update_index_in_dim(lax.empty((N_DEV, rows, 128), F32), flat, me, 0)
        small_send["shapes"] = [a.shape for a in small]
        small_send["handles"] = _xchg_start(None, land, dep=is_me, name="ar_small")
        return small_send["handles"][-1]

    p = {}
    p["sg_ln_g"] = sv[:, 0, :256].reshape(1, d)
    p["sg_ln_b"] = sv[:, 1, :256].reshape(1, d)
    conv_w_full = jnp.transpose(sv[:, 2:8, :704].reshape(N_DEV, DEPTH, 3, 704), (1, 2, 0, 3)).reshape(DEPTH, 3, D_FF)
    p["ffn_conv_w"] = [conv_w_full[l] for l in range(DEPTH)]
    p["ffn_conv_b"] = [ffn_conv_b[l:l + 1] for l in range(DEPTH)]
    p["hg_norm_g"] = hg_norm_g
    p["sg_w_s"] = sg_w_s[0]
    p["sg_bias"] = jnp.broadcast_to(sg_b_s[0][:, :, None], (SG_GROUPS, SG_CHUNK, SG_DIM))
    for n in ("ln1_g", "ln1_b", "ln2_g", "ln2_b"):
        p[n] = [w[n][l:l + 1] for l in range(DEPTH)]

    loss_part, grad_x = _local_step(x[0], loss_target[0], lb_logits, p, weight, send_grad, send_small)
    loss = lax.psum(loss_part, ("x", "y", "c"))

    (landed,) = _xchg_wait(small_send["handles"], grad_x, "arw_small")
    red = _sum_devices(landed, "sum_small").reshape(-1)
    shapes = small_send["shapes"]
    offs = np.cumsum([0] + [int(np.prod(sh)) for sh in shapes])
    r = [red[offs[i]:offs[i + 1]].reshape(shapes[i]) for i in range(len(shapes))]
    gs = {}
    gs["lb_logits"] = r[0]
    gs["hg_norm_g"] = r[1]
    gs["sg_ln_g"] = lax.dynamic_slice(r[2], (0, me * 256), (1, 256))
    gs["sg_ln_b"] = lax.dynamic_slice(r[3], (0, me * 256), (1, 256))
    gs["sg_w_s"] = r[4][None]
    gs["sg_b_s"] = r[5][None]
    gs["ffn_conv_w"] = lax.dynamic_slice(jnp.stack([r[6], r[7]]), (0, 0, me * 704), (DEPTH, 3, 704))
    gs["ffn_conv_b"] = jnp.concatenate([r[8], r[9]], axis=0)
    gs["ln1_g"] = jnp.concatenate([r[10], r[11]], axis=0)
    gs["ln1_b"] = jnp.concatenate([r[12], r[13]], axis=0)
    gs["ln2_g"] = jnp.concatenate([r[14], r[15]], axis=0)
    gs["ln2_b"] = jnp.concatenate([r[16], r[17]], axis=0)

    out_g, out_d, out_m, out_v = {}, {}, {}, {}
    for n in _SMALL:
        out_g[n], out_d[n], out_m[n], out_v[n] = _adamw(w[n], gs[n], m[n], v[n], f"adamw_{n}")

    res, after = {}, out_v["ln2_b"]
    for n in ("ffn_w_down1", "ffn_w_up1", "sg_w_out", "sg_w_in", "ffn_w_down0", "ffn_w_up0", "hg_w_out", "hg_w_in"):
        parts, recv = _xchg_wait(grad_sends[n], after, f"rsw_{n}")
        own = lax.dynamic_index_in_dim(parts, me, 0, keepdims=False)
        if n[-1] in "01":
            base, layer = n[:-1], int(n[-1])
            res[n] = _adamw(w[base], None, m[base], v[base], f"adamw_{n}", parts=recv, own=own, is_me=is_me,
                            layer=layer, prev=res.get(base + "1"))
        else:
            res[n] = _adamw(w[n][0], None, m[n][0], v[n][0], f"adamw_{n}", parts=recv, own=own, is_me=is_me)
        after = res[n][3]
    for n in ("hg_w_in", "hg_w_out", "sg_w_in", "sg_w_out"):
        out_g[n], out_d[n], out_m[n], out_v[n] = (a[None] for a in res[n])
    for n in ("ffn_w_up", "ffn_w_down"):
        out_g[n], out_d[n], out_m[n], out_v[n] = res[n + "0"]

    return (loss, grad_x[None], *[out_g[n] for n in _NAMES], *[out_d[n] for n in _NAMES],
            *[out_m[n] for n in _NAMES], *[out_v[n] for n in _NAMES])
```
